```python
import math
import jax
import jax.numpy as jnp
from jax import lax
import numpy as np

D_MODEL = 2048
BATCH = 4
SEQ = 2048
DEPTH = 2

CTX_LEN = 256
GRID_W = 64

N_EVEN = (DEPTH + 1) // 2
N_ODD = DEPTH // 2
LAST_ATTN_LAYER = ((DEPTH - 1) // 2) * 2
DEEPNORM_ALPHA = (2.0 * DEPTH) ** 0.25
DEEPNORM_BETA = (8.0 * DEPTH) ** -0.25
LN_EPS = 1e-6
NEG_INF = -1e30
Q_BLOCK = 128

MLA_HEADS = 8
MLA_NOPE = 128
MLA_ROPE = 64
MLA_QK = MLA_NOPE + MLA_ROPE
MLA_V = 128
MLA_KV_RANK = 512
ROPE_THETA = 10000.0

NA_HEADS = 8
NA_HEAD_DIM = 128
NA_KH = 8
NA_KW = 16
NA_QC = 16
NA_KC = NA_QC + NA_KW

OFF_CKV = MLA_HEADS * MLA_QK
OFF_KPE = OFF_CKV + MLA_KV_RANK
OFF_QNA = OFF_KPE + MLA_ROPE
OFF_KNA = OFF_QNA + NA_HEADS * NA_HEAD_DIM
OFF_VNA = OFF_KNA + NA_HEADS * NA_HEAD_DIM
IN_AB_W = OFF_VNA + NA_HEADS * NA_HEAD_DIM
MIX_AB_W = MLA_HEADS * MLA_V + NA_HEADS * NA_HEAD_DIM

HY_W = 1024
HY_ORDER = 2
HY_IN_W = (HY_ORDER + 1) * HY_W
HY_SHORT = 3
HY_BANDS = 16
HY_EMB = 1 + 2 * HY_BANDS
HY_FILT_HID = 64
HY_FILT_OUT = HY_ORDER * 2 * HY_W
HY_DECAY_TARGET = 1e-2
HY_FAST_DECAY_PCT = 0.3
HY_SLOW_DECAY_PCT = 1.5
HY_MIN_DECAY = math.log(HY_DECAY_TARGET) / HY_SLOW_DECAY_PCT
HY_MAX_DECAY = math.log(HY_DECAY_TARGET) / HY_FAST_DECAY_PCT

FN_W = 1024
FN_GROUPS = 4
FN_GROUP_W = FN_W // FN_GROUPS
IN_CD_W = HY_IN_W + FN_W
MIX_CD_W = HY_W + FN_W

N_EXPERTS = 16
EC_CAPACITY_FACTOR = 2
EXPERT_FF = 1408

kernel_name = 'hybrid_mla_natten_hyena_fnet_ecmoe_dit'


def layer_norm(x, g, b):
    xf = x.astype(jnp.float32)
    mu = jnp.mean(xf, -1, keepdims=True)
    var = jnp.mean(jnp.square(xf - mu), -1, keepdims=True)
    y = (xf - mu) * lax.rsqrt(var + LN_EPS) * g.astype(jnp.float32) + b.astype(jnp.float32)
    return y.astype(x.dtype)


def rms_norm(x, g):
    xf = x.astype(jnp.float32)
    y = xf * lax.rsqrt(jnp.mean(jnp.square(xf), -1, keepdims=True) + LN_EPS) * g.astype(jnp.float32)
    return y.astype(x.dtype)


def ada_params(cvec, w, b):
    return jax.nn.silu(cvec) @ w + b


def modulate(h, shift, scale):
    return h * (1 + scale) + shift


def post_norm(h, y, gate, g, b):
    return layer_norm(DEEPNORM_ALPHA * h + gate * y, g, b)


def axial_rope_tables(n_tok):
    t = np.arange(n_tok)
    row = (t // GRID_W).astype(np.float32)
    col = (t % GRID_W).astype(np.float32)
    n_freq = MLA_ROPE // 4
    inv = (ROPE_THETA ** (-np.arange(n_freq, dtype=np.float32) / n_freq)).astype(np.float32)
    ang = np.concatenate([row[:, None] * inv, col[:, None] * inv], axis=1)
    return np.cos(ang).astype(np.float32), np.sin(ang).astype(np.float32)


def apply_rope(t, cos, sin):
    cos = jnp.asarray(cos, t.dtype)
    sin = jnp.asarray(sin, t.dtype)
    t1, t2 = jnp.split(t, 2, axis=-1)
    return jnp.concatenate([t1 * cos - t2 * sin, t2 * cos + t1 * sin], axis=-1)


def mla_up(ckv, kv_norm, w_ukv):
    B, L, _ = ckv.shape
    kv = (rms_norm(ckv, kv_norm) @ w_ukv).reshape(B, L, MLA_HEADS, MLA_NOPE + MLA_V)
    return kv[..., :MLA_NOPE], kv[..., MLA_NOPE:]


def mla_attend_latent(q_nope, q_pe, k_nope, k_pe, v, kc_nope, kc_pe, vc):
    B, S, H, _ = q_nope.shape
    nb = S // Q_BLOCK
    scale = MLA_QK ** -0.5

    def to_blocks(t):
        return jnp.moveaxis(t.reshape(B, nb, Q_BLOCK, *t.shape[2:]), 1, 0)

    def one_block(qs):
        qn, qp = qs
        s_lat = jnp.einsum('bqhd,bkhd->bhqk', qn, k_nope) + jnp.einsum('bqhr,bkr->bhqk', qp, k_pe)
        s_ctx = jnp.einsum('bqhd,bkhd->bhqk', qn, kc_nope) + jnp.einsum('bqhr,bkr->bhqk', qp, kc_pe)
        logits = jnp.concatenate([s_lat, s_ctx], -1).astype(jnp.float32) * scale
        p = jax.nn.softmax(logits, axis=-1).astype(v.dtype)
        return (jnp.einsum('bhqk,bkhd->bqhd', p[..., :S], v)
                + jnp.einsum('bhqk,bkhd->bqhd', p[..., S:], vc))

    out = lax.map(one_block, (to_blocks(q_nope), to_blocks(q_pe)))
    return jnp.moveaxis(out, 0, 1).reshape(B, S, H * MLA_V)


def na_tables(rows, kh):
    r = np.arange(rows)
    rs = np.clip(r - kh // 2, 0, rows - kh)
    key_rows = rs[:, None] + np.arange(kh)[None, :]
    ncb = GRID_W // NA_QC
    m = np.arange(ncb)
    cb = np.clip(m * NA_QC - NA_KW // 2, 0, GRID_W - NA_KC)
    key_cols = cb[:, None] + np.arange(NA_KC)[None, :]
    idx = key_rows[:, None, :, None] * GRID_W + key_cols[None, :, None, :]
    qcol = m[:, None] * NA_QC + np.arange(NA_QC)[None, :]
    cs = np.clip(qcol - NA_KW // 2, 0, GRID_W - NA_KW)
    kcol = key_cols[:, None, :]
    col_ok = (kcol >= cs[..., None]) & (kcol < cs[..., None] + NA_KW)
    mask = np.broadcast_to(col_ok[:, :, None, :], (ncb, NA_QC, kh, NA_KC)).reshape(ncb, NA_QC, kh * NA_KC)
    dr = key_rows - r[:, None] + (NA_KH - 1)
    dc = np.clip(kcol - qcol[..., None], -(NA_KW - 1), NA_KW - 1) + (NA_KW - 1)
    return (idx.reshape(rows, ncb * kh * NA_KC).astype(np.int32), dr.astype(np.int32),
            dc.astype(np.int32), mask)


def na_attend_latent(q, k, v, kc, vc, rpb):
    B, S, H, d = q.shape
    rows = S // GRID_W
    kh = min(NA_KH, rows)
    ncb = GRID_W // NA_QC
    nk = kh * NA_KC
    idx, dr, dc, mask = na_tables(rows, kh)
    dc = jnp.asarray(dc)
    mask = jnp.asarray(mask)
    scale = d ** -0.5
    q_rows = jnp.moveaxis(q.reshape(B, rows, ncb, NA_QC, H, d), 1, 0)

    def one_row(args):
        q_r, idx_r, dr_r = args
        k_r = jnp.take(k, idx_r, axis=1).reshape(B, ncb, nk, H, d)
        v_r = jnp.take(v, idx_r, axis=1).reshape(B, ncb, nk, H, d)
        bias = rpb[:, dr_r[None, None, :, None], dc[:, :, None, :]].reshape(H, ncb, NA_QC, nk)
        s_win = jnp.einsum('bmqhd,bmkhd->bhmqk', q_r, k_r).astype(jnp.float32) * scale + bias.astype(jnp.float32)
        s_win = jnp.where(mask, s_win, NEG_INF)
        s_ctx = jnp.einsum('bmqhd,bchd->bhmqc', q_r, kc).astype(jnp.float32) * scale
        p = jax.nn.softmax(jnp.concatenate([s_win, s_ctx], -1), axis=-1).astype(v.dtype)
        return (jnp.einsum('bhmqk,bmkhd->bmqhd', p[..., :nk], v_r)
                + jnp.einsum('bhmqc,bchd->bmqhd', p[..., nk:], vc))

    out = lax.map(one_row, (q_rows, jnp.asarray(idx), jnp.asarray(dr)))
    return jnp.moveaxis(out, 0, 1).reshape(B, S, H * d)


def mixer_ab(u, uc, w_in, kv_norm, w_ukv, rpb, w_out, ctx_queries):
    B, S, _ = u.shape
    Lc = uc.shape[1]
    cos, sin = axial_rope_tables(S)
    p = u @ w_in
    q_mla = p[..., :OFF_CKV].reshape(B, S, MLA_HEADS, MLA_QK)
    q_nope = q_mla[..., :MLA_NOPE]
    q_pe = apply_rope(q_mla[..., MLA_NOPE:], cos[:, None, :], sin[:, None, :])
    k_nope, v_mla = mla_up(p[..., OFF_CKV:OFF_KPE], kv_norm, w_ukv)
    k_pe = apply_rope(p[..., OFF_KPE:OFF_QNA], cos, sin)
    q_na = p[..., OFF_QNA:OFF_KNA].reshape(B, S, NA_HEADS, NA_HEAD_DIM)
    k_na = p[..., OFF_KNA:OFF_VNA].reshape(B, S, NA_HEADS, NA_HEAD_DIM)
    v_na = p[..., OFF_VNA:].reshape(B, S, NA_HEADS, NA_HEAD_DIM)
    pc_mla = uc @ w_in[:, OFF_CKV:OFF_QNA]
    kc_nope, vc_mla = mla_up(pc_mla[..., :MLA_KV_RANK], kv_norm, w_ukv)
    kc_pe = pc_mla[..., MLA_KV_RANK:]
    pc_na = uc @ w_in[:, OFF_KNA:]
    kc_na = pc_na[..., :NA_HEADS * NA_HEAD_DIM].reshape(B, Lc, NA_HEADS, NA_HEAD_DIM)
    vc_na = pc_na[..., NA_HEADS * NA_HEAD_DIM:].reshape(B, Lc, NA_HEADS, NA_HEAD_DIM)

    a_out = mla_attend_latent(q_nope, q_pe, k_nope, k_pe, v_mla, kc_nope, kc_pe, vc_mla)
    b_out = na_attend_latent(q_na, k_na, v_na, kc_na, vc_na, rpb)
    y = jnp.concatenate([a_out, b_out], -1) @ w_out

    yc = None
    if ctx_queries:
        qc = (uc @ w_in[:, :OFF_CKV]).reshape(B, Lc, MLA_HEADS, MLA_QK)
        s = (jnp.einsum('bqhd,bkhd->bhqk', qc[..., :MLA_NOPE], kc_nope)
             + jnp.einsum('bqhr,bkr->bhqk', qc[..., MLA_NOPE:], kc_pe)).astype(jnp.float32) * MLA_QK ** -0.5
        ac = jnp.einsum('bhqk,bkhd->bqhd', jax.nn.softmax(s, -1).astype(vc_mla.dtype), vc_mla).reshape(B, Lc, -1)
        qn = (uc @ w_in[:, OFF_QNA:OFF_KNA]).reshape(B, Lc, NA_HEADS, NA_HEAD_DIM)
        s = jnp.einsum('bqhd,bkhd->bhqk', qn, kc_na).astype(jnp.float32) * NA_HEAD_DIM ** -0.5
        bc = jnp.einsum('bhqk,bkhd->bqhd', jax.nn.softmax(s, -1).astype(vc_na.dtype), vc_na).reshape(B, Lc, -1)
        yc = jnp.concatenate([ac, bc], -1) @ w_out
    return y, yc


def hyena_filters(L, fw1, fb1, ff1, fw2, fb2, ff2, fw3):
    f32 = jnp.float32
    t01 = jnp.linspace(0.0, 1.0, L, dtype=f32)
    w = 2.0 * math.pi * jnp.arange(L, dtype=f32) / L
    bands = jnp.linspace(1e-4, HY_BANDS - 1, HY_BANDS, dtype=f32)
    z = jnp.concatenate([t01[:, None], jnp.cos(w[:, None] * bands), -jnp.sin(w[:, None] * bands)], -1)
    hdn = jnp.sin(ff1.astype(f32) * (z @ fw1.astype(f32) + fb1.astype(f32)))
    hdn = jnp.sin(ff2.astype(f32) * (hdn @ fw2.astype(f32) + fb2.astype(f32)))
    filt = hdn @ fw3.astype(f32)
    deltas = jnp.abs(jnp.linspace(HY_MIN_DECAY, HY_MAX_DECAY, HY_W, dtype=f32))
    decay = jnp.exp(-t01[:, None] * deltas)
    return filt.reshape(L, HY_ORDER, 2, HY_W) * decay[:, None, None, :]


def bidir_fftconv(z, hf, hb, skip):
    L = z.shape[1]
    zf = z.astype(jnp.float32)
    h_circ = jnp.concatenate([hf, jnp.zeros_like(hf[:1]), hb[:0:-1]], axis=0)
    spec = jnp.fft.rfft(zf, n=2 * L, axis=1) * jnp.fft.rfft(h_circ, n=2 * L, axis=0)[None]
    y = jnp.fft.irfft(spec, n=2 * L, axis=1)[:, :L]
    return (y + zf * skip.astype(jnp.float32)).astype(z.dtype)


def hyena_mix(p, conv_w, conv_b, filt, skip):
    L = p.shape[1]
    half = HY_SHORT // 2
    pad = jnp.pad(p, ((0, 0), (half, half), (0, 0)))
    s = conv_b + sum(pad[:, k:k + L] * conv_w[k] for k in range(HY_SHORT))
    parts = jnp.split(s, HY_ORDER + 1, axis=-1)
    z = parts[0]
    for o in range(HY_ORDER):
        z = parts[o + 1] * bidir_fftconv(z, filt[:, o, 0], filt[:, o, 1], skip[o])
    return z


def fnet_mix(p):
    B, L, _ = p.shape
    g = p.astype(jnp.float32).reshape(B, L, FN_GROUPS, FN_GROUP_W)
    y = jnp.fft.fft2(g, axes=(1, 3), norm='ortho').real
    return y.reshape(B, L, FN_W).astype(p.dtype)


def mixer_cd(u, w_in, conv_w, conv_b, fw1, fb1, ff1, fw2, fb2, ff2, fw3, skip, w_out):
    p = u @ w_in
    filt = hyena_filters(u.shape[1], fw1, fb1, ff1, fw2, fb2, ff2, fw3)
    y_hy = hyena_mix(p[..., :HY_IN_W], conv_w, conv_b, filt, skip)
    y_fn = fnet_mix(p[..., HY_IN_W:])
    return jnp.concatenate([y_hy, y_fn], -1) @ w_out


def ec_moe(u, router, w1, w3, w2):
    B, L, D = u.shape
    cap = EC_CAPACITY_FACTOR * L // N_EXPERTS
    aff = jax.nn.softmax((u @ router).astype(jnp.float32), axis=-1)
    gate, idx = lax.top_k(jnp.swapaxes(aff, 1, 2), cap)
    xe = jax.vmap(lambda ub, ib: ub[ib])(u, idx)
    hdn = jax.nn.silu(jnp.einsum('becd,edf->becf', xe, w1)) * jnp.einsum('becd,edf->becf', xe, w3)
    ye = jnp.einsum('becf,efd->becd', hdn, w2) * gate[..., None].astype(u.dtype)
    return jax.vmap(lambda yb, ib: jnp.zeros((L, D), yb.dtype).at[ib.reshape(-1)].add(yb.reshape(-1, D)))(ye, idx)


def setup_inputs(seed: int = 0) -> dict:
    key = jax.random.key(seed)
    ks = iter(jax.random.split(key, 40))
    D = D_MODEL

    def nrm(shape, std):
        return std * jax.random.normal(next(ks), shape, jnp.float32)

    return {
        'x': nrm((BATCH, SEQ, D), 1.0),
        'c': nrm((BATCH, D), 1.0),
        'ctx': nrm((BATCH, CTX_LEN, D), 1.0),
        'c_ctx': nrm((D,), 1.0),
        'ada_w': nrm((DEPTH, D, 6 * D), D ** -0.5),
        'ada_b': nrm((DEPTH, 6 * D), 0.01),
        'ln1_g': 1.0 + nrm((DEPTH, D), 0.02),
        'ln1_b': nrm((DEPTH, D), 0.02),
        'ln2_g': 1.0 + nrm((DEPTH, D), 0.02),
        'ln2_b': nrm((DEPTH, D), 0.02),
        'router': nrm((DEPTH, D, N_EXPERTS), D ** -0.5),
        'exp_w1': nrm((DEPTH, N_EXPERTS, D, EXPERT_FF), D ** -0.5),
        'exp_w3': nrm((DEPTH, N_EXPERTS, D, EXPERT_FF), D ** -0.5),
        'exp_w2': nrm((DEPTH, N_EXPERTS, EXPERT_FF, D), EXPERT_FF ** -0.5 * DEEPNORM_BETA),
        'ab_w_in': nrm((N_EVEN, D, IN_AB_W), D ** -0.5),
        'ab_kv_norm': 1.0 + nrm((N_EVEN, MLA_KV_RANK), 0.02),
        'ab_w_ukv': nrm((N_EVEN, MLA_KV_RANK, MLA_HEADS * (MLA_NOPE + MLA_V)), MLA_KV_RANK ** -0.5),
        'ab_rpb': nrm((N_EVEN, NA_HEADS, 2 * NA_KH - 1, 2 * NA_KW - 1), 0.1),
        'ab_w_out': nrm((N_EVEN, MIX_AB_W, D), MIX_AB_W ** -0.5 * DEEPNORM_BETA),
        'cd_w_in': nrm((N_ODD, D, IN_CD_W), D ** -0.5),
        'cd_conv_w': nrm((N_ODD, HY_SHORT, HY_IN_W), HY_SHORT ** -0.5),
        'cd_conv_b': nrm((N_ODD, HY_IN_W), 0.02),
        'cd_filt_w1': nrm((N_ODD, HY_EMB, HY_FILT_HID), HY_EMB ** -0.5),
        'cd_filt_b1': nrm((N_ODD, HY_FILT_HID), 0.02),
        'cd_filt_freq1': 1.0 + nrm((N_ODD, HY_FILT_HID), 0.02),
        'cd_filt_w2': nrm((N_ODD, HY_FILT_HID, HY_FILT_HID), HY_FILT_HID ** -0.5),
        'cd_filt_b2': nrm((N_ODD, HY_FILT_HID), 0.02),
        'cd_filt_freq2': 1.0 + nrm((N_ODD, HY_FILT_HID), 0.02),
        'cd_filt_w3': nrm((N_ODD, HY_FILT_HID, HY_FILT_OUT), 0.01),
        'cd_skip': nrm((N_ODD, HY_ORDER, HY_W), 0.1),
        'cd_w_out': nrm((N_ODD, MIX_CD_W, D), MIX_CD_W ** -0.5 * DEEPNORM_BETA),
    }


def reference(x, c, ctx, c_ctx, ada_w, ada_b, ln1_g, ln1_b, ln2_g, ln2_b, router, exp_w1, exp_w3, exp_w2,
              ab_w_in, ab_kv_norm, ab_w_ukv, ab_rpb, ab_w_out,
              cd_w_in, cd_conv_w, cd_conv_b, cd_filt_w1, cd_filt_b1, cd_filt_freq1, cd_filt_w2, cd_filt_b2,
              cd_filt_freq2, cd_filt_w3, cd_skip, cd_w_out):
    h = x
    hc = ctx
    for i in range(DEPTH):
        j = i // 2
        attn_layer = (i % 2 == 0)
        ctx_continues = i < LAST_ATTN_LAYER
        m = ada_params(c, ada_w[i], ada_b[i])[:, None, :]
        sh1, sc1, g1, sh2, sc2, g2 = jnp.split(m, 6, axis=-1)
        if ctx_continues:
            mc = jnp.split(ada_params(c_ctx, ada_w[i], ada_b[i]), 6, axis=-1)
        elif attn_layer:
            mc = jnp.split(ada_params(c_ctx, ada_w[i][:, :2 * D_MODEL], ada_b[i][:2 * D_MODEL]), 2, axis=-1)
        if attn_layer or ctx_continues:
            uc = modulate(hc, mc[0], mc[1])
        u = modulate(h, sh1, sc1)
        if attn_layer:
            y, yc = mixer_ab(u, uc, ab_w_in[j], ab_kv_norm[j], ab_w_ukv[j], ab_rpb[j], ab_w_out[j], ctx_continues)
        else:
            cd_args = (cd_w_in[j], cd_conv_w[j], cd_conv_b[j], cd_filt_w1[j], cd_filt_b1[j], cd_filt_freq1[j],
                       cd_filt_w2[j], cd_filt_b2[j], cd_filt_freq2[j], cd_filt_w3[j], cd_skip[j], cd_w_out[j])
            y = mixer_cd(u, *cd_args)
            yc = mixer_cd(uc, *cd_args) if ctx_continues else None
        h = post_norm(h, y, g1, ln1_g[i], ln1_b[i])
        h = post_norm(h, ec_moe(modulate(h, sh2, sc2), router[i], exp_w1[i], exp_w3[i], exp_w2[i]),
                      g2, ln2_g[i], ln2_b[i])
        if ctx_continues:
            hc = post_norm(hc, yc, mc[2], ln1_g[i], ln1_b[i])
            hc = post_norm(hc, ec_moe(modulate(hc, mc[3], mc[4]), router[i], exp_w1[i], exp_w3[i], exp_w2[i]),
                           mc[5], ln2_g[i], ln2_b[i])
    return h
```

```python
import functools
import math

import numpy as np
import jax
import jax.numpy as jnp
from jax import lax
from jax.experimental import pallas as pl
from jax.experimental.pallas import tpu as pltpu

F32 = jnp.float32
BF16 = jnp.bfloat16
I32 = jnp.int32

D_MODEL = 2048
BATCH = 4
SEQ = 2048
DEPTH = 2
CTX_LEN = 256
GRID_W = 64
DEEPNORM_ALPHA = (2.0 * DEPTH) ** 0.25
LN_EPS = 1e-6
NEG_INF = -1e30

MLA_HEADS = 8
MLA_NOPE = 128
MLA_ROPE = 64
MLA_QK = MLA_NOPE + MLA_ROPE
MLA_V = 128
MLA_KV_RANK = 512
ROPE_THETA = 10000.0

NA_HEADS = 8
NA_HEAD_DIM = 128
NA_KH = 8
NA_KW = 16

OFF_CKV = MLA_HEADS * MLA_QK
OFF_KPE = OFF_CKV + MLA_KV_RANK
OFF_QNA = OFF_KPE + MLA_ROPE
OFF_KNA = OFF_QNA + NA_HEADS * NA_HEAD_DIM
OFF_VNA = OFF_KNA + NA_HEADS * NA_HEAD_DIM

HY_W = 1024
HY_ORDER = 2
HY_IN_W = (HY_ORDER + 1) * HY_W
HY_SHORT = 3
HY_BANDS = 16
HY_DECAY_TARGET = 1e-2
HY_MIN_DECAY = math.log(HY_DECAY_TARGET) / 1.5
HY_MAX_DECAY = math.log(HY_DECAY_TARGET) / 0.3

FN_W = 1024
FN_GROUPS = 4
FN_GROUP_W = FN_W // FN_GROUPS

N_EXPERTS = 16
EC_CAPACITY_FACTOR = 2
EXPERT_FF = 1408

LANES = 128
MXU_DIM = 256
VMEM_LIMIT = 56 * 1024 * 1024

QTILE_W = 2 * LANES
T_CKV = MLA_HEADS
T_KPE = T_CKV + MLA_KV_RANK // QTILE_W
T_QNA = T_KPE + 1
T_KNA = T_QNA + NA_HEADS * NA_HEAD_DIM // QTILE_W
T_VNA = T_KNA + NA_HEADS * NA_HEAD_DIM // QTILE_W
N_ABTILES = T_VNA + NA_HEADS * NA_HEAD_DIM // QTILE_W
AB_PERM_W = N_ABTILES * QTILE_W

NA_G = 4
NA_WR = NA_KH + NA_G - 1


def _params(sem, vmem=VMEM_LIMIT):
    return pltpu.CompilerParams(dimension_semantics=sem, vmem_limit_bytes=vmem)


def _dot(a, b):
    return jnp.dot(a, b, preferred_element_type=F32)


def _dot_nt(a, b):
    return lax.dot_general(a, b, (((1,), (1,)), ((), ())), preferred_element_type=F32)


def _dot_tn(a, b):
    return lax.dot_general(a, b, (((0,), (0,)), ((), ())), preferred_element_type=F32)


def _split_bf16(x):
    hi = x.astype(BF16)
    lo = (x - hi.astype(F32)).astype(BF16)
    return hi, lo


def _ada_kernel(c_ref, w_ref, b_ref, o_ref):
    c = c_ref[...]
    a = c / (1.0 + jnp.exp(-c))
    a_hi, a_lo = _split_bf16(a)
    w_hi, w_lo = _split_bf16(w_ref[0])
    rows = a.shape[0]
    r = _dot(jnp.concatenate([a_hi, a_lo], axis=0), w_hi)
    o_ref[0] = r[:rows] + r[rows:] + _dot(a_hi, w_lo) + b_ref[0]


def _ada_call(cv, ada_w, ada_b, tn=1024):
    depth, d, n = ada_w.shape
    rows = cv.shape[0]
    return pl.pallas_call(
        _ada_kernel,
        grid=(depth, n // tn),
        in_specs=[pl.BlockSpec((rows, d), lambda l, j: (0, 0)),
                  pl.BlockSpec((1, d, tn), lambda l, j: (l, 0, j)),
                  pl.BlockSpec((1, 1, tn), lambda l, j: (l, 0, j))],
        out_specs=pl.BlockSpec((1, rows, tn), lambda l, j: (l, 0, j)),
        out_shape=jax.ShapeDtypeStruct((depth, rows, n), F32),
        compiler_params=_params(("arbitrary", "arbitrary")),
        name="ada_params",
    )(cv, ada_w, ada_b.reshape(depth, 1, n))


def _modulate_kernel(x_ref, ss_ref, o_ref):
    o_ref[...] = (x_ref[...] * (1.0 + ss_ref[0, 1:2, :]) + ss_ref[0, 0:1, :]).astype(o_ref.dtype)


def _modulate_call(x2d, ss, rows_per_group, tm=256):
    m, d = x2d.shape
    tpg = rows_per_group // tm
    return pl.pallas_call(
        _modulate_kernel,
        grid=(m // tm,),
        in_specs=[pl.BlockSpec((tm, d), lambda i: (i, 0)),
                  pl.BlockSpec((1, 2, d), lambda i: (i // tpg, 0, 0))],
        out_specs=pl.BlockSpec((tm, d), lambda i: (i, 0)),
        out_shape=jax.ShapeDtypeStruct((m, d), BF16),
        compiler_params=_params(("arbitrary",)),
        name="modulate",
    )(x2d, ss)


def _mm_rope_kernel(a_ref, w_ref, ta_ref, tb_ref, tc_ref, o_ref):
    j = pl.program_id(1)
    acc = _dot(a_ref[...], w_ref[...])
    is_rope = jnp.logical_or(j < T_CKV, j == T_KPE)

    @pl.when(is_rope)
    def _():
        f = jnp.where(j < T_CKV, MLA_QK ** -0.5, 1.0).astype(F32)
        hi = acc[:, LANES:]
        rot = (hi * ta_ref[...] + pltpu.roll(hi, LANES - MLA_ROPE // 2, 1) * tb_ref[...]
               + pltpu.roll(hi, MLA_ROPE // 2, 1) * tc_ref[...])
        o_ref[:, :LANES] = (acc[:, :LANES] * f).astype(o_ref.dtype)
        o_ref[:, LANES:] = (rot * f).astype(o_ref.dtype)

    @pl.when(jnp.logical_not(is_rope))
    def _():
        f = jnp.where(jnp.logical_and(j >= T_QNA, j < T_KNA), NA_HEAD_DIM ** -0.5, 1.0).astype(F32)
        o_ref[...] = (acc * f).astype(o_ref.dtype)


def _mm_rope_call(a, w, tabs, tm):
    m, k = a.shape
    n = w.shape[1]
    ntab = tabs[0].shape[0] // tm
    tab_spec = pl.BlockSpec((tm, LANES), lambda i, j: (i % ntab, 0))
    return pl.pallas_call(
        _mm_rope_kernel,
        grid=(m // tm, n // QTILE_W),
        in_specs=[pl.BlockSpec((tm, k), lambda i, j: (i, 0)),
                  pl.BlockSpec((k, QTILE_W), lambda i, j: (0, j)),
                  tab_spec, tab_spec, tab_spec],
        out_specs=pl.BlockSpec((tm, QTILE_W), lambda i, j: (i, j)),
        out_shape=jax.ShapeDtypeStruct((m, n), BF16),
        compiler_params=_params(("arbitrary", "arbitrary")),
        name="attn_in_proj",
    )(a, w, *tabs)


def _rope_tables(n_tok):
    t = np.arange(n_tok)
    row = (t // GRID_W).astype(np.float32)
    col = (t % GRID_W).astype(np.float32)
    n_freq = MLA_ROPE // 4
    inv = (ROPE_THETA ** (-np.arange(n_freq, dtype=np.float32) / n_freq)).astype(np.float32)
    ang = np.concatenate([row[:, None] * inv, col[:, None] * inv], axis=1)
    cos, sin = np.cos(ang).astype(np.float32), np.sin(ang).astype(np.float32)
    half = MLA_ROPE // 2
    ta = np.zeros((n_tok, LANES), np.float32)
    tb = np.zeros((n_tok, LANES), np.float32)
    tc = np.zeros((n_tok, LANES), np.float32)
    ta[:, :half] = cos
    ta[:, half:2 * half] = cos
    tb[:, :half] = -sin
    tc[:, half:2 * half] = sin
    return jnp.asarray(ta), jnp.asarray(tb), jnp.asarray(tc)


def _identity_rope_tables(n_tok):
    ta = np.zeros((n_tok, LANES), np.float32)
    ta[:, :MLA_ROPE] = 1.0
    z = np.zeros((n_tok, LANES), np.float32)
    return jnp.asarray(ta), jnp.asarray(z), jnp.asarray(z)


def _permute_w_in(w):
    d = w.shape[0]
    zeros = lambda n: jnp.zeros((d, n), w.dtype)
    cols = []
    for h in range(MLA_HEADS):
        cols += [w[:, h * MLA_QK:(h + 1) * MLA_QK], zeros(QTILE_W - MLA_QK)]
    cols += [w[:, OFF_CKV:OFF_KPE], zeros(LANES), w[:, OFF_KPE:OFF_QNA], zeros(LANES - MLA_ROPE),
             w[:, OFF_QNA:]]
    return jnp.concatenate(cols, axis=1).astype(BF16)


def _kvup_kernel(x_ref, g_ref, w_ref, o_ref):
    x = x_ref[...].astype(F32)
    y = x * lax.rsqrt(jnp.mean(x * x, axis=-1, keepdims=True) + LN_EPS) * g_ref[...]
    o_ref[...] = _dot(y.astype(BF16), w_ref[...]).astype(o_ref.dtype)


def _kvup_call(p, g, w, tm):
    m = p.shape[0]
    r, n = w.shape
    return pl.pallas_call(
        _kvup_kernel,
        grid=(m // tm,),
        in_specs=[pl.BlockSpec((tm, r), lambda i: (i, T_CKV * QTILE_W // r)),
                  pl.BlockSpec((1, r), lambda i: (0, 0)),
                  pl.BlockSpec((r, n), lambda i: (0, 0))],
        out_specs=pl.BlockSpec((tm, n), lambda i: (i, 0)),
        out_shape=jax.ShapeDtypeStruct((m, n), BF16),
        compiler_params=_params(("arbitrary",)),
        name="kv_up",
    )(p, g.reshape(1, r), w)


def _permute_w_ukv(w):
    r = w.shape[0]
    w3 = w.reshape(r, MLA_HEADS, MLA_NOPE + MLA_V)
    return jnp.concatenate([w3[:, :, :MLA_NOPE].reshape(r, -1), w3[:, :, MLA_NOPE:].reshape(r, -1)],
                           axis=1).astype(BF16)


def _mla_kernel(q_ref, kn_ref, kp_ref, v_ref, kcn_ref, kcp_ref, vc_ref, o_ref, kcat, vcat):
    s_len = kn_ref.shape[0]

    @pl.when(pl.program_id(2) == 0)
    def _():
        kcat[:s_len, :LANES] = kn_ref[...]
        kcat[:s_len, LANES:] = kp_ref[...]
        kcat[s_len:, :LANES] = kcn_ref[...]
        kcat[s_len:, LANES:] = kcp_ref[...]
        vcat[:s_len, :] = v_ref[...]
        vcat[s_len:, :] = vc_ref[...]

    s = _dot_nt(q_ref[...], kcat[...])
    m = jnp.max(s, axis=-1, keepdims=True)
    p = jnp.exp(s - m)
    l = jnp.sum(p, axis=-1, keepdims=True)
    o = _dot(p.astype(BF16), vcat[...])
    o_ref[...] = (o / l).astype(o_ref.dtype)


def _mla_call(p_lat, kv_lat, p_ctx, kv_ctx, batch, s_len, lc, tq=512):
    nq = s_len // tq
    kpe_blk = T_KPE * 2 + 1
    return pl.pallas_call(
        _mla_kernel,
        grid=(batch, MLA_HEADS, nq),
        in_specs=[pl.BlockSpec((tq, QTILE_W), lambda b, h, i: (b * nq + i, h)),
                  pl.BlockSpec((s_len, LANES), lambda b, h, i: (b, h)),
                  pl.BlockSpec((s_len, LANES), lambda b, h, i: (b, kpe_blk)),
                  pl.BlockSpec((s_len, LANES), lambda b, h, i: (b, MLA_HEADS + h)),
                  pl.BlockSpec((lc, LANES), lambda b, h, i: (b, h)),
                  pl.BlockSpec((lc, LANES), lambda b, h, i: (b, kpe_blk)),
                  pl.BlockSpec((lc, LANES), lambda b, h, i: (b, MLA_HEADS + h))],
        out_specs=pl.BlockSpec((tq, LANES), lambda b, h, i: (b * nq + i, h)),
        out_shape=jax.ShapeDtypeStruct((batch * s_len, MLA_HEADS * MLA_V), BF16),
        scratch_shapes=[pltpu.VMEM((s_len + lc, QTILE_W), BF16),
                        pltpu.VMEM((s_len + lc, LANES), BF16)],
        compiler_params=_params(("arbitrary", "arbitrary", "arbitrary")),
        name="mla_attention",
    )(p_lat, kv_lat, p_lat, kv_lat, kv_ctx, p_ctx, kv_ctx)


def _na_kernel(var_ref, q_ref, k_ref, v_ref, kc_ref, vc_ref, bias_ref, o_ref, *, rows):
    del var_ref
    g = pl.program_id(2)
    start_row = jnp.clip(NA_G * g - NA_KH // 2, 0, rows - NA_WR)
    start = pl.multiple_of(start_row * GRID_W, GRID_W)
    win = NA_WR * GRID_W
    q = q_ref[...]
    sw = _dot_nt(q, k_ref[pl.ds(start, win), :]) + bias_ref[0, 0]
    sc = _dot_nt(q, kc_ref[...])
    m = jnp.maximum(jnp.max(sw, axis=-1, keepdims=True), jnp.max(sc, axis=-1, keepdims=True))
    pw = jnp.exp(sw - m)
    pc = jnp.exp(sc - m)
    l = jnp.sum(pw, axis=-1, keepdims=True) + jnp.sum(pc, axis=-1, keepdims=True)
    o = _dot(pw.astype(BF16), v_ref[pl.ds(start, win), :]) + _dot(pc.astype(BF16), vc_ref[...])
    o_ref[...] = (o / l).astype(o_ref.dtype)


def _na_tables(rows):
    kh = min(NA_KH, rows)
    ng = rows // NA_G
    qr = np.arange(NA_G)[:, None, None, None]
    qc = np.arange(GRID_W)[None, :, None, None]
    kr = np.arange(NA_WR)[None, None, :, None]
    kc = np.arange(GRID_W)[None, None, None, :]
    idx_all, ok_all = [], []
    for g in range(ng):
        start_row = int(np.clip(NA_G * g - NA_KH // 2, 0, rows - NA_WR))
        r = NA_G * g + qr
        krow = start_row + kr
        rs = np.clip(r - kh // 2, 0, rows - kh)
        cs = np.clip(qc - NA_KW // 2, 0, GRID_W - NA_KW)
        ok = (krow >= rs) & (krow < rs + kh) & (kc >= cs) & (kc < cs + NA_KW)
        dr = krow - r + (NA_KH - 1)
        dc = np.clip(kc - qc, -(NA_KW - 1), NA_KW - 1) + (NA_KW - 1)
        flat = np.clip(dr, 0, 2 * NA_KH - 2) * (2 * NA_KW - 1) + dc
        shape = (NA_G * GRID_W, NA_WR * GRID_W)
        idx_all.append(np.broadcast_to(flat, (NA_G, GRID_W, NA_WR, GRID_W)).reshape(shape))
        ok_all.append(np.broadcast_to(ok, (NA_G, GRID_W, NA_WR, GRID_W)).reshape(shape))
    stacked = np.stack([np.where(o, i, -1) for i, o in zip(idx_all, ok_all)])
    uniq, inverse = np.unique(stacked.reshape(ng, -1), axis=0, return_inverse=True)
    uniq = uniq.reshape((-1,) + stacked.shape[1:])
    return inverse.reshape(-1).astype(np.int32), np.maximum(uniq, 0).astype(np.int32), uniq >= 0


def _na_bias(rpb, idx, ok):
    flat = rpb.reshape(rpb.shape[0], -1).astype(F32)
    tab = jnp.take(flat, jnp.asarray(idx), axis=1)
    tab = jnp.where(jnp.asarray(ok)[None], tab, NEG_INF)
    return jnp.transpose(tab, (1, 0, 2, 3))


def _na_call(p_lat, p_ctx, bias, var_map, batch, s_len, lc):
    rows = s_len // GRID_W
    ng = rows // NA_G
    tq = NA_G * GRID_W
    win = NA_WR * GRID_W
    qb, kb, vb = T_QNA * 2, T_KNA * 2, T_VNA * 2
    grid_spec = pltpu.PrefetchScalarGridSpec(
        num_scalar_prefetch=1,
        grid=(batch, NA_HEADS, ng),
        in_specs=[pl.BlockSpec((tq, LANES), lambda b, h, g, vm: (b * ng + g, qb + h)),
                  pl.BlockSpec((s_len, LANES), lambda b, h, g, vm: (b, kb + h)),
                  pl.BlockSpec((s_len, LANES), lambda b, h, g, vm: (b, vb + h)),
                  pl.BlockSpec((lc, LANES), lambda b, h, g, vm: (b, kb + h)),
                  pl.BlockSpec((lc, LANES), lambda b, h, g, vm: (b, vb + h)),
                  pl.BlockSpec((1, 1, tq, win), lambda b, h, g, vm: (vm[g], h, 0, 0))],
        out_specs=pl.BlockSpec((tq, LANES), lambda b, h, g, vm: (b * ng + g, h)),
    )
    return pl.pallas_call(
        functools.partial(_na_kernel, rows=rows),
        grid_spec=grid_spec,
        out_shape=jax.ShapeDtypeStruct((batch * s_len, NA_HEADS * NA_HEAD_DIM), BF16),
        compiler_params=_params(("arbitrary", "arbitrary", "arbitrary")),
        name="na_attention",
    )(var_map, p_lat, p_lat, p_lat, p_ctx, p_ctx, bias)


def _layer_norm(x, g, b):
    mu = jnp.mean(x, axis=-1, keepdims=True)
    xc = x - mu
    var = jnp.mean(xc * xc, axis=-1, keepdims=True)
    return xc * lax.rsqrt(var + LN_EPS) * g + b


def _outln_kernel(a1_ref, a2_ref, w_ref, h_ref, mod_ref, ln_ref, r_ref, h1_ref, u2_ref, lg_ref):
    half = a1_ref.shape[1]
    y = _dot(a1_ref[...], w_ref[:half, :]) + _dot(a2_ref[...], w_ref[half:, :])
    x = DEEPNORM_ALPHA * h_ref[...] + mod_ref[0, 0:1, :] * y
    hn = _layer_norm(x, ln_ref[0:1, :], ln_ref[1:2, :])
    h1_ref[...] = hn
    u = hn * (1.0 + mod_ref[0, 2:3, :]) + mod_ref[0, 1:2, :]
    u_hi, u_lo = _split_bf16(u)
    u2_ref[...] = u_hi
    lg_ref[...] = _dot(u_hi, r_ref[0]) + _dot(u_lo, r_ref[0]) + _dot(u_hi, r_ref[1])


def _outln_call(a1, a2, w, h, mod, ln, router2, rows_per_group, tm=256):
    m, d = h.shape
    half = a1.shape[1]
    tpg = rows_per_group // tm
    return pl.pallas_call(
        _outln_kernel,
        grid=(m // tm,),
        in_specs=[pl.BlockSpec((tm, half), lambda i: (i, 0)),
                  pl.BlockSpec((tm, half), lambda i: (i, 0)),
                  pl.BlockSpec((2 * half, d), lambda i: (0, 0)),
                  pl.BlockSpec((tm, d), lambda i: (i, 0)),
                  pl.BlockSpec((1, 3, d), lambda i: (i // tpg, 0, 0)),
                  pl.BlockSpec((2, d), lambda i: (0, 0)),
                  pl.BlockSpec((2, d, LANES), lambda i: (0, 0, 0))],
        out_specs=[pl.BlockSpec((tm, d), lambda i: (i, 0)),
                   pl.BlockSpec((tm, d), lambda i: (i, 0)),
                   pl.BlockSpec((tm, LANES), lambda i: (i, 0))],
        out_shape=[jax.ShapeDtypeStruct((m, d), F32),
                   jax.ShapeDtypeStruct((m, d), BF16),
                   jax.ShapeDtypeStruct((m, LANES), F32)],
        compiler_params=_params(("arbitrary",)),
        name="out_proj_postnorm",
    )(a1, a2, w, h, mod, ln, router2)


def _router_operand(router):
    d, e = router.shape
    rp = jnp.pad(router.astype(F32), ((0, 0), (0, LANES - e)))
    hi = rp.astype(BF16)
    lo = (rp - hi.astype(F32)).astype(BF16)
    return jnp.stack([hi, lo])


def _route_kernel(lg_ref, slot_ref, aff_ref, tri_ref, *, cap):
    n_tok = lg_ref.shape[2]

    @pl.when(pl.program_id(0) == 0)
    def _():
        chunk = 256
        for r0 in range(0, n_tok, chunk):
            r = r0 + lax.broadcasted_iota(I32, (chunk, n_tok), 0)
            c = lax.broadcasted_iota(I32, (chunk, n_tok), 1)
            tri_ref[r0:r0 + chunk, :] = jnp.where(r < c, 1.0, 0.0).astype(BF16)

    lg = lg_ref[0]
    ex = jnp.exp(lg - jnp.max(lg, axis=0, keepdims=True))
    aff = ex / jnp.sum(ex, axis=0, keepdims=True)
    bits = lax.bitcast_convert_type(aff, I32)
    n_e = lg.shape[0]
    count = lambda mask: jnp.sum(jnp.where(mask, 1.0, 0.0), axis=1, keepdims=True)

    def body(_, lohi):
        lo, hi = lohi
        mid = lo + jnp.right_shift(hi - lo, 1)
        ge = count(bits >= mid) >= cap
        return jnp.where(ge, mid, lo), jnp.where(ge, hi, mid)

    lo0 = jnp.zeros((n_e, 1), I32)
    hi0 = jnp.full((n_e, 1), 0x7F800000, I32)
    thr, _ = lax.fori_loop(0, 31, body, (lo0, hi0))
    gt = bits > thr
    eq = bits == thr
    need = cap - count(gt)
    pre_eq = _dot(jnp.where(eq, 1.0, 0.0).astype(BF16), tri_ref[...])
    sel = jnp.logical_or(gt, jnp.logical_and(eq, pre_eq < need))
    slot = _dot(jnp.where(sel, 1.0, 0.0).astype(BF16), tri_ref[...])
    slot_ref[0] = jnp.where(sel, slot.astype(I32), -1)
    aff_ref[0] = aff


def _route_call(lg_t, cap):
    b, e, n_tok = lg_t.shape
    spec = pl.BlockSpec((1, e, n_tok), lambda i: (i, 0, 0))
    return pl.pallas_call(
        functools.partial(_route_kernel, cap=cap),
        grid=(b,),
        in_specs=[spec],
        out_specs=[spec, spec],
        out_shape=[jax.ShapeDtypeStruct((b, e, n_tok), I32), jax.ShapeDtypeStruct((b, e, n_tok), F32)],
        scratch_shapes=[pltpu.VMEM((n_tok, n_tok), BF16)],
        compiler_params=_params(("arbitrary",)),
        name="ec_route",
    )(lg_t)


def _gather_kernel(slot_ref, u_ref, o_ref):
    cap = o_ref.shape[1]
    n_tok = u_ref.shape[0]
    j = lax.broadcasted_iota(I32, (cap, n_tok), 0)
    onehot = jnp.where(j == slot_ref[0, 0], 1.0, 0.0).astype(BF16)
    o_ref[0] = _dot(onehot, u_ref[...]).astype(o_ref.dtype)


def _gather_call(slot4, u2, cap):
    b, e, _, n_tok = slot4.shape
    d = u2.shape[1]
    return pl.pallas_call(
        _gather_kernel,
        grid=(b, e),
        in_specs=[pl.BlockSpec((1, 1, 1, n_tok), lambda i, j: (i, j, 0, 0)),
                  pl.BlockSpec((n_tok, d), lambda i, j: (i, 0))],
        out_specs=pl.BlockSpec((1, cap, d), lambda i, j: (j, i, 0)),
        out_shape=jax.ShapeDtypeStruct((e, b * cap, d), BF16),
        compiler_params=_params(("arbitrary", "arbitrary")),
        name="moe_gather",
    )(slot4, u2)


def _expert_kernel(x_ref, w1a, w1b, w3a, w3b, w2a, w2b, o_ref, acc_ref, *, nf, odd):
    f = pl.program_id(1)
    wcat = jnp.concatenate([w1a[0, 0].astype(BF16), w1b[0, 0].astype(BF16),
                            w3a[0, 0].astype(BF16), w3b[0, 0].astype(BF16)], axis=1)
    h = _dot(x_ref[0], wcat)
    h1 = h[:, :2 * LANES]
    hd = (h1 / (1.0 + jnp.exp(-h1))) * h[:, 2 * LANES:]
    if odd:
        lane = lax.broadcasted_iota(I32, hd.shape, 1)
        hd = jnp.where(jnp.logical_and(f == nf - 1, lane >= LANES), 0.0, hd)
    w2cat = jnp.concatenate([w2a[0, 0].astype(BF16), w2b[0, 0].astype(BF16)], axis=0)
    contrib = _dot(hd.astype(BF16), w2cat)

    @pl.when(f == 0)
    def _():
        acc_ref[...] = contrib

    @pl.when(f > 0)
    def _():
        acc_ref[...] += contrib

    @pl.when(f == nf - 1)
    def _():
        o_ref[0] = acc_ref[...].astype(o_ref.dtype)


def _expert_call(xe, w1, w3, w2, layer):
    e, m, d = xe.shape
    ff = w1.shape[3]
    nt = ff // LANES
    nf = (nt + 1) // 2
    ta = lambda f: 2 * f
    tb = lambda f: jnp.minimum(2 * f + 1, nt - 1)
    col = lambda t: pl.BlockSpec((1, 1, d, LANES), lambda i, f: (layer, i, 0, t(f)))
    row = lambda t: pl.BlockSpec((1, 1, LANES, d), lambda i, f: (layer, i, t(f), 0))
    return pl.pallas_call(
        functools.partial(_expert_kernel, nf=nf, odd=bool(nt % 2)),
        grid=(e, nf),
        in_specs=[pl.BlockSpec((1, m, d), lambda i, f: (i, 0, 0)),
                  col(ta), col(tb), col(ta), col(tb), row(ta), row(tb)],
        out_specs=pl.BlockSpec((1, m, d), lambda i, f: (i, 0, 0)),
        out_shape=jax.ShapeDtypeStruct((e, m, d), BF16),
        scratch_shapes=[pltpu.VMEM((m, d), F32)],
        compiler_params=_params(("arbitrary", "arbitrary")),
        name="moe_experts",
    )(xe, w1, w1, w3, w3, w2, w2)


def _scatter_kernel(slot_ref, aff_ref, ye_ref, h_ref, mod_ref, ln_ref, *rest, with_next):
    if with_next:
        nmod_ref, h2_ref, un_ref, acc_ref = rest
    else:
        h2_ref, acc_ref = rest
    e = pl.program_id(2)
    cap = ye_ref.shape[1]
    tt = h_ref.shape[0]
    j = lax.broadcasted_iota(I32, (cap, tt), 0)
    pt = jnp.where(j == slot_ref[0, 0], aff_ref[0, 0], 0.0).astype(BF16)
    contrib = _dot_tn(pt, ye_ref[0])

    @pl.when(e == 0)
    def _():
        acc_ref[...] = contrib

    @pl.when(e > 0)
    def _():
        acc_ref[...] += contrib

    @pl.when(e == pl.num_programs(2) - 1)
    def _():
        x = DEEPNORM_ALPHA * h_ref[...] + mod_ref[0] * acc_ref[...]
        hn = _layer_norm(x, ln_ref[0:1, :], ln_ref[1:2, :])
        h2_ref[...] = hn
        if with_next:
            un_ref[...] = (hn * (1.0 + nmod_ref[0, 1:2, :]) + nmod_ref[0, 0:1, :]).astype(un_ref.dtype)


def _scatter_call(slot4, aff4, ye, h1, gate, ln, next_ss, cap, tt=512):
    b, e, _, n_tok = slot4.shape
    m, d = h1.shape
    nt = n_tok // tt
    with_next = next_ss is not None
    tok_spec = pl.BlockSpec((1, 1, 1, tt), lambda i, t, j: (i, j, 0, t))
    row_spec = pl.BlockSpec((tt, d), lambda i, t, j: (i * nt + t, 0))
    in_specs = [tok_spec, tok_spec,
                pl.BlockSpec((1, cap, d), lambda i, t, j: (j, i, 0)),
                row_spec,
                pl.BlockSpec((1, 1, d), lambda i, t, j: (i, 0, 0)),
                pl.BlockSpec((2, d), lambda i, t, j: (0, 0))]
    args = [slot4, aff4, ye, h1, gate, ln]
    out_specs = [row_spec]
    out_shape = [jax.ShapeDtypeStruct((m, d), F32)]
    if with_next:
        in_specs.append(pl.BlockSpec((1, 2, d), lambda i, t, j: (i, 0, 0)))
        args.append(next_ss)
        out_specs.append(row_spec)
        out_shape.append(jax.ShapeDtypeStruct((m, d), BF16))
    res = pl.pallas_call(
        functools.partial(_scatter_kernel, with_next=with_next),
        grid=(b, nt, e),
        in_specs=in_specs,
        out_specs=out_specs,
        out_shape=out_shape,
        scratch_shapes=[pltpu.VMEM((tt, d), F32)],
        compiler_params=_params(("arbitrary", "arbitrary", "arbitrary")),
        name="moe_combine_postnorm",
    )(*args)
    return res if with_next else (res[0], None)


def _ec_moe(h1, u2, logits, w1, w3, w2, layer, gate, ln, next_ss, batch, n_tok):
    cap = EC_CAPACITY_FACTOR * n_tok // N_EXPERTS
    lg_t = jnp.swapaxes(logits[:, :N_EXPERTS].reshape(batch, n_tok, N_EXPERTS), 1, 2)
    slot, aff = _route_call(lg_t, cap)
    slot4 = slot.reshape(batch, N_EXPERTS, 1, n_tok)
    aff4 = aff.reshape(batch, N_EXPERTS, 1, n_tok)
    xe = _gather_call(slot4, u2, cap)
    ye = _expert_call(xe, w1, w3, w2, layer)
    return _scatter_call(slot4, aff4, ye, h1, gate, ln, next_ss, cap)


def _mm_kernel(a_ref, w_ref, o_ref):
    o_ref[...] = _dot(a_ref[...], w_ref[...]).astype(o_ref.dtype)


def _mm_call(a, w, out_dtype, tm=1024, tn=512):
    m, k = a.shape
    n = w.shape[1]
    return pl.pallas_call(
        _mm_kernel,
        grid=(m // tm, n // tn),
        in_specs=[pl.BlockSpec((tm, k), lambda i, j: (i, 0)),
                  pl.BlockSpec((k, tn), lambda i, j: (0, j))],
        out_specs=pl.BlockSpec((tm, tn), lambda i, j: (i, j)),
        out_shape=jax.ShapeDtypeStruct((m, n), out_dtype),
        compiler_params=_params(("arbitrary", "arbitrary")),
        name="matmul",
    )(a, w)


def _sconv_kernel(p_ref, w_ref, b_ref, o_ref):
    x = p_ref[...]
    n = x.shape[0]
    r = lax.broadcasted_iota(I32, x.shape, 0)
    prev = jnp.where(r == 0, 0.0, pltpu.roll(x, 1, 0))
    nxt = jnp.where(r == n - 1, 0.0, pltpu.roll(x, n - 1, 0))
    o_ref[...] = (b_ref[...] + w_ref[0:1, :] * prev + w_ref[1:2, :] * x + w_ref[2:3, :] * nxt
                  ).astype(o_ref.dtype)


def _sconv_call(p, conv_w, conv_b, col0, ncols, out_dtype, batch, n_tok, tc=512):
    c0 = col0 // tc
    return pl.pallas_call(
        _sconv_kernel,
        grid=(batch, ncols // tc),
        in_specs=[pl.BlockSpec((n_tok, tc), lambda b, j: (b, c0 + j)),
                  pl.BlockSpec((HY_SHORT, tc), lambda b, j: (0, c0 + j)),
                  pl.BlockSpec((1, tc), lambda b, j: (0, c0 + j))],
        out_specs=pl.BlockSpec((n_tok, tc), lambda b, j: (b, j)),
        out_shape=jax.ShapeDtypeStruct((batch * n_tok, ncols), out_dtype),
        compiler_params=_params(("arbitrary", "arbitrary")),
        name="hyena_short_conv",
    )(p, conv_w, conv_b.reshape(1, -1))


def _dft_fwd_kernel(fc_ref, fs_ref, r1_ref, r2_ref, *rest, spectral):
    xr = _dot(fc_ref[...], r1_ref[...])
    xi = _dot(fs_ref[...], r2_ref[...])
    if spectral:
        h_ref, nyq_ref, o_ref = rest
        hr = h_ref[0, 0]
        hi = h_ref[0, 1]
        tm = xr.shape[0]
        k = pl.program_id(2) * tm + lax.broadcasted_iota(I32, xr.shape, 0)
        dc = k == 0
        yr = xr * hr - jnp.where(dc, 0.0, xi * hi)
        yi = jnp.where(dc, xi * nyq_ref[...], xr * hi + xi * hr)
    else:
        (o_ref,) = rest
        yr, yi = xr, xi
    o_ref[0, 0] = yr.astype(o_ref.dtype)
    o_ref[0, 1] = yi.astype(o_ref.dtype)


def _dft_fwd_call(fc, fs, r1, r2, spec, out_dtype, batch, n_tok, tm=512, tn=512):
    ncols = r1.shape[1]
    f_spec = pl.BlockSpec((tm, n_tok), lambda b, j, i: (i, 0))
    r_spec = pl.BlockSpec((n_tok, tn), lambda b, j, i: (b, j))
    in_specs = [f_spec, f_spec, r_spec, r_spec]
    args = [fc, fs, r1, r2]
    if spec is not None:
        h, nyq, hcol0 = spec
        c0 = hcol0 // tn
        in_specs += [pl.BlockSpec((1, 2, tm, tn), lambda b, j, i: (0, 0, i, c0 + j)),
                     pl.BlockSpec((1, tn), lambda b, j, i: (0, c0 + j))]
        args += [h, nyq]
    return pl.pallas_call(
        functools.partial(_dft_fwd_kernel, spectral=spec is not None),
        grid=(batch, ncols // tn, n_tok // tm),
        in_specs=in_specs,
        out_specs=pl.BlockSpec((1, 2, tm, tn), lambda b, j, i: (b, 0, i, j)),
        out_shape=jax.ShapeDtypeStruct((batch, 2, n_tok, ncols), out_dtype),
        compiler_params=_params(("arbitrary", "arbitrary", "arbitrary")),
        name="dft_forward",
    )(*args)


def _dual_kernel(a1_ref, a2_ref, b1_ref, b2_ref, *rest, hyena, planes):
    b1 = b1_ref[0, 0] if planes else b1_ref[...]
    b2 = b2_ref[0, 0] if planes else b2_ref[...]
    y = _dot(a1_ref[...], b1) + _dot(a2_ref[...], b2)
    if hyena:
        s_ref, z_ref, skip_ref, o_ref = rest
        y = s_ref[...].astype(F32) * (y + skip_ref[...] * z_ref[...].astype(F32))
    else:
        (o_ref,) = rest
    o_ref[...] = y.astype(o_ref.dtype)


def _dual_call(a1, a2, b1, b2, hy, batch, n_tok, ncols, tm=512, tn=512):
    nm = n_tok // tm
    a_spec = pl.BlockSpec((tm, n_tok), lambda b, j, i: (i, 0))
    planes = b1.ndim == 4
    if planes:
        b_specs = [pl.BlockSpec((1, 1, n_tok, tn), lambda b, j, i: (b, 0, 0, j)),
                   pl.BlockSpec((1, 1, n_tok, tn), lambda b, j, i: (b, 1, 0, j))]
    else:
        b_specs = [pl.BlockSpec((n_tok, tn), lambda b, j, i: (b, j))] * 2
    in_specs = [a_spec, a_spec] + b_specs
    args = [a1, a2, b1, b2]
    o_spec = pl.BlockSpec((tm, tn), lambda b, j, i: (b * nm + i, j))
    if hy is not None:
        s, scol0, z, skip = hy
        c0 = scol0 // tn
        in_specs += [pl.BlockSpec((tm, tn), lambda b, j, i: (b * nm + i, c0 + j)),
                     o_spec,
                     pl.BlockSpec((1, tn), lambda b, j, i: (0, j))]
        args += [s, z, skip]
    return pl.pallas_call(
        functools.partial(_dual_kernel, hyena=hy is not None, planes=planes),
        grid=(batch, ncols // tn, nm),
        in_specs=in_specs,
        out_specs=o_spec,
        out_shape=jax.ShapeDtypeStruct((batch * n_tok, ncols), BF16),
        compiler_params=_params(("arbitrary", "arbitrary", "arbitrary")),
        name="seq_mix_matmul",
    )(*args)


def _fnet_ch_kernel(p_ref, w_ref, gc_ref, gs_ref):
    r = _dot(p_ref[...].astype(BF16), w_ref[...])
    w = gc_ref.shape[1]
    gc_ref[...] = r[:, :w].astype(gc_ref.dtype)
    gs_ref[...] = r[:, w:].astype(gs_ref.dtype)


def _fnet_ch_call(p, w, col0, tm=1024):
    m = p.shape[0]
    gw = w.shape[0]
    c0 = col0 // gw
    o_spec = pl.BlockSpec((tm, gw), lambda i, g: (i, g))
    return pl.pallas_call(
        _fnet_ch_kernel,
        grid=(m // tm, FN_GROUPS),
        in_specs=[pl.BlockSpec((tm, gw), lambda i, g: (i, c0 + g)),
                  pl.BlockSpec((gw, 2 * gw), lambda i, g: (0, 0))],
        out_specs=[o_spec, o_spec],
        out_shape=[jax.ShapeDtypeStruct((m, FN_GROUPS * gw), BF16)] * 2,
        compiler_params=_params(("arbitrary", "arbitrary")),
        name="fnet_channel_dft",
    )(p, w)


def _cos_sin_matrix(n_rows, n_cols, period, split=64):
    r = np.arange(n_rows, dtype=np.int64)[:, None]
    c_hi = (np.arange(n_cols // split, dtype=np.int64) * split)[None, :]
    c_lo = np.arange(split, dtype=np.int64)[None, :]
    ang = lambda c: 2.0 * np.pi * ((r * c) % period).astype(np.float64) / period
    ca, sa = jnp.asarray(np.cos(ang(c_hi)), F32), jnp.asarray(np.sin(ang(c_hi)), F32)
    cb, sb = jnp.asarray(np.cos(ang(c_lo)), F32), jnp.asarray(np.sin(ang(c_lo)), F32)
    cos = ca[:, :, None] * cb[:, None, :] - sa[:, :, None] * sb[:, None, :]
    sin = sa[:, :, None] * cb[:, None, :] + ca[:, :, None] * sb[:, None, :]
    return cos.reshape(n_rows, n_cols), sin.reshape(n_rows, n_cols)


def _hyena_dft_operands(n_tok):
    n = 2 * n_tok
    cos, sin = _cos_sin_matrix(n_tok, n_tok, n)
    alt = jnp.asarray(((-1.0) ** np.arange(n_tok)), F32)
    row0 = (jnp.arange(n_tok) == 0)
    fc = cos
    fs = jnp.where(row0[:, None], alt[None, :], -sin)
    ck = jnp.where(row0, 1.0, 2.0)[None, :] / n
    gc = cos * ck
    gs = jnp.where(row0[None, :], alt[:, None] / n, -sin * (2.0 / n))
    return fc.astype(BF16), fs.astype(BF16), gc.astype(BF16), gs.astype(BF16), alt


def _fnet_operands(n_tok, gw):
    scale = 1.0 / math.sqrt(n_tok * gw)
    cl, sl = _cos_sin_matrix(n_tok, n_tok, n_tok)
    cw, sw = _cos_sin_matrix(gw, gw, gw)
    return ((cl * scale).astype(BF16), (-sl * scale).astype(BF16),
            jnp.concatenate([cw, sw], axis=1).astype(BF16))


def _hyena_filters(n_tok, fw1, fb1, ff1, fw2, fb2, ff2, fw3):
    t01 = jnp.linspace(0.0, 1.0, n_tok, dtype=F32)
    w = 2.0 * math.pi * jnp.arange(n_tok, dtype=F32) / n_tok
    bands = jnp.linspace(1e-4, HY_BANDS - 1, HY_BANDS, dtype=F32)
    z = jnp.concatenate([t01[:, None], jnp.cos(w[:, None] * bands), -jnp.sin(w[:, None] * bands)], -1)
    hp = lax.Precision.HIGHEST
    hdn = jnp.sin(ff1 * (jnp.dot(z, fw1, precision=hp) + fb1))
    hdn = jnp.sin(ff2 * (jnp.dot(hdn, fw2, precision=hp) + fb2))
    filt = jnp.dot(hdn, fw3, precision=hp)
    deltas = jnp.abs(jnp.linspace(HY_MIN_DECAY, HY_MAX_DECAY, HY_W, dtype=F32))
    decay = jnp.exp(-t01[:, None] * deltas)
    return filt.reshape(n_tok, HY_ORDER, 2, HY_W) * decay[:, None, None, :]


def kernel(x, c, ctx, c_ctx, ada_w, ada_b, ln1_g, ln1_b, ln2_g, ln2_b, router, exp_w1, exp_w3, exp_w2,
           ab_w_in, ab_kv_norm, ab_w_ukv, ab_rpb, ab_w_out,
           cd_w_in, cd_conv_w, cd_conv_b, cd_filt_w1, cd_filt_b1, cd_filt_freq1, cd_filt_w2, cd_filt_b2,
           cd_filt_freq2, cd_filt_w3, cd_skip, cd_w_out):
    batch, n_tok, d = x.shape
    lc = ctx.shape[1]
    x2d = x.reshape(batch * n_tok, d)
    ctx2d = ctx.reshape(batch * lc, d)

    pad_rows = (-(batch + 1)) % 8
    cv = jnp.concatenate([c, c_ctx[None, :], jnp.zeros((pad_rows, d), F32)], axis=0)
    ada = _ada_call(cv, ada_w, ada_b)
    mods = [ada[i, :batch].reshape(batch, 6, d) for i in range(DEPTH)]
    ctx_ss = ada[0, batch, :2 * d].reshape(1, 2, d)

    u_lat = _modulate_call(x2d, mods[0][:, 0:2], n_tok)
    u_ctx = _modulate_call(ctx2d, ctx_ss, batch * lc)
    w_in = _permute_w_in(ab_w_in[0])
    p_lat = _mm_rope_call(u_lat, w_in, _rope_tables(n_tok), tm=1024)
    p_ctx = _mm_rope_call(u_ctx, w_in, _identity_rope_tables(lc), tm=lc)
    w_ukv = _permute_w_ukv(ab_w_ukv[0])
    kv_lat = _kvup_call(p_lat, ab_kv_norm[0], w_ukv, tm=512)
    kv_ctx = _kvup_call(p_ctx, ab_kv_norm[0], w_ukv, tm=lc)
    a_out = _mla_call(p_lat, kv_lat, p_ctx, kv_ctx, batch, n_tok, lc)
    var_map, na_idx, na_ok = _na_tables(n_tok // GRID_W)
    b_out = _na_call(p_lat, p_ctx, _na_bias(ab_rpb[0], na_idx, na_ok), jnp.asarray(var_map),
                     batch, n_tok, lc)
    ln1 = jnp.stack([ln1_g, ln1_b], axis=1)
    ln2 = jnp.stack([ln2_g, ln2_b], axis=1)
    h1, u2, logits = _outln_call(a_out, b_out, ab_w_out[0].astype(BF16), x2d,
                                 mods[0][:, 2:5], ln1[0], _router_operand(router[0]), n_tok)
    h, u = _ec_moe(h1, u2, logits, exp_w1, exp_w3, exp_w2, 0, mods[0][:, 5:6], ln2[0],
                   mods[1][:, 0:2], batch, n_tok)

    p1 = _mm_call(u, cd_w_in[0].astype(BF16), F32)
    s0 = _sconv_call(p1, cd_conv_w[0], cd_conv_b[0], 0, HY_W, BF16, batch, n_tok)
    s12 = _sconv_call(p1, cd_conv_w[0], cd_conv_b[0], HY_W, HY_ORDER * HY_W, F32, batch, n_tok)
    filt = _hyena_filters(n_tok, cd_filt_w1[0], cd_filt_b1[0], cd_filt_freq1[0], cd_filt_w2[0],
                          cd_filt_b2[0], cd_filt_freq2[0], cd_filt_w3[0])
    hf = filt[:, :, 0, :].reshape(n_tok, HY_ORDER * HY_W)
    hb = jnp.where((jnp.arange(n_tok) == 0)[:, None], 0.0, filt[:, :, 1, :].reshape(n_tok, HY_ORDER * HY_W))
    fc, fs, gc, gs, alt = _hyena_dft_operands(n_tok)
    h_sum = hf + hb
    h_nyq = jnp.sum(alt[:, None] * h_sum, axis=0, keepdims=True)
    h_spec = _dft_fwd_call(fc, fs, h_sum.astype(BF16), (hf - hb).astype(BF16), None, F32, 1, n_tok)
    z = s0
    for o in range(HY_ORDER):
        y_spec = _dft_fwd_call(fc, fs, z, z, (h_spec, h_nyq, o * HY_W), BF16, batch, n_tok)
        z = _dual_call(gc, gs, y_spec, y_spec, (s12, o * HY_W, z, cd_skip[0][o:o + 1]), batch, n_tok, HY_W)
    cl, msl, cw = _fnet_operands(n_tok, FN_GROUP_W)
    g_cos, g_sin = _fnet_ch_call(p1, cw, HY_IN_W)
    y_fn = _dual_call(cl, msl, g_cos, g_sin, None, batch, n_tok, FN_W)
    h1, u2, logits = _outln_call(z, y_fn, cd_w_out[0].astype(BF16), h,
                                 mods[1][:, 2:5], ln1[1], _router_operand(router[1]), n_tok)
    h, _ = _ec_moe(h1, u2, logits, exp_w1, exp_w3, exp_w2, 1, mods[1][:, 5:6], ln2[1],
                   None, batch, n_tok)
    return h.reshape(batch, n_tok, d)
```

```python
import functools
import math

import numpy as np
import jax
import jax.numpy as jnp
from jax import lax
from jax.experimental import pallas as pl
from jax.experimental.pallas import tpu as pltpu

F32 = jnp.float32
BF16 = jnp.bfloat16
I32 = jnp.int32

D_MODEL = 2048
BATCH = 4
SEQ = 2048
DEPTH = 2
CTX_LEN = 256
GRID_W = 64
DEEPNORM_ALPHA = (2.0 * DEPTH) ** 0.25
LN_EPS = 1e-6
NEG_INF = -1e30

MLA_HEADS = 8
MLA_NOPE = 128
MLA_ROPE = 64
MLA_QK = MLA_NOPE + MLA_ROPE
MLA_V = 128
MLA_KV_RANK = 512
ROPE_THETA = 10000.0

NA_HEADS = 8
NA_HEAD_DIM = 128
NA_KH = 8
NA_KW = 16

OFF_CKV = MLA_HEADS * MLA_QK
OFF_KPE = OFF_CKV + MLA_KV_RANK
OFF_QNA = OFF_KPE + MLA_ROPE
OFF_KNA = OFF_QNA + NA_HEADS * NA_HEAD_DIM
OFF_VNA = OFF_KNA + NA_HEADS * NA_HEAD_DIM

HY_W = 1024
HY_ORDER = 2
HY_IN_W = (HY_ORDER + 1) * HY_W
HY_SHORT = 3
HY_BANDS = 16
HY_DECAY_TARGET = 1e-2
HY_MIN_DECAY = math.log(HY_DECAY_TARGET) / 1.5
HY_MAX_DECAY = math.log(HY_DECAY_TARGET) / 0.3

FN_W = 1024
FN_GROUPS = 4
FN_GROUP_W = FN_W // FN_GROUPS

N_EXPERTS = 16
EC_CAPACITY_FACTOR = 2
EXPERT_FF = 1408

LANES = 128
MXU_DIM = 256
VMEM_LIMIT = 56 * 1024 * 1024

QTILE_W = 2 * LANES
T_CKV = MLA_HEADS
T_KPE = T_CKV + MLA_KV_RANK // QTILE_W
T_QNA = T_KPE + 1
T_KNA = T_QNA + NA_HEADS * NA_HEAD_DIM // QTILE_W
T_VNA = T_KNA + NA_HEADS * NA_HEAD_DIM // QTILE_W
N_ABTILES = T_VNA + NA_HEADS * NA_HEAD_DIM // QTILE_W
AB_PERM_W = N_ABTILES * QTILE_W

NA_G = 4
NA_WR = NA_KH + NA_G - 1


def _params(sem, vmem=VMEM_LIMIT):
    return pltpu.CompilerParams(dimension_semantics=sem, vmem_limit_bytes=vmem)


def _dot(a, b):
    return jnp.dot(a, b, preferred_element_type=F32)


def _dot_nt(a, b):
    return lax.dot_general(a, b, (((1,), (1,)), ((), ())), preferred_element_type=F32)


def _dot_tn(a, b):
    return lax.dot_general(a, b, (((0,), (0,)), ((), ())), preferred_element_type=F32)


def _split_bf16(x):
    hi = x.astype(BF16)
    lo = (x - hi.astype(F32)).astype(BF16)
    return hi, lo


def _ada_kernel(c_ref, w_ref, b_ref, o_ref):
    c = c_ref[...]
    a = c / (1.0 + jnp.exp(-c))
    a_hi, a_lo = _split_bf16(a)
    w_hi, w_lo = _split_bf16(w_ref[0])
    rows = a.shape[0]
    r = _dot(jnp.concatenate([a_hi, a_lo], axis=0), w_hi)
    o_ref[0] = r[:rows] + r[rows:] + _dot(a_hi, w_lo) + b_ref[0]


def _ada_call(cv, ada_w, ada_b, tn=1024):
    depth, d, n = ada_w.shape
    rows = cv.shape[0]
    return pl.pallas_call(
        _ada_kernel,
        grid=(depth, n // tn),
        in_specs=[pl.BlockSpec((rows, d), lambda l, j: (0, 0)),
                  pl.BlockSpec((1, d, tn), lambda l, j: (l, 0, j)),
                  pl.BlockSpec((1, 1, tn), lambda l, j: (l, 0, j))],
        out_specs=pl.BlockSpec((1, rows, tn), lambda l, j: (l, 0, j)),
        out_shape=jax.ShapeDtypeStruct((depth, rows, n), F32),
        compiler_params=_params(("arbitrary", "arbitrary")),
        name="ada_params",
    )(cv, ada_w, ada_b.reshape(depth, 1, n))


def _modulate_kernel(x_ref, ss_ref, o_ref):
    o_ref[...] = (x_ref[...] * (1.0 + ss_ref[0, 1:2, :]) + ss_ref[0, 0:1, :]).astype(o_ref.dtype)


def _modulate_call(x2d, ss, rows_per_group, tm=256):
    m, d = x2d.shape
    tpg = rows_per_group // tm
    return pl.pallas_call(
        _modulate_kernel,
        grid=(m // tm,),
        in_specs=[pl.BlockSpec((tm, d), lambda i: (i, 0)),
                  pl.BlockSpec((1, 2, d), lambda i: (i // tpg, 0, 0))],
        out_specs=pl.BlockSpec((tm, d), lambda i: (i, 0)),
        out_shape=jax.ShapeDtypeStruct((m, d), BF16),
        compiler_params=_params(("arbitrary",)),
        name="modulate",
    )(x2d, ss)


def _mm_rope_kernel(a_ref, w_ref, ta_ref, tb_ref, tc_ref, o_ref):
    j = pl.program_id(1)
    acc = _dot(a_ref[...], w_ref[...])
    is_rope = jnp.logical_or(j < T_CKV, j == T_KPE)

    @pl.when(is_rope)
    def _():
        f = jnp.where(j < T_CKV, MLA_QK ** -0.5, 1.0).astype(F32)
        hi = acc[:, LANES:]
        rot = (hi * ta_ref[...] + pltpu.roll(hi, LANES - MLA_ROPE // 2, 1) * tb_ref[...]
               + pltpu.roll(hi, MLA_ROPE // 2, 1) * tc_ref[...])
        o_ref[:, :LANES] = (acc[:, :LANES] * f).astype(o_ref.dtype)
        o_ref[:, LANES:] = (rot * f).astype(o_ref.dtype)

    @pl.when(jnp.logical_not(is_rope))
    def _():
        f = jnp.where(jnp.logical_and(j >= T_QNA, j < T_KNA), NA_HEAD_DIM ** -0.5, 1.0).astype(F32)
        o_ref[...] = (acc * f).astype(o_ref.dtype)


def _mm_rope_call(a, w, tabs, tm):
    m, k = a.shape
    n = w.shape[1]
    ntab = tabs[0].shape[0] // tm
    tab_spec = pl.BlockSpec((tm, LANES), lambda i, j: (i % ntab, 0))
    return pl.pallas_call(
        _mm_rope_kernel,
        grid=(m // tm, n // QTILE_W),
        in_specs=[pl.BlockSpec((tm, k), lambda i, j: (i, 0)),
                  pl.BlockSpec((k, QTILE_W), lambda i, j: (0, j)),
                  tab_spec, tab_spec, tab_spec],
        out_specs=pl.BlockSpec((tm, QTILE_W), lambda i, j: (i, j)),
        out_shape=jax.ShapeDtypeStruct((m, n), BF16),
        compiler_params=_params(("arbitrary", "arbitrary")),
        name="attn_in_proj",
    )(a, w, *tabs)


def _rope_tables(n_tok):
    t = np.arange(n_tok)
    row = (t // GRID_W).astype(np.float32)
    col = (t % GRID_W).astype(np.float32)
    n_freq = MLA_ROPE // 4
    inv = (ROPE_THETA ** (-np.arange(n_freq, dtype=np.float32) / n_freq)).astype(np.float32)
    ang = np.concatenate([row[:, None] * inv, col[:, None] * inv], axis=1)
    cos, sin = np.cos(ang).astype(np.float32), np.sin(ang).astype(np.float32)
    half = MLA_ROPE // 2
    ta = np.zeros((n_tok, LANES), np.float32)
    tb = np.zeros((n_tok, LANES), np.float32)
    tc = np.zeros((n_tok, LANES), np.float32)
    ta[:, :half] = cos
    ta[:, half:2 * half] = cos
    tb[:, :half] = -sin
    tc[:, half:2 * half] = sin
    return jnp.asarray(ta), jnp.asarray(tb), jnp.asarray(tc)


def _identity_rope_tables(n_tok):
    ta = np.zeros((n_tok, LANES), np.float32)
    ta[:, :MLA_ROPE] = 1.0
    z = np.zeros((n_tok, LANES), np.float32)
    return jnp.asarray(ta), jnp.asarray(z), jnp.asarray(z)


def _permute_w_in(w):
    d = w.shape[0]
    zeros = lambda n: jnp.zeros((d, n), w.dtype)
    cols = []
    for h in range(MLA_HEADS):
        cols += [w[:, h * MLA_QK:(h + 1) * MLA_QK], zeros(QTILE_W - MLA_QK)]
    cols += [w[:, OFF_CKV:OFF_KPE], zeros(LANES), w[:, OFF_KPE:OFF_QNA], zeros(LANES - MLA_ROPE),
             w[:, OFF_QNA:]]
    return jnp.concatenate(cols, axis=1).astype(BF16)


def _kvup_kernel(x_ref, g_ref, w_ref, o_ref):
    x = x_ref[...].astype(F32)
    y = x * lax.rsqrt(jnp.mean(x * x, axis=-1, keepdims=True) + LN_EPS) * g_ref[...]
    o_ref[...] = _dot(y.astype(BF16), w_ref[...]).astype(o_ref.dtype)


def _kvup_call(p, g, w, tm):
    m = p.shape[0]
    r, n = w.shape
    return pl.pallas_call(
        _kvup_kernel,
        grid=(m // tm,),
        in_specs=[pl.BlockSpec((tm, r), lambda i: (i, T_CKV * QTILE_W // r)),
                  pl.BlockSpec((1, r), lambda i: (0, 0)),
                  pl.BlockSpec((r, n), lambda i: (0, 0))],
        out_specs=pl.BlockSpec((tm, n), lambda i: (i, 0)),
        out_shape=jax.ShapeDtypeStruct((m, n), BF16),
        compiler_params=_params(("arbitrary",)),
        name="kv_up",
    )(p, g.reshape(1, r), w)


def _permute_w_ukv(w):
    r = w.shape[0]
    w3 = w.reshape(r, MLA_HEADS, MLA_NOPE + MLA_V)
    return jnp.concatenate([w3[:, :, :MLA_NOPE].reshape(r, -1), w3[:, :, MLA_NOPE:].reshape(r, -1)],
                           axis=1).astype(BF16)


def _mla_kernel(q_ref, kn_ref, kp_ref, v_ref, kcn_ref, kcp_ref, vc_ref, o_ref, kcat, vcat):
    s_len = kn_ref.shape[0]

    @pl.when(pl.program_id(2) == 0)
    def _():
        kcat[:s_len, :LANES] = kn_ref[...]
        kcat[:s_len, LANES:] = kp_ref[...]
        kcat[s_len:, :LANES] = kcn_ref[...]
        kcat[s_len:, LANES:] = kcp_ref[...]
        vcat[:s_len, :] = v_ref[...]
        vcat[s_len:, :] = vc_ref[...]

    s = _dot_nt(q_ref[...], kcat[...])
    m = jnp.max(s, axis=-1, keepdims=True)
    p = jnp.exp(s - m)
    l = jnp.sum(p, axis=-1, keepdims=True)
    o = _dot(p.astype(BF16), vcat[...])
    o_ref[...] = (o / l).astype(o_ref.dtype)


def _mla_call(p_lat, kv_lat, p_ctx, kv_ctx, batch, s_len, lc, tq=512):
    nq = s_len // tq
    kpe_blk = T_KPE * 2 + 1
    return pl.pallas_call(
        _mla_kernel,
        grid=(batch, MLA_HEADS, nq),
        in_specs=[pl.BlockSpec((tq, QTILE_W), lambda b, h, i: (b * nq + i, h)),
                  pl.BlockSpec((s_len, LANES), lambda b, h, i: (b, h)),
                  pl.BlockSpec((s_len, LANES), lambda b, h, i: (b, kpe_blk)),
                  pl.BlockSpec((s_len, LANES), lambda b, h, i: (b, MLA_HEADS + h)),
                  pl.BlockSpec((lc, LANES), lambda b, h, i: (b, h)),
                  pl.BlockSpec((lc, LANES), lambda b, h, i: (b, kpe_blk)),
                  pl.BlockSpec((lc, LANES), lambda b, h, i: (b, MLA_HEADS + h))],
        out_specs=pl.BlockSpec((tq, LANES), lambda b, h, i: (b * nq + i, h)),
        out_shape=jax.ShapeDtypeStruct((batch * s_len, MLA_HEADS * MLA_V), BF16),
        scratch_shapes=[pltpu.VMEM((s_len + lc, QTILE_W), BF16),
                        pltpu.VMEM((s_len + lc, LANES), BF16)],
        compiler_params=_params(("arbitrary", "arbitrary", "arbitrary")),
        name="mla_attention",
    )(p_lat, kv_lat, p_lat, kv_lat, kv_ctx, p_ctx, kv_ctx)


def _na_kernel(var_ref, q_ref, k_ref, v_ref, kc_ref, vc_ref, bias_ref, o_ref, *, rows):
    del var_ref
    g = pl.program_id(2)
    start_row = jnp.clip(NA_G * g - NA_KH // 2, 0, rows - NA_WR)
    start = pl.multiple_of(start_row * GRID_W, GRID_W)
    win = NA_WR * GRID_W
    q = q_ref[...]
    sw = _dot_nt(q, k_ref[pl.ds(start, win), :]) + bias_ref[0, 0]
    sc = _dot_nt(q, kc_ref[...])
    m = jnp.maximum(jnp.max(sw, axis=-1, keepdims=True), jnp.max(sc, axis=-1, keepdims=True))
    pw = jnp.exp(sw - m)
    pc = jnp.exp(sc - m)
    l = jnp.sum(pw, axis=-1, keepdims=True) + jnp.sum(pc, axis=-1, keepdims=True)
    o = _dot(pw.astype(BF16), v_ref[pl.ds(start, win), :]) + _dot(pc.astype(BF16), vc_ref[...])
    o_ref[...] = (o / l).astype(o_ref.dtype)


def _na_tables(rows):
    kh = min(NA_KH, rows)
    ng = rows // NA_G
    qr = np.arange(NA_G)[:, None]
    kr = np.arange(NA_WR)[None, :]
    sel_rows = []
    for g in range(ng):
        start_row = int(np.clip(NA_G * g - NA_KH // 2, 0, rows - NA_WR))
        r = NA_G * g + qr
        krow = start_row + kr
        rs = np.clip(r - kh // 2, 0, rows - kh)
        ok = (krow >= rs) & (krow < rs + kh)
        dr = krow - r + (NA_KH - 1)
        sel_rows.append(np.stack([ok & (dr == d) for d in range(2 * NA_KH - 1)]))
    sel_rows = np.stack(sel_rows)
    uniq, inverse = np.unique(sel_rows.reshape(ng, -1), axis=0, return_inverse=True)
    er = uniq.reshape((-1,) + sel_rows.shape[1:]).astype(np.float32)
    qc = np.arange(GRID_W)[:, None]
    kc = np.arange(GRID_W)[None, :]
    cs = np.clip(qc - NA_KW // 2, 0, GRID_W - NA_KW)
    ok_c = (kc >= cs) & (kc < cs + NA_KW)
    dc = np.clip(kc - qc, -(NA_KW - 1), NA_KW - 1) + (NA_KW - 1)
    ec = np.stack([ok_c & (dc == e) for e in range(2 * NA_KW - 1)]).astype(np.float32)
    return inverse.reshape(-1).astype(np.int32), er, ec


def _na_bias(rpb, er, ec):
    hp = lax.Precision.HIGHEST
    t = jnp.einsum('hde,eqk->hdqk', rpb.astype(F32), jnp.asarray(ec), precision=hp)
    bias = jnp.einsum('vdab,hdqk->vhaqbk', jnp.asarray(er), t, precision=hp)
    ok = np.einsum('vab,qk->vaqbk', er.sum(axis=1), ec.sum(axis=0)) > 0.5
    bias = jnp.where(jnp.asarray(ok)[:, None], bias, NEG_INF)
    v, h = bias.shape[:2]
    return bias.reshape(v, h, NA_G * GRID_W, NA_WR * GRID_W)


def _na_call(p_lat, p_ctx, bias, var_map, batch, s_len, lc):
    rows = s_len // GRID_W
    ng = rows // NA_G
    tq = NA_G * GRID_W
    win = NA_WR * GRID_W
    qb, kb, vb = T_QNA * 2, T_KNA * 2, T_VNA * 2
    grid_spec = pltpu.PrefetchScalarGridSpec(
        num_scalar_prefetch=1,
        grid=(batch, NA_HEADS, ng),
        in_specs=[pl.BlockSpec((tq, LANES), lambda b, h, g, vm: (b * ng + g, qb + h)),
                  pl.BlockSpec((s_len, LANES), lambda b, h, g, vm: (b, kb + h)),
                  pl.BlockSpec((s_len, LANES), lambda b, h, g, vm: (b, vb + h)),
                  pl.BlockSpec((lc, LANES), lambda b, h, g, vm: (b, kb + h)),
                  pl.BlockSpec((lc, LANES), lambda b, h, g, vm: (b, vb + h)),
                  pl.BlockSpec((1, 1, tq, win), lambda b, h, g, vm: (vm[g], h, 0, 0))],
        out_specs=pl.BlockSpec((tq, LANES), lambda b, h, g, vm: (b * ng + g, h)),
    )
    return pl.pallas_call(
        functools.partial(_na_kernel, rows=rows),
        grid_spec=grid_spec,
        out_shape=jax.ShapeDtypeStruct((batch * s_len, NA_HEADS * NA_HEAD_DIM), BF16),
        compiler_params=_params(("arbitrary", "arbitrary", "arbitrary")),
        name="na_attention",
    )(var_map, p_lat, p_lat, p_lat, p_ctx, p_ctx, bias)


def _layer_norm(x, g, b):
    mu = jnp.mean(x, axis=-1, keepdims=True)
    xc = x - mu
    var = jnp.mean(xc * xc, axis=-1, keepdims=True)
    return xc * lax.rsqrt(var + LN_EPS) * g + b


def _outln_kernel(a1_ref, a2_ref, w_ref, h_ref, mod_ref, ln_ref, r_ref, h1_ref, u2_ref, lg_ref):
    half = a1_ref.shape[1]
    y = _dot(a1_ref[...], w_ref[:half, :]) + _dot(a2_ref[...], w_ref[half:, :])
    x = DEEPNORM_ALPHA * h_ref[...] + mod_ref[0, 0:1, :] * y
    hn = _layer_norm(x, ln_ref[0:1, :], ln_ref[1:2, :])
    h1_ref[...] = hn
    u = hn * (1.0 + mod_ref[0, 2:3, :]) + mod_ref[0, 1:2, :]
    u_hi, u_lo = _split_bf16(u)
    u2_ref[...] = u_hi
    lg_ref[...] = _dot(u_hi, r_ref[0]) + _dot(u_lo, r_ref[0]) + _dot(u_hi, r_ref[1])


def _outln_call(a1, a2, w, h, mod, ln, router2, rows_per_group, tm=256):
    m, d = h.shape
    half = a1.shape[1]
    tpg = rows_per_group // tm
    return pl.pallas_call(
        _outln_kernel,
        grid=(m // tm,),
        in_specs=[pl.BlockSpec((tm, half), lambda i: (i, 0)),
                  pl.BlockSpec((tm, half), lambda i: (i, 0)),
                  pl.BlockSpec((2 * half, d), lambda i: (0, 0)),
                  pl.BlockSpec((tm, d), lambda i: (i, 0)),
                  pl.BlockSpec((1, 3, d), lambda i: (i // tpg, 0, 0)),
                  pl.BlockSpec((2, d), lambda i: (0, 0)),
                  pl.BlockSpec((2, d, LANES), lambda i: (0, 0, 0))],
        out_specs=[pl.BlockSpec((tm, d), lambda i: (i, 0)),
                   pl.BlockSpec((tm, d), lambda i: (i, 0)),
                   pl.BlockSpec((tm, LANES), lambda i: (i, 0))],
        out_shape=[jax.ShapeDtypeStruct((m, d), F32),
                   jax.ShapeDtypeStruct((m, d), BF16),
                   jax.ShapeDtypeStruct((m, LANES), F32)],
        compiler_params=_params(("arbitrary",)),
        name="out_proj_postnorm",
    )(a1, a2, w, h, mod, ln, router2)


def _router_operand(router):
    d, e = router.shape
    rp = jnp.pad(router.astype(F32), ((0, 0), (0, LANES - e)))
    hi = rp.astype(BF16)
    lo = (rp - hi.astype(F32)).astype(BF16)
    return jnp.stack([hi, lo])


def _route_kernel(lg_ref, slot_ref, aff_ref, tri_ref, *, cap):
    n_tok = lg_ref.shape[2]

    @pl.when(pl.program_id(0) == 0)
    def _():
        chunk = 256
        for r0 in range(0, n_tok, chunk):
            r = r0 + lax.broadcasted_iota(I32, (chunk, n_tok), 0)
            c = lax.broadcasted_iota(I32, (chunk, n_tok), 1)
            tri_ref[r0:r0 + chunk, :] = jnp.where(r < c, 1.0, 0.0).astype(BF16)

    lg = lg_ref[0]
    ex = jnp.exp(lg - jnp.max(lg, axis=0, keepdims=True))
    aff = ex / jnp.sum(ex, axis=0, keepdims=True)
    bits = lax.bitcast_convert_type(aff, I32)
    n_e = lg.shape[0]
    count = lambda mask: jnp.sum(jnp.where(mask, 1.0, 0.0), axis=1, keepdims=True)

    def body(_, lohi):
        lo, hi = lohi
        mid = lo + jnp.right_shift(hi - lo, 1)
        ge = count(bits >= mid) >= cap
        return jnp.where(ge, mid, lo), jnp.where(ge, hi, mid)

    lo0 = jnp.zeros((n_e, 1), I32)
    hi0 = jnp.full((n_e, 1), 0x7F800000, I32)
    thr, _ = lax.fori_loop(0, 31, body, (lo0, hi0))
    gt = bits > thr
    eq = bits == thr
    need = cap - count(gt)
    pre_eq = _dot(jnp.where(eq, 1.0, 0.0).astype(BF16), tri_ref[...])
    sel = jnp.logical_or(gt, jnp.logical_and(eq, pre_eq < need))
    slot = _dot(jnp.where(sel, 1.0, 0.0).astype(BF16), tri_ref[...])
    slot_ref[0] = jnp.where(sel, slot.astype(I32), -1)
    aff_ref[0] = aff


def _route_call(lg_t, cap):
    b, e, n_tok = lg_t.shape
    spec = pl.BlockSpec((1, e, n_tok), lambda i: (i, 0, 0))
    return pl.pallas_call(
        functools.partial(_route_kernel, cap=cap),
        grid=(b,),
        in_specs=[spec],
        out_specs=[spec, spec],
        out_shape=[jax.ShapeDtypeStruct((b, e, n_tok), I32), jax.ShapeDtypeStruct((b, e, n_tok), F32)],
        scratch_shapes=[pltpu.VMEM((n_tok, n_tok), BF16)],
        compiler_params=_params(("arbitrary",)),
        name="ec_route",
    )(lg_t)


def _gather_kernel(slot_ref, u_ref, o_ref):
    cap = o_ref.shape[1]
    n_tok = u_ref.shape[0]
    j = lax.broadcasted_iota(I32, (cap, n_tok), 0)
    onehot = jnp.where(j == slot_ref[0, 0], 1.0, 0.0).astype(BF16)
    o_ref[0] = _dot(onehot, u_ref[...]).astype(o_ref.dtype)


def _gather_call(slot4, u2, cap):
    b, e, _, n_tok = slot4.shape
    d = u2.shape[1]
    return pl.pallas_call(
        _gather_kernel,
        grid=(b, e),
        in_specs=[pl.BlockSpec((1, 1, 1, n_tok), lambda i, j: (i, j, 0, 0)),
                  pl.BlockSpec((n_tok, d), lambda i, j: (i, 0))],
        out_specs=pl.BlockSpec((1, cap, d), lambda i, j: (j, i, 0)),
        out_shape=jax.ShapeDtypeStruct((e, b * cap, d), BF16),
        compiler_params=_params(("arbitrary", "arbitrary")),
        name="moe_gather",
    )(slot4, u2)


def _expert_kernel(x_ref, w1a, w1b, w3a, w3b, w2a, w2b, o_ref, hd_ref, w2_ref, *, nf, ff):
    f = pl.program_id(1)
    pair = 2 * LANES
    wcat = jnp.concatenate([w1a[0, 0].astype(BF16), w1b[0, 0].astype(BF16),
                            w3a[0, 0].astype(BF16), w3b[0, 0].astype(BF16)], axis=1)
    h = _dot(x_ref[0], wcat)
    h1 = h[:, :pair]
    hd = (h1 / (1.0 + jnp.exp(-h1))) * h[:, pair:]
    w2cat = jnp.concatenate([w2a[0, 0].astype(BF16), w2b[0, 0].astype(BF16)], axis=0)
    for s in range(nf):
        @pl.when(f == s)
        def _(s=s):
            hd_ref[:, s * pair:(s + 1) * pair] = hd.astype(BF16)
            w2_ref[s * pair:(s + 1) * pair, :] = w2cat

    @pl.when(f == nf - 1)
    def _():
        nb, cap = o_ref.shape[0], o_ref.shape[1]
        rows = 2 * cap
        for r0 in range(0, nb * cap, rows):
            y = _dot(hd_ref[r0:r0 + rows, :ff], w2_ref[:ff, :]).astype(o_ref.dtype)
            for b in range(rows // cap):
                o_ref[r0 // cap + b] = y[b * cap:(b + 1) * cap]


def _expert_call(xe, w1, w3, w2, layer, batch):
    e, m, d = xe.shape
    cap = m // batch
    ff = w1.shape[3]
    nt = ff // LANES
    nf = (nt + 1) // 2
    ta = lambda f: 2 * f
    tb = lambda f: jnp.minimum(2 * f + 1, nt - 1)
    col = lambda t: pl.BlockSpec((1, 1, d, LANES), lambda i, f: (layer, i, 0, t(f)))
    row = lambda t: pl.BlockSpec((1, 1, LANES, d), lambda i, f: (layer, i, t(f), 0))
    return pl.pallas_call(
        functools.partial(_expert_kernel, nf=nf, ff=ff),
        grid=(e, nf),
        in_specs=[pl.BlockSpec((1, m, d), lambda i, f: (i, 0, 0)),
                  col(ta), col(tb), col(ta), col(tb), row(ta), row(tb)],
        out_specs=pl.BlockSpec((batch, cap, d), lambda i, f: (0, i, 0)),
        out_shape=jax.ShapeDtypeStruct((batch, e * cap, d), BF16),
        scratch_shapes=[pltpu.VMEM((m, nf * 2 * LANES), BF16),
                        pltpu.VMEM((nf * 2 * LANES, d), BF16)],
        compiler_params=_params(("arbitrary", "arbitrary")),
        name="moe_experts",
    )(xe, w1, w1, w3, w3, w2, w2)


def _scatter_kernel(slot_ref, aff_ref, ye_ref, h_ref, mod_ref, ln_ref, *rest, with_next):
    if with_next:
        nmod_ref, h2_ref, un_ref, pt_ref = rest
    else:
        h2_ref, pt_ref = rest
    n_e = slot_ref.shape[1]
    cap = ye_ref.shape[1] // n_e
    tt = h_ref.shape[0]
    j = lax.broadcasted_iota(I32, (cap, tt), 0)
    for e in range(n_e):
        pt_ref[e * cap:(e + 1) * cap, :] = jnp.where(j == slot_ref[0, e:e + 1, :],
                                                     aff_ref[0, e:e + 1, :], 0.0).astype(BF16)
    moe = _dot_tn(pt_ref[...], ye_ref[0])
    x = DEEPNORM_ALPHA * h_ref[...] + mod_ref[0] * moe
    hn = _layer_norm(x, ln_ref[0:1, :], ln_ref[1:2, :])
    h2_ref[...] = hn
    if with_next:
        un_ref[...] = (hn * (1.0 + nmod_ref[0, 1:2, :]) + nmod_ref[0, 0:1, :]).astype(un_ref.dtype)


def _scatter_call(slot, aff, ye, h1, gate, ln, next_ss, tt=512):
    b, e, n_tok = slot.shape
    m, d = h1.shape
    nt = n_tok // tt
    rows = ye.shape[1]
    with_next = next_ss is not None
    tok_spec = pl.BlockSpec((1, e, tt), lambda i, t: (i, 0, t))
    row_spec = pl.BlockSpec((tt, d), lambda i, t: (i * nt + t, 0))
    in_specs = [tok_spec, tok_spec,
                pl.BlockSpec((1, rows, d), lambda i, t: (i, 0, 0), pipeline_mode=pl.Buffered(1)),
                row_spec,
                pl.BlockSpec((1, 1, d), lambda i, t: (i, 0, 0)),
                pl.BlockSpec((2, d), lambda i, t: (0, 0))]
    args = [slot, aff, ye, h1, gate, ln]
    out_specs = [row_spec]
    out_shape = [jax.ShapeDtypeStruct((m, d), F32)]
    if with_next:
        in_specs.append(pl.BlockSpec((1, 2, d), lambda i, t: (i, 0, 0)))
        args.append(next_ss)
        out_specs.append(row_spec)
        out_shape.append(jax.ShapeDtypeStruct((m, d), BF16))
    res = pl.pallas_call(
        functools.partial(_scatter_kernel, with_next=with_next),
        grid=(b, nt),
        in_specs=in_specs,
        out_specs=out_specs,
        out_shape=out_shape,
        scratch_shapes=[pltpu.VMEM((rows, tt), BF16)],
        compiler_params=_params(("arbitrary", "arbitrary")),
        name="moe_combine_postnorm",
    )(*args)
    return res if with_next else (res[0], None)


def _ec_moe(h1, u2, logits, w1, w3, w2, layer, gate, ln, next_ss, batch, n_tok):
    cap = EC_CAPACITY_FACTOR * n_tok // N_EXPERTS
    lg_t = jnp.swapaxes(logits[:, :N_EXPERTS].reshape(batch, n_tok, N_EXPERTS), 1, 2)
    slot, aff = _route_call(lg_t, cap)
    xe = _gather_call(slot.reshape(batch, N_EXPERTS, 1, n_tok), u2, cap)
    ye = _expert_call(xe, w1, w3, w2, layer, batch)
    return _scatter_call(slot, aff, ye, h1, gate, ln, next_ss)


def _mm_kernel(a_ref, w_ref, o_ref):
    o_ref[...] = _dot(a_ref[...], w_ref[...]).astype(o_ref.dtype)


def _mm_call(a, w, out_dtype, tm=1024, tn=512):
    m, k = a.shape
    n = w.shape[1]
    return pl.pallas_call(
        _mm_kernel,
        grid=(m // tm, n // tn),
        in_specs=[pl.BlockSpec((tm, k), lambda i, j: (i, 0)),
                  pl.BlockSpec((k, tn), lambda i, j: (0, j))],
        out_specs=pl.BlockSpec((tm, tn), lambda i, j: (i, j)),
        out_shape=jax.ShapeDtypeStruct((m, n), out_dtype),
        compiler_params=_params(("arbitrary", "arbitrary")),
        name="matmul",
    )(a, w)


def _sconv_kernel(p_ref, w_ref, b_ref, o_ref):
    x = p_ref[...]
    n = x.shape[0]
    r = lax.broadcasted_iota(I32, x.shape, 0)
    prev = jnp.where(r == 0, 0.0, pltpu.roll(x, 1, 0))
    nxt = jnp.where(r == n - 1, 0.0, pltpu.roll(x, n - 1, 0))
    o_ref[...] = (b_ref[...] + w_ref[0:1, :] * prev + w_ref[1:2, :] * x + w_ref[2:3, :] * nxt
                  ).astype(o_ref.dtype)


def _sconv_call(p, conv_w, conv_b, col0, ncols, out_dtype, batch, n_tok, tc=512):
    c0 = col0 // tc
    return pl.pallas_call(
        _sconv_kernel,
        grid=(batch, ncols // tc),
        in_specs=[pl.BlockSpec((n_tok, tc), lambda b, j: (b, c0 + j)),
                  pl.BlockSpec((HY_SHORT, tc), lambda b, j: (0, c0 + j)),
                  pl.BlockSpec((1, tc), lambda b, j: (0, c0 + j))],
        out_specs=pl.BlockSpec((n_tok, tc), lambda b, j: (b, j)),
        out_shape=jax.ShapeDtypeStruct((batch * n_tok, ncols), out_dtype),
        compiler_params=_params(("arbitrary", "arbitrary")),
        name="hyena_short_conv",
    )(p, conv_w, conv_b.reshape(1, -1))


def _dft_fwd_kernel(fc_ref, fs_ref, r1_ref, r2_ref, *rest, spectral):
    xr = _dot(fc_ref[...], r1_ref[...])
    xi = _dot(fs_ref[...], r2_ref[...])
    if spectral:
        h_ref, nyq_ref, o_ref = rest
        hr = h_ref[0, 0]
        hi = h_ref[0, 1]
        tm = xr.shape[0]
        k = pl.program_id(2) * tm + lax.broadcasted_iota(I32, xr.shape, 0)
        dc = k == 0
        yr = xr * hr - jnp.where(dc, 0.0, xi * hi)
        yi = jnp.where(dc, xi * nyq_ref[...], xr * hi + xi * hr)
    else:
        (o_ref,) = rest
        yr, yi = xr, xi
    o_ref[0, 0] = yr.astype(o_ref.dtype)
    o_ref[0, 1] = yi.astype(o_ref.dtype)


def _dft_fwd_call(fc, fs, r1, r2, spec, out_dtype, batch, n_tok, tm=512, tn=512):
    ncols = r1.shape[1]
    f_spec = pl.BlockSpec((tm, n_tok), lambda b, j, i: (i, 0))
    r_spec = pl.BlockSpec((n_tok, tn), lambda b, j, i: (b, j))
    in_specs = [f_spec, f_spec, r_spec, r_spec]
    args = [fc, fs, r1, r2]
    if spec is not None:
        h, nyq, hcol0 = spec
        c0 = hcol0 // tn
        in_specs += [pl.BlockSpec((1, 2, tm, tn), lambda b, j, i: (0, 0, i, c0 + j)),
                     pl.BlockSpec((1, tn), lambda b, j, i: (0, c0 + j))]
        args += [h, nyq]
    return pl.pallas_call(
        functools.partial(_dft_fwd_kernel, spectral=spec is not None),
        grid=(batch, ncols // tn, n_tok // tm),
        in_specs=in_specs,
        out_specs=pl.BlockSpec((1, 2, tm, tn), lambda b, j, i: (b, 0, i, j)),
        out_shape=jax.ShapeDtypeStruct((batch, 2, n_tok, ncols), out_dtype),
        compiler_params=_params(("arbitrary", "arbitrary", "arbitrary")),
        name="dft_forward",
    )(*args)


def _dual_kernel(a1_ref, a2_ref, b1_ref, b2_ref, *rest, hyena, planes):
    b1 = b1_ref[0, 0] if planes else b1_ref[...]
    b2 = b2_ref[0, 0] if planes else b2_ref[...]
    y = _dot(a1_ref[...], b1) + _dot(a2_ref[...], b2)
    if hyena:
        s_ref, z_ref, skip_ref, o_ref = rest
        y = s_ref[...].astype(F32) * (y + skip_ref[...] * z_ref[...].astype(F32))
    else:
        (o_ref,) = rest
    o_ref[...] = y.astype(o_ref.dtype)


def _dual_call(a1, a2, b1, b2, hy, batch, n_tok, ncols, tm=512, tn=512):
    nm = n_tok // tm
    a_spec = pl.BlockSpec((tm, n_tok), lambda b, j, i: (i, 0))
    planes = b1.ndim == 4
    if planes:
        b_specs = [pl.BlockSpec((1, 1, n_tok, tn), lambda b, j, i: (b, 0, 0, j)),
                   pl.BlockSpec((1, 1, n_tok, tn), lambda b, j, i: (b, 1, 0, j))]
    else:
        b_specs = [pl.BlockSpec((n_tok, tn), lambda b, j, i: (b, j))] * 2
    in_specs = [a_spec, a_spec] + b_specs
    args = [a1, a2, b1, b2]
    o_spec = pl.BlockSpec((tm, tn), lambda b, j, i: (b * nm + i, j))
    if hy is not None:
        s, scol0, z, skip = hy
        c0 = scol0 // tn
        in_specs += [pl.BlockSpec((tm, tn), lambda b, j, i: (b * nm + i, c0 + j)),
                     o_spec,
                     pl.BlockSpec((1, tn), lambda b, j, i: (0, j))]
        args += [s, z, skip]
    return pl.pallas_call(
        functools.partial(_dual_kernel, hyena=hy is not None, planes=planes),
        grid=(batch, ncols // tn, nm),
        in_specs=in_specs,
        out_specs=o_spec,
        out_shape=jax.ShapeDtypeStruct((batch * n_tok, ncols), BF16),
        compiler_params=_params(("arbitrary", "arbitrary", "arbitrary")),
        name="seq_mix_matmul",
    )(*args)


def _fnet_ch_kernel(p_ref, w_ref, gc_ref, gs_ref):
    r = _dot(p_ref[...].astype(BF16), w_ref[...])
    w = gc_ref.shape[1]
    gc_ref[...] = r[:, :w].astype(gc_ref.dtype)
    gs_ref[...] = r[:, w:].astype(gs_ref.dtype)


def _fnet_ch_call(p, w, col0, tm=1024):
    m = p.shape[0]
    gw = w.shape[0]
    c0 = col0 // gw
    o_spec = pl.BlockSpec((tm, gw), lambda i, g: (i, g))
    return pl.pallas_call(
        _fnet_ch_kernel,
        grid=(m // tm, FN_GROUPS),
        in_specs=[pl.BlockSpec((tm, gw), lambda i, g: (i, c0 + g)),
                  pl.BlockSpec((gw, 2 * gw), lambda i, g: (0, 0))],
        out_specs=[o_spec, o_spec],
        out_shape=[jax.ShapeDtypeStruct((m, FN_GROUPS * gw), BF16)] * 2,
        compiler_params=_params(("arbitrary", "arbitrary")),
        name="fnet_channel_dft",
    )(p, w)


def _cos_sin_matrix(n_rows, n_cols, period, split=64):
    r = np.arange(n_rows, dtype=np.int64)[:, None]
    c_hi = (np.arange(n_cols // split, dtype=np.int64) * split)[None, :]
    c_lo = np.arange(split, dtype=np.int64)[None, :]
    ang = lambda c: 2.0 * np.pi * ((r * c) % period).astype(np.float64) / period
    ca, sa = jnp.asarray(np.cos(ang(c_hi)), F32), jnp.asarray(np.sin(ang(c_hi)), F32)
    cb, sb = jnp.asarray(np.cos(ang(c_lo)), F32), jnp.asarray(np.sin(ang(c_lo)), F32)
    cos = ca[:, :, None] * cb[:, None, :] - sa[:, :, None] * sb[:, None, :]
    sin = sa[:, :, None] * cb[:, None, :] + ca[:, :, None] * sb[:, None, :]
    return cos.reshape(n_rows, n_cols), sin.reshape(n_rows, n_cols)


def _hyena_dft_operands(n_tok):
    n = 2 * n_tok
    cos, sin = _cos_sin_matrix(n_tok, n_tok, n)
    alt = jnp.asarray(((-1.0) ** np.arange(n_tok)), F32)
    row0 = (jnp.arange(n_tok) == 0)
    fc = cos
    fs = jnp.where(row0[:, None], alt[None, :], -sin)
    ck = jnp.where(row0, 1.0, 2.0)[None, :] / n
    gc = cos * ck
    gs = jnp.where(row0[None, :], alt[:, None] / n, -sin * (2.0 / n))
    return fc.astype(BF16), fs.astype(BF16), gc.astype(BF16), gs.astype(BF16), alt


def _fnet_operands(n_tok, gw):
    scale = 1.0 / math.sqrt(n_tok * gw)
    cl, sl = _cos_sin_matrix(n_tok, n_tok, n_tok)
    cw, sw = _cos_sin_matrix(gw, gw, gw)
    return ((cl * scale).astype(BF16), (-sl * scale).astype(BF16),
            jnp.concatenate([cw, sw], axis=1).astype(BF16))


def _hyena_filters(n_tok, fw1, fb1, ff1, fw2, fb2, ff2, fw3):
    t01 = jnp.linspace(0.0, 1.0, n_tok, dtype=F32)
    w = 2.0 * math.pi * jnp.arange(n_tok, dtype=F32) / n_tok
    bands = jnp.linspace(1e-4, HY_BANDS - 1, HY_BANDS, dtype=F32)
    z = jnp.concatenate([t01[:, None], jnp.cos(w[:, None] * bands), -jnp.sin(w[:, None] * bands)], -1)
    hp = lax.Precision.HIGHEST
    hdn = jnp.sin(ff1 * (jnp.dot(z, fw1, precision=hp) + fb1))
    hdn = jnp.sin(ff2 * (jnp.dot(hdn, fw2, precision=hp) + fb2))
    filt = jnp.dot(hdn, fw3, precision=hp)
    deltas = jnp.abs(jnp.linspace(HY_MIN_DECAY, HY_MAX_DECAY, HY_W, dtype=F32))
    decay = jnp.exp(-t01[:, None] * deltas)
    return filt.reshape(n_tok, HY_ORDER, 2, HY_W) * decay[:, None, None, :]


def kernel(x, c, ctx, c_ctx, ada_w, ada_b, ln1_g, ln1_b, ln2_g, ln2_b, router, exp_w1, exp_w3, exp_w2,
           ab_w_in, ab_kv_norm, ab_w_ukv, ab_rpb, ab_w_out,
           cd_w_in, cd_conv_w, cd_conv_b, cd_filt_w1, cd_filt_b1, cd_filt_freq1, cd_filt_w2, cd_filt_b2,
           cd_filt_freq2, cd_filt_w3, cd_skip, cd_w_out):
    batch, n_tok, d = x.shape
    lc = ctx.shape[1]
    x2d = x.reshape(batch * n_tok, d)
    ctx2d = ctx.reshape(batch * lc, d)

    pad_rows = (-(batch + 1)) % 8
    cv = jnp.concatenate([c, c_ctx[None, :], jnp.zeros((pad_rows, d), F32)], axis=0)
    ada = _ada_call(cv, ada_w, ada_b)
    mods = [ada[i, :batch].reshape(batch, 6, d) for i in range(DEPTH)]
    ctx_ss = ada[0, batch, :2 * d].reshape(1, 2, d)

    u_lat = _modulate_call(x2d, mods[0][:, 0:2], n_tok)
    u_ctx = _modulate_call(ctx2d, ctx_ss, batch * lc)
    w_in = _permute_w_in(ab_w_in[0])
    p_lat = _mm_rope_call(u_lat, w_in, _rope_tables(n_tok), tm=1024)
    p_ctx = _mm_rope_call(u_ctx, w_in, _identity_rope_tables(lc), tm=lc)
    w_ukv = _permute_w_ukv(ab_w_ukv[0])
    kv_lat = _kvup_call(p_lat, ab_kv_norm[0], w_ukv, tm=512)
    kv_ctx = _kvup_call(p_ctx, ab_kv_norm[0], w_ukv, tm=lc)
    a_out = _mla_call(p_lat, kv_lat, p_ctx, kv_ctx, batch, n_tok, lc)
    var_map, na_er, na_ec = _na_tables(n_tok // GRID_W)
    b_out = _na_call(p_lat, p_ctx, _na_bias(ab_rpb[0], na_er, na_ec), jnp.asarray(var_map),
                     batch, n_tok, lc)
    ln1 = jnp.stack([ln1_g, ln1_b], axis=1)
    ln2 = jnp.stack([ln2_g, ln2_b], axis=1)
    h1, u2, logits = _outln_call(a_out, b_out, ab_w_out[0].astype(BF16), x2d,
                                 mods[0][:, 2:5], ln1[0], _router_operand(router[0]), n_tok)
    h, u = _ec_moe(h1, u2, logits, exp_w1, exp_w3, exp_w2, 0, mods[0][:, 5:6], ln2[0],
                   mods[1][:, 0:2], batch, n_tok)

    p1 = _mm_call(u, cd_w_in[0].astype(BF16), F32)
    s0 = _sconv_call(p1, cd_conv_w[0], cd_conv_b[0], 0, HY_W, BF16, batch, n_tok)
    s12 = _sconv_call(p1, cd_conv_w[0], cd_conv_b[0], HY_W, HY_ORDER * HY_W, F32, batch, n_tok)
    filt = _hyena_filters(n_tok, cd_filt_w1[0], cd_filt_b1[0], cd_filt_freq1[0], cd_filt_w2[0],
                          cd_filt_b2[0], cd_filt_freq2[0], cd_filt_w3[0])
    hf = filt[:, :, 0, :].reshape(n_tok, HY_ORDER * HY_W)
    hb = jnp.where((jnp.arange(n_tok) == 0)[:, None], 0.0, filt[:, :, 1, :].reshape(n_tok, HY_ORDER * HY_W))
    fc, fs, gc, gs, alt = _hyena_dft_operands(n_tok)
    h_sum = hf + hb
    h_nyq = jnp.sum(alt[:, None] * h_sum, axis=0, keepdims=True)
    h_spec = _dft_fwd_call(fc, fs, h_sum.astype(BF16), (hf - hb).astype(BF16), None, F32, 1, n_tok)
    z = s0
    for o in range(HY_ORDER):
        y_spec = _dft_fwd_call(fc, fs, z, z, (h_spec, h_nyq, o * HY_W), BF16, batch, n_tok)
        z = _dual_call(gc, gs, y_spec, y_spec, (s12, o * HY_W, z, cd_skip[0][o:o + 1]), batch, n_tok, HY_W)
    cl, msl, cw = _fnet_operands(n_tok, FN_GROUP_W)
    g_cos, g_sin = _fnet_ch_call(p1, cw, HY_IN_W)
    y_fn = _dual_call(cl, msl, g_cos, g_sin, None, batch, n_tok, FN_W)
    h1, u2, logits = _outln_call(z, y_fn, cd_w_out[0].astype(BF16), h,
                                 mods[1][:, 2:5], ln1[1], _router_operand(router[1]), n_tok)
    h, _ = _ec_moe(h1, u2, logits, exp_w1, exp_w3, exp_w2, 1, mods[1][:, 5:6], ln2[1],
                   None, batch, n_tok)
    return h.reshape(batch, n_tok, d)
```

```python
import functools
import math

import numpy as np
import jax
import jax.numpy as jnp
from jax import lax
from jax.experimental import pallas as pl
from jax.experimental.pallas import tpu as pltpu

F32 = jnp.float32
BF16 = jnp.bfloat16
I32 = jnp.int32

D_MODEL = 2048
BATCH = 4
SEQ = 2048
DEPTH = 2
CTX_LEN = 256
GRID_W = 64
DEEPNORM_ALPHA = (2.0 * DEPTH) ** 0.25
LN_EPS = 1e-6
NEG_INF = -1e30

MLA_HEADS = 8
MLA_NOPE = 128
MLA_ROPE = 64
MLA_QK = MLA_NOPE + MLA_ROPE
MLA_V = 128
MLA_KV_RANK = 512
ROPE_THETA = 10000.0

NA_HEADS = 8
NA_HEAD_DIM = 128
NA_KH = 8
NA_KW = 16

OFF_CKV = MLA_HEADS * MLA_QK
OFF_KPE = OFF_CKV + MLA_KV_RANK
OFF_QNA = OFF_KPE + MLA_ROPE
OFF_KNA = OFF_QNA + NA_HEADS * NA_HEAD_DIM
OFF_VNA = OFF_KNA + NA_HEADS * NA_HEAD_DIM

HY_W = 1024
HY_ORDER = 2
HY_IN_W = (HY_ORDER + 1) * HY_W
HY_SHORT = 3
HY_BANDS = 16
HY_DECAY_TARGET = 1e-2
HY_MIN_DECAY = math.log(HY_DECAY_TARGET) / 1.5
HY_MAX_DECAY = math.log(HY_DECAY_TARGET) / 0.3

FN_W = 1024
FN_GROUPS = 4
FN_GROUP_W = FN_W // FN_GROUPS

N_EXPERTS = 16
EC_CAPACITY_FACTOR = 2
EXPERT_FF = 1408

LANES = 128
MXU_DIM = 256
VMEM_LIMIT = 56 * 1024 * 1024

QTILE_W = 2 * LANES
T_CKV = MLA_HEADS
T_KPE = T_CKV + MLA_KV_RANK // QTILE_W
T_QNA = T_KPE + 1
T_KNA = T_QNA + NA_HEADS * NA_HEAD_DIM // QTILE_W
T_VNA = T_KNA + NA_HEADS * NA_HEAD_DIM // QTILE_W
N_ABTILES = T_VNA + NA_HEADS * NA_HEAD_DIM // QTILE_W
AB_PERM_W = N_ABTILES * QTILE_W

MLA_SUBTILES = 2
NA_G = 4
NA_WR = NA_KH + NA_G - 1


def _params(sem, vmem=VMEM_LIMIT):
    return pltpu.CompilerParams(dimension_semantics=sem, vmem_limit_bytes=vmem)


def _dot(a, b):
    return jnp.dot(a, b, preferred_element_type=F32)


def _dot_nt(a, b):
    return lax.dot_general(a, b, (((1,), (1,)), ((), ())), preferred_element_type=F32)


def _dot_tn(a, b):
    return lax.dot_general(a, b, (((0,), (0,)), ((), ())), preferred_element_type=F32)


def _split_bf16(x):
    hi = x.astype(BF16)
    lo = (x - hi.astype(F32)).astype(BF16)
    return hi, lo


def _ada_kernel(c_ref, w_ref, b_ref, o_ref):
    c = c_ref[...]
    a = c / (1.0 + jnp.exp(-c))
    a_hi, a_lo = _split_bf16(a)
    w_hi, w_lo = _split_bf16(w_ref[0])
    rows = a.shape[0]
    r = _dot(jnp.concatenate([a_hi, a_lo], axis=0), w_hi)
    o_ref[0] = r[:rows] + r[rows:] + _dot(a_hi, w_lo) + b_ref[0]


def _ada_call(cv, ada_w, ada_b, tn=1024):
    depth, d, n = ada_w.shape
    rows = cv.shape[0]
    return pl.pallas_call(
        _ada_kernel,
        grid=(depth, n // tn),
        in_specs=[pl.BlockSpec((rows, d), lambda l, j: (0, 0)),
                  pl.BlockSpec((1, d, tn), lambda l, j: (l, 0, j)),
                  pl.BlockSpec((1, 1, tn), lambda l, j: (l, 0, j))],
        out_specs=pl.BlockSpec((1, rows, tn), lambda l, j: (l, 0, j)),
        out_shape=jax.ShapeDtypeStruct((depth, rows, n), F32),
        compiler_params=_params(("arbitrary", "arbitrary")),
        name="ada_params",
    )(cv, ada_w, ada_b.reshape(depth, 1, n))


def _modulate_kernel(x_ref, ss_ref, o_ref):
    o_ref[...] = (x_ref[...] * (1.0 + ss_ref[0, 1:2, :]) + ss_ref[0, 0:1, :]).astype(o_ref.dtype)


def _modulate_call(x2d, ss, rows_per_group, tm=256):
    m, d = x2d.shape
    tpg = rows_per_group // tm
    return pl.pallas_call(
        _modulate_kernel,
        grid=(m // tm,),
        in_specs=[pl.BlockSpec((tm, d), lambda i: (i, 0)),
                  pl.BlockSpec((1, 2, d), lambda i: (i // tpg, 0, 0))],
        out_specs=pl.BlockSpec((tm, d), lambda i: (i, 0)),
        out_shape=jax.ShapeDtypeStruct((m, d), BF16),
        compiler_params=_params(("arbitrary",)),
        name="modulate",
    )(x2d, ss)


def _mm_rope_kernel(a_ref, w_ref, ta_ref, tb_ref, tc_ref, o_ref):
    j = pl.program_id(1)
    acc = _dot(a_ref[...], w_ref[...])
    is_rope = jnp.logical_or(j < T_CKV, j == T_KPE)

    @pl.when(is_rope)
    def _():
        f = jnp.where(j < T_CKV, MLA_QK ** -0.5, 1.0).astype(F32)
        hi = acc[:, LANES:]
        rot = (hi * ta_ref[...] + pltpu.roll(hi, LANES - MLA_ROPE // 2, 1) * tb_ref[...]
               + pltpu.roll(hi, MLA_ROPE // 2, 1) * tc_ref[...])
        o_ref[:, :LANES] = (acc[:, :LANES] * f).astype(o_ref.dtype)
        o_ref[:, LANES:] = (rot * f).astype(o_ref.dtype)

    @pl.when(jnp.logical_not(is_rope))
    def _():
        f = jnp.where(jnp.logical_and(j >= T_QNA, j < T_KNA), NA_HEAD_DIM ** -0.5, 1.0).astype(F32)
        o_ref[...] = (acc * f).astype(o_ref.dtype)


def _mm_rope_call(a, w, tabs, tm):
    m, k = a.shape
    n = w.shape[1]
    ntab = tabs[0].shape[0] // tm
    tab_spec = pl.BlockSpec((tm, LANES), lambda i, j: (i % ntab, 0))
    return pl.pallas_call(
        _mm_rope_kernel,
        grid=(m // tm, n // QTILE_W),
        in_specs=[pl.BlockSpec((tm, k), lambda i, j: (i, 0)),
                  pl.BlockSpec((k, QTILE_W), lambda i, j: (0, j)),
                  tab_spec, tab_spec, tab_spec],
        out_specs=pl.BlockSpec((tm, QTILE_W), lambda i, j: (i, j)),
        out_shape=jax.ShapeDtypeStruct((m, n), BF16),
        compiler_params=_params(("arbitrary", "arbitrary")),
        name="attn_in_proj",
    )(a, w, *tabs)


def _rope_tables(n_tok):
    t = np.arange(n_tok)
    row = (t // GRID_W).astype(np.float32)
    col = (t % GRID_W).astype(np.float32)
    n_freq = MLA_ROPE // 4
    inv = (ROPE_THETA ** (-np.arange(n_freq, dtype=np.float32) / n_freq)).astype(np.float32)
    ang = np.concatenate([row[:, None] * inv, col[:, None] * inv], axis=1)
    cos, sin = np.cos(ang).astype(np.float32), np.sin(ang).astype(np.float32)
    half = MLA_ROPE // 2
    ta = np.zeros((n_tok, LANES), np.float32)
    tb = np.zeros((n_tok, LANES), np.float32)
    tc = np.zeros((n_tok, LANES), np.float32)
    ta[:, :half] = cos
    ta[:, half:2 * half] = cos
    tb[:, :half] = -sin
    tc[:, half:2 * half] = sin
    return jnp.asarray(ta), jnp.asarray(tb), jnp.asarray(tc)


def _identity_rope_tables(n_tok):
    ta = np.zeros((n_tok, LANES), np.float32)
    ta[:, :MLA_ROPE] = 1.0
    z = np.zeros((n_tok, LANES), np.float32)
    return jnp.asarray(ta), jnp.asarray(z), jnp.asarray(z)


def _permute_w_in(w):
    d = w.shape[0]
    zeros = lambda n: jnp.zeros((d, n), w.dtype)
    cols = []
    for h in range(MLA_HEADS):
        cols += [w[:, h * MLA_QK:(h + 1) * MLA_QK], zeros(QTILE_W - MLA_QK)]
    cols += [w[:, OFF_CKV:OFF_KPE], zeros(LANES), w[:, OFF_KPE:OFF_QNA], zeros(LANES - MLA_ROPE),
             w[:, OFF_QNA:]]
    return jnp.concatenate(cols, axis=1).astype(BF16)


def _kvup_kernel(x_ref, g_ref, w_ref, o_ref):
    x = x_ref[...].astype(F32)
    y = x * lax.rsqrt(jnp.mean(x * x, axis=-1, keepdims=True) + LN_EPS) * g_ref[...]
    o_ref[...] = _dot(y.astype(BF16), w_ref[...]).astype(o_ref.dtype)


def _kvup_call(p, g, w, tm):
    m = p.shape[0]
    r, n = w.shape
    return pl.pallas_call(
        _kvup_kernel,
        grid=(m // tm,),
        in_specs=[pl.BlockSpec((tm, r), lambda i: (i, T_CKV * QTILE_W // r)),
                  pl.BlockSpec((1, r), lambda i: (0, 0)),
                  pl.BlockSpec((r, n), lambda i: (0, 0))],
        out_specs=pl.BlockSpec((tm, n), lambda i: (i, 0)),
        out_shape=jax.ShapeDtypeStruct((m, n), BF16),
        compiler_params=_params(("arbitrary",)),
        name="kv_up",
    )(p, g.reshape(1, r), w)


def _permute_w_ukv(w):
    r = w.shape[0]
    w3 = w.reshape(r, MLA_HEADS, MLA_NOPE + MLA_V)
    return jnp.concatenate([w3[:, :, :MLA_NOPE].reshape(r, -1), w3[:, :, MLA_NOPE:].reshape(r, -1)],
                           axis=1).astype(BF16)


def _mla_kernel(q_ref, kn_ref, kp_ref, v_ref, kcn_ref, kcp_ref, vc_ref, o_ref, kcat, vcat):
    s_len = kn_ref.shape[0]

    @pl.when(pl.program_id(2) == 0)
    def _():
        kcat[:s_len, :LANES] = kn_ref[...]
        kcat[:s_len, LANES:] = kp_ref[...]
        kcat[s_len:, :LANES] = kcn_ref[...]
        kcat[s_len:, LANES:] = kcp_ref[...]
        vcat[:s_len, :] = v_ref[...]
        vcat[s_len:, :] = vc_ref[...]

    sub = q_ref.shape[0] // MLA_SUBTILES
    for r0 in range(0, q_ref.shape[0], sub):
        s = _dot_nt(q_ref[r0:r0 + sub, :], kcat[...])
        m = jnp.max(s, axis=-1, keepdims=True)
        p = jnp.exp(s - m)
        l = jnp.sum(p, axis=-1, keepdims=True)
        o = _dot(p.astype(BF16), vcat[...])
        o_ref[r0:r0 + sub, :] = (o / l).astype(o_ref.dtype)


def _mla_call(p_lat, kv_lat, p_ctx, kv_ctx, batch, s_len, lc, tq=512):
    nq = s_len // tq
    kpe_blk = T_KPE * 2 + 1
    return pl.pallas_call(
        _mla_kernel,
        grid=(batch, MLA_HEADS, nq),
        in_specs=[pl.BlockSpec((tq, QTILE_W), lambda b, h, i: (b * nq + i, h)),
                  pl.BlockSpec((s_len, LANES), lambda b, h, i: (b, h)),
                  pl.BlockSpec((s_len, LANES), lambda b, h, i: (b, kpe_blk)),
                  pl.BlockSpec((s_len, LANES), lambda b, h, i: (b, MLA_HEADS + h)),
                  pl.BlockSpec((lc, LANES), lambda b, h, i: (b, h)),
                  pl.BlockSpec((lc, LANES), lambda b, h, i: (b, kpe_blk)),
                  pl.BlockSpec((lc, LANES), lambda b, h, i: (b, MLA_HEADS + h))],
        out_specs=pl.BlockSpec((tq, LANES), lambda b, h, i: (b * nq + i, h)),
        out_shape=jax.ShapeDtypeStruct((batch * s_len, MLA_HEADS * MLA_V), BF16),
        scratch_shapes=[pltpu.VMEM((s_len + lc, QTILE_W), BF16),
                        pltpu.VMEM((s_len + lc, LANES), BF16)],
        compiler_params=_params(("arbitrary", "arbitrary", "arbitrary")),
        name="mla_attention",
    )(p_lat, kv_lat, p_lat, kv_lat, kv_ctx, p_ctx, kv_ctx)


def _na_kernel(var_ref, q_ref, k_ref, v_ref, kc_ref, vc_ref, bias_ref, o_ref, *, rows):
    del var_ref
    g = pl.program_id(2)
    start_row = jnp.clip(NA_G * g - NA_KH // 2, 0, rows - NA_WR)
    start = pl.multiple_of(start_row * GRID_W, GRID_W)
    win = NA_WR * GRID_W
    q = q_ref[...]
    sw = _dot_nt(q, k_ref[pl.ds(start, win), :]) + bias_ref[0, 0]
    sc = _dot_nt(q, kc_ref[...])
    m = jnp.maximum(jnp.max(sw, axis=-1, keepdims=True), jnp.max(sc, axis=-1, keepdims=True))
    pw = jnp.exp(sw - m)
    pc = jnp.exp(sc - m)
    l = jnp.sum(pw, axis=-1, keepdims=True) + jnp.sum(pc, axis=-1, keepdims=True)
    o = _dot(pw.astype(BF16), v_ref[pl.ds(start, win), :]) + _dot(pc.astype(BF16), vc_ref[...])
    o_ref[...] = (o / l).astype(o_ref.dtype)


def _na_tables(rows):
    kh = min(NA_KH, rows)
    ng = rows // NA_G
    qr = np.arange(NA_G)[:, None]
    kr = np.arange(NA_WR)[None, :]
    sel_rows = []
    for g in range(ng):
        start_row = int(np.clip(NA_G * g - NA_KH // 2, 0, rows - NA_WR))
        r = NA_G * g + qr
        krow = start_row + kr
        rs = np.clip(r - kh // 2, 0, rows - kh)
        ok = (krow >= rs) & (krow < rs + kh)
        dr = krow - r + (NA_KH - 1)
        sel_rows.append(np.stack([ok & (dr == d) for d in range(2 * NA_KH - 1)]))
    sel_rows = np.stack(sel_rows)
    uniq, inverse = np.unique(sel_rows.reshape(ng, -1), axis=0, return_inverse=True)
    er = uniq.reshape((-1,) + sel_rows.shape[1:]).astype(np.float32)
    qc = np.arange(GRID_W)[:, None]
    kc = np.arange(GRID_W)[None, :]
    cs = np.clip(qc - NA_KW // 2, 0, GRID_W - NA_KW)
    ok_c = (kc >= cs) & (kc < cs + NA_KW)
    dc = np.clip(kc - qc, -(NA_KW - 1), NA_KW - 1) + (NA_KW - 1)
    ec = np.stack([ok_c & (dc == e) for e in range(2 * NA_KW - 1)]).astype(np.float32)
    return inverse.reshape(-1).astype(np.int32), er, ec


def _na_bias(rpb, er, ec):
    hp = lax.Precision.HIGHEST
    t = jnp.einsum('hde,eqk->hdqk', rpb.astype(F32), jnp.asarray(ec), precision=hp)
    t = jnp.where(jnp.asarray(ec.sum(axis=0) > 0.5), t, NEG_INF)
    outside = jnp.full(t[:, 0].shape, NEG_INF, F32)
    d_of = np.where(er.sum(axis=1) > 0.5, er.argmax(axis=1), -1)
    block = lambda d: outside if d < 0 else t[:, d]
    return jnp.stack([
        jnp.concatenate([jnp.concatenate([block(int(d)) for d in row], axis=-1) for row in var], axis=-2)
        for var in d_of])


def _na_call(p_lat, p_ctx, bias, var_map, batch, s_len, lc):
    rows = s_len // GRID_W
    ng = rows // NA_G
    tq = NA_G * GRID_W
    win = NA_WR * GRID_W
    qb, kb, vb = T_QNA * 2, T_KNA * 2, T_VNA * 2
    grid_spec = pltpu.PrefetchScalarGridSpec(
        num_scalar_prefetch=1,
        grid=(batch, NA_HEADS, ng),
        in_specs=[pl.BlockSpec((tq, LANES), lambda b, h, g, vm: (b * ng + g, qb + h)),
                  pl.BlockSpec((s_len, LANES), lambda b, h, g, vm: (b, kb + h)),
                  pl.BlockSpec((s_len, LANES), lambda b, h, g, vm: (b, vb + h)),
                  pl.BlockSpec((lc, LANES), lambda b, h, g, vm: (b, kb + h)),
                  pl.BlockSpec((lc, LANES), lambda b, h, g, vm: (b, vb + h)),
                  pl.BlockSpec((1, 1, tq, win), lambda b, h, g, vm: (vm[g], h, 0, 0))],
        out_specs=pl.BlockSpec((tq, LANES), lambda b, h, g, vm: (b * ng + g, h)),
    )
    return pl.pallas_call(
        functools.partial(_na_kernel, rows=rows),
        grid_spec=grid_spec,
        out_shape=jax.ShapeDtypeStruct((batch * s_len, NA_HEADS * NA_HEAD_DIM), BF16),
        compiler_params=_params(("arbitrary", "arbitrary", "arbitrary")),
        name="na_attention",
    )(var_map, p_lat, p_lat, p_lat, p_ctx, p_ctx, bias)


def _layer_norm(x, g, b):
    mu = jnp.mean(x, axis=-1, keepdims=True)
    xc = x - mu
    var = jnp.mean(xc * xc, axis=-1, keepdims=True)
    return xc * lax.rsqrt(var + LN_EPS) * g + b


def _outln_kernel(a1_ref, a2_ref, w_ref, h_ref, mod_ref, ln_ref, r_ref, h1_ref, u2_ref, lg_ref):
    half = a1_ref.shape[1]
    y = _dot(a1_ref[...], w_ref[:half, :]) + _dot(a2_ref[...], w_ref[half:, :])
    x = DEEPNORM_ALPHA * h_ref[...] + mod_ref[0, 0:1, :] * y
    hn = _layer_norm(x, ln_ref[0:1, :], ln_ref[1:2, :])
    h1_ref[...] = hn
    u = hn * (1.0 + mod_ref[0, 2:3, :]) + mod_ref[0, 1:2, :]
    u_hi, u_lo = _split_bf16(u)
    u2_ref[...] = u_hi
    lg_ref[...] = _dot(u_hi, r_ref[0]) + _dot(u_lo, r_ref[0]) + _dot(u_hi, r_ref[1])


def _outln_call(a1, a2, w, h, mod, ln, router2, rows_per_group, tm=256):
    m, d = h.shape
    half = a1.shape[1]
    tpg = rows_per_group // tm
    return pl.pallas_call(
        _outln_kernel,
        grid=(m // tm,),
        in_specs=[pl.BlockSpec((tm, half), lambda i: (i, 0)),
                  pl.BlockSpec((tm, half), lambda i: (i, 0)),
                  pl.BlockSpec((2 * half, d), lambda i: (0, 0)),
                  pl.BlockSpec((tm, d), lambda i: (i, 0)),
                  pl.BlockSpec((1, 3, d), lambda i: (i // tpg, 0, 0)),
                  pl.BlockSpec((2, d), lambda i: (0, 0)),
                  pl.BlockSpec((2, d, LANES), lambda i: (0, 0, 0))],
        out_specs=[pl.BlockSpec((tm, d), lambda i: (i, 0)),
                   pl.BlockSpec((tm, d), lambda i: (i, 0)),
                   pl.BlockSpec((tm, LANES), lambda i: (i, 0))],
        out_shape=[jax.ShapeDtypeStruct((m, d), F32),
                   jax.ShapeDtypeStruct((m, d), BF16),
                   jax.ShapeDtypeStruct((m, LANES), F32)],
        compiler_params=_params(("arbitrary",)),
        name="out_proj_postnorm",
    )(a1, a2, w, h, mod, ln, router2)


def _router_operand(router):
    d, e = router.shape
    rp = jnp.pad(router.astype(F32), ((0, 0), (0, LANES - e)))
    hi = rp.astype(BF16)
    lo = (rp - hi.astype(F32)).astype(BF16)
    return jnp.stack([hi, lo])


def _route_kernel(lg_ref, slot_ref, aff_ref, tri_ref, *, cap):
    n_tok = lg_ref.shape[2]

    @pl.when(pl.program_id(0) == 0)
    def _():
        chunk = 256
        for r0 in range(0, n_tok, chunk):
            r = r0 + lax.broadcasted_iota(I32, (chunk, n_tok), 0)
            c = lax.broadcasted_iota(I32, (chunk, n_tok), 1)
            tri_ref[r0:r0 + chunk, :] = jnp.where(r < c, 1.0, 0.0).astype(BF16)

    lg = lg_ref[0]
    ex = jnp.exp(lg - jnp.max(lg, axis=0, keepdims=True))
    aff = ex / jnp.sum(ex, axis=0, keepdims=True)
    bits = lax.bitcast_convert_type(aff, I32)
    n_e = lg.shape[0]
    count = lambda mask: jnp.sum(jnp.where(mask, 1.0, 0.0), axis=1, keepdims=True)

    def body(_, lohi):
        lo, hi = lohi
        mid = lo + jnp.right_shift(hi - lo, 1)
        ge = count(bits >= mid) >= cap
        return jnp.where(ge, mid, lo), jnp.where(ge, hi, mid)

    lo0 = jnp.zeros((n_e, 1), I32)
    hi0 = jnp.full((n_e, 1), 0x7F800000, I32)
    thr, _ = lax.fori_loop(0, 31, body, (lo0, hi0))
    gt = bits > thr
    eq = bits == thr
    need = cap - count(gt)
    pre_eq = _dot(jnp.where(eq, 1.0, 0.0).astype(BF16), tri_ref[...])
    sel = jnp.logical_or(gt, jnp.logical_and(eq, pre_eq < need))
    slot = _dot(jnp.where(sel, 1.0, 0.0).astype(BF16), tri_ref[...])
    slot_ref[0] = jnp.where(sel, slot.astype(I32), -1)
    aff_ref[0] = aff


def _route_call(lg_t, cap):
    b, e, n_tok = lg_t.shape
    spec = pl.BlockSpec((1, e, n_tok), lambda i: (i, 0, 0))
    return pl.pallas_call(
        functools.partial(_route_kernel, cap=cap),
        grid=(b,),
        in_specs=[spec],
        out_specs=[spec, spec],
        out_shape=[jax.ShapeDtypeStruct((b, e, n_tok), I32), jax.ShapeDtypeStruct((b, e, n_tok), F32)],
        scratch_shapes=[pltpu.VMEM((n_tok, n_tok), BF16)],
        compiler_params=_params(("arbitrary",)),
        name="ec_route",
    )(lg_t)


def _gather_kernel(slot_ref, u_ref, o_ref):
    cap = o_ref.shape[1]
    n_tok = u_ref.shape[0]
    j = lax.broadcasted_iota(I32, (cap, n_tok), 0)
    onehot = jnp.where(j == slot_ref[0, 0], 1.0, 0.0).astype(BF16)
    o_ref[0] = _dot(onehot, u_ref[...]).astype(o_ref.dtype)


def _gather_call(slot4, u2, cap):
    b, e, _, n_tok = slot4.shape
    d = u2.shape[1]
    return pl.pallas_call(
        _gather_kernel,
        grid=(b, e),
        in_specs=[pl.BlockSpec((1, 1, 1, n_tok), lambda i, j: (i, j, 0, 0)),
                  pl.BlockSpec((n_tok, d), lambda i, j: (i, 0))],
        out_specs=pl.BlockSpec((1, cap, d), lambda i, j: (j, i, 0)),
        out_shape=jax.ShapeDtypeStruct((e, b * cap, d), BF16),
        compiler_params=_params(("arbitrary", "arbitrary")),
        name="moe_gather",
    )(slot4, u2)


def _expert_kernel(x_ref, w1_ref, w3_ref, w2_ref, o_ref, hd_ref, w2b_ref, *, nf, ff):
    f = pl.program_id(1)
    tf = w1_ref.shape[3]
    h = _dot(x_ref[0], jnp.concatenate([w1_ref[0, 0].astype(BF16), w3_ref[0, 0].astype(BF16)], axis=1))
    h1 = h[:, :tf]
    hd = ((h1 / (1.0 + jnp.exp(-h1))) * h[:, tf:]).astype(BF16)
    w2 = w2_ref[0, 0].astype(BF16)
    for s in range(nf):
        @pl.when(f == s)
        def _(s=s):
            hd_ref[:, s * tf:(s + 1) * tf] = hd
            w2b_ref[s * tf:(s + 1) * tf, :] = w2

    @pl.when(f == nf - 1)
    def _():
        nb, cap = o_ref.shape[0], o_ref.shape[1]
        rows = 2 * cap
        for r0 in range(0, nb * cap, rows):
            y = _dot(hd_ref[r0:r0 + rows, :ff], w2b_ref[:ff, :]).astype(o_ref.dtype)
            for b in range(rows // cap):
                o_ref[r0 // cap + b] = y[b * cap:(b + 1) * cap]


def _expert_call(xe, w1, w3, w2, layer, batch, tf=MXU_DIM):
    e, m, d = xe.shape
    cap = m // batch
    ff = w1.shape[3]
    nf = pl.cdiv(ff, tf)
    return pl.pallas_call(
        functools.partial(_expert_kernel, nf=nf, ff=ff),
        grid=(e, nf),
        in_specs=[pl.BlockSpec((1, m, d), lambda i, f: (i, 0, 0)),
                  pl.BlockSpec((1, 1, d, tf), lambda i, f: (layer, i, 0, f)),
                  pl.BlockSpec((1, 1, d, tf), lambda i, f: (layer, i, 0, f)),
                  pl.BlockSpec((1, 1, tf, d), lambda i, f: (layer, i, f, 0))],
        out_specs=pl.BlockSpec((batch, cap, d), lambda i, f: (0, i, 0)),
        out_shape=jax.ShapeDtypeStruct((batch, e * cap, d), BF16),
        scratch_shapes=[pltpu.VMEM((m, nf * tf), BF16),
                        pltpu.VMEM((nf * tf, d), BF16)],
        compiler_params=_params(("arbitrary", "arbitrary")),
        name="moe_experts",
    )(xe, w1, w3, w2)


def _scatter_kernel(slot_ref, aff_ref, ye_ref, h_ref, mod_ref, ln_ref, *rest, with_next):
    if with_next:
        nmod_ref, h2_ref, un_ref, pt_ref = rest
    else:
        h2_ref, pt_ref = rest
    n_e = slot_ref.shape[1]
    cap = ye_ref.shape[1] // n_e
    tt = h_ref.shape[0]
    j = lax.broadcasted_iota(I32, (cap, tt), 0)
    for e in range(n_e):
        pt_ref[e * cap:(e + 1) * cap, :] = jnp.where(j == slot_ref[0, e:e + 1, :],
                                                     aff_ref[0, e:e + 1, :], 0.0).astype(BF16)
    moe = _dot_tn(pt_ref[...], ye_ref[0])
    x = DEEPNORM_ALPHA * h_ref[...] + mod_ref[0] * moe
    hn = _layer_norm(x, ln_ref[0:1, :], ln_ref[1:2, :])
    h2_ref[...] = hn
    if with_next:
        un_ref[...] = (hn * (1.0 + nmod_ref[0, 1:2, :]) + nmod_ref[0, 0:1, :]).astype(un_ref.dtype)


def _scatter_call(slot, aff, ye, h1, gate, ln, next_ss, tt=512):
    b, e, n_tok = slot.shape
    m, d = h1.shape
    nt = n_tok // tt
    rows = ye.shape[1]
    with_next = next_ss is not None
    tok_spec = pl.BlockSpec((1, e, tt), lambda i, t: (i, 0, t))
    row_spec = pl.BlockSpec((tt, d), lambda i, t: (i * nt + t, 0))
    in_specs = [tok_spec, tok_spec,
                pl.BlockSpec((1, rows, d), lambda i, t: (i, 0, 0), pipeline_mode=pl.Buffered(1)),
                row_spec,
                pl.BlockSpec((1, 1, d), lambda i, t: (i, 0, 0)),
                pl.BlockSpec((2, d), lambda i, t: (0, 0))]
    args = [slot, aff, ye, h1, gate, ln]
    out_specs = [row_spec]
    out_shape = [jax.ShapeDtypeStruct((m, d), F32)]
    if with_next:
        in_specs.append(pl.BlockSpec((1, 2, d), lambda i, t: (i, 0, 0)))
        args.append(next_ss)
        out_specs.append(row_spec)
        out_shape.append(jax.ShapeDtypeStruct((m, d), BF16))
    res = pl.pallas_call(
        functools.partial(_scatter_kernel, with_next=with_next),
        grid=(b, nt),
        in_specs=in_specs,
        out_specs=out_specs,
        out_shape=out_shape,
        scratch_shapes=[pltpu.VMEM((rows, tt), BF16)],
        compiler_params=_params(("arbitrary", "arbitrary")),
        name="moe_combine_postnorm",
    )(*args)
    return res if with_next else (res[0], None)


def _ec_moe(h1, u2, logits, w1, w3, w2, layer, gate, ln, next_ss, batch, n_tok):
    cap = EC_CAPACITY_FACTOR * n_tok // N_EXPERTS
    lg_t = jnp.swapaxes(logits[:, :N_EXPERTS].reshape(batch, n_tok, N_EXPERTS), 1, 2)
    slot, aff = _route_call(lg_t, cap)
    xe = _gather_call(slot.reshape(batch, N_EXPERTS, 1, n_tok), u2, cap)
    ye = _expert_call(xe, w1, w3, w2, layer, batch)
    return _scatter_call(slot, aff, ye, h1, gate, ln, next_ss)


def _mm_kernel(a_ref, w_ref, o_ref):
    o_ref[...] = _dot(a_ref[...], w_ref[...]).astype(o_ref.dtype)


def _mm_call(a, w, out_dtype, tm=1024, tn=512):
    m, k = a.shape
    n = w.shape[1]
    return pl.pallas_call(
        _mm_kernel,
        grid=(m // tm, n // tn),
        in_specs=[pl.BlockSpec((tm, k), lambda i, j: (i, 0)),
                  pl.BlockSpec((k, tn), lambda i, j: (0, j))],
        out_specs=pl.BlockSpec((tm, tn), lambda i, j: (i, j)),
        out_shape=jax.ShapeDtypeStruct((m, n), out_dtype),
        compiler_params=_params(("arbitrary", "arbitrary")),
        name="matmul",
    )(a, w)


def _sconv_kernel(p_ref, w_ref, b_ref, o_ref):
    x = p_ref[...].astype(F32)
    n = x.shape[0]
    r = lax.broadcasted_iota(I32, x.shape, 0)
    prev = jnp.where(r == 0, 0.0, pltpu.roll(x, 1, 0))
    nxt = jnp.where(r == n - 1, 0.0, pltpu.roll(x, n - 1, 0))
    o_ref[...] = (b_ref[...] + w_ref[0:1, :] * prev + w_ref[1:2, :] * x + w_ref[2:3, :] * nxt
                  ).astype(o_ref.dtype)


def _sconv_call(p, conv_w, conv_b, col0, ncols, out_dtype, batch, n_tok, tc=512):
    c0 = col0 // tc
    return pl.pallas_call(
        _sconv_kernel,
        grid=(batch, ncols // tc),
        in_specs=[pl.BlockSpec((n_tok, tc), lambda b, j: (b, c0 + j)),
                  pl.BlockSpec((HY_SHORT, tc), lambda b, j: (0, c0 + j)),
                  pl.BlockSpec((1, tc), lambda b, j: (0, c0 + j))],
        out_specs=pl.BlockSpec((n_tok, tc), lambda b, j: (b, j)),
        out_shape=jax.ShapeDtypeStruct((batch * n_tok, ncols), out_dtype),
        compiler_params=_params(("arbitrary", "arbitrary")),
        name="hyena_short_conv",
    )(p, conv_w, conv_b.reshape(1, -1))


def _dft_fwd_kernel(fc_ref, fs_ref, r1_ref, r2_ref, *rest, spectral):
    xr = _dot(fc_ref[...], r1_ref[...])
    xi = _dot(fs_ref[...], r2_ref[...])
    if spectral:
        h_ref, nyq_ref, o_ref = rest
        hr = h_ref[0, 0]
        hi = h_ref[0, 1]
        dc = lax.broadcasted_iota(I32, xr.shape, 0) == 0
        yr = xr * hr - jnp.where(dc, 0.0, xi * hi)
        yi = jnp.where(dc, xi * nyq_ref[...], xr * hi + xi * hr)
    else:
        (o_ref,) = rest
        yr, yi = xr, xi
    o_ref[0, 0] = yr.astype(o_ref.dtype)
    o_ref[0, 1] = yi.astype(o_ref.dtype)


def _resident_spec(shape):
    return pl.BlockSpec(shape, lambda *_: (0,) * len(shape), pipeline_mode=pl.Buffered(1))


def _dft_fwd_call(fc, fs, r1, r2, spec, out_dtype, batch, n_tok, tn=256):
    ncols = r1.shape[1]
    r_spec = pl.BlockSpec((n_tok, tn), lambda j, b: (b, j))
    in_specs = [_resident_spec(fc.shape), _resident_spec(fs.shape), r_spec, r_spec]
    args = [fc, fs, r1, r2]
    if spec is not None:
        h, nyq, hcol0 = spec
        c0 = hcol0 // tn
        in_specs += [pl.BlockSpec((1, 2, n_tok, tn), lambda j, b: (0, 0, 0, c0 + j)),
                     pl.BlockSpec((1, tn), lambda j, b: (0, c0 + j))]
        args += [h, nyq]
    return pl.pallas_call(
        functools.partial(_dft_fwd_kernel, spectral=spec is not None),
        grid=(ncols // tn, batch),
        in_specs=in_specs,
        out_specs=pl.BlockSpec((1, 2, n_tok, tn), lambda j, b: (b, 0, 0, j)),
        out_shape=jax.ShapeDtypeStruct((batch, 2, n_tok, ncols), out_dtype),
        compiler_params=_params(("arbitrary", "arbitrary")),
        name="dft_forward",
    )(*args)


def _dual_kernel(a1_ref, a2_ref, b1_ref, b2_ref, *rest, hyena, planes):
    b1 = b1_ref[0, 0] if planes else b1_ref[...]
    b2 = b2_ref[0, 0] if planes else b2_ref[...]
    y = _dot(a1_ref[...], b1) + _dot(a2_ref[...], b2)
    if hyena:
        s_ref, z_ref, skip_ref, o_ref = rest
        y = s_ref[...].astype(F32) * (y + skip_ref[...] * z_ref[...].astype(F32))
    else:
        (o_ref,) = rest
    o_ref[...] = y.astype(o_ref.dtype)


def _dual_call(a1, a2, b1, b2, hy, batch, n_tok, ncols, tn=256):
    planes = b1.ndim == 4
    if planes:
        b_specs = [pl.BlockSpec((1, 1, n_tok, tn), lambda b, j: (b, 0, 0, j)),
                   pl.BlockSpec((1, 1, n_tok, tn), lambda b, j: (b, 1, 0, j))]
    else:
        b_specs = [pl.BlockSpec((n_tok, tn), lambda b, j: (b, j))] * 2
    in_specs = [_resident_spec(a1.shape), _resident_spec(a2.shape)] + b_specs
    args = [a1, a2, b1, b2]
    o_spec = pl.BlockSpec((n_tok, tn), lambda b, j: (b, j))
    if hy is not None:
        s, scol0, z, skip = hy
        c0 = scol0 // tn
        in_specs += [pl.BlockSpec((n_tok, tn), lambda b, j: (b, c0 + j)),
                     o_spec,
                     pl.BlockSpec((1, tn), lambda b, j: (0, j))]
        args += [s, z, skip]
    return pl.pallas_call(
        functools.partial(_dual_kernel, hyena=hy is not None, planes=planes),
        grid=(batch, ncols // tn),
        in_specs=in_specs,
        out_specs=o_spec,
        out_shape=jax.ShapeDtypeStruct((batch * n_tok, ncols), BF16),
        compiler_params=_params(("arbitrary", "arbitrary")),
        name="seq_mix_matmul",
    )(*args)


def _fnet_ch_kernel(p_ref, w_ref, gc_ref, gs_ref):
    r = _dot(p_ref[...].astype(BF16), w_ref[...])
    w = gc_ref.shape[1]
    gc_ref[...] = r[:, :w].astype(gc_ref.dtype)
    gs_ref[...] = r[:, w:].astype(gs_ref.dtype)


def _fnet_ch_call(p, w, col0, tm=1024):
    m = p.shape[0]
    gw = w.shape[0]
    c0 = col0 // gw
    o_spec = pl.BlockSpec((tm, gw), lambda i, g: (i, g))
    return pl.pallas_call(
        _fnet_ch_kernel,
        grid=(m // tm, FN_GROUPS),
        in_specs=[pl.BlockSpec((tm, gw), lambda i, g: (i, c0 + g)),
                  pl.BlockSpec((gw, 2 * gw), lambda i, g: (0, 0))],
        out_specs=[o_spec, o_spec],
        out_shape=[jax.ShapeDtypeStruct((m, FN_GROUPS * gw), BF16)] * 2,
        compiler_params=_params(("arbitrary", "arbitrary")),
        name="fnet_channel_dft",
    )(p, w)


def _cos_sin_matrix(n_rows, n_cols, period, split=64):
    r = np.arange(n_rows, dtype=np.int64)[:, None]
    c_hi = (np.arange(n_cols // split, dtype=np.int64) * split)[None, :]
    c_lo = np.arange(split, dtype=np.int64)[None, :]
    ang = lambda c: 2.0 * np.pi * ((r * c) % period).astype(np.float64) / period
    ca, sa = jnp.asarray(np.cos(ang(c_hi)), F32), jnp.asarray(np.sin(ang(c_hi)), F32)
    cb, sb = jnp.asarray(np.cos(ang(c_lo)), F32), jnp.asarray(np.sin(ang(c_lo)), F32)
    cos = ca[:, :, None] * cb[:, None, :] - sa[:, :, None] * sb[:, None, :]
    sin = sa[:, :, None] * cb[:, None, :] + ca[:, :, None] * sb[:, None, :]
    return cos.reshape(n_rows, n_cols), sin.reshape(n_rows, n_cols)


def _hyena_dft_operands(n_tok):
    n = 2 * n_tok
    cos, sin = _cos_sin_matrix(n_tok, n_tok, n)
    alt = jnp.asarray(((-1.0) ** np.arange(n_tok)), F32)
    row0 = (jnp.arange(n_tok) == 0)
    fc = cos
    fs = jnp.where(row0[:, None], alt[None, :], -sin)
    ck = jnp.where(row0, 1.0, 2.0)[None, :] / n
    gc = cos * ck
    gs = jnp.where(row0[None, :], alt[:, None] / n, -sin * (2.0 / n))
    return fc.astype(BF16), fs.astype(BF16), gc.astype(BF16), gs.astype(BF16), alt


def _fnet_operands(n_tok, gw):
    scale = 1.0 / math.sqrt(n_tok * gw)
    cl, sl = _cos_sin_matrix(n_tok, n_tok, n_tok)
    cw, sw = _cos_sin_matrix(gw, gw, gw)
    return ((cl * scale).astype(BF16), (-sl * scale).astype(BF16),
            jnp.concatenate([cw, sw], axis=1).astype(BF16))


def _hyena_filters(n_tok, fw1, fb1, ff1, fw2, fb2, ff2, fw3):
    t01 = jnp.linspace(0.0, 1.0, n_tok, dtype=F32)
    w = 2.0 * math.pi * jnp.arange(n_tok, dtype=F32) / n_tok
    bands = jnp.linspace(1e-4, HY_BANDS - 1, HY_BANDS, dtype=F32)
    z = jnp.concatenate([t01[:, None], jnp.cos(w[:, None] * bands), -jnp.sin(w[:, None] * bands)], -1)
    hp = lax.Precision.HIGHEST
    hdn = jnp.sin(ff1 * (jnp.dot(z, fw1, precision=hp) + fb1))
    hdn = jnp.sin(ff2 * (jnp.dot(hdn, fw2, precision=hp) + fb2))
    filt = jnp.dot(hdn, fw3, precision=hp)
    deltas = jnp.abs(jnp.linspace(HY_MIN_DECAY, HY_MAX_DECAY, HY_W, dtype=F32))
    decay = jnp.exp(-t01[:, None] * deltas)
    return filt.reshape(n_tok, HY_ORDER, 2, HY_W) * decay[:, None, None, :]


def kernel(x, c, ctx, c_ctx, ada_w, ada_b, ln1_g, ln1_b, ln2_g, ln2_b, router, exp_w1, exp_w3, exp_w2,
           ab_w_in, ab_kv_norm, ab_w_ukv, ab_rpb, ab_w_out,
           cd_w_in, cd_conv_w, cd_conv_b, cd_filt_w1, cd_filt_b1, cd_filt_freq1, cd_filt_w2, cd_filt_b2,
           cd_filt_freq2, cd_filt_w3, cd_skip, cd_w_out):
    batch, n_tok, d = x.shape
    lc = ctx.shape[1]
    x2d = x.reshape(batch * n_tok, d)
    ctx2d = ctx.reshape(batch * lc, d)

    pad_rows = (-(batch + 1)) % 8
    cv = jnp.concatenate([c, c_ctx[None, :], jnp.zeros((pad_rows, d), F32)], axis=0)
    ada = _ada_call(cv, ada_w, ada_b)
    mods = [ada[i, :batch].reshape(batch, 6, d) for i in range(DEPTH)]
    ctx_ss = ada[0, batch, :2 * d].reshape(1, 2, d)

    u_lat = _modulate_call(x2d, mods[0][:, 0:2], n_tok)
    u_ctx = _modulate_call(ctx2d, ctx_ss, batch * lc)
    w_in = _permute_w_in(ab_w_in[0])
    p_lat = _mm_rope_call(u_lat, w_in, _rope_tables(n_tok), tm=1024)
    p_ctx = _mm_rope_call(u_ctx, w_in, _identity_rope_tables(lc), tm=lc)
    w_ukv = _permute_w_ukv(ab_w_ukv[0])
    kv_lat = _kvup_call(p_lat, ab_kv_norm[0], w_ukv, tm=512)
    kv_ctx = _kvup_call(p_ctx, ab_kv_norm[0], w_ukv, tm=lc)
    a_out = _mla_call(p_lat, kv_lat, p_ctx, kv_ctx, batch, n_tok, lc)
    var_map, na_er, na_ec = _na_tables(n_tok // GRID_W)
    b_out = _na_call(p_lat, p_ctx, _na_bias(ab_rpb[0], na_er, na_ec), jnp.asarray(var_map),
                     batch, n_tok, lc)
    ln1 = jnp.stack([ln1_g, ln1_b], axis=1)
    ln2 = jnp.stack([ln2_g, ln2_b], axis=1)
    h1, u2, logits = _outln_call(a_out, b_out, ab_w_out[0].astype(BF16), x2d,
                                 mods[0][:, 2:5], ln1[0], _router_operand(router[0]), n_tok)
    h, u = _ec_moe(h1, u2, logits, exp_w1, exp_w3, exp_w2, 0, mods[0][:, 5:6], ln2[0],
                   mods[1][:, 0:2], batch, n_tok)

    p1 = _mm_call(u, cd_w_in[0].astype(BF16), BF16, tn=1024)
    s0 = _sconv_call(p1, cd_conv_w[0], cd_conv_b[0], 0, HY_W, BF16, batch, n_tok)
    s12 = _sconv_call(p1, cd_conv_w[0], cd_conv_b[0], HY_W, HY_ORDER * HY_W, F32, batch, n_tok)
    filt = _hyena_filters(n_tok, cd_filt_w1[0], cd_filt_b1[0], cd_filt_freq1[0], cd_filt_w2[0],
                          cd_filt_b2[0], cd_filt_freq2[0], cd_filt_w3[0])
    hf = filt[:, :, 0, :].reshape(n_tok, HY_ORDER * HY_W)
    hb = jnp.where((jnp.arange(n_tok) == 0)[:, None], 0.0, filt[:, :, 1, :].reshape(n_tok, HY_ORDER * HY_W))
    fc, fs, gc, gs, alt = _hyena_dft_operands(n_tok)
    h_sum = hf + hb
    h_nyq = jnp.sum(alt[:, None] * h_sum, axis=0, keepdims=True)
    h_spec = _dft_fwd_call(fc, fs, h_sum.astype(BF16), (hf - hb).astype(BF16), None, F32, 1, n_tok)
    z = s0
    for o in range(HY_ORDER):
        y_spec = _dft_fwd_call(fc, fs, z, z, (h_spec, h_nyq, o * HY_W), BF16, batch, n_tok)
        z = _dual_call(gc, gs, y_spec, y_spec, (s12, o * HY_W, z, cd_skip[0][o:o + 1]), batch, n_tok, HY_W)
    cl, msl, cw = _fnet_operands(n_tok, FN_GROUP_W)
    g_cos, g_sin = _fnet_ch_call(p1, cw, HY_IN_W)
    y_fn = _dual_call(cl, msl, g_cos, g_sin, None, batch, n_tok, FN_W)
    h1, u2, logits = _outln_call(z, y_fn, cd_w_out[0].astype(BF16), h,
                                 mods[1][:, 2:5], ln1[1], _router_operand(router[1]), n_tok)
    h, _ = _ec_moe(h1, u2, logits, exp_w1, exp_w3, exp_w2, 1, mods[1][:, 5:6], ln2[1],
                   None, batch, n_tok)
    return h.reshape(batch, n_tok, d)
```

```python
import functools
import math

import numpy as np
import jax
import jax.numpy as jnp
from jax import lax
from jax.experimental import pallas as pl
from jax.experimental.pallas import tpu as pltpu

F32 = jnp.float32
BF16 = jnp.bfloat16
I32 = jnp.int32

D_MODEL = 2048
BATCH = 4
SEQ = 2048
DEPTH = 2
CTX_LEN = 256
GRID_W = 64
DEEPNORM_ALPHA = (2.0 * DEPTH) ** 0.25
LN_EPS = 1e-6
NEG_INF = -1e30

MLA_HEADS = 8
MLA_NOPE = 128
MLA_ROPE = 64
MLA_QK = MLA_NOPE + MLA_ROPE
MLA_V = 128
MLA_KV_RANK = 512
ROPE_THETA = 10000.0

NA_HEADS = 8
NA_HEAD_DIM = 128
NA_KH = 8
NA_KW = 16

OFF_CKV = MLA_HEADS * MLA_QK
OFF_KPE = OFF_CKV + MLA_KV_RANK
OFF_QNA = OFF_KPE + MLA_ROPE
OFF_KNA = OFF_QNA + NA_HEADS * NA_HEAD_DIM
OFF_VNA = OFF_KNA + NA_HEADS * NA_HEAD_DIM

HY_W = 1024
HY_ORDER = 2
HY_IN_W = (HY_ORDER + 1) * HY_W
HY_SHORT = 3
HY_BANDS = 16
HY_DECAY_TARGET = 1e-2
HY_MIN_DECAY = math.log(HY_DECAY_TARGET) / 1.5
HY_MAX_DECAY = math.log(HY_DECAY_TARGET) / 0.3

FN_W = 1024
FN_GROUPS = 4
FN_GROUP_W = FN_W // FN_GROUPS

N_EXPERTS = 16
EC_CAPACITY_FACTOR = 2
EXPERT_FF = 1408

LANES = 128
MXU_DIM = 256
VMEM_LIMIT = 56 * 1024 * 1024

QTILE_W = 2 * LANES
T_KPE = MLA_HEADS
T_CKV = T_KPE + 2
T_QNA = T_CKV + MLA_KV_RANK // QTILE_W
T_KNA = T_QNA + NA_HEADS * NA_HEAD_DIM // QTILE_W
T_VNA = T_KNA + NA_HEADS * NA_HEAD_DIM // QTILE_W
N_ABTILES = T_VNA + NA_HEADS * NA_HEAD_DIM // QTILE_W
AB_PERM_W = N_ABTILES * QTILE_W

MLA_SUBTILES = 2
OUTLN_SUBTILES = 2
NA_G = 4
NA_WR = NA_KH + NA_G - 1
NA_GROUPS_PER_STEP = 2


def _params(sem, vmem=VMEM_LIMIT):
    return pltpu.CompilerParams(dimension_semantics=sem, vmem_limit_bytes=vmem)


def _dot(a, b):
    return jnp.dot(a, b, preferred_element_type=F32)


def _dot_nt(a, b):
    return lax.dot_general(a, b, (((1,), (1,)), ((), ())), preferred_element_type=F32)


def _dot_tn(a, b):
    return lax.dot_general(a, b, (((0,), (0,)), ((), ())), preferred_element_type=F32)


def _split_bf16(x):
    hi = x.astype(BF16)
    lo = (x - hi.astype(F32)).astype(BF16)
    return hi, lo


def _ada_kernel(c_ref, w_ref, b_ref, o_ref):
    c = c_ref[...]
    a = c / (1.0 + jnp.exp(-c))
    a_hi, a_lo = _split_bf16(a)
    w_hi, w_lo = _split_bf16(w_ref[0])
    rows = a.shape[0]
    r = _dot(jnp.concatenate([a_hi, a_lo], axis=0), w_hi)
    o_ref[0] = r[:rows] + r[rows:] + _dot(a_hi, w_lo) + b_ref[0]


def _ada_call(cv, ada_w, ada_b, tn=1024):
    depth, d, n = ada_w.shape
    rows = cv.shape[0]
    return pl.pallas_call(
        _ada_kernel,
        grid=(depth, n // tn),
        in_specs=[pl.BlockSpec((rows, d), lambda l, j: (0, 0)),
                  pl.BlockSpec((1, d, tn), lambda l, j: (l, 0, j)),
                  pl.BlockSpec((1, 1, tn), lambda l, j: (l, 0, j))],
        out_specs=pl.BlockSpec((1, rows, tn), lambda l, j: (l, 0, j)),
        out_shape=jax.ShapeDtypeStruct((depth, rows, n), F32),
        compiler_params=_params(("arbitrary", "arbitrary")),
        name="ada_params",
    )(cv, ada_w, ada_b.reshape(depth, 1, n))


def _modulate_kernel(x_ref, ss_ref, o_ref):
    o_ref[...] = (x_ref[...] * (1.0 + ss_ref[0, 1:2, :]) + ss_ref[0, 0:1, :]).astype(o_ref.dtype)


def _modulate_call(x2d, ss, rows_per_group, tm=256):
    m, d = x2d.shape
    tpg = rows_per_group // tm
    return pl.pallas_call(
        _modulate_kernel,
        grid=(m // tm,),
        in_specs=[pl.BlockSpec((tm, d), lambda i: (i, 0)),
                  pl.BlockSpec((1, 2, d), lambda i: (i // tpg, 0, 0))],
        out_specs=pl.BlockSpec((tm, d), lambda i: (i, 0)),
        out_shape=jax.ShapeDtypeStruct((m, d), BF16),
        compiler_params=_params(("arbitrary",)),
        name="modulate",
    )(x2d, ss)


def _mm_rope_kernel(a_ref, w_ref, ta_ref, tb_ref, tc_ref, o_ref):
    a = a_ref[...]
    for half in range(2):
        t = 2 * pl.program_id(1) + half
        c0 = half * QTILE_W
        acc = _dot(a, w_ref[:, c0:c0 + QTILE_W])
        is_rope = t <= T_KPE
        f = jnp.where(t < T_KPE, MLA_QK ** -0.5,
                      jnp.where(jnp.logical_and(t >= T_QNA, t < T_KNA), NA_HEAD_DIM ** -0.5, 1.0)).astype(F32)
        hi = acc[:, LANES:]
        rot = (hi * jnp.where(is_rope, ta_ref[...], 1.0)
               + pltpu.roll(hi, LANES - MLA_ROPE // 2, 1) * jnp.where(is_rope, tb_ref[...], 0.0)
               + pltpu.roll(hi, MLA_ROPE // 2, 1) * jnp.where(is_rope, tc_ref[...], 0.0))
        o_ref[:, c0:c0 + LANES] = (acc[:, :LANES] * f).astype(o_ref.dtype)
        o_ref[:, c0 + LANES:c0 + QTILE_W] = (rot * f).astype(o_ref.dtype)


def _mm_rope_call(a, w, tabs, tm):
    m, k = a.shape
    n = w.shape[1]
    tn = 2 * QTILE_W
    ntab = tabs[0].shape[0] // tm
    tab_spec = pl.BlockSpec((tm, LANES), lambda i, j: (i % ntab, 0))
    return pl.pallas_call(
        _mm_rope_kernel,
        grid=(m // tm, n // tn),
        in_specs=[pl.BlockSpec((tm, k), lambda i, j: (i, 0)),
                  pl.BlockSpec((k, tn), lambda i, j: (0, j)),
                  tab_spec, tab_spec, tab_spec],
        out_specs=pl.BlockSpec((tm, tn), lambda i, j: (i, j)),
        out_shape=jax.ShapeDtypeStruct((m, n), BF16),
        compiler_params=_params(("arbitrary", "arbitrary")),
        name="attn_in_proj",
    )(a, w, *tabs)


def _rope_tables(n_tok):
    t = np.arange(n_tok)
    row = (t // GRID_W).astype(np.float32)
    col = (t % GRID_W).astype(np.float32)
    n_freq = MLA_ROPE // 4
    inv = (ROPE_THETA ** (-np.arange(n_freq, dtype=np.float32) / n_freq)).astype(np.float32)
    ang = np.concatenate([row[:, None] * inv, col[:, None] * inv], axis=1)
    cos, sin = np.cos(ang).astype(np.float32), np.sin(ang).astype(np.float32)
    half = MLA_ROPE // 2
    ta = np.zeros((n_tok, LANES), np.float32)
    tb = np.zeros((n_tok, LANES), np.float32)
    tc = np.zeros((n_tok, LANES), np.float32)
    ta[:, :half] = cos
    ta[:, half:2 * half] = cos
    tb[:, :half] = -sin
    tc[:, half:2 * half] = sin
    return jnp.asarray(ta), jnp.asarray(tb), jnp.asarray(tc)


def _identity_rope_tables(n_tok):
    ta = np.zeros((n_tok, LANES), np.float32)
    ta[:, :MLA_ROPE] = 1.0
    z = np.zeros((n_tok, LANES), np.float32)
    return jnp.asarray(ta), jnp.asarray(z), jnp.asarray(z)


def _w_in_prep_kernel(w_ref, o_ref):
    x = w_ref[0]
    zeros = lambda n: jnp.zeros((x.shape[0], n), x.dtype)
    cols = []
    for h in range(MLA_HEADS):
        cols += [x[:, h * MLA_QK:(h + 1) * MLA_QK], zeros(QTILE_W - MLA_QK)]
    cols += [zeros(LANES), x[:, OFF_KPE:OFF_QNA], zeros(LANES - MLA_ROPE), zeros(QTILE_W),
             x[:, OFF_CKV:OFF_KPE], x[:, OFF_QNA:]]
    o_ref[...] = jnp.concatenate(cols, axis=1).astype(o_ref.dtype)


def _w_in_prep_call(w_in, layer, tm=256):
    _, d, n = w_in.shape
    return pl.pallas_call(
        _w_in_prep_kernel,
        grid=(d // tm,),
        in_specs=[pl.BlockSpec((1, tm, n), lambda i: (layer, i, 0))],
        out_specs=pl.BlockSpec((tm, AB_PERM_W), lambda i: (i, 0)),
        out_shape=jax.ShapeDtypeStruct((d, AB_PERM_W), BF16),
        compiler_params=_params(("arbitrary",)),
        name="attn_w_in_prep",
    )(w_in)


def _kvup_kernel(x_ref, g_ref, w_ref, o_ref):
    x = x_ref[...].astype(F32)
    y = x * lax.rsqrt(jnp.mean(x * x, axis=-1, keepdims=True) + LN_EPS) * g_ref[...]
    o_ref[...] = _dot(y.astype(BF16), w_ref[...]).astype(o_ref.dtype)


def _kvup_call(p, g, w, tm):
    m = p.shape[0]
    r, n = w.shape
    return pl.pallas_call(
        _kvup_kernel,
        grid=(m // tm,),
        in_specs=[pl.BlockSpec((tm, r), lambda i: (i, T_CKV * QTILE_W // r)),
                  pl.BlockSpec((1, r), lambda i: (0, 0)),
                  pl.BlockSpec((r, n), lambda i: (0, 0))],
        out_specs=pl.BlockSpec((tm, n), lambda i: (i, 0)),
        out_shape=jax.ShapeDtypeStruct((m, n), BF16),
        compiler_params=_params(("arbitrary",)),
        name="kv_up",
    )(p, g.reshape(1, r), w)


def _permute_w_ukv(w):
    r = w.shape[0]
    w3 = w.reshape(r, MLA_HEADS, MLA_NOPE + MLA_V)
    return jnp.concatenate([w3[:, :, :MLA_NOPE].reshape(r, -1), w3[:, :, MLA_NOPE:].reshape(r, -1)],
                           axis=1).astype(BF16)


def _mla_kernel(q_ref, kn_ref, kp_ref, v_ref, kcn_ref, kcp_ref, vc_ref, o_ref, kcat, vcat):
    s_len = kn_ref.shape[0]

    @pl.when(pl.program_id(2) == 0)
    def _():
        kcat[:s_len, :LANES] = kn_ref[...]
        kcat[:s_len, LANES:] = kp_ref[...]
        kcat[s_len:, :LANES] = kcn_ref[...]
        kcat[s_len:, LANES:] = kcp_ref[...]
        vcat[:s_len, :] = v_ref[...]
        vcat[s_len:, :] = vc_ref[...]

    sub = q_ref.shape[0] // MLA_SUBTILES
    for r0 in range(0, q_ref.shape[0], sub):
        s = _dot_nt(q_ref[r0:r0 + sub, :], kcat[...])
        m = jnp.max(s, axis=-1, keepdims=True)
        p = jnp.exp(s - m)
        l = jnp.sum(p, axis=-1, keepdims=True)
        o = _dot(p.astype(BF16), vcat[...])
        o_ref[r0:r0 + sub, :] = (o / l).astype(o_ref.dtype)


def _mla_call(p_lat, kv_lat, p_ctx, kv_ctx, batch, s_len, lc, tq=512):
    nq = s_len // tq
    kpe_blk = T_KPE * 2 + 1
    return pl.pallas_call(
        _mla_kernel,
        grid=(batch, MLA_HEADS, nq),
        in_specs=[pl.BlockSpec((tq, QTILE_W), lambda b, h, i: (b * nq + i, h)),
                  pl.BlockSpec((s_len, LANES), lambda b, h, i: (b, h)),
                  pl.BlockSpec((s_len, LANES), lambda b, h, i: (b, kpe_blk)),
                  pl.BlockSpec((s_len, LANES), lambda b, h, i: (b, MLA_HEADS + h)),
                  pl.BlockSpec((lc, LANES), lambda b, h, i: (b, h)),
                  pl.BlockSpec((lc, LANES), lambda b, h, i: (b, kpe_blk)),
                  pl.BlockSpec((lc, LANES), lambda b, h, i: (b, MLA_HEADS + h))],
        out_specs=pl.BlockSpec((tq, LANES), lambda b, h, i: (b * nq + i, h)),
        out_shape=jax.ShapeDtypeStruct((batch * s_len, MLA_HEADS * MLA_V), BF16),
        scratch_shapes=[pltpu.VMEM((s_len + lc, QTILE_W), BF16),
                        pltpu.VMEM((s_len + lc, LANES), BF16)],
        compiler_params=_params(("arbitrary", "arbitrary", "arbitrary")),
        name="mla_attention",
    )(p_lat, kv_lat, p_lat, kv_lat, kv_ctx, p_ctx, kv_ctx)


def _na_kernel(var_ref, q_ref, k_ref, v_ref, kc_ref, vc_ref, *rest, rows):
    del var_ref
    bias_refs, o_ref = rest[:-1], rest[-1]
    win = NA_WR * GRID_W
    tq = NA_G * GRID_W
    for i, bias_ref in enumerate(bias_refs):
        g = len(bias_refs) * pl.program_id(2) + i
        start_row = jnp.clip(NA_G * g - NA_KH // 2, 0, rows - NA_WR)
        start = pl.multiple_of(start_row * GRID_W, GRID_W)
        q = q_ref[i * tq:(i + 1) * tq, :]
        sw = _dot_nt(q, k_ref[pl.ds(start, win), :]) + bias_ref[0, 0]
        sc = _dot_nt(q, kc_ref[...])
        m = jnp.maximum(jnp.max(sw, axis=-1, keepdims=True), jnp.max(sc, axis=-1, keepdims=True))
        pw = jnp.exp(sw - m)
        pc = jnp.exp(sc - m)
        l = jnp.sum(pw, axis=-1, keepdims=True) + jnp.sum(pc, axis=-1, keepdims=True)
        o = _dot(pw.astype(BF16), v_ref[pl.ds(start, win), :]) + _dot(pc.astype(BF16), vc_ref[...])
        o_ref[i * tq:(i + 1) * tq, :] = (o / l).astype(o_ref.dtype)


def _na_tables(rows):
    kh = min(NA_KH, rows)
    ng = rows // NA_G
    qr = np.arange(NA_G)[:, None]
    kr = np.arange(NA_WR)[None, :]
    sel_rows = []
    for g in range(ng):
        start_row = int(np.clip(NA_G * g - NA_KH // 2, 0, rows - NA_WR))
        r = NA_G * g + qr
        krow = start_row + kr
        rs = np.clip(r - kh // 2, 0, rows - kh)
        ok = (krow >= rs) & (krow < rs + kh)
        dr = krow - r + (NA_KH - 1)
        sel_rows.append(np.stack([ok & (dr == d) for d in range(2 * NA_KH - 1)]))
    sel_rows = np.stack(sel_rows)
    uniq, inverse = np.unique(sel_rows.reshape(ng, -1), axis=0, return_inverse=True)
    er = uniq.reshape((-1,) + sel_rows.shape[1:]).astype(np.float32)
    qc = np.arange(GRID_W)[:, None]
    kc = np.arange(GRID_W)[None, :]
    cs = np.clip(qc - NA_KW // 2, 0, GRID_W - NA_KW)
    ok_c = (kc >= cs) & (kc < cs + NA_KW)
    dc = np.clip(kc - qc, -(NA_KW - 1), NA_KW - 1) + (NA_KW - 1)
    ec = np.stack([ok_c & (dc == e) for e in range(2 * NA_KW - 1)]).astype(np.float32)
    return inverse.reshape(-1).astype(np.int32), er, ec


def _na_bias(rpb, er, ec):
    hp = lax.Precision.HIGHEST
    t = jnp.einsum('hde,eqk->hdqk', rpb.astype(F32), jnp.asarray(ec), precision=hp)
    t = jnp.where(jnp.asarray(ec.sum(axis=0) > 0.5), t, NEG_INF)
    outside = jnp.full(t[:, 0].shape, NEG_INF, F32)
    d_of = np.where(er.sum(axis=1) > 0.5, er.argmax(axis=1), -1)
    block = lambda d: outside if d < 0 else t[:, d]
    return jnp.stack([
        jnp.concatenate([jnp.concatenate([block(int(d)) for d in row], axis=-1) for row in var], axis=-2)
        for var in d_of])


def _na_call(p_lat, p_ctx, bias, var_map, batch, s_len, lc):
    rows = s_len // GRID_W
    gps = NA_GROUPS_PER_STEP
    ns = rows // (NA_G * gps)
    tq = NA_G * GRID_W
    win = NA_WR * GRID_W
    qb, kb, vb = T_QNA * 2, T_KNA * 2, T_VNA * 2
    bias_spec = lambda i: pl.BlockSpec((1, 1, tq, win), lambda b, h, s, vm: (vm[gps * s + i], h, 0, 0))
    grid_spec = pltpu.PrefetchScalarGridSpec(
        num_scalar_prefetch=1,
        grid=(batch, NA_HEADS, ns),
        in_specs=[pl.BlockSpec((gps * tq, LANES), lambda b, h, s, vm: (b * ns + s, qb + h)),
                  pl.BlockSpec((s_len, LANES), lambda b, h, s, vm: (b, kb + h)),
                  pl.BlockSpec((s_len, LANES), lambda b, h, s, vm: (b, vb + h)),
                  pl.BlockSpec((lc, LANES), lambda b, h, s, vm: (b, kb + h)),
                  pl.BlockSpec((lc, LANES), lambda b, h, s, vm: (b, vb + h))]
                 + [bias_spec(i) for i in range(gps)],
        out_specs=pl.BlockSpec((gps * tq, LANES), lambda b, h, s, vm: (b * ns + s, h)),
    )
    return pl.pallas_call(
        functools.partial(_na_kernel, rows=rows),
        grid_spec=grid_spec,
        out_shape=jax.ShapeDtypeStruct((batch * s_len, NA_HEADS * NA_HEAD_DIM), BF16),
        compiler_params=_params(("arbitrary", "arbitrary", "arbitrary")),
        name="na_attention",
    )(var_map, p_lat, p_lat, p_lat, p_ctx, p_ctx, *([bias] * gps))


def _layer_norm(x, g, b):
    mu = jnp.mean(x, axis=-1, keepdims=True)
    xc = x - mu
    var = jnp.mean(xc * xc, axis=-1, keepdims=True)
    return xc * lax.rsqrt(var + LN_EPS) * g + b


def _outln_kernel(a1_ref, a2_ref, w_ref, h_ref, mod_ref, ln_ref, r_ref, h1_ref, u2_ref, lg_ref):
    half = a1_ref.shape[1]
    sub = a1_ref.shape[0] // OUTLN_SUBTILES
    for r0 in range(0, a1_ref.shape[0], sub):
        rs = slice(r0, r0 + sub)
        y = _dot(a1_ref[rs, :], w_ref[:half, :]) + _dot(a2_ref[rs, :], w_ref[half:, :])
        x = DEEPNORM_ALPHA * h_ref[rs, :] + mod_ref[0, 0:1, :] * y
        hn = _layer_norm(x, ln_ref[0:1, :], ln_ref[1:2, :])
        h1_ref[rs, :] = hn
        u = hn * (1.0 + mod_ref[0, 2:3, :]) + mod_ref[0, 1:2, :]
        u_hi, u_lo = _split_bf16(u)
        u2_ref[rs, :] = u_hi
        lg_ref[rs, :] = _dot(u_hi, r_ref[0]) + _dot(u_lo, r_ref[0]) + _dot(u_hi, r_ref[1])


def _outln_call(a1, a2, w, h, mod, ln, router2, rows_per_group, tm=512):
    m, d = h.shape
    half = a1.shape[1]
    tpg = rows_per_group // tm
    return pl.pallas_call(
        _outln_kernel,
        grid=(m // tm,),
        in_specs=[pl.BlockSpec((tm, half), lambda i: (i, 0)),
                  pl.BlockSpec((tm, half), lambda i: (i, 0)),
                  _resident_spec((2 * half, d)),
                  pl.BlockSpec((tm, d), lambda i: (i, 0)),
                  pl.BlockSpec((1, 3, d), lambda i: (i // tpg, 0, 0)),
                  pl.BlockSpec((2, d), lambda i: (0, 0)),
                  _resident_spec((2, d, LANES))],
        out_specs=[pl.BlockSpec((tm, d), lambda i: (i, 0)),
                   pl.BlockSpec((tm, d), lambda i: (i, 0)),
                   pl.BlockSpec((tm, LANES), lambda i: (i, 0))],
        out_shape=[jax.ShapeDtypeStruct((m, d), F32),
                   jax.ShapeDtypeStruct((m, d), BF16),
                   jax.ShapeDtypeStruct((m, LANES), F32)],
        compiler_params=_params(("arbitrary",)),
        name="out_proj_postnorm",
    )(a1, a2, w, h, mod, ln, router2)


def _router_operand(router):
    d, e = router.shape
    rp = jnp.pad(router.astype(F32), ((0, 0), (0, LANES - e)))
    hi = rp.astype(BF16)
    lo = (rp - hi.astype(F32)).astype(BF16)
    return jnp.stack([hi, lo])


def _route_kernel(lg_ref, slot_ref, aff_ref, tri_ref, *, cap):
    n_tok = lg_ref.shape[2]

    @pl.when(pl.program_id(0) == 0)
    def _():
        chunk = 256
        for r0 in range(0, n_tok, chunk):
            r = r0 + lax.broadcasted_iota(I32, (chunk, n_tok), 0)
            c = lax.broadcasted_iota(I32, (chunk, n_tok), 1)
            tri_ref[r0:r0 + chunk, :] = jnp.where(r < c, 1.0, 0.0).astype(BF16)

    lg = lg_ref[0]
    ex = jnp.exp(lg - jnp.max(lg, axis=0, keepdims=True))
    aff = ex / jnp.sum(ex, axis=0, keepdims=True)
    bits = lax.bitcast_convert_type(aff, I32)
    n_e = lg.shape[0]
    count = lambda mask: jnp.sum(jnp.where(mask, 1.0, 0.0), axis=1, keepdims=True)

    def body(_, lohi):
        lo, hi = lohi
        mid = lo + jnp.right_shift(hi - lo, 1)
        ge = count(bits >= mid) >= cap
        return jnp.where(ge, mid, lo), jnp.where(ge, hi, mid)

    lo0 = jnp.zeros((n_e, 1), I32)
    hi0 = jnp.full((n_e, 1), 0x7F800000, I32)
    thr, _ = lax.fori_loop(0, 31, body, (lo0, hi0))
    gt = bits > thr
    eq = bits == thr
    need = cap - count(gt)
    pre_eq = _dot(jnp.where(eq, 1.0, 0.0).astype(BF16), tri_ref[...])
    sel = jnp.logical_or(gt, jnp.logical_and(eq, pre_eq < need))
    slot = _dot(jnp.where(sel, 1.0, 0.0).astype(BF16), tri_ref[...])
    slot_ref[0] = jnp.where(sel, slot.astype(I32), -1)
    aff_ref[0] = aff


def _route_call(lg_t, cap):
    b, e, n_tok = lg_t.shape
    spec = pl.BlockSpec((1, e, n_tok), lambda i: (i, 0, 0))
    return pl.pallas_call(
        functools.partial(_route_kernel, cap=cap),
        grid=(b,),
        in_specs=[spec],
        out_specs=[spec, spec],
        out_shape=[jax.ShapeDtypeStruct((b, e, n_tok), I32), jax.ShapeDtypeStruct((b, e, n_tok), F32)],
        scratch_shapes=[pltpu.VMEM((n_tok, n_tok), BF16)],
        compiler_params=_params(("arbitrary",)),
        name="ec_route",
    )(lg_t)


def _gather_kernel(slot_ref, u_ref, o_ref):
    cap = o_ref.shape[1]
    n_tok = u_ref.shape[0]
    j = lax.broadcasted_iota(I32, (cap, n_tok), 0)
    onehot = jnp.where(j == slot_ref[0, 0], 1.0, 0.0).astype(BF16)
    o_ref[0] = _dot(onehot, u_ref[...]).astype(o_ref.dtype)


def _gather_call(slot4, u2, cap):
    b, e, _, n_tok = slot4.shape
    d = u2.shape[1]
    return pl.pallas_call(
        _gather_kernel,
        grid=(b, e),
        in_specs=[pl.BlockSpec((1, 1, 1, n_tok), lambda i, j: (i, j, 0, 0)),
                  pl.BlockSpec((n_tok, d), lambda i, j: (i, 0))],
        out_specs=pl.BlockSpec((1, cap, d), lambda i, j: (j, i, 0)),
        out_shape=jax.ShapeDtypeStruct((e, b * cap, d), BF16),
        compiler_params=_params(("arbitrary", "arbitrary")),
        name="moe_gather",
    )(slot4, u2)


def _expert_kernel(x_ref, w1_ref, w3_ref, w2_ref, o_ref, hd_ref, w2b_ref, *, nf, ff):
    f = pl.program_id(1)
    tf = w1_ref.shape[3]
    h = _dot(x_ref[0], jnp.concatenate([w1_ref[0, 0].astype(BF16), w3_ref[0, 0].astype(BF16)], axis=1))
    h1 = h[:, :tf]
    hd = ((h1 / (1.0 + jnp.exp(-h1))) * h[:, tf:]).astype(BF16)
    w2 = w2_ref[0, 0].astype(BF16)
    for s in range(nf):
        @pl.when(f == s)
        def _(s=s):
            hd_ref[:, s * tf:(s + 1) * tf] = hd
            w2b_ref[s * tf:(s + 1) * tf, :] = w2

    @pl.when(f == nf - 1)
    def _():
        nb, cap = o_ref.shape[0], o_ref.shape[1]
        rows = 2 * cap
        for r0 in range(0, nb * cap, rows):
            y = _dot(hd_ref[r0:r0 + rows, :ff], w2b_ref[:ff, :]).astype(o_ref.dtype)
            for b in range(rows // cap):
                o_ref[r0 // cap + b] = y[b * cap:(b + 1) * cap]


def _expert_call(xe, w1, w3, w2, layer, batch, tf=MXU_DIM):
    e, m, d = xe.shape
    cap = m // batch
    ff = w1.shape[3]
    nf = pl.cdiv(ff, tf)
    return pl.pallas_call(
        functools.partial(_expert_kernel, nf=nf, ff=ff),
        grid=(e, nf),
        in_specs=[pl.BlockSpec((1, m, d), lambda i, f: (i, 0, 0)),
                  pl.BlockSpec((1, 1, d, tf), lambda i, f: (layer, i, 0, f)),
                  pl.BlockSpec((1, 1, d, tf), lambda i, f: (layer, i, 0, f)),
                  pl.BlockSpec((1, 1, tf, d), lambda i, f: (layer, i, f, 0))],
        out_specs=pl.BlockSpec((batch, cap, d), lambda i, f: (0, i, 0)),
        out_shape=jax.ShapeDtypeStruct((batch, e * cap, d), BF16),
        scratch_shapes=[pltpu.VMEM((m, nf * tf), BF16),
                        pltpu.VMEM((nf * tf, d), BF16)],
        compiler_params=_params(("arbitrary", "arbitrary")),
        name="moe_experts",
    )(xe, w1, w3, w2)


def _scatter_kernel(slot_ref, aff_ref, ye_ref, h_ref, mod_ref, ln_ref, *rest, with_next):
    if with_next:
        nmod_ref, h2_ref, un_ref, pt_ref = rest
    else:
        h2_ref, pt_ref = rest
    n_e = slot_ref.shape[1]
    cap = ye_ref.shape[1] // n_e
    tt = h_ref.shape[0]
    j = lax.broadcasted_iota(I32, (cap, tt), 0)
    for e in range(n_e):
        pt_ref[e * cap:(e + 1) * cap, :] = jnp.where(j == slot_ref[0, e:e + 1, :],
                                                     aff_ref[0, e:e + 1, :], 0.0).astype(BF16)
    moe = _dot_tn(pt_ref[...], ye_ref[0])
    x = DEEPNORM_ALPHA * h_ref[...] + mod_ref[0] * moe
    hn = _layer_norm(x, ln_ref[0:1, :], ln_ref[1:2, :])
    h2_ref[...] = hn
    if with_next:
        un_ref[...] = (hn * (1.0 + nmod_ref[0, 1:2, :]) + nmod_ref[0, 0:1, :]).astype(un_ref.dtype)


def _scatter_call(slot, aff, ye, h1, gate, ln, next_ss, tt=512):
    b, e, n_tok = slot.shape
    m, d = h1.shape
    nt = n_tok // tt
    rows = ye.shape[1]
    with_next = next_ss is not None
    tok_spec = pl.BlockSpec((1, e, tt), lambda i, t: (i, 0, t))
    row_spec = pl.BlockSpec((tt, d), lambda i, t: (i * nt + t, 0))
    in_specs = [tok_spec, tok_spec,
                pl.BlockSpec((1, rows, d), lambda i, t: (i, 0, 0), pipeline_mode=pl.Buffered(1)),
                row_spec,
                pl.BlockSpec((1, 1, d), lambda i, t: (i, 0, 0)),
                pl.BlockSpec((2, d), lambda i, t: (0, 0))]
    args = [slot, aff, ye, h1, gate, ln]
    out_specs = [row_spec]
    out_shape = [jax.ShapeDtypeStruct((m, d), F32)]
    if with_next:
        in_specs.append(pl.BlockSpec((1, 2, d), lambda i, t: (i, 0, 0)))
        args.append(next_ss)
        out_specs.append(row_spec)
        out_shape.append(jax.ShapeDtypeStruct((m, d), BF16))
    res = pl.pallas_call(
        functools.partial(_scatter_kernel, with_next=with_next),
        grid=(b, nt),
        in_specs=in_specs,
        out_specs=out_specs,
        out_shape=out_shape,
        scratch_shapes=[pltpu.VMEM((rows, tt), BF16)],
        compiler_params=_params(("arbitrary", "arbitrary")),
        name="moe_combine_postnorm",
    )(*args)
    return res if with_next else (res[0], None)


def _ec_moe(h1, u2, logits, w1, w3, w2, layer, gate, ln, next_ss, batch, n_tok):
    cap = EC_CAPACITY_FACTOR * n_tok // N_EXPERTS
    lg_t = jnp.swapaxes(logits[:, :N_EXPERTS].reshape(batch, n_tok, N_EXPERTS), 1, 2)
    slot, aff = _route_call(lg_t, cap)
    xe = _gather_call(slot.reshape(batch, N_EXPERTS, 1, n_tok), u2, cap)
    ye = _expert_call(xe, w1, w3, w2, layer, batch)
    return _scatter_call(slot, aff, ye, h1, gate, ln, next_ss)


def _mm_kernel(a_ref, w_ref, o_ref):
    o_ref[...] = _dot(a_ref[...], w_ref[...]).astype(o_ref.dtype)


def _mm_call(a, w, out_dtype, tm=1024, tn=512):
    m, k = a.shape
    n = w.shape[1]
    return pl.pallas_call(
        _mm_kernel,
        grid=(m // tm, n // tn),
        in_specs=[pl.BlockSpec((tm, k), lambda i, j: (i, 0)),
                  pl.BlockSpec((k, tn), lambda i, j: (0, j))],
        out_specs=pl.BlockSpec((tm, tn), lambda i, j: (i, j)),
        out_shape=jax.ShapeDtypeStruct((m, n), out_dtype),
        compiler_params=_params(("arbitrary", "arbitrary")),
        name="matmul",
    )(a, w)


def _sconv_kernel(p_ref, w_ref, b_ref, o_ref):
    x = p_ref[...].astype(F32)
    n = x.shape[0]
    r = lax.broadcasted_iota(I32, x.shape, 0)
    prev = jnp.where(r == 0, 0.0, pltpu.roll(x, 1, 0))
    nxt = jnp.where(r == n - 1, 0.0, pltpu.roll(x, n - 1, 0))
    o_ref[...] = (b_ref[...] + w_ref[0:1, :] * prev + w_ref[1:2, :] * x + w_ref[2:3, :] * nxt
                  ).astype(o_ref.dtype)


def _sconv_call(p, conv_w, conv_b, col0, ncols, out_dtype, batch, n_tok, tc=512):
    c0 = col0 // tc
    return pl.pallas_call(
        _sconv_kernel,
        grid=(batch, ncols // tc),
        in_specs=[pl.BlockSpec((n_tok, tc), lambda b, j: (b, c0 + j)),
                  pl.BlockSpec((HY_SHORT, tc), lambda b, j: (0, c0 + j)),
                  pl.BlockSpec((1, tc), lambda b, j: (0, c0 + j))],
        out_specs=pl.BlockSpec((n_tok, tc), lambda b, j: (b, j)),
        out_shape=jax.ShapeDtypeStruct((batch * n_tok, ncols), out_dtype),
        compiler_params=_params(("arbitrary", "arbitrary")),
        name="hyena_short_conv",
    )(p, conv_w, conv_b.reshape(1, -1))


def _dft_fwd_kernel(fc_ref, fs_ref, r1_ref, r2_ref, *rest, spectral):
    xr = _dot(fc_ref[...], r1_ref[...])
    xi = _dot(fs_ref[...], r2_ref[...])
    if spectral:
        h_ref, nyq_ref, o_ref = rest
        hr = h_ref[0, 0]
        hi = h_ref[0, 1]
        dc = lax.broadcasted_iota(I32, xr.shape, 0) == 0
        yr = xr * hr - jnp.where(dc, 0.0, xi * hi)
        yi = jnp.where(dc, xi * nyq_ref[...], xr * hi + xi * hr)
    else:
        (o_ref,) = rest
        yr, yi = xr, xi
    o_ref[0, 0] = yr.astype(o_ref.dtype)
    o_ref[0, 1] = yi.astype(o_ref.dtype)


def _resident_spec(shape):
    return pl.BlockSpec(shape, lambda *_: (0,) * len(shape), pipeline_mode=pl.Buffered(1))


def _dft_fwd_call(fc, fs, r1, r2, spec, out_dtype, batch, n_tok, tn=256):
    ncols = r1.shape[1]
    r_spec = pl.BlockSpec((n_tok, tn), lambda j, b: (b, j))
    in_specs = [_resident_spec(fc.shape), _resident_spec(fs.shape), r_spec, r_spec]
    args = [fc, fs, r1, r2]
    if spec is not None:
        h, nyq, hcol0 = spec
        c0 = hcol0 // tn
        in_specs += [pl.BlockSpec((1, 2, n_tok, tn), lambda j, b: (0, 0, 0, c0 + j)),
                     pl.BlockSpec((1, tn), lambda j, b: (0, c0 + j))]
        args += [h, nyq]
    return pl.pallas_call(
        functools.partial(_dft_fwd_kernel, spectral=spec is not None),
        grid=(ncols // tn, batch),
        in_specs=in_specs,
        out_specs=pl.BlockSpec((1, 2, n_tok, tn), lambda j, b: (b, 0, 0, j)),
        out_shape=jax.ShapeDtypeStruct((batch, 2, n_tok, ncols), out_dtype),
        compiler_params=_params(("arbitrary", "arbitrary")),
        name="dft_forward",
    )(*args)


def _dual_kernel(a1_ref, a2_ref, b1_ref, b2_ref, *rest, hyena, planes):
    b1 = b1_ref[0, 0] if planes else b1_ref[...]
    b2 = b2_ref[0, 0] if planes else b2_ref[...]
    y = _dot(a1_ref[...], b1) + _dot(a2_ref[...], b2)
    if hyena:
        s_ref, z_ref, skip_ref, o_ref = rest
        y = s_ref[...].astype(F32) * (y + skip_ref[...] * z_ref[...].astype(F32))
    else:
        (o_ref,) = rest
    o_ref[...] = y.astype(o_ref.dtype)


def _dual_call(a1, a2, b1, b2, hy, batch, n_tok, ncols, tn=256):
    planes = b1.ndim == 4
    if planes:
        b_specs = [pl.BlockSpec((1, 1, n_tok, tn), lambda b, j: (b, 0, 0, j)),
                   pl.BlockSpec((1, 1, n_tok, tn), lambda b, j: (b, 1, 0, j))]
    else:
        b_specs = [pl.BlockSpec((n_tok, tn), lambda b, j: (b, j))] * 2
    in_specs = [_resident_spec(a1.shape), _resident_spec(a2.shape)] + b_specs
    args = [a1, a2, b1, b2]
    o_spec = pl.BlockSpec((n_tok, tn), lambda b, j: (b, j))
    if hy is not None:
        s, scol0, z, skip = hy
        c0 = scol0 // tn
        in_specs += [pl.BlockSpec((n_tok, tn), lambda b, j: (b, c0 + j)),
                     o_spec,
                     pl.BlockSpec((1, tn), lambda b, j: (0, j))]
        args += [s, z, skip]
    return pl.pallas_call(
        functools.partial(_dual_kernel, hyena=hy is not None, planes=planes),
        grid=(batch, ncols // tn),
        in_specs=in_specs,
        out_specs=o_spec,
        out_shape=jax.ShapeDtypeStruct((batch * n_tok, ncols), BF16),
        compiler_params=_params(("arbitrary", "arbitrary")),
        name="seq_mix_matmul",
    )(*args)


def _fnet_ch_kernel(p_ref, w_ref, gc_ref, gs_ref):
    r = _dot(p_ref[...].astype(BF16), w_ref[...])
    w = gc_ref.shape[1]
    gc_ref[...] = r[:, :w].astype(gc_ref.dtype)
    gs_ref[...] = r[:, w:].astype(gs_ref.dtype)


def _fnet_ch_call(p, w, col0, tm=1024):
    m = p.shape[0]
    gw = w.shape[0]
    c0 = col0 // gw
    o_spec = pl.BlockSpec((tm, gw), lambda i, g: (i, g))
    return pl.pallas_call(
        _fnet_ch_kernel,
        grid=(m // tm, FN_GROUPS),
        in_specs=[pl.BlockSpec((tm, gw), lambda i, g: (i, c0 + g)),
                  pl.BlockSpec((gw, 2 * gw), lambda i, g: (0, 0))],
        out_specs=[o_spec, o_spec],
        out_shape=[jax.ShapeDtypeStruct((m, FN_GROUPS * gw), BF16)] * 2,
        compiler_params=_params(("arbitrary", "arbitrary")),
        name="fnet_channel_dft",
    )(p, w)


def _cos_sin_matrix(n_rows, n_cols, period, split=64):
    r = np.arange(n_rows, dtype=np.int64)[:, None]
    c_hi = (np.arange(n_cols // split, dtype=np.int64) * split)[None, :]
    c_lo = np.arange(split, dtype=np.int64)[None, :]
    ang = lambda c: 2.0 * np.pi * ((r * c) % period).astype(np.float64) / period
    ca, sa = jnp.asarray(np.cos(ang(c_hi)), F32), jnp.asarray(np.sin(ang(c_hi)), F32)
    cb, sb = jnp.asarray(np.cos(ang(c_lo)), F32), jnp.asarray(np.sin(ang(c_lo)), F32)
    cos = ca[:, :, None] * cb[:, None, :] - sa[:, :, None] * sb[:, None, :]
    sin = sa[:, :, None] * cb[:, None, :] + ca[:, :, None] * sb[:, None, :]
    return cos.reshape(n_rows, n_cols), sin.reshape(n_rows, n_cols)


def _trig_kernel(ta_ref, tb_ref, ea_ref, eb_ref, *o_refs, mode, period, scale):
    sa_ca = _dot(ta_ref[...], ea_ref[...])
    sb_cb = _dot(tb_ref[...], eb_ref[...])
    nc = sa_ca.shape[1] // 2
    ca, sa, cb, sb = sa_ca[:, :nc], sa_ca[:, nc:], sb_cb[:, :nc], sb_cb[:, nc:]
    cos = ca * cb - sa * sb
    sin = sa * cb + ca * sb
    tm = cos.shape[0]
    row = pl.program_id(0) * tm + lax.broadcasted_iota(I32, cos.shape, 0)
    col = lax.broadcasted_iota(I32, cos.shape, 1)
    alt = lambda idx: (1 - 2 * jnp.bitwise_and(idx, 1)).astype(F32)
    if mode == "hyena":
        fc_ref, fs_ref, gc_ref, gs_ref = o_refs
        fc_ref[...] = cos.astype(fc_ref.dtype)
        fs_ref[...] = jnp.where(row == 0, alt(col), -sin).astype(fs_ref.dtype)
        gc_ref[...] = (cos * jnp.where(col == 0, 1.0 / period, 2.0 / period)).astype(gc_ref.dtype)
        gs_ref[...] = jnp.where(col == 0, alt(row) * (1.0 / period), sin * (-2.0 / period)).astype(gs_ref.dtype)
    else:
        c_ref, ms_ref = o_refs
        c_ref[...] = (cos * scale).astype(c_ref.dtype)
        ms_ref[...] = (sin * (-scale)).astype(ms_ref.dtype)


def _split3(x):
    bf = jnp.bfloat16
    x = np.asarray(x, np.float32)
    hi = x.astype(bf)
    r1 = x - hi.astype(np.float32)
    mid = r1.astype(bf)
    lo = (r1 - mid.astype(np.float32)).astype(bf)
    return [hi, mid, lo]


def _trig_call(n, period, mode, scale=1.0, split=64, tm=256):
    r = np.arange(n, dtype=np.int64)[:, None]
    c_hi = (np.arange(n // split, dtype=np.int64) * split)[None, :]
    c_lo = np.arange(split, dtype=np.int64)[None, :]
    ang = lambda c: 2.0 * np.pi * ((r * c) % period).astype(np.float64) / period
    def operands(a, col_group):
        terms = _split3(np.cos(a)) + _split3(np.sin(a))
        k = a.shape[1]
        spread = (col_group[None, :] == np.arange(k)[:, None]).astype(np.float32)
        zero = np.zeros_like(spread)
        e = np.concatenate([np.concatenate([spread, zero], 1)] * 3 + [np.concatenate([zero, spread], 1)] * 3, 0)
        return jnp.asarray(np.concatenate(terms, axis=1)), jnp.asarray(e, BF16)
    cols = np.arange(n)
    ta, ea = operands(ang(c_hi), cols // split)
    tb, eb = operands(ang(c_lo), cols % split)
    n_out = 4 if mode == "hyena" else 2
    o_spec = pl.BlockSpec((tm, n), lambda i: (i, 0))
    return pl.pallas_call(
        functools.partial(_trig_kernel, mode=mode, period=period, scale=scale),
        grid=(n // tm,),
        in_specs=[pl.BlockSpec((tm, ta.shape[1]), lambda i: (i, 0)),
                  pl.BlockSpec((tm, tb.shape[1]), lambda i: (i, 0)),
                  pl.BlockSpec(ea.shape, lambda i: (0, 0)),
                  pl.BlockSpec(eb.shape, lambda i: (0, 0))],
        out_specs=[o_spec] * n_out,
        out_shape=[jax.ShapeDtypeStruct((n, n), BF16)] * n_out,
        compiler_params=_params(("arbitrary",)),
        name="dft_tables",
    )(ta, tb, ea, eb)


def _hyena_dft_operands(n_tok):
    fc, fs, gc, gs = _trig_call(n_tok, 2 * n_tok, "hyena")
    alt = jnp.asarray(((-1.0) ** np.arange(n_tok)), F32)
    return fc, fs, gc, gs, alt


def _fnet_operands(n_tok, gw):
    cl, msl = _trig_call(n_tok, n_tok, "fnet", scale=1.0 / math.sqrt(n_tok * gw))
    cw, sw = _cos_sin_matrix(gw, gw, gw)
    return cl, msl, jnp.concatenate([cw, sw], axis=1).astype(BF16)


def _hyena_filters(n_tok, fw1, fb1, ff1, fw2, fb2, ff2, fw3):
    t01 = jnp.linspace(0.0, 1.0, n_tok, dtype=F32)
    w = 2.0 * math.pi * jnp.arange(n_tok, dtype=F32) / n_tok
    bands = jnp.linspace(1e-4, HY_BANDS - 1, HY_BANDS, dtype=F32)
    z = jnp.concatenate([t01[:, None], jnp.cos(w[:, None] * bands), -jnp.sin(w[:, None] * bands)], -1)
    hp = lax.Precision.HIGHEST
    hdn = jnp.sin(ff1 * (jnp.dot(z, fw1, precision=hp) + fb1))
    hdn = jnp.sin(ff2 * (jnp.dot(hdn, fw2, precision=hp) + fb2))
    filt = jnp.dot(hdn, fw3, precision=hp)
    deltas = jnp.abs(jnp.linspace(HY_MIN_DECAY, HY_MAX_DECAY, HY_W, dtype=F32))
    decay = jnp.exp(-t01[:, None] * deltas)
    return filt.reshape(n_tok, HY_ORDER, 2, HY_W) * decay[:, None, None, :]


def kernel(x, c, ctx, c_ctx, ada_w, ada_b, ln1_g, ln1_b, ln2_g, ln2_b, router, exp_w1, exp_w3, exp_w2,
           ab_w_in, ab_kv_norm, ab_w_ukv, ab_rpb, ab_w_out,
           cd_w_in, cd_conv_w, cd_conv_b, cd_filt_w1, cd_filt_b1, cd_filt_freq1, cd_filt_w2, cd_filt_b2,
           cd_filt_freq2, cd_filt_w3, cd_skip, cd_w_out):
    batch, n_tok, d = x.shape
    lc = ctx.shape[1]
    x2d = x.reshape(batch * n_tok, d)
    ctx2d = ctx.reshape(batch * lc, d)

    pad_rows = (-(batch + 1)) % 8
    cv = jnp.concatenate([c, c_ctx[None, :], jnp.zeros((pad_rows, d), F32)], axis=0)
    ada = _ada_call(cv, ada_w, ada_b)
    mods = [ada[i, :batch].reshape(batch, 6, d) for i in range(DEPTH)]
    ctx_ss = ada[0, batch, :2 * d].reshape(1, 2, d)

    u_lat = _modulate_call(x2d, mods[0][:, 0:2], n_tok)
    u_ctx = _modulate_call(ctx2d, ctx_ss, batch * lc)
    w_in = _w_in_prep_call(ab_w_in, 0)
    p_lat = _mm_rope_call(u_lat, w_in, _rope_tables(n_tok), tm=1024)
    p_ctx = _mm_rope_call(u_ctx, w_in, _identity_rope_tables(lc), tm=lc)
    w_ukv = _permute_w_ukv(ab_w_ukv[0])
    kv_lat = _kvup_call(p_lat, ab_kv_norm[0], w_ukv, tm=512)
    kv_ctx = _kvup_call(p_ctx, ab_kv_norm[0], w_ukv, tm=lc)
    a_out = _mla_call(p_lat, kv_lat, p_ctx, kv_ctx, batch, n_tok, lc)
    var_map, na_er, na_ec = _na_tables(n_tok // GRID_W)
    b_out = _na_call(p_lat, p_ctx, _na_bias(ab_rpb[0], na_er, na_ec), jnp.asarray(var_map),
                     batch, n_tok, lc)
    ln1 = jnp.stack([ln1_g, ln1_b], axis=1)
    ln2 = jnp.stack([ln2_g, ln2_b], axis=1)
    h1, u2, logits = _outln_call(a_out, b_out, ab_w_out[0].astype(BF16), x2d,
                                 mods[0][:, 2:5], ln1[0], _router_operand(router[0]), n_tok)
    h, u = _ec_moe(h1, u2, logits, exp_w1, exp_w3, exp_w2, 0, mods[0][:, 5:6], ln2[0],
                   mods[1][:, 0:2], batch, n_tok)

    p1 = _mm_call(u, cd_w_in[0].astype(BF16), BF16, tn=1024)
    s0 = _sconv_call(p1, cd_conv_w[0], cd_conv_b[0], 0, HY_W, BF16, batch, n_tok)
    s12 = _sconv_call(p1, cd_conv_w[0], cd_conv_b[0], HY_W, HY_ORDER * HY_W, F32, batch, n_tok)
    filt = _hyena_filters(n_tok, cd_filt_w1[0], cd_filt_b1[0], cd_filt_freq1[0], cd_filt_w2[0],
                          cd_filt_b2[0], cd_filt_freq2[0], cd_filt_w3[0])
    hf = filt[:, :, 0, :].reshape(n_tok, HY_ORDER * HY_W)
    hb = jnp.where((jnp.arange(n_tok) == 0)[:, None], 0.0, filt[:, :, 1, :].reshape(n_tok, HY_ORDER * HY_W))
    fc, fs, gc, gs, alt = _hyena_dft_operands(n_tok)
    h_sum = hf + hb
    h_nyq = jnp.sum(alt[:, None] * h_sum, axis=0, keepdims=True)
    h_spec = _dft_fwd_call(fc, fs, h_sum.astype(BF16), (hf - hb).astype(BF16), None, F32, 1, n_tok)
    z = s0
    for o in range(HY_ORDER):
        y_spec = _dft_fwd_call(fc, fs, z, z, (h_spec, h_nyq, o * HY_W), BF16, batch, n_tok)
        z = _dual_call(gc, gs, y_spec, y_spec, (s12, o * HY_W, z, cd_skip[0][o:o + 1]), batch, n_tok, HY_W)
    cl, msl, cw = _fnet_operands(n_tok, FN_GROUP_W)
    g_cos, g_sin = _fnet_ch_call(p1, cw, HY_IN_W)
    y_fn = _dual_call(cl, msl, g_cos, g_sin, None, batch, n_tok, FN_W)
    h1, u2, logits = _outln_call(z, y_fn, cd_w_out[0].astype(BF16), h,
                                 mods[1][:, 2:5], ln1[1], _router_operand(router[1]), n_tok)
    h, _ = _ec_moe(h1, u2, logits, exp_w1, exp_w3, exp_w2, 1, mods[1][:, 5:6], ln2[1],
                   None, batch, n_tok)
    return h.reshape(batch, n_tok, d)
```

```python
import functools
import math

import numpy as np
import jax
import jax.numpy as jnp
from jax import lax
from jax.experimental import pallas as pl
from jax.experimental.pallas import tpu as pltpu

F32 = jnp.float32
BF16 = jnp.bfloat16
I32 = jnp.int32

D_MODEL = 2048
BATCH = 4
SEQ = 2048
DEPTH = 2
CTX_LEN = 256
GRID_W = 64
DEEPNORM_ALPHA = (2.0 * DEPTH) ** 0.25
LN_EPS = 1e-6
NEG_INF = -1e30

MLA_HEADS = 8
MLA_NOPE = 128
MLA_ROPE = 64
MLA_QK = MLA_NOPE + MLA_ROPE
MLA_V = 128
MLA_KV_RANK = 512
ROPE_THETA = 10000.0

NA_HEADS = 8
NA_HEAD_DIM = 128
NA_KH = 8
NA_KW = 16

OFF_CKV = MLA_HEADS * MLA_QK
OFF_KPE = OFF_CKV + MLA_KV_RANK
OFF_QNA = OFF_KPE + MLA_ROPE
OFF_KNA = OFF_QNA + NA_HEADS * NA_HEAD_DIM
OFF_VNA = OFF_KNA + NA_HEADS * NA_HEAD_DIM

HY_W = 1024
HY_ORDER = 2
HY_IN_W = (HY_ORDER + 1) * HY_W
HY_SHORT = 3
HY_BANDS = 16
HY_DECAY_TARGET = 1e-2
HY_MIN_DECAY = math.log(HY_DECAY_TARGET) / 1.5
HY_MAX_DECAY = math.log(HY_DECAY_TARGET) / 0.3

FN_W = 1024
FN_GROUPS = 4
FN_GROUP_W = FN_W // FN_GROUPS

N_EXPERTS = 16
EC_CAPACITY_FACTOR = 2
EXPERT_FF = 1408

LANES = 128
MXU_DIM = 256
VMEM_LIMIT = 56 * 1024 * 1024

QTILE_W = 2 * LANES
T_KPE = MLA_HEADS
T_CKV = T_KPE + 2
T_QNA = T_CKV + MLA_KV_RANK // QTILE_W
T_KNA = T_QNA + NA_HEADS * NA_HEAD_DIM // QTILE_W
T_VNA = T_KNA + NA_HEADS * NA_HEAD_DIM // QTILE_W
N_ABTILES = T_VNA + NA_HEADS * NA_HEAD_DIM // QTILE_W
AB_PERM_W = N_ABTILES * QTILE_W

MLA_SUBTILES = 2
OUTLN_SUBTILES = 2
NA_G = 4
NA_WR = NA_KH + NA_G - 1
NA_GROUPS_PER_STEP = 2


def _params(sem, vmem=VMEM_LIMIT):
    return pltpu.CompilerParams(dimension_semantics=sem, vmem_limit_bytes=vmem)


def _dot(a, b):
    return jnp.dot(a, b, preferred_element_type=F32)


def _dot_nt(a, b):
    return lax.dot_general(a, b, (((1,), (1,)), ((), ())), preferred_element_type=F32)


def _dot_tn(a, b):
    return lax.dot_general(a, b, (((0,), (0,)), ((), ())), preferred_element_type=F32)


def _split_bf16(x):
    hi = x.astype(BF16)
    lo = (x - hi.astype(F32)).astype(BF16)
    return hi, lo


def _ada_kernel(c_ref, w_ref, b_ref, o_ref):
    c = c_ref[...]
    a = c / (1.0 + jnp.exp(-c))
    a_hi, a_lo = _split_bf16(a)
    w_hi, w_lo = _split_bf16(w_ref[0])
    rows = a.shape[0]
    r = _dot(jnp.concatenate([a_hi, a_lo], axis=0), w_hi)
    o_ref[0] = r[:rows] + r[rows:] + _dot(a_hi, w_lo) + b_ref[0]


def _ada_call(cv, ada_w, ada_b, tn=1024):
    depth, d, n = ada_w.shape
    rows = cv.shape[0]
    return pl.pallas_call(
        _ada_kernel,
        grid=(depth, n // tn),
        in_specs=[pl.BlockSpec((rows, d), lambda l, j: (0, 0)),
                  pl.BlockSpec((1, d, tn), lambda l, j: (l, 0, j)),
                  pl.BlockSpec((1, 1, tn), lambda l, j: (l, 0, j))],
        out_specs=pl.BlockSpec((1, rows, tn), lambda l, j: (l, 0, j)),
        out_shape=jax.ShapeDtypeStruct((depth, rows, n), F32),
        compiler_params=_params(("arbitrary", "arbitrary")),
        name="ada_params",
    )(cv, ada_w, ada_b.reshape(depth, 1, n))


def _mm_rope_kernel(x_ref, ss_ref, w_ref, ta_ref, tb_ref, tc_ref, o_ref, a_ref):
    @pl.when(pl.program_id(1) == 0)
    def _():
        a_ref[...] = (x_ref[...] * (1.0 + ss_ref[0, 1:2, :]) + ss_ref[0, 0:1, :]).astype(a_ref.dtype)

    a = a_ref[...]
    for half in range(2):
        t = 2 * pl.program_id(1) + half
        c0 = half * QTILE_W
        acc = _dot_nt(a, w_ref[c0:c0 + QTILE_W, :])
        is_rope = t <= T_KPE
        f = jnp.where(t < T_KPE, MLA_QK ** -0.5,
                      jnp.where(jnp.logical_and(t >= T_QNA, t < T_KNA), NA_HEAD_DIM ** -0.5, 1.0)).astype(F32)
        hi = acc[:, LANES:]
        rot = (hi * jnp.where(is_rope, ta_ref[...], 1.0)
               + pltpu.roll(hi, LANES - MLA_ROPE // 2, 1) * jnp.where(is_rope, tb_ref[...], 0.0)
               + pltpu.roll(hi, MLA_ROPE // 2, 1) * jnp.where(is_rope, tc_ref[...], 0.0))
        o_ref[:, c0:c0 + LANES] = (acc[:, :LANES] * f).astype(o_ref.dtype)
        o_ref[:, c0 + LANES:c0 + QTILE_W] = (rot * f).astype(o_ref.dtype)


def _mm_rope_call(x, ss, rows_per_group, w, tabs, tm):
    m, k = x.shape
    n = w.shape[0]
    tn = 2 * QTILE_W
    ntab = tabs[0].shape[0] // tm
    tpg = rows_per_group // tm
    tab_spec = pl.BlockSpec((tm, LANES), lambda i, j: (i % ntab, 0))
    return pl.pallas_call(
        _mm_rope_kernel,
        grid=(m // tm, n // tn),
        in_specs=[pl.BlockSpec((tm, k), lambda i, j: (i, 0)),
                  pl.BlockSpec((1, 2, k), lambda i, j: (i // tpg, 0, 0)),
                  pl.BlockSpec((tn, k), lambda i, j: (j, 0)),
                  tab_spec, tab_spec, tab_spec],
        out_specs=pl.BlockSpec((tm, tn), lambda i, j: (i, j)),
        out_shape=jax.ShapeDtypeStruct((m, n), BF16),
        scratch_shapes=[pltpu.VMEM((tm, k), BF16)],
        compiler_params=_params(("arbitrary", "arbitrary")),
        name="attn_in_proj",
    )(x, ss, w, *tabs)


def _rope_tables(n_tok):
    t = np.arange(n_tok)
    row = (t // GRID_W).astype(np.float32)
    col = (t % GRID_W).astype(np.float32)
    n_freq = MLA_ROPE // 4
    inv = (ROPE_THETA ** (-np.arange(n_freq, dtype=np.float32) / n_freq)).astype(np.float32)
    ang = np.concatenate([row[:, None] * inv, col[:, None] * inv], axis=1)
    cos, sin = np.cos(ang).astype(np.float32), np.sin(ang).astype(np.float32)
    half = MLA_ROPE // 2
    ta = np.zeros((n_tok, LANES), np.float32)
    tb = np.zeros((n_tok, LANES), np.float32)
    tc = np.zeros((n_tok, LANES), np.float32)
    ta[:, :half] = cos
    ta[:, half:2 * half] = cos
    tb[:, :half] = -sin
    tc[:, half:2 * half] = sin
    return jnp.asarray(ta), jnp.asarray(tb), jnp.asarray(tc)


def _identity_rope_tables(n_tok):
    ta = np.zeros((n_tok, LANES), np.float32)
    ta[:, :MLA_ROPE] = 1.0
    z = np.zeros((n_tok, LANES), np.float32)
    return jnp.asarray(ta), jnp.asarray(z), jnp.asarray(z)


def _w_in_prep_kernel(w_ref, o_ref):
    x = w_ref[0]
    zeros = lambda n: jnp.zeros((n, x.shape[1]), x.dtype)
    rows = []
    for h in range(MLA_HEADS):
        rows += [x[h * MLA_QK:(h + 1) * MLA_QK, :], zeros(QTILE_W - MLA_QK)]
    rows += [zeros(LANES), x[OFF_KPE:OFF_QNA, :], zeros(LANES - MLA_ROPE), zeros(QTILE_W),
             x[OFF_CKV:OFF_KPE, :], x[OFF_QNA:, :]]
    o_ref[...] = jnp.concatenate(rows, axis=0).astype(o_ref.dtype)


def _w_in_prep_call(w_in_t, layer, tk=256):
    _, n, d = w_in_t.shape
    return pl.pallas_call(
        _w_in_prep_kernel,
        grid=(d // tk,),
        in_specs=[pl.BlockSpec((1, n, tk), lambda i: (layer, 0, i))],
        out_specs=pl.BlockSpec((AB_PERM_W, tk), lambda i: (0, i)),
        out_shape=jax.ShapeDtypeStruct((AB_PERM_W, d), BF16),
        compiler_params=_params(("arbitrary",)),
        name="attn_w_in_prep",
    )(w_in_t)


def _kvup_kernel(x_ref, g_ref, w_ref, o_ref):
    x = x_ref[...].astype(F32)
    y = x * lax.rsqrt(jnp.mean(x * x, axis=-1, keepdims=True) + LN_EPS) * g_ref[...]
    o_ref[...] = _dot(y.astype(BF16), w_ref[...]).astype(o_ref.dtype)


def _kvup_call(p, g, w, tm):
    m = p.shape[0]
    r, n = w.shape
    return pl.pallas_call(
        _kvup_kernel,
        grid=(m // tm,),
        in_specs=[pl.BlockSpec((tm, r), lambda i: (i, T_CKV * QTILE_W // r)),
                  pl.BlockSpec((1, r), lambda i: (0, 0)),
                  pl.BlockSpec((r, n), lambda i: (0, 0))],
        out_specs=pl.BlockSpec((tm, n), lambda i: (i, 0)),
        out_shape=jax.ShapeDtypeStruct((m, n), BF16),
        compiler_params=_params(("arbitrary",)),
        name="kv_up",
    )(p, g.reshape(1, r), w)


def _permute_w_ukv(w):
    r = w.shape[0]
    w3 = w.reshape(r, MLA_HEADS, MLA_NOPE + MLA_V)
    return jnp.concatenate([w3[:, :, :MLA_NOPE].reshape(r, -1), w3[:, :, MLA_NOPE:].reshape(r, -1)],
                           axis=1).astype(BF16)


def _mla_kernel(q_ref, kn_ref, kp_ref, v_ref, kcn_ref, kcp_ref, vc_ref, o_ref, kcat, vcat):
    s_len = kn_ref.shape[0]

    @pl.when(pl.program_id(2) == 0)
    def _():
        kcat[:s_len, :LANES] = kn_ref[...]
        kcat[:s_len, LANES:] = kp_ref[...]
        kcat[s_len:, :LANES] = kcn_ref[...]
        kcat[s_len:, LANES:] = kcp_ref[...]
        vcat[:s_len, :] = v_ref[...]
        vcat[s_len:, :] = vc_ref[...]

    sub = q_ref.shape[0] // MLA_SUBTILES
    for r0 in range(0, q_ref.shape[0], sub):
        s = _dot_nt(q_ref[r0:r0 + sub, :], kcat[...])
        m = jnp.max(s, axis=-1, keepdims=True)
        p = jnp.exp(s - m)
        l = jnp.sum(p, axis=-1, keepdims=True)
        o = _dot(p.astype(BF16), vcat[...])
        o_ref[r0:r0 + sub, :] = (o / l).astype(o_ref.dtype)


def _mla_call(p_lat, kv_lat, p_ctx, kv_ctx, batch, s_len, lc, tq=512):
    nq = s_len // tq
    kpe_blk = T_KPE * 2 + 1
    return pl.pallas_call(
        _mla_kernel,
        grid=(batch, MLA_HEADS, nq),
        in_specs=[pl.BlockSpec((tq, QTILE_W), lambda b, h, i: (b * nq + i, h)),
                  pl.BlockSpec((s_len, LANES), lambda b, h, i: (b, h)),
                  pl.BlockSpec((s_len, LANES), lambda b, h, i: (b, kpe_blk)),
                  pl.BlockSpec((s_len, LANES), lambda b, h, i: (b, MLA_HEADS + h)),
                  pl.BlockSpec((lc, LANES), lambda b, h, i: (b, h)),
                  pl.BlockSpec((lc, LANES), lambda b, h, i: (b, kpe_blk)),
                  pl.BlockSpec((lc, LANES), lambda b, h, i: (b, MLA_HEADS + h))],
        out_specs=pl.BlockSpec((tq, LANES), lambda b, h, i: (b * nq + i, h)),
        out_shape=jax.ShapeDtypeStruct((batch * s_len, MLA_HEADS * MLA_V), BF16),
        scratch_shapes=[pltpu.VMEM((s_len + lc, QTILE_W), BF16),
                        pltpu.VMEM((s_len + lc, LANES), BF16)],
        compiler_params=_params(("arbitrary", "arbitrary", "arbitrary")),
        name="mla_attention",
    )(p_lat, kv_lat, p_lat, kv_lat, kv_ctx, p_ctx, kv_ctx)


def _na_kernel(var_ref, q_ref, k_ref, v_ref, kc_ref, vc_ref, *rest, rows):
    del var_ref
    bias_refs, o_ref = rest[:-1], rest[-1]
    win = NA_WR * GRID_W
    tq = NA_G * GRID_W
    for i, bias_ref in enumerate(bias_refs):
        g = len(bias_refs) * pl.program_id(2) + i
        start_row = jnp.clip(NA_G * g - NA_KH // 2, 0, rows - NA_WR)
        start = pl.multiple_of(start_row * GRID_W, GRID_W)
        q = q_ref[i * tq:(i + 1) * tq, :]
        sw = _dot_nt(q, k_ref[pl.ds(start, win), :]) + bias_ref[0, 0]
        sc = _dot_nt(q, kc_ref[...])
        m = jnp.maximum(jnp.max(sw, axis=-1, keepdims=True), jnp.max(sc, axis=-1, keepdims=True))
        pw = jnp.exp(sw - m)
        pc = jnp.exp(sc - m)
        l = jnp.sum(pw, axis=-1, keepdims=True) + jnp.sum(pc, axis=-1, keepdims=True)
        o = _dot(pw.astype(BF16), v_ref[pl.ds(start, win), :]) + _dot(pc.astype(BF16), vc_ref[...])
        o_ref[i * tq:(i + 1) * tq, :] = (o / l).astype(o_ref.dtype)


def _na_tables(rows):
    kh = min(NA_KH, rows)
    ng = rows // NA_G
    qr = np.arange(NA_G)[:, None]
    kr = np.arange(NA_WR)[None, :]
    sel_rows = []
    for g in range(ng):
        start_row = int(np.clip(NA_G * g - NA_KH // 2, 0, rows - NA_WR))
        r = NA_G * g + qr
        krow = start_row + kr
        rs = np.clip(r - kh // 2, 0, rows - kh)
        ok = (krow >= rs) & (krow < rs + kh)
        dr = krow - r + (NA_KH - 1)
        sel_rows.append(np.stack([ok & (dr == d) for d in range(2 * NA_KH - 1)]))
    sel_rows = np.stack(sel_rows)
    uniq, inverse = np.unique(sel_rows.reshape(ng, -1), axis=0, return_inverse=True)
    er = uniq.reshape((-1,) + sel_rows.shape[1:]).astype(np.float32)
    qc = np.arange(GRID_W)[:, None]
    kc = np.arange(GRID_W)[None, :]
    cs = np.clip(qc - NA_KW // 2, 0, GRID_W - NA_KW)
    ok_c = (kc >= cs) & (kc < cs + NA_KW)
    dc = np.clip(kc - qc, -(NA_KW - 1), NA_KW - 1) + (NA_KW - 1)
    ec = np.stack([ok_c & (dc == e) for e in range(2 * NA_KW - 1)]).astype(np.float32)
    return inverse.reshape(-1).astype(np.int32), er, ec


def _na_bias(rpb, er, ec):
    hp = lax.Precision.HIGHEST
    t = jnp.einsum('hde,eqk->hdqk', rpb.astype(F32), jnp.asarray(ec), precision=hp)
    t = jnp.where(jnp.asarray(ec.sum(axis=0) > 0.5), t, NEG_INF)
    outside = jnp.full(t[:, 0].shape, NEG_INF, F32)
    d_of = np.where(er.sum(axis=1) > 0.5, er.argmax(axis=1), -1)
    block = lambda d: outside if d < 0 else t[:, d]
    return jnp.stack([
        jnp.concatenate([jnp.concatenate([block(int(d)) for d in row], axis=-1) for row in var], axis=-2)
        for var in d_of])


def _na_call(p_lat, p_ctx, bias, var_map, batch, s_len, lc):
    rows = s_len // GRID_W
    gps = NA_GROUPS_PER_STEP
    ns = rows // (NA_G * gps)
    tq = NA_G * GRID_W
    win = NA_WR * GRID_W
    qb, kb, vb = T_QNA * 2, T_KNA * 2, T_VNA * 2
    bias_spec = lambda i: pl.BlockSpec((1, 1, tq, win), lambda b, h, s, vm: (vm[gps * s + i], h, 0, 0))
    grid_spec = pltpu.PrefetchScalarGridSpec(
        num_scalar_prefetch=1,
        grid=(batch, NA_HEADS, ns),
        in_specs=[pl.BlockSpec((gps * tq, LANES), lambda b, h, s, vm: (b * ns + s, qb + h)),
                  pl.BlockSpec((s_len, LANES), lambda b, h, s, vm: (b, kb + h)),
                  pl.BlockSpec((s_len, LANES), lambda b, h, s, vm: (b, vb + h)),
                  pl.BlockSpec((lc, LANES), lambda b, h, s, vm: (b, kb + h)),
                  pl.BlockSpec((lc, LANES), lambda b, h, s, vm: (b, vb + h))]
                 + [bias_spec(i) for i in range(gps)],
        out_specs=pl.BlockSpec((gps * tq, LANES), lambda b, h, s, vm: (b * ns + s, h)),
    )
    return pl.pallas_call(
        functools.partial(_na_kernel, rows=rows),
        grid_spec=grid_spec,
        out_shape=jax.ShapeDtypeStruct((batch * s_len, NA_HEADS * NA_HEAD_DIM), BF16),
        compiler_params=_params(("arbitrary", "arbitrary", "arbitrary")),
        name="na_attention",
    )(var_map, p_lat, p_lat, p_lat, p_ctx, p_ctx, *([bias] * gps))


def _layer_norm(x, g, b):
    mu = jnp.mean(x, axis=-1, keepdims=True)
    xc = x - mu
    var = jnp.mean(xc * xc, axis=-1, keepdims=True)
    return xc * lax.rsqrt(var + LN_EPS) * g + b


def _outln_kernel(a1_ref, a2_ref, w_ref, h_ref, mod_ref, ln_ref, r_ref, h1_ref, u2_ref, lg_ref):
    half = a1_ref.shape[1]
    sub = a1_ref.shape[0] // OUTLN_SUBTILES
    for r0 in range(0, a1_ref.shape[0], sub):
        rs = slice(r0, r0 + sub)
        y = _dot(a1_ref[rs, :], w_ref[:half, :]) + _dot(a2_ref[rs, :], w_ref[half:, :])
        x = DEEPNORM_ALPHA * h_ref[rs, :] + mod_ref[0, 0:1, :] * y
        hn = _layer_norm(x, ln_ref[0:1, :], ln_ref[1:2, :])
        h1_ref[rs, :] = hn
        u = hn * (1.0 + mod_ref[0, 2:3, :]) + mod_ref[0, 1:2, :]
        u_hi, u_lo = _split_bf16(u)
        u2_ref[rs, :] = u_hi
        lg_ref[rs, :] = _dot(u_hi, r_ref[0]) + _dot(u_lo, r_ref[0]) + _dot(u_hi, r_ref[1])


def _outln_call(a1, a2, w, h, mod, ln, router2, rows_per_group, tm=512):
    m, d = h.shape
    half = a1.shape[1]
    tpg = rows_per_group // tm
    return pl.pallas_call(
        _outln_kernel,
        grid=(m // tm,),
        in_specs=[pl.BlockSpec((tm, half), lambda i: (i, 0)),
                  pl.BlockSpec((tm, half), lambda i: (i, 0)),
                  _resident_spec((2 * half, d)),
                  pl.BlockSpec((tm, d), lambda i: (i, 0)),
                  pl.BlockSpec((1, 3, d), lambda i: (i // tpg, 0, 0)),
                  pl.BlockSpec((2, d), lambda i: (0, 0)),
                  _resident_spec((2, d, LANES))],
        out_specs=[pl.BlockSpec((tm, d), lambda i: (i, 0)),
                   pl.BlockSpec((tm, d), lambda i: (i, 0)),
                   pl.BlockSpec((tm, LANES), lambda i: (i, 0))],
        out_shape=[jax.ShapeDtypeStruct((m, d), F32),
                   jax.ShapeDtypeStruct((m, d), BF16),
                   jax.ShapeDtypeStruct((m, LANES), F32)],
        compiler_params=_params(("arbitrary",)),
        name="out_proj_postnorm",
    )(a1, a2, w, h, mod, ln, router2)


def _router_operand(router):
    d, e = router.shape
    rp = jnp.pad(router.astype(F32), ((0, 0), (0, LANES - e)))
    hi = rp.astype(BF16)
    lo = (rp - hi.astype(F32)).astype(BF16)
    return jnp.stack([hi, lo])


def _route_kernel(lg_ref, slot_ref, aff_ref, tri_ref, *, cap):
    n_tok = lg_ref.shape[2]

    @pl.when(pl.program_id(0) == 0)
    def _():
        chunk = 256
        for r0 in range(0, n_tok, chunk):
            r = r0 + lax.broadcasted_iota(I32, (chunk, n_tok), 0)
            c = lax.broadcasted_iota(I32, (chunk, n_tok), 1)
            tri_ref[r0:r0 + chunk, :] = jnp.where(r < c, 1.0, 0.0).astype(BF16)

    lg = lg_ref[0]
    ex = jnp.exp(lg - jnp.max(lg, axis=0, keepdims=True))
    aff = ex / jnp.sum(ex, axis=0, keepdims=True)
    bits = lax.bitcast_convert_type(aff, I32)
    n_e = lg.shape[0]
    count = lambda mask: jnp.sum(jnp.where(mask, 1.0, 0.0), axis=1, keepdims=True)

    def body(_, lohi):
        lo, hi = lohi
        mid = lo + jnp.right_shift(hi - lo, 1)
        ge = count(bits >= mid) >= cap
        return jnp.where(ge, mid, lo), jnp.where(ge, hi, mid)

    lo0 = jnp.zeros((n_e, 1), I32)
    hi0 = jnp.full((n_e, 1), 0x7F800000, I32)
    thr, _ = lax.fori_loop(0, 31, body, (lo0, hi0))
    gt = bits > thr
    eq = bits == thr
    need = cap - count(gt)
    pre_eq = _dot(jnp.where(eq, 1.0, 0.0).astype(BF16), tri_ref[...])
    sel = jnp.logical_or(gt, jnp.logical_and(eq, pre_eq < need))
    slot = _dot(jnp.where(sel, 1.0, 0.0).astype(BF16), tri_ref[...])
    slot_ref[0] = jnp.where(sel, slot.astype(I32), -1)
    aff_ref[0] = aff


def _route_call(lg_t, cap):
    b, e, n_tok = lg_t.shape
    spec = pl.BlockSpec((1, e, n_tok), lambda i: (i, 0, 0))
    return pl.pallas_call(
        functools.partial(_route_kernel, cap=cap),
        grid=(b,),
        in_specs=[spec],
        out_specs=[spec, spec],
        out_shape=[jax.ShapeDtypeStruct((b, e, n_tok), I32), jax.ShapeDtypeStruct((b, e, n_tok), F32)],
        scratch_shapes=[pltpu.VMEM((n_tok, n_tok), BF16)],
        compiler_params=_params(("arbitrary",)),
        name="ec_route",
    )(lg_t)


def _gather_kernel(slot_ref, u_ref, o_ref):
    cap = o_ref.shape[1]
    n_tok = u_ref.shape[0]
    j = lax.broadcasted_iota(I32, (cap, n_tok), 0)
    onehot = jnp.where(j == slot_ref[0, 0], 1.0, 0.0).astype(BF16)
    o_ref[0] = _dot(onehot, u_ref[...]).astype(o_ref.dtype)


def _gather_call(slot4, u2, cap):
    b, e, _, n_tok = slot4.shape
    d = u2.shape[1]
    return pl.pallas_call(
        _gather_kernel,
        grid=(b, e),
        in_specs=[pl.BlockSpec((1, 1, 1, n_tok), lambda i, j: (i, j, 0, 0)),
                  pl.BlockSpec((n_tok, d), lambda i, j: (i, 0))],
        out_specs=pl.BlockSpec((1, cap, d), lambda i, j: (j, i, 0)),
        out_shape=jax.ShapeDtypeStruct((e, b * cap, d), BF16),
        compiler_params=_params(("arbitrary", "arbitrary")),
        name="moe_gather",
    )(slot4, u2)


def _expert_hidden_kernel(x_ref, w1_ref, w3_ref, o_ref):
    tf = w1_ref.shape[3]
    h = _dot(x_ref[0], jnp.concatenate([w1_ref[0, 0].astype(BF16), w3_ref[0, 0].astype(BF16)], axis=1))
    h1 = h[:, :tf]
    o_ref[0] = ((h1 / (1.0 + jnp.exp(-h1))) * h[:, tf:]).astype(o_ref.dtype)


def _expert_out_kernel(hd_ref, w2_ref, o_ref):
    y = _dot(hd_ref[0], w2_ref[0, 0].astype(BF16)).astype(o_ref.dtype)
    cap = o_ref.shape[1]
    for b in range(o_ref.shape[0]):
        o_ref[b] = y[b * cap:(b + 1) * cap]


def _expert_call(xe, w1, w3, w2, layer, batch, tf=512, tn=1024):
    e, m, d = xe.shape
    cap = m // batch
    ff = w1.shape[3]
    hidden = pl.pallas_call(
        _expert_hidden_kernel,
        grid=(e, pl.cdiv(ff, tf)),
        in_specs=[pl.BlockSpec((1, m, d), lambda i, f: (i, 0, 0)),
                  pl.BlockSpec((1, 1, d, tf), lambda i, f: (layer, i, 0, f)),
                  pl.BlockSpec((1, 1, d, tf), lambda i, f: (layer, i, 0, f))],
        out_specs=pl.BlockSpec((1, m, tf), lambda i, f: (i, 0, f)),
        out_shape=jax.ShapeDtypeStruct((e, m, ff), BF16),
        compiler_params=_params(("arbitrary", "arbitrary")),
        name="moe_expert_hidden",
    )(xe, w1, w3)
    return pl.pallas_call(
        _expert_out_kernel,
        grid=(e, d // tn),
        in_specs=[pl.BlockSpec((1, m, ff), lambda i, j: (i, 0, 0)),
                  pl.BlockSpec((1, 1, ff, tn), lambda i, j: (layer, i, 0, j))],
        out_specs=pl.BlockSpec((batch, cap, tn), lambda i, j: (0, i, j)),
        out_shape=jax.ShapeDtypeStruct((batch, e * cap, d), BF16),
        compiler_params=_params(("arbitrary", "arbitrary")),
        name="moe_expert_out",
    )(hidden, w2)


def _scatter_kernel(slot_ref, aff_ref, ye_ref, h_ref, mod_ref, ln_ref, *rest, with_next):
    if with_next:
        nmod_ref, h2_ref, un_ref, pt_ref = rest
    else:
        h2_ref, pt_ref = rest
    n_e = slot_ref.shape[1]
    cap = ye_ref.shape[1] // n_e
    tt = h_ref.shape[0]
    j = lax.broadcasted_iota(I32, (cap, tt), 0)
    for e in range(n_e):
        pt_ref[e * cap:(e + 1) * cap, :] = jnp.where(j == slot_ref[0, e:e + 1, :],
                                                     aff_ref[0, e:e + 1, :], 0.0).astype(BF16)
    moe = _dot_tn(pt_ref[...], ye_ref[0])
    x = DEEPNORM_ALPHA * h_ref[...] + mod_ref[0] * moe
    hn = _layer_norm(x, ln_ref[0:1, :], ln_ref[1:2, :])
    h2_ref[...] = hn
    if with_next:
        un_ref[...] = (hn * (1.0 + nmod_ref[0, 1:2, :]) + nmod_ref[0, 0:1, :]).astype(un_ref.dtype)


def _scatter_call(slot, aff, ye, h1, gate, ln, next_ss, tt=512):
    b, e, n_tok = slot.shape
    m, d = h1.shape
    nt = n_tok // tt
    rows = ye.shape[1]
    with_next = next_ss is not None
    tok_spec = pl.BlockSpec((1, e, tt), lambda i, t: (i, 0, t))
    row_spec = pl.BlockSpec((tt, d), lambda i, t: (i * nt + t, 0))
    in_specs = [tok_spec, tok_spec,
                pl.BlockSpec((1, rows, d), lambda i, t: (i, 0, 0), pipeline_mode=pl.Buffered(1)),
                row_spec,
                pl.BlockSpec((1, 1, d), lambda i, t: (i, 0, 0)),
                pl.BlockSpec((2, d), lambda i, t: (0, 0))]
    args = [slot, aff, ye, h1, gate, ln]
    out_specs = [row_spec]
    out_shape = [jax.ShapeDtypeStruct((m, d), F32)]
    if with_next:
        in_specs.append(pl.BlockSpec((1, 2, d), lambda i, t: (i, 0, 0)))
        args.append(next_ss)
        out_specs.append(row_spec)
        out_shape.append(jax.ShapeDtypeStruct((m, d), BF16))
    res = pl.pallas_call(
        functools.partial(_scatter_kernel, with_next=with_next),
        grid=(b, nt),
        in_specs=in_specs,
        out_specs=out_specs,
        out_shape=out_shape,
        scratch_shapes=[pltpu.VMEM((rows, tt), BF16)],
        compiler_params=_params(("arbitrary", "arbitrary")),
        name="moe_combine_postnorm",
    )(*args)
    return res if with_next else (res[0], None)


def _ec_moe(h1, u2, logits, w1, w3, w2, layer, gate, ln, next_ss, batch, n_tok):
    cap = EC_CAPACITY_FACTOR * n_tok // N_EXPERTS
    lg_t = jnp.swapaxes(logits[:, :N_EXPERTS].reshape(batch, n_tok, N_EXPERTS), 1, 2)
    slot, aff = _route_call(lg_t, cap)
    xe = _gather_call(slot.reshape(batch, N_EXPERTS, 1, n_tok), u2, cap)
    ye = _expert_call(xe, w1, w3, w2, layer, batch)
    return _scatter_call(slot, aff, ye, h1, gate, ln, next_ss)


def _mm_kernel(a_ref, w_ref, o_ref):
    o_ref[...] = _dot(a_ref[...], w_ref[...]).astype(o_ref.dtype)


def _mm_call(a, w, out_dtype, tm=1024, tn=512):
    m, k = a.shape
    n = w.shape[1]
    return pl.pallas_call(
        _mm_kernel,
        grid=(m // tm, n // tn),
        in_specs=[pl.BlockSpec((tm, k), lambda i, j: (i, 0)),
                  pl.BlockSpec((k, tn), lambda i, j: (0, j))],
        out_specs=pl.BlockSpec((tm, tn), lambda i, j: (i, j)),
        out_shape=jax.ShapeDtypeStruct((m, n), out_dtype),
        compiler_params=_params(("arbitrary", "arbitrary")),
        name="matmul",
    )(a, w)


def _sconv_kernel(p_ref, w_ref, b_ref, o_ref):
    x = p_ref[...].astype(F32)
    n = x.shape[0]
    r = lax.broadcasted_iota(I32, x.shape, 0)
    prev = jnp.where(r == 0, 0.0, pltpu.roll(x, 1, 0))
    nxt = jnp.where(r == n - 1, 0.0, pltpu.roll(x, n - 1, 0))
    o_ref[...] = (b_ref[...] + w_ref[0:1, :] * prev + w_ref[1:2, :] * x + w_ref[2:3, :] * nxt
                  ).astype(o_ref.dtype)


def _sconv_call(p, conv_w, conv_b, col0, ncols, out_dtype, batch, n_tok, tc=512):
    c0 = col0 // tc
    return pl.pallas_call(
        _sconv_kernel,
        grid=(batch, ncols // tc),
        in_specs=[pl.BlockSpec((n_tok, tc), lambda b, j: (b, c0 + j)),
                  pl.BlockSpec((HY_SHORT, tc), lambda b, j: (0, c0 + j)),
                  pl.BlockSpec((1, tc), lambda b, j: (0, c0 + j))],
        out_specs=pl.BlockSpec((n_tok, tc), lambda b, j: (b, j)),
        out_shape=jax.ShapeDtypeStruct((batch * n_tok, ncols), out_dtype),
        compiler_params=_params(("arbitrary", "arbitrary")),
        name="hyena_short_conv",
    )(p, conv_w, conv_b.reshape(1, -1))


def _dft_fwd_kernel(fc_ref, fs_ref, r1_ref, r2_ref, *rest, spectral):
    xr = _dot(fc_ref[...], r1_ref[...])
    xi = _dot(fs_ref[...], r2_ref[...])
    if spectral:
        h_ref, nyq_ref, o_ref = rest
        hr = h_ref[0, 0]
        hi = h_ref[0, 1]
        dc = lax.broadcasted_iota(I32, xr.shape, 0) == 0
        yr = xr * hr - jnp.where(dc, 0.0, xi * hi)
        yi = jnp.where(dc, xi * nyq_ref[...], xr * hi + xi * hr)
    else:
        (o_ref,) = rest
        yr, yi = xr, xi
    o_ref[0, 0] = yr.astype(o_ref.dtype)
    o_ref[0, 1] = yi.astype(o_ref.dtype)


def _resident_spec(shape):
    return pl.BlockSpec(shape, lambda *_: (0,) * len(shape), pipeline_mode=pl.Buffered(1))


def _dft_fwd_call(fc, fs, r1, r2, spec, out_dtype, batch, n_tok, tn=256):
    ncols = r1.shape[1]
    r_spec = pl.BlockSpec((n_tok, tn), lambda j, b: (b, j))
    in_specs = [_resident_spec(fc.shape), _resident_spec(fs.shape), r_spec, r_spec]
    args = [fc, fs, r1, r2]
    if spec is not None:
        h, nyq, hcol0 = spec
        c0 = hcol0 // tn
        in_specs += [pl.BlockSpec((1, 2, n_tok, tn), lambda j, b: (0, 0, 0, c0 + j)),
                     pl.BlockSpec((1, tn), lambda j, b: (0, c0 + j))]
        args += [h, nyq]
    return pl.pallas_call(
        functools.partial(_dft_fwd_kernel, spectral=spec is not None),
        grid=(ncols // tn, batch),
        in_specs=in_specs,
        out_specs=pl.BlockSpec((1, 2, n_tok, tn), lambda j, b: (b, 0, 0, j)),
        out_shape=jax.ShapeDtypeStruct((batch, 2, n_tok, ncols), out_dtype),
        compiler_params=_params(("arbitrary", "arbitrary")),
        name="dft_forward",
    )(*args)


def _dual_kernel(a1_ref, a2_ref, b1_ref, b2_ref, *rest, hyena, planes):
    b1 = b1_ref[0, 0] if planes else b1_ref[...]
    b2 = b2_ref[0, 0] if planes else b2_ref[...]
    y = _dot(a1_ref[...], b1) + _dot(a2_ref[...], b2)
    if hyena:
        s_ref, z_ref, skip_ref, o_ref = rest
        y = s_ref[...].astype(F32) * (y + skip_ref[...] * z_ref[...].astype(F32))
    else:
        (o_ref,) = rest
    o_ref[...] = y.astype(o_ref.dtype)


def _dual_call(a1, a2, b1, b2, hy, batch, n_tok, ncols, tn=256):
    planes = b1.ndim == 4
    if planes:
        b_specs = [pl.BlockSpec((1, 1, n_tok, tn), lambda b, j: (b, 0, 0, j)),
                   pl.BlockSpec((1, 1, n_tok, tn), lambda b, j: (b, 1, 0, j))]
    else:
        b_specs = [pl.BlockSpec((n_tok, tn), lambda b, j: (b, j))] * 2
    in_specs = [_resident_spec(a1.shape), _resident_spec(a2.shape)] + b_specs
    args = [a1, a2, b1, b2]
    o_spec = pl.BlockSpec((n_tok, tn), lambda b, j: (b, j))
    if hy is not None:
        s, scol0, z, skip = hy
        c0 = scol0 // tn
        in_specs += [pl.BlockSpec((n_tok, tn), lambda b, j: (b, c0 + j)),
                     o_spec,
                     pl.BlockSpec((1, tn), lambda b, j: (0, j))]
        args += [s, z, skip]
    return pl.pallas_call(
        functools.partial(_dual_kernel, hyena=hy is not None, planes=planes),
        grid=(batch, ncols // tn),
        in_specs=in_specs,
        out_specs=o_spec,
        out_shape=jax.ShapeDtypeStruct((batch * n_tok, ncols), BF16),
        compiler_params=_params(("arbitrary", "arbitrary")),
        name="seq_mix_matmul",
    )(*args)


def _fnet_ch_kernel(p_ref, w_ref, gc_ref, gs_ref):
    r = _dot(p_ref[...].astype(BF16), w_ref[...])
    w = gc_ref.shape[1]
    gc_ref[...] = r[:, :w].astype(gc_ref.dtype)
    gs_ref[...] = r[:, w:].astype(gs_ref.dtype)


def _fnet_ch_call(p, w, col0, tm=1024):
    m = p.shape[0]
    gw = w.shape[0]
    c0 = col0 // gw
    o_spec = pl.BlockSpec((tm, gw), lambda i, g: (i, g))
    return pl.pallas_call(
        _fnet_ch_kernel,
        grid=(m // tm, FN_GROUPS),
        in_specs=[pl.BlockSpec((tm, gw), lambda i, g: (i, c0 + g)),
                  pl.BlockSpec((gw, 2 * gw), lambda i, g: (0, 0))],
        out_specs=[o_spec, o_spec],
        out_shape=[jax.ShapeDtypeStruct((m, FN_GROUPS * gw), BF16)] * 2,
        compiler_params=_params(("arbitrary", "arbitrary")),
        name="fnet_channel_dft",
    )(p, w)


def _cos_sin_matrix(n_rows, n_cols, period, split=64):
    r = np.arange(n_rows, dtype=np.int64)[:, None]
    c_hi = (np.arange(n_cols // split, dtype=np.int64) * split)[None, :]
    c_lo = np.arange(split, dtype=np.int64)[None, :]
    ang = lambda c: 2.0 * np.pi * ((r * c) % period).astype(np.float64) / period
    ca, sa = jnp.asarray(np.cos(ang(c_hi)), F32), jnp.asarray(np.sin(ang(c_hi)), F32)
    cb, sb = jnp.asarray(np.cos(ang(c_lo)), F32), jnp.asarray(np.sin(ang(c_lo)), F32)
    cos = ca[:, :, None] * cb[:, None, :] - sa[:, :, None] * sb[:, None, :]
    sin = sa[:, :, None] * cb[:, None, :] + ca[:, :, None] * sb[:, None, :]
    return cos.reshape(n_rows, n_cols), sin.reshape(n_rows, n_cols)


def _trig_kernel(ta_ref, tb_ref, ea_ref, eb_ref, *o_refs, mode, period, scale):
    sa_ca = _dot(ta_ref[...], ea_ref[...])
    sb_cb = _dot(tb_ref[...], eb_ref[...])
    nc = sa_ca.shape[1] // 2
    ca, sa, cb, sb = sa_ca[:, :nc], sa_ca[:, nc:], sb_cb[:, :nc], sb_cb[:, nc:]
    cos = ca * cb - sa * sb
    sin = sa * cb + ca * sb
    tm = cos.shape[0]
    row = pl.program_id(0) * tm + lax.broadcasted_iota(I32, cos.shape, 0)
    col = lax.broadcasted_iota(I32, cos.shape, 1)
    alt = lambda idx: (1 - 2 * jnp.bitwise_and(idx, 1)).astype(F32)
    if mode == "hyena":
        fc_ref, fs_ref, gc_ref, gs_ref = o_refs
        fc_ref[...] = cos.astype(fc_ref.dtype)
        fs_ref[...] = jnp.where(row == 0, alt(col), -sin).astype(fs_ref.dtype)
        gc_ref[...] = (cos * jnp.where(col == 0, 1.0 / period, 2.0 / period)).astype(gc_ref.dtype)
        gs_ref[...] = jnp.where(col == 0, alt(row) * (1.0 / period), sin * (-2.0 / period)).astype(gs_ref.dtype)
    else:
        c_ref, ms_ref = o_refs
        c_ref[...] = (cos * scale).astype(c_ref.dtype)
        ms_ref[...] = (sin * (-scale)).astype(ms_ref.dtype)


def _split3(x):
    bf = jnp.bfloat16
    x = np.asarray(x, np.float32)
    hi = x.astype(bf)
    r1 = x - hi.astype(np.float32)
    mid = r1.astype(bf)
    lo = (r1 - mid.astype(np.float32)).astype(bf)
    return [hi, mid, lo]


def _trig_call(n, period, mode, scale=1.0, split=64, tm=256):
    r = np.arange(n, dtype=np.int64)[:, None]
    c_hi = (np.arange(n // split, dtype=np.int64) * split)[None, :]
    c_lo = np.arange(split, dtype=np.int64)[None, :]
    ang = lambda c: 2.0 * np.pi * ((r * c) % period).astype(np.float64) / period
    def operands(a, col_group):
        terms = _split3(np.cos(a)) + _split3(np.sin(a))
        k = a.shape[1]
        spread = (col_group[None, :] == np.arange(k)[:, None]).astype(np.float32)
        zero = np.zeros_like(spread)
        e = np.concatenate([np.concatenate([spread, zero], 1)] * 3 + [np.concatenate([zero, spread], 1)] * 3, 0)
        return jnp.asarray(np.concatenate(terms, axis=1)), jnp.asarray(e, BF16)
    cols = np.arange(n)
    ta, ea = operands(ang(c_hi), cols // split)
    tb, eb = operands(ang(c_lo), cols % split)
    n_out = 4 if mode == "hyena" else 2
    o_spec = pl.BlockSpec((tm, n), lambda i: (i, 0))
    return pl.pallas_call(
        functools.partial(_trig_kernel, mode=mode, period=period, scale=scale),
        grid=(n // tm,),
        in_specs=[pl.BlockSpec((tm, ta.shape[1]), lambda i: (i, 0)),
                  pl.BlockSpec((tm, tb.shape[1]), lambda i: (i, 0)),
                  pl.BlockSpec(ea.shape, lambda i: (0, 0)),
                  pl.BlockSpec(eb.shape, lambda i: (0, 0))],
        out_specs=[o_spec] * n_out,
        out_shape=[jax.ShapeDtypeStruct((n, n), BF16)] * n_out,
        compiler_params=_params(("arbitrary",)),
        name="dft_tables",
    )(ta, tb, ea, eb)


def _hyena_dft_operands(n_tok):
    return _trig_call(n_tok, 2 * n_tok, "hyena")


def _fnet_operands(n_tok, gw):
    cl, msl = _trig_call(n_tok, n_tok, "fnet", scale=1.0 / math.sqrt(n_tok * gw))
    cw, sw = _cos_sin_matrix(gw, gw, gw)
    return cl, msl, jnp.concatenate([cw, sw], axis=1).astype(BF16)


def _filter_kernel(z_ref, w1_ref, b1_ref, f1_ref, w2_ref, b2_ref, f2_ref, w3f_ref, w3b_ref, dl_ref,
                   sum_ref, dif_ref, nyq_ref, hdn_ref):
    @pl.when(jnp.logical_and(pl.program_id(0) == 0, pl.program_id(1) == 0))
    def _():
        hdot = lambda a, b: jnp.dot(a, b, precision=lax.Precision.HIGHEST, preferred_element_type=F32)
        hdn = jnp.sin(f1_ref[...] * (hdot(z_ref[...], w1_ref[...]) + b1_ref[...]))
        hdn = jnp.sin(f2_ref[...] * (hdot(hdn, w2_ref[...]) + b2_ref[...]))
        hdn_ref[...] = hdn.astype(hdn_ref.dtype)

    n = hdn_ref.shape[0]
    row = lax.broadcasted_iota(I32, (n, dl_ref.shape[1]), 0)
    decay = jnp.exp(-(row.astype(F32) / (n - 1)) * dl_ref[...])
    hf = _dot(hdn_ref[...], w3f_ref[...].astype(BF16)) * decay
    hb = jnp.where(row == 0, 0.0, _dot(hdn_ref[...], w3b_ref[...].astype(BF16)) * decay)
    tot = hf + hb
    sum_ref[...] = tot.astype(sum_ref.dtype)
    dif_ref[...] = (hf - hb).astype(dif_ref.dtype)
    nyq_ref[...] = jnp.sum(jnp.where(jnp.bitwise_and(row, 1) == 0, tot, -tot), axis=0, keepdims=True)


def _filter_call(n_tok, fw1, fb1, ff1, fw2, fb2, ff2, fw3, tc=512):
    t01 = np.linspace(0.0, 1.0, n_tok, dtype=np.float32)
    w = (2.0 * math.pi * np.arange(n_tok, dtype=np.float32) / n_tok).astype(np.float32)
    bands = np.linspace(1e-4, HY_BANDS - 1, HY_BANDS, dtype=np.float32)
    z = np.concatenate([t01[:, None], np.cos(w[:, None] * bands), -np.sin(w[:, None] * bands)], -1)
    deltas = np.abs(np.linspace(HY_MIN_DECAY, HY_MAX_DECAY, HY_W, dtype=np.float32))[None, :]
    emb, hid = fw1.shape
    nc = HY_W // tc
    full = lambda shape: pl.BlockSpec(shape, lambda o, j: (0,) * len(shape))
    o_spec = pl.BlockSpec((n_tok, tc), lambda o, j: (0, o * nc + j))
    return pl.pallas_call(
        _filter_kernel,
        grid=(HY_ORDER, nc),
        in_specs=[full((n_tok, emb)), full((emb, hid)), full((1, hid)), full((1, hid)),
                  full((hid, hid)), full((1, hid)), full((1, hid)),
                  pl.BlockSpec((hid, tc), lambda o, j: (0, (2 * o) * nc + j)),
                  pl.BlockSpec((hid, tc), lambda o, j: (0, (2 * o + 1) * nc + j)),
                  pl.BlockSpec((1, tc), lambda o, j: (0, j))],
        out_specs=[o_spec, o_spec, pl.BlockSpec((1, tc), lambda o, j: (0, o * nc + j))],
        out_shape=[jax.ShapeDtypeStruct((n_tok, HY_ORDER * HY_W), BF16),
                   jax.ShapeDtypeStruct((n_tok, HY_ORDER * HY_W), BF16),
                   jax.ShapeDtypeStruct((1, HY_ORDER * HY_W), F32)],
        scratch_shapes=[pltpu.VMEM((n_tok, hid), BF16)],
        compiler_params=_params(("arbitrary", "arbitrary")),
        name="hyena_filters",
    )(jnp.asarray(z.astype(np.float32)), fw1, fb1.reshape(1, hid), ff1.reshape(1, hid),
      fw2, fb2.reshape(1, hid), ff2.reshape(1, hid), fw3, fw3, jnp.asarray(deltas))


def kernel(x, c, ctx, c_ctx, ada_w, ada_b, ln1_g, ln1_b, ln2_g, ln2_b, router, exp_w1, exp_w3, exp_w2,
           ab_w_in, ab_kv_norm, ab_w_ukv, ab_rpb, ab_w_out,
           cd_w_in, cd_conv_w, cd_conv_b, cd_filt_w1, cd_filt_b1, cd_filt_freq1, cd_filt_w2, cd_filt_b2,
           cd_filt_freq2, cd_filt_w3, cd_skip, cd_w_out):
    batch, n_tok, d = x.shape
    lc = ctx.shape[1]
    x2d = x.reshape(batch * n_tok, d)
    ctx2d = ctx.reshape(batch * lc, d)

    pad_rows = (-(batch + 1)) % 8
    cv = jnp.concatenate([c, c_ctx[None, :], jnp.zeros((pad_rows, d), F32)], axis=0)
    ada = _ada_call(cv, ada_w, ada_b)
    mods = [ada[i, :batch].reshape(batch, 6, d) for i in range(DEPTH)]
    ctx_ss = ada[0, batch, :2 * d].reshape(1, 2, d)

    w_in = _w_in_prep_call(jnp.swapaxes(ab_w_in, 1, 2), 0)
    p_lat = _mm_rope_call(x2d, mods[0][:, 0:2], n_tok, w_in, _rope_tables(n_tok), tm=1024)
    p_ctx = _mm_rope_call(ctx2d, ctx_ss, batch * lc, w_in, _identity_rope_tables(lc), tm=lc)
    w_ukv = _permute_w_ukv(ab_w_ukv[0])
    kv_lat = _kvup_call(p_lat, ab_kv_norm[0], w_ukv, tm=512)
    kv_ctx = _kvup_call(p_ctx, ab_kv_norm[0], w_ukv, tm=lc)
    a_out = _mla_call(p_lat, kv_lat, p_ctx, kv_ctx, batch, n_tok, lc)
    var_map, na_er, na_ec = _na_tables(n_tok // GRID_W)
    b_out = _na_call(p_lat, p_ctx, _na_bias(ab_rpb[0], na_er, na_ec), jnp.asarray(var_map),
                     batch, n_tok, lc)
    ln1 = jnp.stack([ln1_g, ln1_b], axis=1)
    ln2 = jnp.stack([ln2_g, ln2_b], axis=1)
    h1, u2, logits = _outln_call(a_out, b_out, ab_w_out[0].astype(BF16), x2d,
                                 mods[0][:, 2:5], ln1[0], _router_operand(router[0]), n_tok)
    h, u = _ec_moe(h1, u2, logits, exp_w1, exp_w3, exp_w2, 0, mods[0][:, 5:6], ln2[0],
                   mods[1][:, 0:2], batch, n_tok)

    p1 = _mm_call(u, cd_w_in[0].astype(BF16), BF16, tn=1024)
    s0 = _sconv_call(p1, cd_conv_w[0], cd_conv_b[0], 0, HY_W, BF16, batch, n_tok)
    s12 = _sconv_call(p1, cd_conv_w[0], cd_conv_b[0], HY_W, HY_ORDER * HY_W, F32, batch, n_tok)
    h_sum, h_dif, h_nyq = _filter_call(n_tok, cd_filt_w1[0], cd_filt_b1[0], cd_filt_freq1[0], cd_filt_w2[0],
                                       cd_filt_b2[0], cd_filt_freq2[0], cd_filt_w3[0])
    fc, fs, gc, gs = _hyena_dft_operands(n_tok)
    h_spec = _dft_fwd_call(fc, fs, h_sum, h_dif, None, F32, 1, n_tok)
    z = s0
    for o in range(HY_ORDER):
        y_spec = _dft_fwd_call(fc, fs, z, z, (h_spec, h_nyq, o * HY_W), BF16, batch, n_tok)
        z = _dual_call(gc, gs, y_spec, y_spec, (s12, o * HY_W, z, cd_skip[0][o:o + 1]), batch, n_tok, HY_W)
    cl, msl, cw = _fnet_operands(n_tok, FN_GROUP_W)
    g_cos, g_sin = _fnet_ch_call(p1, cw, HY_IN_W)
    y_fn = _dual_call(cl, msl, g_cos, g_sin, None, batch, n_tok, FN_W)
    h1, u2, logits = _outln_call(z, y_fn, cd_w_out[0].astype(BF16), h,
                                 mods[1][:, 2:5], ln1[1], _router_operand(router[1]), n_tok)
    h, _ = _ec_moe(h1, u2, logits, exp_w1, exp_w3, exp_w2, 1, mods[1][:, 5:6], ln2[1],
                   None, batch, n_tok)
    return h.reshape(batch, n_tok, d)
```

```python
import functools
import math

import numpy as np
import jax
import jax.numpy as jnp
from jax import lax
from jax.experimental import pallas as pl
from jax.experimental.pallas import tpu as pltpu

F32 = jnp.float32
BF16 = jnp.bfloat16
I32 = jnp.int32

D_MODEL = 2048
BATCH = 4
SEQ = 2048
DEPTH = 2
CTX_LEN = 256
GRID_W = 64
DEEPNORM_ALPHA = (2.0 * DEPTH) ** 0.25
LN_EPS = 1e-6
NEG_INF = -1e30
LOG2E = math.log2(math.e)

MLA_HEADS = 8
MLA_NOPE = 128
MLA_ROPE = 64
MLA_QK = MLA_NOPE + MLA_ROPE
MLA_V = 128
MLA_KV_RANK = 512
ROPE_THETA = 10000.0

NA_HEADS = 8
NA_HEAD_DIM = 128
NA_KH = 8
NA_KW = 16

OFF_CKV = MLA_HEADS * MLA_QK
OFF_KPE = OFF_CKV + MLA_KV_RANK
OFF_QNA = OFF_KPE + MLA_ROPE
OFF_KNA = OFF_QNA + NA_HEADS * NA_HEAD_DIM
OFF_VNA = OFF_KNA + NA_HEADS * NA_HEAD_DIM

HY_W = 1024
HY_ORDER = 2
HY_IN_W = (HY_ORDER + 1) * HY_W
HY_SHORT = 3
HY_BANDS = 16
HY_DECAY_TARGET = 1e-2
HY_MIN_DECAY = math.log(HY_DECAY_TARGET) / 1.5
HY_MAX_DECAY = math.log(HY_DECAY_TARGET) / 0.3

FN_W = 1024
FN_GROUPS = 4
FN_GROUP_W = FN_W // FN_GROUPS

N_EXPERTS = 16
EC_CAPACITY_FACTOR = 2
EXPERT_FF = 1408

LANES = 128
MXU_DIM = 256
VMEM_LIMIT = 56 * 1024 * 1024

QTILE_W = 2 * LANES
T_KPE = MLA_HEADS
T_CKV = T_KPE + 2
T_QNA = T_CKV + MLA_KV_RANK // QTILE_W
T_KNA = T_QNA + NA_HEADS * NA_HEAD_DIM // QTILE_W
T_VNA = T_KNA + NA_HEADS * NA_HEAD_DIM // QTILE_W
N_ABTILES = T_VNA + NA_HEADS * NA_HEAD_DIM // QTILE_W
AB_PERM_W = N_ABTILES * QTILE_W

MLA_SUBTILES = 2
OUTLN_SUBTILES = 2
NA_G = 4
NA_WR = NA_KH + NA_G - 1
NA_GROUPS_PER_STEP = 2


def _params(sem, vmem=VMEM_LIMIT):
    return pltpu.CompilerParams(dimension_semantics=sem, vmem_limit_bytes=vmem)


def _dot(a, b):
    return jnp.dot(a, b, preferred_element_type=F32)


def _dot_nt(a, b):
    return lax.dot_general(a, b, (((1,), (1,)), ((), ())), preferred_element_type=F32)


def _dot_tn(a, b):
    return lax.dot_general(a, b, (((0,), (0,)), ((), ())), preferred_element_type=F32)


def _split_bf16(x):
    hi = x.astype(BF16)
    lo = (x - hi.astype(F32)).astype(BF16)
    return hi, lo


def _ada_kernel(c_ref, w_ref, b_ref, o_ref):
    c = c_ref[...]
    a = c / (1.0 + jnp.exp(-c))
    a_hi, a_lo = _split_bf16(a)
    w_hi, w_lo = _split_bf16(w_ref[0])
    rows = a.shape[0]
    r = _dot(jnp.concatenate([a_hi, a_lo], axis=0), w_hi)
    o_ref[0] = r[:rows] + r[rows:] + _dot(a_hi, w_lo) + b_ref[0]


def _ada_call(cv, ada_w, ada_b, tn=1024):
    depth, d, n = ada_w.shape
    rows = cv.shape[0]
    return pl.pallas_call(
        _ada_kernel,
        grid=(depth, n // tn),
        in_specs=[pl.BlockSpec((rows, d), lambda l, j: (0, 0)),
                  pl.BlockSpec((1, d, tn), lambda l, j: (l, 0, j)),
                  pl.BlockSpec((1, 1, tn), lambda l, j: (l, 0, j))],
        out_specs=pl.BlockSpec((1, rows, tn), lambda l, j: (l, 0, j)),
        out_shape=jax.ShapeDtypeStruct((depth, rows, n), F32),
        compiler_params=_params(("arbitrary", "arbitrary")),
        name="ada_params",
    )(cv, ada_w, ada_b.reshape(depth, 1, n))


def _mm_rope_kernel(x_ref, ss_ref, w_ref, ta_ref, tb_ref, tc_ref, o_ref, a_ref):
    @pl.when(pl.program_id(1) == 0)
    def _():
        a_ref[...] = (x_ref[...] * (1.0 + ss_ref[0, 1:2, :]) + ss_ref[0, 0:1, :]).astype(a_ref.dtype)

    a = a_ref[...]
    for half in range(2):
        t = 2 * pl.program_id(1) + half
        c0 = half * QTILE_W
        acc = _dot_nt(a, w_ref[c0:c0 + QTILE_W, :])
        is_rope = t <= T_KPE
        f = jnp.where(t < T_KPE, LOG2E * MLA_QK ** -0.5,
                      jnp.where(jnp.logical_and(t >= T_QNA, t < T_KNA), LOG2E * NA_HEAD_DIM ** -0.5, 1.0)
                      ).astype(F32)
        hi = acc[:, LANES:]
        rot = (hi * jnp.where(is_rope, ta_ref[...], 1.0)
               + pltpu.roll(hi, LANES - MLA_ROPE // 2, 1) * jnp.where(is_rope, tb_ref[...], 0.0)
               + pltpu.roll(hi, MLA_ROPE // 2, 1) * jnp.where(is_rope, tc_ref[...], 0.0))
        o_ref[:, c0:c0 + LANES] = (acc[:, :LANES] * f).astype(o_ref.dtype)
        o_ref[:, c0 + LANES:c0 + QTILE_W] = (rot * f).astype(o_ref.dtype)


def _mm_rope_call(x, ss, rows_per_group, w, tabs, tm):
    m, k = x.shape
    n = w.shape[0]
    tn = 2 * QTILE_W
    ntab = tabs[0].shape[0] // tm
    tpg = rows_per_group // tm
    tab_spec = pl.BlockSpec((tm, LANES), lambda i, j: (i % ntab, 0))
    return pl.pallas_call(
        _mm_rope_kernel,
        grid=(m // tm, n // tn),
        in_specs=[pl.BlockSpec((tm, k), lambda i, j: (i, 0)),
                  pl.BlockSpec((1, 2, k), lambda i, j: (i // tpg, 0, 0)),
                  pl.BlockSpec((tn, k), lambda i, j: (j, 0)),
                  tab_spec, tab_spec, tab_spec],
        out_specs=pl.BlockSpec((tm, tn), lambda i, j: (i, j)),
        out_shape=jax.ShapeDtypeStruct((m, n), BF16),
        scratch_shapes=[pltpu.VMEM((tm, k), BF16)],
        compiler_params=_params(("arbitrary", "arbitrary")),
        name="attn_in_proj",
    )(x, ss, w, *tabs)


def _rope_tables(n_tok):
    t = np.arange(n_tok)
    row = (t // GRID_W).astype(np.float32)
    col = (t % GRID_W).astype(np.float32)
    n_freq = MLA_ROPE // 4
    inv = (ROPE_THETA ** (-np.arange(n_freq, dtype=np.float32) / n_freq)).astype(np.float32)
    ang = np.concatenate([row[:, None] * inv, col[:, None] * inv], axis=1)
    cos, sin = np.cos(ang).astype(np.float32), np.sin(ang).astype(np.float32)
    half = MLA_ROPE // 2
    ta = np.zeros((n_tok, LANES), np.float32)
    tb = np.zeros((n_tok, LANES), np.float32)
    tc = np.zeros((n_tok, LANES), np.float32)
    ta[:, :half] = cos
    ta[:, half:2 * half] = cos
    tb[:, :half] = -sin
    tc[:, half:2 * half] = sin
    return jnp.asarray(ta), jnp.asarray(tb), jnp.asarray(tc)


def _identity_rope_tables(n_tok):
    ta = np.zeros((n_tok, LANES), np.float32)
    ta[:, :MLA_ROPE] = 1.0
    z = np.zeros((n_tok, LANES), np.float32)
    return jnp.asarray(ta), jnp.asarray(z), jnp.asarray(z)


def _w_in_prep_kernel(w_ref, o_ref):
    x = w_ref[0]
    zeros = lambda n: jnp.zeros((n, x.shape[1]), x.dtype)
    rows = []
    for h in range(MLA_HEADS):
        rows += [x[h * MLA_QK:(h + 1) * MLA_QK, :], zeros(QTILE_W - MLA_QK)]
    rows += [zeros(LANES), x[OFF_KPE:OFF_QNA, :], zeros(LANES - MLA_ROPE), zeros(QTILE_W),
             x[OFF_CKV:OFF_KPE, :], x[OFF_QNA:, :]]
    o_ref[...] = jnp.concatenate(rows, axis=0).astype(o_ref.dtype)


def _w_in_prep_call(w_in_t, layer, tk=256):
    _, n, d = w_in_t.shape
    return pl.pallas_call(
        _w_in_prep_kernel,
        grid=(d // tk,),
        in_specs=[pl.BlockSpec((1, n, tk), lambda i: (layer, 0, i))],
        out_specs=pl.BlockSpec((AB_PERM_W, tk), lambda i: (0, i)),
        out_shape=jax.ShapeDtypeStruct((AB_PERM_W, d), BF16),
        compiler_params=_params(("arbitrary",)),
        name="attn_w_in_prep",
    )(w_in_t)


def _kvup_kernel(x_ref, g_ref, w_ref, o_ref):
    x = x_ref[...].astype(F32)
    y = x * lax.rsqrt(jnp.mean(x * x, axis=-1, keepdims=True) + LN_EPS) * g_ref[...]
    o_ref[...] = _dot(y.astype(BF16), w_ref[...]).astype(o_ref.dtype)


def _kvup_call(p, g, w, tm):
    m = p.shape[0]
    r, n = w.shape
    return pl.pallas_call(
        _kvup_kernel,
        grid=(m // tm,),
        in_specs=[pl.BlockSpec((tm, r), lambda i: (i, T_CKV * QTILE_W // r)),
                  pl.BlockSpec((1, r), lambda i: (0, 0)),
                  pl.BlockSpec((r, n), lambda i: (0, 0))],
        out_specs=pl.BlockSpec((tm, n), lambda i: (i, 0)),
        out_shape=jax.ShapeDtypeStruct((m, n), BF16),
        compiler_params=_params(("arbitrary",)),
        name="kv_up",
    )(p, g.reshape(1, r), w)


def _permute_w_ukv(w):
    r = w.shape[0]
    w3 = w.reshape(r, MLA_HEADS, MLA_NOPE + MLA_V)
    return jnp.concatenate([w3[:, :, :MLA_NOPE].reshape(r, -1), w3[:, :, MLA_NOPE:].reshape(r, -1)],
                           axis=1).astype(BF16)


def _mla_kernel(q_ref, kn_ref, kp_ref, v_ref, kcn_ref, kcp_ref, vc_ref, o_ref, kcat, vcat):
    s_len = kn_ref.shape[0]

    @pl.when(pl.program_id(2) == 0)
    def _():
        kcat[:s_len, :LANES] = kn_ref[...]
        kcat[:s_len, LANES:] = kp_ref[...]
        kcat[s_len:, :LANES] = kcn_ref[...]
        kcat[s_len:, LANES:] = kcp_ref[...]
        vcat[:s_len, :] = v_ref[...]
        vcat[s_len:, :] = vc_ref[...]

    sub = q_ref.shape[0] // MLA_SUBTILES
    for r0 in range(0, q_ref.shape[0], sub):
        s = _dot_nt(q_ref[r0:r0 + sub, :], kcat[...])
        m = jnp.max(s, axis=-1, keepdims=True)
        p = jnp.exp2(s - m)
        l = jnp.sum(p, axis=-1, keepdims=True)
        o = _dot(p.astype(BF16), vcat[...])
        o_ref[r0:r0 + sub, :] = (o / l).astype(o_ref.dtype)


def _mla_call(p_lat, kv_lat, p_ctx, kv_ctx, batch, s_len, lc, tq=512):
    nq = s_len // tq
    kpe_blk = T_KPE * 2 + 1
    return pl.pallas_call(
        _mla_kernel,
        grid=(batch, MLA_HEADS, nq),
        in_specs=[pl.BlockSpec((tq, QTILE_W), lambda b, h, i: (b * nq + i, h)),
                  pl.BlockSpec((s_len, LANES), lambda b, h, i: (b, h)),
                  pl.BlockSpec((s_len, LANES), lambda b, h, i: (b, kpe_blk)),
                  pl.BlockSpec((s_len, LANES), lambda b, h, i: (b, MLA_HEADS + h)),
                  pl.BlockSpec((lc, LANES), lambda b, h, i: (b, h)),
                  pl.BlockSpec((lc, LANES), lambda b, h, i: (b, kpe_blk)),
                  pl.BlockSpec((lc, LANES), lambda b, h, i: (b, MLA_HEADS + h))],
        out_specs=pl.BlockSpec((tq, LANES), lambda b, h, i: (b * nq + i, h)),
        out_shape=jax.ShapeDtypeStruct((batch * s_len, MLA_HEADS * MLA_V), BF16),
        scratch_shapes=[pltpu.VMEM((s_len + lc, QTILE_W), BF16),
                        pltpu.VMEM((s_len + lc, LANES), BF16)],
        compiler_params=_params(("arbitrary", "arbitrary", "arbitrary")),
        name="mla_attention",
    )(p_lat, kv_lat, p_lat, kv_lat, kv_ctx, p_ctx, kv_ctx)


def _na_kernel(var_ref, q_ref, k_ref, v_ref, kc_ref, vc_ref, *rest, rows):
    del var_ref
    bias_refs, o_ref = rest[:-1], rest[-1]
    win = NA_WR * GRID_W
    tq = NA_G * GRID_W
    for i, bias_ref in enumerate(bias_refs):
        g = len(bias_refs) * pl.program_id(2) + i
        start_row = jnp.clip(NA_G * g - NA_KH // 2, 0, rows - NA_WR)
        start = pl.multiple_of(start_row * GRID_W, GRID_W)
        q = q_ref[i * tq:(i + 1) * tq, :]
        sw = _dot_nt(q, k_ref[pl.ds(start, win), :]) + bias_ref[0, 0]
        sc = _dot_nt(q, kc_ref[...])
        m = jnp.maximum(jnp.max(sw, axis=-1, keepdims=True), jnp.max(sc, axis=-1, keepdims=True))
        pw = jnp.exp2(sw - m)
        pc = jnp.exp2(sc - m)
        l = jnp.sum(pw, axis=-1, keepdims=True) + jnp.sum(pc, axis=-1, keepdims=True)
        o = _dot(pw.astype(BF16), v_ref[pl.ds(start, win), :]) + _dot(pc.astype(BF16), vc_ref[...])
        o_ref[i * tq:(i + 1) * tq, :] = (o / l).astype(o_ref.dtype)


def _na_tables(rows):
    kh = min(NA_KH, rows)
    ng = rows // NA_G
    qr = np.arange(NA_G)[:, None]
    kr = np.arange(NA_WR)[None, :]
    sel_rows = []
    for g in range(ng):
        start_row = int(np.clip(NA_G * g - NA_KH // 2, 0, rows - NA_WR))
        r = NA_G * g + qr
        krow = start_row + kr
        rs = np.clip(r - kh // 2, 0, rows - kh)
        ok = (krow >= rs) & (krow < rs + kh)
        dr = krow - r + (NA_KH - 1)
        sel_rows.append(np.stack([ok & (dr == d) for d in range(2 * NA_KH - 1)]))
    sel_rows = np.stack(sel_rows)
    uniq, inverse = np.unique(sel_rows.reshape(ng, -1), axis=0, return_inverse=True)
    er = uniq.reshape((-1,) + sel_rows.shape[1:]).astype(np.float32)
    qc = np.arange(GRID_W)[:, None]
    kc = np.arange(GRID_W)[None, :]
    cs = np.clip(qc - NA_KW // 2, 0, GRID_W - NA_KW)
    ok_c = (kc >= cs) & (kc < cs + NA_KW)
    dc = np.clip(kc - qc, -(NA_KW - 1), NA_KW - 1) + (NA_KW - 1)
    ec = np.stack([ok_c & (dc == e) for e in range(2 * NA_KW - 1)]).astype(np.float32)
    return inverse.reshape(-1).astype(np.int32), er, ec


def _na_bias(rpb, er, ec):
    hp = lax.Precision.HIGHEST
    t = jnp.einsum('hde,eqk->hdqk', rpb.astype(F32), jnp.asarray(ec), precision=hp)
    t = jnp.where(jnp.asarray(ec.sum(axis=0) > 0.5), t * LOG2E, NEG_INF)
    outside = jnp.full(t[:, 0].shape, NEG_INF, F32)
    d_of = np.where(er.sum(axis=1) > 0.5, er.argmax(axis=1), -1)
    block = lambda d: outside if d < 0 else t[:, d]
    return jnp.stack([
        jnp.concatenate([jnp.concatenate([block(int(d)) for d in row], axis=-1) for row in var], axis=-2)
        for var in d_of])


def _na_call(p_lat, p_ctx, bias, var_map, batch, s_len, lc):
    rows = s_len // GRID_W
    gps = NA_GROUPS_PER_STEP
    ns = rows // (NA_G * gps)
    tq = NA_G * GRID_W
    win = NA_WR * GRID_W
    qb, kb, vb = T_QNA * 2, T_KNA * 2, T_VNA * 2
    bias_spec = lambda i: pl.BlockSpec((1, 1, tq, win), lambda b, h, s, vm: (vm[gps * s + i], h, 0, 0))
    grid_spec = pltpu.PrefetchScalarGridSpec(
        num_scalar_prefetch=1,
        grid=(batch, NA_HEADS, ns),
        in_specs=[pl.BlockSpec((gps * tq, LANES), lambda b, h, s, vm: (b * ns + s, qb + h)),
                  pl.BlockSpec((s_len, LANES), lambda b, h, s, vm: (b, kb + h)),
                  pl.BlockSpec((s_len, LANES), lambda b, h, s, vm: (b, vb + h)),
                  pl.BlockSpec((lc, LANES), lambda b, h, s, vm: (b, kb + h)),
                  pl.BlockSpec((lc, LANES), lambda b, h, s, vm: (b, vb + h))]
                 + [bias_spec(i) for i in range(gps)],
        out_specs=pl.BlockSpec((gps * tq, LANES), lambda b, h, s, vm: (b * ns + s, h)),
    )
    return pl.pallas_call(
        functools.partial(_na_kernel, rows=rows),
        grid_spec=grid_spec,
        out_shape=jax.ShapeDtypeStruct((batch * s_len, NA_HEADS * NA_HEAD_DIM), BF16),
        compiler_params=_params(("arbitrary", "arbitrary", "arbitrary")),
        name="na_attention",
    )(var_map, p_lat, p_lat, p_lat, p_ctx, p_ctx, *([bias] * gps))


def _layer_norm(x, g, b):
    mu = jnp.mean(x, axis=-1, keepdims=True)
    xc = x - mu
    var = jnp.mean(xc * xc, axis=-1, keepdims=True)
    return xc * lax.rsqrt(var + LN_EPS) * g + b


def _outln_kernel(a1_ref, a2_ref, w_ref, h_ref, mod_ref, ln_ref, r_ref, h1_ref, u2_ref, lg_ref):
    half = a1_ref.shape[1]
    sub = a1_ref.shape[0] // OUTLN_SUBTILES
    for r0 in range(0, a1_ref.shape[0], sub):
        rs = slice(r0, r0 + sub)
        y = _dot(a1_ref[rs, :], w_ref[:half, :]) + _dot(a2_ref[rs, :], w_ref[half:, :])
        x = DEEPNORM_ALPHA * h_ref[rs, :] + mod_ref[0, 0:1, :] * y
        hn = _layer_norm(x, ln_ref[0:1, :], ln_ref[1:2, :])
        h1_ref[rs, :] = hn
        u = hn * (1.0 + mod_ref[0, 2:3, :]) + mod_ref[0, 1:2, :]
        u_bf = u.astype(BF16)
        u2_ref[rs, :] = u_bf
        lg_ref[rs, :] = _dot(u_bf, r_ref[...])


def _outln_call(a1, a2, w, h, mod, ln, router2, rows_per_group, tm=512):
    m, d = h.shape
    half = a1.shape[1]
    tpg = rows_per_group // tm
    return pl.pallas_call(
        _outln_kernel,
        grid=(m // tm,),
        in_specs=[pl.BlockSpec((tm, half), lambda i: (i, 0)),
                  pl.BlockSpec((tm, half), lambda i: (i, 0)),
                  _resident_spec((2 * half, d)),
                  pl.BlockSpec((tm, d), lambda i: (i, 0)),
                  pl.BlockSpec((1, 3, d), lambda i: (i // tpg, 0, 0)),
                  pl.BlockSpec((2, d), lambda i: (0, 0)),
                  _resident_spec((d, LANES))],
        out_specs=[pl.BlockSpec((tm, d), lambda i: (i, 0)),
                   pl.BlockSpec((tm, d), lambda i: (i, 0)),
                   pl.BlockSpec((tm, LANES), lambda i: (i, 0))],
        out_shape=[jax.ShapeDtypeStruct((m, d), F32),
                   jax.ShapeDtypeStruct((m, d), BF16),
                   jax.ShapeDtypeStruct((m, LANES), F32)],
        compiler_params=_params(("arbitrary",)),
        name="out_proj_postnorm",
    )(a1, a2, w, h, mod, ln, router2)


def _router_operand(router):
    d, e = router.shape
    return jnp.pad(router, ((0, 0), (0, LANES - e))).astype(BF16)


def _route_kernel(lg_ref, slot_ref, aff_ref, tri_ref, *, cap):
    n_tok = lg_ref.shape[2]

    @pl.when(pl.program_id(0) == 0)
    def _():
        chunk = 256
        for r0 in range(0, n_tok, chunk):
            r = r0 + lax.broadcasted_iota(I32, (chunk, n_tok), 0)
            c = lax.broadcasted_iota(I32, (chunk, n_tok), 1)
            tri_ref[r0:r0 + chunk, :] = jnp.where(r < c, 1.0, 0.0).astype(BF16)

    lg = lg_ref[0]
    ex = jnp.exp(lg - jnp.max(lg, axis=0, keepdims=True))
    aff = ex / jnp.sum(ex, axis=0, keepdims=True)
    bits = lax.bitcast_convert_type(aff, I32)
    n_e = lg.shape[0]
    count = lambda mask: jnp.sum(jnp.where(mask, 1.0, 0.0), axis=1, keepdims=True)

    def body(_, lohi):
        lo, hi = lohi
        mid = lo + jnp.right_shift(hi - lo, 1)
        ge = count(bits >= mid) >= cap
        return jnp.where(ge, mid, lo), jnp.where(ge, hi, mid)

    lo0 = jnp.zeros((n_e, 1), I32)
    hi0 = jnp.full((n_e, 1), 0x7F800000, I32)
    thr, _ = lax.fori_loop(0, 31, body, (lo0, hi0))
    gt = bits > thr
    eq = bits == thr
    need = cap - count(gt)
    pre_eq = _dot(jnp.where(eq, 1.0, 0.0).astype(BF16), tri_ref[...])
    sel = jnp.logical_or(gt, jnp.logical_and(eq, pre_eq < need))
    slot = _dot(jnp.where(sel, 1.0, 0.0).astype(BF16), tri_ref[...])
    slot_ref[0] = jnp.where(sel, slot.astype(I32), -1)
    aff_ref[0] = aff


def _route_call(lg_t, cap):
    b, e, n_tok = lg_t.shape
    spec = pl.BlockSpec((1, e, n_tok), lambda i: (i, 0, 0))
    return pl.pallas_call(
        functools.partial(_route_kernel, cap=cap),
        grid=(b,),
        in_specs=[spec],
        out_specs=[spec, spec],
        out_shape=[jax.ShapeDtypeStruct((b, e, n_tok), I32), jax.ShapeDtypeStruct((b, e, n_tok), F32)],
        scratch_shapes=[pltpu.VMEM((n_tok, n_tok), BF16)],
        compiler_params=_params(("arbitrary",)),
        name="ec_route",
    )(lg_t)


def _gather_kernel(slot_ref, u_ref, o_ref):
    cap = o_ref.shape[1]
    n_tok = u_ref.shape[0]
    j = lax.broadcasted_iota(I32, (cap, n_tok), 0)
    onehot = jnp.where(j == slot_ref[0, 0], 1.0, 0.0).astype(BF16)
    o_ref[0] = _dot(onehot, u_ref[...]).astype(o_ref.dtype)


def _gather_call(slot4, u2, cap):
    b, e, _, n_tok = slot4.shape
    d = u2.shape[1]
    return pl.pallas_call(
        _gather_kernel,
        grid=(b, e),
        in_specs=[pl.BlockSpec((1, 1, 1, n_tok), lambda i, j: (i, j, 0, 0)),
                  pl.BlockSpec((n_tok, d), lambda i, j: (i, 0))],
        out_specs=pl.BlockSpec((1, cap, d), lambda i, j: (j, i, 0)),
        out_shape=jax.ShapeDtypeStruct((e, b * cap, d), BF16),
        compiler_params=_params(("arbitrary", "arbitrary")),
        name="moe_gather",
    )(slot4, u2)


def _expert_hidden_kernel(x_ref, w1_ref, w3_ref, o_ref):
    tf = w1_ref.shape[3]
    h = _dot(x_ref[0], jnp.concatenate([w1_ref[0, 0].astype(BF16), w3_ref[0, 0].astype(BF16)], axis=1))
    h1 = h[:, :tf]
    o_ref[0] = ((h1 / (1.0 + jnp.exp(-h1))) * h[:, tf:]).astype(o_ref.dtype)


def _expert_out_kernel(hd_ref, w2_ref, o_ref):
    y = _dot(hd_ref[0], w2_ref[0, 0].astype(BF16)).astype(o_ref.dtype)
    cap = o_ref.shape[1]
    for b in range(o_ref.shape[0]):
        o_ref[b] = y[b * cap:(b + 1) * cap]


def _expert_call(xe, w1, w3, w2, layer, batch, tf=512, tn=1024):
    e, m, d = xe.shape
    cap = m // batch
    ff = w1.shape[3]
    hidden = pl.pallas_call(
        _expert_hidden_kernel,
        grid=(e, pl.cdiv(ff, tf)),
        in_specs=[pl.BlockSpec((1, m, d), lambda i, f: (i, 0, 0)),
                  pl.BlockSpec((1, 1, d, tf), lambda i, f: (layer, i, 0, f)),
                  pl.BlockSpec((1, 1, d, tf), lambda i, f: (layer, i, 0, f))],
        out_specs=pl.BlockSpec((1, m, tf), lambda i, f: (i, 0, f)),
        out_shape=jax.ShapeDtypeStruct((e, m, ff), BF16),
        compiler_params=_params(("arbitrary", "arbitrary")),
        name="moe_expert_hidden",
    )(xe, w1, w3)
    return pl.pallas_call(
        _expert_out_kernel,
        grid=(e, d // tn),
        in_specs=[pl.BlockSpec((1, m, ff), lambda i, j: (i, 0, 0)),
                  pl.BlockSpec((1, 1, ff, tn), lambda i, j: (layer, i, 0, j))],
        out_specs=pl.BlockSpec((batch, cap, tn), lambda i, j: (0, i, j)),
        out_shape=jax.ShapeDtypeStruct((batch, e * cap, d), BF16),
        compiler_params=_params(("arbitrary", "arbitrary")),
        name="moe_expert_out",
    )(hidden, w2)


def _scatter_kernel(slot_ref, aff_ref, ye_ref, h_ref, mod_ref, ln_ref, *rest, with_next):
    if with_next:
        nmod_ref, h2_ref, un_ref, pt_ref = rest
    else:
        h2_ref, pt_ref = rest
    n_e = slot_ref.shape[1]
    cap = ye_ref.shape[1] // n_e
    tt = h_ref.shape[0]
    j = lax.broadcasted_iota(I32, (cap, tt), 0)
    for e in range(n_e):
        pt_ref[e * cap:(e + 1) * cap, :] = jnp.where(j == slot_ref[0, e:e + 1, :],
                                                     aff_ref[0, e:e + 1, :], 0.0).astype(BF16)
    moe = _dot_tn(pt_ref[...], ye_ref[0])
    x = DEEPNORM_ALPHA * h_ref[...] + mod_ref[0] * moe
    hn = _layer_norm(x, ln_ref[0:1, :], ln_ref[1:2, :])
    h2_ref[...] = hn
    if with_next:
        un_ref[...] = (hn * (1.0 + nmod_ref[0, 1:2, :]) + nmod_ref[0, 0:1, :]).astype(un_ref.dtype)


def _scatter_call(slot, aff, ye, h1, gate, ln, next_ss, tt=256):
    b, e, n_tok = slot.shape
    m, d = h1.shape
    nt = n_tok // tt
    rows = ye.shape[1]
    with_next = next_ss is not None
    tok_spec = pl.BlockSpec((1, e, tt), lambda i, t: (i, 0, t))
    row_spec = pl.BlockSpec((tt, d), lambda i, t: (i * nt + t, 0))
    in_specs = [tok_spec, tok_spec,
                pl.BlockSpec((1, rows, d), lambda i, t: (i, 0, 0)),
                row_spec,
                pl.BlockSpec((1, 1, d), lambda i, t: (i, 0, 0)),
                pl.BlockSpec((2, d), lambda i, t: (0, 0))]
    args = [slot, aff, ye, h1, gate, ln]
    out_specs = [row_spec]
    out_shape = [jax.ShapeDtypeStruct((m, d), F32)]
    if with_next:
        in_specs.append(pl.BlockSpec((1, 2, d), lambda i, t: (i, 0, 0)))
        args.append(next_ss)
        out_specs.append(row_spec)
        out_shape.append(jax.ShapeDtypeStruct((m, d), BF16))
    res = pl.pallas_call(
        functools.partial(_scatter_kernel, with_next=with_next),
        grid=(b, nt),
        in_specs=in_specs,
        out_specs=out_specs,
        out_shape=out_shape,
        scratch_shapes=[pltpu.VMEM((rows, tt), BF16)],
        compiler_params=_params(("arbitrary", "arbitrary")),
        name="moe_combine_postnorm",
    )(*args)
    return res if with_next else (res[0], None)


def _ec_moe(h1, u2, logits, w1, w3, w2, layer, gate, ln, next_ss, batch, n_tok):
    cap = EC_CAPACITY_FACTOR * n_tok // N_EXPERTS
    lg_t = jnp.swapaxes(logits[:, :N_EXPERTS].reshape(batch, n_tok, N_EXPERTS), 1, 2)
    slot, aff = _route_call(lg_t, cap)
    xe = _gather_call(slot.reshape(batch, N_EXPERTS, 1, n_tok), u2, cap)
    ye = _expert_call(xe, w1, w3, w2, layer, batch)
    return _scatter_call(slot, aff, ye, h1, gate, ln, next_ss)


def _mm_kernel(a_ref, w_ref, o_ref):
    o_ref[...] = _dot(a_ref[...], w_ref[...]).astype(o_ref.dtype)


def _mm_call(a, w, out_dtype, tm=1024, tn=512):
    m, k = a.shape
    n = w.shape[1]
    return pl.pallas_call(
        _mm_kernel,
        grid=(m // tm, n // tn),
        in_specs=[pl.BlockSpec((tm, k), lambda i, j: (i, 0)),
                  pl.BlockSpec((k, tn), lambda i, j: (0, j))],
        out_specs=pl.BlockSpec((tm, tn), lambda i, j: (i, j)),
        out_shape=jax.ShapeDtypeStruct((m, n), out_dtype),
        compiler_params=_params(("arbitrary", "arbitrary")),
        name="matmul",
    )(a, w)


def _short_conv(p, w, b):
    x = p.astype(F32)
    n = x.shape[0]
    r = lax.broadcasted_iota(I32, x.shape, 0)
    prev = jnp.where(r == 0, 0.0, pltpu.roll(x, 1, 0))
    nxt = jnp.where(r == n - 1, 0.0, pltpu.roll(x, n - 1, 0))
    return b + w[0:1, :] * prev + w[1:2, :] * x + w[2:3, :] * nxt


def _sconv_kernel(p_ref, w_ref, b_ref, o_ref):
    o_ref[...] = _short_conv(p_ref[...], w_ref[...], b_ref[...]).astype(o_ref.dtype)


def _sconv_call(p, conv_w, conv_b, col0, ncols, out_dtype, batch, n_tok, tc=512):
    c0 = col0 // tc
    return pl.pallas_call(
        _sconv_kernel,
        grid=(batch, ncols // tc),
        in_specs=[pl.BlockSpec((n_tok, tc), lambda b, j: (b, c0 + j)),
                  pl.BlockSpec((HY_SHORT, tc), lambda b, j: (0, c0 + j)),
                  pl.BlockSpec((1, tc), lambda b, j: (0, c0 + j))],
        out_specs=pl.BlockSpec((n_tok, tc), lambda b, j: (b, j)),
        out_shape=jax.ShapeDtypeStruct((batch * n_tok, ncols), out_dtype),
        compiler_params=_params(("arbitrary", "arbitrary")),
        name="hyena_short_conv",
    )(p, conv_w, conv_b.reshape(1, -1))


def _dft_fwd_kernel(fc_ref, fs_ref, r1_ref, r2_ref, *rest, spectral):
    xr = _dot(fc_ref[...], r1_ref[...])
    xi = _dot(fs_ref[...], r2_ref[...])
    if spectral:
        h_ref, nyq_ref, o_ref = rest
        hr = h_ref[0, 0]
        hi = h_ref[0, 1]
        dc = lax.broadcasted_iota(I32, xr.shape, 0) == 0
        yr = xr * hr - jnp.where(dc, 0.0, xi * hi)
        yi = jnp.where(dc, xi * nyq_ref[...], xr * hi + xi * hr)
    else:
        (o_ref,) = rest
        yr, yi = xr, xi
    o_ref[0, 0] = yr.astype(o_ref.dtype)
    o_ref[0, 1] = yi.astype(o_ref.dtype)


def _resident_spec(shape):
    return pl.BlockSpec(shape, lambda *_: (0,) * len(shape), pipeline_mode=pl.Buffered(1))


def _dft_fwd_call(fc, fs, r1, r2, spec, out_dtype, batch, n_tok, tn=256):
    ncols = r1.shape[1]
    r_spec = pl.BlockSpec((n_tok, tn), lambda j, b: (b, j))
    in_specs = [_resident_spec(fc.shape), _resident_spec(fs.shape), r_spec, r_spec]
    args = [fc, fs, r1, r2]
    if spec is not None:
        h, nyq, hcol0 = spec
        c0 = hcol0 // tn
        in_specs += [pl.BlockSpec((1, 2, n_tok, tn), lambda j, b: (0, 0, 0, c0 + j)),
                     pl.BlockSpec((1, tn), lambda j, b: (0, c0 + j))]
        args += [h, nyq]
    return pl.pallas_call(
        functools.partial(_dft_fwd_kernel, spectral=spec is not None),
        grid=(ncols // tn, batch),
        in_specs=in_specs,
        out_specs=pl.BlockSpec((1, 2, n_tok, tn), lambda j, b: (b, 0, 0, j)),
        out_shape=jax.ShapeDtypeStruct((batch, 2, n_tok, ncols), out_dtype),
        compiler_params=_params(("arbitrary", "arbitrary")),
        name="dft_forward",
    )(*args)


def _dual_kernel(a1_ref, a2_ref, b1_ref, b2_ref, *rest, hyena, planes):
    b1 = b1_ref[0, 0] if planes else b1_ref[...]
    b2 = b2_ref[0, 0] if planes else b2_ref[...]
    y = _dot(a1_ref[...], b1) + _dot(a2_ref[...], b2)
    if hyena:
        p_ref, cw_ref, cb_ref, z_ref, skip_ref, o_ref = rest
        gate = _short_conv(p_ref[...], cw_ref[...], cb_ref[...])
        y = gate * (y + skip_ref[...] * z_ref[...].astype(F32))
    else:
        (o_ref,) = rest
    o_ref[...] = y.astype(o_ref.dtype)


def _dual_call(a1, a2, b1, b2, hy, batch, n_tok, ncols, tn=256):
    planes = b1.ndim == 4
    if planes:
        b_specs = [pl.BlockSpec((1, 1, n_tok, tn), lambda b, j: (b, 0, 0, j)),
                   pl.BlockSpec((1, 1, n_tok, tn), lambda b, j: (b, 1, 0, j))]
    else:
        b_specs = [pl.BlockSpec((n_tok, tn), lambda b, j: (b, j))] * 2
    in_specs = [_resident_spec(a1.shape), _resident_spec(a2.shape)] + b_specs
    args = [a1, a2, b1, b2]
    o_spec = pl.BlockSpec((n_tok, tn), lambda b, j: (b, j))
    if hy is not None:
        p, conv_w, conv_b, pcol0, z, skip = hy
        c0 = pcol0 // tn
        in_specs += [pl.BlockSpec((n_tok, tn), lambda b, j: (b, c0 + j)),
                     pl.BlockSpec((HY_SHORT, tn), lambda b, j: (0, c0 + j)),
                     pl.BlockSpec((1, tn), lambda b, j: (0, c0 + j)),
                     o_spec,
                     pl.BlockSpec((1, tn), lambda b, j: (0, j))]
        args += [p, conv_w, conv_b.reshape(1, -1), z, skip]
    return pl.pallas_call(
        functools.partial(_dual_kernel, hyena=hy is not None, planes=planes),
        grid=(batch, ncols // tn),
        in_specs=in_specs,
        out_specs=o_spec,
        out_shape=jax.ShapeDtypeStruct((batch * n_tok, ncols), BF16),
        compiler_params=_params(("arbitrary", "arbitrary")),
        name="seq_mix_matmul",
    )(*args)


def _fnet_kernel(cl_ref, msl_ref, x_ref, cw_ref, o_ref):
    x = x_ref[...]
    seq = jnp.concatenate([_dot(cl_ref[...], x).astype(BF16), _dot(msl_ref[...], x).astype(BF16)], axis=1)
    o_ref[...] = _dot(seq, cw_ref[...]).astype(o_ref.dtype)


def _fnet_call(cl, msl, p, cw, col0, batch, n_tok):
    gw = cw.shape[1]
    c0 = col0 // gw
    return pl.pallas_call(
        _fnet_kernel,
        grid=(batch, FN_GROUPS),
        in_specs=[_resident_spec(cl.shape), _resident_spec(msl.shape),
                  pl.BlockSpec((n_tok, gw), lambda b, g: (b, c0 + g)),
                  pl.BlockSpec(cw.shape, lambda b, g: (0, 0))],
        out_specs=pl.BlockSpec((n_tok, gw), lambda b, g: (b, g)),
        out_shape=jax.ShapeDtypeStruct((batch * n_tok, FN_GROUPS * gw), BF16),
        compiler_params=_params(("arbitrary", "arbitrary")),
        name="fnet_mix",
    )(cl, msl, p, cw)


def _cos_sin_matrix(n_rows, n_cols, period, split=64):
    r = np.arange(n_rows, dtype=np.int64)[:, None]
    c_hi = (np.arange(n_cols // split, dtype=np.int64) * split)[None, :]
    c_lo = np.arange(split, dtype=np.int64)[None, :]
    ang = lambda c: 2.0 * np.pi * ((r * c) % period).astype(np.float64) / period
    ca, sa = jnp.asarray(np.cos(ang(c_hi)), F32), jnp.asarray(np.sin(ang(c_hi)), F32)
    cb, sb = jnp.asarray(np.cos(ang(c_lo)), F32), jnp.asarray(np.sin(ang(c_lo)), F32)
    cos = ca[:, :, None] * cb[:, None, :] - sa[:, :, None] * sb[:, None, :]
    sin = sa[:, :, None] * cb[:, None, :] + ca[:, :, None] * sb[:, None, :]
    return cos.reshape(n_rows, n_cols), sin.reshape(n_rows, n_cols)


def _trig_kernel(ta_ref, tb_ref, ea_ref, eb_ref, *o_refs, mode, period, scale):
    sa_ca = _dot(ta_ref[...], ea_ref[...])
    sb_cb = _dot(tb_ref[...], eb_ref[...])
    nc = sa_ca.shape[1] // 2
    ca, sa, cb, sb = sa_ca[:, :nc], sa_ca[:, nc:], sb_cb[:, :nc], sb_cb[:, nc:]
    cos = ca * cb - sa * sb
    sin = sa * cb + ca * sb
    tm = cos.shape[0]
    row = pl.program_id(0) * tm + lax.broadcasted_iota(I32, cos.shape, 0)
    col = lax.broadcasted_iota(I32, cos.shape, 1)
    alt = lambda idx: (1 - 2 * jnp.bitwise_and(idx, 1)).astype(F32)
    if mode == "hyena":
        fc_ref, fs_ref, gc_ref, gs_ref = o_refs
        fc_ref[...] = cos.astype(fc_ref.dtype)
        fs_ref[...] = jnp.where(row == 0, alt(col), -sin).astype(fs_ref.dtype)
        gc_ref[...] = (cos * jnp.where(col == 0, 1.0 / period, 2.0 / period)).astype(gc_ref.dtype)
        gs_ref[...] = jnp.where(col == 0, alt(row) * (1.0 / period), sin * (-2.0 / period)).astype(gs_ref.dtype)
    else:
        c_ref, ms_ref = o_refs
        c_ref[...] = (cos * scale).astype(c_ref.dtype)
        ms_ref[...] = (sin * (-scale)).astype(ms_ref.dtype)


def _split3(x):
    bf = jnp.bfloat16
    x = np.asarray(x, np.float32)
    hi = x.astype(bf)
    r1 = x - hi.astype(np.float32)
    mid = r1.astype(bf)
    lo = (r1 - mid.astype(np.float32)).astype(bf)
    return [hi, mid, lo]


def _trig_call(n, period, mode, scale=1.0, split=64, tm=256):
    r = np.arange(n, dtype=np.int64)[:, None]
    c_hi = (np.arange(n // split, dtype=np.int64) * split)[None, :]
    c_lo = np.arange(split, dtype=np.int64)[None, :]
    ang = lambda c: 2.0 * np.pi * ((r * c) % period).astype(np.float64) / period
    def operands(a, col_group):
        terms = _split3(np.cos(a)) + _split3(np.sin(a))
        k = a.shape[1]
        spread = (col_group[None, :] == np.arange(k)[:, None]).astype(np.float32)
        zero = np.zeros_like(spread)
        e = np.concatenate([np.concatenate([spread, zero], 1)] * 3 + [np.concatenate([zero, spread], 1)] * 3, 0)
        return jnp.asarray(np.concatenate(terms, axis=1)), jnp.asarray(e, BF16)
    cols = np.arange(n)
    ta, ea = operands(ang(c_hi), cols // split)
    tb, eb = operands(ang(c_lo), cols % split)
    n_out = 4 if mode == "hyena" else 2
    o_spec = pl.BlockSpec((tm, n), lambda i: (i, 0))
    return pl.pallas_call(
        functools.partial(_trig_kernel, mode=mode, period=period, scale=scale),
        grid=(n // tm,),
        in_specs=[pl.BlockSpec((tm, ta.shape[1]), lambda i: (i, 0)),
                  pl.BlockSpec((tm, tb.shape[1]), lambda i: (i, 0)),
                  pl.BlockSpec(ea.shape, lambda i: (0, 0)),
                  pl.BlockSpec(eb.shape, lambda i: (0, 0))],
        out_specs=[o_spec] * n_out,
        out_shape=[jax.ShapeDtypeStruct((n, n), BF16)] * n_out,
        compiler_params=_params(("arbitrary",)),
        name="dft_tables",
    )(ta, tb, ea, eb)


def _hyena_dft_operands(n_tok):
    return _trig_call(n_tok, 2 * n_tok, "hyena")


def _fnet_operands(n_tok, gw):
    cl, msl = _trig_call(n_tok, n_tok, "fnet", scale=1.0 / math.sqrt(n_tok * gw))
    cw, sw = _cos_sin_matrix(gw, gw, gw)
    return cl, msl, jnp.concatenate([cw, sw], axis=0).astype(BF16)


def _filter_kernel(z_ref, w1_ref, b1_ref, f1_ref, w2_ref, b2_ref, f2_ref, w3f_ref, w3b_ref, dl_ref,
                   sum_ref, dif_ref, nyq_ref, hdn_ref):
    @pl.when(jnp.logical_and(pl.program_id(0) == 0, pl.program_id(1) == 0))
    def _():
        hdot = lambda a, b: jnp.dot(a, b, precision=lax.Precision.HIGHEST, preferred_element_type=F32)
        hdn = jnp.sin(f1_ref[...] * (hdot(z_ref[...], w1_ref[...]) + b1_ref[...]))
        hdn = jnp.sin(f2_ref[...] * (hdot(hdn, w2_ref[...]) + b2_ref[...]))
        hdn_ref[...] = hdn.astype(hdn_ref.dtype)

    n = hdn_ref.shape[0]
    row = lax.broadcasted_iota(I32, (n, dl_ref.shape[1]), 0)
    decay = jnp.exp(-(row.astype(F32) / (n - 1)) * dl_ref[...])
    hf = _dot(hdn_ref[...], w3f_ref[...].astype(BF16)) * decay
    hb = jnp.where(row == 0, 0.0, _dot(hdn_ref[...], w3b_ref[...].astype(BF16)) * decay)
    tot = hf + hb
    sum_ref[...] = tot.astype(sum_ref.dtype)
    dif_ref[...] = (hf - hb).astype(dif_ref.dtype)
    nyq_ref[...] = jnp.sum(jnp.where(jnp.bitwise_and(row, 1) == 0, tot, -tot), axis=0, keepdims=True)


def _filter_call(n_tok, fw1, fb1, ff1, fw2, fb2, ff2, fw3, tc=512):
    t01 = np.linspace(0.0, 1.0, n_tok, dtype=np.float32)
    w = (2.0 * math.pi * np.arange(n_tok, dtype=np.float32) / n_tok).astype(np.float32)
    bands = np.linspace(1e-4, HY_BANDS - 1, HY_BANDS, dtype=np.float32)
    z = np.concatenate([t01[:, None], np.cos(w[:, None] * bands), -np.sin(w[:, None] * bands)], -1)
    deltas = np.abs(np.linspace(HY_MIN_DECAY, HY_MAX_DECAY, HY_W, dtype=np.float32))[None, :]
    emb, hid = fw1.shape
    nc = HY_W // tc
    full = lambda shape: pl.BlockSpec(shape, lambda o, j: (0,) * len(shape))
    o_spec = pl.BlockSpec((n_tok, tc), lambda o, j: (0, o * nc + j))
    return pl.pallas_call(
        _filter_kernel,
        grid=(HY_ORDER, nc),
        in_specs=[full((n_tok, emb)), full((emb, hid)), full((1, hid)), full((1, hid)),
                  full((hid, hid)), full((1, hid)), full((1, hid)),
                  pl.BlockSpec((hid, tc), lambda o, j: (0, (2 * o) * nc + j)),
                  pl.BlockSpec((hid, tc), lambda o, j: (0, (2 * o + 1) * nc + j)),
                  pl.BlockSpec((1, tc), lambda o, j: (0, j))],
        out_specs=[o_spec, o_spec, pl.BlockSpec((1, tc), lambda o, j: (0, o * nc + j))],
        out_shape=[jax.ShapeDtypeStruct((n_tok, HY_ORDER * HY_W), BF16),
                   jax.ShapeDtypeStruct((n_tok, HY_ORDER * HY_W), BF16),
                   jax.ShapeDtypeStruct((1, HY_ORDER * HY_W), F32)],
        scratch_shapes=[pltpu.VMEM((n_tok, hid), BF16)],
        compiler_params=_params(("arbitrary", "arbitrary")),
        name="hyena_filters",
    )(jnp.asarray(z.astype(np.float32)), fw1, fb1.reshape(1, hid), ff1.reshape(1, hid),
      fw2, fb2.reshape(1, hid), ff2.reshape(1, hid), fw3, fw3, jnp.asarray(deltas))


def kernel(x, c, ctx, c_ctx, ada_w, ada_b, ln1_g, ln1_b, ln2_g, ln2_b, router, exp_w1, exp_w3, exp_w2,
           ab_w_in, ab_kv_norm, ab_w_ukv, ab_rpb, ab_w_out,
           cd_w_in, cd_conv_w, cd_conv_b, cd_filt_w1, cd_filt_b1, cd_filt_freq1, cd_filt_w2, cd_filt_b2,
           cd_filt_freq2, cd_filt_w3, cd_skip, cd_w_out):
    batch, n_tok, d = x.shape
    lc = ctx.shape[1]
    x2d = x.reshape(batch * n_tok, d)
    ctx2d = ctx.reshape(batch * lc, d)

    pad_rows = (-(batch + 1)) % 8
    cv = jnp.concatenate([c, c_ctx[None, :], jnp.zeros((pad_rows, d), F32)], axis=0)
    ada = _ada_call(cv, ada_w, ada_b)
    mods = [ada[i, :batch].reshape(batch, 6, d) for i in range(DEPTH)]
    ctx_ss = ada[0, batch, :2 * d].reshape(1, 2, d)

    w_in = _w_in_prep_call(jnp.swapaxes(ab_w_in, 1, 2), 0)
    p_lat = _mm_rope_call(x2d, mods[0][:, 0:2], n_tok, w_in, _rope_tables(n_tok), tm=1024)
    p_ctx = _mm_rope_call(ctx2d, ctx_ss, batch * lc, w_in, _identity_rope_tables(lc), tm=lc)
    w_ukv = _permute_w_ukv(ab_w_ukv[0])
    kv_lat = _kvup_call(p_lat, ab_kv_norm[0], w_ukv, tm=512)
    kv_ctx = _kvup_call(p_ctx, ab_kv_norm[0], w_ukv, tm=lc)
    a_out = _mla_call(p_lat, kv_lat, p_ctx, kv_ctx, batch, n_tok, lc)
    var_map, na_er, na_ec = _na_tables(n_tok // GRID_W)
    b_out = _na_call(p_lat, p_ctx, _na_bias(ab_rpb[0], na_er, na_ec), jnp.asarray(var_map),
                     batch, n_tok, lc)
    ln1 = jnp.stack([ln1_g, ln1_b], axis=1)
    ln2 = jnp.stack([ln2_g, ln2_b], axis=1)
    h1, u2, logits = _outln_call(a_out, b_out, ab_w_out[0].astype(BF16), x2d,
                                 mods[0][:, 2:5], ln1[0], _router_operand(router[0]), n_tok)
    h, u = _ec_moe(h1, u2, logits, exp_w1, exp_w3, exp_w2, 0, mods[0][:, 5:6], ln2[0],
                   mods[1][:, 0:2], batch, n_tok)

    p1 = _mm_call(u, cd_w_in[0].astype(BF16), BF16, tn=1024)
    s0 = _sconv_call(p1, cd_conv_w[0], cd_conv_b[0], 0, HY_W, BF16, batch, n_tok)
    h_sum, h_dif, h_nyq = _filter_call(n_tok, cd_filt_w1[0], cd_filt_b1[0], cd_filt_freq1[0], cd_filt_w2[0],
                                       cd_filt_b2[0], cd_filt_freq2[0], cd_filt_w3[0])
    fc, fs, gc, gs = _hyena_dft_operands(n_tok)
    h_spec = _dft_fwd_call(fc, fs, h_sum, h_dif, None, F32, 1, n_tok)
    z = s0
    for o in range(HY_ORDER):
        y_spec = _dft_fwd_call(fc, fs, z, z, (h_spec, h_nyq, o * HY_W), BF16, batch, n_tok)
        z = _dual_call(gc, gs, y_spec, y_spec,
                       (p1, cd_conv_w[0], cd_conv_b[0], (o + 1) * HY_W, z, cd_skip[0][o:o + 1]),
                       batch, n_tok, HY_W)
    cl, msl, cw = _fnet_operands(n_tok, FN_GROUP_W)
    y_fn = _fnet_call(cl, msl, p1, cw, HY_IN_W, batch, n_tok)
    h1, u2, logits = _outln_call(z, y_fn, cd_w_out[0].astype(BF16), h,
                                 mods[1][:, 2:5], ln1[1], _router_operand(router[1]), n_tok)
    h, _ = _ec_moe(h1, u2, logits, exp_w1, exp_w3, exp_w2, 1, mods[1][:, 5:6], ln2[1],
                   None, batch, n_tok)
    return h.reshape(batch, n_tok, d)
```

```python
import functools
import math

import numpy as np
import jax
import jax.numpy as jnp
from jax import lax
from jax.experimental import pallas as pl
from jax.experimental.pallas import tpu as pltpu

F32 = jnp.float32
BF16 = jnp.bfloat16
I32 = jnp.int32

D_MODEL = 2048
BATCH = 4
SEQ = 2048
DEPTH = 2
CTX_LEN = 256
GRID_W = 64
DEEPNORM_ALPHA = (2.0 * DEPTH) ** 0.25
LN_EPS = 1e-6
NEG_INF = -1e30
LOG2E = math.log2(math.e)

MLA_HEADS = 8
MLA_NOPE = 128
MLA_ROPE = 64
MLA_QK = MLA_NOPE + MLA_ROPE
MLA_V = 128
MLA_KV_RANK = 512
ROPE_THETA = 10000.0

NA_HEADS = 8
NA_HEAD_DIM = 128
NA_KH = 8
NA_KW = 16

OFF_CKV = MLA_HEADS * MLA_QK
OFF_KPE = OFF_CKV + MLA_KV_RANK
OFF_QNA = OFF_KPE + MLA_ROPE
OFF_KNA = OFF_QNA + NA_HEADS * NA_HEAD_DIM
OFF_VNA = OFF_KNA + NA_HEADS * NA_HEAD_DIM

HY_W = 1024
HY_ORDER = 2
HY_IN_W = (HY_ORDER + 1) * HY_W
HY_SHORT = 3
HY_BANDS = 16
HY_DECAY_TARGET = 1e-2
HY_MIN_DECAY = math.log(HY_DECAY_TARGET) / 1.5
HY_MAX_DECAY = math.log(HY_DECAY_TARGET) / 0.3

FN_W = 1024
FN_GROUPS = 4
FN_GROUP_W = FN_W // FN_GROUPS

N_EXPERTS = 16
EC_CAPACITY_FACTOR = 2
EXPERT_FF = 1408

LANES = 128
MXU_DIM = 256
VMEM_LIMIT = 56 * 1024 * 1024

QTILE_W = 2 * LANES
T_KPE = MLA_HEADS
T_CKV = T_KPE + 2
T_QNA = T_CKV + MLA_KV_RANK // QTILE_W
T_KNA = T_QNA + NA_HEADS * NA_HEAD_DIM // QTILE_W
T_VNA = T_KNA + NA_HEADS * NA_HEAD_DIM // QTILE_W
N_ABTILES = T_VNA + NA_HEADS * NA_HEAD_DIM // QTILE_W
AB_PERM_W = N_ABTILES * QTILE_W

MLA_SUBTILES = 2
OUTLN_SUBTILES = 2
MOE_CHUNK = 256
MOE_WIN = 128
NA_G = 4
NA_WR = NA_KH + NA_G - 1
NA_GROUPS_PER_STEP = 2


def _params(sem, vmem=VMEM_LIMIT):
    return pltpu.CompilerParams(dimension_semantics=sem, vmem_limit_bytes=vmem)


def _dot(a, b):
    return jnp.dot(a, b, preferred_element_type=F32)


def _dot_nt(a, b):
    return lax.dot_general(a, b, (((1,), (1,)), ((), ())), preferred_element_type=F32)


def _dot_tn(a, b):
    return lax.dot_general(a, b, (((0,), (0,)), ((), ())), preferred_element_type=F32)


def _split_bf16(x):
    hi = x.astype(BF16)
    lo = (x - hi.astype(F32)).astype(BF16)
    return hi, lo


def _ada_kernel(c_ref, w_ref, b_ref, o_ref):
    c = c_ref[...]
    a = c / (1.0 + jnp.exp(-c))
    a_hi, a_lo = _split_bf16(a)
    w_hi, w_lo = _split_bf16(w_ref[0])
    rows = a.shape[0]
    r = _dot(jnp.concatenate([a_hi, a_lo], axis=0), w_hi)
    o_ref[0] = r[:rows] + r[rows:] + _dot(a_hi, w_lo) + b_ref[0]


def _ada_call(cv, ada_w, ada_b, tn=1024):
    depth, d, n = ada_w.shape
    rows = cv.shape[0]
    return pl.pallas_call(
        _ada_kernel,
        grid=(depth, n // tn),
        in_specs=[pl.BlockSpec((rows, d), lambda l, j: (0, 0)),
                  pl.BlockSpec((1, d, tn), lambda l, j: (l, 0, j)),
                  pl.BlockSpec((1, 1, tn), lambda l, j: (l, 0, j))],
        out_specs=pl.BlockSpec((1, rows, tn), lambda l, j: (l, 0, j)),
        out_shape=jax.ShapeDtypeStruct((depth, rows, n), F32),
        compiler_params=_params(("arbitrary", "arbitrary")),
        name="ada_params",
    )(cv, ada_w, ada_b.reshape(depth, 1, n))


def _mm_rope_kernel(x_ref, ss_ref, w_ref, ta_ref, tb_ref, tc_ref, o_ref, a_ref):
    @pl.when(pl.program_id(1) == 0)
    def _():
        a_ref[...] = (x_ref[...] * (1.0 + ss_ref[0, 1:2, :]) + ss_ref[0, 0:1, :]).astype(a_ref.dtype)

    a = a_ref[...]
    for half in range(2):
        t = 2 * pl.program_id(1) + half
        c0 = half * QTILE_W
        acc = _dot_nt(a, w_ref[c0:c0 + QTILE_W, :])
        is_rope = t <= T_KPE
        f = jnp.where(t < T_KPE, LOG2E * MLA_QK ** -0.5,
                      jnp.where(jnp.logical_and(t >= T_QNA, t < T_KNA), LOG2E * NA_HEAD_DIM ** -0.5, 1.0)
                      ).astype(F32)
        hi = acc[:, LANES:]
        rot = (hi * jnp.where(is_rope, ta_ref[...], 1.0)
               + pltpu.roll(hi, LANES - MLA_ROPE // 2, 1) * jnp.where(is_rope, tb_ref[...], 0.0)
               + pltpu.roll(hi, MLA_ROPE // 2, 1) * jnp.where(is_rope, tc_ref[...], 0.0))
        o_ref[:, c0:c0 + LANES] = (acc[:, :LANES] * f).astype(o_ref.dtype)
        o_ref[:, c0 + LANES:c0 + QTILE_W] = (rot * f).astype(o_ref.dtype)


def _mm_rope_call(x, ss, rows_per_group, w, tabs, tm):
    m, k = x.shape
    n = w.shape[0]
    tn = 2 * QTILE_W
    ntab = tabs[0].shape[0] // tm
    tpg = rows_per_group // tm
    tab_spec = pl.BlockSpec((tm, LANES), lambda i, j: (i % ntab, 0))
    return pl.pallas_call(
        _mm_rope_kernel,
        grid=(m // tm, n // tn),
        in_specs=[pl.BlockSpec((tm, k), lambda i, j: (i, 0)),
                  pl.BlockSpec((1, 2, k), lambda i, j: (i // tpg, 0, 0)),
                  pl.BlockSpec((tn, k), lambda i, j: (j, 0)),
                  tab_spec, tab_spec, tab_spec],
        out_specs=pl.BlockSpec((tm, tn), lambda i, j: (i, j)),
        out_shape=jax.ShapeDtypeStruct((m, n), BF16),
        scratch_shapes=[pltpu.VMEM((tm, k), BF16)],
        compiler_params=_params(("arbitrary", "arbitrary")),
        name="attn_in_proj",
    )(x, ss, w, *tabs)


def _rope_tables(n_tok):
    t = np.arange(n_tok)
    row = (t // GRID_W).astype(np.float32)
    col = (t % GRID_W).astype(np.float32)
    n_freq = MLA_ROPE // 4
    inv = (ROPE_THETA ** (-np.arange(n_freq, dtype=np.float32) / n_freq)).astype(np.float32)
    ang = np.concatenate([row[:, None] * inv, col[:, None] * inv], axis=1)
    cos, sin = np.cos(ang).astype(np.float32), np.sin(ang).astype(np.float32)
    half = MLA_ROPE // 2
    ta = np.zeros((n_tok, LANES), np.float32)
    tb = np.zeros((n_tok, LANES), np.float32)
    tc = np.zeros((n_tok, LANES), np.float32)
    ta[:, :half] = cos
    ta[:, half:2 * half] = cos
    tb[:, :half] = -sin
    tc[:, half:2 * half] = sin
    return jnp.asarray(ta), jnp.asarray(tb), jnp.asarray(tc)


def _identity_rope_tables(n_tok):
    ta = np.zeros((n_tok, LANES), np.float32)
    ta[:, :MLA_ROPE] = 1.0
    z = np.zeros((n_tok, LANES), np.float32)
    return jnp.asarray(ta), jnp.asarray(z), jnp.asarray(z)


def _w_in_prep_kernel(w_ref, o_ref):
    x = w_ref[0]
    zeros = lambda n: jnp.zeros((n, x.shape[1]), x.dtype)
    rows = []
    for h in range(MLA_HEADS):
        rows += [x[h * MLA_QK:(h + 1) * MLA_QK, :], zeros(QTILE_W - MLA_QK)]
    rows += [zeros(LANES), x[OFF_KPE:OFF_QNA, :], zeros(LANES - MLA_ROPE), zeros(QTILE_W),
             x[OFF_CKV:OFF_KPE, :], x[OFF_QNA:, :]]
    o_ref[...] = jnp.concatenate(rows, axis=0).astype(o_ref.dtype)


def _w_in_prep_call(w_in_t, layer, tk=256):
    _, n, d = w_in_t.shape
    return pl.pallas_call(
        _w_in_prep_kernel,
        grid=(d // tk,),
        in_specs=[pl.BlockSpec((1, n, tk), lambda i: (layer, 0, i))],
        out_specs=pl.BlockSpec((AB_PERM_W, tk), lambda i: (0, i)),
        out_shape=jax.ShapeDtypeStruct((AB_PERM_W, d), BF16),
        compiler_params=_params(("arbitrary",)),
        name="attn_w_in_prep",
    )(w_in_t)


def _kvup_kernel(x_ref, g_ref, w_ref, o_ref):
    x = x_ref[...].astype(F32)
    y = x * lax.rsqrt(jnp.mean(x * x, axis=-1, keepdims=True) + LN_EPS) * g_ref[...]
    o_ref[...] = _dot(y.astype(BF16), w_ref[...]).astype(o_ref.dtype)


def _kvup_call(p, g, w, tm):
    m = p.shape[0]
    r, n = w.shape
    return pl.pallas_call(
        _kvup_kernel,
        grid=(m // tm,),
        in_specs=[pl.BlockSpec((tm, r), lambda i: (i, T_CKV * QTILE_W // r)),
                  pl.BlockSpec((1, r), lambda i: (0, 0)),
                  pl.BlockSpec((r, n), lambda i: (0, 0))],
        out_specs=pl.BlockSpec((tm, n), lambda i: (i, 0)),
        out_shape=jax.ShapeDtypeStruct((m, n), BF16),
        compiler_params=_params(("arbitrary",)),
        name="kv_up",
    )(p, g.reshape(1, r), w)


def _permute_w_ukv(w):
    r = w.shape[0]
    w3 = w.reshape(r, MLA_HEADS, MLA_NOPE + MLA_V)
    return jnp.concatenate([w3[:, :, :MLA_NOPE].reshape(r, -1), w3[:, :, MLA_NOPE:].reshape(r, -1)],
                           axis=1).astype(BF16)


def _mla_kernel(q_ref, kn_ref, kp_ref, v_ref, kcn_ref, kcp_ref, vc_ref, o_ref, kcat, vcat):
    s_len = kn_ref.shape[0]

    @pl.when(pl.program_id(2) == 0)
    def _():
        kcat[:s_len, :LANES] = kn_ref[...]
        kcat[:s_len, LANES:] = kp_ref[...]
        kcat[s_len:, :LANES] = kcn_ref[...]
        kcat[s_len:, LANES:] = kcp_ref[...]
        vcat[:s_len, :] = v_ref[...]
        vcat[s_len:, :] = vc_ref[...]

    sub = q_ref.shape[0] // MLA_SUBTILES
    for r0 in range(0, q_ref.shape[0], sub):
        s = _dot_nt(q_ref[r0:r0 + sub, :], kcat[...])
        m = jnp.max(s, axis=-1, keepdims=True)
        p = jnp.exp2(s - m)
        l = jnp.sum(p, axis=-1, keepdims=True)
        o = _dot(p.astype(BF16), vcat[...])
        o_ref[r0:r0 + sub, :] = (o / l).astype(o_ref.dtype)


def _mla_call(p_lat, kv_lat, p_ctx, kv_ctx, batch, s_len, lc, tq=512):
    nq = s_len // tq
    kpe_blk = T_KPE * 2 + 1
    return pl.pallas_call(
        _mla_kernel,
        grid=(batch, MLA_HEADS, nq),
        in_specs=[pl.BlockSpec((tq, QTILE_W), lambda b, h, i: (b * nq + i, h)),
                  pl.BlockSpec((s_len, LANES), lambda b, h, i: (b, h)),
                  pl.BlockSpec((s_len, LANES), lambda b, h, i: (b, kpe_blk)),
                  pl.BlockSpec((s_len, LANES), lambda b, h, i: (b, MLA_HEADS + h)),
                  pl.BlockSpec((lc, LANES), lambda b, h, i: (b, h)),
                  pl.BlockSpec((lc, LANES), lambda b, h, i: (b, kpe_blk)),
                  pl.BlockSpec((lc, LANES), lambda b, h, i: (b, MLA_HEADS + h))],
        out_specs=pl.BlockSpec((tq, LANES), lambda b, h, i: (b * nq + i, h)),
        out_shape=jax.ShapeDtypeStruct((batch * s_len, MLA_HEADS * MLA_V), BF16),
        scratch_shapes=[pltpu.VMEM((s_len + lc, QTILE_W), BF16),
                        pltpu.VMEM((s_len + lc, LANES), BF16)],
        compiler_params=_params(("arbitrary", "arbitrary", "arbitrary")),
        name="mla_attention",
    )(p_lat, kv_lat, p_lat, kv_lat, kv_ctx, p_ctx, kv_ctx)


def _na_kernel(var_ref, q_ref, k_ref, v_ref, kc_ref, vc_ref, *rest, rows):
    del var_ref
    bias_refs, o_ref = rest[:-1], rest[-1]
    win = NA_WR * GRID_W
    tq = NA_G * GRID_W
    for i, bias_ref in enumerate(bias_refs):
        g = len(bias_refs) * pl.program_id(2) + i
        start_row = jnp.clip(NA_G * g - NA_KH // 2, 0, rows - NA_WR)
        start = pl.multiple_of(start_row * GRID_W, GRID_W)
        q = q_ref[i * tq:(i + 1) * tq, :]
        sw = _dot_nt(q, k_ref[pl.ds(start, win), :]) + bias_ref[0, 0]
        sc = _dot_nt(q, kc_ref[...])
        m = jnp.maximum(jnp.max(sw, axis=-1, keepdims=True), jnp.max(sc, axis=-1, keepdims=True))
        pw = jnp.exp2(sw - m)
        pc = jnp.exp2(sc - m)
        l = jnp.sum(pw, axis=-1, keepdims=True) + jnp.sum(pc, axis=-1, keepdims=True)
        o = _dot(pw.astype(BF16), v_ref[pl.ds(start, win), :]) + _dot(pc.astype(BF16), vc_ref[...])
        o_ref[i * tq:(i + 1) * tq, :] = (o / l).astype(o_ref.dtype)


def _na_tables(rows):
    kh = min(NA_KH, rows)
    ng = rows // NA_G
    qr = np.arange(NA_G)[:, None]
    kr = np.arange(NA_WR)[None, :]
    sel_rows = []
    for g in range(ng):
        start_row = int(np.clip(NA_G * g - NA_KH // 2, 0, rows - NA_WR))
        r = NA_G * g + qr
        krow = start_row + kr
        rs = np.clip(r - kh // 2, 0, rows - kh)
        ok = (krow >= rs) & (krow < rs + kh)
        dr = krow - r + (NA_KH - 1)
        sel_rows.append(np.stack([ok & (dr == d) for d in range(2 * NA_KH - 1)]))
    sel_rows = np.stack(sel_rows)
    uniq, inverse = np.unique(sel_rows.reshape(ng, -1), axis=0, return_inverse=True)
    er = uniq.reshape((-1,) + sel_rows.shape[1:]).astype(np.float32)
    qc = np.arange(GRID_W)[:, None]
    kc = np.arange(GRID_W)[None, :]
    cs = np.clip(qc - NA_KW // 2, 0, GRID_W - NA_KW)
    ok_c = (kc >= cs) & (kc < cs + NA_KW)
    dc = np.clip(kc - qc, -(NA_KW - 1), NA_KW - 1) + (NA_KW - 1)
    ec = np.stack([ok_c & (dc == e) for e in range(2 * NA_KW - 1)]).astype(np.float32)
    return inverse.reshape(-1).astype(np.int32), er, ec


def _na_bias(rpb, er, ec):
    hp = lax.Precision.HIGHEST
    t = jnp.einsum('hde,eqk->hdqk', rpb.astype(F32), jnp.asarray(ec), precision=hp)
    t = jnp.where(jnp.asarray(ec.sum(axis=0) > 0.5), t * LOG2E, NEG_INF)
    outside = jnp.full(t[:, 0].shape, NEG_INF, F32)
    d_of = np.where(er.sum(axis=1) > 0.5, er.argmax(axis=1), -1)
    block = lambda d: outside if d < 0 else t[:, d]
    return jnp.stack([
        jnp.concatenate([jnp.concatenate([block(int(d)) for d in row], axis=-1) for row in var], axis=-2)
        for var in d_of])


def _na_call(p_lat, p_ctx, bias, var_map, batch, s_len, lc):
    rows = s_len // GRID_W
    gps = NA_GROUPS_PER_STEP
    ns = rows // (NA_G * gps)
    tq = NA_G * GRID_W
    win = NA_WR * GRID_W
    qb, kb, vb = T_QNA * 2, T_KNA * 2, T_VNA * 2
    bias_spec = lambda i: pl.BlockSpec((1, 1, tq, win), lambda b, h, s, vm: (vm[gps * s + i], h, 0, 0))
    grid_spec = pltpu.PrefetchScalarGridSpec(
        num_scalar_prefetch=1,
        grid=(batch, NA_HEADS, ns),
        in_specs=[pl.BlockSpec((gps * tq, LANES), lambda b, h, s, vm: (b * ns + s, qb + h)),
                  pl.BlockSpec((s_len, LANES), lambda b, h, s, vm: (b, kb + h)),
                  pl.BlockSpec((s_len, LANES), lambda b, h, s, vm: (b, vb + h)),
                  pl.BlockSpec((lc, LANES), lambda b, h, s, vm: (b, kb + h)),
                  pl.BlockSpec((lc, LANES), lambda b, h, s, vm: (b, vb + h))]
                 + [bias_spec(i) for i in range(gps)],
        out_specs=pl.BlockSpec((gps * tq, LANES), lambda b, h, s, vm: (b * ns + s, h)),
    )
    return pl.pallas_call(
        functools.partial(_na_kernel, rows=rows),
        grid_spec=grid_spec,
        out_shape=jax.ShapeDtypeStruct((batch * s_len, NA_HEADS * NA_HEAD_DIM), BF16),
        compiler_params=_params(("arbitrary", "arbitrary", "arbitrary")),
        name="na_attention",
    )(var_map, p_lat, p_lat, p_lat, p_ctx, p_ctx, *([bias] * gps))


def _layer_norm(x, g, b):
    mu = jnp.mean(x, axis=-1, keepdims=True)
    xc = x - mu
    var = jnp.mean(xc * xc, axis=-1, keepdims=True)
    return xc * lax.rsqrt(var + LN_EPS) * g + b


def _outln_kernel(a1_ref, a2_ref, w_ref, h_ref, mod_ref, ln_ref, r_ref, h1_ref, u2_ref, lg_ref):
    half = a1_ref.shape[1]
    sub = a1_ref.shape[0] // OUTLN_SUBTILES
    for r0 in range(0, a1_ref.shape[0], sub):
        rs = slice(r0, r0 + sub)
        y = _dot(a1_ref[rs, :], w_ref[:half, :]) + _dot(a2_ref[rs, :], w_ref[half:, :])
        x = DEEPNORM_ALPHA * h_ref[rs, :] + mod_ref[0, 0:1, :] * y
        hn = _layer_norm(x, ln_ref[0:1, :], ln_ref[1:2, :])
        h1_ref[rs, :] = hn
        u = hn * (1.0 + mod_ref[0, 2:3, :]) + mod_ref[0, 1:2, :]
        u_bf = u.astype(BF16)
        u2_ref[rs, :] = u_bf
        lg_ref[rs, :] = _dot(u_bf, r_ref[...])


def _outln_call(a1, a2, w, h, mod, ln, router2, rows_per_group, tm=512):
    m, d = h.shape
    half = a1.shape[1]
    tpg = rows_per_group // tm
    return pl.pallas_call(
        _outln_kernel,
        grid=(m // tm,),
        in_specs=[pl.BlockSpec((tm, half), lambda i: (i, 0)),
                  pl.BlockSpec((tm, half), lambda i: (i, 0)),
                  _resident_spec((2 * half, d)),
                  pl.BlockSpec((tm, d), lambda i: (i, 0)),
                  pl.BlockSpec((1, 3, d), lambda i: (i // tpg, 0, 0)),
                  pl.BlockSpec((2, d), lambda i: (0, 0)),
                  _resident_spec((d, LANES))],
        out_specs=[pl.BlockSpec((tm, d), lambda i: (i, 0)),
                   pl.BlockSpec((tm, d), lambda i: (i, 0)),
                   pl.BlockSpec((tm, LANES), lambda i: (i, 0))],
        out_shape=[jax.ShapeDtypeStruct((m, d), F32),
                   jax.ShapeDtypeStruct((m, d), BF16),
                   jax.ShapeDtypeStruct((m, LANES), F32)],
        compiler_params=_params(("arbitrary",)),
        name="out_proj_postnorm",
    )(a1, a2, w, h, mod, ln, router2)


def _router_operand(router):
    d, e = router.shape
    return jnp.pad(router, ((0, 0), (0, LANES - e))).astype(BF16)


def _route_kernel(lg_ref, slot_ref, aff_ref, cnt_ref, tri_ref, *, cap):
    n_tok = lg_ref.shape[2]

    @pl.when(pl.program_id(0) == 0)
    def _():
        chunk = 256
        for r0 in range(0, n_tok, chunk):
            r = r0 + lax.broadcasted_iota(I32, (chunk, n_tok), 0)
            c = lax.broadcasted_iota(I32, (chunk, n_tok), 1)
            tri_ref[r0:r0 + chunk, :] = jnp.where(r < c, 1.0, 0.0).astype(BF16)

    lg = lg_ref[0]
    ex = jnp.exp(lg - jnp.max(lg, axis=0, keepdims=True))
    aff = ex / jnp.sum(ex, axis=0, keepdims=True)
    bits = lax.bitcast_convert_type(aff, I32)
    n_e = lg.shape[0]
    count = lambda mask: jnp.sum(jnp.where(mask, 1.0, 0.0), axis=1, keepdims=True)

    def body(_, lohi):
        lo, hi = lohi
        mid = lo + jnp.right_shift(hi - lo, 1)
        ge = count(bits >= mid) >= cap
        return jnp.where(ge, mid, lo), jnp.where(ge, hi, mid)

    lo0 = jnp.zeros((n_e, 1), I32)
    hi0 = jnp.full((n_e, 1), 0x7F800000, I32)
    thr, _ = lax.fori_loop(0, 31, body, (lo0, hi0))
    gt = bits > thr
    eq = bits == thr
    need = cap - count(gt)
    pre_eq = _dot(jnp.where(eq, 1.0, 0.0).astype(BF16), tri_ref[...])
    sel = jnp.logical_or(gt, jnp.logical_and(eq, pre_eq < need))
    sel_bf = jnp.where(sel, 1.0, 0.0).astype(BF16)
    slot = _dot(sel_bf, tri_ref[...])
    slot_ref[0] = jnp.where(sel, slot.astype(I32), -1)
    aff_ref[0] = aff
    r = lax.broadcasted_iota(I32, (n_tok, LANES), 0)
    c = lax.broadcasted_iota(I32, (n_tok, LANES), 1)
    before = jnp.where(r < c * MOE_CHUNK, 1.0, 0.0).astype(BF16)
    cnt_ref[0] = _dot(sel_bf, before).astype(I32)


def _route_call(lg_t, cap):
    b, e, n_tok = lg_t.shape
    spec = pl.BlockSpec((1, e, n_tok), lambda i: (i, 0, 0))
    return pl.pallas_call(
        functools.partial(_route_kernel, cap=cap),
        grid=(b,),
        in_specs=[spec],
        out_specs=[spec, spec, pl.BlockSpec((1, e, LANES), lambda i: (i, 0, 0))],
        out_shape=[jax.ShapeDtypeStruct((b, e, n_tok), I32), jax.ShapeDtypeStruct((b, e, n_tok), F32),
                   jax.ShapeDtypeStruct((b, e, LANES), I32)],
        scratch_shapes=[pltpu.VMEM((n_tok, n_tok), BF16)],
        compiler_params=_params(("arbitrary",)),
        name="ec_route",
    )(lg_t)


def _slot_windows(cnt, cap, n_chunk):
    start = cnt[:, :, :n_chunk]
    end = cnt[:, :, 1:n_chunk + 1]
    a = jnp.minimum((start // 16) * 16, cap - MOE_WIN)
    fits = jnp.all(end <= a + MOE_WIN, axis=1)
    return jnp.swapaxes(a, 1, 2).reshape(-1).astype(I32), fits.reshape(-1).astype(I32)


def _gather_kernel(win_ref, fit_ref, slot_ref, u_ref, o_ref, *, group):
    b, kc = pl.program_id(0), pl.program_id(1)
    n_e, cap = o_ref.shape[0], o_ref.shape[1]
    chunk = u_ref.shape[0]
    step = b * pl.num_programs(1) + kc

    @pl.when(kc == 0)
    def _():
        o_ref[...] = jnp.zeros(o_ref.shape, o_ref.dtype)

    def place(rows, starts, group):
        row = lax.broadcasted_iota(I32, (rows, chunk), 0)
        for g0 in range(0, n_e, group):
            onehot = jnp.concatenate(
                [jnp.where(row + starts[e] == slot_ref[0, e:e + 1, :], 1.0, 0.0).astype(BF16)
                 for e in range(g0, g0 + group)], axis=0)
            got = _dot(onehot, u_ref[...]).astype(o_ref.dtype)
            for i, e in enumerate(range(g0, g0 + group)):
                o_ref[e, pl.ds(starts[e], rows), :] += got[i * rows:(i + 1) * rows]

    @pl.when(fit_ref[step] != 0)
    def _():
        place(MOE_WIN, [pl.multiple_of(win_ref[step * n_e + e], 16) for e in range(n_e)], group)

    @pl.when(fit_ref[step] == 0)
    def _():
        place(cap, [0] * n_e, max(1, group * MOE_WIN // cap))


def _gather_call(win, fit, slot, u2, cap, group=4):
    b, e, n_tok = slot.shape
    d = u2.shape[1]
    nc = n_tok // MOE_CHUNK
    grid_spec = pltpu.PrefetchScalarGridSpec(
        num_scalar_prefetch=2,
        grid=(b, nc),
        in_specs=[pl.BlockSpec((1, e, MOE_CHUNK), lambda i, k, w, f: (i, 0, k)),
                  pl.BlockSpec((MOE_CHUNK, d), lambda i, k, w, f: (i * nc + k, 0))],
        out_specs=pl.BlockSpec((e, cap, d), lambda i, k, w, f: (0, i, 0)),
    )
    return pl.pallas_call(
        functools.partial(_gather_kernel, group=group),
        grid_spec=grid_spec,
        out_shape=jax.ShapeDtypeStruct((e, b * cap, d), BF16),
        compiler_params=_params(("arbitrary", "arbitrary")),
        name="moe_gather",
    )(win, fit, slot, u2)


def _expert_hidden_kernel(x_ref, w1_ref, w3_ref, o_ref):
    tf = w1_ref.shape[3]
    h = _dot(x_ref[0], jnp.concatenate([w1_ref[0, 0].astype(BF16), w3_ref[0, 0].astype(BF16)], axis=1))
    h1 = h[:, :tf]
    o_ref[0] = ((h1 / (1.0 + jnp.exp(-h1))) * h[:, tf:]).astype(o_ref.dtype)


def _expert_out_kernel(hd_ref, w2_ref, o_ref):
    y = _dot(hd_ref[0], w2_ref[0, 0].astype(BF16)).astype(o_ref.dtype)
    cap = o_ref.shape[1]
    for b in range(o_ref.shape[0]):
        o_ref[b] = y[b * cap:(b + 1) * cap]


def _expert_call(xe, w1, w3, w2, layer, batch, tf=512, tn=1024):
    e, m, d = xe.shape
    cap = m // batch
    ff = w1.shape[3]
    hidden = pl.pallas_call(
        _expert_hidden_kernel,
        grid=(e, pl.cdiv(ff, tf)),
        in_specs=[pl.BlockSpec((1, m, d), lambda i, f: (i, 0, 0)),
                  pl.BlockSpec((1, 1, d, tf), lambda i, f: (layer, i, 0, f)),
                  pl.BlockSpec((1, 1, d, tf), lambda i, f: (layer, i, 0, f))],
        out_specs=pl.BlockSpec((1, m, tf), lambda i, f: (i, 0, f)),
        out_shape=jax.ShapeDtypeStruct((e, m, ff), BF16),
        compiler_params=_params(("arbitrary", "arbitrary")),
        name="moe_expert_hidden",
    )(xe, w1, w3)
    return pl.pallas_call(
        _expert_out_kernel,
        grid=(e, d // tn),
        in_specs=[pl.BlockSpec((1, m, ff), lambda i, j: (i, 0, 0)),
                  pl.BlockSpec((1, 1, ff, tn), lambda i, j: (layer, i, 0, j))],
        out_specs=pl.BlockSpec((batch, cap, tn), lambda i, j: (0, i, j)),
        out_shape=jax.ShapeDtypeStruct((batch, e * cap, d), BF16),
        compiler_params=_params(("arbitrary", "arbitrary")),
        name="moe_expert_out",
    )(hidden, w2)


def _scatter_kernel(win_ref, fit_ref, slot_ref, aff_ref, ye_ref, h_ref, mod_ref, ln_ref, *rest, with_next):
    if with_next:
        nmod_ref, h2_ref, un_ref, pt_ref, yew_ref, moe_ref = rest
    else:
        h2_ref, pt_ref, yew_ref, moe_ref = rest
    n_e = slot_ref.shape[1]
    cap = ye_ref.shape[1] // n_e
    tt = h_ref.shape[0]
    step = pl.program_id(0) * pl.num_programs(1) + pl.program_id(1)

    def gates(rows, e, start):
        row = lax.broadcasted_iota(I32, (rows, tt), 0)
        return jnp.where(row + start == slot_ref[0, e:e + 1, :], aff_ref[0, e:e + 1, :], 0.0).astype(BF16)

    @pl.when(fit_ref[step] != 0)
    def _():
        for e in range(n_e):
            start = pl.multiple_of(win_ref[step * n_e + e], 16)
            pt_ref[e * MOE_WIN:(e + 1) * MOE_WIN, :] = gates(MOE_WIN, e, start)
            yew_ref[e * MOE_WIN:(e + 1) * MOE_WIN, :] = ye_ref[0, pl.ds(e * cap + start, MOE_WIN), :]
        moe_ref[...] = _dot_tn(pt_ref[:n_e * MOE_WIN, :], yew_ref[...])

    @pl.when(fit_ref[step] == 0)
    def _():
        for e in range(n_e):
            pt_ref[e * cap:(e + 1) * cap, :] = gates(cap, e, 0)
        moe_ref[...] = _dot_tn(pt_ref[...], ye_ref[0])

    x = DEEPNORM_ALPHA * h_ref[...] + mod_ref[0] * moe_ref[...]
    hn = _layer_norm(x, ln_ref[0:1, :], ln_ref[1:2, :])
    h2_ref[...] = hn
    if with_next:
        un_ref[...] = (hn * (1.0 + nmod_ref[0, 1:2, :]) + nmod_ref[0, 0:1, :]).astype(un_ref.dtype)


def _scatter_call(win, fit, slot, aff, ye, h1, gate, ln, next_ss):
    b, e, n_tok = slot.shape
    m, d = h1.shape
    tt = MOE_CHUNK
    nt = n_tok // tt
    rows = ye.shape[1]
    with_next = next_ss is not None
    tok_spec = pl.BlockSpec((1, e, tt), lambda i, t, w, f: (i, 0, t))
    row_spec = pl.BlockSpec((tt, d), lambda i, t, w, f: (i * nt + t, 0))
    in_specs = [tok_spec, tok_spec,
                pl.BlockSpec((1, rows, d), lambda i, t, w, f: (i, 0, 0), pipeline_mode=pl.Buffered(1)),
                row_spec,
                pl.BlockSpec((1, 1, d), lambda i, t, w, f: (i, 0, 0)),
                pl.BlockSpec((2, d), lambda i, t, w, f: (0, 0))]
    args = [slot, aff, ye, h1, gate, ln]
    out_specs = [row_spec]
    out_shape = [jax.ShapeDtypeStruct((m, d), F32)]
    if with_next:
        in_specs.append(pl.BlockSpec((1, 2, d), lambda i, t, w, f: (i, 0, 0)))
        args.append(next_ss)
        out_specs.append(row_spec)
        out_shape.append(jax.ShapeDtypeStruct((m, d), BF16))
    grid_spec = pltpu.PrefetchScalarGridSpec(
        num_scalar_prefetch=2,
        grid=(b, nt),
        in_specs=in_specs,
        out_specs=out_specs,
        scratch_shapes=[pltpu.VMEM((rows, tt), BF16),
                        pltpu.VMEM((e * MOE_WIN, d), BF16),
                        pltpu.VMEM((tt, d), F32)],
    )
    res = pl.pallas_call(
        functools.partial(_scatter_kernel, with_next=with_next),
        grid_spec=grid_spec,
        out_shape=out_shape,
        compiler_params=_params(("arbitrary", "arbitrary")),
        name="moe_combine_postnorm",
    )(win, fit, *args)
    return res if with_next else (res[0], None)


def _ec_moe(h1, u2, logits, w1, w3, w2, layer, gate, ln, next_ss, batch, n_tok):
    cap = EC_CAPACITY_FACTOR * n_tok // N_EXPERTS
    lg_t = jnp.swapaxes(logits[:, :N_EXPERTS].reshape(batch, n_tok, N_EXPERTS), 1, 2)
    slot, aff, cnt = _route_call(lg_t, cap)
    win, fit = _slot_windows(cnt, cap, n_tok // MOE_CHUNK)
    xe = _gather_call(win, fit, slot, u2, cap)
    ye = _expert_call(xe, w1, w3, w2, layer, batch)
    return _scatter_call(win, fit, slot, aff, ye, h1, gate, ln, next_ss)


def _mm_kernel(a_ref, w_ref, o_ref):
    o_ref[...] = _dot(a_ref[...], w_ref[...]).astype(o_ref.dtype)


def _mm_call(a, w, out_dtype, tm=1024, tn=512):
    m, k = a.shape
    n = w.shape[1]
    return pl.pallas_call(
        _mm_kernel,
        grid=(m // tm, n // tn),
        in_specs=[pl.BlockSpec((tm, k), lambda i, j: (i, 0)),
                  pl.BlockSpec((k, tn), lambda i, j: (0, j))],
        out_specs=pl.BlockSpec((tm, tn), lambda i, j: (i, j)),
        out_shape=jax.ShapeDtypeStruct((m, n), out_dtype),
        compiler_params=_params(("arbitrary", "arbitrary")),
        name="matmul",
    )(a, w)


def _short_conv(p, w, b):
    x = p.astype(F32)
    n = x.shape[0]
    r = lax.broadcasted_iota(I32, x.shape, 0)
    prev = jnp.where(r == 0, 0.0, pltpu.roll(x, 1, 0))
    nxt = jnp.where(r == n - 1, 0.0, pltpu.roll(x, n - 1, 0))
    return b + w[0:1, :] * prev + w[1:2, :] * x + w[2:3, :] * nxt


def _sconv_kernel(p_ref, w_ref, b_ref, o_ref):
    o_ref[...] = _short_conv(p_ref[...], w_ref[...], b_ref[...]).astype(o_ref.dtype)


def _sconv_call(p, conv_w, conv_b, col0, ncols, out_dtype, batch, n_tok, tc=512):
    c0 = col0 // tc
    return pl.pallas_call(
        _sconv_kernel,
        grid=(batch, ncols // tc),
        in_specs=[pl.BlockSpec((n_tok, tc), lambda b, j: (b, c0 + j)),
                  pl.BlockSpec((HY_SHORT, tc), lambda b, j: (0, c0 + j)),
                  pl.BlockSpec((1, tc), lambda b, j: (0, c0 + j))],
        out_specs=pl.BlockSpec((n_tok, tc), lambda b, j: (b, j)),
        out_shape=jax.ShapeDtypeStruct((batch * n_tok, ncols), out_dtype),
        compiler_params=_params(("arbitrary", "arbitrary")),
        name="hyena_short_conv",
    )(p, conv_w, conv_b.reshape(1, -1))


def _dft_fwd_kernel(fc_ref, fs_ref, r1_ref, r2_ref, *rest, spectral):
    xr = _dot(fc_ref[...], r1_ref[...])
    xi = _dot(fs_ref[...], r2_ref[...])
    if spectral:
        h_ref, nyq_ref, o_ref = rest
        hr = h_ref[0, 0]
        hi = h_ref[0, 1]
        dc = lax.broadcasted_iota(I32, xr.shape, 0) == 0
        yr = xr * hr - jnp.where(dc, 0.0, xi * hi)
        yi = jnp.where(dc, xi * nyq_ref[...], xr * hi + xi * hr)
    else:
        (o_ref,) = rest
        yr, yi = xr, xi
    o_ref[0, 0] = yr.astype(o_ref.dtype)
    o_ref[0, 1] = yi.astype(o_ref.dtype)


def _resident_spec(shape):
    return pl.BlockSpec(shape, lambda *_: (0,) * len(shape), pipeline_mode=pl.Buffered(1))


def _dft_fwd_call(fc, fs, r1, r2, spec, out_dtype, batch, n_tok, tn=256):
    ncols = r1.shape[1]
    r_spec = pl.BlockSpec((n_tok, tn), lambda j, b: (b, j))
    in_specs = [_resident_spec(fc.shape), _resident_spec(fs.shape), r_spec, r_spec]
    args = [fc, fs, r1, r2]
    if spec is not None:
        h, nyq, hcol0 = spec
        c0 = hcol0 // tn
        in_specs += [pl.BlockSpec((1, 2, n_tok, tn), lambda j, b: (0, 0, 0, c0 + j)),
                     pl.BlockSpec((1, tn), lambda j, b: (0, c0 + j))]
        args += [h, nyq]
    return pl.pallas_call(
        functools.partial(_dft_fwd_kernel, spectral=spec is not None),
        grid=(ncols // tn, batch),
        in_specs=in_specs,
        out_specs=pl.BlockSpec((1, 2, n_tok, tn), lambda j, b: (b, 0, 0, j)),
        out_shape=jax.ShapeDtypeStruct((batch, 2, n_tok, ncols), out_dtype),
        compiler_params=_params(("arbitrary", "arbitrary")),
        name="dft_forward",
    )(*args)


def _dual_kernel(a1_ref, a2_ref, b1_ref, b2_ref, *rest, hyena, planes):
    b1 = b1_ref[0, 0] if planes else b1_ref[...]
    b2 = b2_ref[0, 0] if planes else b2_ref[...]
    y = _dot(a1_ref[...], b1) + _dot(a2_ref[...], b2)
    if hyena:
        p_ref, cw_ref, cb_ref, z_ref, skip_ref, o_ref = rest
        gate = _short_conv(p_ref[...], cw_ref[...], cb_ref[...])
        y = gate * (y + skip_ref[...] * z_ref[...].astype(F32))
    else:
        (o_ref,) = rest
    o_ref[...] = y.astype(o_ref.dtype)


def _dual_call(a1, a2, b1, b2, hy, batch, n_tok, ncols, tn=256):
    planes = b1.ndim == 4
    if planes:
        b_specs = [pl.BlockSpec((1, 1, n_tok, tn), lambda b, j: (b, 0, 0, j)),
                   pl.BlockSpec((1, 1, n_tok, tn), lambda b, j: (b, 1, 0, j))]
    else:
        b_specs = [pl.BlockSpec((n_tok, tn), lambda b, j: (b, j))] * 2
    in_specs = [_resident_spec(a1.shape), _resident_spec(a2.shape)] + b_specs
    args = [a1, a2, b1, b2]
    o_spec = pl.BlockSpec((n_tok, tn), lambda b, j: (b, j))
    if hy is not None:
        p, conv_w, conv_b, pcol0, z, skip = hy
        c0 = pcol0 // tn
        in_specs += [pl.BlockSpec((n_tok, tn), lambda b, j: (b, c0 + j)),
                     pl.BlockSpec((HY_SHORT, tn), lambda b, j: (0, c0 + j)),
                     pl.BlockSpec((1, tn), lambda b, j: (0, c0 + j)),
                     o_spec,
                     pl.BlockSpec((1, tn), lambda b, j: (0, j))]
        args += [p, conv_w, conv_b.reshape(1, -1), z, skip]
    return pl.pallas_call(
        functools.partial(_dual_kernel, hyena=hy is not None, planes=planes),
        grid=(batch, ncols // tn),
        in_specs=in_specs,
        out_specs=o_spec,
        out_shape=jax.ShapeDtypeStruct((batch * n_tok, ncols), BF16),
        compiler_params=_params(("arbitrary", "arbitrary")),
        name="seq_mix_matmul",
    )(*args)


def _fnet_kernel(cl_ref, msl_ref, x_ref, cw_ref, o_ref):
    x = x_ref[...]
    seq = jnp.concatenate([_dot(cl_ref[...], x).astype(BF16), _dot(msl_ref[...], x).astype(BF16)], axis=1)
    o_ref[...] = _dot(seq, cw_ref[...]).astype(o_ref.dtype)


def _fnet_call(cl, msl, p, cw, col0, batch, n_tok):
    gw = cw.shape[1]
    c0 = col0 // gw
    return pl.pallas_call(
        _fnet_kernel,
        grid=(batch, FN_GROUPS),
        in_specs=[_resident_spec(cl.shape), _resident_spec(msl.shape),
                  pl.BlockSpec((n_tok, gw), lambda b, g: (b, c0 + g)),
                  pl.BlockSpec(cw.shape, lambda b, g: (0, 0))],
        out_specs=pl.BlockSpec((n_tok, gw), lambda b, g: (b, g)),
        out_shape=jax.ShapeDtypeStruct((batch * n_tok, FN_GROUPS * gw), BF16),
        compiler_params=_params(("arbitrary", "arbitrary")),
        name="fnet_mix",
    )(cl, msl, p, cw)


def _cos_sin_matrix(n_rows, n_cols, period, split=64):
    r = np.arange(n_rows, dtype=np.int64)[:, None]
    c_hi = (np.arange(n_cols // split, dtype=np.int64) * split)[None, :]
    c_lo = np.arange(split, dtype=np.int64)[None, :]
    ang = lambda c: 2.0 * np.pi * ((r * c) % period).astype(np.float64) / period
    ca, sa = jnp.asarray(np.cos(ang(c_hi)), F32), jnp.asarray(np.sin(ang(c_hi)), F32)
    cb, sb = jnp.asarray(np.cos(ang(c_lo)), F32), jnp.asarray(np.sin(ang(c_lo)), F32)
    cos = ca[:, :, None] * cb[:, None, :] - sa[:, :, None] * sb[:, None, :]
    sin = sa[:, :, None] * cb[:, None, :] + ca[:, :, None] * sb[:, None, :]
    return cos.reshape(n_rows, n_cols), sin.reshape(n_rows, n_cols)


def _trig_kernel(ta_ref, tb_ref, ea_ref, eb_ref, *o_refs, mode, period, scale):
    sa_ca = _dot(ta_ref[...], ea_ref[...])
    sb_cb = _dot(tb_ref[...], eb_ref[...])
    nc = sa_ca.shape[1] // 2
    ca, sa, cb, sb = sa_ca[:, :nc], sa_ca[:, nc:], sb_cb[:, :nc], sb_cb[:, nc:]
    cos = ca * cb - sa * sb
    sin = sa * cb + ca * sb
    tm = cos.shape[0]
    row = pl.program_id(0) * tm + lax.broadcasted_iota(I32, cos.shape, 0)
    col = lax.broadcasted_iota(I32, cos.shape, 1)
    alt = lambda idx: (1 - 2 * jnp.bitwise_and(idx, 1)).astype(F32)
    if mode == "hyena":
        fc_ref, fs_ref, gc_ref, gs_ref = o_refs
        fc_ref[...] = cos.astype(fc_ref.dtype)
        fs_ref[...] = jnp.where(row == 0, alt(col), -sin).astype(fs_ref.dtype)
        gc_ref[...] = (cos * jnp.where(col == 0, 1.0 / period, 2.0 / period)).astype(gc_ref.dtype)
        gs_ref[...] = jnp.where(col == 0, alt(row) * (1.0 / period), sin * (-2.0 / period)).astype(gs_ref.dtype)
    else:
        c_ref, ms_ref = o_refs
        c_ref[...] = (cos * scale).astype(c_ref.dtype)
        ms_ref[...] = (sin * (-scale)).astype(ms_ref.dtype)


def _split3(x):
    bf = jnp.bfloat16
    x = np.asarray(x, np.float32)
    hi = x.astype(bf)
    r1 = x - hi.astype(np.float32)
    mid = r1.astype(bf)
    lo = (r1 - mid.astype(np.float32)).astype(bf)
    return [hi, mid, lo]


def _trig_call(n, period, mode, scale=1.0, split=64, tm=256):
    r = np.arange(n, dtype=np.int64)[:, None]
    c_hi = (np.arange(n // split, dtype=np.int64) * split)[None, :]
    c_lo = np.arange(split, dtype=np.int64)[None, :]
    ang = lambda c: 2.0 * np.pi * ((r * c) % period).astype(np.float64) / period
    def operands(a, col_group):
        terms = _split3(np.cos(a)) + _split3(np.sin(a))
        k = a.shape[1]
        spread = (col_group[None, :] == np.arange(k)[:, None]).astype(np.float32)
        zero = np.zeros_like(spread)
        e = np.concatenate([np.concatenate([spread, zero], 1)] * 3 + [np.concatenate([zero, spread], 1)] * 3, 0)
        return jnp.asarray(np.concatenate(terms, axis=1)), jnp.asarray(e, BF16)
    cols = np.arange(n)
    ta, ea = operands(ang(c_hi), cols // split)
    tb, eb = operands(ang(c_lo), cols % split)
    n_out = 4 if mode == "hyena" else 2
    o_spec = pl.BlockSpec((tm, n), lambda i: (i, 0))
    return pl.pallas_call(
        functools.partial(_trig_kernel, mode=mode, period=period, scale=scale),
        grid=(n // tm,),
        in_specs=[pl.BlockSpec((tm, ta.shape[1]), lambda i: (i, 0)),
                  pl.BlockSpec((tm, tb.shape[1]), lambda i: (i, 0)),
                  pl.BlockSpec(ea.shape, lambda i: (0, 0)),
                  pl.BlockSpec(eb.shape, lambda i: (0, 0))],
        out_specs=[o_spec] * n_out,
        out_shape=[jax.ShapeDtypeStruct((n, n), BF16)] * n_out,
        compiler_params=_params(("arbitrary",)),
        name="dft_tables",
    )(ta, tb, ea, eb)


def _hyena_dft_operands(n_tok):
    return _trig_call(n_tok, 2 * n_tok, "hyena")


def _fnet_operands(n_tok, gw):
    cl, msl = _trig_call(n_tok, n_tok, "fnet", scale=1.0 / math.sqrt(n_tok * gw))
    cw, sw = _cos_sin_matrix(gw, gw, gw)
    return cl, msl, jnp.concatenate([cw, sw], axis=0).astype(BF16)


def _filter_kernel(z_ref, w1_ref, b1_ref, f1_ref, w2_ref, b2_ref, f2_ref, w3f_ref, w3b_ref, dl_ref,
                   sum_ref, dif_ref, nyq_ref, hdn_ref):
    @pl.when(jnp.logical_and(pl.program_id(0) == 0, pl.program_id(1) == 0))
    def _():
        hdot = lambda a, b: jnp.dot(a, b, precision=lax.Precision.HIGHEST, preferred_element_type=F32)
        hdn = jnp.sin(f1_ref[...] * (hdot(z_ref[...], w1_ref[...]) + b1_ref[...]))
        hdn = jnp.sin(f2_ref[...] * (hdot(hdn, w2_ref[...]) + b2_ref[...]))
        hdn_ref[...] = hdn.astype(hdn_ref.dtype)

    n = hdn_ref.shape[0]
    row = lax.broadcasted_iota(I32, (n, dl_ref.shape[1]), 0)
    decay = jnp.exp(-(row.astype(F32) / (n - 1)) * dl_ref[...])
    hf = _dot(hdn_ref[...], w3f_ref[...].astype(BF16)) * decay
    hb = jnp.where(row == 0, 0.0, _dot(hdn_ref[...], w3b_ref[...].astype(BF16)) * decay)
    tot = hf + hb
    sum_ref[...] = tot.astype(sum_ref.dtype)
    dif_ref[...] = (hf - hb).astype(dif_ref.dtype)
    nyq_ref[...] = jnp.sum(jnp.where(jnp.bitwise_and(row, 1) == 0, tot, -tot), axis=0, keepdims=True)


def _filter_call(n_tok, fw1, fb1, ff1, fw2, fb2, ff2, fw3, tc=512):
    t01 = np.linspace(0.0, 1.0, n_tok, dtype=np.float32)
    w = (2.0 * math.pi * np.arange(n_tok, dtype=np.float32) / n_tok).astype(np.float32)
    bands = np.linspace(1e-4, HY_BANDS - 1, HY_BANDS, dtype=np.float32)
    z = np.concatenate([t01[:, None], np.cos(w[:, None] * bands), -np.sin(w[:, None] * bands)], -1)
    deltas = np.abs(np.linspace(HY_MIN_DECAY, HY_MAX_DECAY, HY_W, dtype=np.float32))[None, :]
    emb, hid = fw1.shape
    nc = HY_W // tc
    full = lambda shape: pl.BlockSpec(shape, lambda o, j: (0,) * len(shape))
    o_spec = pl.BlockSpec((n_tok, tc), lambda o, j: (0, o * nc + j))
    return pl.pallas_call(
        _filter_kernel,
        grid=(HY_ORDER, nc),
        in_specs=[full((n_tok, emb)), full((emb, hid)), full((1, hid)), full((1, hid)),
                  full((hid, hid)), full((1, hid)), full((1, hid)),
                  pl.BlockSpec((hid, tc), lambda o, j: (0, (2 * o) * nc + j)),
                  pl.BlockSpec((hid, tc), lambda o, j: (0, (2 * o + 1) * nc + j)),
                  pl.BlockSpec((1, tc), lambda o, j: (0, j))],
        out_specs=[o_spec, o_spec, pl.BlockSpec((1, tc), lambda o, j: (0, o * nc + j))],
        out_shape=[jax.ShapeDtypeStruct((n_tok, HY_ORDER * HY_W), BF16),
                   jax.ShapeDtypeStruct((n_tok, HY_ORDER * HY_W), BF16),
                   jax.ShapeDtypeStruct((1, HY_ORDER * HY_W), F32)],
        scratch_shapes=[pltpu.VMEM((n_tok, hid), BF16)],
        compiler_params=_params(("arbitrary", "arbitrary")),
        name="hyena_filters",
    )(jnp.asarray(z.astype(np.float32)), fw1, fb1.reshape(1, hid), ff1.reshape(1, hid),
      fw2, fb2.reshape(1, hid), ff2.reshape(1, hid), fw3, fw3, jnp.asarray(deltas))


def kernel(x, c, ctx, c_ctx, ada_w, ada_b, ln1_g, ln1_b, ln2_g, ln2_b, router, exp_w1, exp_w3, exp_w2,
           ab_w_in, ab_kv_norm, ab_w_ukv, ab_rpb, ab_w_out,
           cd_w_in, cd_conv_w, cd_conv_b, cd_filt_w1, cd_filt_b1, cd_filt_freq1, cd_filt_w2, cd_filt_b2,
           cd_filt_freq2, cd_filt_w3, cd_skip, cd_w_out):
    batch, n_tok, d = x.shape
    lc = ctx.shape[1]
    x2d = x.reshape(batch * n_tok, d)
    ctx2d = ctx.reshape(batch * lc, d)

    pad_rows = (-(batch + 1)) % 8
    cv = jnp.concatenate([c, c_ctx[None, :], jnp.zeros((pad_rows, d), F32)], axis=0)
    ada = _ada_call(cv, ada_w, ada_b)
    mods = [ada[i, :batch].reshape(batch, 6, d) for i in range(DEPTH)]
    ctx_ss = ada[0, batch, :2 * d].reshape(1, 2, d)

    w_in = _w_in_prep_call(jnp.swapaxes(ab_w_in, 1, 2), 0)
    p_lat = _mm_rope_call(x2d, mods[0][:, 0:2], n_tok, w_in, _rope_tables(n_tok), tm=1024)
    p_ctx = _mm_rope_call(ctx2d, ctx_ss, batch * lc, w_in, _identity_rope_tables(lc), tm=lc)
    w_ukv = _permute_w_ukv(ab_w_ukv[0])
    kv_lat = _kvup_call(p_lat, ab_kv_norm[0], w_ukv, tm=512)
    kv_ctx = _kvup_call(p_ctx, ab_kv_norm[0], w_ukv, tm=lc)
    a_out = _mla_call(p_lat, kv_lat, p_ctx, kv_ctx, batch, n_tok, lc)
    var_map, na_er, na_ec = _na_tables(n_tok // GRID_W)
    b_out = _na_call(p_lat, p_ctx, _na_bias(ab_rpb[0], na_er, na_ec), jnp.asarray(var_map),
                     batch, n_tok, lc)
    ln1 = jnp.stack([ln1_g, ln1_b], axis=1)
    ln2 = jnp.stack([ln2_g, ln2_b], axis=1)
    h1, u2, logits = _outln_call(a_out, b_out, ab_w_out[0].astype(BF16), x2d,
                                 mods[0][:, 2:5], ln1[0], _router_operand(router[0]), n_tok)
    h, u = _ec_moe(h1, u2, logits, exp_w1, exp_w3, exp_w2, 0, mods[0][:, 5:6], ln2[0],
                   mods[1][:, 0:2], batch, n_tok)

    p1 = _mm_call(u, cd_w_in[0].astype(BF16), BF16, tn=1024)
    s0 = _sconv_call(p1, cd_conv_w[0], cd_conv_b[0], 0, HY_W, BF16, batch, n_tok)
    h_sum, h_dif, h_nyq = _filter_call(n_tok, cd_filt_w1[0], cd_filt_b1[0], cd_filt_freq1[0], cd_filt_w2[0],
                                       cd_filt_b2[0], cd_filt_freq2[0], cd_filt_w3[0])
    fc, fs, gc, gs = _hyena_dft_operands(n_tok)
    h_spec = _dft_fwd_call(fc, fs, h_sum, h_dif, None, F32, 1, n_tok)
    z = s0
    for o in range(HY_ORDER):
        y_spec = _dft_fwd_call(fc, fs, z, z, (h_spec, h_nyq, o * HY_W), BF16, batch, n_tok)
        z = _dual_call(gc, gs, y_spec, y_spec,
                       (p1, cd_conv_w[0], cd_conv_b[0], (o + 1) * HY_W, z, cd_skip[0][o:o + 1]),
                       batch, n_tok, HY_W)
    cl, msl, cw = _fnet_operands(n_tok, FN_GROUP_W)
    y_fn = _fnet_call(cl, msl, p1, cw, HY_IN_W, batch, n_tok)
    h1, u2, logits = _outln_call(z, y_fn, cd_w_out[0].astype(BF16), h,
                                 mods[1][:, 2:5], ln1[1], _router_operand(router[1]), n_tok)
    h, _ = _ec_moe(h1, u2, logits, exp_w1, exp_w3, exp_w2, 1, mods[1][:, 5:6], ln2[1],
                   None, batch, n_tok)
    return h.reshape(batch, n_tok, d)
```

```python
import functools
import math

import numpy as np
import jax
import jax.numpy as jnp
from jax import lax
from jax.experimental import pallas as pl
from jax.experimental.pallas import tpu as pltpu

F32 = jnp.float32
BF16 = jnp.bfloat16
I32 = jnp.int32

D_MODEL = 2048
BATCH = 4
SEQ = 2048
DEPTH = 2
CTX_LEN = 256
GRID_W = 64
DEEPNORM_ALPHA = (2.0 * DEPTH) ** 0.25
LN_EPS = 1e-6
NEG_INF = -1e30
LOG2E = math.log2(math.e)

MLA_HEADS = 8
MLA_NOPE = 128
MLA_ROPE = 64
MLA_QK = MLA_NOPE + MLA_ROPE
MLA_V = 128
MLA_KV_RANK = 512
ROPE_THETA = 10000.0

NA_HEADS = 8
NA_HEAD_DIM = 128
NA_KH = 8
NA_KW = 16

OFF_CKV = MLA_HEADS * MLA_QK
OFF_KPE = OFF_CKV + MLA_KV_RANK
OFF_QNA = OFF_KPE + MLA_ROPE
OFF_KNA = OFF_QNA + NA_HEADS * NA_HEAD_DIM
OFF_VNA = OFF_KNA + NA_HEADS * NA_HEAD_DIM

HY_W = 1024
HY_ORDER = 2
HY_IN_W = (HY_ORDER + 1) * HY_W
HY_SHORT = 3
HY_BANDS = 16
HY_DECAY_TARGET = 1e-2
HY_MIN_DECAY = math.log(HY_DECAY_TARGET) / 1.5
HY_MAX_DECAY = math.log(HY_DECAY_TARGET) / 0.3

FN_W = 1024
FN_GROUPS = 4
FN_GROUP_W = FN_W // FN_GROUPS

N_EXPERTS = 16
EC_CAPACITY_FACTOR = 2
EXPERT_FF = 1408

LANES = 128
MXU_DIM = 256
VMEM_LIMIT = 56 * 1024 * 1024

QTILE_W = 2 * LANES
T_KPE = MLA_HEADS
T_CKV = T_KPE + 2
T_QNA = T_CKV + MLA_KV_RANK // QTILE_W
T_KNA = T_QNA + NA_HEADS * NA_HEAD_DIM // QTILE_W
T_VNA = T_KNA + NA_HEADS * NA_HEAD_DIM // QTILE_W
N_ABTILES = T_VNA + NA_HEADS * NA_HEAD_DIM // QTILE_W
AB_PERM_W = N_ABTILES * QTILE_W

MLA_SUBTILES = 2
OUTLN_SUBTILES = 2
MOE_CHUNK = 256
MOE_WIN = 64
NA_G = 4
NA_WR = NA_KH + NA_G - 1
NA_GROUPS_PER_STEP = 2


def _params(sem, vmem=VMEM_LIMIT):
    return pltpu.CompilerParams(dimension_semantics=sem, vmem_limit_bytes=vmem)


def _dot(a, b):
    return jnp.dot(a, b, preferred_element_type=F32)


def _dot_nt(a, b):
    return lax.dot_general(a, b, (((1,), (1,)), ((), ())), preferred_element_type=F32)


def _dot_tn(a, b):
    return lax.dot_general(a, b, (((0,), (0,)), ((), ())), preferred_element_type=F32)


def _split_bf16(x):
    hi = x.astype(BF16)
    lo = (x - hi.astype(F32)).astype(BF16)
    return hi, lo


def _ada_kernel(c_ref, w_ref, b_ref, o_ref):
    c = c_ref[...]
    a = c / (1.0 + jnp.exp(-c))
    a_hi, a_lo = _split_bf16(a)
    w_hi, w_lo = _split_bf16(w_ref[0])
    rows = a.shape[0]
    r = _dot(jnp.concatenate([a_hi, a_lo], axis=0), w_hi)
    o_ref[0] = r[:rows] + r[rows:] + _dot(a_hi, w_lo) + b_ref[0]


def _ada_call(cv, ada_w, ada_b, tn=1024):
    depth, d, n = ada_w.shape
    rows = cv.shape[0]
    return pl.pallas_call(
        _ada_kernel,
        grid=(depth, n // tn),
        in_specs=[pl.BlockSpec((rows, d), lambda l, j: (0, 0)),
                  pl.BlockSpec((1, d, tn), lambda l, j: (l, 0, j)),
                  pl.BlockSpec((1, 1, tn), lambda l, j: (l, 0, j))],
        out_specs=pl.BlockSpec((1, rows, tn), lambda l, j: (l, 0, j)),
        out_shape=jax.ShapeDtypeStruct((depth, rows, n), F32),
        compiler_params=_params(("arbitrary", "arbitrary")),
        name="ada_params",
    )(cv, ada_w, ada_b.reshape(depth, 1, n))


def _mm_rope_kernel(x_ref, ss_ref, w_ref, ta_ref, tb_ref, tc_ref, o_ref, a_ref):
    @pl.when(pl.program_id(1) == 0)
    def _():
        a_ref[...] = (x_ref[...] * (1.0 + ss_ref[0, 1:2, :]) + ss_ref[0, 0:1, :]).astype(a_ref.dtype)

    a = a_ref[...]
    for half in range(2):
        t = 2 * pl.program_id(1) + half
        c0 = half * QTILE_W
        acc = _dot_nt(a, w_ref[c0:c0 + QTILE_W, :])
        is_rope = t <= T_KPE
        f = jnp.where(t < T_KPE, LOG2E * MLA_QK ** -0.5,
                      jnp.where(jnp.logical_and(t >= T_QNA, t < T_KNA), LOG2E * NA_HEAD_DIM ** -0.5, 1.0)
                      ).astype(F32)
        hi = acc[:, LANES:]
        rot = (hi * jnp.where(is_rope, ta_ref[...], 1.0)
               + pltpu.roll(hi, LANES - MLA_ROPE // 2, 1) * jnp.where(is_rope, tb_ref[...], 0.0)
               + pltpu.roll(hi, MLA_ROPE // 2, 1) * jnp.where(is_rope, tc_ref[...], 0.0))
        o_ref[:, c0:c0 + LANES] = (acc[:, :LANES] * f).astype(o_ref.dtype)
        o_ref[:, c0 + LANES:c0 + QTILE_W] = (rot * f).astype(o_ref.dtype)


def _mm_rope_call(x, ss, rows_per_group, w, tabs, tm):
    m, k = x.shape
    n = w.shape[0]
    tn = 2 * QTILE_W
    ntab = tabs[0].shape[0] // tm
    tpg = rows_per_group // tm
    tab_spec = pl.BlockSpec((tm, LANES), lambda i, j: (i % ntab, 0))
    return pl.pallas_call(
        _mm_rope_kernel,
        grid=(m // tm, n // tn),
        in_specs=[pl.BlockSpec((tm, k), lambda i, j: (i, 0)),
                  pl.BlockSpec((1, 2, k), lambda i, j: (i // tpg, 0, 0)),
                  pl.BlockSpec((tn, k), lambda i, j: (j, 0)),
                  tab_spec, tab_spec, tab_spec],
        out_specs=pl.BlockSpec((tm, tn), lambda i, j: (i, j)),
        out_shape=jax.ShapeDtypeStruct((m, n), BF16),
        scratch_shapes=[pltpu.VMEM((tm, k), BF16)],
        compiler_params=_params(("arbitrary", "arbitrary")),
        name="attn_in_proj",
    )(x, ss, w, *tabs)


def _rope_tables(n_tok):
    t = np.arange(n_tok)
    row = (t // GRID_W).astype(np.float32)
    col = (t % GRID_W).astype(np.float32)
    n_freq = MLA_ROPE // 4
    inv = (ROPE_THETA ** (-np.arange(n_freq, dtype=np.float32) / n_freq)).astype(np.float32)
    ang = np.concatenate([row[:, None] * inv, col[:, None] * inv], axis=1)
    cos, sin = np.cos(ang).astype(np.float32), np.sin(ang).astype(np.float32)
    half = MLA_ROPE // 2
    ta = np.zeros((n_tok, LANES), np.float32)
    tb = np.zeros((n_tok, LANES), np.float32)
    tc = np.zeros((n_tok, LANES), np.float32)
    ta[:, :half] = cos
    ta[:, half:2 * half] = cos
    tb[:, :half] = -sin
    tc[:, half:2 * half] = sin
    return jnp.asarray(ta), jnp.asarray(tb), jnp.asarray(tc)


def _identity_rope_tables(n_tok):
    ta = np.zeros((n_tok, LANES), np.float32)
    ta[:, :MLA_ROPE] = 1.0
    z = np.zeros((n_tok, LANES), np.float32)
    return jnp.asarray(ta), jnp.asarray(z), jnp.asarray(z)


def _w_in_prep_kernel(w_ref, o_ref):
    x = w_ref[0]
    zeros = lambda n: jnp.zeros((n, x.shape[1]), x.dtype)
    rows = []
    for h in range(MLA_HEADS):
        rows += [x[h * MLA_QK:(h + 1) * MLA_QK, :], zeros(QTILE_W - MLA_QK)]
    rows += [zeros(LANES), x[OFF_KPE:OFF_QNA, :], zeros(LANES - MLA_ROPE), zeros(QTILE_W),
             x[OFF_CKV:OFF_KPE, :], x[OFF_QNA:, :]]
    o_ref[...] = jnp.concatenate(rows, axis=0).astype(o_ref.dtype)


def _w_in_prep_call(w_in_t, layer, tk=256):
    _, n, d = w_in_t.shape
    return pl.pallas_call(
        _w_in_prep_kernel,
        grid=(d // tk,),
        in_specs=[pl.BlockSpec((1, n, tk), lambda i: (layer, 0, i))],
        out_specs=pl.BlockSpec((AB_PERM_W, tk), lambda i: (0, i)),
        out_shape=jax.ShapeDtypeStruct((AB_PERM_W, d), BF16),
        compiler_params=_params(("arbitrary",)),
        name="attn_w_in_prep",
    )(w_in_t)


def _kvup_kernel(x_ref, g_ref, w_ref, o_ref):
    x = x_ref[...].astype(F32)
    y = x * lax.rsqrt(jnp.mean(x * x, axis=-1, keepdims=True) + LN_EPS) * g_ref[...]
    o_ref[...] = _dot(y.astype(BF16), w_ref[...]).astype(o_ref.dtype)


def _kvup_call(p, g, w, tm):
    m = p.shape[0]
    r, n = w.shape
    return pl.pallas_call(
        _kvup_kernel,
        grid=(m // tm,),
        in_specs=[pl.BlockSpec((tm, r), lambda i: (i, T_CKV * QTILE_W // r)),
                  pl.BlockSpec((1, r), lambda i: (0, 0)),
                  pl.BlockSpec((r, n), lambda i: (0, 0))],
        out_specs=pl.BlockSpec((tm, n), lambda i: (i, 0)),
        out_shape=jax.ShapeDtypeStruct((m, n), BF16),
        compiler_params=_params(("arbitrary",)),
        name="kv_up",
    )(p, g.reshape(1, r), w)


def _permute_w_ukv(w):
    r = w.shape[0]
    w3 = w.reshape(r, MLA_HEADS, MLA_NOPE + MLA_V)
    return jnp.concatenate([w3[:, :, :MLA_NOPE].reshape(r, -1), w3[:, :, MLA_NOPE:].reshape(r, -1)],
                           axis=1).astype(BF16)


def _mla_kernel(q_ref, kn_ref, kp_ref, v_ref, kcn_ref, kcp_ref, vc_ref, o_ref, kcat, vcat):
    s_len = kn_ref.shape[0]

    @pl.when(pl.program_id(2) == 0)
    def _():
        kcat[:s_len, :LANES] = kn_ref[...]
        kcat[:s_len, LANES:] = kp_ref[...]
        kcat[s_len:, :LANES] = kcn_ref[...]
        kcat[s_len:, LANES:] = kcp_ref[...]
        vcat[:s_len, :] = v_ref[...]
        vcat[s_len:, :] = vc_ref[...]

    sub = q_ref.shape[0] // MLA_SUBTILES
    for r0 in range(0, q_ref.shape[0], sub):
        s = _dot_nt(q_ref[r0:r0 + sub, :], kcat[...])
        m = jnp.max(s, axis=-1, keepdims=True)
        p = jnp.exp2(s - m)
        l = jnp.sum(p, axis=-1, keepdims=True)
        o = _dot(p.astype(BF16), vcat[...])
        o_ref[r0:r0 + sub, :] = (o / l).astype(o_ref.dtype)


def _mla_call(p_lat, kv_lat, p_ctx, kv_ctx, batch, s_len, lc, tq=512):
    nq = s_len // tq
    kpe_blk = T_KPE * 2 + 1
    return pl.pallas_call(
        _mla_kernel,
        grid=(batch, MLA_HEADS, nq),
        in_specs=[pl.BlockSpec((tq, QTILE_W), lambda b, h, i: (b * nq + i, h)),
                  pl.BlockSpec((s_len, LANES), lambda b, h, i: (b, h)),
                  pl.BlockSpec((s_len, LANES), lambda b, h, i: (b, kpe_blk)),
                  pl.BlockSpec((s_len, LANES), lambda b, h, i: (b, MLA_HEADS + h)),
                  pl.BlockSpec((lc, LANES), lambda b, h, i: (b, h)),
                  pl.BlockSpec((lc, LANES), lambda b, h, i: (b, kpe_blk)),
                  pl.BlockSpec((lc, LANES), lambda b, h, i: (b, MLA_HEADS + h))],
        out_specs=pl.BlockSpec((tq, LANES), lambda b, h, i: (b * nq + i, h)),
        out_shape=jax.ShapeDtypeStruct((batch * s_len, MLA_HEADS * MLA_V), BF16),
        scratch_shapes=[pltpu.VMEM((s_len + lc, QTILE_W), BF16),
                        pltpu.VMEM((s_len + lc, LANES), BF16)],
        compiler_params=_params(("arbitrary", "arbitrary", "arbitrary")),
        name="mla_attention",
    )(p_lat, kv_lat, p_lat, kv_lat, kv_ctx, p_ctx, kv_ctx)


def _na_kernel(var_ref, q_ref, k_ref, v_ref, kc_ref, vc_ref, *rest, rows):
    del var_ref
    bias_refs, o_ref = rest[:-1], rest[-1]
    win = NA_WR * GRID_W
    tq = NA_G * GRID_W
    for i, bias_ref in enumerate(bias_refs):
        g = len(bias_refs) * pl.program_id(2) + i
        start_row = jnp.clip(NA_G * g - NA_KH // 2, 0, rows - NA_WR)
        start = pl.multiple_of(start_row * GRID_W, GRID_W)
        q = q_ref[i * tq:(i + 1) * tq, :]
        sw = _dot_nt(q, k_ref[pl.ds(start, win), :]) + bias_ref[0, 0]
        sc = _dot_nt(q, kc_ref[...])
        m = jnp.maximum(jnp.max(sw, axis=-1, keepdims=True), jnp.max(sc, axis=-1, keepdims=True))
        pw = jnp.exp2(sw - m)
        pc = jnp.exp2(sc - m)
        l = jnp.sum(pw, axis=-1, keepdims=True) + jnp.sum(pc, axis=-1, keepdims=True)
        o = _dot(pw.astype(BF16), v_ref[pl.ds(start, win), :]) + _dot(pc.astype(BF16), vc_ref[...])
        o_ref[i * tq:(i + 1) * tq, :] = (o / l).astype(o_ref.dtype)


def _na_tables(rows):
    kh = min(NA_KH, rows)
    ng = rows // NA_G
    qr = np.arange(NA_G)[:, None]
    kr = np.arange(NA_WR)[None, :]
    sel_rows = []
    for g in range(ng):
        start_row = int(np.clip(NA_G * g - NA_KH // 2, 0, rows - NA_WR))
        r = NA_G * g + qr
        krow = start_row + kr
        rs = np.clip(r - kh // 2, 0, rows - kh)
        ok = (krow >= rs) & (krow < rs + kh)
        dr = krow - r + (NA_KH - 1)
        sel_rows.append(np.stack([ok & (dr == d) for d in range(2 * NA_KH - 1)]))
    sel_rows = np.stack(sel_rows)
    uniq, inverse = np.unique(sel_rows.reshape(ng, -1), axis=0, return_inverse=True)
    er = uniq.reshape((-1,) + sel_rows.shape[1:]).astype(np.float32)
    qc = np.arange(GRID_W)[:, None]
    kc = np.arange(GRID_W)[None, :]
    cs = np.clip(qc - NA_KW // 2, 0, GRID_W - NA_KW)
    ok_c = (kc >= cs) & (kc < cs + NA_KW)
    dc = np.clip(kc - qc, -(NA_KW - 1), NA_KW - 1) + (NA_KW - 1)
    ec = np.stack([ok_c & (dc == e) for e in range(2 * NA_KW - 1)]).astype(np.float32)
    return inverse.reshape(-1).astype(np.int32), er, ec


def _na_bias(rpb, er, ec):
    hp = lax.Precision.HIGHEST
    t = jnp.einsum('hde,eqk->hdqk', rpb.astype(F32), jnp.asarray(ec), precision=hp)
    t = jnp.where(jnp.asarray(ec.sum(axis=0) > 0.5), t * LOG2E, NEG_INF)
    outside = jnp.full(t[:, 0].shape, NEG_INF, F32)
    d_of = np.where(er.sum(axis=1) > 0.5, er.argmax(axis=1), -1)
    block = lambda d: outside if d < 0 else t[:, d]
    return jnp.stack([
        jnp.concatenate([jnp.concatenate([block(int(d)) for d in row], axis=-1) for row in var], axis=-2)
        for var in d_of])


def _na_call(p_lat, p_ctx, bias, var_map, batch, s_len, lc):
    rows = s_len // GRID_W
    gps = NA_GROUPS_PER_STEP
    ns = rows // (NA_G * gps)
    tq = NA_G * GRID_W
    win = NA_WR * GRID_W
    qb, kb, vb = T_QNA * 2, T_KNA * 2, T_VNA * 2
    bias_spec = lambda i: pl.BlockSpec((1, 1, tq, win), lambda b, h, s, vm: (vm[gps * s + i], h, 0, 0))
    grid_spec = pltpu.PrefetchScalarGridSpec(
        num_scalar_prefetch=1,
        grid=(batch, NA_HEADS, ns),
        in_specs=[pl.BlockSpec((gps * tq, LANES), lambda b, h, s, vm: (b * ns + s, qb + h)),
                  pl.BlockSpec((s_len, LANES), lambda b, h, s, vm: (b, kb + h)),
                  pl.BlockSpec((s_len, LANES), lambda b, h, s, vm: (b, vb + h)),
                  pl.BlockSpec((lc, LANES), lambda b, h, s, vm: (b, kb + h)),
                  pl.BlockSpec((lc, LANES), lambda b, h, s, vm: (b, vb + h))]
                 + [bias_spec(i) for i in range(gps)],
        out_specs=pl.BlockSpec((gps * tq, LANES), lambda b, h, s, vm: (b * ns + s, h)),
    )
    return pl.pallas_call(
        functools.partial(_na_kernel, rows=rows),
        grid_spec=grid_spec,
        out_shape=jax.ShapeDtypeStruct((batch * s_len, NA_HEADS * NA_HEAD_DIM), BF16),
        compiler_params=_params(("arbitrary", "arbitrary", "arbitrary")),
        name="na_attention",
    )(var_map, p_lat, p_lat, p_lat, p_ctx, p_ctx, *([bias] * gps))


def _layer_norm(x, g, b):
    mu = jnp.mean(x, axis=-1, keepdims=True)
    xc = x - mu
    var = jnp.mean(xc * xc, axis=-1, keepdims=True)
    return xc * lax.rsqrt(var + LN_EPS) * g + b


def _outln_kernel(a1_ref, a2_ref, w_ref, h_ref, mod_ref, ln_ref, r_ref, h1_ref, u2_ref, lg_ref):
    half = a1_ref.shape[1]
    sub = a1_ref.shape[0] // OUTLN_SUBTILES
    for r0 in range(0, a1_ref.shape[0], sub):
        rs = slice(r0, r0 + sub)
        y = _dot(a1_ref[rs, :], w_ref[:half, :]) + _dot(a2_ref[rs, :], w_ref[half:, :])
        x = DEEPNORM_ALPHA * h_ref[rs, :] + mod_ref[0, 0:1, :] * y
        hn = _layer_norm(x, ln_ref[0:1, :], ln_ref[1:2, :])
        h1_ref[rs, :] = hn
        u = hn * (1.0 + mod_ref[0, 2:3, :]) + mod_ref[0, 1:2, :]
        u_bf = u.astype(BF16)
        u2_ref[rs, :] = u_bf
        lg_ref[rs, :] = _dot(u_bf, r_ref[...])


def _outln_call(a1, a2, w, h, mod, ln, router2, rows_per_group, tm=512):
    m, d = h.shape
    half = a1.shape[1]
    tpg = rows_per_group // tm
    return pl.pallas_call(
        _outln_kernel,
        grid=(m // tm,),
        in_specs=[pl.BlockSpec((tm, half), lambda i: (i, 0)),
                  pl.BlockSpec((tm, half), lambda i: (i, 0)),
                  _resident_spec((2 * half, d)),
                  pl.BlockSpec((tm, d), lambda i: (i, 0)),
                  pl.BlockSpec((1, 3, d), lambda i: (i // tpg, 0, 0)),
                  pl.BlockSpec((2, d), lambda i: (0, 0)),
                  _resident_spec((d, LANES))],
        out_specs=[pl.BlockSpec((tm, d), lambda i: (i, 0)),
                   pl.BlockSpec((tm, d), lambda i: (i, 0)),
                   pl.BlockSpec((tm, LANES), lambda i: (i, 0))],
        out_shape=[jax.ShapeDtypeStruct((m, d), F32),
                   jax.ShapeDtypeStruct((m, d), BF16),
                   jax.ShapeDtypeStruct((m, LANES), F32)],
        compiler_params=_params(("arbitrary",)),
        name="out_proj_postnorm",
    )(a1, a2, w, h, mod, ln, router2)


def _router_operand(router):
    d, e = router.shape
    return jnp.pad(router, ((0, 0), (0, LANES - e))).astype(BF16)


def _route_kernel(lg_ref, slot_ref, aff_ref, cnt_ref, tri_ref, *, cap):
    n_tok = lg_ref.shape[2]

    @pl.when(pl.program_id(0) == 0)
    def _():
        chunk = 256
        for r0 in range(0, n_tok, chunk):
            r = r0 + lax.broadcasted_iota(I32, (chunk, n_tok), 0)
            c = lax.broadcasted_iota(I32, (chunk, n_tok), 1)
            tri_ref[r0:r0 + chunk, :] = jnp.where(r < c, 1.0, 0.0).astype(BF16)

    lg = lg_ref[0]
    ex = jnp.exp(lg - jnp.max(lg, axis=0, keepdims=True))
    aff = ex / jnp.sum(ex, axis=0, keepdims=True)
    bits = lax.bitcast_convert_type(aff, I32)
    n_e = lg.shape[0]
    count = lambda mask: jnp.sum(jnp.where(mask, 1.0, 0.0), axis=1, keepdims=True)

    def body(_, lohi):
        lo, hi = lohi
        mid = lo + jnp.right_shift(hi - lo, 1)
        ge = count(bits >= mid) >= cap
        return jnp.where(ge, mid, lo), jnp.where(ge, hi, mid)

    lo0 = jnp.zeros((n_e, 1), I32)
    hi0 = jnp.full((n_e, 1), 0x7F800000, I32)
    thr, _ = lax.fori_loop(0, 31, body, (lo0, hi0))
    gt = bits > thr
    eq = bits == thr
    need = cap - count(gt)
    pre_eq = _dot(jnp.where(eq, 1.0, 0.0).astype(BF16), tri_ref[...])
    sel = jnp.logical_or(gt, jnp.logical_and(eq, pre_eq < need))
    sel_bf = jnp.where(sel, 1.0, 0.0).astype(BF16)
    slot = _dot(sel_bf, tri_ref[...])
    slot_ref[0] = jnp.where(sel, slot.astype(I32), -1)
    aff_ref[0] = aff
    r = lax.broadcasted_iota(I32, (n_tok, LANES), 0)
    c = lax.broadcasted_iota(I32, (n_tok, LANES), 1)
    before = jnp.where(r < c * MOE_CHUNK, 1.0, 0.0).astype(BF16)
    cnt_ref[0] = _dot(sel_bf, before).astype(I32)


def _route_call(lg_t, cap):
    b, e, n_tok = lg_t.shape
    spec = pl.BlockSpec((1, e, n_tok), lambda i: (i, 0, 0))
    return pl.pallas_call(
        functools.partial(_route_kernel, cap=cap),
        grid=(b,),
        in_specs=[spec],
        out_specs=[spec, spec, pl.BlockSpec((1, e, LANES), lambda i: (i, 0, 0))],
        out_shape=[jax.ShapeDtypeStruct((b, e, n_tok), I32), jax.ShapeDtypeStruct((b, e, n_tok), F32),
                   jax.ShapeDtypeStruct((b, e, LANES), I32)],
        scratch_shapes=[pltpu.VMEM((n_tok, n_tok), BF16)],
        compiler_params=_params(("arbitrary",)),
        name="ec_route",
    )(lg_t)


def _slot_windows(cnt, cap, n_chunk):
    start = cnt[:, :, :n_chunk]
    end = cnt[:, :, 1:n_chunk + 1]
    a = jnp.minimum((start // 16) * 16, cap - MOE_WIN)
    fits = jnp.all(end <= a + MOE_WIN, axis=1)
    return jnp.swapaxes(a, 1, 2).reshape(-1).astype(I32), fits.reshape(-1).astype(I32)


def _gather_kernel(win_ref, fit_ref, slot_ref, u_ref, o_ref, *, group):
    b, kc = pl.program_id(0), pl.program_id(1)
    n_e, cap = o_ref.shape[0], o_ref.shape[1]
    chunk = u_ref.shape[0]
    step = b * pl.num_programs(1) + kc

    @pl.when(kc == 0)
    def _():
        o_ref[...] = jnp.zeros(o_ref.shape, o_ref.dtype)

    def place(rows, starts, group):
        row = lax.broadcasted_iota(I32, (rows, chunk), 0)
        for g0 in range(0, n_e, group):
            onehot = jnp.concatenate(
                [jnp.where(row + starts[e] == slot_ref[0, e:e + 1, :], 1.0, 0.0).astype(BF16)
                 for e in range(g0, g0 + group)], axis=0)
            got = _dot(onehot, u_ref[...]).astype(o_ref.dtype)
            for i, e in enumerate(range(g0, g0 + group)):
                o_ref[e, pl.ds(starts[e], rows), :] += got[i * rows:(i + 1) * rows]

    @pl.when(fit_ref[step] != 0)
    def _():
        place(MOE_WIN, [pl.multiple_of(win_ref[step * n_e + e], 16) for e in range(n_e)], group)

    @pl.when(fit_ref[step] == 0)
    def _():
        place(cap, [0] * n_e, max(1, group * MOE_WIN // cap))


def _gather_call(win, fit, slot, u2, cap, group=512 // MOE_WIN):
    b, e, n_tok = slot.shape
    d = u2.shape[1]
    nc = n_tok // MOE_CHUNK
    grid_spec = pltpu.PrefetchScalarGridSpec(
        num_scalar_prefetch=2,
        grid=(b, nc),
        in_specs=[pl.BlockSpec((1, e, MOE_CHUNK), lambda i, k, w, f: (i, 0, k)),
                  pl.BlockSpec((MOE_CHUNK, d), lambda i, k, w, f: (i * nc + k, 0))],
        out_specs=pl.BlockSpec((e, cap, d), lambda i, k, w, f: (0, i, 0)),
    )
    return pl.pallas_call(
        functools.partial(_gather_kernel, group=group),
        grid_spec=grid_spec,
        out_shape=jax.ShapeDtypeStruct((e, b * cap, d), BF16),
        compiler_params=_params(("arbitrary", "arbitrary")),
        name="moe_gather",
    )(win, fit, slot, u2)


def _expert_hidden_kernel(x_ref, w1_ref, w3_ref, o_ref):
    tf = w1_ref.shape[3]
    h = _dot(x_ref[0], jnp.concatenate([w1_ref[0, 0].astype(BF16), w3_ref[0, 0].astype(BF16)], axis=1))
    h1 = h[:, :tf]
    o_ref[0] = ((h1 / (1.0 + jnp.exp(-h1))) * h[:, tf:]).astype(o_ref.dtype)


def _expert_out_kernel(hd_ref, w2_ref, o_ref):
    y = _dot(hd_ref[0], w2_ref[0, 0].astype(BF16)).astype(o_ref.dtype)
    cap = o_ref.shape[1]
    for b in range(o_ref.shape[0]):
        o_ref[b] = y[b * cap:(b + 1) * cap]


def _expert_call(xe, w1, w3, w2, layer, batch, tf=512, tn=1024):
    e, m, d = xe.shape
    cap = m // batch
    ff = w1.shape[3]
    hidden = pl.pallas_call(
        _expert_hidden_kernel,
        grid=(e, pl.cdiv(ff, tf)),
        in_specs=[pl.BlockSpec((1, m, d), lambda i, f: (i, 0, 0)),
                  pl.BlockSpec((1, 1, d, tf), lambda i, f: (layer, i, 0, f)),
                  pl.BlockSpec((1, 1, d, tf), lambda i, f: (layer, i, 0, f))],
        out_specs=pl.BlockSpec((1, m, tf), lambda i, f: (i, 0, f)),
        out_shape=jax.ShapeDtypeStruct((e, m, ff), BF16),
        compiler_params=_params(("arbitrary", "arbitrary")),
        name="moe_expert_hidden",
    )(xe, w1, w3)
    return pl.pallas_call(
        _expert_out_kernel,
        grid=(e, d // tn),
        in_specs=[pl.BlockSpec((1, m, ff), lambda i, j: (i, 0, 0)),
                  pl.BlockSpec((1, 1, ff, tn), lambda i, j: (layer, i, 0, j))],
        out_specs=pl.BlockSpec((batch, cap, tn), lambda i, j: (0, i, j)),
        out_shape=jax.ShapeDtypeStruct((batch, e * cap, d), BF16),
        compiler_params=_params(("arbitrary", "arbitrary")),
        name="moe_expert_out",
    )(hidden, w2)


def _scatter_kernel(win_ref, fit_ref, slot_ref, aff_ref, ye_ref, h_ref, mod_ref, ln_ref, *rest, with_next):
    if with_next:
        nmod_ref, h2_ref, un_ref, pt_ref, yew_ref, moe_ref = rest
    else:
        h2_ref, pt_ref, yew_ref, moe_ref = rest
    n_e = slot_ref.shape[1]
    cap = ye_ref.shape[1] // n_e
    tt = h_ref.shape[0]
    step = pl.program_id(0) * pl.num_programs(1) + pl.program_id(1)

    def gates(rows, e, start):
        row = lax.broadcasted_iota(I32, (rows, tt), 0)
        return jnp.where(row + start == slot_ref[0, e:e + 1, :], aff_ref[0, e:e + 1, :], 0.0).astype(BF16)

    @pl.when(fit_ref[step] != 0)
    def _():
        for e in range(n_e):
            start = pl.multiple_of(win_ref[step * n_e + e], 16)
            pt_ref[e * MOE_WIN:(e + 1) * MOE_WIN, :] = gates(MOE_WIN, e, start)
            yew_ref[e * MOE_WIN:(e + 1) * MOE_WIN, :] = ye_ref[0, pl.ds(e * cap + start, MOE_WIN), :]
        moe_ref[...] = _dot_tn(pt_ref[:n_e * MOE_WIN, :], yew_ref[...])

    @pl.when(fit_ref[step] == 0)
    def _():
        for e in range(n_e):
            pt_ref[e * cap:(e + 1) * cap, :] = gates(cap, e, 0)
        moe_ref[...] = _dot_tn(pt_ref[...], ye_ref[0])

    x = DEEPNORM_ALPHA * h_ref[...] + mod_ref[0] * moe_ref[...]
    hn = _layer_norm(x, ln_ref[0:1, :], ln_ref[1:2, :])
    h2_ref[...] = hn
    if with_next:
        un_ref[...] = (hn * (1.0 + nmod_ref[0, 1:2, :]) + nmod_ref[0, 0:1, :]).astype(un_ref.dtype)


def _scatter_call(win, fit, slot, aff, ye, h1, gate, ln, next_ss):
    b, e, n_tok = slot.shape
    m, d = h1.shape
    tt = MOE_CHUNK
    nt = n_tok // tt
    rows = ye.shape[1]
    with_next = next_ss is not None
    tok_spec = pl.BlockSpec((1, e, tt), lambda i, t, w, f: (i, 0, t))
    row_spec = pl.BlockSpec((tt, d), lambda i, t, w, f: (i * nt + t, 0))
    in_specs = [tok_spec, tok_spec,
                pl.BlockSpec((1, rows, d), lambda i, t, w, f: (i, 0, 0), pipeline_mode=pl.Buffered(1)),
                row_spec,
                pl.BlockSpec((1, 1, d), lambda i, t, w, f: (i, 0, 0)),
                pl.BlockSpec((2, d), lambda i, t, w, f: (0, 0))]
    args = [slot, aff, ye, h1, gate, ln]
    out_specs = [row_spec]
    out_shape = [jax.ShapeDtypeStruct((m, d), F32)]
    if with_next:
        in_specs.append(pl.BlockSpec((1, 2, d), lambda i, t, w, f: (i, 0, 0)))
        args.append(next_ss)
        out_specs.append(row_spec)
        out_shape.append(jax.ShapeDtypeStruct((m, d), BF16))
    grid_spec = pltpu.PrefetchScalarGridSpec(
        num_scalar_prefetch=2,
        grid=(b, nt),
        in_specs=in_specs,
        out_specs=out_specs,
        scratch_shapes=[pltpu.VMEM((rows, tt), BF16),
                        pltpu.VMEM((e * MOE_WIN, d), BF16),
                        pltpu.VMEM((tt, d), F32)],
    )
    res = pl.pallas_call(
        functools.partial(_scatter_kernel, with_next=with_next),
        grid_spec=grid_spec,
        out_shape=out_shape,
        compiler_params=_params(("arbitrary", "arbitrary")),
        name="moe_combine_postnorm",
    )(win, fit, *args)
    return res if with_next else (res[0], None)


def _ec_moe(h1, u2, logits, w1, w3, w2, layer, gate, ln, next_ss, batch, n_tok):
    cap = EC_CAPACITY_FACTOR * n_tok // N_EXPERTS
    lg_t = jnp.swapaxes(logits[:, :N_EXPERTS].reshape(batch, n_tok, N_EXPERTS), 1, 2)
    slot, aff, cnt = _route_call(lg_t, cap)
    win, fit = _slot_windows(cnt, cap, n_tok // MOE_CHUNK)
    xe = _gather_call(win, fit, slot, u2, cap)
    ye = _expert_call(xe, w1, w3, w2, layer, batch)
    return _scatter_call(win, fit, slot, aff, ye, h1, gate, ln, next_ss)


def _mm_kernel(a_ref, w_ref, o_ref):
    o_ref[...] = _dot(a_ref[...], w_ref[...]).astype(o_ref.dtype)


def _mm_call(a, w, out_dtype, tm=1024, tn=512):
    m, k = a.shape
    n = w.shape[1]
    return pl.pallas_call(
        _mm_kernel,
        grid=(m // tm, n // tn),
        in_specs=[pl.BlockSpec((tm, k), lambda i, j: (i, 0)),
                  pl.BlockSpec((k, tn), lambda i, j: (0, j))],
        out_specs=pl.BlockSpec((tm, tn), lambda i, j: (i, j)),
        out_shape=jax.ShapeDtypeStruct((m, n), out_dtype),
        compiler_params=_params(("arbitrary", "arbitrary")),
        name="matmul",
    )(a, w)


def _short_conv(p, w, b):
    x = p.astype(F32)
    n = x.shape[0]
    r = lax.broadcasted_iota(I32, x.shape, 0)
    prev = jnp.where(r == 0, 0.0, pltpu.roll(x, 1, 0))
    nxt = jnp.where(r == n - 1, 0.0, pltpu.roll(x, n - 1, 0))
    return b + w[0:1, :] * prev + w[1:2, :] * x + w[2:3, :] * nxt


def _sconv_kernel(p_ref, w_ref, b_ref, o_ref):
    o_ref[...] = _short_conv(p_ref[...], w_ref[...], b_ref[...]).astype(o_ref.dtype)


def _sconv_call(p, conv_w, conv_b, col0, ncols, out_dtype, batch, n_tok, tc=512):
    c0 = col0 // tc
    return pl.pallas_call(
        _sconv_kernel,
        grid=(batch, ncols // tc),
        in_specs=[pl.BlockSpec((n_tok, tc), lambda b, j: (b, c0 + j)),
                  pl.BlockSpec((HY_SHORT, tc), lambda b, j: (0, c0 + j)),
                  pl.BlockSpec((1, tc), lambda b, j: (0, c0 + j))],
        out_specs=pl.BlockSpec((n_tok, tc), lambda b, j: (b, j)),
        out_shape=jax.ShapeDtypeStruct((batch * n_tok, ncols), out_dtype),
        compiler_params=_params(("arbitrary", "arbitrary")),
        name="hyena_short_conv",
    )(p, conv_w, conv_b.reshape(1, -1))


def _dft_fwd_kernel(fc_ref, fs_ref, r1_ref, r2_ref, *rest, spectral):
    xr = _dot(fc_ref[...], r1_ref[...])
    xi = _dot(fs_ref[...], r2_ref[...])
    if spectral:
        h_ref, nyq_ref, o_ref = rest
        hr = h_ref[0, 0]
        hi = h_ref[0, 1]
        dc = lax.broadcasted_iota(I32, xr.shape, 0) == 0
        yr = xr * hr - jnp.where(dc, 0.0, xi * hi)
        yi = jnp.where(dc, xi * nyq_ref[...], xr * hi + xi * hr)
    else:
        (o_ref,) = rest
        yr, yi = xr, xi
    o_ref[0, 0] = yr.astype(o_ref.dtype)
    o_ref[0, 1] = yi.astype(o_ref.dtype)


def _resident_spec(shape):
    return pl.BlockSpec(shape, lambda *_: (0,) * len(shape), pipeline_mode=pl.Buffered(1))


def _dft_fwd_call(fc, fs, r1, r2, spec, out_dtype, batch, n_tok, tn=256):
    ncols = r1.shape[1]
    r_spec = pl.BlockSpec((n_tok, tn), lambda j, b: (b, j))
    in_specs = [_resident_spec(fc.shape), _resident_spec(fs.shape), r_spec, r_spec]
    args = [fc, fs, r1, r2]
    if spec is not None:
        h, nyq, hcol0 = spec
        c0 = hcol0 // tn
        in_specs += [pl.BlockSpec((1, 2, n_tok, tn), lambda j, b: (0, 0, 0, c0 + j)),
                     pl.BlockSpec((1, tn), lambda j, b: (0, c0 + j))]
        args += [h, nyq]
    return pl.pallas_call(
        functools.partial(_dft_fwd_kernel, spectral=spec is not None),
        grid=(ncols // tn, batch),
        in_specs=in_specs,
        out_specs=pl.BlockSpec((1, 2, n_tok, tn), lambda j, b: (b, 0, 0, j)),
        out_shape=jax.ShapeDtypeStruct((batch, 2, n_tok, ncols), out_dtype),
        compiler_params=_params(("arbitrary", "arbitrary")),
        name="dft_forward",
    )(*args)


def _dual_kernel(a1_ref, a2_ref, b1_ref, b2_ref, *rest, hyena, planes):
    b1 = b1_ref[0, 0] if planes else b1_ref[...]
    b2 = b2_ref[0, 0] if planes else b2_ref[...]
    y = _dot(a1_ref[...], b1) + _dot(a2_ref[...], b2)
    if hyena:
        p_ref, cw_ref, cb_ref, z_ref, skip_ref, o_ref = rest
        gate = _short_conv(p_ref[...], cw_ref[...], cb_ref[...])
        y = gate * (y + skip_ref[...] * z_ref[...].astype(F32))
    else:
        (o_ref,) = rest
    o_ref[...] = y.astype(o_ref.dtype)


def _dual_call(a1, a2, b1, b2, hy, batch, n_tok, ncols, tn=256):
    planes = b1.ndim == 4
    if planes:
        b_specs = [pl.BlockSpec((1, 1, n_tok, tn), lambda b, j: (b, 0, 0, j)),
                   pl.BlockSpec((1, 1, n_tok, tn), lambda b, j: (b, 1, 0, j))]
    else:
        b_specs = [pl.BlockSpec((n_tok, tn), lambda b, j: (b, j))] * 2
    in_specs = [_resident_spec(a1.shape), _resident_spec(a2.shape)] + b_specs
    args = [a1, a2, b1, b2]
    o_spec = pl.BlockSpec((n_tok, tn), lambda b, j: (b, j))
    if hy is not None:
        p, conv_w, conv_b, pcol0, z, skip = hy
        c0 = pcol0 // tn
        in_specs += [pl.BlockSpec((n_tok, tn), lambda b, j: (b, c0 + j)),
                     pl.BlockSpec((HY_SHORT, tn), lambda b, j: (0, c0 + j)),
                     pl.BlockSpec((1, tn), lambda b, j: (0, c0 + j)),
                     o_spec,
                     pl.BlockSpec((1, tn), lambda b, j: (0, j))]
        args += [p, conv_w, conv_b.reshape(1, -1), z, skip]
    return pl.pallas_call(
        functools.partial(_dual_kernel, hyena=hy is not None, planes=planes),
        grid=(batch, ncols // tn),
        in_specs=in_specs,
        out_specs=o_spec,
        out_shape=jax.ShapeDtypeStruct((batch * n_tok, ncols), BF16),
        compiler_params=_params(("arbitrary", "arbitrary")),
        name="seq_mix_matmul",
    )(*args)


def _fnet_kernel(cl_ref, msl_ref, x_ref, cw_ref, o_ref):
    x = x_ref[...]
    seq = jnp.concatenate([_dot(cl_ref[...], x).astype(BF16), _dot(msl_ref[...], x).astype(BF16)], axis=1)
    o_ref[...] = _dot(seq, cw_ref[...]).astype(o_ref.dtype)


def _fnet_call(cl, msl, p, cw, col0, batch, n_tok):
    gw = cw.shape[1]
    c0 = col0 // gw
    return pl.pallas_call(
        _fnet_kernel,
        grid=(batch, FN_GROUPS),
        in_specs=[_resident_spec(cl.shape), _resident_spec(msl.shape),
                  pl.BlockSpec((n_tok, gw), lambda b, g: (b, c0 + g)),
                  pl.BlockSpec(cw.shape, lambda b, g: (0, 0))],
        out_specs=pl.BlockSpec((n_tok, gw), lambda b, g: (b, g)),
        out_shape=jax.ShapeDtypeStruct((batch * n_tok, FN_GROUPS * gw), BF16),
        compiler_params=_params(("arbitrary", "arbitrary")),
        name="fnet_mix",
    )(cl, msl, p, cw)


def _cos_sin_matrix(n_rows, n_cols, period, split=64):
    r = np.arange(n_rows, dtype=np.int64)[:, None]
    c_hi = (np.arange(n_cols // split, dtype=np.int64) * split)[None, :]
    c_lo = np.arange(split, dtype=np.int64)[None, :]
    ang = lambda c: 2.0 * np.pi * ((r * c) % period).astype(np.float64) / period
    ca, sa = jnp.asarray(np.cos(ang(c_hi)), F32), jnp.asarray(np.sin(ang(c_hi)), F32)
    cb, sb = jnp.asarray(np.cos(ang(c_lo)), F32), jnp.asarray(np.sin(ang(c_lo)), F32)
    cos = ca[:, :, None] * cb[:, None, :] - sa[:, :, None] * sb[:, None, :]
    sin = sa[:, :, None] * cb[:, None, :] + ca[:, :, None] * sb[:, None, :]
    return cos.reshape(n_rows, n_cols), sin.reshape(n_rows, n_cols)


def _trig_kernel(ta_ref, tb_ref, ea_ref, eb_ref, *o_refs, mode, period, scale):
    sa_ca = _dot(ta_ref[...], ea_ref[...])
    sb_cb = _dot(tb_ref[...], eb_ref[...])
    nc = sa_ca.shape[1] // 2
    ca, sa, cb, sb = sa_ca[:, :nc], sa_ca[:, nc:], sb_cb[:, :nc], sb_cb[:, nc:]
    cos = ca * cb - sa * sb
    sin = sa * cb + ca * sb
    tm = cos.shape[0]
    row = pl.program_id(0) * tm + lax.broadcasted_iota(I32, cos.shape, 0)
    col = lax.broadcasted_iota(I32, cos.shape, 1)
    alt = lambda idx: (1 - 2 * jnp.bitwise_and(idx, 1)).astype(F32)
    if mode == "hyena":
        fc_ref, fs_ref, gc_ref, gs_ref = o_refs
        fc_ref[...] = cos.astype(fc_ref.dtype)
        fs_ref[...] = jnp.where(row == 0, alt(col), -sin).astype(fs_ref.dtype)
        gc_ref[...] = (cos * jnp.where(col == 0, 1.0 / period, 2.0 / period)).astype(gc_ref.dtype)
        gs_ref[...] = jnp.where(col == 0, alt(row) * (1.0 / period), sin * (-2.0 / period)).astype(gs_ref.dtype)
    else:
        c_ref, ms_ref = o_refs
        c_ref[...] = (cos * scale).astype(c_ref.dtype)
        ms_ref[...] = (sin * (-scale)).astype(ms_ref.dtype)


def _split3(x):
    bf = jnp.bfloat16
    x = np.asarray(x, np.float32)
    hi = x.astype(bf)
    r1 = x - hi.astype(np.float32)
    mid = r1.astype(bf)
    lo = (r1 - mid.astype(np.float32)).astype(bf)
    return [hi, mid, lo]


def _trig_call(n, period, mode, scale=1.0, split=64, tm=256):
    r = np.arange(n, dtype=np.int64)[:, None]
    c_hi = (np.arange(n // split, dtype=np.int64) * split)[None, :]
    c_lo = np.arange(split, dtype=np.int64)[None, :]
    ang = lambda c: 2.0 * np.pi * ((r * c) % period).astype(np.float64) / period
    def operands(a, col_group):
        terms = _split3(np.cos(a)) + _split3(np.sin(a))
        k = a.shape[1]
        spread = (col_group[None, :] == np.arange(k)[:, None]).astype(np.float32)
        zero = np.zeros_like(spread)
        e = np.concatenate([np.concatenate([spread, zero], 1)] * 3 + [np.concatenate([zero, spread], 1)] * 3, 0)
        return jnp.asarray(np.concatenate(terms, axis=1)), jnp.asarray(e, BF16)
    cols = np.arange(n)
    ta, ea = operands(ang(c_hi), cols // split)
    tb, eb = operands(ang(c_lo), cols % split)
    n_out = 4 if mode == "hyena" else 2
    o_spec = pl.BlockSpec((tm, n), lambda i: (i, 0))
    return pl.pallas_call(
        functools.partial(_trig_kernel, mode=mode, period=period, scale=scale),
        grid=(n // tm,),
        in_specs=[pl.BlockSpec((tm, ta.shape[1]), lambda i: (i, 0)),
                  pl.BlockSpec((tm, tb.shape[1]), lambda i: (i, 0)),
                  pl.BlockSpec(ea.shape, lambda i: (0, 0)),
                  pl.BlockSpec(eb.shape, lambda i: (0, 0))],
        out_specs=[o_spec] * n_out,
        out_shape=[jax.ShapeDtypeStruct((n, n), BF16)] * n_out,
        compiler_params=_params(("arbitrary",)),
        name="dft_tables",
    )(ta, tb, ea, eb)


def _hyena_dft_operands(n_tok):
    return _trig_call(n_tok, 2 * n_tok, "hyena")


def _fnet_operands(n_tok, gw):
    cl, msl = _trig_call(n_tok, n_tok, "fnet", scale=1.0 / math.sqrt(n_tok * gw))
    cw, sw = _cos_sin_matrix(gw, gw, gw)
    return cl, msl, jnp.concatenate([cw, sw], axis=0).astype(BF16)


def _filter_kernel(z_ref, w1_ref, b1_ref, f1_ref, w2_ref, b2_ref, f2_ref, w3f_ref, w3b_ref, dl_ref,
                   sum_ref, dif_ref, nyq_ref, hdn_ref):
    @pl.when(jnp.logical_and(pl.program_id(0) == 0, pl.program_id(1) == 0))
    def _():
        hdot = lambda a, b: jnp.dot(a, b, precision=lax.Precision.HIGHEST, preferred_element_type=F32)
        hdn = jnp.sin(f1_ref[...] * (hdot(z_ref[...], w1_ref[...]) + b1_ref[...]))
        hdn = jnp.sin(f2_ref[...] * (hdot(hdn, w2_ref[...]) + b2_ref[...]))
        hdn_ref[...] = hdn.astype(hdn_ref.dtype)

    n = hdn_ref.shape[0]
    row = lax.broadcasted_iota(I32, (n, dl_ref.shape[1]), 0)
    decay = jnp.exp(-(row.astype(F32) / (n - 1)) * dl_ref[...])
    hf = _dot(hdn_ref[...], w3f_ref[...].astype(BF16)) * decay
    hb = jnp.where(row == 0, 0.0, _dot(hdn_ref[...], w3b_ref[...].astype(BF16)) * decay)
    tot = hf + hb
    sum_ref[...] = tot.astype(sum_ref.dtype)
    dif_ref[...] = (hf - hb).astype(dif_ref.dtype)
    nyq_ref[...] = jnp.sum(jnp.where(jnp.bitwise_and(row, 1) == 0, tot, -tot), axis=0, keepdims=True)


def _filter_call(n_tok, fw1, fb1, ff1, fw2, fb2, ff2, fw3, tc=512):
    t01 = np.linspace(0.0, 1.0, n_tok, dtype=np.float32)
    w = (2.0 * math.pi * np.arange(n_tok, dtype=np.float32) / n_tok).astype(np.float32)
    bands = np.linspace(1e-4, HY_BANDS - 1, HY_BANDS, dtype=np.float32)
    z = np.concatenate([t01[:, None], np.cos(w[:, None] * bands), -np.sin(w[:, None] * bands)], -1)
    deltas = np.abs(np.linspace(HY_MIN_DECAY, HY_MAX_DECAY, HY_W, dtype=np.float32))[None, :]
    emb, hid = fw1.shape
    nc = HY_W // tc
    full = lambda shape: pl.BlockSpec(shape, lambda o, j: (0,) * len(shape))
    o_spec = pl.BlockSpec((n_tok, tc), lambda o, j: (0, o * nc + j))
    return pl.pallas_call(
        _filter_kernel,
        grid=(HY_ORDER, nc),
        in_specs=[full((n_tok, emb)), full((emb, hid)), full((1, hid)), full((1, hid)),
                  full((hid, hid)), full((1, hid)), full((1, hid)),
                  pl.BlockSpec((hid, tc), lambda o, j: (0, (2 * o) * nc + j)),
                  pl.BlockSpec((hid, tc), lambda o, j: (0, (2 * o + 1) * nc + j)),
                  pl.BlockSpec((1, tc), lambda o, j: (0, j))],
        out_specs=[o_spec, o_spec, pl.BlockSpec((1, tc), lambda o, j: (0, o * nc + j))],
        out_shape=[jax.ShapeDtypeStruct((n_tok, HY_ORDER * HY_W), BF16),
                   jax.ShapeDtypeStruct((n_tok, HY_ORDER * HY_W), BF16),
                   jax.ShapeDtypeStruct((1, HY_ORDER * HY_W), F32)],
        scratch_shapes=[pltpu.VMEM((n_tok, hid), BF16)],
        compiler_params=_params(("arbitrary", "arbitrary")),
        name="hyena_filters",
    )(jnp.asarray(z.astype(np.float32)), fw1, fb1.reshape(1, hid), ff1.reshape(1, hid),
      fw2, fb2.reshape(1, hid), ff2.reshape(1, hid), fw3, fw3, jnp.asarray(deltas))


def kernel(x, c, ctx, c_ctx, ada_w, ada_b, ln1_g, ln1_b, ln2_g, ln2_b, router, exp_w1, exp_w3, exp_w2,
           ab_w_in, ab_kv_norm, ab_w_ukv, ab_rpb, ab_w_out,
           cd_w_in, cd_conv_w, cd_conv_b, cd_filt_w1, cd_filt_b1, cd_filt_freq1, cd_filt_w2, cd_filt_b2,
           cd_filt_freq2, cd_filt_w3, cd_skip, cd_w_out):
    batch, n_tok, d = x.shape
    lc = ctx.shape[1]
    x2d = x.reshape(batch * n_tok, d)
    ctx2d = ctx.reshape(batch * lc, d)

    pad_rows = (-(batch + 1)) % 8
    cv = jnp.concatenate([c, c_ctx[None, :], jnp.zeros((pad_rows, d), F32)], axis=0)
    ada = _ada_call(cv, ada_w, ada_b)
    mods = [ada[i, :batch].reshape(batch, 6, d) for i in range(DEPTH)]
    ctx_ss = ada[0, batch, :2 * d].reshape(1, 2, d)

    w_in = _w_in_prep_call(jnp.swapaxes(ab_w_in, 1, 2), 0)
    p_lat = _mm_rope_call(x2d, mods[0][:, 0:2], n_tok, w_in, _rope_tables(n_tok), tm=1024)
    p_ctx = _mm_rope_call(ctx2d, ctx_ss, batch * lc, w_in, _identity_rope_tables(lc), tm=lc)
    w_ukv = _permute_w_ukv(ab_w_ukv[0])
    kv_lat = _kvup_call(p_lat, ab_kv_norm[0], w_ukv, tm=512)
    kv_ctx = _kvup_call(p_ctx, ab_kv_norm[0], w_ukv, tm=lc)
    a_out = _mla_call(p_lat, kv_lat, p_ctx, kv_ctx, batch, n_tok, lc)
    var_map, na_er, na_ec = _na_tables(n_tok // GRID_W)
    b_out = _na_call(p_lat, p_ctx, _na_bias(ab_rpb[0], na_er, na_ec), jnp.asarray(var_map),
                     batch, n_tok, lc)
    ln1 = jnp.stack([ln1_g, ln1_b], axis=1)
    ln2 = jnp.stack([ln2_g, ln2_b], axis=1)
    h1, u2, logits = _outln_call(a_out, b_out, ab_w_out[0].astype(BF16), x2d,
                                 mods[0][:, 2:5], ln1[0], _router_operand(router[0]), n_tok)
    h, u = _ec_moe(h1, u2, logits, exp_w1, exp_w3, exp_w2, 0, mods[0][:, 5:6], ln2[0],
                   mods[1][:, 0:2], batch, n_tok)

    p1 = _mm_call(u, cd_w_in[0].astype(BF16), BF16, tn=1024)
    s0 = _sconv_call(p1, cd_conv_w[0], cd_conv_b[0], 0, HY_W, BF16, batch, n_tok)
    h_sum, h_dif, h_nyq = _filter_call(n_tok, cd_filt_w1[0], cd_filt_b1[0], cd_filt_freq1[0], cd_filt_w2[0],
                                       cd_filt_b2[0], cd_filt_freq2[0], cd_filt_w3[0])
    fc, fs, gc, gs = _hyena_dft_operands(n_tok)
    h_spec = _dft_fwd_call(fc, fs, h_sum, h_dif, None, F32, 1, n_tok)
    z = s0
    for o in range(HY_ORDER):
        y_spec = _dft_fwd_call(fc, fs, z, z, (h_spec, h_nyq, o * HY_W), BF16, batch, n_tok)
        z = _dual_call(gc, gs, y_spec, y_spec,
                       (p1, cd_conv_w[0], cd_conv_b[0], (o + 1) * HY_W, z, cd_skip[0][o:o + 1]),
                       batch, n_tok, HY_W)
    cl, msl, cw = _fnet_operands(n_tok, FN_GROUP_W)
    y_fn = _fnet_call(cl, msl, p1, cw, HY_IN_W, batch, n_tok)
    h1, u2, logits = _outln_call(z, y_fn, cd_w_out[0].astype(BF16), h,
                                 mods[1][:, 2:5], ln1[1], _router_operand(router[1]), n_tok)
    h, _ = _ec_moe(h1, u2, logits, exp_w1, exp_w3, exp_w2, 1, mods[1][:, 5:6], ln2[1],
                   None, batch, n_tok)
    return h.reshape(batch, n_tok, d)
```

```python
import functools
import math

import numpy as np
import jax
import jax.numpy as jnp
from jax import lax
from jax.experimental import pallas as pl
from jax.experimental.pallas import tpu as pltpu

F32 = jnp.float32
BF16 = jnp.bfloat16
I32 = jnp.int32

D_MODEL = 2048
BATCH = 4
SEQ = 2048
DEPTH = 2
CTX_LEN = 256
GRID_W = 64
DEEPNORM_ALPHA = (2.0 * DEPTH) ** 0.25
LN_EPS = 1e-6
NEG_INF = -1e30
LOG2E = math.log2(math.e)

MLA_HEADS = 8
MLA_NOPE = 128
MLA_ROPE = 64
MLA_QK = MLA_NOPE + MLA_ROPE
MLA_V = 128
MLA_KV_RANK = 512
ROPE_THETA = 10000.0

NA_HEADS = 8
NA_HEAD_DIM = 128
NA_KH = 8
NA_KW = 16

OFF_CKV = MLA_HEADS * MLA_QK
OFF_KPE = OFF_CKV + MLA_KV_RANK
OFF_QNA = OFF_KPE + MLA_ROPE
OFF_KNA = OFF_QNA + NA_HEADS * NA_HEAD_DIM
OFF_VNA = OFF_KNA + NA_HEADS * NA_HEAD_DIM

HY_W = 1024
HY_ORDER = 2
HY_IN_W = (HY_ORDER + 1) * HY_W
HY_SHORT = 3
HY_BANDS = 16
HY_DECAY_TARGET = 1e-2
HY_MIN_DECAY = math.log(HY_DECAY_TARGET) / 1.5
HY_MAX_DECAY = math.log(HY_DECAY_TARGET) / 0.3

FN_W = 1024
FN_GROUPS = 4
FN_GROUP_W = FN_W // FN_GROUPS

N_EXPERTS = 16
EC_CAPACITY_FACTOR = 2
EXPERT_FF = 1408

LANES = 128
MXU_DIM = 256
VMEM_LIMIT = 56 * 1024 * 1024

QTILE_W = 2 * LANES
T_KPE = MLA_HEADS
T_CKV = T_KPE + 2
T_QNA = T_CKV + MLA_KV_RANK // QTILE_W
T_KNA = T_QNA + NA_HEADS * NA_HEAD_DIM // QTILE_W
T_VNA = T_KNA + NA_HEADS * NA_HEAD_DIM // QTILE_W
N_ABTILES = T_VNA + NA_HEADS * NA_HEAD_DIM // QTILE_W
AB_PERM_W = N_ABTILES * QTILE_W
CTX_TILE0 = T_KPE
INPROJ_TILES_PER_STEP = 8

MLA_SUBTILES = 8
OUTLN_SUBTILES = 2
MOE_CHUNK = 256
MOE_WIN = 64
NA_G = 4
NA_WR = NA_KH + NA_G - 1
NA_GROUPS_PER_STEP = 8


def _params(sem, vmem=VMEM_LIMIT):
    return pltpu.CompilerParams(dimension_semantics=sem, vmem_limit_bytes=vmem)


def _dot(a, b):
    return jnp.dot(a, b, preferred_element_type=F32)


def _dot_nt(a, b):
    return lax.dot_general(a, b, (((1,), (1,)), ((), ())), preferred_element_type=F32)


def _dot_tn(a, b):
    return lax.dot_general(a, b, (((0,), (0,)), ((), ())), preferred_element_type=F32)


def _split_bf16(x):
    hi = x.astype(BF16)
    lo = (x - hi.astype(F32)).astype(BF16)
    return hi, lo


def _ada_kernel(c_ref, w_ref, b_ref, o_ref):
    c = c_ref[...]
    a = c / (1.0 + jnp.exp(-c))
    a_hi, a_lo = _split_bf16(a)
    w_hi, w_lo = _split_bf16(w_ref[0])
    rows = a.shape[0]
    r = _dot(jnp.concatenate([a_hi, a_lo], axis=0), w_hi)
    o_ref[0] = r[:rows] + r[rows:] + _dot(a_hi, w_lo) + b_ref[0]


def _ada_call(cv, ada_w, ada_b, tn=1024):
    depth, d, n = ada_w.shape
    rows = cv.shape[0]
    return pl.pallas_call(
        _ada_kernel,
        grid=(depth, n // tn),
        in_specs=[pl.BlockSpec((rows, d), lambda l, j: (0, 0)),
                  pl.BlockSpec((1, d, tn), lambda l, j: (l, 0, j)),
                  pl.BlockSpec((1, 1, tn), lambda l, j: (l, 0, j))],
        out_specs=pl.BlockSpec((1, rows, tn), lambda l, j: (l, 0, j)),
        out_shape=jax.ShapeDtypeStruct((depth, rows, n), F32),
        compiler_params=_params(("arbitrary", "arbitrary")),
        name="ada_params",
    )(cv, ada_w, ada_b.reshape(depth, 1, n))


def _mm_rope_kernel(x_ref, ss_ref, w_ref, ta_ref, tb_ref, tc_ref, o_ref, a_ref, *, tile0):
    @pl.when(pl.program_id(1) == 0)
    def _():
        a_ref[...] = (x_ref[...] * (1.0 + ss_ref[0, 1:2, :]) + ss_ref[0, 0:1, :]).astype(a_ref.dtype)

    a = a_ref[...]
    for q in range(INPROJ_TILES_PER_STEP):
        t = tile0 + INPROJ_TILES_PER_STEP * pl.program_id(1) + q
        c0 = q * QTILE_W
        acc = _dot_nt(a, w_ref[c0:c0 + QTILE_W, :])
        is_rope = t <= T_KPE
        f = jnp.where(t < T_KPE, LOG2E * MLA_QK ** -0.5,
                      jnp.where(jnp.logical_and(t >= T_QNA, t < T_KNA), LOG2E * NA_HEAD_DIM ** -0.5, 1.0)
                      ).astype(F32)
        hi = acc[:, LANES:]
        rot = (hi * jnp.where(is_rope, ta_ref[...], 1.0)
               + pltpu.roll(hi, LANES - MLA_ROPE // 2, 1) * jnp.where(is_rope, tb_ref[...], 0.0)
               + pltpu.roll(hi, MLA_ROPE // 2, 1) * jnp.where(is_rope, tc_ref[...], 0.0))
        o_ref[:, c0:c0 + LANES] = (acc[:, :LANES] * f).astype(o_ref.dtype)
        o_ref[:, c0 + LANES:c0 + QTILE_W] = (rot * f).astype(o_ref.dtype)


def _mm_rope_call(x, ss, rows_per_group, w, tabs, tm, tile0=0):
    m, k = x.shape
    tn = INPROJ_TILES_PER_STEP * QTILE_W
    n = w.shape[0] - tile0 * QTILE_W
    j0 = tile0 // INPROJ_TILES_PER_STEP
    ntab = tabs[0].shape[0] // tm
    tpg = rows_per_group // tm
    tab_spec = pl.BlockSpec((tm, LANES), lambda i, j: (i % ntab, 0))
    return pl.pallas_call(
        functools.partial(_mm_rope_kernel, tile0=tile0),
        grid=(m // tm, n // tn),
        in_specs=[pl.BlockSpec((tm, k), lambda i, j: (i, 0)),
                  pl.BlockSpec((1, 2, k), lambda i, j: (i // tpg, 0, 0)),
                  pl.BlockSpec((tn, k), lambda i, j: (j0 + j, 0)),
                  tab_spec, tab_spec, tab_spec],
        out_specs=pl.BlockSpec((tm, tn), lambda i, j: (i, j)),
        out_shape=jax.ShapeDtypeStruct((m, n), BF16),
        scratch_shapes=[pltpu.VMEM((tm, k), BF16)],
        compiler_params=_params(("arbitrary", "arbitrary")),
        name="attn_in_proj",
    )(x, ss, w, *tabs)


def _rope_tables(n_tok):
    t = np.arange(n_tok)
    row = (t // GRID_W).astype(np.float32)
    col = (t % GRID_W).astype(np.float32)
    n_freq = MLA_ROPE // 4
    inv = (ROPE_THETA ** (-np.arange(n_freq, dtype=np.float32) / n_freq)).astype(np.float32)
    ang = np.concatenate([row[:, None] * inv, col[:, None] * inv], axis=1)
    cos, sin = np.cos(ang).astype(np.float32), np.sin(ang).astype(np.float32)
    half = MLA_ROPE // 2
    ta = np.zeros((n_tok, LANES), np.float32)
    tb = np.zeros((n_tok, LANES), np.float32)
    tc = np.zeros((n_tok, LANES), np.float32)
    ta[:, :half] = cos
    ta[:, half:2 * half] = cos
    tb[:, :half] = -sin
    tc[:, half:2 * half] = sin
    return jnp.asarray(ta), jnp.asarray(tb), jnp.asarray(tc)


def _identity_rope_tables(n_tok):
    ta = np.zeros((n_tok, LANES), np.float32)
    ta[:, :MLA_ROPE] = 1.0
    z = np.zeros((n_tok, LANES), np.float32)
    return jnp.asarray(ta), jnp.asarray(z), jnp.asarray(z)


def _w_in_prep_kernel(w_ref, o_ref):
    x = w_ref[0]
    zeros = lambda n: jnp.zeros((n, x.shape[1]), x.dtype)
    rows = []
    for h in range(MLA_HEADS):
        rows += [x[h * MLA_QK:(h + 1) * MLA_QK, :], zeros(QTILE_W - MLA_QK)]
    rows += [zeros(LANES), x[OFF_KPE:OFF_QNA, :], zeros(LANES - MLA_ROPE), zeros(QTILE_W),
             x[OFF_CKV:OFF_KPE, :], x[OFF_QNA:, :]]
    o_ref[...] = jnp.concatenate(rows, axis=0).astype(o_ref.dtype)


def _w_in_prep_call(w_in_t, layer, tk=256):
    _, n, d = w_in_t.shape
    return pl.pallas_call(
        _w_in_prep_kernel,
        grid=(d // tk,),
        in_specs=[pl.BlockSpec((1, n, tk), lambda i: (layer, 0, i))],
        out_specs=pl.BlockSpec((AB_PERM_W, tk), lambda i: (0, i)),
        out_shape=jax.ShapeDtypeStruct((AB_PERM_W, d), BF16),
        compiler_params=_params(("arbitrary",)),
        name="attn_w_in_prep",
    )(w_in_t)


def _kvup_kernel(x_ref, g_ref, w_ref, o_ref):
    x = x_ref[...].astype(F32)
    y = x * lax.rsqrt(jnp.mean(x * x, axis=-1, keepdims=True) + LN_EPS) * g_ref[...]
    o_ref[...] = _dot(y.astype(BF16), w_ref[...]).astype(o_ref.dtype)


def _kvup_call(p, g, w, tm, tile0=0):
    m = p.shape[0]
    r, n = w.shape
    return pl.pallas_call(
        _kvup_kernel,
        grid=(m // tm,),
        in_specs=[pl.BlockSpec((tm, r), lambda i: (i, (T_CKV - tile0) * QTILE_W // r)),
                  pl.BlockSpec((1, r), lambda i: (0, 0)),
                  pl.BlockSpec((r, n), lambda i: (0, 0))],
        out_specs=pl.BlockSpec((tm, n), lambda i: (i, 0)),
        out_shape=jax.ShapeDtypeStruct((m, n), BF16),
        compiler_params=_params(("arbitrary",)),
        name="kv_up",
    )(p, g.reshape(1, r), w)


def _permute_w_ukv(w):
    r = w.shape[0]
    w3 = w.reshape(r, MLA_HEADS, MLA_NOPE + MLA_V)
    return jnp.concatenate([w3[:, :, :MLA_NOPE].reshape(r, -1), w3[:, :, MLA_NOPE:].reshape(r, -1)],
                           axis=1).astype(BF16)


def _mla_kernel(q_ref, kn_ref, kp_ref, v_ref, kcn_ref, kcp_ref, vc_ref, o_ref, kcat, vcat):
    s_len = kn_ref.shape[0]

    @pl.when(pl.program_id(2) == 0)
    def _():
        kcat[:s_len, :LANES] = kn_ref[...]
        kcat[:s_len, LANES:] = kp_ref[...]
        kcat[s_len:, :LANES] = kcn_ref[...]
        kcat[s_len:, LANES:] = kcp_ref[...]
        vcat[:s_len, :] = v_ref[...]
        vcat[s_len:, :] = vc_ref[...]

    sub = q_ref.shape[0] // MLA_SUBTILES
    for r0 in range(0, q_ref.shape[0], sub):
        s = _dot_nt(q_ref[r0:r0 + sub, :], kcat[...])
        m = jnp.max(s, axis=-1, keepdims=True)
        p = jnp.exp2(s - m)
        l = jnp.sum(p, axis=-1, keepdims=True)
        o = _dot(p.astype(BF16), vcat[...])
        o_ref[r0:r0 + sub, :] = (o / l).astype(o_ref.dtype)


def _mla_call(p_lat, kv_lat, p_ctx, kv_ctx, batch, s_len, lc, tq=2048):
    nq = s_len // tq
    kpe_blk = T_KPE * 2 + 1
    return pl.pallas_call(
        _mla_kernel,
        grid=(batch, MLA_HEADS, nq),
        in_specs=[pl.BlockSpec((tq, QTILE_W), lambda b, h, i: (b * nq + i, h)),
                  pl.BlockSpec((s_len, LANES), lambda b, h, i: (b, h)),
                  pl.BlockSpec((s_len, LANES), lambda b, h, i: (b, kpe_blk)),
                  pl.BlockSpec((s_len, LANES), lambda b, h, i: (b, MLA_HEADS + h)),
                  pl.BlockSpec((lc, LANES), lambda b, h, i: (b, h)),
                  pl.BlockSpec((lc, LANES), lambda b, h, i: (b, kpe_blk - 2 * CTX_TILE0)),
                  pl.BlockSpec((lc, LANES), lambda b, h, i: (b, MLA_HEADS + h))],
        out_specs=pl.BlockSpec((tq, LANES), lambda b, h, i: (b * nq + i, h)),
        out_shape=jax.ShapeDtypeStruct((batch * s_len, MLA_HEADS * MLA_V), BF16),
        scratch_shapes=[pltpu.VMEM((s_len + lc, QTILE_W), BF16),
                        pltpu.VMEM((s_len + lc, LANES), BF16)],
        compiler_params=_params(("arbitrary", "arbitrary", "arbitrary")),
        name="mla_attention",
    )(p_lat, kv_lat, p_lat, kv_lat, kv_ctx, p_ctx, kv_ctx)


def _na_kernel(var_ref, q_ref, k_ref, v_ref, kc_ref, vc_ref, *rest, rows):
    del var_ref
    bias_refs, o_ref = rest[:-1], rest[-1]
    win = NA_WR * GRID_W
    tq = NA_G * GRID_W
    for i, bias_ref in enumerate(bias_refs):
        g = len(bias_refs) * pl.program_id(2) + i
        start_row = jnp.clip(NA_G * g - NA_KH // 2, 0, rows - NA_WR)
        start = pl.multiple_of(start_row * GRID_W, GRID_W)
        q = q_ref[i * tq:(i + 1) * tq, :]
        sw = _dot_nt(q, k_ref[pl.ds(start, win), :]) + bias_ref[0, 0]
        sc = _dot_nt(q, kc_ref[...])
        m = jnp.maximum(jnp.max(sw, axis=-1, keepdims=True), jnp.max(sc, axis=-1, keepdims=True))
        pw = jnp.exp2(sw - m)
        pc = jnp.exp2(sc - m)
        l = jnp.sum(pw, axis=-1, keepdims=True) + jnp.sum(pc, axis=-1, keepdims=True)
        o = _dot(pw.astype(BF16), v_ref[pl.ds(start, win), :]) + _dot(pc.astype(BF16), vc_ref[...])
        o_ref[i * tq:(i + 1) * tq, :] = (o / l).astype(o_ref.dtype)


def _na_tables(rows):
    kh = min(NA_KH, rows)
    ng = rows // NA_G
    qr = np.arange(NA_G)[:, None]
    kr = np.arange(NA_WR)[None, :]
    sel_rows = []
    for g in range(ng):
        start_row = int(np.clip(NA_G * g - NA_KH // 2, 0, rows - NA_WR))
        r = NA_G * g + qr
        krow = start_row + kr
        rs = np.clip(r - kh // 2, 0, rows - kh)
        ok = (krow >= rs) & (krow < rs + kh)
        dr = krow - r + (NA_KH - 1)
        sel_rows.append(np.stack([ok & (dr == d) for d in range(2 * NA_KH - 1)]))
    sel_rows = np.stack(sel_rows)
    uniq, inverse = np.unique(sel_rows.reshape(ng, -1), axis=0, return_inverse=True)
    er = uniq.reshape((-1,) + sel_rows.shape[1:]).astype(np.float32)
    qc = np.arange(GRID_W)[:, None]
    kc = np.arange(GRID_W)[None, :]
    cs = np.clip(qc - NA_KW // 2, 0, GRID_W - NA_KW)
    ok_c = (kc >= cs) & (kc < cs + NA_KW)
    dc = np.clip(kc - qc, -(NA_KW - 1), NA_KW - 1) + (NA_KW - 1)
    ec = np.stack([ok_c & (dc == e) for e in range(2 * NA_KW - 1)]).astype(np.float32)
    return inverse.reshape(-1).astype(np.int32), er, ec


def _na_bias(rpb, er, ec):
    hp = lax.Precision.HIGHEST
    t = jnp.einsum('hde,eqk->hdqk', rpb.astype(F32), jnp.asarray(ec), precision=hp)
    t = jnp.where(jnp.asarray(ec.sum(axis=0) > 0.5), t * LOG2E, NEG_INF)
    outside = jnp.full(t[:, 0].shape, NEG_INF, F32)
    d_of = np.where(er.sum(axis=1) > 0.5, er.argmax(axis=1), -1)
    block = lambda d: outside if d < 0 else t[:, d]
    return jnp.stack([
        jnp.concatenate([jnp.concatenate([block(int(d)) for d in row], axis=-1) for row in var], axis=-2)
        for var in d_of])


def _na_call(p_lat, p_ctx, bias, var_map, batch, s_len, lc):
    rows = s_len // GRID_W
    gps = NA_GROUPS_PER_STEP
    ns = rows // (NA_G * gps)
    tq = NA_G * GRID_W
    win = NA_WR * GRID_W
    qb, kb, vb = T_QNA * 2, T_KNA * 2, T_VNA * 2
    bias_spec = lambda i: pl.BlockSpec((1, 1, tq, win), lambda b, h, s, vm: (vm[gps * s + i], h, 0, 0))
    grid_spec = pltpu.PrefetchScalarGridSpec(
        num_scalar_prefetch=1,
        grid=(batch, NA_HEADS, ns),
        in_specs=[pl.BlockSpec((gps * tq, LANES), lambda b, h, s, vm: (b * ns + s, qb + h)),
                  pl.BlockSpec((s_len, LANES), lambda b, h, s, vm: (b, kb + h)),
                  pl.BlockSpec((s_len, LANES), lambda b, h, s, vm: (b, vb + h)),
                  pl.BlockSpec((lc, LANES), lambda b, h, s, vm: (b, kb + h - 2 * CTX_TILE0)),
                  pl.BlockSpec((lc, LANES), lambda b, h, s, vm: (b, vb + h - 2 * CTX_TILE0))]
                 + [bias_spec(i) for i in range(gps)],
        out_specs=pl.BlockSpec((gps * tq, LANES), lambda b, h, s, vm: (b * ns + s, h)),
    )
    return pl.pallas_call(
        functools.partial(_na_kernel, rows=rows),
        grid_spec=grid_spec,
        out_shape=jax.ShapeDtypeStruct((batch * s_len, NA_HEADS * NA_HEAD_DIM), BF16),
        compiler_params=_params(("arbitrary", "arbitrary", "arbitrary")),
        name="na_attention",
    )(var_map, p_lat, p_lat, p_lat, p_ctx, p_ctx, *([bias] * gps))


def _layer_norm(x, g, b):
    mu = jnp.mean(x, axis=-1, keepdims=True)
    xc = x - mu
    var = jnp.mean(xc * xc, axis=-1, keepdims=True)
    return xc * lax.rsqrt(var + LN_EPS) * g + b


def _outln_kernel(a1_ref, a2_ref, w_ref, h_ref, mod_ref, ln_ref, r_ref, h1_ref, u2_ref, lg_ref):
    half = a1_ref.shape[1]
    sub = a1_ref.shape[0] // OUTLN_SUBTILES
    for r0 in range(0, a1_ref.shape[0], sub):
        rs = slice(r0, r0 + sub)
        y = _dot(a1_ref[rs, :], w_ref[:half, :]) + _dot(a2_ref[rs, :], w_ref[half:, :])
        x = DEEPNORM_ALPHA * h_ref[rs, :] + mod_ref[0, 0:1, :] * y
        hn = _layer_norm(x, ln_ref[0:1, :], ln_ref[1:2, :])
        h1_ref[rs, :] = hn
        u = hn * (1.0 + mod_ref[0, 2:3, :]) + mod_ref[0, 1:2, :]
        u_bf = u.astype(BF16)
        u2_ref[rs, :] = u_bf
        lg_ref[rs, :] = _dot(u_bf, r_ref[...])


def _outln_call(a1, a2, w, h, mod, ln, router2, rows_per_group, tm=512):
    m, d = h.shape
    half = a1.shape[1]
    tpg = rows_per_group // tm
    return pl.pallas_call(
        _outln_kernel,
        grid=(m // tm,),
        in_specs=[pl.BlockSpec((tm, half), lambda i: (i, 0)),
                  pl.BlockSpec((tm, half), lambda i: (i, 0)),
                  _resident_spec((2 * half, d)),
                  pl.BlockSpec((tm, d), lambda i: (i, 0)),
                  pl.BlockSpec((1, 3, d), lambda i: (i // tpg, 0, 0)),
                  pl.BlockSpec((2, d), lambda i: (0, 0)),
                  _resident_spec((d, LANES))],
        out_specs=[pl.BlockSpec((tm, d), lambda i: (i, 0)),
                   pl.BlockSpec((tm, d), lambda i: (i, 0)),
                   pl.BlockSpec((tm, LANES), lambda i: (i, 0))],
        out_shape=[jax.ShapeDtypeStruct((m, d), F32),
                   jax.ShapeDtypeStruct((m, d), BF16),
                   jax.ShapeDtypeStruct((m, LANES), F32)],
        compiler_params=_params(("arbitrary",)),
        name="out_proj_postnorm",
    )(a1, a2, w, h, mod, ln, router2)


def _router_operand(router):
    d, e = router.shape
    return jnp.pad(router, ((0, 0), (0, LANES - e))).astype(BF16)


def _route_kernel(lg_ref, slot_ref, aff_ref, cnt_ref, tri_ref, *, cap):
    n_tok = lg_ref.shape[2]

    @pl.when(pl.program_id(0) == 0)
    def _():
        chunk = 256
        for r0 in range(0, n_tok, chunk):
            r = r0 + lax.broadcasted_iota(I32, (chunk, n_tok), 0)
            c = lax.broadcasted_iota(I32, (chunk, n_tok), 1)
            tri_ref[r0:r0 + chunk, :] = jnp.where(r < c, 1.0, 0.0).astype(BF16)

    lg = lg_ref[0]
    ex = jnp.exp(lg - jnp.max(lg, axis=0, keepdims=True))
    aff = ex / jnp.sum(ex, axis=0, keepdims=True)
    bits = lax.bitcast_convert_type(aff, I32)
    n_e = lg.shape[0]
    count = lambda mask: jnp.sum(jnp.where(mask, 1.0, 0.0), axis=1, keepdims=True)

    def body(_, lohi):
        lo, hi = lohi
        mid = lo + jnp.right_shift(hi - lo, 1)
        ge = count(bits >= mid) >= cap
        return jnp.where(ge, mid, lo), jnp.where(ge, hi, mid)

    lo0 = jnp.zeros((n_e, 1), I32)
    hi0 = jnp.full((n_e, 1), 0x7F800000, I32)
    thr, _ = lax.fori_loop(0, 31, body, (lo0, hi0))
    gt = bits > thr
    eq = bits == thr
    need = cap - count(gt)
    pre_eq = _dot(jnp.where(eq, 1.0, 0.0).astype(BF16), tri_ref[...])
    sel = jnp.logical_or(gt, jnp.logical_and(eq, pre_eq < need))
    sel_bf = jnp.where(sel, 1.0, 0.0).astype(BF16)
    slot = _dot(sel_bf, tri_ref[...])
    slot_ref[0] = jnp.where(sel, slot.astype(I32), -1)
    aff_ref[0] = aff
    r = lax.broadcasted_iota(I32, (n_tok, LANES), 0)
    c = lax.broadcasted_iota(I32, (n_tok, LANES), 1)
    before = jnp.where(r < c * MOE_CHUNK, 1.0, 0.0).astype(BF16)
    cnt_ref[0] = _dot(sel_bf, before).astype(I32)


def _route_call(lg_t, cap):
    b, e, n_tok = lg_t.shape
    spec = pl.BlockSpec((1, e, n_tok), lambda i: (i, 0, 0))
    return pl.pallas_call(
        functools.partial(_route_kernel, cap=cap),
        grid=(b,),
        in_specs=[spec],
        out_specs=[spec, spec, pl.BlockSpec((1, e, LANES), lambda i: (i, 0, 0))],
        out_shape=[jax.ShapeDtypeStruct((b, e, n_tok), I32), jax.ShapeDtypeStruct((b, e, n_tok), F32),
                   jax.ShapeDtypeStruct((b, e, LANES), I32)],
        scratch_shapes=[pltpu.VMEM((n_tok, n_tok), BF16)],
        compiler_params=_params(("arbitrary",)),
        name="ec_route",
    )(lg_t)


def _slot_windows(cnt, cap, n_chunk):
    start = cnt[:, :, :n_chunk]
    end = cnt[:, :, 1:n_chunk + 1]
    a = jnp.minimum((start // 16) * 16, cap - MOE_WIN)
    fits = jnp.all(end <= a + MOE_WIN, axis=1)
    return jnp.swapaxes(a, 1, 2).reshape(-1).astype(I32), fits.reshape(-1).astype(I32)


def _gather_kernel(win_ref, fit_ref, slot_ref, u_ref, o_ref, *, group):
    b, kc = pl.program_id(0), pl.program_id(1)
    n_e, cap = o_ref.shape[0], o_ref.shape[1]
    chunk = u_ref.shape[0]
    step = b * pl.num_programs(1) + kc

    @pl.when(kc == 0)
    def _():
        o_ref[...] = jnp.zeros(o_ref.shape, o_ref.dtype)

    def place(rows, starts, group):
        row = lax.broadcasted_iota(I32, (rows, chunk), 0)
        for g0 in range(0, n_e, group):
            onehot = jnp.concatenate(
                [jnp.where(row + starts[e] == slot_ref[0, e:e + 1, :], 1.0, 0.0).astype(BF16)
                 for e in range(g0, g0 + group)], axis=0)
            got = _dot(onehot, u_ref[...]).astype(o_ref.dtype)
            for i, e in enumerate(range(g0, g0 + group)):
                o_ref[e, pl.ds(starts[e], rows), :] += got[i * rows:(i + 1) * rows]

    @pl.when(fit_ref[step] != 0)
    def _():
        place(MOE_WIN, [pl.multiple_of(win_ref[step * n_e + e], 16) for e in range(n_e)], group)

    @pl.when(fit_ref[step] == 0)
    def _():
        place(cap, [0] * n_e, max(1, group * MOE_WIN // cap))


def _gather_call(win, fit, slot, u2, cap, group=512 // MOE_WIN):
    b, e, n_tok = slot.shape
    d = u2.shape[1]
    nc = n_tok // MOE_CHUNK
    grid_spec = pltpu.PrefetchScalarGridSpec(
        num_scalar_prefetch=2,
        grid=(b, nc),
        in_specs=[pl.BlockSpec((1, e, MOE_CHUNK), lambda i, k, w, f: (i, 0, k)),
                  pl.BlockSpec((MOE_CHUNK, d), lambda i, k, w, f: (i * nc + k, 0))],
        out_specs=pl.BlockSpec((e, cap, d), lambda i, k, w, f: (0, i, 0)),
    )
    return pl.pallas_call(
        functools.partial(_gather_kernel, group=group),
        grid_spec=grid_spec,
        out_shape=jax.ShapeDtypeStruct((e, b * cap, d), BF16),
        compiler_params=_params(("arbitrary", "arbitrary")),
        name="moe_gather",
    )(win, fit, slot, u2)


def _expert_hidden_kernel(x_ref, w1_ref, w3_ref, o_ref):
    tf = w1_ref.shape[3]
    h = _dot(x_ref[0], jnp.concatenate([w1_ref[0, 0].astype(BF16), w3_ref[0, 0].astype(BF16)], axis=1))
    h1 = h[:, :tf]
    o_ref[0] = ((h1 / (1.0 + jnp.exp(-h1))) * h[:, tf:]).astype(o_ref.dtype)


def _expert_out_kernel(hd_ref, w2_ref, o_ref):
    y = _dot(hd_ref[0], w2_ref[0, 0].astype(BF16)).astype(o_ref.dtype)
    cap = o_ref.shape[1]
    for b in range(o_ref.shape[0]):
        o_ref[b] = y[b * cap:(b + 1) * cap]


def _expert_call(xe, w1, w3, w2, layer, batch, tf=512, tn=1024):
    e, m, d = xe.shape
    cap = m // batch
    ff = w1.shape[3]
    hidden = pl.pallas_call(
        _expert_hidden_kernel,
        grid=(e, pl.cdiv(ff, tf)),
        in_specs=[pl.BlockSpec((1, m, d), lambda i, f: (i, 0, 0)),
                  pl.BlockSpec((1, 1, d, tf), lambda i, f: (layer, i, 0, f)),
                  pl.BlockSpec((1, 1, d, tf), lambda i, f: (layer, i, 0, f))],
        out_specs=pl.BlockSpec((1, m, tf), lambda i, f: (i, 0, f)),
        out_shape=jax.ShapeDtypeStruct((e, m, ff), BF16),
        compiler_params=_params(("arbitrary", "arbitrary")),
        name="moe_expert_hidden",
    )(xe, w1, w3)
    return pl.pallas_call(
        _expert_out_kernel,
        grid=(e, d // tn),
        in_specs=[pl.BlockSpec((1, m, ff), lambda i, j: (i, 0, 0)),
                  pl.BlockSpec((1, 1, ff, tn), lambda i, j: (layer, i, 0, j))],
        out_specs=pl.BlockSpec((batch, cap, tn), lambda i, j: (0, i, j)),
        out_shape=jax.ShapeDtypeStruct((batch, e * cap, d), BF16),
        compiler_params=_params(("arbitrary", "arbitrary")),
        name="moe_expert_out",
    )(hidden, w2)


def _scatter_kernel(win_ref, fit_ref, slot_ref, aff_ref, ye_ref, h_ref, mod_ref, ln_ref, *rest, with_next):
    if with_next:
        nmod_ref, h2_ref, un_ref, pt_ref, yew_ref, moe_ref = rest
    else:
        h2_ref, pt_ref, yew_ref, moe_ref = rest
    n_e = slot_ref.shape[1]
    cap = ye_ref.shape[1] // n_e
    tt = h_ref.shape[0]
    step = pl.program_id(0) * pl.num_programs(1) + pl.program_id(1)

    def gates(rows, e, start):
        row = lax.broadcasted_iota(I32, (rows, tt), 0)
        return jnp.where(row + start == slot_ref[0, e:e + 1, :], aff_ref[0, e:e + 1, :], 0.0).astype(BF16)

    @pl.when(fit_ref[step] != 0)
    def _():
        for e in range(n_e):
            start = pl.multiple_of(win_ref[step * n_e + e], 16)
            pt_ref[e * MOE_WIN:(e + 1) * MOE_WIN, :] = gates(MOE_WIN, e, start)
            yew_ref[e * MOE_WIN:(e + 1) * MOE_WIN, :] = ye_ref[0, pl.ds(e * cap + start, MOE_WIN), :]
        moe_ref[...] = _dot_tn(pt_ref[:n_e * MOE_WIN, :], yew_ref[...])

    @pl.when(fit_ref[step] == 0)
    def _():
        for e in range(n_e):
            pt_ref[e * cap:(e + 1) * cap, :] = gates(cap, e, 0)
        moe_ref[...] = _dot_tn(pt_ref[...], ye_ref[0])

    x = DEEPNORM_ALPHA * h_ref[...] + mod_ref[0] * moe_ref[...]
    hn = _layer_norm(x, ln_ref[0:1, :], ln_ref[1:2, :])
    h2_ref[...] = hn
    if with_next:
        un_ref[...] = (hn * (1.0 + nmod_ref[0, 1:2, :]) + nmod_ref[0, 0:1, :]).astype(un_ref.dtype)


def _scatter_call(win, fit, slot, aff, ye, h1, gate, ln, next_ss):
    b, e, n_tok = slot.shape
    m, d = h1.shape
    tt = MOE_CHUNK
    nt = n_tok // tt
    rows = ye.shape[1]
    with_next = next_ss is not None
    tok_spec = pl.BlockSpec((1, e, tt), lambda i, t, w, f: (i, 0, t))
    row_spec = pl.BlockSpec((tt, d), lambda i, t, w, f: (i * nt + t, 0))
    in_specs = [tok_spec, tok_spec,
                pl.BlockSpec((1, rows, d), lambda i, t, w, f: (i, 0, 0), pipeline_mode=pl.Buffered(1)),
                row_spec,
                pl.BlockSpec((1, 1, d), lambda i, t, w, f: (i, 0, 0)),
                pl.BlockSpec((2, d), lambda i, t, w, f: (0, 0))]
    args = [slot, aff, ye, h1, gate, ln]
    out_specs = [row_spec]
    out_shape = [jax.ShapeDtypeStruct((m, d), F32)]
    if with_next:
        in_specs.append(pl.BlockSpec((1, 2, d), lambda i, t, w, f: (i, 0, 0)))
        args.append(next_ss)
        out_specs.append(row_spec)
        out_shape.append(jax.ShapeDtypeStruct((m, d), BF16))
    grid_spec = pltpu.PrefetchScalarGridSpec(
        num_scalar_prefetch=2,
        grid=(b, nt),
        in_specs=in_specs,
        out_specs=out_specs,
        scratch_shapes=[pltpu.VMEM((rows, tt), BF16),
                        pltpu.VMEM((e * MOE_WIN, d), BF16),
                        pltpu.VMEM((tt, d), F32)],
    )
    res = pl.pallas_call(
        functools.partial(_scatter_kernel, with_next=with_next),
        grid_spec=grid_spec,
        out_shape=out_shape,
        compiler_params=_params(("arbitrary", "arbitrary")),
        name="moe_combine_postnorm",
    )(win, fit, *args)
    return res if with_next else (res[0], None)


def _ec_moe(h1, u2, logits, w1, w3, w2, layer, gate, ln, next_ss, batch, n_tok):
    cap = EC_CAPACITY_FACTOR * n_tok // N_EXPERTS
    lg_t = jnp.swapaxes(logits[:, :N_EXPERTS].reshape(batch, n_tok, N_EXPERTS), 1, 2)
    slot, aff, cnt = _route_call(lg_t, cap)
    win, fit = _slot_windows(cnt, cap, n_tok // MOE_CHUNK)
    xe = _gather_call(win, fit, slot, u2, cap)
    ye = _expert_call(xe, w1, w3, w2, layer, batch)
    return _scatter_call(win, fit, slot, aff, ye, h1, gate, ln, next_ss)


def _mm_kernel(a_ref, w_ref, o_ref, wb_ref):
    @pl.when(pl.program_id(1) == 0)
    def _():
        wb_ref[...] = w_ref[0].astype(wb_ref.dtype)

    o_ref[...] = _dot(a_ref[...], wb_ref[...]).astype(o_ref.dtype)


def _mm_call(a, w, layer, out_dtype, tm=1024, tn=1024):
    m, k = a.shape
    n = w.shape[2]
    return pl.pallas_call(
        _mm_kernel,
        grid=(n // tn, m // tm),
        in_specs=[pl.BlockSpec((tm, k), lambda j, i: (i, 0)),
                  pl.BlockSpec((1, k, tn), lambda j, i: (layer, 0, j))],
        out_specs=pl.BlockSpec((tm, tn), lambda j, i: (i, j)),
        out_shape=jax.ShapeDtypeStruct((m, n), out_dtype),
        scratch_shapes=[pltpu.VMEM((k, tn), BF16)],
        compiler_params=_params(("arbitrary", "arbitrary")),
        name="matmul",
    )(a, w)


def _short_conv(p, w, b):
    x = p.astype(F32)
    n = x.shape[0]
    r = lax.broadcasted_iota(I32, x.shape, 0)
    prev = jnp.where(r == 0, 0.0, pltpu.roll(x, 1, 0))
    nxt = jnp.where(r == n - 1, 0.0, pltpu.roll(x, n - 1, 0))
    return b + w[0:1, :] * prev + w[1:2, :] * x + w[2:3, :] * nxt


def _sconv_kernel(p_ref, w_ref, b_ref, o_ref):
    o_ref[...] = _short_conv(p_ref[...], w_ref[...], b_ref[...]).astype(o_ref.dtype)


def _sconv_call(p, conv_w, conv_b, col0, ncols, out_dtype, batch, n_tok, tc=512):
    c0 = col0 // tc
    return pl.pallas_call(
        _sconv_kernel,
        grid=(batch, ncols // tc),
        in_specs=[pl.BlockSpec((n_tok, tc), lambda b, j: (b, c0 + j)),
                  pl.BlockSpec((HY_SHORT, tc), lambda b, j: (0, c0 + j)),
                  pl.BlockSpec((1, tc), lambda b, j: (0, c0 + j))],
        out_specs=pl.BlockSpec((n_tok, tc), lambda b, j: (b, j)),
        out_shape=jax.ShapeDtypeStruct((batch * n_tok, ncols), out_dtype),
        compiler_params=_params(("arbitrary", "arbitrary")),
        name="hyena_short_conv",
    )(p, conv_w, conv_b.reshape(1, -1))


def _dft_fwd_kernel(fc_ref, fs_ref, r1_ref, r2_ref, *rest, spectral):
    xr = _dot(fc_ref[...], r1_ref[...])
    xi = _dot(fs_ref[...], r2_ref[...])
    if spectral:
        h_ref, nyq_ref, o_ref = rest
        hr = h_ref[0, 0]
        hi = h_ref[0, 1]
        dc = lax.broadcasted_iota(I32, xr.shape, 0) == 0
        yr = xr * hr - jnp.where(dc, 0.0, xi * hi)
        yi = jnp.where(dc, xi * nyq_ref[...], xr * hi + xi * hr)
    else:
        (o_ref,) = rest
        yr, yi = xr, xi
    o_ref[0, 0] = yr.astype(o_ref.dtype)
    o_ref[0, 1] = yi.astype(o_ref.dtype)


def _resident_spec(shape):
    return pl.BlockSpec(shape, lambda *_: (0,) * len(shape), pipeline_mode=pl.Buffered(1))


def _dft_fwd_call(fc, fs, r1, r2, spec, out_dtype, batch, n_tok, tn=256):
    ncols = r1.shape[1]
    r_spec = pl.BlockSpec((n_tok, tn), lambda j, b: (b, j))
    in_specs = [_resident_spec(fc.shape), _resident_spec(fs.shape), r_spec, r_spec]
    args = [fc, fs, r1, r2]
    if spec is not None:
        h, nyq, hcol0 = spec
        c0 = hcol0 // tn
        in_specs += [pl.BlockSpec((1, 2, n_tok, tn), lambda j, b: (0, 0, 0, c0 + j)),
                     pl.BlockSpec((1, tn), lambda j, b: (0, c0 + j))]
        args += [h, nyq]
    return pl.pallas_call(
        functools.partial(_dft_fwd_kernel, spectral=spec is not None),
        grid=(ncols // tn, batch),
        in_specs=in_specs,
        out_specs=pl.BlockSpec((1, 2, n_tok, tn), lambda j, b: (b, 0, 0, j)),
        out_shape=jax.ShapeDtypeStruct((batch, 2, n_tok, ncols), out_dtype),
        compiler_params=_params(("arbitrary", "arbitrary")),
        name="dft_forward",
    )(*args)


def _dual_kernel(a1_ref, a2_ref, b1_ref, b2_ref, *rest, hyena, planes):
    b1 = b1_ref[0, 0] if planes else b1_ref[...]
    b2 = b2_ref[0, 0] if planes else b2_ref[...]
    y = _dot(a1_ref[...], b1) + _dot(a2_ref[...], b2)
    if hyena:
        p_ref, cw_ref, cb_ref, z_ref, skip_ref, o_ref = rest
        gate = _short_conv(p_ref[...], cw_ref[...], cb_ref[...])
        y = gate * (y + skip_ref[...] * z_ref[...].astype(F32))
    else:
        (o_ref,) = rest
    o_ref[...] = y.astype(o_ref.dtype)


def _dual_call(a1, a2, b1, b2, hy, batch, n_tok, ncols, tn=256):
    planes = b1.ndim == 4
    if planes:
        b_specs = [pl.BlockSpec((1, 1, n_tok, tn), lambda b, j: (b, 0, 0, j)),
                   pl.BlockSpec((1, 1, n_tok, tn), lambda b, j: (b, 1, 0, j))]
    else:
        b_specs = [pl.BlockSpec((n_tok, tn), lambda b, j: (b, j))] * 2
    in_specs = [_resident_spec(a1.shape), _resident_spec(a2.shape)] + b_specs
    args = [a1, a2, b1, b2]
    o_spec = pl.BlockSpec((n_tok, tn), lambda b, j: (b, j))
    if hy is not None:
        p, conv_w, conv_b, pcol0, z, skip = hy
        c0 = pcol0 // tn
        in_specs += [pl.BlockSpec((n_tok, tn), lambda b, j: (b, c0 + j)),
                     pl.BlockSpec((HY_SHORT, tn), lambda b, j: (0, c0 + j)),
                     pl.BlockSpec((1, tn), lambda b, j: (0, c0 + j)),
                     o_spec,
                     pl.BlockSpec((1, tn), lambda b, j: (0, j))]
        args += [p, conv_w, conv_b.reshape(1, -1), z, skip]
    return pl.pallas_call(
        functools.partial(_dual_kernel, hyena=hy is not None, planes=planes),
        grid=(batch, ncols // tn),
        in_specs=in_specs,
        out_specs=o_spec,
        out_shape=jax.ShapeDtypeStruct((batch * n_tok, ncols), BF16),
        compiler_params=_params(("arbitrary", "arbitrary")),
        name="seq_mix_matmul",
    )(*args)


def _fnet_kernel(cl_ref, msl_ref, x_ref, cw_ref, o_ref):
    x = x_ref[...]
    seq = jnp.concatenate([_dot(cl_ref[...], x).astype(BF16), _dot(msl_ref[...], x).astype(BF16)], axis=1)
    o_ref[...] = _dot(seq, cw_ref[...]).astype(o_ref.dtype)


def _fnet_call(cl, msl, p, cw, col0, batch, n_tok):
    gw = cw.shape[1]
    c0 = col0 // gw
    return pl.pallas_call(
        _fnet_kernel,
        grid=(batch, FN_GROUPS),
        in_specs=[_resident_spec(cl.shape), _resident_spec(msl.shape),
                  pl.BlockSpec((n_tok, gw), lambda b, g: (b, c0 + g)),
                  pl.BlockSpec(cw.shape, lambda b, g: (0, 0))],
        out_specs=pl.BlockSpec((n_tok, gw), lambda b, g: (b, g)),
        out_shape=jax.ShapeDtypeStruct((batch * n_tok, FN_GROUPS * gw), BF16),
        compiler_params=_params(("arbitrary", "arbitrary")),
        name="fnet_mix",
    )(cl, msl, p, cw)


def _cos_sin_matrix(n_rows, n_cols, period, split=64):
    r = np.arange(n_rows, dtype=np.int64)[:, None]
    c_hi = (np.arange(n_cols // split, dtype=np.int64) * split)[None, :]
    c_lo = np.arange(split, dtype=np.int64)[None, :]
    ang = lambda c: 2.0 * np.pi * ((r * c) % period).astype(np.float64) / period
    ca, sa = jnp.asarray(np.cos(ang(c_hi)), F32), jnp.asarray(np.sin(ang(c_hi)), F32)
    cb, sb = jnp.asarray(np.cos(ang(c_lo)), F32), jnp.asarray(np.sin(ang(c_lo)), F32)
    cos = ca[:, :, None] * cb[:, None, :] - sa[:, :, None] * sb[:, None, :]
    sin = sa[:, :, None] * cb[:, None, :] + ca[:, :, None] * sb[:, None, :]
    return cos.reshape(n_rows, n_cols), sin.reshape(n_rows, n_cols)


def _trig_kernel(ta_ref, tb_ref, ea_ref, eb_ref, *o_refs, mode, period, scale):
    sa_ca = _dot(ta_ref[...], ea_ref[...])
    sb_cb = _dot(tb_ref[...], eb_ref[...])
    nc = sa_ca.shape[1] // 2
    ca, sa, cb, sb = sa_ca[:, :nc], sa_ca[:, nc:], sb_cb[:, :nc], sb_cb[:, nc:]
    cos = ca * cb - sa * sb
    sin = sa * cb + ca * sb
    tm = cos.shape[0]
    row = pl.program_id(0) * tm + lax.broadcasted_iota(I32, cos.shape, 0)
    col = lax.broadcasted_iota(I32, cos.shape, 1)
    alt = lambda idx: (1 - 2 * jnp.bitwise_and(idx, 1)).astype(F32)
    if mode == "hyena":
        fc_ref, fs_ref, gc_ref, gs_ref = o_refs
        fc_ref[...] = cos.astype(fc_ref.dtype)
        fs_ref[...] = jnp.where(row == 0, alt(col), -sin).astype(fs_ref.dtype)
        gc_ref[...] = (cos * jnp.where(col == 0, 1.0 / period, 2.0 / period)).astype(gc_ref.dtype)
        gs_ref[...] = jnp.where(col == 0, alt(row) * (1.0 / period), sin * (-2.0 / period)).astype(gs_ref.dtype)
    else:
        c_ref, ms_ref = o_refs
        c_ref[...] = (cos * scale).astype(c_ref.dtype)
        ms_ref[...] = (sin * (-scale)).astype(ms_ref.dtype)


def _split3(x):
    bf = jnp.bfloat16
    x = np.asarray(x, np.float32)
    hi = x.astype(bf)
    r1 = x - hi.astype(np.float32)
    mid = r1.astype(bf)
    lo = (r1 - mid.astype(np.float32)).astype(bf)
    return [hi, mid, lo]


def _trig_call(n, period, mode, scale=1.0, split=64, tm=256):
    r = np.arange(n, dtype=np.int64)[:, None]
    c_hi = (np.arange(n // split, dtype=np.int64) * split)[None, :]
    c_lo = np.arange(split, dtype=np.int64)[None, :]
    ang = lambda c: 2.0 * np.pi * ((r * c) % period).astype(np.float64) / period
    def operands(a, col_group):
        terms = _split3(np.cos(a)) + _split3(np.sin(a))
        k = a.shape[1]
        spread = (col_group[None, :] == np.arange(k)[:, None]).astype(np.float32)
        zero = np.zeros_like(spread)
        e = np.concatenate([np.concatenate([spread, zero], 1)] * 3 + [np.concatenate([zero, spread], 1)] * 3, 0)
        return jnp.asarray(np.concatenate(terms, axis=1)), jnp.asarray(e, BF16)
    cols = np.arange(n)
    ta, ea = operands(ang(c_hi), cols // split)
    tb, eb = operands(ang(c_lo), cols % split)
    n_out = 4 if mode == "hyena" else 2
    o_spec = pl.BlockSpec((tm, n), lambda i: (i, 0))
    return pl.pallas_call(
        functools.partial(_trig_kernel, mode=mode, period=period, scale=scale),
        grid=(n // tm,),
        in_specs=[pl.BlockSpec((tm, ta.shape[1]), lambda i: (i, 0)),
                  pl.BlockSpec((tm, tb.shape[1]), lambda i: (i, 0)),
                  pl.BlockSpec(ea.shape, lambda i: (0, 0)),
                  pl.BlockSpec(eb.shape, lambda i: (0, 0))],
        out_specs=[o_spec] * n_out,
        out_shape=[jax.ShapeDtypeStruct((n, n), BF16)] * n_out,
        compiler_params=_params(("arbitrary",)),
        name="dft_tables",
    )(ta, tb, ea, eb)


def _hyena_dft_operands(n_tok):
    return _trig_call(n_tok, 2 * n_tok, "hyena")


def _fnet_operands(n_tok, gw):
    cl, msl = _trig_call(n_tok, n_tok, "fnet", scale=1.0 / math.sqrt(n_tok * gw))
    cw, sw = _cos_sin_matrix(gw, gw, gw)
    return cl, msl, jnp.concatenate([cw, sw], axis=0).astype(BF16)


def _filter_kernel(z_ref, w1_ref, b1_ref, f1_ref, w2_ref, b2_ref, f2_ref, w3f_ref, w3b_ref, dl_ref,
                   sum_ref, dif_ref, nyq_ref, hdn_ref):
    @pl.when(jnp.logical_and(pl.program_id(0) == 0, pl.program_id(1) == 0))
    def _():
        hdot = lambda a, b: jnp.dot(a, b, precision=lax.Precision.HIGHEST, preferred_element_type=F32)
        hdn = jnp.sin(f1_ref[...] * (hdot(z_ref[...], w1_ref[...]) + b1_ref[...]))
        hdn = jnp.sin(f2_ref[...] * (hdot(hdn, w2_ref[...]) + b2_ref[...]))
        hdn_ref[...] = hdn.astype(hdn_ref.dtype)

    n = hdn_ref.shape[0]
    row = lax.broadcasted_iota(I32, (n, dl_ref.shape[1]), 0)
    decay = jnp.exp(-(row.astype(F32) / (n - 1)) * dl_ref[...])
    hf = _dot(hdn_ref[...], w3f_ref[...].astype(BF16)) * decay
    hb = jnp.where(row == 0, 0.0, _dot(hdn_ref[...], w3b_ref[...].astype(BF16)) * decay)
    tot = hf + hb
    sum_ref[...] = tot.astype(sum_ref.dtype)
    dif_ref[...] = (hf - hb).astype(dif_ref.dtype)
    nyq_ref[...] = jnp.sum(jnp.where(jnp.bitwise_and(row, 1) == 0, tot, -tot), axis=0, keepdims=True)


def _filter_call(n_tok, fw1, fb1, ff1, fw2, fb2, ff2, fw3, tc=512):
    t01 = np.linspace(0.0, 1.0, n_tok, dtype=np.float32)
    w = (2.0 * math.pi * np.arange(n_tok, dtype=np.float32) / n_tok).astype(np.float32)
    bands = np.linspace(1e-4, HY_BANDS - 1, HY_BANDS, dtype=np.float32)
    z = np.concatenate([t01[:, None], np.cos(w[:, None] * bands), -np.sin(w[:, None] * bands)], -1)
    deltas = np.abs(np.linspace(HY_MIN_DECAY, HY_MAX_DECAY, HY_W, dtype=np.float32))[None, :]
    emb, hid = fw1.shape
    nc = HY_W // tc
    full = lambda shape: pl.BlockSpec(shape, lambda o, j: (0,) * len(shape))
    o_spec = pl.BlockSpec((n_tok, tc), lambda o, j: (0, o * nc + j))
    return pl.pallas_call(
        _filter_kernel,
        grid=(HY_ORDER, nc),
        in_specs=[full((n_tok, emb)), full((emb, hid)), full((1, hid)), full((1, hid)),
                  full((hid, hid)), full((1, hid)), full((1, hid)),
                  pl.BlockSpec((hid, tc), lambda o, j: (0, (2 * o) * nc + j)),
                  pl.BlockSpec((hid, tc), lambda o, j: (0, (2 * o + 1) * nc + j)),
                  pl.BlockSpec((1, tc), lambda o, j: (0, j))],
        out_specs=[o_spec, o_spec, pl.BlockSpec((1, tc), lambda o, j: (0, o * nc + j))],
        out_shape=[jax.ShapeDtypeStruct((n_tok, HY_ORDER * HY_W), BF16),
                   jax.ShapeDtypeStruct((n_tok, HY_ORDER * HY_W), BF16),
                   jax.ShapeDtypeStruct((1, HY_ORDER * HY_W), F32)],
        scratch_shapes=[pltpu.VMEM((n_tok, hid), BF16)],
        compiler_params=_params(("arbitrary", "arbitrary")),
        name="hyena_filters",
    )(jnp.asarray(z.astype(np.float32)), fw1, fb1.reshape(1, hid), ff1.reshape(1, hid),
      fw2, fb2.reshape(1, hid), ff2.reshape(1, hid), fw3, fw3, jnp.asarray(deltas))


def kernel(x, c, ctx, c_ctx, ada_w, ada_b, ln1_g, ln1_b, ln2_g, ln2_b, router, exp_w1, exp_w3, exp_w2,
           ab_w_in, ab_kv_norm, ab_w_ukv, ab_rpb, ab_w_out,
           cd_w_in, cd_conv_w, cd_conv_b, cd_filt_w1, cd_filt_b1, cd_filt_freq1, cd_filt_w2, cd_filt_b2,
           cd_filt_freq2, cd_filt_w3, cd_skip, cd_w_out):
    batch, n_tok, d = x.shape
    lc = ctx.shape[1]
    x2d = x.reshape(batch * n_tok, d)
    ctx2d = ctx.reshape(batch * lc, d)

    pad_rows = (-(batch + 1)) % 8
    cv = jnp.concatenate([c, c_ctx[None, :], jnp.zeros((pad_rows, d), F32)], axis=0)
    ada = _ada_call(cv, ada_w, ada_b)
    mods = [ada[i, :batch].reshape(batch, 6, d) for i in range(DEPTH)]
    ctx_ss = ada[0, batch, :2 * d].reshape(1, 2, d)

    w_in = _w_in_prep_call(jnp.swapaxes(ab_w_in, 1, 2), 0)
    p_lat = _mm_rope_call(x2d, mods[0][:, 0:2], n_tok, w_in, _rope_tables(n_tok), tm=1024)
    p_ctx = _mm_rope_call(ctx2d, ctx_ss, batch * lc, w_in, _identity_rope_tables(batch * lc), tm=batch * lc,
                          tile0=CTX_TILE0)
    w_ukv = _permute_w_ukv(ab_w_ukv[0])
    kv_lat = _kvup_call(p_lat, ab_kv_norm[0], w_ukv, tm=512)
    kv_ctx = _kvup_call(p_ctx, ab_kv_norm[0], w_ukv, tm=lc, tile0=CTX_TILE0)
    a_out = _mla_call(p_lat, kv_lat, p_ctx, kv_ctx, batch, n_tok, lc)
    var_map, na_er, na_ec = _na_tables(n_tok // GRID_W)
    b_out = _na_call(p_lat, p_ctx, _na_bias(ab_rpb[0], na_er, na_ec), jnp.asarray(var_map),
                     batch, n_tok, lc)
    ln1 = jnp.stack([ln1_g, ln1_b], axis=1)
    ln2 = jnp.stack([ln2_g, ln2_b], axis=1)
    h1, u2, logits = _outln_call(a_out, b_out, ab_w_out[0].astype(BF16), x2d,
                                 mods[0][:, 2:5], ln1[0], _router_operand(router[0]), n_tok)
    h, u = _ec_moe(h1, u2, logits, exp_w1, exp_w3, exp_w2, 0, mods[0][:, 5:6], ln2[0],
                   mods[1][:, 0:2], batch, n_tok)

    p1 = _mm_call(u, cd_w_in, 0, BF16)
    s0 = _sconv_call(p1, cd_conv_w[0], cd_conv_b[0], 0, HY_W, BF16, batch, n_tok)
    h_sum, h_dif, h_nyq = _filter_call(n_tok, cd_filt_w1[0], cd_filt_b1[0], cd_filt_freq1[0], cd_filt_w2[0],
                                       cd_filt_b2[0], cd_filt_freq2[0], cd_filt_w3[0])
    fc, fs, gc, gs = _hyena_dft_operands(n_tok)
    h_spec = _dft_fwd_call(fc, fs, h_sum, h_dif, None, F32, 1, n_tok)
    z = s0
    for o in range(HY_ORDER):
        y_spec = _dft_fwd_call(fc, fs, z, z, (h_spec, h_nyq, o * HY_W), BF16, batch, n_tok)
        z = _dual_call(gc, gs, y_spec, y_spec,
                       (p1, cd_conv_w[0], cd_conv_b[0], (o + 1) * HY_W, z, cd_skip[0][o:o + 1]),
                       batch, n_tok, HY_W)
    cl, msl, cw = _fnet_operands(n_tok, FN_GROUP_W)
    y_fn = _fnet_call(cl, msl, p1, cw, HY_IN_W, batch, n_tok)
    h1, u2, logits = _outln_call(z, y_fn, cd_w_out[0].astype(BF16), h,
                                 mods[1][:, 2:5], ln1[1], _router_operand(router[1]), n_tok)
    h, _ = _ec_moe(h1, u2, logits, exp_w1, exp_w3, exp_w2, 1, mods[1][:, 5:6], ln2[1],
                   None, batch, n_tok)
    return h.reshape(batch, n_tok, d)
```

```python
import functools
import math

import numpy as np
import jax
import jax.numpy as jnp
from jax import lax
from jax.experimental import pallas as pl
from jax.experimental.pallas import tpu as pltpu

F32 = jnp.float32
BF16 = jnp.bfloat16
I32 = jnp.int32

D_MODEL = 2048
BATCH = 4
SEQ = 2048
DEPTH = 2
CTX_LEN = 256
GRID_W = 64
DEEPNORM_ALPHA = (2.0 * DEPTH) ** 0.25
LN_EPS = 1e-6
NEG_INF = -1e30
LOG2E = math.log2(math.e)

MLA_HEADS = 8
MLA_NOPE = 128
MLA_ROPE = 64
MLA_QK = MLA_NOPE + MLA_ROPE
MLA_V = 128
MLA_KV_RANK = 512
ROPE_THETA = 10000.0

NA_HEADS = 8
NA_HEAD_DIM = 128
NA_KH = 8
NA_KW = 16

OFF_CKV = MLA_HEADS * MLA_QK
OFF_KPE = OFF_CKV + MLA_KV_RANK
OFF_QNA = OFF_KPE + MLA_ROPE
OFF_KNA = OFF_QNA + NA_HEADS * NA_HEAD_DIM
OFF_VNA = OFF_KNA + NA_HEADS * NA_HEAD_DIM

HY_W = 1024
HY_ORDER = 2
HY_IN_W = (HY_ORDER + 1) * HY_W
HY_SHORT = 3
HY_BANDS = 16
HY_DECAY_TARGET = 1e-2
HY_MIN_DECAY = math.log(HY_DECAY_TARGET) / 1.5
HY_MAX_DECAY = math.log(HY_DECAY_TARGET) / 0.3

FN_W = 1024
FN_GROUPS = 4
FN_GROUP_W = FN_W // FN_GROUPS

N_EXPERTS = 16
EC_CAPACITY_FACTOR = 2
EXPERT_FF = 1408

LANES = 128
MXU_DIM = 256
VMEM_LIMIT = 56 * 1024 * 1024

QTILE_W = 2 * LANES
T_KPE = MLA_HEADS
T_CKV = T_KPE + 2
T_QNA = T_CKV + MLA_KV_RANK // QTILE_W
T_KNA = T_QNA + NA_HEADS * NA_HEAD_DIM // QTILE_W
T_VNA = T_KNA + NA_HEADS * NA_HEAD_DIM // QTILE_W
N_ABTILES = T_VNA + NA_HEADS * NA_HEAD_DIM // QTILE_W
AB_PERM_W = N_ABTILES * QTILE_W
CTX_TILE0 = T_KPE
INPROJ_TILES_PER_STEP = 8

MLA_SUBTILES = 8
OUTLN_SUBTILES = 2
MOE_CHUNK = 256
MOE_WIN = 64
NA_G = 4
NA_WR = NA_KH + NA_G - 1
NA_GROUPS_PER_STEP = 8


def _params(sem, vmem=VMEM_LIMIT):
    return pltpu.CompilerParams(dimension_semantics=sem, vmem_limit_bytes=vmem)


def _dot(a, b):
    return jnp.dot(a, b, preferred_element_type=F32)


def _dot_nt(a, b):
    return lax.dot_general(a, b, (((1,), (1,)), ((), ())), preferred_element_type=F32)


def _dot_tn(a, b):
    return lax.dot_general(a, b, (((0,), (0,)), ((), ())), preferred_element_type=F32)


def _split_bf16(x):
    hi = x.astype(BF16)
    lo = (x - hi.astype(F32)).astype(BF16)
    return hi, lo


def _ada_kernel(c_ref, w_ref, b_ref, o_ref):
    c = c_ref[...]
    a = c / (1.0 + jnp.exp(-c))
    a_hi, a_lo = _split_bf16(a)
    w_hi, w_lo = _split_bf16(w_ref[0])
    rows = a.shape[0]
    r = _dot(jnp.concatenate([a_hi, a_lo], axis=0), w_hi)
    o_ref[0] = r[:rows] + r[rows:] + _dot(a_hi, w_lo) + b_ref[0]


def _ada_call(cv, ada_w, ada_b, tn=1024):
    depth, d, n = ada_w.shape
    rows = cv.shape[0]
    return pl.pallas_call(
        _ada_kernel,
        grid=(depth, n // tn),
        in_specs=[pl.BlockSpec((rows, d), lambda l, j: (0, 0)),
                  pl.BlockSpec((1, d, tn), lambda l, j: (l, 0, j)),
                  pl.BlockSpec((1, 1, tn), lambda l, j: (l, 0, j))],
        out_specs=pl.BlockSpec((1, rows, tn), lambda l, j: (l, 0, j)),
        out_shape=jax.ShapeDtypeStruct((depth, rows, n), F32),
        compiler_params=_params(("arbitrary", "arbitrary")),
        name="ada_params",
    )(cv, ada_w, ada_b.reshape(depth, 1, n))


def _mm_rope_kernel(x_ref, ss_ref, w_ref, ta_ref, tb_ref, tc_ref, o_ref, a_ref, *, tile0):
    @pl.when(pl.program_id(1) == 0)
    def _():
        a_ref[...] = (x_ref[...] * (1.0 + ss_ref[0, 1:2, :]) + ss_ref[0, 0:1, :]).astype(a_ref.dtype)

    a = a_ref[...]
    for q in range(INPROJ_TILES_PER_STEP):
        t = tile0 + INPROJ_TILES_PER_STEP * pl.program_id(1) + q
        c0 = q * QTILE_W
        acc = _dot_nt(a, w_ref[c0:c0 + QTILE_W, :])
        is_rope = t <= T_KPE
        f = jnp.where(t < T_KPE, LOG2E * MLA_QK ** -0.5,
                      jnp.where(jnp.logical_and(t >= T_QNA, t < T_KNA), LOG2E * NA_HEAD_DIM ** -0.5, 1.0)
                      ).astype(F32)
        hi = acc[:, LANES:]
        rot = (hi * jnp.where(is_rope, ta_ref[...], 1.0)
               + pltpu.roll(hi, LANES - MLA_ROPE // 2, 1) * jnp.where(is_rope, tb_ref[...], 0.0)
               + pltpu.roll(hi, MLA_ROPE // 2, 1) * jnp.where(is_rope, tc_ref[...], 0.0))
        o_ref[:, c0:c0 + LANES] = (acc[:, :LANES] * f).astype(o_ref.dtype)
        o_ref[:, c0 + LANES:c0 + QTILE_W] = (rot * f).astype(o_ref.dtype)


def _mm_rope_call(x, ss, rows_per_group, w, tabs, tm, tile0=0):
    m, k = x.shape
    tn = INPROJ_TILES_PER_STEP * QTILE_W
    n = w.shape[0] - tile0 * QTILE_W
    j0 = tile0 // INPROJ_TILES_PER_STEP
    ntab = tabs[0].shape[0] // tm
    tpg = rows_per_group // tm
    tab_spec = pl.BlockSpec((tm, LANES), lambda i, j: (i % ntab, 0))
    return pl.pallas_call(
        functools.partial(_mm_rope_kernel, tile0=tile0),
        grid=(m // tm, n // tn),
        in_specs=[pl.BlockSpec((tm, k), lambda i, j: (i, 0)),
                  pl.BlockSpec((1, 2, k), lambda i, j: (i // tpg, 0, 0)),
                  pl.BlockSpec((tn, k), lambda i, j: (j0 + j, 0)),
                  tab_spec, tab_spec, tab_spec],
        out_specs=pl.BlockSpec((tm, tn), lambda i, j: (i, j)),
        out_shape=jax.ShapeDtypeStruct((m, n), BF16),
        scratch_shapes=[pltpu.VMEM((tm, k), BF16)],
        compiler_params=_params(("arbitrary", "arbitrary")),
        name="attn_in_proj",
    )(x, ss, w, *tabs)


def _rope_tables(n_tok):
    t = np.arange(n_tok)
    row = (t // GRID_W).astype(np.float32)
    col = (t % GRID_W).astype(np.float32)
    n_freq = MLA_ROPE // 4
    inv = (ROPE_THETA ** (-np.arange(n_freq, dtype=np.float32) / n_freq)).astype(np.float32)
    ang = np.concatenate([row[:, None] * inv, col[:, None] * inv], axis=1)
    cos, sin = np.cos(ang).astype(np.float32), np.sin(ang).astype(np.float32)
    half = MLA_ROPE // 2
    ta = np.zeros((n_tok, LANES), np.float32)
    tb = np.zeros((n_tok, LANES), np.float32)
    tc = np.zeros((n_tok, LANES), np.float32)
    ta[:, :half] = cos
    ta[:, half:2 * half] = cos
    tb[:, :half] = -sin
    tc[:, half:2 * half] = sin
    return jnp.asarray(ta), jnp.asarray(tb), jnp.asarray(tc)


def _identity_rope_tables(n_tok):
    ta = np.zeros((n_tok, LANES), np.float32)
    ta[:, :MLA_ROPE] = 1.0
    z = np.zeros((n_tok, LANES), np.float32)
    return jnp.asarray(ta), jnp.asarray(z), jnp.asarray(z)


def _w_in_prep_kernel(w_ref, o_ref):
    x = w_ref[0]
    zeros = lambda n: jnp.zeros((n, x.shape[1]), x.dtype)
    rows = []
    for h in range(MLA_HEADS):
        rows += [x[h * MLA_QK:(h + 1) * MLA_QK, :], zeros(QTILE_W - MLA_QK)]
    rows += [zeros(LANES), x[OFF_KPE:OFF_QNA, :], zeros(LANES - MLA_ROPE), zeros(QTILE_W),
             x[OFF_CKV:OFF_KPE, :], x[OFF_QNA:, :]]
    o_ref[...] = jnp.concatenate(rows, axis=0).astype(o_ref.dtype)


def _w_in_prep_call(w_in_t, layer, tk=256):
    _, n, d = w_in_t.shape
    return pl.pallas_call(
        _w_in_prep_kernel,
        grid=(d // tk,),
        in_specs=[pl.BlockSpec((1, n, tk), lambda i: (layer, 0, i))],
        out_specs=pl.BlockSpec((AB_PERM_W, tk), lambda i: (0, i)),
        out_shape=jax.ShapeDtypeStruct((AB_PERM_W, d), BF16),
        compiler_params=_params(("arbitrary",)),
        name="attn_w_in_prep",
    )(w_in_t)


def _kvup_kernel(x_ref, g_ref, w_ref, o_ref):
    x = x_ref[...].astype(F32)
    y = x * lax.rsqrt(jnp.mean(x * x, axis=-1, keepdims=True) + LN_EPS) * g_ref[...]
    o_ref[...] = _dot(y.astype(BF16), w_ref[...]).astype(o_ref.dtype)


def _kvup_call(p, g, w, tm, tile0=0):
    m = p.shape[0]
    r, n = w.shape
    return pl.pallas_call(
        _kvup_kernel,
        grid=(m // tm,),
        in_specs=[pl.BlockSpec((tm, r), lambda i: (i, (T_CKV - tile0) * QTILE_W // r)),
                  pl.BlockSpec((1, r), lambda i: (0, 0)),
                  pl.BlockSpec((r, n), lambda i: (0, 0))],
        out_specs=pl.BlockSpec((tm, n), lambda i: (i, 0)),
        out_shape=jax.ShapeDtypeStruct((m, n), BF16),
        compiler_params=_params(("arbitrary",)),
        name="kv_up",
    )(p, g.reshape(1, r), w)


def _permute_w_ukv(w):
    r = w.shape[0]
    w3 = w.reshape(r, MLA_HEADS, MLA_NOPE + MLA_V)
    return jnp.concatenate([w3[:, :, :MLA_NOPE].reshape(r, -1), w3[:, :, MLA_NOPE:].reshape(r, -1)],
                           axis=1).astype(BF16)


def _mla_kernel(q_ref, kn_ref, kp_ref, v_ref, kcn_ref, kcp_ref, vc_ref, o_ref, kcat, vcat):
    s_len = kn_ref.shape[0]

    @pl.when(pl.program_id(2) == 0)
    def _():
        kcat[:s_len, :LANES] = kn_ref[...]
        kcat[:s_len, LANES:] = kp_ref[...]
        kcat[s_len:, :LANES] = kcn_ref[...]
        kcat[s_len:, LANES:] = kcp_ref[...]
        vcat[:s_len, :] = v_ref[...]
        vcat[s_len:, :] = vc_ref[...]

    sub = q_ref.shape[0] // MLA_SUBTILES
    for r0 in range(0, q_ref.shape[0], sub):
        s = _dot_nt(q_ref[r0:r0 + sub, :], kcat[...])
        m = jnp.max(s, axis=-1, keepdims=True)
        p = jnp.exp2(s - m)
        l = jnp.sum(p, axis=-1, keepdims=True)
        o = _dot(p.astype(BF16), vcat[...])
        o_ref[r0:r0 + sub, :] = (o / l).astype(o_ref.dtype)


def _mla_call(p_lat, kv_lat, p_ctx, kv_ctx, batch, s_len, lc, tq=2048):
    nq = s_len // tq
    kpe_blk = T_KPE * 2 + 1
    return pl.pallas_call(
        _mla_kernel,
        grid=(batch, MLA_HEADS, nq),
        in_specs=[pl.BlockSpec((tq, QTILE_W), lambda b, h, i: (b * nq + i, h)),
                  pl.BlockSpec((s_len, LANES), lambda b, h, i: (b, h)),
                  pl.BlockSpec((s_len, LANES), lambda b, h, i: (b, kpe_blk)),
                  pl.BlockSpec((s_len, LANES), lambda b, h, i: (b, MLA_HEADS + h)),
                  pl.BlockSpec((lc, LANES), lambda b, h, i: (b, h)),
                  pl.BlockSpec((lc, LANES), lambda b, h, i: (b, kpe_blk - 2 * CTX_TILE0)),
                  pl.BlockSpec((lc, LANES), lambda b, h, i: (b, MLA_HEADS + h))],
        out_specs=pl.BlockSpec((tq, LANES), lambda b, h, i: (b * nq + i, h)),
        out_shape=jax.ShapeDtypeStruct((batch * s_len, MLA_HEADS * MLA_V), BF16),
        scratch_shapes=[pltpu.VMEM((s_len + lc, QTILE_W), BF16),
                        pltpu.VMEM((s_len + lc, LANES), BF16)],
        compiler_params=_params(("arbitrary", "arbitrary", "arbitrary")),
        name="mla_attention",
    )(p_lat, kv_lat, p_lat, kv_lat, kv_ctx, p_ctx, kv_ctx)


def _na_kernel(var_ref, q_ref, k_ref, v_ref, kc_ref, vc_ref, *rest, rows):
    del var_ref
    bias_refs, o_ref = rest[:-1], rest[-1]
    win = NA_WR * GRID_W
    tq = NA_G * GRID_W
    for i, bias_ref in enumerate(bias_refs):
        g = len(bias_refs) * pl.program_id(2) + i
        start_row = jnp.clip(NA_G * g - NA_KH // 2, 0, rows - NA_WR)
        start = pl.multiple_of(start_row * GRID_W, GRID_W)
        q = q_ref[i * tq:(i + 1) * tq, :]
        sw = _dot_nt(q, k_ref[pl.ds(start, win), :]) + bias_ref[0, 0]
        sc = _dot_nt(q, kc_ref[...])
        m = jnp.maximum(jnp.max(sw, axis=-1, keepdims=True), jnp.max(sc, axis=-1, keepdims=True))
        pw = jnp.exp2(sw - m)
        pc = jnp.exp2(sc - m)
        l = jnp.sum(pw, axis=-1, keepdims=True) + jnp.sum(pc, axis=-1, keepdims=True)
        o = _dot(pw.astype(BF16), v_ref[pl.ds(start, win), :]) + _dot(pc.astype(BF16), vc_ref[...])
        o_ref[i * tq:(i + 1) * tq, :] = (o / l).astype(o_ref.dtype)


def _na_tables(rows):
    kh = min(NA_KH, rows)
    ng = rows // NA_G
    qr = np.arange(NA_G)[:, None]
    kr = np.arange(NA_WR)[None, :]
    sel_rows = []
    for g in range(ng):
        start_row = int(np.clip(NA_G * g - NA_KH // 2, 0, rows - NA_WR))
        r = NA_G * g + qr
        krow = start_row + kr
        rs = np.clip(r - kh // 2, 0, rows - kh)
        ok = (krow >= rs) & (krow < rs + kh)
        dr = krow - r + (NA_KH - 1)
        sel_rows.append(np.stack([ok & (dr == d) for d in range(2 * NA_KH - 1)]))
    sel_rows = np.stack(sel_rows)
    uniq, inverse = np.unique(sel_rows.reshape(ng, -1), axis=0, return_inverse=True)
    er = uniq.reshape((-1,) + sel_rows.shape[1:]).astype(np.float32)
    qc = np.arange(GRID_W)[:, None]
    kc = np.arange(GRID_W)[None, :]
    cs = np.clip(qc - NA_KW // 2, 0, GRID_W - NA_KW)
    ok_c = (kc >= cs) & (kc < cs + NA_KW)
    dc = np.clip(kc - qc, -(NA_KW - 1), NA_KW - 1) + (NA_KW - 1)
    ec = np.stack([ok_c & (dc == e) for e in range(2 * NA_KW - 1)]).astype(np.float32)
    return inverse.reshape(-1).astype(np.int32), er, ec


def _na_bias(rpb, er, ec):
    hp = lax.Precision.HIGHEST
    t = jnp.einsum('hde,eqk->hdqk', rpb.astype(F32), jnp.asarray(ec), precision=hp)
    t = jnp.where(jnp.asarray(ec.sum(axis=0) > 0.5), t * LOG2E, NEG_INF)
    outside = jnp.full(t[:, 0].shape, NEG_INF, F32)
    d_of = np.where(er.sum(axis=1) > 0.5, er.argmax(axis=1), -1)
    block = lambda d: outside if d < 0 else t[:, d]
    return jnp.stack([
        jnp.concatenate([jnp.concatenate([block(int(d)) for d in row], axis=-1) for row in var], axis=-2)
        for var in d_of])


def _na_call(p_lat, p_ctx, bias, var_map, batch, s_len, lc):
    rows = s_len // GRID_W
    gps = NA_GROUPS_PER_STEP
    ns = rows // (NA_G * gps)
    tq = NA_G * GRID_W
    win = NA_WR * GRID_W
    qb, kb, vb = T_QNA * 2, T_KNA * 2, T_VNA * 2
    bias_spec = lambda i: pl.BlockSpec((1, 1, tq, win), lambda b, h, s, vm: (vm[gps * s + i], h, 0, 0))
    grid_spec = pltpu.PrefetchScalarGridSpec(
        num_scalar_prefetch=1,
        grid=(batch, NA_HEADS, ns),
        in_specs=[pl.BlockSpec((gps * tq, LANES), lambda b, h, s, vm: (b * ns + s, qb + h)),
                  pl.BlockSpec((s_len, LANES), lambda b, h, s, vm: (b, kb + h)),
                  pl.BlockSpec((s_len, LANES), lambda b, h, s, vm: (b, vb + h)),
                  pl.BlockSpec((lc, LANES), lambda b, h, s, vm: (b, kb + h - 2 * CTX_TILE0)),
                  pl.BlockSpec((lc, LANES), lambda b, h, s, vm: (b, vb + h - 2 * CTX_TILE0))]
                 + [bias_spec(i) for i in range(gps)],
        out_specs=pl.BlockSpec((gps * tq, LANES), lambda b, h, s, vm: (b * ns + s, h)),
    )
    return pl.pallas_call(
        functools.partial(_na_kernel, rows=rows),
        grid_spec=grid_spec,
        out_shape=jax.ShapeDtypeStruct((batch * s_len, NA_HEADS * NA_HEAD_DIM), BF16),
        compiler_params=_params(("arbitrary", "arbitrary", "arbitrary")),
        name="na_attention",
    )(var_map, p_lat, p_lat, p_lat, p_ctx, p_ctx, *([bias] * gps))


def _layer_norm(x, g, b):
    mu = jnp.mean(x, axis=-1, keepdims=True)
    xc = x - mu
    var = jnp.mean(xc * xc, axis=-1, keepdims=True)
    return xc * lax.rsqrt(var + LN_EPS) * g + b


def _outln_kernel(a1_ref, a2_ref, w_ref, h_ref, mod_ref, ln_ref, r_ref, h1_ref, u2_ref, lg_ref, wb_ref):
    half = a1_ref.shape[1]

    @pl.when(pl.program_id(0) == 0)
    def _():
        rows = wb_ref.shape[0] // 4
        for r0 in range(0, wb_ref.shape[0], rows):
            wb_ref[r0:r0 + rows, :] = w_ref[0, r0:r0 + rows, :].astype(wb_ref.dtype)

    sub = a1_ref.shape[0] // OUTLN_SUBTILES
    for r0 in range(0, a1_ref.shape[0], sub):
        rs = slice(r0, r0 + sub)
        y = _dot(a1_ref[rs, :], wb_ref[:half, :]) + _dot(a2_ref[rs, :], wb_ref[half:, :])
        x = DEEPNORM_ALPHA * h_ref[rs, :] + mod_ref[0, 0:1, :] * y
        hn = _layer_norm(x, ln_ref[0:1, :], ln_ref[1:2, :])
        h1_ref[rs, :] = hn
        u = hn * (1.0 + mod_ref[0, 2:3, :]) + mod_ref[0, 1:2, :]
        u_bf = u.astype(BF16)
        u2_ref[rs, :] = u_bf
        lg_ref[rs, :] = _dot(u_bf, r_ref[...])


def _outln_call(a1, a2, w, layer, h, mod, ln, router2, rows_per_group, tm=512):
    m, d = h.shape
    half = a1.shape[1]
    tpg = rows_per_group // tm
    return pl.pallas_call(
        _outln_kernel,
        grid=(m // tm,),
        in_specs=[pl.BlockSpec((tm, half), lambda i: (i, 0)),
                  pl.BlockSpec((tm, half), lambda i: (i, 0)),
                  pl.BlockSpec((1, 2 * half, d), lambda i: (layer, 0, 0), pipeline_mode=pl.Buffered(1)),
                  pl.BlockSpec((tm, d), lambda i: (i, 0)),
                  pl.BlockSpec((1, 3, d), lambda i: (i // tpg, 0, 0)),
                  pl.BlockSpec((2, d), lambda i: (0, 0)),
                  _resident_spec((d, LANES))],
        out_specs=[pl.BlockSpec((tm, d), lambda i: (i, 0)),
                   pl.BlockSpec((tm, d), lambda i: (i, 0)),
                   pl.BlockSpec((tm, LANES), lambda i: (i, 0))],
        out_shape=[jax.ShapeDtypeStruct((m, d), F32),
                   jax.ShapeDtypeStruct((m, d), BF16),
                   jax.ShapeDtypeStruct((m, LANES), F32)],
        scratch_shapes=[pltpu.VMEM((2 * half, d), BF16)],
        compiler_params=_params(("arbitrary",)),
        name="out_proj_postnorm",
    )(a1, a2, w, h, mod, ln, router2)


def _router_operand(router):
    d, e = router.shape
    return jnp.pad(router, ((0, 0), (0, LANES - e))).astype(BF16)


def _route_kernel(lg_ref, slot_ref, aff_ref, cnt_ref, tri_ref, *, cap):
    n_tok = lg_ref.shape[2]

    @pl.when(pl.program_id(0) == 0)
    def _():
        chunk = 256
        for r0 in range(0, n_tok, chunk):
            r = r0 + lax.broadcasted_iota(I32, (chunk, n_tok), 0)
            c = lax.broadcasted_iota(I32, (chunk, n_tok), 1)
            tri_ref[r0:r0 + chunk, :] = jnp.where(r < c, 1.0, 0.0).astype(BF16)

    lg = lg_ref[0]
    ex = jnp.exp(lg - jnp.max(lg, axis=0, keepdims=True))
    aff = ex / jnp.sum(ex, axis=0, keepdims=True)
    bits = lax.bitcast_convert_type(aff, I32)
    n_e = lg.shape[0]
    count = lambda mask: jnp.sum(jnp.where(mask, 1.0, 0.0), axis=1, keepdims=True)

    def body(_, lohi):
        lo, hi = lohi
        mid = lo + jnp.right_shift(hi - lo, 1)
        ge = count(bits >= mid) >= cap
        return jnp.where(ge, mid, lo), jnp.where(ge, hi, mid)

    lo0 = jnp.zeros((n_e, 1), I32)
    hi0 = jnp.full((n_e, 1), 0x7F800000, I32)
    thr, _ = lax.fori_loop(0, 31, body, (lo0, hi0))
    gt = bits > thr
    eq = bits == thr
    need = cap - count(gt)
    pre_eq = _dot(jnp.where(eq, 1.0, 0.0).astype(BF16), tri_ref[...])
    sel = jnp.logical_or(gt, jnp.logical_and(eq, pre_eq < need))
    sel_bf = jnp.where(sel, 1.0, 0.0).astype(BF16)
    slot = _dot(sel_bf, tri_ref[...])
    slot_ref[0] = jnp.where(sel, slot.astype(I32), -1)
    aff_ref[0] = aff
    r = lax.broadcasted_iota(I32, (n_tok, LANES), 0)
    c = lax.broadcasted_iota(I32, (n_tok, LANES), 1)
    before = jnp.where(r < c * MOE_CHUNK, 1.0, 0.0).astype(BF16)
    cnt_ref[0] = _dot(sel_bf, before).astype(I32)


def _route_call(lg_t, cap):
    b, e, n_tok = lg_t.shape
    spec = pl.BlockSpec((1, e, n_tok), lambda i: (i, 0, 0))
    return pl.pallas_call(
        functools.partial(_route_kernel, cap=cap),
        grid=(b,),
        in_specs=[spec],
        out_specs=[spec, spec, pl.BlockSpec((1, e, LANES), lambda i: (i, 0, 0))],
        out_shape=[jax.ShapeDtypeStruct((b, e, n_tok), I32), jax.ShapeDtypeStruct((b, e, n_tok), F32),
                   jax.ShapeDtypeStruct((b, e, LANES), I32)],
        scratch_shapes=[pltpu.VMEM((n_tok, n_tok), BF16)],
        compiler_params=_params(("arbitrary",)),
        name="ec_route",
    )(lg_t)


def _slot_windows(cnt, cap, n_chunk):
    start = cnt[:, :, :n_chunk]
    end = cnt[:, :, 1:n_chunk + 1]
    a = jnp.minimum((start // 16) * 16, cap - MOE_WIN)
    fits = jnp.all(end <= a + MOE_WIN, axis=1)
    return jnp.swapaxes(a, 1, 2).reshape(-1).astype(I32), fits.reshape(-1).astype(I32)


def _gather_kernel(win_ref, fit_ref, slot_ref, u_ref, o_ref, *, group):
    b, kc = pl.program_id(0), pl.program_id(1)
    n_e, cap = o_ref.shape[0], o_ref.shape[1]
    chunk = u_ref.shape[0]
    step = b * pl.num_programs(1) + kc

    @pl.when(kc == 0)
    def _():
        o_ref[...] = jnp.zeros(o_ref.shape, o_ref.dtype)

    def place(rows, starts, group):
        row = lax.broadcasted_iota(I32, (rows, chunk), 0)
        for g0 in range(0, n_e, group):
            onehot = jnp.concatenate(
                [jnp.where(row + starts[e] == slot_ref[0, e:e + 1, :], 1.0, 0.0).astype(BF16)
                 for e in range(g0, g0 + group)], axis=0)
            got = _dot(onehot, u_ref[...]).astype(o_ref.dtype)
            for i, e in enumerate(range(g0, g0 + group)):
                o_ref[e, pl.ds(starts[e], rows), :] += got[i * rows:(i + 1) * rows]

    @pl.when(fit_ref[step] != 0)
    def _():
        place(MOE_WIN, [pl.multiple_of(win_ref[step * n_e + e], 16) for e in range(n_e)], group)

    @pl.when(fit_ref[step] == 0)
    def _():
        place(cap, [0] * n_e, max(1, group * MOE_WIN // cap))


def _gather_call(win, fit, slot, u2, cap, group=512 // MOE_WIN):
    b, e, n_tok = slot.shape
    d = u2.shape[1]
    nc = n_tok // MOE_CHUNK
    grid_spec = pltpu.PrefetchScalarGridSpec(
        num_scalar_prefetch=2,
        grid=(b, nc),
        in_specs=[pl.BlockSpec((1, e, MOE_CHUNK), lambda i, k, w, f: (i, 0, k)),
                  pl.BlockSpec((MOE_CHUNK, d), lambda i, k, w, f: (i * nc + k, 0))],
        out_specs=pl.BlockSpec((e, cap, d), lambda i, k, w, f: (0, i, 0)),
    )
    return pl.pallas_call(
        functools.partial(_gather_kernel, group=group),
        grid_spec=grid_spec,
        out_shape=jax.ShapeDtypeStruct((e, b * cap, d), BF16),
        compiler_params=_params(("arbitrary", "arbitrary")),
        name="moe_gather",
    )(win, fit, slot, u2)


def _expert_hidden_kernel(x_ref, w1_ref, w3_ref, o_ref, *, ff):
    tf = w1_ref.shape[3]
    last = pl.num_programs(1) - 1

    def block(width):
        w = jnp.concatenate([w1_ref[0, 0, :, :width].astype(BF16), w3_ref[0, 0, :, :width].astype(BF16)], axis=1)
        h = _dot(x_ref[0], w)
        h1 = h[:, :width]
        o_ref[0, :, :width] = ((h1 / (1.0 + jnp.exp(-h1))) * h[:, width:]).astype(o_ref.dtype)

    tail = ff - (pl.cdiv(ff, tf) - 1) * tf
    if tail == tf:
        block(tf)
    else:
        pl.when(pl.program_id(1) < last)(lambda: block(tf))
        pl.when(pl.program_id(1) == last)(lambda: block(tail))


def _expert_out_kernel(hd_ref, w2_ref, o_ref):
    y = _dot(hd_ref[0], w2_ref[0, 0].astype(BF16)).astype(o_ref.dtype)
    cap = o_ref.shape[1]
    for b in range(o_ref.shape[0]):
        o_ref[b] = y[b * cap:(b + 1) * cap]


def _expert_call(xe, w1, w3, w2, layer, batch, tf=512, tn=1024):
    e, m, d = xe.shape
    cap = m // batch
    ff = w1.shape[3]
    hidden = pl.pallas_call(
        functools.partial(_expert_hidden_kernel, ff=ff),
        grid=(e, pl.cdiv(ff, tf)),
        in_specs=[pl.BlockSpec((1, m, d), lambda i, f: (i, 0, 0)),
                  pl.BlockSpec((1, 1, d, tf), lambda i, f: (layer, i, 0, f)),
                  pl.BlockSpec((1, 1, d, tf), lambda i, f: (layer, i, 0, f))],
        out_specs=pl.BlockSpec((1, m, tf), lambda i, f: (i, 0, f)),
        out_shape=jax.ShapeDtypeStruct((e, m, ff), BF16),
        compiler_params=_params(("arbitrary", "arbitrary")),
        name="moe_expert_hidden",
    )(xe, w1, w3)
    return pl.pallas_call(
        _expert_out_kernel,
        grid=(e, d // tn),
        in_specs=[pl.BlockSpec((1, m, ff), lambda i, j: (i, 0, 0)),
                  pl.BlockSpec((1, 1, ff, tn), lambda i, j: (layer, i, 0, j))],
        out_specs=pl.BlockSpec((batch, cap, tn), lambda i, j: (0, i, j)),
        out_shape=jax.ShapeDtypeStruct((batch, e * cap, d), BF16),
        compiler_params=_params(("arbitrary", "arbitrary")),
        name="moe_expert_out",
    )(hidden, w2)


def _scatter_kernel(win_ref, fit_ref, slot_ref, aff_ref, ye_ref, h_ref, mod_ref, ln_ref, *rest, with_next):
    if with_next:
        nmod_ref, h2_ref, un_ref, pt_ref, yew_ref, moe_ref = rest
    else:
        h2_ref, pt_ref, yew_ref, moe_ref = rest
    n_e = slot_ref.shape[1]
    cap = ye_ref.shape[1] // n_e
    tt = h_ref.shape[0]
    step = pl.program_id(0) * pl.num_programs(1) + pl.program_id(1)

    def gates(rows, e, start):
        row = lax.broadcasted_iota(I32, (rows, tt), 0)
        return jnp.where(row + start == slot_ref[0, e:e + 1, :], aff_ref[0, e:e + 1, :], 0.0).astype(BF16)

    @pl.when(fit_ref[step] != 0)
    def _():
        for e in range(n_e):
            start = pl.multiple_of(win_ref[step * n_e + e], 16)
            pt_ref[e * MOE_WIN:(e + 1) * MOE_WIN, :] = gates(MOE_WIN, e, start)
            yew_ref[e * MOE_WIN:(e + 1) * MOE_WIN, :] = ye_ref[0, pl.ds(e * cap + start, MOE_WIN), :]
        moe_ref[...] = _dot_tn(pt_ref[:n_e * MOE_WIN, :], yew_ref[...])

    @pl.when(fit_ref[step] == 0)
    def _():
        for e in range(n_e):
            pt_ref[e * cap:(e + 1) * cap, :] = gates(cap, e, 0)
        moe_ref[...] = _dot_tn(pt_ref[...], ye_ref[0])

    x = DEEPNORM_ALPHA * h_ref[...] + mod_ref[0] * moe_ref[...]
    hn = _layer_norm(x, ln_ref[0:1, :], ln_ref[1:2, :])
    h2_ref[...] = hn
    if with_next:
        un_ref[...] = (hn * (1.0 + nmod_ref[0, 1:2, :]) + nmod_ref[0, 0:1, :]).astype(un_ref.dtype)


def _scatter_call(win, fit, slot, aff, ye, h1, gate, ln, next_ss):
    b, e, n_tok = slot.shape
    m, d = h1.shape
    tt = MOE_CHUNK
    nt = n_tok // tt
    rows = ye.shape[1]
    with_next = next_ss is not None
    tok_spec = pl.BlockSpec((1, e, tt), lambda i, t, w, f: (i, 0, t))
    row_spec = pl.BlockSpec((tt, d), lambda i, t, w, f: (i * nt + t, 0))
    in_specs = [tok_spec, tok_spec,
                pl.BlockSpec((1, rows, d), lambda i, t, w, f: (i, 0, 0), pipeline_mode=pl.Buffered(1)),
                row_spec,
                pl.BlockSpec((1, 1, d), lambda i, t, w, f: (i, 0, 0)),
                pl.BlockSpec((2, d), lambda i, t, w, f: (0, 0))]
    args = [slot, aff, ye, h1, gate, ln]
    out_specs = [row_spec]
    out_shape = [jax.ShapeDtypeStruct((m, d), F32)]
    if with_next:
        in_specs.append(pl.BlockSpec((1, 2, d), lambda i, t, w, f: (i, 0, 0)))
        args.append(next_ss)
        out_specs.append(row_spec)
        out_shape.append(jax.ShapeDtypeStruct((m, d), BF16))
    grid_spec = pltpu.PrefetchScalarGridSpec(
        num_scalar_prefetch=2,
        grid=(b, nt),
        in_specs=in_specs,
        out_specs=out_specs,
        scratch_shapes=[pltpu.VMEM((rows, tt), BF16),
                        pltpu.VMEM((e * MOE_WIN, d), BF16),
                        pltpu.VMEM((tt, d), F32)],
    )
    res = pl.pallas_call(
        functools.partial(_scatter_kernel, with_next=with_next),
        grid_spec=grid_spec,
        out_shape=out_shape,
        compiler_params=_params(("arbitrary", "arbitrary")),
        name="moe_combine_postnorm",
    )(win, fit, *args)
    return res if with_next else (res[0], None)


def _ec_moe(h1, u2, logits, w1, w3, w2, layer, gate, ln, next_ss, batch, n_tok):
    cap = EC_CAPACITY_FACTOR * n_tok // N_EXPERTS
    lg_t = jnp.swapaxes(logits[:, :N_EXPERTS].reshape(batch, n_tok, N_EXPERTS), 1, 2)
    slot, aff, cnt = _route_call(lg_t, cap)
    win, fit = _slot_windows(cnt, cap, n_tok // MOE_CHUNK)
    xe = _gather_call(win, fit, slot, u2, cap)
    ye = _expert_call(xe, w1, w3, w2, layer, batch)
    return _scatter_call(win, fit, slot, aff, ye, h1, gate, ln, next_ss)


def _mm_kernel(a_ref, w_ref, o_ref, wb_ref):
    @pl.when(pl.program_id(1) == 0)
    def _():
        wb_ref[...] = w_ref[0].astype(wb_ref.dtype)

    o_ref[...] = _dot(a_ref[...], wb_ref[...]).astype(o_ref.dtype)


def _mm_call(a, w, layer, out_dtype, tm=1024, tn=1024):
    m, k = a.shape
    n = w.shape[2]
    return pl.pallas_call(
        _mm_kernel,
        grid=(n // tn, m // tm),
        in_specs=[pl.BlockSpec((tm, k), lambda j, i: (i, 0)),
                  pl.BlockSpec((1, k, tn), lambda j, i: (layer, 0, j))],
        out_specs=pl.BlockSpec((tm, tn), lambda j, i: (i, j)),
        out_shape=jax.ShapeDtypeStruct((m, n), out_dtype),
        scratch_shapes=[pltpu.VMEM((k, tn), BF16)],
        compiler_params=_params(("arbitrary", "arbitrary")),
        name="matmul",
    )(a, w)


def _short_conv(p, w, b):
    x = p.astype(F32)
    n = x.shape[0]
    r = lax.broadcasted_iota(I32, x.shape, 0)
    prev = jnp.where(r == 0, 0.0, pltpu.roll(x, 1, 0))
    nxt = jnp.where(r == n - 1, 0.0, pltpu.roll(x, n - 1, 0))
    return b + w[0:1, :] * prev + w[1:2, :] * x + w[2:3, :] * nxt


def _sconv_kernel(p_ref, w_ref, b_ref, o_ref):
    o_ref[...] = _short_conv(p_ref[...], w_ref[...], b_ref[...]).astype(o_ref.dtype)


def _sconv_call(p, conv_w, conv_b, col0, ncols, out_dtype, batch, n_tok, tc=512):
    c0 = col0 // tc
    return pl.pallas_call(
        _sconv_kernel,
        grid=(batch, ncols // tc),
        in_specs=[pl.BlockSpec((n_tok, tc), lambda b, j: (b, c0 + j)),
                  pl.BlockSpec((HY_SHORT, tc), lambda b, j: (0, c0 + j)),
                  pl.BlockSpec((1, tc), lambda b, j: (0, c0 + j))],
        out_specs=pl.BlockSpec((n_tok, tc), lambda b, j: (b, j)),
        out_shape=jax.ShapeDtypeStruct((batch * n_tok, ncols), out_dtype),
        compiler_params=_params(("arbitrary", "arbitrary")),
        name="hyena_short_conv",
    )(p, conv_w, conv_b.reshape(1, -1))


def _dft_fwd_kernel(fc_ref, fs_ref, r1_ref, r2_ref, *rest, spectral):
    xr = _dot(fc_ref[...], r1_ref[...])
    xi = _dot(fs_ref[...], r2_ref[...])
    if spectral:
        h_ref, nyq_ref, o_ref = rest
        hr = h_ref[0, 0]
        hi = h_ref[0, 1]
        dc = lax.broadcasted_iota(I32, xr.shape, 0) == 0
        yr = xr * hr - jnp.where(dc, 0.0, xi * hi)
        yi = jnp.where(dc, xi * nyq_ref[...], xr * hi + xi * hr)
    else:
        (o_ref,) = rest
        yr, yi = xr, xi
    o_ref[0, 0] = yr.astype(o_ref.dtype)
    o_ref[0, 1] = yi.astype(o_ref.dtype)


def _resident_spec(shape):
    return pl.BlockSpec(shape, lambda *_: (0,) * len(shape), pipeline_mode=pl.Buffered(1))


def _dft_fwd_call(fc, fs, r1, r2, spec, out_dtype, batch, n_tok, tn=256):
    ncols = r1.shape[1]
    r_spec = pl.BlockSpec((n_tok, tn), lambda j, b: (b, j))
    in_specs = [_resident_spec(fc.shape), _resident_spec(fs.shape), r_spec, r_spec]
    args = [fc, fs, r1, r2]
    if spec is not None:
        h, nyq, hcol0 = spec
        c0 = hcol0 // tn
        in_specs += [pl.BlockSpec((1, 2, n_tok, tn), lambda j, b: (0, 0, 0, c0 + j)),
                     pl.BlockSpec((1, tn), lambda j, b: (0, c0 + j))]
        args += [h, nyq]
    return pl.pallas_call(
        functools.partial(_dft_fwd_kernel, spectral=spec is not None),
        grid=(ncols // tn, batch),
        in_specs=in_specs,
        out_specs=pl.BlockSpec((1, 2, n_tok, tn), lambda j, b: (b, 0, 0, j)),
        out_shape=jax.ShapeDtypeStruct((batch, 2, n_tok, ncols), out_dtype),
        compiler_params=_params(("arbitrary", "arbitrary")),
        name="dft_forward",
    )(*args)


def _dual_kernel(a1_ref, a2_ref, b1_ref, b2_ref, *rest, hyena, planes):
    b1 = b1_ref[0, 0] if planes else b1_ref[...]
    b2 = b2_ref[0, 0] if planes else b2_ref[...]
    y = _dot(a1_ref[...], b1) + _dot(a2_ref[...], b2)
    if hyena:
        p_ref, cw_ref, cb_ref, z_ref, skip_ref, o_ref = rest
        gate = _short_conv(p_ref[...], cw_ref[...], cb_ref[...])
        y = gate * (y + skip_ref[...] * z_ref[...].astype(F32))
    else:
        (o_ref,) = rest
    o_ref[...] = y.astype(o_ref.dtype)


def _dual_call(a1, a2, b1, b2, hy, batch, n_tok, ncols, tn=256):
    planes = b1.ndim == 4
    if planes:
        b_specs = [pl.BlockSpec((1, 1, n_tok, tn), lambda b, j: (b, 0, 0, j)),
                   pl.BlockSpec((1, 1, n_tok, tn), lambda b, j: (b, 1, 0, j))]
    else:
        b_specs = [pl.BlockSpec((n_tok, tn), lambda b, j: (b, j))] * 2
    in_specs = [_resident_spec(a1.shape), _resident_spec(a2.shape)] + b_specs
    args = [a1, a2, b1, b2]
    o_spec = pl.BlockSpec((n_tok, tn), lambda b, j: (b, j))
    if hy is not None:
        p, conv_w, conv_b, pcol0, z, skip = hy
        c0 = pcol0 // tn
        in_specs += [pl.BlockSpec((n_tok, tn), lambda b, j: (b, c0 + j)),
                     pl.BlockSpec((HY_SHORT, tn), lambda b, j: (0, c0 + j)),
                     pl.BlockSpec((1, tn), lambda b, j: (0, c0 + j)),
                     o_spec,
                     pl.BlockSpec((1, tn), lambda b, j: (0, j))]
        args += [p, conv_w, conv_b.reshape(1, -1), z, skip]
    return pl.pallas_call(
        functools.partial(_dual_kernel, hyena=hy is not None, planes=planes),
        grid=(batch, ncols // tn),
        in_specs=in_specs,
        out_specs=o_spec,
        out_shape=jax.ShapeDtypeStruct((batch * n_tok, ncols), BF16),
        compiler_params=_params(("arbitrary", "arbitrary")),
        name="seq_mix_matmul",
    )(*args)


def _fnet_kernel(cl_ref, msl_ref, x_ref, cw_ref, o_ref):
    x = x_ref[...]
    seq = jnp.concatenate([_dot(cl_ref[...], x).astype(BF16), _dot(msl_ref[...], x).astype(BF16)], axis=1)
    o_ref[...] = _dot(seq, cw_ref[...]).astype(o_ref.dtype)


def _fnet_call(cl, msl, p, cw, col0, batch, n_tok):
    gw = cw.shape[1]
    c0 = col0 // gw
    return pl.pallas_call(
        _fnet_kernel,
        grid=(batch, FN_GROUPS),
        in_specs=[_resident_spec(cl.shape), _resident_spec(msl.shape),
                  pl.BlockSpec((n_tok, gw), lambda b, g: (b, c0 + g)),
                  pl.BlockSpec(cw.shape, lambda b, g: (0, 0))],
        out_specs=pl.BlockSpec((n_tok, gw), lambda b, g: (b, g)),
        out_shape=jax.ShapeDtypeStruct((batch * n_tok, FN_GROUPS * gw), BF16),
        compiler_params=_params(("arbitrary", "arbitrary")),
        name="fnet_mix",
    )(cl, msl, p, cw)


def _cos_sin_matrix(n_rows, n_cols, period, split=64):
    r = np.arange(n_rows, dtype=np.int64)[:, None]
    c_hi = (np.arange(n_cols // split, dtype=np.int64) * split)[None, :]
    c_lo = np.arange(split, dtype=np.int64)[None, :]
    ang = lambda c: 2.0 * np.pi * ((r * c) % period).astype(np.float64) / period
    ca, sa = jnp.asarray(np.cos(ang(c_hi)), F32), jnp.asarray(np.sin(ang(c_hi)), F32)
    cb, sb = jnp.asarray(np.cos(ang(c_lo)), F32), jnp.asarray(np.sin(ang(c_lo)), F32)
    cos = ca[:, :, None] * cb[:, None, :] - sa[:, :, None] * sb[:, None, :]
    sin = sa[:, :, None] * cb[:, None, :] + ca[:, :, None] * sb[:, None, :]
    return cos.reshape(n_rows, n_cols), sin.reshape(n_rows, n_cols)


def _trig_kernel(ta_ref, tb_ref, ea_ref, eb_ref, *o_refs, mode, period, scale):
    sa_ca = _dot(ta_ref[...], ea_ref[...])
    sb_cb = _dot(tb_ref[...], eb_ref[...])
    nc = sa_ca.shape[1] // 2
    ca, sa, cb, sb = sa_ca[:, :nc], sa_ca[:, nc:], sb_cb[:, :nc], sb_cb[:, nc:]
    cos = ca * cb - sa * sb
    sin = sa * cb + ca * sb
    tm = cos.shape[0]
    row = pl.program_id(0) * tm + lax.broadcasted_iota(I32, cos.shape, 0)
    col = lax.broadcasted_iota(I32, cos.shape, 1)
    alt = lambda idx: (1 - 2 * jnp.bitwise_and(idx, 1)).astype(F32)
    if mode == "hyena":
        fc_ref, fs_ref, gc_ref, gs_ref = o_refs
        fc_ref[...] = cos.astype(fc_ref.dtype)
        fs_ref[...] = jnp.where(row == 0, alt(col), -sin).astype(fs_ref.dtype)
        gc_ref[...] = (cos * jnp.where(col == 0, 1.0 / period, 2.0 / period)).astype(gc_ref.dtype)
        gs_ref[...] = jnp.where(col == 0, alt(row) * (1.0 / period), sin * (-2.0 / period)).astype(gs_ref.dtype)
    else:
        c_ref, ms_ref = o_refs
        c_ref[...] = (cos * scale).astype(c_ref.dtype)
        ms_ref[...] = (sin * (-scale)).astype(ms_ref.dtype)


def _split3(x):
    bf = jnp.bfloat16
    x = np.asarray(x, np.float32)
    hi = x.astype(bf)
    r1 = x - hi.astype(np.float32)
    mid = r1.astype(bf)
    lo = (r1 - mid.astype(np.float32)).astype(bf)
    return [hi, mid, lo]


def _trig_call(n, period, mode, scale=1.0, split=64, tm=256):
    r = np.arange(n, dtype=np.int64)[:, None]
    c_hi = (np.arange(n // split, dtype=np.int64) * split)[None, :]
    c_lo = np.arange(split, dtype=np.int64)[None, :]
    ang = lambda c: 2.0 * np.pi * ((r * c) % period).astype(np.float64) / period
    def operands(a, col_group):
        terms = _split3(np.cos(a)) + _split3(np.sin(a))
        k = a.shape[1]
        spread = (col_group[None, :] == np.arange(k)[:, None]).astype(np.float32)
        zero = np.zeros_like(spread)
        e = np.concatenate([np.concatenate([spread, zero], 1)] * 3 + [np.concatenate([zero, spread], 1)] * 3, 0)
        return jnp.asarray(np.concatenate(terms, axis=1)), jnp.asarray(e, BF16)
    cols = np.arange(n)
    ta, ea = operands(ang(c_hi), cols // split)
    tb, eb = operands(ang(c_lo), cols % split)
    n_out = 4 if mode == "hyena" else 2
    o_spec = pl.BlockSpec((tm, n), lambda i: (i, 0))
    return pl.pallas_call(
        functools.partial(_trig_kernel, mode=mode, period=period, scale=scale),
        grid=(n // tm,),
        in_specs=[pl.BlockSpec((tm, ta.shape[1]), lambda i: (i, 0)),
                  pl.BlockSpec((tm, tb.shape[1]), lambda i: (i, 0)),
                  pl.BlockSpec(ea.shape, lambda i: (0, 0)),
                  pl.BlockSpec(eb.shape, lambda i: (0, 0))],
        out_specs=[o_spec] * n_out,
        out_shape=[jax.ShapeDtypeStruct((n, n), BF16)] * n_out,
        compiler_params=_params(("arbitrary",)),
        name="dft_tables",
    )(ta, tb, ea, eb)


def _hyena_dft_operands(n_tok):
    return _trig_call(n_tok, 2 * n_tok, "hyena")


def _fnet_operands(n_tok, gw):
    cl, msl = _trig_call(n_tok, n_tok, "fnet", scale=1.0 / math.sqrt(n_tok * gw))
    cw, sw = _cos_sin_matrix(gw, gw, gw)
    return cl, msl, jnp.concatenate([cw, sw], axis=0).astype(BF16)


def _filter_kernel(z_ref, w1_ref, b1_ref, f1_ref, w2_ref, b2_ref, f2_ref, w3f_ref, w3b_ref, dl_ref,
                   sum_ref, dif_ref, nyq_ref, hdn_ref):
    @pl.when(jnp.logical_and(pl.program_id(0) == 0, pl.program_id(1) == 0))
    def _():
        hdot = lambda a, b: jnp.dot(a, b, precision=lax.Precision.HIGHEST, preferred_element_type=F32)
        hdn = jnp.sin(f1_ref[...] * (hdot(z_ref[...], w1_ref[...]) + b1_ref[...]))
        hdn = jnp.sin(f2_ref[...] * (hdot(hdn, w2_ref[...]) + b2_ref[...]))
        hdn_ref[...] = hdn.astype(hdn_ref.dtype)

    n = hdn_ref.shape[0]
    row = lax.broadcasted_iota(I32, (n, dl_ref.shape[1]), 0)
    decay = jnp.exp(-(row.astype(F32) / (n - 1)) * dl_ref[...])
    hf = _dot(hdn_ref[...], w3f_ref[...].astype(BF16)) * decay
    hb = jnp.where(row == 0, 0.0, _dot(hdn_ref[...], w3b_ref[...].astype(BF16)) * decay)
    tot = hf + hb
    sum_ref[...] = tot.astype(sum_ref.dtype)
    dif_ref[...] = (hf - hb).astype(dif_ref.dtype)
    nyq_ref[...] = jnp.sum(jnp.where(jnp.bitwise_and(row, 1) == 0, tot, -tot), axis=0, keepdims=True)


def _filter_call(n_tok, fw1, fb1, ff1, fw2, fb2, ff2, fw3, tc=512):
    t01 = np.linspace(0.0, 1.0, n_tok, dtype=np.float32)
    w = (2.0 * math.pi * np.arange(n_tok, dtype=np.float32) / n_tok).astype(np.float32)
    bands = np.linspace(1e-4, HY_BANDS - 1, HY_BANDS, dtype=np.float32)
    z = np.concatenate([t01[:, None], np.cos(w[:, None] * bands), -np.sin(w[:, None] * bands)], -1)
    deltas = np.abs(np.linspace(HY_MIN_DECAY, HY_MAX_DECAY, HY_W, dtype=np.float32))[None, :]
    emb, hid = fw1.shape
    nc = HY_W // tc
    full = lambda shape: pl.BlockSpec(shape, lambda o, j: (0,) * len(shape))
    o_spec = pl.BlockSpec((n_tok, tc), lambda o, j: (0, o * nc + j))
    return pl.pallas_call(
        _filter_kernel,
        grid=(HY_ORDER, nc),
        in_specs=[full((n_tok, emb)), full((emb, hid)), full((1, hid)), full((1, hid)),
                  full((hid, hid)), full((1, hid)), full((1, hid)),
                  pl.BlockSpec((hid, tc), lambda o, j: (0, (2 * o) * nc + j)),
                  pl.BlockSpec((hid, tc), lambda o, j: (0, (2 * o + 1) * nc + j)),
                  pl.BlockSpec((1, tc), lambda o, j: (0, j))],
        out_specs=[o_spec, o_spec, pl.BlockSpec((1, tc), lambda o, j: (0, o * nc + j))],
        out_shape=[jax.ShapeDtypeStruct((n_tok, HY_ORDER * HY_W), BF16),
                   jax.ShapeDtypeStruct((n_tok, HY_ORDER * HY_W), BF16),
                   jax.ShapeDtypeStruct((1, HY_ORDER * HY_W), F32)],
        scratch_shapes=[pltpu.VMEM((n_tok, hid), BF16)],
        compiler_params=_params(("arbitrary", "arbitrary")),
        name="hyena_filters",
    )(jnp.asarray(z.astype(np.float32)), fw1, fb1.reshape(1, hid), ff1.reshape(1, hid),
      fw2, fb2.reshape(1, hid), ff2.reshape(1, hid), fw3, fw3, jnp.asarray(deltas))


def kernel(x, c, ctx, c_ctx, ada_w, ada_b, ln1_g, ln1_b, ln2_g, ln2_b, router, exp_w1, exp_w3, exp_w2,
           ab_w_in, ab_kv_norm, ab_w_ukv, ab_rpb, ab_w_out,
           cd_w_in, cd_conv_w, cd_conv_b, cd_filt_w1, cd_filt_b1, cd_filt_freq1, cd_filt_w2, cd_filt_b2,
           cd_filt_freq2, cd_filt_w3, cd_skip, cd_w_out):
    batch, n_tok, d = x.shape
    lc = ctx.shape[1]
    x2d = x.reshape(batch * n_tok, d)
    ctx2d = ctx.reshape(batch * lc, d)

    pad_rows = (-(batch + 1)) % 8
    cv = jnp.concatenate([c, c_ctx[None, :], jnp.zeros((pad_rows, d), F32)], axis=0)
    ada = _ada_call(cv, ada_w, ada_b)
    mods = [ada[i, :batch].reshape(batch, 6, d) for i in range(DEPTH)]
    ctx_ss = ada[0, batch, :2 * d].reshape(1, 2, d)

    w_in = _w_in_prep_call(jnp.swapaxes(ab_w_in, 1, 2), 0)
    p_lat = _mm_rope_call(x2d, mods[0][:, 0:2], n_tok, w_in, _rope_tables(n_tok), tm=1024)
    p_ctx = _mm_rope_call(ctx2d, ctx_ss, batch * lc, w_in, _identity_rope_tables(batch * lc), tm=batch * lc,
                          tile0=CTX_TILE0)
    w_ukv = _permute_w_ukv(ab_w_ukv[0])
    kv_lat = _kvup_call(p_lat, ab_kv_norm[0], w_ukv, tm=512)
    kv_ctx = _kvup_call(p_ctx, ab_kv_norm[0], w_ukv, tm=lc, tile0=CTX_TILE0)
    a_out = _mla_call(p_lat, kv_lat, p_ctx, kv_ctx, batch, n_tok, lc)
    var_map, na_er, na_ec = _na_tables(n_tok // GRID_W)
    b_out = _na_call(p_lat, p_ctx, _na_bias(ab_rpb[0], na_er, na_ec), jnp.asarray(var_map),
                     batch, n_tok, lc)
    ln1 = jnp.stack([ln1_g, ln1_b], axis=1)
    ln2 = jnp.stack([ln2_g, ln2_b], axis=1)
    h1, u2, logits = _outln_call(a_out, b_out, ab_w_out, 0, x2d,
                                 mods[0][:, 2:5], ln1[0], _router_operand(router[0]), n_tok)
    h, u = _ec_moe(h1, u2, logits, exp_w1, exp_w3, exp_w2, 0, mods[0][:, 5:6], ln2[0],
                   mods[1][:, 0:2], batch, n_tok)

    p1 = _mm_call(u, cd_w_in, 0, BF16)
    s0 = _sconv_call(p1, cd_conv_w[0], cd_conv_b[0], 0, HY_W, BF16, batch, n_tok)
    h_sum, h_dif, h_nyq = _filter_call(n_tok, cd_filt_w1[0], cd_filt_b1[0], cd_filt_freq1[0], cd_filt_w2[0],
                                       cd_filt_b2[0], cd_filt_freq2[0], cd_filt_w3[0])
    fc, fs, gc, gs = _hyena_dft_operands(n_tok)
    h_spec = _dft_fwd_call(fc, fs, h_sum, h_dif, None, F32, 1, n_tok)
    z = s0
    for o in range(HY_ORDER):
        y_spec = _dft_fwd_call(fc, fs, z, z, (h_spec, h_nyq, o * HY_W), BF16, batch, n_tok)
        z = _dual_call(gc, gs, y_spec, y_spec,
                       (p1, cd_conv_w[0], cd_conv_b[0], (o + 1) * HY_W, z, cd_skip[0][o:o + 1]),
                       batch, n_tok, HY_W)
    cl, msl, cw = _fnet_operands(n_tok, FN_GROUP_W)
    y_fn = _fnet_call(cl, msl, p1, cw, HY_IN_W, batch, n_tok)
    h1, u2, logits = _outln_call(z, y_fn, cd_w_out, 0, h,
                                 mods[1][:, 2:5], ln1[1], _router_operand(router[1]), n_tok)
    h, _ = _ec_moe(h1, u2, logits, exp_w1, exp_w3, exp_w2, 1, mods[1][:, 5:6], ln2[1],
                   None, batch, n_tok)
    return h.reshape(batch, n_tok, d)
```

```python
import functools
import math

import numpy as np
import jax
import jax.numpy as jnp
from jax import lax
from jax.experimental import pallas as pl
from jax.experimental.pallas import tpu as pltpu

F32 = jnp.float32
BF16 = jnp.bfloat16
I32 = jnp.int32

D_MODEL = 2048
BATCH = 4
SEQ = 2048
DEPTH = 2
CTX_LEN = 256
GRID_W = 64
DEEPNORM_ALPHA = (2.0 * DEPTH) ** 0.25
LN_EPS = 1e-6
NEG_INF = -1e30
LOG2E = math.log2(math.e)

MLA_HEADS = 8
MLA_NOPE = 128
MLA_ROPE = 64
MLA_QK = MLA_NOPE + MLA_ROPE
MLA_V = 128
MLA_KV_RANK = 512
ROPE_THETA = 10000.0

NA_HEADS = 8
NA_HEAD_DIM = 128
NA_KH = 8
NA_KW = 16

OFF_CKV = MLA_HEADS * MLA_QK
OFF_KPE = OFF_CKV + MLA_KV_RANK
OFF_QNA = OFF_KPE + MLA_ROPE
OFF_KNA = OFF_QNA + NA_HEADS * NA_HEAD_DIM
OFF_VNA = OFF_KNA + NA_HEADS * NA_HEAD_DIM

HY_W = 1024
HY_ORDER = 2
HY_IN_W = (HY_ORDER + 1) * HY_W
HY_SHORT = 3
HY_BANDS = 16
HY_DECAY_TARGET = 1e-2
HY_MIN_DECAY = math.log(HY_DECAY_TARGET) / 1.5
HY_MAX_DECAY = math.log(HY_DECAY_TARGET) / 0.3

FN_W = 1024
FN_GROUPS = 4
FN_GROUP_W = FN_W // FN_GROUPS

N_EXPERTS = 16
EC_CAPACITY_FACTOR = 2
EXPERT_FF = 1408

LANES = 128
MXU_DIM = 256
VMEM_LIMIT = 56 * 1024 * 1024

QTILE_W = 2 * LANES
T_KPE = MLA_HEADS
T_CKV = T_KPE + 2
T_QNA = T_CKV + MLA_KV_RANK // QTILE_W
T_KNA = T_QNA + NA_HEADS * NA_HEAD_DIM // QTILE_W
T_VNA = T_KNA + NA_HEADS * NA_HEAD_DIM // QTILE_W
N_ABTILES = T_VNA + NA_HEADS * NA_HEAD_DIM // QTILE_W
AB_PERM_W = N_ABTILES * QTILE_W
CTX_TILE0 = T_KPE
INPROJ_TILES_PER_STEP = 8

MLA_SUBTILES = 8
OUTLN_SUBTILES = 2
MOE_CHUNK = 256
MOE_WIN = 64
NA_G = 4
NA_WR = NA_KH + NA_G - 1
NA_GROUPS_PER_STEP = 8


def _params(sem, vmem=VMEM_LIMIT):
    return pltpu.CompilerParams(dimension_semantics=sem, vmem_limit_bytes=vmem)


def _dot(a, b):
    return jnp.dot(a, b, preferred_element_type=F32)


def _dot_nt(a, b):
    return lax.dot_general(a, b, (((1,), (1,)), ((), ())), preferred_element_type=F32)


def _dot_tn(a, b):
    return lax.dot_general(a, b, (((0,), (0,)), ((), ())), preferred_element_type=F32)


def _ada_kernel(c_ref, w_ref, b_ref, o_ref):
    c = c_ref[...]
    a = c / (1.0 + jnp.exp(-c))
    o_ref[0] = _dot(a.astype(BF16), w_ref[0].astype(BF16)) + b_ref[0]


def _ada_call(cv, ada_w, ada_b, tn=2048):
    depth, d, n = ada_w.shape
    rows = cv.shape[0]
    return pl.pallas_call(
        _ada_kernel,
        grid=(depth, n // tn),
        in_specs=[pl.BlockSpec((rows, d), lambda l, j: (0, 0)),
                  pl.BlockSpec((1, d, tn), lambda l, j: (l, 0, j)),
                  pl.BlockSpec((1, 1, tn), lambda l, j: (l, 0, j))],
        out_specs=pl.BlockSpec((1, rows, tn), lambda l, j: (l, 0, j)),
        out_shape=jax.ShapeDtypeStruct((depth, rows, n), F32),
        compiler_params=_params(("arbitrary", "arbitrary")),
        name="ada_params",
    )(cv, ada_w, ada_b.reshape(depth, 1, n))


def _mm_rope_kernel(x_ref, ss_ref, w_ref, ta_ref, tb_ref, tc_ref, o_ref, a_ref, *, tile0):
    @pl.when(pl.program_id(1) == 0)
    def _():
        a_ref[...] = (x_ref[...] * (1.0 + ss_ref[0, 1:2, :]) + ss_ref[0, 0:1, :]).astype(a_ref.dtype)

    a = a_ref[...]
    for q in range(INPROJ_TILES_PER_STEP):
        t = tile0 + INPROJ_TILES_PER_STEP * pl.program_id(1) + q
        c0 = q * QTILE_W
        acc = _dot_nt(a, w_ref[c0:c0 + QTILE_W, :])
        is_rope = t <= T_KPE
        f = jnp.where(t < T_KPE, LOG2E * MLA_QK ** -0.5,
                      jnp.where(jnp.logical_and(t >= T_QNA, t < T_KNA), LOG2E * NA_HEAD_DIM ** -0.5, 1.0)
                      ).astype(F32)
        hi = acc[:, LANES:]
        rot = (hi * jnp.where(is_rope, ta_ref[...], 1.0)
               + pltpu.roll(hi, LANES - MLA_ROPE // 2, 1) * jnp.where(is_rope, tb_ref[...], 0.0)
               + pltpu.roll(hi, MLA_ROPE // 2, 1) * jnp.where(is_rope, tc_ref[...], 0.0))
        o_ref[:, c0:c0 + LANES] = (acc[:, :LANES] * f).astype(o_ref.dtype)
        o_ref[:, c0 + LANES:c0 + QTILE_W] = (rot * f).astype(o_ref.dtype)


def _mm_rope_call(x, ss, rows_per_group, w, tabs, tm, tile0=0):
    m, k = x.shape
    tn = INPROJ_TILES_PER_STEP * QTILE_W
    n = w.shape[0] - tile0 * QTILE_W
    j0 = tile0 // INPROJ_TILES_PER_STEP
    ntab = tabs[0].shape[0] // tm
    tpg = rows_per_group // tm
    tab_spec = pl.BlockSpec((tm, LANES), lambda i, j: (i % ntab, 0))
    return pl.pallas_call(
        functools.partial(_mm_rope_kernel, tile0=tile0),
        grid=(m // tm, n // tn),
        in_specs=[pl.BlockSpec((tm, k), lambda i, j: (i, 0)),
                  pl.BlockSpec((1, 2, k), lambda i, j: (i // tpg, 0, 0)),
                  pl.BlockSpec((tn, k), lambda i, j: (j0 + j, 0)),
                  tab_spec, tab_spec, tab_spec],
        out_specs=pl.BlockSpec((tm, tn), lambda i, j: (i, j)),
        out_shape=jax.ShapeDtypeStruct((m, n), BF16),
        scratch_shapes=[pltpu.VMEM((tm, k), BF16)],
        compiler_params=_params(("arbitrary", "arbitrary")),
        name="attn_in_proj",
    )(x, ss, w, *tabs)


def _rope_tables(n_tok):
    t = np.arange(n_tok)
    row = (t // GRID_W).astype(np.float32)
    col = (t % GRID_W).astype(np.float32)
    n_freq = MLA_ROPE // 4
    inv = (ROPE_THETA ** (-np.arange(n_freq, dtype=np.float32) / n_freq)).astype(np.float32)
    ang = np.concatenate([row[:, None] * inv, col[:, None] * inv], axis=1)
    cos, sin = np.cos(ang).astype(np.float32), np.sin(ang).astype(np.float32)
    half = MLA_ROPE // 2
    ta = np.zeros((n_tok, LANES), np.float32)
    tb = np.zeros((n_tok, LANES), np.float32)
    tc = np.zeros((n_tok, LANES), np.float32)
    ta[:, :half] = cos
    ta[:, half:2 * half] = cos
    tb[:, :half] = -sin
    tc[:, half:2 * half] = sin
    return jnp.asarray(ta), jnp.asarray(tb), jnp.asarray(tc)


def _identity_rope_tables(n_tok):
    ta = np.zeros((n_tok, LANES), np.float32)
    ta[:, :MLA_ROPE] = 1.0
    z = np.zeros((n_tok, LANES), np.float32)
    return jnp.asarray(ta), jnp.asarray(z), jnp.asarray(z)


def _w_in_prep_kernel(w_ref, o_ref):
    x = w_ref[0]
    zeros = lambda n: jnp.zeros((n, x.shape[1]), x.dtype)
    rows = []
    for h in range(MLA_HEADS):
        rows += [x[h * MLA_QK:(h + 1) * MLA_QK, :], zeros(QTILE_W - MLA_QK)]
    rows += [zeros(LANES), x[OFF_KPE:OFF_QNA, :], zeros(LANES - MLA_ROPE), zeros(QTILE_W),
             x[OFF_CKV:OFF_KPE, :], x[OFF_QNA:, :]]
    o_ref[...] = jnp.concatenate(rows, axis=0).astype(o_ref.dtype)


def _w_in_prep_call(w_in_t, layer, tk=256):
    _, n, d = w_in_t.shape
    return pl.pallas_call(
        _w_in_prep_kernel,
        grid=(d // tk,),
        in_specs=[pl.BlockSpec((1, n, tk), lambda i: (layer, 0, i))],
        out_specs=pl.BlockSpec((AB_PERM_W, tk), lambda i: (0, i)),
        out_shape=jax.ShapeDtypeStruct((AB_PERM_W, d), BF16),
        compiler_params=_params(("arbitrary",)),
        name="attn_w_in_prep",
    )(w_in_t)


def _kvup_kernel(x_ref, g_ref, w_ref, o_ref):
    x = x_ref[...].astype(F32)
    y = x * lax.rsqrt(jnp.mean(x * x, axis=-1, keepdims=True) + LN_EPS) * g_ref[...]
    o_ref[...] = _dot(y.astype(BF16), w_ref[...]).astype(o_ref.dtype)


def _kvup_call(p, g, w, tm, tile0=0):
    m = p.shape[0]
    r, n = w.shape
    return pl.pallas_call(
        _kvup_kernel,
        grid=(m // tm,),
        in_specs=[pl.BlockSpec((tm, r), lambda i: (i, (T_CKV - tile0) * QTILE_W // r)),
                  pl.BlockSpec((1, r), lambda i: (0, 0)),
                  pl.BlockSpec((r, n), lambda i: (0, 0))],
        out_specs=pl.BlockSpec((tm, n), lambda i: (i, 0)),
        out_shape=jax.ShapeDtypeStruct((m, n), BF16),
        compiler_params=_params(("arbitrary",)),
        name="kv_up",
    )(p, g.reshape(1, r), w)


def _permute_w_ukv(w):
    r = w.shape[0]
    w3 = w.reshape(r, MLA_HEADS, MLA_NOPE + MLA_V)
    return jnp.concatenate([w3[:, :, :MLA_NOPE].reshape(r, -1), w3[:, :, MLA_NOPE:].reshape(r, -1)],
                           axis=1).astype(BF16)


def _mla_kernel(q_ref, kn_ref, kp_ref, v_ref, kcn_ref, kcp_ref, vc_ref, o_ref, kcat, vcat):
    s_len = kn_ref.shape[0]

    @pl.when(pl.program_id(2) == 0)
    def _():
        kcat[:s_len, :LANES] = kn_ref[...]
        kcat[:s_len, LANES:] = kp_ref[...]
        kcat[s_len:, :LANES] = kcn_ref[...]
        kcat[s_len:, LANES:] = kcp_ref[...]
        vcat[:s_len, :] = v_ref[...]
        vcat[s_len:, :] = vc_ref[...]

    sub = q_ref.shape[0] // MLA_SUBTILES
    for r0 in range(0, q_ref.shape[0], sub):
        s = _dot_nt(q_ref[r0:r0 + sub, :], kcat[...])
        m = jnp.max(s, axis=-1, keepdims=True)
        p = jnp.exp2(s - m)
        l = jnp.sum(p, axis=-1, keepdims=True)
        o = _dot(p.astype(BF16), vcat[...])
        o_ref[r0:r0 + sub, :] = (o / l).astype(o_ref.dtype)


def _mla_call(p_lat, kv_lat, p_ctx, kv_ctx, batch, s_len, lc, tq=2048):
    nq = s_len // tq
    kpe_blk = T_KPE * 2 + 1
    return pl.pallas_call(
        _mla_kernel,
        grid=(batch, MLA_HEADS, nq),
        in_specs=[pl.BlockSpec((tq, QTILE_W), lambda b, h, i: (b * nq + i, h)),
                  pl.BlockSpec((s_len, LANES), lambda b, h, i: (b, h)),
                  pl.BlockSpec((s_len, LANES), lambda b, h, i: (b, kpe_blk)),
                  pl.BlockSpec((s_len, LANES), lambda b, h, i: (b, MLA_HEADS + h)),
                  pl.BlockSpec((lc, LANES), lambda b, h, i: (b, h)),
                  pl.BlockSpec((lc, LANES), lambda b, h, i: (b, kpe_blk - 2 * CTX_TILE0)),
                  pl.BlockSpec((lc, LANES), lambda b, h, i: (b, MLA_HEADS + h))],
        out_specs=pl.BlockSpec((tq, LANES), lambda b, h, i: (b * nq + i, h)),
        out_shape=jax.ShapeDtypeStruct((batch * s_len, MLA_HEADS * MLA_V), BF16),
        scratch_shapes=[pltpu.VMEM((s_len + lc, QTILE_W), BF16),
                        pltpu.VMEM((s_len + lc, LANES), BF16)],
        compiler_params=_params(("arbitrary", "arbitrary", "arbitrary")),
        name="mla_attention",
    )(p_lat, kv_lat, p_lat, kv_lat, kv_ctx, p_ctx, kv_ctx)


def _na_kernel(var_ref, q_ref, k_ref, v_ref, kc_ref, vc_ref, *rest, rows):
    del var_ref
    bias_refs, o_ref = rest[:-1], rest[-1]
    win = NA_WR * GRID_W
    tq = NA_G * GRID_W
    for i, bias_ref in enumerate(bias_refs):
        g = len(bias_refs) * pl.program_id(2) + i
        start_row = jnp.clip(NA_G * g - NA_KH // 2, 0, rows - NA_WR)
        start = pl.multiple_of(start_row * GRID_W, GRID_W)
        q = q_ref[i * tq:(i + 1) * tq, :]
        sw = _dot_nt(q, k_ref[pl.ds(start, win), :]) + bias_ref[0, 0]
        sc = _dot_nt(q, kc_ref[...])
        m = jnp.maximum(jnp.max(sw, axis=-1, keepdims=True), jnp.max(sc, axis=-1, keepdims=True))
        pw = jnp.exp2(sw - m)
        pc = jnp.exp2(sc - m)
        l = jnp.sum(pw, axis=-1, keepdims=True) + jnp.sum(pc, axis=-1, keepdims=True)
        o = _dot(pw.astype(BF16), v_ref[pl.ds(start, win), :]) + _dot(pc.astype(BF16), vc_ref[...])
        o_ref[i * tq:(i + 1) * tq, :] = (o / l).astype(o_ref.dtype)


def _na_tables(rows):
    kh = min(NA_KH, rows)
    ng = rows // NA_G
    qr = np.arange(NA_G)[:, None]
    kr = np.arange(NA_WR)[None, :]
    sel_rows = []
    for g in range(ng):
        start_row = int(np.clip(NA_G * g - NA_KH // 2, 0, rows - NA_WR))
        r = NA_G * g + qr
        krow = start_row + kr
        rs = np.clip(r - kh // 2, 0, rows - kh)
        ok = (krow >= rs) & (krow < rs + kh)
        dr = krow - r + (NA_KH - 1)
        sel_rows.append(np.stack([ok & (dr == d) for d in range(2 * NA_KH - 1)]))
    sel_rows = np.stack(sel_rows)
    uniq, inverse = np.unique(sel_rows.reshape(ng, -1), axis=0, return_inverse=True)
    er = uniq.reshape((-1,) + sel_rows.shape[1:]).astype(np.float32)
    qc = np.arange(GRID_W)[:, None]
    kc = np.arange(GRID_W)[None, :]
    cs = np.clip(qc - NA_KW // 2, 0, GRID_W - NA_KW)
    ok_c = (kc >= cs) & (kc < cs + NA_KW)
    dc = np.clip(kc - qc, -(NA_KW - 1), NA_KW - 1) + (NA_KW - 1)
    ec = np.stack([ok_c & (dc == e) for e in range(2 * NA_KW - 1)]).astype(np.float32)
    return inverse.reshape(-1).astype(np.int32), er, ec


def _na_bias(rpb, er, ec):
    hp = lax.Precision.HIGHEST
    t = jnp.einsum('hde,eqk->hdqk', rpb.astype(F32), jnp.asarray(ec), precision=hp)
    t = jnp.where(jnp.asarray(ec.sum(axis=0) > 0.5), t * LOG2E, NEG_INF)
    outside = jnp.full(t[:, 0].shape, NEG_INF, F32)
    d_of = np.where(er.sum(axis=1) > 0.5, er.argmax(axis=1), -1)
    block = lambda d: outside if d < 0 else t[:, d]
    return jnp.stack([
        jnp.concatenate([jnp.concatenate([block(int(d)) for d in row], axis=-1) for row in var], axis=-2)
        for var in d_of])


def _na_call(p_lat, p_ctx, bias, var_map, batch, s_len, lc):
    rows = s_len // GRID_W
    gps = NA_GROUPS_PER_STEP
    ns = rows // (NA_G * gps)
    tq = NA_G * GRID_W
    win = NA_WR * GRID_W
    qb, kb, vb = T_QNA * 2, T_KNA * 2, T_VNA * 2
    bias_spec = lambda i: pl.BlockSpec((1, 1, tq, win), lambda b, h, s, vm: (vm[gps * s + i], h, 0, 0))
    grid_spec = pltpu.PrefetchScalarGridSpec(
        num_scalar_prefetch=1,
        grid=(batch, NA_HEADS, ns),
        in_specs=[pl.BlockSpec((gps * tq, LANES), lambda b, h, s, vm: (b * ns + s, qb + h)),
                  pl.BlockSpec((s_len, LANES), lambda b, h, s, vm: (b, kb + h)),
                  pl.BlockSpec((s_len, LANES), lambda b, h, s, vm: (b, vb + h)),
                  pl.BlockSpec((lc, LANES), lambda b, h, s, vm: (b, kb + h - 2 * CTX_TILE0)),
                  pl.BlockSpec((lc, LANES), lambda b, h, s, vm: (b, vb + h - 2 * CTX_TILE0))]
                 + [bias_spec(i) for i in range(gps)],
        out_specs=pl.BlockSpec((gps * tq, LANES), lambda b, h, s, vm: (b * ns + s, h)),
    )
    return pl.pallas_call(
        functools.partial(_na_kernel, rows=rows),
        grid_spec=grid_spec,
        out_shape=jax.ShapeDtypeStruct((batch * s_len, NA_HEADS * NA_HEAD_DIM), BF16),
        compiler_params=_params(("arbitrary", "arbitrary", "arbitrary")),
        name="na_attention",
    )(var_map, p_lat, p_lat, p_lat, p_ctx, p_ctx, *([bias] * gps))


def _layer_norm(x, g, b):
    mu = jnp.mean(x, axis=-1, keepdims=True)
    xc = x - mu
    var = jnp.mean(xc * xc, axis=-1, keepdims=True)
    return xc * lax.rsqrt(var + LN_EPS) * g + b


def _outln_kernel(a1_ref, a2_ref, w_ref, h_ref, mod_ref, ln_ref, r_ref, h1_ref, u2_ref, lg_ref, wb_ref):
    half = a1_ref.shape[1]

    @pl.when(pl.program_id(0) == 0)
    def _():
        rows = wb_ref.shape[0] // 4
        for r0 in range(0, wb_ref.shape[0], rows):
            wb_ref[r0:r0 + rows, :] = w_ref[0, r0:r0 + rows, :].astype(wb_ref.dtype)

    sub = a1_ref.shape[0] // OUTLN_SUBTILES
    for r0 in range(0, a1_ref.shape[0], sub):
        rs = slice(r0, r0 + sub)
        y = _dot(a1_ref[rs, :], wb_ref[:half, :]) + _dot(a2_ref[rs, :], wb_ref[half:, :])
        x = DEEPNORM_ALPHA * h_ref[rs, :] + mod_ref[0, 0:1, :] * y
        hn = _layer_norm(x, ln_ref[0:1, :], ln_ref[1:2, :])
        h1_ref[rs, :] = hn
        u = hn * (1.0 + mod_ref[0, 2:3, :]) + mod_ref[0, 1:2, :]
        u_bf = u.astype(BF16)
        u2_ref[rs, :] = u_bf
        lg_ref[rs, :] = _dot(u_bf, r_ref[...])


def _outln_call(a1, a2, w, layer, h, mod, ln, router2, rows_per_group, tm=512):
    m, d = h.shape
    half = a1.shape[1]
    tpg = rows_per_group // tm
    return pl.pallas_call(
        _outln_kernel,
        grid=(m // tm,),
        in_specs=[pl.BlockSpec((tm, half), lambda i: (i, 0)),
                  pl.BlockSpec((tm, half), lambda i: (i, 0)),
                  pl.BlockSpec((1, 2 * half, d), lambda i: (layer, 0, 0), pipeline_mode=pl.Buffered(1)),
                  pl.BlockSpec((tm, d), lambda i: (i, 0)),
                  pl.BlockSpec((1, 3, d), lambda i: (i // tpg, 0, 0)),
                  pl.BlockSpec((2, d), lambda i: (0, 0)),
                  _resident_spec((d, LANES))],
        out_specs=[pl.BlockSpec((tm, d), lambda i: (i, 0)),
                   pl.BlockSpec((tm, d), lambda i: (i, 0)),
                   pl.BlockSpec((tm, LANES), lambda i: (i, 0))],
        out_shape=[jax.ShapeDtypeStruct((m, d), F32),
                   jax.ShapeDtypeStruct((m, d), BF16),
                   jax.ShapeDtypeStruct((m, LANES), F32)],
        scratch_shapes=[pltpu.VMEM((2 * half, d), BF16)],
        compiler_params=_params(("arbitrary",)),
        name="out_proj_postnorm",
    )(a1, a2, w, h, mod, ln, router2)


def _router_operand(router):
    d, e = router.shape
    return jnp.pad(router, ((0, 0), (0, LANES - e))).astype(BF16)


def _route_kernel(lg_ref, slot_ref, aff_ref, cnt_ref, tri_ref, *, cap):
    n_tok = lg_ref.shape[2]

    @pl.when(pl.program_id(0) == 0)
    def _():
        chunk = 256
        for r0 in range(0, n_tok, chunk):
            r = r0 + lax.broadcasted_iota(I32, (chunk, n_tok), 0)
            c = lax.broadcasted_iota(I32, (chunk, n_tok), 1)
            tri_ref[r0:r0 + chunk, :] = jnp.where(r < c, 1.0, 0.0).astype(BF16)

    lg = lg_ref[0]
    ex = jnp.exp(lg - jnp.max(lg, axis=0, keepdims=True))
    aff = ex / jnp.sum(ex, axis=0, keepdims=True)
    bits = lax.bitcast_convert_type(aff, I32)
    n_e = lg.shape[0]
    count = lambda mask: jnp.sum(jnp.where(mask, 1.0, 0.0), axis=1, keepdims=True)

    def body(_, lohi):
        lo, hi = lohi
        mid = lo + jnp.right_shift(hi - lo, 1)
        ge = count(bits >= mid) >= cap
        return jnp.where(ge, mid, lo), jnp.where(ge, hi, mid)

    lo0 = jnp.zeros((n_e, 1), I32)
    hi0 = jnp.full((n_e, 1), 0x7F800000, I32)
    thr, _ = lax.fori_loop(0, 31, body, (lo0, hi0))
    gt = bits > thr
    eq = bits == thr
    need = cap - count(gt)
    pre_eq = _dot(jnp.where(eq, 1.0, 0.0).astype(BF16), tri_ref[...])
    sel = jnp.logical_or(gt, jnp.logical_and(eq, pre_eq < need))
    sel_bf = jnp.where(sel, 1.0, 0.0).astype(BF16)
    slot = _dot(sel_bf, tri_ref[...])
    slot_ref[0] = jnp.where(sel, slot.astype(I32), -1)
    aff_ref[0] = aff
    r = lax.broadcasted_iota(I32, (n_tok, LANES), 0)
    c = lax.broadcasted_iota(I32, (n_tok, LANES), 1)
    before = jnp.where(r < c * MOE_CHUNK, 1.0, 0.0).astype(BF16)
    cnt_ref[0] = _dot(sel_bf, before).astype(I32)


def _route_call(lg_t, cap):
    b, e, n_tok = lg_t.shape
    spec = pl.BlockSpec((1, e, n_tok), lambda i: (i, 0, 0))
    return pl.pallas_call(
        functools.partial(_route_kernel, cap=cap),
        grid=(b,),
        in_specs=[spec],
        out_specs=[spec, spec, pl.BlockSpec((1, e, LANES), lambda i: (i, 0, 0))],
        out_shape=[jax.ShapeDtypeStruct((b, e, n_tok), I32), jax.ShapeDtypeStruct((b, e, n_tok), F32),
                   jax.ShapeDtypeStruct((b, e, LANES), I32)],
        scratch_shapes=[pltpu.VMEM((n_tok, n_tok), BF16)],
        compiler_params=_params(("arbitrary",)),
        name="ec_route",
    )(lg_t)


def _slot_windows(cnt, cap, n_chunk):
    start = cnt[:, :, :n_chunk]
    end = cnt[:, :, 1:n_chunk + 1]
    a = jnp.minimum((start // 16) * 16, cap - MOE_WIN)
    fits = jnp.all(end <= a + MOE_WIN, axis=1)
    return jnp.swapaxes(a, 1, 2).reshape(-1).astype(I32), fits.reshape(-1).astype(I32)


def _gather_kernel(win_ref, fit_ref, slot_ref, u_ref, o_ref, *, group):
    b, kc = pl.program_id(0), pl.program_id(1)
    n_e, cap = o_ref.shape[0], o_ref.shape[1]
    chunk = u_ref.shape[0]
    step = b * pl.num_programs(1) + kc

    @pl.when(kc == 0)
    def _():
        o_ref[...] = jnp.zeros(o_ref.shape, o_ref.dtype)

    def place(rows, starts, group):
        row = lax.broadcasted_iota(I32, (rows, chunk), 0)
        for g0 in range(0, n_e, group):
            onehot = jnp.concatenate(
                [jnp.where(row + starts[e] == slot_ref[0, e:e + 1, :], 1.0, 0.0).astype(BF16)
                 for e in range(g0, g0 + group)], axis=0)
            got = _dot(onehot, u_ref[...]).astype(o_ref.dtype)
            for i, e in enumerate(range(g0, g0 + group)):
                o_ref[e, pl.ds(starts[e], rows), :] += got[i * rows:(i + 1) * rows]

    @pl.when(fit_ref[step] != 0)
    def _():
        place(MOE_WIN, [pl.multiple_of(win_ref[step * n_e + e], 16) for e in range(n_e)], group)

    @pl.when(fit_ref[step] == 0)
    def _():
        place(cap, [0] * n_e, max(1, group * MOE_WIN // cap))


def _gather_call(win, fit, slot, u2, cap, group=512 // MOE_WIN):
    b, e, n_tok = slot.shape
    d = u2.shape[1]
    nc = n_tok // MOE_CHUNK
    grid_spec = pltpu.PrefetchScalarGridSpec(
        num_scalar_prefetch=2,
        grid=(b, nc),
        in_specs=[pl.BlockSpec((1, e, MOE_CHUNK), lambda i, k, w, f: (i, 0, k)),
                  pl.BlockSpec((MOE_CHUNK, d), lambda i, k, w, f: (i * nc + k, 0))],
        out_specs=pl.BlockSpec((e, cap, d), lambda i, k, w, f: (0, i, 0)),
    )
    return pl.pallas_call(
        functools.partial(_gather_kernel, group=group),
        grid_spec=grid_spec,
        out_shape=jax.ShapeDtypeStruct((e, b * cap, d), BF16),
        compiler_params=_params(("arbitrary", "arbitrary")),
        name="moe_gather",
    )(win, fit, slot, u2)


def _expert_hidden_kernel(x_ref, w1_ref, w3_ref, o_ref, *, ff):
    tf = w1_ref.shape[3]
    last = pl.num_programs(1) - 1

    def block(width):
        w = jnp.concatenate([w1_ref[0, 0, :, :width].astype(BF16), w3_ref[0, 0, :, :width].astype(BF16)], axis=1)
        h = _dot(x_ref[0], w)
        h1 = h[:, :width]
        o_ref[0, :, :width] = ((h1 / (1.0 + jnp.exp(-h1))) * h[:, width:]).astype(o_ref.dtype)

    tail = ff - (pl.cdiv(ff, tf) - 1) * tf
    if tail == tf:
        block(tf)
    else:
        pl.when(pl.program_id(1) < last)(lambda: block(tf))
        pl.when(pl.program_id(1) == last)(lambda: block(tail))


def _expert_out_kernel(hd_ref, w2_ref, o_ref):
    y = _dot(hd_ref[0], w2_ref[0, 0].astype(BF16)).astype(o_ref.dtype)
    cap = o_ref.shape[1]
    for b in range(o_ref.shape[0]):
        o_ref[b] = y[b * cap:(b + 1) * cap]


def _expert_call(xe, w1, w3, w2, layer, batch, tf=768, tn=1024):
    e, m, d = xe.shape
    cap = m // batch
    ff = w1.shape[3]
    hidden = pl.pallas_call(
        functools.partial(_expert_hidden_kernel, ff=ff),
        grid=(e, pl.cdiv(ff, tf)),
        in_specs=[pl.BlockSpec((1, m, d), lambda i, f: (i, 0, 0)),
                  pl.BlockSpec((1, 1, d, tf), lambda i, f: (layer, i, 0, f)),
                  pl.BlockSpec((1, 1, d, tf), lambda i, f: (layer, i, 0, f))],
        out_specs=pl.BlockSpec((1, m, tf), lambda i, f: (i, 0, f)),
        out_shape=jax.ShapeDtypeStruct((e, m, ff), BF16),
        compiler_params=_params(("arbitrary", "arbitrary")),
        name="moe_expert_hidden",
    )(xe, w1, w3)
    return pl.pallas_call(
        _expert_out_kernel,
        grid=(e, d // tn),
        in_specs=[pl.BlockSpec((1, m, ff), lambda i, j: (i, 0, 0)),
                  pl.BlockSpec((1, 1, ff, tn), lambda i, j: (layer, i, 0, j))],
        out_specs=pl.BlockSpec((batch, cap, tn), lambda i, j: (0, i, j)),
        out_shape=jax.ShapeDtypeStruct((batch, e * cap, d), BF16),
        compiler_params=_params(("arbitrary", "arbitrary")),
        name="moe_expert_out",
    )(hidden, w2)


def _scatter_kernel(win_ref, fit_ref, slot_ref, aff_ref, ye_ref, h_ref, mod_ref, ln_ref, *rest, with_next):
    if with_next:
        nmod_ref, h2_ref, un_ref, pt_ref, yew_ref, moe_ref = rest
    else:
        h2_ref, pt_ref, yew_ref, moe_ref = rest
    n_e = slot_ref.shape[1]
    cap = ye_ref.shape[1] // n_e
    tt = h_ref.shape[0]
    step = pl.program_id(0) * pl.num_programs(1) + pl.program_id(1)

    def gates(rows, e, start):
        row = lax.broadcasted_iota(I32, (rows, tt), 0)
        return jnp.where(row + start == slot_ref[0, e:e + 1, :], aff_ref[0, e:e + 1, :], 0.0).astype(BF16)

    @pl.when(fit_ref[step] != 0)
    def _():
        for e in range(n_e):
            start = pl.multiple_of(win_ref[step * n_e + e], 16)
            pt_ref[e * MOE_WIN:(e + 1) * MOE_WIN, :] = gates(MOE_WIN, e, start)
            yew_ref[e * MOE_WIN:(e + 1) * MOE_WIN, :] = ye_ref[0, pl.ds(e * cap + start, MOE_WIN), :]
        moe_ref[...] = _dot_tn(pt_ref[:n_e * MOE_WIN, :], yew_ref[...])

    @pl.when(fit_ref[step] == 0)
    def _():
        for e in range(n_e):
            pt_ref[e * cap:(e + 1) * cap, :] = gates(cap, e, 0)
        moe_ref[...] = _dot_tn(pt_ref[...], ye_ref[0])

    x = DEEPNORM_ALPHA * h_ref[...] + mod_ref[0] * moe_ref[...]
    hn = _layer_norm(x, ln_ref[0:1, :], ln_ref[1:2, :])
    h2_ref[...] = hn
    if with_next:
        un_ref[...] = (hn * (1.0 + nmod_ref[0, 1:2, :]) + nmod_ref[0, 0:1, :]).astype(un_ref.dtype)


def _scatter_call(win, fit, slot, aff, ye, h1, gate, ln, next_ss):
    b, e, n_tok = slot.shape
    m, d = h1.shape
    tt = MOE_CHUNK
    nt = n_tok // tt
    rows = ye.shape[1]
    with_next = next_ss is not None
    tok_spec = pl.BlockSpec((1, e, tt), lambda i, t, w, f: (i, 0, t))
    row_spec = pl.BlockSpec((tt, d), lambda i, t, w, f: (i * nt + t, 0))
    in_specs = [tok_spec, tok_spec,
                pl.BlockSpec((1, rows, d), lambda i, t, w, f: (i, 0, 0), pipeline_mode=pl.Buffered(1)),
                row_spec,
                pl.BlockSpec((1, 1, d), lambda i, t, w, f: (i, 0, 0)),
                pl.BlockSpec((2, d), lambda i, t, w, f: (0, 0))]
    args = [slot, aff, ye, h1, gate, ln]
    out_specs = [row_spec]
    out_shape = [jax.ShapeDtypeStruct((m, d), F32)]
    if with_next:
        in_specs.append(pl.BlockSpec((1, 2, d), lambda i, t, w, f: (i, 0, 0)))
        args.append(next_ss)
        out_specs.append(row_spec)
        out_shape.append(jax.ShapeDtypeStruct((m, d), BF16))
    grid_spec = pltpu.PrefetchScalarGridSpec(
        num_scalar_prefetch=2,
        grid=(b, nt),
        in_specs=in_specs,
        out_specs=out_specs,
        scratch_shapes=[pltpu.VMEM((rows, tt), BF16),
                        pltpu.VMEM((e * MOE_WIN, d), BF16),
                        pltpu.VMEM((tt, d), F32)],
    )
    res = pl.pallas_call(
        functools.partial(_scatter_kernel, with_next=with_next),
        grid_spec=grid_spec,
        out_shape=out_shape,
        compiler_params=_params(("arbitrary", "arbitrary")),
        name="moe_combine_postnorm",
    )(win, fit, *args)
    return res if with_next else (res[0], None)


def _ec_moe(h1, u2, logits, w1, w3, w2, layer, gate, ln, next_ss, batch, n_tok):
    cap = EC_CAPACITY_FACTOR * n_tok // N_EXPERTS
    lg_t = jnp.swapaxes(logits[:, :N_EXPERTS].reshape(batch, n_tok, N_EXPERTS), 1, 2)
    slot, aff, cnt = _route_call(lg_t, cap)
    win, fit = _slot_windows(cnt, cap, n_tok // MOE_CHUNK)
    xe = _gather_call(win, fit, slot, u2, cap)
    ye = _expert_call(xe, w1, w3, w2, layer, batch)
    return _scatter_call(win, fit, slot, aff, ye, h1, gate, ln, next_ss)


def _mm_kernel(a_ref, w_ref, o_ref, wb_ref):
    @pl.when(pl.program_id(1) == 0)
    def _():
        wb_ref[...] = w_ref[0].astype(wb_ref.dtype)

    o_ref[...] = _dot(a_ref[...], wb_ref[...]).astype(o_ref.dtype)


def _mm_call(a, w, layer, out_dtype, tm=1024, tn=1024):
    m, k = a.shape
    n = w.shape[2]
    return pl.pallas_call(
        _mm_kernel,
        grid=(n // tn, m // tm),
        in_specs=[pl.BlockSpec((tm, k), lambda j, i: (i, 0)),
                  pl.BlockSpec((1, k, tn), lambda j, i: (layer, 0, j))],
        out_specs=pl.BlockSpec((tm, tn), lambda j, i: (i, j)),
        out_shape=jax.ShapeDtypeStruct((m, n), out_dtype),
        scratch_shapes=[pltpu.VMEM((k, tn), BF16)],
        compiler_params=_params(("arbitrary", "arbitrary")),
        name="matmul",
    )(a, w)


def _short_conv(p, w, b):
    x = p.astype(F32)
    n = x.shape[0]
    r = lax.broadcasted_iota(I32, x.shape, 0)
    prev = jnp.where(r == 0, 0.0, pltpu.roll(x, 1, 0))
    nxt = jnp.where(r == n - 1, 0.0, pltpu.roll(x, n - 1, 0))
    return b + w[0:1, :] * prev + w[1:2, :] * x + w[2:3, :] * nxt


def _sconv_kernel(p_ref, w_ref, b_ref, o_ref):
    o_ref[...] = _short_conv(p_ref[...], w_ref[...], b_ref[...]).astype(o_ref.dtype)


def _sconv_call(p, conv_w, conv_b, col0, ncols, out_dtype, batch, n_tok, tc=512):
    c0 = col0 // tc
    return pl.pallas_call(
        _sconv_kernel,
        grid=(batch, ncols // tc),
        in_specs=[pl.BlockSpec((n_tok, tc), lambda b, j: (b, c0 + j)),
                  pl.BlockSpec((HY_SHORT, tc), lambda b, j: (0, c0 + j)),
                  pl.BlockSpec((1, tc), lambda b, j: (0, c0 + j))],
        out_specs=pl.BlockSpec((n_tok, tc), lambda b, j: (b, j)),
        out_shape=jax.ShapeDtypeStruct((batch * n_tok, ncols), out_dtype),
        compiler_params=_params(("arbitrary", "arbitrary")),
        name="hyena_short_conv",
    )(p, conv_w, conv_b.reshape(1, -1))


def _dft_fwd_kernel(fc_ref, fs_ref, r1_ref, r2_ref, *rest, spectral):
    xr = _dot(fc_ref[...], r1_ref[...])
    xi = _dot(fs_ref[...], r2_ref[...])
    if spectral:
        h_ref, nyq_ref, o_ref = rest
        hr = h_ref[0, 0]
        hi = h_ref[0, 1]
        dc = lax.broadcasted_iota(I32, xr.shape, 0) == 0
        yr = xr * hr - jnp.where(dc, 0.0, xi * hi)
        yi = jnp.where(dc, xi * nyq_ref[...], xr * hi + xi * hr)
    else:
        (o_ref,) = rest
        yr, yi = xr, xi
    o_ref[0, 0] = yr.astype(o_ref.dtype)
    o_ref[0, 1] = yi.astype(o_ref.dtype)


def _resident_spec(shape):
    return pl.BlockSpec(shape, lambda *_: (0,) * len(shape), pipeline_mode=pl.Buffered(1))


def _dft_fwd_call(fc, fs, r1, r2, spec, out_dtype, batch, n_tok, tn=256):
    ncols = r1.shape[1]
    r_spec = pl.BlockSpec((n_tok, tn), lambda j, b: (b, j))
    in_specs = [_resident_spec(fc.shape), _resident_spec(fs.shape), r_spec, r_spec]
    args = [fc, fs, r1, r2]
    if spec is not None:
        h, nyq, hcol0 = spec
        c0 = hcol0 // tn
        in_specs += [pl.BlockSpec((1, 2, n_tok, tn), lambda j, b: (0, 0, 0, c0 + j)),
                     pl.BlockSpec((1, tn), lambda j, b: (0, c0 + j))]
        args += [h, nyq]
    return pl.pallas_call(
        functools.partial(_dft_fwd_kernel, spectral=spec is not None),
        grid=(ncols // tn, batch),
        in_specs=in_specs,
        out_specs=pl.BlockSpec((1, 2, n_tok, tn), lambda j, b: (b, 0, 0, j)),
        out_shape=jax.ShapeDtypeStruct((batch, 2, n_tok, ncols), out_dtype),
        compiler_params=_params(("arbitrary", "arbitrary")),
        name="dft_forward",
    )(*args)


def _dual_kernel(a1_ref, a2_ref, b1_ref, b2_ref, *rest, hyena, planes):
    b1 = b1_ref[0, 0] if planes else b1_ref[...]
    b2 = b2_ref[0, 0] if planes else b2_ref[...]
    y = _dot(a1_ref[...], b1) + _dot(a2_ref[...], b2)
    if hyena:
        p_ref, cw_ref, cb_ref, z_ref, skip_ref, o_ref = rest
        gate = _short_conv(p_ref[...], cw_ref[...], cb_ref[...])
        y = gate * (y + skip_ref[...] * z_ref[...].astype(F32))
    else:
        (o_ref,) = rest
    o_ref[...] = y.astype(o_ref.dtype)


def _dual_call(a1, a2, b1, b2, hy, batch, n_tok, ncols, tn=256):
    planes = b1.ndim == 4
    if planes:
        b_specs = [pl.BlockSpec((1, 1, n_tok, tn), lambda b, j: (b, 0, 0, j)),
                   pl.BlockSpec((1, 1, n_tok, tn), lambda b, j: (b, 1, 0, j))]
    else:
        b_specs = [pl.BlockSpec((n_tok, tn), lambda b, j: (b, j))] * 2
    in_specs = [_resident_spec(a1.shape), _resident_spec(a2.shape)] + b_specs
    args = [a1, a2, b1, b2]
    o_spec = pl.BlockSpec((n_tok, tn), lambda b, j: (b, j))
    if hy is not None:
        p, conv_w, conv_b, pcol0, z, skip = hy
        c0 = pcol0 // tn
        in_specs += [pl.BlockSpec((n_tok, tn), lambda b, j: (b, c0 + j)),
                     pl.BlockSpec((HY_SHORT, tn), lambda b, j: (0, c0 + j)),
                     pl.BlockSpec((1, tn), lambda b, j: (0, c0 + j)),
                     o_spec,
                     pl.BlockSpec((1, tn), lambda b, j: (0, j))]
        args += [p, conv_w, conv_b.reshape(1, -1), z, skip]
    return pl.pallas_call(
        functools.partial(_dual_kernel, hyena=hy is not None, planes=planes),
        grid=(batch, ncols // tn),
        in_specs=in_specs,
        out_specs=o_spec,
        out_shape=jax.ShapeDtypeStruct((batch * n_tok, ncols), BF16),
        compiler_params=_params(("arbitrary", "arbitrary")),
        name="seq_mix_matmul",
    )(*args)


def _fnet_kernel(cl_ref, msl_ref, x_ref, cw_ref, o_ref):
    x = x_ref[...]
    seq = jnp.concatenate([_dot(cl_ref[...], x).astype(BF16), _dot(msl_ref[...], x).astype(BF16)], axis=1)
    o_ref[...] = _dot(seq, cw_ref[...]).astype(o_ref.dtype)


def _fnet_call(cl, msl, p, cw, col0, batch, n_tok):
    gw = cw.shape[1]
    c0 = col0 // gw
    return pl.pallas_call(
        _fnet_kernel,
        grid=(batch, FN_GROUPS),
        in_specs=[_resident_spec(cl.shape), _resident_spec(msl.shape),
                  pl.BlockSpec((n_tok, gw), lambda b, g: (b, c0 + g)),
                  pl.BlockSpec(cw.shape, lambda b, g: (0, 0))],
        out_specs=pl.BlockSpec((n_tok, gw), lambda b, g: (b, g)),
        out_shape=jax.ShapeDtypeStruct((batch * n_tok, FN_GROUPS * gw), BF16),
        compiler_params=_params(("arbitrary", "arbitrary")),
        name="fnet_mix",
    )(cl, msl, p, cw)


def _cos_sin_matrix(n_rows, n_cols, period, split=64):
    r = np.arange(n_rows, dtype=np.int64)[:, None]
    c_hi = (np.arange(n_cols // split, dtype=np.int64) * split)[None, :]
    c_lo = np.arange(split, dtype=np.int64)[None, :]
    ang = lambda c: 2.0 * np.pi * ((r * c) % period).astype(np.float64) / period
    ca, sa = jnp.asarray(np.cos(ang(c_hi)), F32), jnp.asarray(np.sin(ang(c_hi)), F32)
    cb, sb = jnp.asarray(np.cos(ang(c_lo)), F32), jnp.asarray(np.sin(ang(c_lo)), F32)
    cos = ca[:, :, None] * cb[:, None, :] - sa[:, :, None] * sb[:, None, :]
    sin = sa[:, :, None] * cb[:, None, :] + ca[:, :, None] * sb[:, None, :]
    return cos.reshape(n_rows, n_cols), sin.reshape(n_rows, n_cols)


def _trig_kernel(ta_ref, tb_ref, ea_ref, eb_ref, *o_refs, mode, period, scale):
    sa_ca = _dot(ta_ref[...], ea_ref[...])
    sb_cb = _dot(tb_ref[...], eb_ref[...])
    nc = sa_ca.shape[1] // 2
    ca, sa, cb, sb = sa_ca[:, :nc], sa_ca[:, nc:], sb_cb[:, :nc], sb_cb[:, nc:]
    cos = ca * cb - sa * sb
    sin = sa * cb + ca * sb
    tm = cos.shape[0]
    row = pl.program_id(0) * tm + lax.broadcasted_iota(I32, cos.shape, 0)
    col = lax.broadcasted_iota(I32, cos.shape, 1)
    alt = lambda idx: (1 - 2 * jnp.bitwise_and(idx, 1)).astype(F32)
    if mode == "hyena":
        fc_ref, fs_ref, gc_ref, gs_ref = o_refs
        fc_ref[...] = cos.astype(fc_ref.dtype)
        fs_ref[...] = jnp.where(row == 0, alt(col), -sin).astype(fs_ref.dtype)
        gc_ref[...] = (cos * jnp.where(col == 0, 1.0 / period, 2.0 / period)).astype(gc_ref.dtype)
        gs_ref[...] = jnp.where(col == 0, alt(row) * (1.0 / period), sin * (-2.0 / period)).astype(gs_ref.dtype)
    else:
        c_ref, ms_ref = o_refs
        c_ref[...] = (cos * scale).astype(c_ref.dtype)
        ms_ref[...] = (sin * (-scale)).astype(ms_ref.dtype)


def _split3(x):
    bf = jnp.bfloat16
    x = np.asarray(x, np.float32)
    hi = x.astype(bf)
    r1 = x - hi.astype(np.float32)
    mid = r1.astype(bf)
    lo = (r1 - mid.astype(np.float32)).astype(bf)
    return [hi, mid, lo]


def _trig_call(n, period, mode, scale=1.0, split=64, tm=256):
    r = np.arange(n, dtype=np.int64)[:, None]
    c_hi = (np.arange(n // split, dtype=np.int64) * split)[None, :]
    c_lo = np.arange(split, dtype=np.int64)[None, :]
    ang = lambda c: 2.0 * np.pi * ((r * c) % period).astype(np.float64) / period
    def operands(a, col_group):
        terms = _split3(np.cos(a)) + _split3(np.sin(a))
        k = a.shape[1]
        spread = (col_group[None, :] == np.arange(k)[:, None]).astype(np.float32)
        zero = np.zeros_like(spread)
        e = np.concatenate([np.concatenate([spread, zero], 1)] * 3 + [np.concatenate([zero, spread], 1)] * 3, 0)
        return jnp.asarray(np.concatenate(terms, axis=1)), jnp.asarray(e, BF16)
    cols = np.arange(n)
    ta, ea = operands(ang(c_hi), cols // split)
    tb, eb = operands(ang(c_lo), cols % split)
    n_out = 4 if mode == "hyena" else 2
    o_spec = pl.BlockSpec((tm, n), lambda i: (i, 0))
    return pl.pallas_call(
        functools.partial(_trig_kernel, mode=mode, period=period, scale=scale),
        grid=(n // tm,),
        in_specs=[pl.BlockSpec((tm, ta.shape[1]), lambda i: (i, 0)),
                  pl.BlockSpec((tm, tb.shape[1]), lambda i: (i, 0)),
                  pl.BlockSpec(ea.shape, lambda i: (0, 0)),
                  pl.BlockSpec(eb.shape, lambda i: (0, 0))],
        out_specs=[o_spec] * n_out,
        out_shape=[jax.ShapeDtypeStruct((n, n), BF16)] * n_out,
        compiler_params=_params(("arbitrary",)),
        name="dft_tables",
    )(ta, tb, ea, eb)


def _hyena_dft_operands(n_tok):
    return _trig_call(n_tok, 2 * n_tok, "hyena")


def _fnet_operands(n_tok, gw):
    cl, msl = _trig_call(n_tok, n_tok, "fnet", scale=1.0 / math.sqrt(n_tok * gw))
    cw, sw = _cos_sin_matrix(gw, gw, gw)
    return cl, msl, jnp.concatenate([cw, sw], axis=0).astype(BF16)


def _filter_kernel(z_ref, w1_ref, b1_ref, f1_ref, w2_ref, b2_ref, f2_ref, w3f_ref, w3b_ref, dl_ref,
                   sum_ref, dif_ref, nyq_ref, hdn_ref):
    @pl.when(jnp.logical_and(pl.program_id(0) == 0, pl.program_id(1) == 0))
    def _():
        hdot = lambda a, b: jnp.dot(a, b, precision=lax.Precision.HIGHEST, preferred_element_type=F32)
        hdn = jnp.sin(f1_ref[...] * (hdot(z_ref[...], w1_ref[...]) + b1_ref[...]))
        hdn = jnp.sin(f2_ref[...] * (hdot(hdn, w2_ref[...]) + b2_ref[...]))
        hdn_ref[...] = hdn.astype(hdn_ref.dtype)

    n = hdn_ref.shape[0]
    row = lax.broadcasted_iota(I32, (n, dl_ref.shape[1]), 0)
    decay = jnp.exp(-(row.astype(F32) / (n - 1)) * dl_ref[...])
    hf = _dot(hdn_ref[...], w3f_ref[...].astype(BF16)) * decay
    hb = jnp.where(row == 0, 0.0, _dot(hdn_ref[...], w3b_ref[...].astype(BF16)) * decay)
    tot = hf + hb
    sum_ref[...] = tot.astype(sum_ref.dtype)
    dif_ref[...] = (hf - hb).astype(dif_ref.dtype)
    nyq_ref[...] = jnp.sum(jnp.where(jnp.bitwise_and(row, 1) == 0, tot, -tot), axis=0, keepdims=True)


def _filter_call(n_tok, fw1, fb1, ff1, fw2, fb2, ff2, fw3, tc=512):
    t01 = np.linspace(0.0, 1.0, n_tok, dtype=np.float32)
    w = (2.0 * math.pi * np.arange(n_tok, dtype=np.float32) / n_tok).astype(np.float32)
    bands = np.linspace(1e-4, HY_BANDS - 1, HY_BANDS, dtype=np.float32)
    z = np.concatenate([t01[:, None], np.cos(w[:, None] * bands), -np.sin(w[:, None] * bands)], -1)
    deltas = np.abs(np.linspace(HY_MIN_DECAY, HY_MAX_DECAY, HY_W, dtype=np.float32))[None, :]
    emb, hid = fw1.shape
    nc = HY_W // tc
    full = lambda shape: pl.BlockSpec(shape, lambda o, j: (0,) * len(shape))
    o_spec = pl.BlockSpec((n_tok, tc), lambda o, j: (0, o * nc + j))
    return pl.pallas_call(
        _filter_kernel,
        grid=(HY_ORDER, nc),
        in_specs=[full((n_tok, emb)), full((emb, hid)), full((1, hid)), full((1, hid)),
                  full((hid, hid)), full((1, hid)), full((1, hid)),
                  pl.BlockSpec((hid, tc), lambda o, j: (0, (2 * o) * nc + j)),
                  pl.BlockSpec((hid, tc), lambda o, j: (0, (2 * o + 1) * nc + j)),
                  pl.BlockSpec((1, tc), lambda o, j: (0, j))],
        out_specs=[o_spec, o_spec, pl.BlockSpec((1, tc), lambda o, j: (0, o * nc + j))],
        out_shape=[jax.ShapeDtypeStruct((n_tok, HY_ORDER * HY_W), BF16),
                   jax.ShapeDtypeStruct((n_tok, HY_ORDER * HY_W), BF16),
                   jax.ShapeDtypeStruct((1, HY_ORDER * HY_W), F32)],
        scratch_shapes=[pltpu.VMEM((n_tok, hid), BF16)],
        compiler_params=_params(("arbitrary", "arbitrary")),
        name="hyena_filters",
    )(jnp.asarray(z.astype(np.float32)), fw1, fb1.reshape(1, hid), ff1.reshape(1, hid),
      fw2, fb2.reshape(1, hid), ff2.reshape(1, hid), fw3, fw3, jnp.asarray(deltas))


def kernel(x, c, ctx, c_ctx, ada_w, ada_b, ln1_g, ln1_b, ln2_g, ln2_b, router, exp_w1, exp_w3, exp_w2,
           ab_w_in, ab_kv_norm, ab_w_ukv, ab_rpb, ab_w_out,
           cd_w_in, cd_conv_w, cd_conv_b, cd_filt_w1, cd_filt_b1, cd_filt_freq1, cd_filt_w2, cd_filt_b2,
           cd_filt_freq2, cd_filt_w3, cd_skip, cd_w_out):
    batch, n_tok, d = x.shape
    lc = ctx.shape[1]
    x2d = x.reshape(batch * n_tok, d)
    ctx2d = ctx.reshape(batch * lc, d)

    pad_rows = (-(batch + 1)) % 8
    cv = jnp.concatenate([c, c_ctx[None, :], jnp.zeros((pad_rows, d), F32)], axis=0)
    ada = _ada_call(cv, ada_w, ada_b)
    mods = [ada[i, :batch].reshape(batch, 6, d) for i in range(DEPTH)]
    ctx_ss = ada[0, batch, :2 * d].reshape(1, 2, d)

    w_in = _w_in_prep_call(jnp.swapaxes(ab_w_in, 1, 2), 0)
    p_lat = _mm_rope_call(x2d, mods[0][:, 0:2], n_tok, w_in, _rope_tables(n_tok), tm=1024)
    p_ctx = _mm_rope_call(ctx2d, ctx_ss, batch * lc, w_in, _identity_rope_tables(batch * lc), tm=batch * lc,
                          tile0=CTX_TILE0)
    w_ukv = _permute_w_ukv(ab_w_ukv[0])
    kv_lat = _kvup_call(p_lat, ab_kv_norm[0], w_ukv, tm=512)
    kv_ctx = _kvup_call(p_ctx, ab_kv_norm[0], w_ukv, tm=lc, tile0=CTX_TILE0)
    a_out = _mla_call(p_lat, kv_lat, p_ctx, kv_ctx, batch, n_tok, lc)
    var_map, na_er, na_ec = _na_tables(n_tok // GRID_W)
    b_out = _na_call(p_lat, p_ctx, _na_bias(ab_rpb[0], na_er, na_ec), jnp.asarray(var_map),
                     batch, n_tok, lc)
    ln1 = jnp.stack([ln1_g, ln1_b], axis=1)
    ln2 = jnp.stack([ln2_g, ln2_b], axis=1)
    h1, u2, logits = _outln_call(a_out, b_out, ab_w_out, 0, x2d,
                                 mods[0][:, 2:5], ln1[0], _router_operand(router[0]), n_tok)
    h, u = _ec_moe(h1, u2, logits, exp_w1, exp_w3, exp_w2, 0, mods[0][:, 5:6], ln2[0],
                   mods[1][:, 0:2], batch, n_tok)

    p1 = _mm_call(u, cd_w_in, 0, BF16)
    s0 = _sconv_call(p1, cd_conv_w[0], cd_conv_b[0], 0, HY_W, BF16, batch, n_tok)
    h_sum, h_dif, h_nyq = _filter_call(n_tok, cd_filt_w1[0], cd_filt_b1[0], cd_filt_freq1[0], cd_filt_w2[0],
                                       cd_filt_b2[0], cd_filt_freq2[0], cd_filt_w3[0])
    fc, fs, gc, gs = _hyena_dft_operands(n_tok)
    h_spec = _dft_fwd_call(fc, fs, h_sum, h_dif, None, F32, 1, n_tok)
    z = s0
    for o in range(HY_ORDER):
        y_spec = _dft_fwd_call(fc, fs, z, z, (h_spec, h_nyq, o * HY_W), BF16, batch, n_tok)
        z = _dual_call(gc, gs, y_spec, y_spec,
                       (p1, cd_conv_w[0], cd_conv_b[0], (o + 1) * HY_W, z, cd_skip[0][o:o + 1]),
                       batch, n_tok, HY_W)
    cl, msl, cw = _fnet_operands(n_tok, FN_GROUP_W)
    y_fn = _fnet_call(cl, msl, p1, cw, HY_IN_W, batch, n_tok)
    h1, u2, logits = _outln_call(z, y_fn, cd_w_out, 0, h,
                                 mods[1][:, 2:5], ln1[1], _router_operand(router[1]), n_tok)
    h, _ = _ec_moe(h1, u2, logits, exp_w1, exp_w3, exp_w2, 1, mods[1][:, 5:6], ln2[1],
                   None, batch, n_tok)
    return h.reshape(batch, n_tok, d)
```

```python
import functools
import math

import numpy as np
import jax
import jax.numpy as jnp
from jax import lax
from jax.experimental import pallas as pl
from jax.experimental.pallas import tpu as pltpu

F32 = jnp.float32
BF16 = jnp.bfloat16
I32 = jnp.int32

D_MODEL = 2048
BATCH = 4
SEQ = 2048
DEPTH = 2
CTX_LEN = 256
GRID_W = 64
DEEPNORM_ALPHA = (2.0 * DEPTH) ** 0.25
LN_EPS = 1e-6
NEG_INF = -1e30
LOG2E = math.log2(math.e)

MLA_HEADS = 8
MLA_NOPE = 128
MLA_ROPE = 64
MLA_QK = MLA_NOPE + MLA_ROPE
MLA_V = 128
MLA_KV_RANK = 512
ROPE_THETA = 10000.0

NA_HEADS = 8
NA_HEAD_DIM = 128
NA_KH = 8
NA_KW = 16

OFF_CKV = MLA_HEADS * MLA_QK
OFF_KPE = OFF_CKV + MLA_KV_RANK
OFF_QNA = OFF_KPE + MLA_ROPE
OFF_KNA = OFF_QNA + NA_HEADS * NA_HEAD_DIM
OFF_VNA = OFF_KNA + NA_HEADS * NA_HEAD_DIM

HY_W = 1024
HY_ORDER = 2
HY_IN_W = (HY_ORDER + 1) * HY_W
HY_SHORT = 3
HY_BANDS = 16
HY_DECAY_TARGET = 1e-2
HY_MIN_DECAY = math.log(HY_DECAY_TARGET) / 1.5
HY_MAX_DECAY = math.log(HY_DECAY_TARGET) / 0.3

FN_W = 1024
FN_GROUPS = 4
FN_GROUP_W = FN_W // FN_GROUPS

N_EXPERTS = 16
EC_CAPACITY_FACTOR = 2
EXPERT_FF = 1408

LANES = 128
MXU_DIM = 256
VMEM_LIMIT = 56 * 1024 * 1024

QTILE_W = 2 * LANES
T_KPE = MLA_HEADS
T_CKV = T_KPE + 2
T_QNA = T_CKV + MLA_KV_RANK // QTILE_W
T_KNA = T_QNA + NA_HEADS * NA_HEAD_DIM // QTILE_W
T_VNA = T_KNA + NA_HEADS * NA_HEAD_DIM // QTILE_W
N_ABTILES = T_VNA + NA_HEADS * NA_HEAD_DIM // QTILE_W
AB_PERM_W = N_ABTILES * QTILE_W
CTX_TILE0 = T_KPE
INPROJ_TILES_PER_STEP = 8

MLA_SUBTILES = 8
OUTLN_SUBTILES = 2
MOE_CHUNK = 256
MOE_WIN = 64
GATHER_CHUNK = 256
GATHER_WIN = 64
NA_G = 4
NA_WR = NA_KH + NA_G - 1
NA_HEADS_PER_STEP = 2


def _params(sem, vmem=VMEM_LIMIT):
    return pltpu.CompilerParams(dimension_semantics=sem, vmem_limit_bytes=vmem)


def _dot(a, b):
    return jnp.dot(a, b, preferred_element_type=F32)


def _dot_nt(a, b):
    return lax.dot_general(a, b, (((1,), (1,)), ((), ())), preferred_element_type=F32)


def _dot_tn(a, b):
    return lax.dot_general(a, b, (((0,), (0,)), ((), ())), preferred_element_type=F32)


def _ada_kernel(c_ref, w_ref, b_ref, o_ref):
    c = c_ref[...]
    a = c / (1.0 + jnp.exp(-c))
    o_ref[0] = _dot(a.astype(BF16), w_ref[0].astype(BF16)) + b_ref[0]


def _ada_call(cv, ada_w, ada_b, tn=2048):
    depth, d, n = ada_w.shape
    rows = cv.shape[0]
    return pl.pallas_call(
        _ada_kernel,
        grid=(depth, n // tn),
        in_specs=[pl.BlockSpec((rows, d), lambda l, j: (0, 0)),
                  pl.BlockSpec((1, d, tn), lambda l, j: (l, 0, j)),
                  pl.BlockSpec((1, 1, tn), lambda l, j: (l, 0, j))],
        out_specs=pl.BlockSpec((1, rows, tn), lambda l, j: (l, 0, j)),
        out_shape=jax.ShapeDtypeStruct((depth, rows, n), F32),
        compiler_params=_params(("arbitrary", "arbitrary")),
        name="ada_params",
    )(cv, ada_w, ada_b.reshape(depth, 1, n))


def _mm_rope_kernel(x_ref, ss_ref, w_ref, ta_ref, tb_ref, tc_ref, o_ref, a_ref, *, tile0):
    @pl.when(pl.program_id(1) == 0)
    def _():
        a_ref[...] = (x_ref[...] * (1.0 + ss_ref[0, 1:2, :]) + ss_ref[0, 0:1, :]).astype(a_ref.dtype)

    a = a_ref[...]
    for q in range(INPROJ_TILES_PER_STEP):
        t = tile0 + INPROJ_TILES_PER_STEP * pl.program_id(1) + q
        c0 = q * QTILE_W
        acc = _dot_nt(a, w_ref[c0:c0 + QTILE_W, :])
        is_rope = t <= T_KPE
        f = jnp.where(t < T_KPE, LOG2E * MLA_QK ** -0.5,
                      jnp.where(jnp.logical_and(t >= T_QNA, t < T_KNA), LOG2E * NA_HEAD_DIM ** -0.5, 1.0)
                      ).astype(F32)
        hi = acc[:, LANES:]
        rot = (hi * jnp.where(is_rope, ta_ref[...], 1.0)
               + pltpu.roll(hi, LANES - MLA_ROPE // 2, 1) * jnp.where(is_rope, tb_ref[...], 0.0)
               + pltpu.roll(hi, MLA_ROPE // 2, 1) * jnp.where(is_rope, tc_ref[...], 0.0))
        o_ref[:, c0:c0 + LANES] = (acc[:, :LANES] * f).astype(o_ref.dtype)
        o_ref[:, c0 + LANES:c0 + QTILE_W] = (rot * f).astype(o_ref.dtype)


def _mm_rope_call(x, ss, rows_per_group, w, tabs, tm, tile0=0):
    m, k = x.shape
    tn = INPROJ_TILES_PER_STEP * QTILE_W
    n = w.shape[0] - tile0 * QTILE_W
    j0 = tile0 // INPROJ_TILES_PER_STEP
    ntab = tabs[0].shape[0] // tm
    tpg = rows_per_group // tm
    tab_spec = pl.BlockSpec((tm, LANES), lambda i, j: (i % ntab, 0))
    return pl.pallas_call(
        functools.partial(_mm_rope_kernel, tile0=tile0),
        grid=(m // tm, n // tn),
        in_specs=[pl.BlockSpec((tm, k), lambda i, j: (i, 0)),
                  pl.BlockSpec((1, 2, k), lambda i, j: (i // tpg, 0, 0)),
                  pl.BlockSpec((tn, k), lambda i, j: (j0 + j, 0)),
                  tab_spec, tab_spec, tab_spec],
        out_specs=pl.BlockSpec((tm, tn), lambda i, j: (i, j)),
        out_shape=jax.ShapeDtypeStruct((m, n), BF16),
        scratch_shapes=[pltpu.VMEM((tm, k), BF16)],
        compiler_params=_params(("arbitrary", "arbitrary")),
        name="attn_in_proj",
    )(x, ss, w, *tabs)


def _rope_tables(n_tok):
    t = np.arange(n_tok)
    row = (t // GRID_W).astype(np.float32)
    col = (t % GRID_W).astype(np.float32)
    n_freq = MLA_ROPE // 4
    inv = (ROPE_THETA ** (-np.arange(n_freq, dtype=np.float32) / n_freq)).astype(np.float32)
    ang = np.concatenate([row[:, None] * inv, col[:, None] * inv], axis=1)
    cos, sin = np.cos(ang).astype(np.float32), np.sin(ang).astype(np.float32)
    half = MLA_ROPE // 2
    ta = np.zeros((n_tok, LANES), np.float32)
    tb = np.zeros((n_tok, LANES), np.float32)
    tc = np.zeros((n_tok, LANES), np.float32)
    ta[:, :half] = cos
    ta[:, half:2 * half] = cos
    tb[:, :half] = -sin
    tc[:, half:2 * half] = sin
    return jnp.asarray(ta), jnp.asarray(tb), jnp.asarray(tc)


def _identity_rope_tables(n_tok):
    ta = np.zeros((n_tok, LANES), np.float32)
    ta[:, :MLA_ROPE] = 1.0
    z = np.zeros((n_tok, LANES), np.float32)
    return jnp.asarray(ta), jnp.asarray(z), jnp.asarray(z)


def _w_in_prep_kernel(w_ref, o_ref):
    x = w_ref[0]
    zeros = lambda n: jnp.zeros((n, x.shape[1]), x.dtype)
    rows = []
    for h in range(MLA_HEADS):
        rows += [x[h * MLA_QK:(h + 1) * MLA_QK, :], zeros(QTILE_W - MLA_QK)]
    rows += [zeros(LANES), x[OFF_KPE:OFF_QNA, :], zeros(LANES - MLA_ROPE), zeros(QTILE_W),
             x[OFF_CKV:OFF_KPE, :], x[OFF_QNA:, :]]
    o_ref[...] = jnp.concatenate(rows, axis=0).astype(o_ref.dtype)


def _w_in_prep_call(w_in_t, layer, tk=256):
    _, n, d = w_in_t.shape
    return pl.pallas_call(
        _w_in_prep_kernel,
        grid=(d // tk,),
        in_specs=[pl.BlockSpec((1, n, tk), lambda i: (layer, 0, i))],
        out_specs=pl.BlockSpec((AB_PERM_W, tk), lambda i: (0, i)),
        out_shape=jax.ShapeDtypeStruct((AB_PERM_W, d), BF16),
        compiler_params=_params(("arbitrary",)),
        name="attn_w_in_prep",
    )(w_in_t)


def _kvup_kernel(x_ref, g_ref, w_ref, o_ref):
    x = x_ref[...].astype(F32)
    y = x * lax.rsqrt(jnp.mean(x * x, axis=-1, keepdims=True) + LN_EPS) * g_ref[...]
    o_ref[...] = _dot(y.astype(BF16), w_ref[...]).astype(o_ref.dtype)


def _kvup_call(p, g, w, tm, tile0=0):
    m = p.shape[0]
    r, n = w.shape
    return pl.pallas_call(
        _kvup_kernel,
        grid=(m // tm,),
        in_specs=[pl.BlockSpec((tm, r), lambda i: (i, (T_CKV - tile0) * QTILE_W // r)),
                  pl.BlockSpec((1, r), lambda i: (0, 0)),
                  pl.BlockSpec((r, n), lambda i: (0, 0))],
        out_specs=pl.BlockSpec((tm, n), lambda i: (i, 0)),
        out_shape=jax.ShapeDtypeStruct((m, n), BF16),
        compiler_params=_params(("arbitrary",)),
        name="kv_up",
    )(p, g.reshape(1, r), w)


def _permute_w_ukv(w):
    r = w.shape[0]
    w3 = w.reshape(r, MLA_HEADS, MLA_NOPE + MLA_V)
    return jnp.concatenate([w3[:, :, :MLA_NOPE].reshape(r, -1), w3[:, :, MLA_NOPE:].reshape(r, -1)],
                           axis=1).astype(BF16)


def _mla_kernel(q_ref, kn_ref, kp_ref, v_ref, kcn_ref, kcp_ref, vc_ref, o_ref, kcat, vcat):
    s_len = kn_ref.shape[0]
    n_h = kcat.shape[0]
    for h in range(n_h):
        kcat[h, :s_len, :LANES] = kn_ref[:, h * LANES:(h + 1) * LANES]
        kcat[h, :s_len, LANES:] = kp_ref[...]
        kcat[h, s_len:, :LANES] = kcn_ref[:, h * LANES:(h + 1) * LANES]
        kcat[h, s_len:, LANES:] = kcp_ref[...]
        vcat[h, :s_len, :] = v_ref[:, h * LANES:(h + 1) * LANES]
        vcat[h, s_len:, :] = vc_ref[:, h * LANES:(h + 1) * LANES]

    sub = q_ref.shape[0] // MLA_SUBTILES
    for h in range(n_h):
        for r0 in range(0, q_ref.shape[0], sub):
            s = _dot_nt(q_ref[r0:r0 + sub, h * QTILE_W:(h + 1) * QTILE_W], kcat[h])
            m = jnp.max(s, axis=-1, keepdims=True)
            p = jnp.exp2(s - m)
            l = jnp.sum(p, axis=-1, keepdims=True)
            o = _dot(p.astype(BF16), vcat[h])
            o_ref[r0:r0 + sub, h * LANES:(h + 1) * LANES] = (o / l).astype(o_ref.dtype)


def _mla_call(p_lat, kv_lat, p_ctx, kv_ctx, batch, s_len, lc, heads_per_step=2):
    hp = heads_per_step
    kpe_blk = T_KPE * 2 + 1
    v0 = MLA_HEADS // hp
    return pl.pallas_call(
        _mla_kernel,
        grid=(batch, MLA_HEADS // hp),
        in_specs=[pl.BlockSpec((s_len, hp * QTILE_W), lambda b, h: (b, h)),
                  pl.BlockSpec((s_len, hp * LANES), lambda b, h: (b, h)),
                  pl.BlockSpec((s_len, LANES), lambda b, h: (b, kpe_blk)),
                  pl.BlockSpec((s_len, hp * LANES), lambda b, h: (b, v0 + h)),
                  pl.BlockSpec((lc, hp * LANES), lambda b, h: (b, h)),
                  pl.BlockSpec((lc, LANES), lambda b, h: (b, kpe_blk - 2 * CTX_TILE0)),
                  pl.BlockSpec((lc, hp * LANES), lambda b, h: (b, v0 + h))],
        out_specs=pl.BlockSpec((s_len, hp * LANES), lambda b, h: (b, h)),
        out_shape=jax.ShapeDtypeStruct((batch * s_len, MLA_HEADS * MLA_V), BF16),
        scratch_shapes=[pltpu.VMEM((hp, s_len + lc, QTILE_W), BF16),
                        pltpu.VMEM((hp, s_len + lc, LANES), BF16)],
        compiler_params=_params(("arbitrary", "arbitrary")),
        name="mla_attention",
    )(p_lat, kv_lat, p_lat, kv_lat, kv_ctx, p_ctx, kv_ctx)


def _na_kernel(q_ref, k_ref, v_ref, kc_ref, vc_ref, bias_ref, o_ref, *, rows, var_map):
    win = NA_WR * GRID_W
    tq = NA_G * GRID_W
    for h in range(q_ref.shape[1] // LANES):
        hs = slice(h * LANES, (h + 1) * LANES)
        for g in range(rows // NA_G):
            start = int(np.clip(NA_G * g - NA_KH // 2, 0, rows - NA_WR)) * GRID_W
            q = q_ref[g * tq:(g + 1) * tq, hs]
            sw = _dot_nt(q, k_ref[start:start + win, hs]) + bias_ref[var_map[g], h]
            sc = _dot_nt(q, kc_ref[:, hs])
            m = jnp.maximum(jnp.max(sw, axis=-1, keepdims=True), jnp.max(sc, axis=-1, keepdims=True))
            pw = jnp.exp2(sw - m)
            pc = jnp.exp2(sc - m)
            l = jnp.sum(pw, axis=-1, keepdims=True) + jnp.sum(pc, axis=-1, keepdims=True)
            o = _dot(pw.astype(BF16), v_ref[start:start + win, hs]) + _dot(pc.astype(BF16), vc_ref[:, hs])
            o_ref[g * tq:(g + 1) * tq, hs] = (o / l).astype(o_ref.dtype)


def _na_tables(rows):
    kh = min(NA_KH, rows)
    ng = rows // NA_G
    qr = np.arange(NA_G)[:, None]
    kr = np.arange(NA_WR)[None, :]
    sel_rows = []
    for g in range(ng):
        start_row = int(np.clip(NA_G * g - NA_KH // 2, 0, rows - NA_WR))
        r = NA_G * g + qr
        krow = start_row + kr
        rs = np.clip(r - kh // 2, 0, rows - kh)
        ok = (krow >= rs) & (krow < rs + kh)
        dr = krow - r + (NA_KH - 1)
        sel_rows.append(np.stack([ok & (dr == d) for d in range(2 * NA_KH - 1)]))
    sel_rows = np.stack(sel_rows)
    uniq, inverse = np.unique(sel_rows.reshape(ng, -1), axis=0, return_inverse=True)
    er = uniq.reshape((-1,) + sel_rows.shape[1:]).astype(np.float32)
    qc = np.arange(GRID_W)[:, None]
    kc = np.arange(GRID_W)[None, :]
    cs = np.clip(qc - NA_KW // 2, 0, GRID_W - NA_KW)
    ok_c = (kc >= cs) & (kc < cs + NA_KW)
    dc = np.clip(kc - qc, -(NA_KW - 1), NA_KW - 1) + (NA_KW - 1)
    ec = np.stack([ok_c & (dc == e) for e in range(2 * NA_KW - 1)]).astype(np.float32)
    return inverse.reshape(-1).astype(np.int32), er, ec


def _na_bias(rpb, er, ec):
    hp = lax.Precision.HIGHEST
    t = jnp.einsum('hde,eqk->hdqk', rpb.astype(F32), jnp.asarray(ec), precision=hp)
    t = jnp.where(jnp.asarray(ec.sum(axis=0) > 0.5), t * LOG2E, NEG_INF)
    outside = jnp.full(t[:, 0].shape, NEG_INF, F32)
    d_of = np.where(er.sum(axis=1) > 0.5, er.argmax(axis=1), -1)
    block = lambda d: outside if d < 0 else t[:, d]
    return jnp.stack([
        jnp.concatenate([jnp.concatenate([block(int(d)) for d in row], axis=-1) for row in var], axis=-2)
        for var in d_of])


def _na_call(p_lat, p_ctx, bias, var_map, batch, s_len, lc):
    rows = s_len // GRID_W
    hp = NA_HEADS_PER_STEP
    tw = hp * LANES
    tq = NA_G * GRID_W
    win = NA_WR * GRID_W
    qb, kb, vb = (T_QNA * QTILE_W // tw, T_KNA * QTILE_W // tw, T_VNA * QTILE_W // tw)
    cb = CTX_TILE0 * QTILE_W // tw
    n_var = bias.shape[0]
    return pl.pallas_call(
        functools.partial(_na_kernel, rows=rows, var_map=tuple(int(v) for v in var_map)),
        grid=(batch, NA_HEADS // hp),
        in_specs=[pl.BlockSpec((s_len, tw), lambda b, h: (b, qb + h)),
                  pl.BlockSpec((s_len, tw), lambda b, h: (b, kb + h)),
                  pl.BlockSpec((s_len, tw), lambda b, h: (b, vb + h)),
                  pl.BlockSpec((lc, tw), lambda b, h: (b, kb + h - cb)),
                  pl.BlockSpec((lc, tw), lambda b, h: (b, vb + h - cb)),
                  pl.BlockSpec((n_var, hp, tq, win), lambda b, h: (0, h, 0, 0))],
        out_specs=pl.BlockSpec((s_len, tw), lambda b, h: (b, h)),
        out_shape=jax.ShapeDtypeStruct((batch * s_len, NA_HEADS * NA_HEAD_DIM), BF16),
        compiler_params=_params(("arbitrary", "arbitrary")),
        name="na_attention",
    )(p_lat, p_lat, p_lat, p_ctx, p_ctx, bias)


def _layer_norm(x, g, b):
    mu = jnp.mean(x, axis=-1, keepdims=True)
    xc = x - mu
    var = jnp.mean(xc * xc, axis=-1, keepdims=True)
    return xc * lax.rsqrt(var + LN_EPS) * g + b


def _outln_kernel(a1_ref, a2_ref, w_ref, h_ref, mod_ref, ln_ref, r_ref, h1_ref, u2_ref, lg_ref, wb_ref):
    half = a1_ref.shape[1]

    @pl.when(pl.program_id(0) == 0)
    def _():
        rows = wb_ref.shape[0] // 4
        for r0 in range(0, wb_ref.shape[0], rows):
            wb_ref[r0:r0 + rows, :] = w_ref[0, r0:r0 + rows, :].astype(wb_ref.dtype)

    sub = a1_ref.shape[0] // OUTLN_SUBTILES
    for r0 in range(0, a1_ref.shape[0], sub):
        rs = slice(r0, r0 + sub)
        y = _dot(a1_ref[rs, :], wb_ref[:half, :]) + _dot(a2_ref[rs, :], wb_ref[half:, :])
        x = DEEPNORM_ALPHA * h_ref[rs, :] + mod_ref[0, 0:1, :] * y
        hn = _layer_norm(x, ln_ref[0:1, :], ln_ref[1:2, :])
        h1_ref[rs, :] = hn
        u = hn * (1.0 + mod_ref[0, 2:3, :]) + mod_ref[0, 1:2, :]
        u_bf = u.astype(BF16)
        u2_ref[rs, :] = u_bf
        lg_ref[rs, :] = _dot(u_bf, r_ref[...])


def _outln_call(a1, a2, w, layer, h, mod, ln, router2, rows_per_group, tm=512):
    m, d = h.shape
    half = a1.shape[1]
    tpg = rows_per_group // tm
    return pl.pallas_call(
        _outln_kernel,
        grid=(m // tm,),
        in_specs=[pl.BlockSpec((tm, half), lambda i: (i, 0)),
                  pl.BlockSpec((tm, half), lambda i: (i, 0)),
                  pl.BlockSpec((1, 2 * half, d), lambda i: (layer, 0, 0), pipeline_mode=pl.Buffered(1)),
                  pl.BlockSpec((tm, d), lambda i: (i, 0)),
                  pl.BlockSpec((1, 3, d), lambda i: (i // tpg, 0, 0)),
                  pl.BlockSpec((2, d), lambda i: (0, 0)),
                  _resident_spec((d, LANES))],
        out_specs=[pl.BlockSpec((tm, d), lambda i: (i, 0)),
                   pl.BlockSpec((tm, d), lambda i: (i, 0)),
                   pl.BlockSpec((tm, LANES), lambda i: (i, 0))],
        out_shape=[jax.ShapeDtypeStruct((m, d), F32),
                   jax.ShapeDtypeStruct((m, d), BF16),
                   jax.ShapeDtypeStruct((m, LANES), F32)],
        scratch_shapes=[pltpu.VMEM((2 * half, d), BF16)],
        compiler_params=_params(("arbitrary",)),
        name="out_proj_postnorm",
    )(a1, a2, w, h, mod, ln, router2)


def _router_operand(router):
    d, e = router.shape
    return jnp.pad(router, ((0, 0), (0, LANES - e))).astype(BF16)


def _route_kernel(lg_ref, slot_ref, aff_ref, cnt_ref, tri_ref, *, cap):
    n_tok = lg_ref.shape[2]

    @pl.when(pl.program_id(0) == 0)
    def _():
        chunk = 256
        for r0 in range(0, n_tok, chunk):
            r = r0 + lax.broadcasted_iota(I32, (chunk, n_tok), 0)
            c = lax.broadcasted_iota(I32, (chunk, n_tok), 1)
            tri_ref[r0:r0 + chunk, :] = jnp.where(r < c, 1.0, 0.0).astype(BF16)

    lg = lg_ref[0]
    ex = jnp.exp(lg - jnp.max(lg, axis=0, keepdims=True))
    aff = ex / jnp.sum(ex, axis=0, keepdims=True)
    bits = lax.bitcast_convert_type(aff, I32)
    n_e = lg.shape[0]
    count = lambda mask: jnp.sum(jnp.where(mask, 1.0, 0.0), axis=1, keepdims=True)

    def body(_, lohi):
        lo, hi = lohi
        mid = lo + jnp.right_shift(hi - lo, 1)
        ge = count(bits >= mid) >= cap
        return jnp.where(ge, mid, lo), jnp.where(ge, hi, mid)

    lo0 = jnp.zeros((n_e, 1), I32)
    hi0 = jnp.full((n_e, 1), 0x7F800000, I32)
    thr, _ = lax.fori_loop(0, 31, body, (lo0, hi0))
    gt = bits > thr
    eq = bits == thr
    need = cap - count(gt)
    pre_eq = _dot(jnp.where(eq, 1.0, 0.0).astype(BF16), tri_ref[...])
    sel = jnp.logical_or(gt, jnp.logical_and(eq, pre_eq < need))
    sel_bf = jnp.where(sel, 1.0, 0.0).astype(BF16)
    slot = _dot(sel_bf, tri_ref[...])
    slot_ref[0] = jnp.where(sel, slot.astype(I32), -1)
    aff_ref[0] = aff
    r = lax.broadcasted_iota(I32, (n_tok, LANES), 0)
    c = lax.broadcasted_iota(I32, (n_tok, LANES), 1)
    before = jnp.where(r < c * MOE_CHUNK, 1.0, 0.0).astype(BF16)
    cnt_ref[0] = _dot(sel_bf, before).astype(I32)


def _route_call(lg_t, cap):
    b, e, n_tok = lg_t.shape
    spec = pl.BlockSpec((1, e, n_tok), lambda i: (i, 0, 0))
    return pl.pallas_call(
        functools.partial(_route_kernel, cap=cap),
        grid=(b,),
        in_specs=[spec],
        out_specs=[spec, spec, pl.BlockSpec((1, e, LANES), lambda i: (i, 0, 0))],
        out_shape=[jax.ShapeDtypeStruct((b, e, n_tok), I32), jax.ShapeDtypeStruct((b, e, n_tok), F32),
                   jax.ShapeDtypeStruct((b, e, LANES), I32)],
        scratch_shapes=[pltpu.VMEM((n_tok, n_tok), BF16)],
        compiler_params=_params(("arbitrary",)),
        name="ec_route",
    )(lg_t)


def _slot_windows(cnt, cap, n_tok, chunk, win):
    bounds = cnt[:, :, 0:n_tok // MOE_CHUNK + 1:chunk // MOE_CHUNK]
    start, end = bounds[:, :, :-1], bounds[:, :, 1:]
    a = jnp.minimum((start // 16) * 16, cap - win)
    fits = jnp.all(end <= a + win, axis=1)
    return jnp.swapaxes(a, 1, 2).reshape(-1).astype(I32), fits.reshape(-1).astype(I32)


def _gather_kernel(win_ref, fit_ref, slot_ref, u_ref, o_ref, *, win, chunk):
    b, ks = pl.program_id(0), pl.program_id(1)
    n_e, cap = o_ref.shape[0], o_ref.shape[1]
    n_sub = u_ref.shape[0] // chunk
    step = b * pl.num_programs(1) + ks

    @pl.when(ks == 0)
    def _():
        o_ref[...] = jnp.zeros(o_ref.shape, o_ref.dtype)

    def place(rows, start_of, group, tok0, n_t):
        row = lax.broadcasted_iota(I32, (rows, n_t), 0)
        for g0 in range(0, n_e, group):
            starts = [start_of(e) for e in range(g0, g0 + group)]
            onehot = jnp.concatenate(
                [jnp.where(row + st == slot_ref[0, e:e + 1, tok0:tok0 + n_t], 1.0, 0.0).astype(BF16)
                 for st, e in zip(starts, range(g0, g0 + group))], axis=0)
            got = _dot(onehot, u_ref[tok0:tok0 + n_t, :]).astype(o_ref.dtype)
            for i, (st, e) in enumerate(zip(starts, range(g0, g0 + group))):
                o_ref[e, pl.ds(st, rows), :] += got[i * rows:(i + 1) * rows]

    @pl.when(fit_ref[step] != 0)
    def _():
        for sub in range(n_sub):
            base = (step * n_sub + sub) * n_e
            place(win, lambda e, base=base: pl.multiple_of(win_ref[base + e], 16), 512 // win, sub * chunk, chunk)

    @pl.when(fit_ref[step] == 0)
    def _():
        place(cap, lambda e: 0, 2, 0, n_sub * chunk)


def _gather_call(starts, fit, slot, u2, cap, chunk, win, chunks_per_step=2):
    b, e, n_tok = slot.shape
    d = u2.shape[1]
    tt = chunk * chunks_per_step
    nc = n_tok // tt
    fit = jnp.min(fit.reshape(-1, chunks_per_step), axis=1)
    grid_spec = pltpu.PrefetchScalarGridSpec(
        num_scalar_prefetch=2,
        grid=(b, nc),
        in_specs=[pl.BlockSpec((1, e, tt), lambda i, k, w, f: (i, 0, k)),
                  pl.BlockSpec((tt, d), lambda i, k, w, f: (i * nc + k, 0))],
        out_specs=pl.BlockSpec((e, cap, d), lambda i, k, w, f: (0, i, 0)),
    )
    return pl.pallas_call(
        functools.partial(_gather_kernel, win=win, chunk=chunk),
        grid_spec=grid_spec,
        out_shape=jax.ShapeDtypeStruct((e, b * cap, d), BF16),
        compiler_params=_params(("arbitrary", "arbitrary")),
        name="moe_gather",
    )(starts, fit, slot, u2)


def _expert_hidden_kernel(x_ref, w1_ref, w3_ref, o_ref, *, ff):
    tf = w1_ref.shape[3]
    last = pl.num_programs(1) - 1

    def block(width):
        w = jnp.concatenate([w1_ref[0, 0, :, :width].astype(BF16), w3_ref[0, 0, :, :width].astype(BF16)], axis=1)
        h = _dot(x_ref[0], w)
        h1 = h[:, :width]
        o_ref[0, :, :width] = ((h1 / (1.0 + jnp.exp(-h1))) * h[:, width:]).astype(o_ref.dtype)

    tail = ff - (pl.cdiv(ff, tf) - 1) * tf
    if tail == tf:
        block(tf)
    else:
        pl.when(pl.program_id(1) < last)(lambda: block(tf))
        pl.when(pl.program_id(1) == last)(lambda: block(tail))


def _expert_out_kernel(hd_ref, w2_ref, o_ref):
    y = _dot(hd_ref[0], w2_ref[0, 0].astype(BF16)).astype(o_ref.dtype)
    cap = o_ref.shape[1]
    for b in range(o_ref.shape[0]):
        o_ref[b] = y[b * cap:(b + 1) * cap]


def _expert_call(xe, w1, w3, w2, layer, batch, tf=768, tn=2048):
    e, m, d = xe.shape
    cap = m // batch
    ff = w1.shape[3]
    hidden = pl.pallas_call(
        functools.partial(_expert_hidden_kernel, ff=ff),
        grid=(e, pl.cdiv(ff, tf)),
        in_specs=[pl.BlockSpec((1, m, d), lambda i, f: (i, 0, 0)),
                  pl.BlockSpec((1, 1, d, tf), lambda i, f: (layer, i, 0, f)),
                  pl.BlockSpec((1, 1, d, tf), lambda i, f: (layer, i, 0, f))],
        out_specs=pl.BlockSpec((1, m, tf), lambda i, f: (i, 0, f)),
        out_shape=jax.ShapeDtypeStruct((e, m, ff), BF16),
        compiler_params=_params(("arbitrary", "arbitrary")),
        name="moe_expert_hidden",
    )(xe, w1, w3)
    return pl.pallas_call(
        _expert_out_kernel,
        grid=(e, d // tn),
        in_specs=[pl.BlockSpec((1, m, ff), lambda i, j: (i, 0, 0)),
                  pl.BlockSpec((1, 1, ff, tn), lambda i, j: (layer, i, 0, j))],
        out_specs=pl.BlockSpec((batch, cap, tn), lambda i, j: (0, i, j)),
        out_shape=jax.ShapeDtypeStruct((batch, e * cap, d), BF16),
        compiler_params=_params(("arbitrary", "arbitrary")),
        name="moe_expert_out",
    )(hidden, w2)


def _scatter_kernel(win_ref, fit_ref, slot_ref, aff_ref, ye_ref, h_ref, mod_ref, ln_ref, *rest, with_next):
    if with_next:
        nmod_ref, h2_ref, un_ref, pt_ref, yew_ref, moe_ref = rest
    else:
        h2_ref, pt_ref, yew_ref, moe_ref = rest
    n_e = slot_ref.shape[1]
    cap = ye_ref.shape[1] // n_e
    tt = h_ref.shape[0]
    step = pl.program_id(0) * pl.num_programs(1) + pl.program_id(1)

    def gates(rows, e, start):
        row = lax.broadcasted_iota(I32, (rows, tt), 0)
        return jnp.where(row + start == slot_ref[0, e:e + 1, :], aff_ref[0, e:e + 1, :], 0.0).astype(BF16)

    @pl.when(fit_ref[step] != 0)
    def _():
        for e in range(n_e):
            start = pl.multiple_of(win_ref[step * n_e + e], 16)
            pt_ref[e * MOE_WIN:(e + 1) * MOE_WIN, :] = gates(MOE_WIN, e, start)
            yew_ref[e * MOE_WIN:(e + 1) * MOE_WIN, :] = ye_ref[0, pl.ds(e * cap + start, MOE_WIN), :]
        moe_ref[...] = _dot_tn(pt_ref[:n_e * MOE_WIN, :], yew_ref[...])

    @pl.when(fit_ref[step] == 0)
    def _():
        for e in range(n_e):
            pt_ref[e * cap:(e + 1) * cap, :] = gates(cap, e, 0)
        moe_ref[...] = _dot_tn(pt_ref[...], ye_ref[0])

    x = DEEPNORM_ALPHA * h_ref[...] + mod_ref[0] * moe_ref[...]
    hn = _layer_norm(x, ln_ref[0:1, :], ln_ref[1:2, :])
    h2_ref[...] = hn
    if with_next:
        un_ref[...] = (hn * (1.0 + nmod_ref[0, 1:2, :]) + nmod_ref[0, 0:1, :]).astype(un_ref.dtype)


def _scatter_call(win, fit, slot, aff, ye, h1, gate, ln, next_ss):
    b, e, n_tok = slot.shape
    m, d = h1.shape
    tt = MOE_CHUNK
    nt = n_tok // tt
    rows = ye.shape[1]
    with_next = next_ss is not None
    tok_spec = pl.BlockSpec((1, e, tt), lambda i, t, w, f: (i, 0, t))
    row_spec = pl.BlockSpec((tt, d), lambda i, t, w, f: (i * nt + t, 0))
    in_specs = [tok_spec, tok_spec,
                pl.BlockSpec((1, rows, d), lambda i, t, w, f: (i, 0, 0), pipeline_mode=pl.Buffered(1)),
                row_spec,
                pl.BlockSpec((1, 1, d), lambda i, t, w, f: (i, 0, 0)),
                pl.BlockSpec((2, d), lambda i, t, w, f: (0, 0))]
    args = [slot, aff, ye, h1, gate, ln]
    out_specs = [row_spec]
    out_shape = [jax.ShapeDtypeStruct((m, d), F32)]
    if with_next:
        in_specs.append(pl.BlockSpec((1, 2, d), lambda i, t, w, f: (i, 0, 0)))
        args.append(next_ss)
        out_specs.append(row_spec)
        out_shape.append(jax.ShapeDtypeStruct((m, d), BF16))
    grid_spec = pltpu.PrefetchScalarGridSpec(
        num_scalar_prefetch=2,
        grid=(b, nt),
        in_specs=in_specs,
        out_specs=out_specs,
        scratch_shapes=[pltpu.VMEM((rows, tt), BF16),
                        pltpu.VMEM((e * MOE_WIN, d), BF16),
                        pltpu.VMEM((tt, d), F32)],
    )
    res = pl.pallas_call(
        functools.partial(_scatter_kernel, with_next=with_next),
        grid_spec=grid_spec,
        out_shape=out_shape,
        compiler_params=_params(("arbitrary", "arbitrary")),
        name="moe_combine_postnorm",
    )(win, fit, *args)
    return res if with_next else (res[0], None)


def _ec_moe(h1, u2, logits, w1, w3, w2, layer, gate, ln, next_ss, batch, n_tok):
    cap = EC_CAPACITY_FACTOR * n_tok // N_EXPERTS
    lg_t = jnp.swapaxes(logits[:, :N_EXPERTS].reshape(batch, n_tok, N_EXPERTS), 1, 2)
    slot, aff, cnt = _route_call(lg_t, cap)
    g_win, g_fit = _slot_windows(cnt, cap, n_tok, GATHER_CHUNK, GATHER_WIN)
    xe = _gather_call(g_win, g_fit, slot, u2, cap, GATHER_CHUNK, GATHER_WIN)
    ye = _expert_call(xe, w1, w3, w2, layer, batch)
    c_win, c_fit = _slot_windows(cnt, cap, n_tok, MOE_CHUNK, MOE_WIN)
    return _scatter_call(c_win, c_fit, slot, aff, ye, h1, gate, ln, next_ss)


def _mm_kernel(a_ref, w_ref, o_ref, wb_ref):
    @pl.when(pl.program_id(1) == 0)
    def _():
        wb_ref[...] = w_ref[0].astype(wb_ref.dtype)

    o_ref[...] = _dot(a_ref[...], wb_ref[...]).astype(o_ref.dtype)


def _mm_call(a, w, layer, out_dtype, tm=1024, tn=1024):
    m, k = a.shape
    n = w.shape[2]
    return pl.pallas_call(
        _mm_kernel,
        grid=(n // tn, m // tm),
        in_specs=[pl.BlockSpec((tm, k), lambda j, i: (i, 0)),
                  pl.BlockSpec((1, k, tn), lambda j, i: (layer, 0, j))],
        out_specs=pl.BlockSpec((tm, tn), lambda j, i: (i, j)),
        out_shape=jax.ShapeDtypeStruct((m, n), out_dtype),
        scratch_shapes=[pltpu.VMEM((k, tn), BF16)],
        compiler_params=_params(("arbitrary", "arbitrary")),
        name="matmul",
    )(a, w)


def _short_conv(p, w, b):
    x = p.astype(F32)
    n = x.shape[0]
    r = lax.broadcasted_iota(I32, x.shape, 0)
    prev = jnp.where(r == 0, 0.0, pltpu.roll(x, 1, 0))
    nxt = jnp.where(r == n - 1, 0.0, pltpu.roll(x, n - 1, 0))
    return b + w[0:1, :] * prev + w[1:2, :] * x + w[2:3, :] * nxt


def _sconv_kernel(p_ref, w_ref, b_ref, o_ref):
    o_ref[...] = _short_conv(p_ref[...], w_ref[...], b_ref[...]).astype(o_ref.dtype)


def _sconv_call(p, conv_w, conv_b, col0, ncols, out_dtype, batch, n_tok, tc=512):
    c0 = col0 // tc
    return pl.pallas_call(
        _sconv_kernel,
        grid=(batch, ncols // tc),
        in_specs=[pl.BlockSpec((n_tok, tc), lambda b, j: (b, c0 + j)),
                  pl.BlockSpec((HY_SHORT, tc), lambda b, j: (0, c0 + j)),
                  pl.BlockSpec((1, tc), lambda b, j: (0, c0 + j))],
        out_specs=pl.BlockSpec((n_tok, tc), lambda b, j: (b, j)),
        out_shape=jax.ShapeDtypeStruct((batch * n_tok, ncols), out_dtype),
        compiler_params=_params(("arbitrary", "arbitrary")),
        name="hyena_short_conv",
    )(p, conv_w, conv_b.reshape(1, -1))


def _dft_fwd_kernel(fc_ref, fs_ref, r1_ref, r2_ref, *rest, spectral):
    xr = _dot(fc_ref[...], r1_ref[...])
    xi = _dot(fs_ref[...], r2_ref[...])
    if spectral:
        h_ref, nyq_ref, o_ref = rest
        hr = h_ref[0, 0]
        hi = h_ref[0, 1]
        dc = lax.broadcasted_iota(I32, xr.shape, 0) == 0
        yr = xr * hr - jnp.where(dc, 0.0, xi * hi)
        yi = jnp.where(dc, xi * nyq_ref[...], xr * hi + xi * hr)
    else:
        (o_ref,) = rest
        yr, yi = xr, xi
    o_ref[0, 0] = yr.astype(o_ref.dtype)
    o_ref[0, 1] = yi.astype(o_ref.dtype)


def _resident_spec(shape):
    return pl.BlockSpec(shape, lambda *_: (0,) * len(shape), pipeline_mode=pl.Buffered(1))


def _dft_fwd_call(fc, fs, r1, r2, spec, out_dtype, batch, n_tok, tn=256):
    ncols = r1.shape[1]
    r_spec = pl.BlockSpec((n_tok, tn), lambda j, b: (b, j))
    in_specs = [_resident_spec(fc.shape), _resident_spec(fs.shape), r_spec, r_spec]
    args = [fc, fs, r1, r2]
    if spec is not None:
        h, nyq, hcol0 = spec
        c0 = hcol0 // tn
        in_specs += [pl.BlockSpec((1, 2, n_tok, tn), lambda j, b: (0, 0, 0, c0 + j)),
                     pl.BlockSpec((1, tn), lambda j, b: (0, c0 + j))]
        args += [h, nyq]
    return pl.pallas_call(
        functools.partial(_dft_fwd_kernel, spectral=spec is not None),
        grid=(ncols // tn, batch),
        in_specs=in_specs,
        out_specs=pl.BlockSpec((1, 2, n_tok, tn), lambda j, b: (b, 0, 0, j)),
        out_shape=jax.ShapeDtypeStruct((batch, 2, n_tok, ncols), out_dtype),
        compiler_params=_params(("arbitrary", "arbitrary")),
        name="dft_forward",
    )(*args)


def _dual_kernel(a1_ref, a2_ref, b1_ref, b2_ref, *rest, hyena, planes):
    b1 = b1_ref[0, 0] if planes else b1_ref[...]
    b2 = b2_ref[0, 0] if planes else b2_ref[...]
    y = _dot(a1_ref[...], b1) + _dot(a2_ref[...], b2)
    if hyena:
        p_ref, cw_ref, cb_ref, z_ref, skip_ref, o_ref = rest
        gate = _short_conv(p_ref[...], cw_ref[...], cb_ref[...])
        y = gate * (y + skip_ref[...] * z_ref[...].astype(F32))
    else:
        (o_ref,) = rest
    o_ref[...] = y.astype(o_ref.dtype)


def _dual_call(a1, a2, b1, b2, hy, batch, n_tok, ncols, tn=256):
    planes = b1.ndim == 4
    if planes:
        b_specs = [pl.BlockSpec((1, 1, n_tok, tn), lambda b, j: (b, 0, 0, j)),
                   pl.BlockSpec((1, 1, n_tok, tn), lambda b, j: (b, 1, 0, j))]
    else:
        b_specs = [pl.BlockSpec((n_tok, tn), lambda b, j: (b, j))] * 2
    in_specs = [_resident_spec(a1.shape), _resident_spec(a2.shape)] + b_specs
    args = [a1, a2, b1, b2]
    o_spec = pl.BlockSpec((n_tok, tn), lambda b, j: (b, j))
    if hy is not None:
        p, conv_w, conv_b, pcol0, z, skip = hy
        c0 = pcol0 // tn
        in_specs += [pl.BlockSpec((n_tok, tn), lambda b, j: (b, c0 + j)),
                     pl.BlockSpec((HY_SHORT, tn), lambda b, j: (0, c0 + j)),
                     pl.BlockSpec((1, tn), lambda b, j: (0, c0 + j)),
                     o_spec,
                     pl.BlockSpec((1, tn), lambda b, j: (0, j))]
        args += [p, conv_w, conv_b.reshape(1, -1), z, skip]
    return pl.pallas_call(
        functools.partial(_dual_kernel, hyena=hy is not None, planes=planes),
        grid=(batch, ncols // tn),
        in_specs=in_specs,
        out_specs=o_spec,
        out_shape=jax.ShapeDtypeStruct((batch * n_tok, ncols), BF16),
        compiler_params=_params(("arbitrary", "arbitrary")),
        name="seq_mix_matmul",
    )(*args)


def _fnet_kernel(cl_ref, msl_ref, x_ref, cw_ref, o_ref):
    gw = cw_ref.shape[1]
    for c0 in range(0, x_ref.shape[1], gw):
        x = x_ref[:, c0:c0 + gw]
        seq = jnp.concatenate([_dot(cl_ref[...], x).astype(BF16), _dot(msl_ref[...], x).astype(BF16)], axis=1)
        o_ref[:, c0:c0 + gw] = _dot(seq, cw_ref[...]).astype(o_ref.dtype)


def _fnet_call(cl, msl, p, cw, col0, batch, n_tok, groups_per_step=2):
    gw = cw.shape[1]
    tw = gw * groups_per_step
    c0 = col0 // tw
    return pl.pallas_call(
        _fnet_kernel,
        grid=(batch, FN_GROUPS // groups_per_step),
        in_specs=[_resident_spec(cl.shape), _resident_spec(msl.shape),
                  pl.BlockSpec((n_tok, tw), lambda b, g: (b, c0 + g)),
                  pl.BlockSpec(cw.shape, lambda b, g: (0, 0))],
        out_specs=pl.BlockSpec((n_tok, tw), lambda b, g: (b, g)),
        out_shape=jax.ShapeDtypeStruct((batch * n_tok, FN_GROUPS * gw), BF16),
        compiler_params=_params(("arbitrary", "arbitrary")),
        name="fnet_mix",
    )(cl, msl, p, cw)


def _cos_sin_matrix(n_rows, n_cols, period, split=64):
    r = np.arange(n_rows, dtype=np.int64)[:, None]
    c_hi = (np.arange(n_cols // split, dtype=np.int64) * split)[None, :]
    c_lo = np.arange(split, dtype=np.int64)[None, :]
    ang = lambda c: 2.0 * np.pi * ((r * c) % period).astype(np.float64) / period
    ca, sa = jnp.asarray(np.cos(ang(c_hi)), F32), jnp.asarray(np.sin(ang(c_hi)), F32)
    cb, sb = jnp.asarray(np.cos(ang(c_lo)), F32), jnp.asarray(np.sin(ang(c_lo)), F32)
    cos = ca[:, :, None] * cb[:, None, :] - sa[:, :, None] * sb[:, None, :]
    sin = sa[:, :, None] * cb[:, None, :] + ca[:, :, None] * sb[:, None, :]
    return cos.reshape(n_rows, n_cols), sin.reshape(n_rows, n_cols)


def _trig_kernel(ta_ref, tb_ref, ea_ref, eb_ref, *o_refs, mode, period, scale):
    sa_ca = _dot(ta_ref[...], ea_ref[...])
    sb_cb = _dot(tb_ref[...], eb_ref[...])
    nc = sa_ca.shape[1] // 2
    ca, sa, cb, sb = sa_ca[:, :nc], sa_ca[:, nc:], sb_cb[:, :nc], sb_cb[:, nc:]
    cos = ca * cb - sa * sb
    sin = sa * cb + ca * sb
    tm = cos.shape[0]
    row = pl.program_id(0) * tm + lax.broadcasted_iota(I32, cos.shape, 0)
    col = lax.broadcasted_iota(I32, cos.shape, 1)
    alt = lambda idx: (1 - 2 * jnp.bitwise_and(idx, 1)).astype(F32)
    if mode == "hyena":
        fc_ref, fs_ref, gc_ref, gs_ref = o_refs
        fc_ref[...] = cos.astype(fc_ref.dtype)
        fs_ref[...] = jnp.where(row == 0, alt(col), -sin).astype(fs_ref.dtype)
        gc_ref[...] = (cos * jnp.where(col == 0, 1.0 / period, 2.0 / period)).astype(gc_ref.dtype)
        gs_ref[...] = jnp.where(col == 0, alt(row) * (1.0 / period), sin * (-2.0 / period)).astype(gs_ref.dtype)
    else:
        c_ref, ms_ref = o_refs
        c_ref[...] = (cos * scale).astype(c_ref.dtype)
        ms_ref[...] = (sin * (-scale)).astype(ms_ref.dtype)


def _split3(x):
    bf = jnp.bfloat16
    x = np.asarray(x, np.float32)
    hi = x.astype(bf)
    r1 = x - hi.astype(np.float32)
    mid = r1.astype(bf)
    lo = (r1 - mid.astype(np.float32)).astype(bf)
    return [hi, mid, lo]


def _trig_call(n, period, mode, scale=1.0, split=64, tm=256):
    r = np.arange(n, dtype=np.int64)[:, None]
    c_hi = (np.arange(n // split, dtype=np.int64) * split)[None, :]
    c_lo = np.arange(split, dtype=np.int64)[None, :]
    ang = lambda c: 2.0 * np.pi * ((r * c) % period).astype(np.float64) / period
    def operands(a, col_group):
        terms = _split3(np.cos(a)) + _split3(np.sin(a))
        k = a.shape[1]
        spread = (col_group[None, :] == np.arange(k)[:, None]).astype(np.float32)
        zero = np.zeros_like(spread)
        e = np.concatenate([np.concatenate([spread, zero], 1)] * 3 + [np.concatenate([zero, spread], 1)] * 3, 0)
        return jnp.asarray(np.concatenate(terms, axis=1)), jnp.asarray(e, BF16)
    cols = np.arange(n)
    ta, ea = operands(ang(c_hi), cols // split)
    tb, eb = operands(ang(c_lo), cols % split)
    n_out = 4 if mode == "hyena" else 2
    o_spec = pl.BlockSpec((tm, n), lambda i: (i, 0))
    return pl.pallas_call(
        functools.partial(_trig_kernel, mode=mode, period=period, scale=scale),
        grid=(n // tm,),
        in_specs=[pl.BlockSpec((tm, ta.shape[1]), lambda i: (i, 0)),
                  pl.BlockSpec((tm, tb.shape[1]), lambda i: (i, 0)),
                  pl.BlockSpec(ea.shape, lambda i: (0, 0)),
                  pl.BlockSpec(eb.shape, lambda i: (0, 0))],
        out_specs=[o_spec] * n_out,
        out_shape=[jax.ShapeDtypeStruct((n, n), BF16)] * n_out,
        compiler_params=_params(("arbitrary",)),
        name="dft_tables",
    )(ta, tb, ea, eb)


def _hyena_dft_operands(n_tok):
    return _trig_call(n_tok, 2 * n_tok, "hyena")


def _fnet_operands(n_tok, gw):
    cl, msl = _trig_call(n_tok, n_tok, "fnet", scale=1.0 / math.sqrt(n_tok * gw))
    cw, sw = _cos_sin_matrix(gw, gw, gw)
    return cl, msl, jnp.concatenate([cw, sw], axis=0).astype(BF16)


def _filter_kernel(z_ref, w1_ref, b1_ref, f1_ref, w2_ref, b2_ref, f2_ref, w3f_ref, w3b_ref, dl_ref,
                   sum_ref, dif_ref, nyq_ref, hdn_ref):
    @pl.when(jnp.logical_and(pl.program_id(0) == 0, pl.program_id(1) == 0))
    def _():
        hdot = lambda a, b: jnp.dot(a, b, precision=lax.Precision.HIGHEST, preferred_element_type=F32)
        hdn = jnp.sin(f1_ref[...] * (hdot(z_ref[...], w1_ref[...]) + b1_ref[...]))
        hdn = jnp.sin(f2_ref[...] * (hdot(hdn, w2_ref[...]) + b2_ref[...]))
        hdn_ref[...] = hdn.astype(hdn_ref.dtype)

    n = hdn_ref.shape[0]
    row = lax.broadcasted_iota(I32, (n, dl_ref.shape[1]), 0)
    decay = jnp.exp(-(row.astype(F32) / (n - 1)) * dl_ref[...])
    hf = _dot(hdn_ref[...], w3f_ref[...].astype(BF16)) * decay
    hb = jnp.where(row == 0, 0.0, _dot(hdn_ref[...], w3b_ref[...].astype(BF16)) * decay)
    tot = hf + hb
    sum_ref[...] = tot.astype(sum_ref.dtype)
    dif_ref[...] = (hf - hb).astype(dif_ref.dtype)
    nyq_ref[...] = jnp.sum(jnp.where(jnp.bitwise_and(row, 1) == 0, tot, -tot), axis=0, keepdims=True)


def _filter_call(n_tok, fw1, fb1, ff1, fw2, fb2, ff2, fw3, tc=512):
    t01 = np.linspace(0.0, 1.0, n_tok, dtype=np.float32)
    w = (2.0 * math.pi * np.arange(n_tok, dtype=np.float32) / n_tok).astype(np.float32)
    bands = np.linspace(1e-4, HY_BANDS - 1, HY_BANDS, dtype=np.float32)
    z = np.concatenate([t01[:, None], np.cos(w[:, None] * bands), -np.sin(w[:, None] * bands)], -1)
    deltas = np.abs(np.linspace(HY_MIN_DECAY, HY_MAX_DECAY, HY_W, dtype=np.float32))[None, :]
    emb, hid = fw1.shape
    nc = HY_W // tc
    full = lambda shape: pl.BlockSpec(shape, lambda o, j: (0,) * len(shape))
    o_spec = pl.BlockSpec((n_tok, tc), lambda o, j: (0, o * nc + j))
    return pl.pallas_call(
        _filter_kernel,
        grid=(HY_ORDER, nc),
        in_specs=[full((n_tok, emb)), full((emb, hid)), full((1, hid)), full((1, hid)),
                  full((hid, hid)), full((1, hid)), full((1, hid)),
                  pl.BlockSpec((hid, tc), lambda o, j: (0, (2 * o) * nc + j)),
                  pl.BlockSpec((hid, tc), lambda o, j: (0, (2 * o + 1) * nc + j)),
                  pl.BlockSpec((1, tc), lambda o, j: (0, j))],
        out_specs=[o_spec, o_spec, pl.BlockSpec((1, tc), lambda o, j: (0, o * nc + j))],
        out_shape=[jax.ShapeDtypeStruct((n_tok, HY_ORDER * HY_W), BF16),
                   jax.ShapeDtypeStruct((n_tok, HY_ORDER * HY_W), BF16),
                   jax.ShapeDtypeStruct((1, HY_ORDER * HY_W), F32)],
        scratch_shapes=[pltpu.VMEM((n_tok, hid), BF16)],
        compiler_params=_params(("arbitrary", "arbitrary")),
        name="hyena_filters",
    )(jnp.asarray(z.astype(np.float32)), fw1, fb1.reshape(1, hid), ff1.reshape(1, hid),
      fw2, fb2.reshape(1, hid), ff2.reshape(1, hid), fw3, fw3, jnp.asarray(deltas))


def kernel(x, c, ctx, c_ctx, ada_w, ada_b, ln1_g, ln1_b, ln2_g, ln2_b, router, exp_w1, exp_w3, exp_w2,
           ab_w_in, ab_kv_norm, ab_w_ukv, ab_rpb, ab_w_out,
           cd_w_in, cd_conv_w, cd_conv_b, cd_filt_w1, cd_filt_b1, cd_filt_freq1, cd_filt_w2, cd_filt_b2,
           cd_filt_freq2, cd_filt_w3, cd_skip, cd_w_out):
    batch, n_tok, d = x.shape
    lc = ctx.shape[1]
    x2d = x.reshape(batch * n_tok, d)
    ctx2d = ctx.reshape(batch * lc, d)

    pad_rows = (-(batch + 1)) % 8
    cv = jnp.concatenate([c, c_ctx[None, :], jnp.zeros((pad_rows, d), F32)], axis=0)
    ada = _ada_call(cv, ada_w, ada_b)
    mods = [ada[i, :batch].reshape(batch, 6, d) for i in range(DEPTH)]
    ctx_ss = ada[0, batch, :2 * d].reshape(1, 2, d)

    w_in = _w_in_prep_call(jnp.swapaxes(ab_w_in, 1, 2), 0)
    p_lat = _mm_rope_call(x2d, mods[0][:, 0:2], n_tok, w_in, _rope_tables(n_tok), tm=1024)
    p_ctx = _mm_rope_call(ctx2d, ctx_ss, batch * lc, w_in, _identity_rope_tables(batch * lc), tm=batch * lc,
                          tile0=CTX_TILE0)
    w_ukv = _permute_w_ukv(ab_w_ukv[0])
    kv_lat = _kvup_call(p_lat, ab_kv_norm[0], w_ukv, tm=1024)
    kv_ctx = _kvup_call(p_ctx, ab_kv_norm[0], w_ukv, tm=lc, tile0=CTX_TILE0)
    a_out = _mla_call(p_lat, kv_lat, p_ctx, kv_ctx, batch, n_tok, lc)
    var_map, na_er, na_ec = _na_tables(n_tok // GRID_W)
    b_out = _na_call(p_lat, p_ctx, _na_bias(ab_rpb[0], na_er, na_ec), var_map, batch, n_tok, lc)
    ln1 = jnp.stack([ln1_g, ln1_b], axis=1)
    ln2 = jnp.stack([ln2_g, ln2_b], axis=1)
    h1, u2, logits = _outln_call(a_out, b_out, ab_w_out, 0, x2d,
                                 mods[0][:, 2:5], ln1[0], _router_operand(router[0]), n_tok)
    h, u = _ec_moe(h1, u2, logits, exp_w1, exp_w3, exp_w2, 0, mods[0][:, 5:6], ln2[0],
                   mods[1][:, 0:2], batch, n_tok)

    p1 = _mm_call(u, cd_w_in, 0, BF16)
    s0 = _sconv_call(p1, cd_conv_w[0], cd_conv_b[0], 0, HY_W, BF16, batch, n_tok)
    h_sum, h_dif, h_nyq = _filter_call(n_tok, cd_filt_w1[0], cd_filt_b1[0], cd_filt_freq1[0], cd_filt_w2[0],
                                       cd_filt_b2[0], cd_filt_freq2[0], cd_filt_w3[0])
    fc, fs, gc, gs = _hyena_dft_operands(n_tok)
    h_spec = _dft_fwd_call(fc, fs, h_sum, h_dif, None, F32, 1, n_tok)
    z = s0
    for o in range(HY_ORDER):
        y_spec = _dft_fwd_call(fc, fs, z, z, (h_spec, h_nyq, o * HY_W), BF16, batch, n_tok)
        z = _dual_call(gc, gs, y_spec, y_spec,
                       (p1, cd_conv_w[0], cd_conv_b[0], (o + 1) * HY_W, z, cd_skip[0][o:o + 1]),
                       batch, n_tok, HY_W)
    cl, msl, cw = _fnet_operands(n_tok, FN_GROUP_W)
    y_fn = _fnet_call(cl, msl, p1, cw, HY_IN_W, batch, n_tok)
    h1, u2, logits = _outln_call(z, y_fn, cd_w_out, 0, h,
                                 mods[1][:, 2:5], ln1[1], _router_operand(router[1]), n_tok)
    h, _ = _ec_moe(h1, u2, logits, exp_w1, exp_w3, exp_w2, 1, mods[1][:, 5:6], ln2[1],
                   None, batch, n_tok)
    return h.reshape(batch, n_tok, d)
```

```python
import functools
import math

import numpy as np
import jax
import jax.numpy as jnp
from jax import lax
from jax.experimental import pallas as pl
from jax.experimental.pallas import tpu as pltpu

F32 = jnp.float32
BF16 = jnp.bfloat16
I32 = jnp.int32

D_MODEL = 2048
BATCH = 4
SEQ = 2048
DEPTH = 2
CTX_LEN = 256
GRID_W = 64
DEEPNORM_ALPHA = (2.0 * DEPTH) ** 0.25
LN_EPS = 1e-6
NEG_INF = -1e30
LOG2E = math.log2(math.e)

MLA_HEADS = 8
MLA_NOPE = 128
MLA_ROPE = 64
MLA_QK = MLA_NOPE + MLA_ROPE
MLA_V = 128
MLA_KV_RANK = 512
ROPE_THETA = 10000.0

NA_HEADS = 8
NA_HEAD_DIM = 128
NA_KH = 8
NA_KW = 16

OFF_CKV = MLA_HEADS * MLA_QK
OFF_KPE = OFF_CKV + MLA_KV_RANK
OFF_QNA = OFF_KPE + MLA_ROPE
OFF_KNA = OFF_QNA + NA_HEADS * NA_HEAD_DIM
OFF_VNA = OFF_KNA + NA_HEADS * NA_HEAD_DIM

HY_W = 1024
HY_ORDER = 2
HY_IN_W = (HY_ORDER + 1) * HY_W
HY_SHORT = 3
HY_BANDS = 16
HY_DECAY_TARGET = 1e-2
HY_MIN_DECAY = math.log(HY_DECAY_TARGET) / 1.5
HY_MAX_DECAY = math.log(HY_DECAY_TARGET) / 0.3

FN_W = 1024
FN_GROUPS = 4
FN_GROUP_W = FN_W // FN_GROUPS

N_EXPERTS = 16
EC_CAPACITY_FACTOR = 2
EXPERT_FF = 1408

LANES = 128
MXU_DIM = 256
VMEM_LIMIT = 56 * 1024 * 1024

QTILE_W = 2 * LANES
T_KPE = MLA_HEADS
T_CKV = T_KPE + 2
T_QNA = T_CKV + MLA_KV_RANK // QTILE_W
T_KNA = T_QNA + NA_HEADS * NA_HEAD_DIM // QTILE_W
T_VNA = T_KNA + NA_HEADS * NA_HEAD_DIM // QTILE_W
N_ABTILES = T_VNA + NA_HEADS * NA_HEAD_DIM // QTILE_W
AB_PERM_W = N_ABTILES * QTILE_W
CTX_TILE0 = T_KPE
INPROJ_TILES_PER_STEP = 8

MLA_SUBTILES = 8
OUTLN_SUBTILES = 2
MOE_CHUNK = 256
MOE_WIN = 64
GATHER_CHUNK = 256
GATHER_WIN = 64
NA_G = 4
NA_WR = NA_KH + NA_G - 1
NA_HEADS_PER_STEP = 2


def _params(sem, vmem=VMEM_LIMIT):
    return pltpu.CompilerParams(dimension_semantics=sem, vmem_limit_bytes=vmem)


def _dot(a, b):
    return jnp.dot(a, b, preferred_element_type=F32)


def _dot_nt(a, b):
    return lax.dot_general(a, b, (((1,), (1,)), ((), ())), preferred_element_type=F32)


def _dot_tn(a, b):
    return lax.dot_general(a, b, (((0,), (0,)), ((), ())), preferred_element_type=F32)


def _ada_kernel(c_ref, w_ref, b_ref, o_ref):
    c = c_ref[...]
    a = c / (1.0 + jnp.exp(-c))
    o_ref[0] = _dot(a.astype(BF16), w_ref[0].astype(BF16)) + b_ref[0]


def _ada_call(cv, ada_w, ada_b, tn=2048):
    depth, d, n = ada_w.shape
    rows = cv.shape[0]
    return pl.pallas_call(
        _ada_kernel,
        grid=(depth, n // tn),
        in_specs=[pl.BlockSpec((rows, d), lambda l, j: (0, 0)),
                  pl.BlockSpec((1, d, tn), lambda l, j: (l, 0, j)),
                  pl.BlockSpec((1, 1, tn), lambda l, j: (l, 0, j))],
        out_specs=pl.BlockSpec((1, rows, tn), lambda l, j: (l, 0, j)),
        out_shape=jax.ShapeDtypeStruct((depth, rows, n), F32),
        compiler_params=_params(("arbitrary", "arbitrary")),
        name="ada_params",
    )(cv, ada_w, ada_b.reshape(depth, 1, n))


def _mm_rope_kernel(x_ref, ss_ref, w_ref, ta_ref, tb_ref, tc_ref, o_ref, a_ref, *, tile0):
    @pl.when(pl.program_id(1) == 0)
    def _():
        a_ref[...] = (x_ref[...] * (1.0 + ss_ref[0, 1:2, :]) + ss_ref[0, 0:1, :]).astype(a_ref.dtype)

    a = a_ref[...]
    for q in range(INPROJ_TILES_PER_STEP):
        t = tile0 + INPROJ_TILES_PER_STEP * pl.program_id(1) + q
        c0 = q * QTILE_W
        acc = _dot_nt(a, w_ref[c0:c0 + QTILE_W, :])
        is_rope = t <= T_KPE
        f = jnp.where(t < T_KPE, LOG2E * MLA_QK ** -0.5,
                      jnp.where(jnp.logical_and(t >= T_QNA, t < T_KNA), LOG2E * NA_HEAD_DIM ** -0.5, 1.0)
                      ).astype(F32)
        hi = acc[:, LANES:]
        rot = (hi * jnp.where(is_rope, ta_ref[...], 1.0)
               + pltpu.roll(hi, LANES - MLA_ROPE // 2, 1) * jnp.where(is_rope, tb_ref[...], 0.0)
               + pltpu.roll(hi, MLA_ROPE // 2, 1) * jnp.where(is_rope, tc_ref[...], 0.0))
        o_ref[:, c0:c0 + LANES] = (acc[:, :LANES] * f).astype(o_ref.dtype)
        o_ref[:, c0 + LANES:c0 + QTILE_W] = (rot * f).astype(o_ref.dtype)


def _mm_rope_call(x, ss, rows_per_group, w, tabs, tm, tile0=0):
    m, k = x.shape
    tn = INPROJ_TILES_PER_STEP * QTILE_W
    n = w.shape[0] - tile0 * QTILE_W
    j0 = tile0 // INPROJ_TILES_PER_STEP
    ntab = tabs[0].shape[0] // tm
    tpg = rows_per_group // tm
    tab_spec = pl.BlockSpec((tm, LANES), lambda i, j: (i % ntab, 0))
    return pl.pallas_call(
        functools.partial(_mm_rope_kernel, tile0=tile0),
        grid=(m // tm, n // tn),
        in_specs=[pl.BlockSpec((tm, k), lambda i, j: (i, 0)),
                  pl.BlockSpec((1, 2, k), lambda i, j: (i // tpg, 0, 0)),
                  pl.BlockSpec((tn, k), lambda i, j: (j0 + j, 0)),
                  tab_spec, tab_spec, tab_spec],
        out_specs=pl.BlockSpec((tm, tn), lambda i, j: (i, j)),
        out_shape=jax.ShapeDtypeStruct((m, n), BF16),
        scratch_shapes=[pltpu.VMEM((tm, k), BF16)],
        compiler_params=_params(("arbitrary", "arbitrary")),
        name="attn_in_proj",
    )(x, ss, w, *tabs)


def _rope_tables(n_tok):
    t = np.arange(n_tok)
    row = (t // GRID_W).astype(np.float32)
    col = (t % GRID_W).astype(np.float32)
    n_freq = MLA_ROPE // 4
    inv = (ROPE_THETA ** (-np.arange(n_freq, dtype=np.float32) / n_freq)).astype(np.float32)
    ang = np.concatenate([row[:, None] * inv, col[:, None] * inv], axis=1)
    cos, sin = np.cos(ang).astype(np.float32), np.sin(ang).astype(np.float32)
    half = MLA_ROPE // 2
    ta = np.zeros((n_tok, LANES), np.float32)
    tb = np.zeros((n_tok, LANES), np.float32)
    tc = np.zeros((n_tok, LANES), np.float32)
    ta[:, :half] = cos
    ta[:, half:2 * half] = cos
    tb[:, :half] = -sin
    tc[:, half:2 * half] = sin
    return jnp.asarray(ta), jnp.asarray(tb), jnp.asarray(tc)


def _identity_rope_tables(n_tok):
    ta = np.zeros((n_tok, LANES), np.float32)
    ta[:, :MLA_ROPE] = 1.0
    z = np.zeros((n_tok, LANES), np.float32)
    return jnp.asarray(ta), jnp.asarray(z), jnp.asarray(z)


def _w_in_prep_kernel(w_ref, o_ref):
    x = w_ref[0]
    zeros = lambda n: jnp.zeros((n, x.shape[1]), x.dtype)
    rows = []
    for h in range(MLA_HEADS):
        rows += [x[h * MLA_QK:(h + 1) * MLA_QK, :], zeros(QTILE_W - MLA_QK)]
    rows += [zeros(LANES), x[OFF_KPE:OFF_QNA, :], zeros(LANES - MLA_ROPE), zeros(QTILE_W),
             x[OFF_CKV:OFF_KPE, :], x[OFF_QNA:, :]]
    o_ref[...] = jnp.concatenate(rows, axis=0).astype(o_ref.dtype)


def _w_in_prep_call(w_in_t, layer, tk=256):
    _, n, d = w_in_t.shape
    return pl.pallas_call(
        _w_in_prep_kernel,
        grid=(d // tk,),
        in_specs=[pl.BlockSpec((1, n, tk), lambda i: (layer, 0, i))],
        out_specs=pl.BlockSpec((AB_PERM_W, tk), lambda i: (0, i)),
        out_shape=jax.ShapeDtypeStruct((AB_PERM_W, d), BF16),
        compiler_params=_params(("arbitrary",)),
        name="attn_w_in_prep",
    )(w_in_t)


def _kvup_kernel(x_ref, g_ref, w_ref, o_ref):
    x = x_ref[...].astype(F32)
    y = x * lax.rsqrt(jnp.mean(x * x, axis=-1, keepdims=True) + LN_EPS) * g_ref[...]
    o_ref[...] = _dot(y.astype(BF16), w_ref[...]).astype(o_ref.dtype)


def _kvup_call(p, g, w, tm, tile0=0):
    m = p.shape[0]
    r, n = w.shape
    return pl.pallas_call(
        _kvup_kernel,
        grid=(m // tm,),
        in_specs=[pl.BlockSpec((tm, r), lambda i: (i, (T_CKV - tile0) * QTILE_W // r)),
                  pl.BlockSpec((1, r), lambda i: (0, 0)),
                  pl.BlockSpec((r, n), lambda i: (0, 0))],
        out_specs=pl.BlockSpec((tm, n), lambda i: (i, 0)),
        out_shape=jax.ShapeDtypeStruct((m, n), BF16),
        compiler_params=_params(("arbitrary",)),
        name="kv_up",
    )(p, g.reshape(1, r), w)


def _permute_w_ukv(w):
    r = w.shape[0]
    w3 = w.reshape(r, MLA_HEADS, MLA_NOPE + MLA_V)
    return jnp.concatenate([w3[:, :, :MLA_NOPE].reshape(r, -1), w3[:, :, MLA_NOPE:].reshape(r, -1)],
                           axis=1).astype(BF16)


def _mla_kernel(q_ref, kn_ref, kp_ref, v_ref, kcn_ref, kcp_ref, vc_ref, o_ref, kcat, vcat):
    s_len = kn_ref.shape[0]
    n_h = kcat.shape[0]
    for h in range(n_h):
        kcat[h, :s_len, :LANES] = kn_ref[:, h * LANES:(h + 1) * LANES]
        kcat[h, :s_len, LANES:] = kp_ref[...]
        kcat[h, s_len:, :LANES] = kcn_ref[:, h * LANES:(h + 1) * LANES]
        kcat[h, s_len:, LANES:] = kcp_ref[...]
        vcat[h, :s_len, :] = v_ref[:, h * LANES:(h + 1) * LANES]
        vcat[h, s_len:, :] = vc_ref[:, h * LANES:(h + 1) * LANES]

    sub = q_ref.shape[0] // MLA_SUBTILES
    for h in range(n_h):
        for r0 in range(0, q_ref.shape[0], sub):
            s = _dot_nt(q_ref[r0:r0 + sub, h * QTILE_W:(h + 1) * QTILE_W], kcat[h])
            m = jnp.max(s, axis=-1, keepdims=True)
            p = jnp.exp2(s - m)
            l = jnp.sum(p, axis=-1, keepdims=True)
            o = _dot(p.astype(BF16), vcat[h])
            o_ref[r0:r0 + sub, h * LANES:(h + 1) * LANES] = (o / l).astype(o_ref.dtype)


def _mla_call(p_lat, kv_lat, p_ctx, kv_ctx, batch, s_len, lc, heads_per_step=2):
    hp = heads_per_step
    kpe_blk = T_KPE * 2 + 1
    v0 = MLA_HEADS // hp
    return pl.pallas_call(
        _mla_kernel,
        grid=(batch, MLA_HEADS // hp),
        in_specs=[pl.BlockSpec((s_len, hp * QTILE_W), lambda b, h: (b, h)),
                  pl.BlockSpec((s_len, hp * LANES), lambda b, h: (b, h)),
                  pl.BlockSpec((s_len, LANES), lambda b, h: (b, kpe_blk)),
                  pl.BlockSpec((s_len, hp * LANES), lambda b, h: (b, v0 + h)),
                  pl.BlockSpec((lc, hp * LANES), lambda b, h: (b, h)),
                  pl.BlockSpec((lc, LANES), lambda b, h: (b, kpe_blk - 2 * CTX_TILE0)),
                  pl.BlockSpec((lc, hp * LANES), lambda b, h: (b, v0 + h))],
        out_specs=pl.BlockSpec((s_len, hp * LANES), lambda b, h: (b, h)),
        out_shape=jax.ShapeDtypeStruct((batch * s_len, MLA_HEADS * MLA_V), BF16),
        scratch_shapes=[pltpu.VMEM((hp, s_len + lc, QTILE_W), BF16),
                        pltpu.VMEM((hp, s_len + lc, LANES), BF16)],
        compiler_params=_params(("arbitrary", "arbitrary")),
        name="mla_attention",
    )(p_lat, kv_lat, p_lat, kv_lat, kv_ctx, p_ctx, kv_ctx)


def _na_kernel(q_ref, k_ref, v_ref, kc_ref, vc_ref, bias_ref, o_ref, *, rows, var_map):
    win = NA_WR * GRID_W
    tq = NA_G * GRID_W
    for h in range(q_ref.shape[1] // LANES):
        hs = slice(h * LANES, (h + 1) * LANES)
        for g in range(rows // NA_G):
            start = int(np.clip(NA_G * g - NA_KH // 2, 0, rows - NA_WR)) * GRID_W
            q = q_ref[g * tq:(g + 1) * tq, hs]
            sw = _dot_nt(q, k_ref[start:start + win, hs]) + bias_ref[var_map[g], h]
            sc = _dot_nt(q, kc_ref[:, hs])
            m = jnp.maximum(jnp.max(sw, axis=-1, keepdims=True), jnp.max(sc, axis=-1, keepdims=True))
            pw = jnp.exp2(sw - m)
            pc = jnp.exp2(sc - m)
            l = jnp.sum(pw, axis=-1, keepdims=True) + jnp.sum(pc, axis=-1, keepdims=True)
            o = _dot(pw.astype(BF16), v_ref[start:start + win, hs]) + _dot(pc.astype(BF16), vc_ref[:, hs])
            o_ref[g * tq:(g + 1) * tq, hs] = (o / l).astype(o_ref.dtype)


def _na_tables(rows):
    kh = min(NA_KH, rows)
    ng = rows // NA_G
    qr = np.arange(NA_G)[:, None]
    kr = np.arange(NA_WR)[None, :]
    sel_rows = []
    for g in range(ng):
        start_row = int(np.clip(NA_G * g - NA_KH // 2, 0, rows - NA_WR))
        r = NA_G * g + qr
        krow = start_row + kr
        rs = np.clip(r - kh // 2, 0, rows - kh)
        ok = (krow >= rs) & (krow < rs + kh)
        dr = krow - r + (NA_KH - 1)
        sel_rows.append(np.stack([ok & (dr == d) for d in range(2 * NA_KH - 1)]))
    sel_rows = np.stack(sel_rows)
    uniq, inverse = np.unique(sel_rows.reshape(ng, -1), axis=0, return_inverse=True)
    er = uniq.reshape((-1,) + sel_rows.shape[1:]).astype(np.float32)
    qc = np.arange(GRID_W)[:, None]
    kc = np.arange(GRID_W)[None, :]
    cs = np.clip(qc - NA_KW // 2, 0, GRID_W - NA_KW)
    ok_c = (kc >= cs) & (kc < cs + NA_KW)
    dc = np.clip(kc - qc, -(NA_KW - 1), NA_KW - 1) + (NA_KW - 1)
    ec = np.stack([ok_c & (dc == e) for e in range(2 * NA_KW - 1)]).astype(np.float32)
    return inverse.reshape(-1).astype(np.int32), er, ec


def _na_bias(rpb, er, ec):
    hp = lax.Precision.HIGHEST
    t = jnp.einsum('hde,eqk->hdqk', rpb.astype(F32), jnp.asarray(ec), precision=hp)
    t = jnp.where(jnp.asarray(ec.sum(axis=0) > 0.5), t * LOG2E, NEG_INF)
    outside = jnp.full(t[:, 0].shape, NEG_INF, F32)
    d_of = np.where(er.sum(axis=1) > 0.5, er.argmax(axis=1), -1)
    block = lambda d: outside if d < 0 else t[:, d]
    return jnp.stack([
        jnp.concatenate([jnp.concatenate([block(int(d)) for d in row], axis=-1) for row in var], axis=-2)
        for var in d_of])


def _na_call(p_lat, p_ctx, bias, var_map, batch, s_len, lc):
    rows = s_len // GRID_W
    hp = NA_HEADS_PER_STEP
    tw = hp * LANES
    tq = NA_G * GRID_W
    win = NA_WR * GRID_W
    qb, kb, vb = (T_QNA * QTILE_W // tw, T_KNA * QTILE_W // tw, T_VNA * QTILE_W // tw)
    cb = CTX_TILE0 * QTILE_W // tw
    n_var = bias.shape[0]
    return pl.pallas_call(
        functools.partial(_na_kernel, rows=rows, var_map=tuple(int(v) for v in var_map)),
        grid=(batch, NA_HEADS // hp),
        in_specs=[pl.BlockSpec((s_len, tw), lambda b, h: (b, qb + h)),
                  pl.BlockSpec((s_len, tw), lambda b, h: (b, kb + h)),
                  pl.BlockSpec((s_len, tw), lambda b, h: (b, vb + h)),
                  pl.BlockSpec((lc, tw), lambda b, h: (b, kb + h - cb)),
                  pl.BlockSpec((lc, tw), lambda b, h: (b, vb + h - cb)),
                  pl.BlockSpec((n_var, hp, tq, win), lambda b, h: (0, h, 0, 0))],
        out_specs=pl.BlockSpec((s_len, tw), lambda b, h: (b, h)),
        out_shape=jax.ShapeDtypeStruct((batch * s_len, NA_HEADS * NA_HEAD_DIM), BF16),
        compiler_params=_params(("arbitrary", "arbitrary")),
        name="na_attention",
    )(p_lat, p_lat, p_lat, p_ctx, p_ctx, bias)


def _layer_norm(x, g, b):
    mu = jnp.mean(x, axis=-1, keepdims=True)
    xc = x - mu
    var = jnp.mean(xc * xc, axis=-1, keepdims=True)
    return xc * lax.rsqrt(var + LN_EPS) * g + b


def _outln_kernel(a1_ref, a2_ref, w_ref, h_ref, mod_ref, ln_ref, r_ref, h1_ref, u2_ref, lg_ref, wb_ref):
    half = a1_ref.shape[1]

    @pl.when(pl.program_id(0) == 0)
    def _():
        rows = wb_ref.shape[0] // 4
        for r0 in range(0, wb_ref.shape[0], rows):
            wb_ref[r0:r0 + rows, :] = w_ref[0, r0:r0 + rows, :].astype(wb_ref.dtype)

    sub = a1_ref.shape[0] // OUTLN_SUBTILES
    for r0 in range(0, a1_ref.shape[0], sub):
        rs = slice(r0, r0 + sub)
        y = _dot(a1_ref[rs, :], wb_ref[:half, :]) + _dot(a2_ref[rs, :], wb_ref[half:, :])
        x = DEEPNORM_ALPHA * h_ref[rs, :] + mod_ref[0, 0:1, :] * y
        hn = _layer_norm(x, ln_ref[0:1, :], ln_ref[1:2, :])
        h1_ref[rs, :] = hn
        u = hn * (1.0 + mod_ref[0, 2:3, :]) + mod_ref[0, 1:2, :]
        u_bf = u.astype(BF16)
        u2_ref[rs, :] = u_bf
        lg_ref[rs, :] = _dot(u_bf, r_ref[...])


def _outln_call(a1, a2, w, layer, h, mod, ln, router2, rows_per_group, tm=512):
    m, d = h.shape
    half = a1.shape[1]
    tpg = rows_per_group // tm
    return pl.pallas_call(
        _outln_kernel,
        grid=(m // tm,),
        in_specs=[pl.BlockSpec((tm, half), lambda i: (i, 0)),
                  pl.BlockSpec((tm, half), lambda i: (i, 0)),
                  pl.BlockSpec((1, 2 * half, d), lambda i: (layer, 0, 0), pipeline_mode=pl.Buffered(1)),
                  pl.BlockSpec((tm, d), lambda i: (i, 0)),
                  pl.BlockSpec((1, 3, d), lambda i: (i // tpg, 0, 0)),
                  pl.BlockSpec((2, d), lambda i: (0, 0)),
                  _resident_spec((d, LANES))],
        out_specs=[pl.BlockSpec((tm, d), lambda i: (i, 0)),
                   pl.BlockSpec((tm, d), lambda i: (i, 0)),
                   pl.BlockSpec((tm, LANES), lambda i: (i, 0))],
        out_shape=[jax.ShapeDtypeStruct((m, d), F32),
                   jax.ShapeDtypeStruct((m, d), BF16),
                   jax.ShapeDtypeStruct((m, LANES), F32)],
        scratch_shapes=[pltpu.VMEM((2 * half, d), BF16)],
        compiler_params=_params(("arbitrary",)),
        name="out_proj_postnorm",
    )(a1, a2, w, h, mod, ln, router2)


def _router_operand(router):
    d, e = router.shape
    return jnp.pad(router, ((0, 0), (0, LANES - e))).astype(BF16)


def _route_kernel(lg_ref, slot_ref, aff_ref, cnt_ref, tri_ref, *, cap):
    n_tok = lg_ref.shape[2]

    @pl.when(pl.program_id(0) == 0)
    def _():
        chunk = 256
        for r0 in range(0, n_tok, chunk):
            r = r0 + lax.broadcasted_iota(I32, (chunk, n_tok), 0)
            c = lax.broadcasted_iota(I32, (chunk, n_tok), 1)
            tri_ref[r0:r0 + chunk, :] = jnp.where(r < c, 1.0, 0.0).astype(BF16)

    lg = lg_ref[0]
    ex = jnp.exp(lg - jnp.max(lg, axis=0, keepdims=True))
    aff = ex / jnp.sum(ex, axis=0, keepdims=True)
    bits = lax.bitcast_convert_type(aff, I32)
    n_e = lg.shape[0]
    count = lambda mask: jnp.sum(jnp.where(mask, 1.0, 0.0), axis=1, keepdims=True)

    def body(_, lohi):
        lo, hi = lohi
        mid = lo + jnp.right_shift(hi - lo, 1)
        ge = count(bits >= mid) >= cap
        return jnp.where(ge, mid, lo), jnp.where(ge, hi, mid)

    lo0 = jnp.zeros((n_e, 1), I32)
    hi0 = jnp.full((n_e, 1), 0x7F800000, I32)
    thr, _ = lax.fori_loop(0, 31, body, (lo0, hi0))
    gt = bits > thr
    eq = bits == thr
    need = cap - count(gt)
    pre_eq = _dot(jnp.where(eq, 1.0, 0.0).astype(BF16), tri_ref[...])
    sel = jnp.logical_or(gt, jnp.logical_and(eq, pre_eq < need))
    sel_bf = jnp.where(sel, 1.0, 0.0).astype(BF16)
    slot = _dot(sel_bf, tri_ref[...])
    slot_ref[0] = jnp.where(sel, slot.astype(I32), -1)
    aff_ref[0] = aff
    r = lax.broadcasted_iota(I32, (n_tok, LANES), 0)
    c = lax.broadcasted_iota(I32, (n_tok, LANES), 1)
    before = jnp.where(r < c * MOE_CHUNK, 1.0, 0.0).astype(BF16)
    cnt_ref[0] = _dot(sel_bf, before).astype(I32)


def _route_call(lg_t, cap):
    b, e, n_tok = lg_t.shape
    spec = pl.BlockSpec((1, e, n_tok), lambda i: (i, 0, 0))
    return pl.pallas_call(
        functools.partial(_route_kernel, cap=cap),
        grid=(b,),
        in_specs=[spec],
        out_specs=[spec, spec, pl.BlockSpec((1, e, LANES), lambda i: (i, 0, 0))],
        out_shape=[jax.ShapeDtypeStruct((b, e, n_tok), I32), jax.ShapeDtypeStruct((b, e, n_tok), F32),
                   jax.ShapeDtypeStruct((b, e, LANES), I32)],
        scratch_shapes=[pltpu.VMEM((n_tok, n_tok), BF16)],
        compiler_params=_params(("arbitrary",)),
        name="ec_route",
    )(lg_t)


def _slot_windows(cnt, cap, n_tok, chunk, win):
    bounds = cnt[:, :, 0:n_tok // MOE_CHUNK + 1:chunk // MOE_CHUNK]
    start, end = bounds[:, :, :-1], bounds[:, :, 1:]
    a = jnp.minimum((start // 16) * 16, cap - win)
    fits = jnp.all(end <= a + win, axis=1)
    return jnp.swapaxes(a, 1, 2).reshape(-1).astype(I32), fits.reshape(-1).astype(I32)


def _gather_kernel(win_ref, fit_ref, slot_ref, u_ref, o_ref, *, win, chunk):
    b, ks = pl.program_id(0), pl.program_id(1)
    n_e, cap = o_ref.shape[0], o_ref.shape[1]
    n_sub = u_ref.shape[0] // chunk
    step = b * pl.num_programs(1) + ks

    @pl.when(ks == 0)
    def _():
        o_ref[...] = jnp.zeros(o_ref.shape, o_ref.dtype)

    def place(rows, start_of, group, tok0, n_t):
        row = lax.broadcasted_iota(I32, (rows, n_t), 0)
        for g0 in range(0, n_e, group):
            starts = [start_of(e) for e in range(g0, g0 + group)]
            onehot = jnp.concatenate(
                [jnp.where(row + st == slot_ref[0, e:e + 1, tok0:tok0 + n_t], 1.0, 0.0).astype(BF16)
                 for st, e in zip(starts, range(g0, g0 + group))], axis=0)
            got = _dot(onehot, u_ref[tok0:tok0 + n_t, :]).astype(o_ref.dtype)
            for i, (st, e) in enumerate(zip(starts, range(g0, g0 + group))):
                o_ref[e, pl.ds(st, rows), :] += got[i * rows:(i + 1) * rows]

    for sub in range(n_sub):
        c = step * n_sub + sub

        @pl.when(fit_ref[c] != 0)
        def _(c=c, sub=sub):
            place(win, lambda e: pl.multiple_of(win_ref[c * n_e + e], 16), 512 // win, sub * chunk, chunk)

        @pl.when(fit_ref[c] == 0)
        def _(sub=sub):
            place(cap, lambda e: 0, 2, sub * chunk, chunk)


def _gather_call(starts, fit, slot, u2, cap, chunk, win, chunks_per_step=2):
    b, e, n_tok = slot.shape
    d = u2.shape[1]
    tt = chunk * chunks_per_step
    nc = n_tok // tt
    grid_spec = pltpu.PrefetchScalarGridSpec(
        num_scalar_prefetch=2,
        grid=(b, nc),
        in_specs=[pl.BlockSpec((1, e, tt), lambda i, k, w, f: (i, 0, k)),
                  pl.BlockSpec((tt, d), lambda i, k, w, f: (i * nc + k, 0))],
        out_specs=pl.BlockSpec((e, cap, d), lambda i, k, w, f: (0, i, 0)),
    )
    return pl.pallas_call(
        functools.partial(_gather_kernel, win=win, chunk=chunk),
        grid_spec=grid_spec,
        out_shape=jax.ShapeDtypeStruct((e, b * cap, d), BF16),
        compiler_params=_params(("arbitrary", "arbitrary")),
        name="moe_gather",
    )(starts, fit, slot, u2)


def _expert_hidden_kernel(x_ref, w1_ref, w3_ref, o_ref, *, ff):
    tf = w1_ref.shape[3]
    last = pl.num_programs(1) - 1

    def block(width):
        w = jnp.concatenate([w1_ref[0, 0, :, :width].astype(BF16), w3_ref[0, 0, :, :width].astype(BF16)], axis=1)
        h = _dot(x_ref[0], w)
        h1 = h[:, :width]
        o_ref[0, :, :width] = ((h1 / (1.0 + jnp.exp(-h1))) * h[:, width:]).astype(o_ref.dtype)

    tail = ff - (pl.cdiv(ff, tf) - 1) * tf
    if tail == tf:
        block(tf)
    else:
        pl.when(pl.program_id(1) < last)(lambda: block(tf))
        pl.when(pl.program_id(1) == last)(lambda: block(tail))


def _expert_out_kernel(hd_ref, w2_ref, o_ref):
    y = _dot(hd_ref[0], w2_ref[0, 0].astype(BF16)).astype(o_ref.dtype)
    cap = o_ref.shape[1]
    for b in range(o_ref.shape[0]):
        o_ref[b] = y[b * cap:(b + 1) * cap]


def _expert_call(xe, w1, w3, w2, layer, batch, tf=768, tn=2048):
    e, m, d = xe.shape
    cap = m // batch
    ff = w1.shape[3]
    hidden = pl.pallas_call(
        functools.partial(_expert_hidden_kernel, ff=ff),
        grid=(e, pl.cdiv(ff, tf)),
        in_specs=[pl.BlockSpec((1, m, d), lambda i, f: (i, 0, 0)),
                  pl.BlockSpec((1, 1, d, tf), lambda i, f: (layer, i, 0, f)),
                  pl.BlockSpec((1, 1, d, tf), lambda i, f: (layer, i, 0, f))],
        out_specs=pl.BlockSpec((1, m, tf), lambda i, f: (i, 0, f)),
        out_shape=jax.ShapeDtypeStruct((e, m, ff), BF16),
        compiler_params=_params(("arbitrary", "arbitrary")),
        name="moe_expert_hidden",
    )(xe, w1, w3)
    return pl.pallas_call(
        _expert_out_kernel,
        grid=(e, d // tn),
        in_specs=[pl.BlockSpec((1, m, ff), lambda i, j: (i, 0, 0)),
                  pl.BlockSpec((1, 1, ff, tn), lambda i, j: (layer, i, 0, j))],
        out_specs=pl.BlockSpec((batch, cap, tn), lambda i, j: (0, i, j)),
        out_shape=jax.ShapeDtypeStruct((batch, e * cap, d), BF16),
        compiler_params=_params(("arbitrary", "arbitrary")),
        name="moe_expert_out",
    )(hidden, w2)


def _scatter_kernel(win_ref, fit_ref, slot_ref, aff_ref, ye_ref, h_ref, mod_ref, ln_ref, *rest, with_next):
    if with_next:
        nmod_ref, h2_ref, un_ref, pt_ref, yew_ref, moe_ref = rest
    else:
        h2_ref, pt_ref, yew_ref, moe_ref = rest
    n_e = slot_ref.shape[1]
    cap = ye_ref.shape[1] // n_e
    tt = h_ref.shape[0]
    step = pl.program_id(0) * pl.num_programs(1) + pl.program_id(1)

    def gates(rows, e, start):
        row = lax.broadcasted_iota(I32, (rows, tt), 0)
        return jnp.where(row + start == slot_ref[0, e:e + 1, :], aff_ref[0, e:e + 1, :], 0.0).astype(BF16)

    @pl.when(fit_ref[step] != 0)
    def _():
        for e in range(n_e):
            start = pl.multiple_of(win_ref[step * n_e + e], 16)
            pt_ref[e * MOE_WIN:(e + 1) * MOE_WIN, :] = gates(MOE_WIN, e, start)
            yew_ref[e * MOE_WIN:(e + 1) * MOE_WIN, :] = ye_ref[0, pl.ds(e * cap + start, MOE_WIN), :]
        moe_ref[...] = _dot_tn(pt_ref[:n_e * MOE_WIN, :], yew_ref[...])

    @pl.when(fit_ref[step] == 0)
    def _():
        for e in range(n_e):
            pt_ref[e * cap:(e + 1) * cap, :] = gates(cap, e, 0)
        moe_ref[...] = _dot_tn(pt_ref[...], ye_ref[0])

    x = DEEPNORM_ALPHA * h_ref[...] + mod_ref[0] * moe_ref[...]
    hn = _layer_norm(x, ln_ref[0:1, :], ln_ref[1:2, :])
    h2_ref[...] = hn
    if with_next:
        un_ref[...] = (hn * (1.0 + nmod_ref[0, 1:2, :]) + nmod_ref[0, 0:1, :]).astype(un_ref.dtype)


def _scatter_call(win, fit, slot, aff, ye, h1, gate, ln, next_ss):
    b, e, n_tok = slot.shape
    m, d = h1.shape
    tt = MOE_CHUNK
    nt = n_tok // tt
    rows = ye.shape[1]
    with_next = next_ss is not None
    tok_spec = pl.BlockSpec((1, e, tt), lambda i, t, w, f: (i, 0, t))
    row_spec = pl.BlockSpec((tt, d), lambda i, t, w, f: (i * nt + t, 0))
    in_specs = [tok_spec, tok_spec,
                pl.BlockSpec((1, rows, d), lambda i, t, w, f: (i, 0, 0), pipeline_mode=pl.Buffered(1)),
                row_spec,
                pl.BlockSpec((1, 1, d), lambda i, t, w, f: (i, 0, 0)),
                pl.BlockSpec((2, d), lambda i, t, w, f: (0, 0))]
    args = [slot, aff, ye, h1, gate, ln]
    out_specs = [row_spec]
    out_shape = [jax.ShapeDtypeStruct((m, d), F32)]
    if with_next:
        in_specs.append(pl.BlockSpec((1, 2, d), lambda i, t, w, f: (i, 0, 0)))
        args.append(next_ss)
        out_specs.append(row_spec)
        out_shape.append(jax.ShapeDtypeStruct((m, d), BF16))
    grid_spec = pltpu.PrefetchScalarGridSpec(
        num_scalar_prefetch=2,
        grid=(b, nt),
        in_specs=in_specs,
        out_specs=out_specs,
        scratch_shapes=[pltpu.VMEM((rows, tt), BF16),
                        pltpu.VMEM((e * MOE_WIN, d), BF16),
                        pltpu.VMEM((tt, d), F32)],
    )
    res = pl.pallas_call(
        functools.partial(_scatter_kernel, with_next=with_next),
        grid_spec=grid_spec,
        out_shape=out_shape,
        compiler_params=_params(("arbitrary", "arbitrary")),
        name="moe_combine_postnorm",
    )(win, fit, *args)
    return res if with_next else (res[0], None)


def _ec_moe(h1, u2, logits, w1, w3, w2, layer, gate, ln, next_ss, batch, n_tok):
    cap = EC_CAPACITY_FACTOR * n_tok // N_EXPERTS
    lg_t = jnp.swapaxes(logits[:, :N_EXPERTS].reshape(batch, n_tok, N_EXPERTS), 1, 2)
    slot, aff, cnt = _route_call(lg_t, cap)
    g_win, g_fit = _slot_windows(cnt, cap, n_tok, GATHER_CHUNK, GATHER_WIN)
    xe = _gather_call(g_win, g_fit, slot, u2, cap, GATHER_CHUNK, GATHER_WIN)
    ye = _expert_call(xe, w1, w3, w2, layer, batch)
    c_win, c_fit = _slot_windows(cnt, cap, n_tok, MOE_CHUNK, MOE_WIN)
    return _scatter_call(c_win, c_fit, slot, aff, ye, h1, gate, ln, next_ss)


def _mm_kernel(a_ref, w_ref, o_ref, wb_ref):
    @pl.when(pl.program_id(1) == 0)
    def _():
        wb_ref[...] = w_ref[0].astype(wb_ref.dtype)

    o_ref[...] = _dot(a_ref[...], wb_ref[...]).astype(o_ref.dtype)


def _mm_call(a, w, layer, out_dtype, tm=1024, tn=1024):
    m, k = a.shape
    n = w.shape[2]
    return pl.pallas_call(
        _mm_kernel,
        grid=(n // tn, m // tm),
        in_specs=[pl.BlockSpec((tm, k), lambda j, i: (i, 0)),
                  pl.BlockSpec((1, k, tn), lambda j, i: (layer, 0, j))],
        out_specs=pl.BlockSpec((tm, tn), lambda j, i: (i, j)),
        out_shape=jax.ShapeDtypeStruct((m, n), out_dtype),
        scratch_shapes=[pltpu.VMEM((k, tn), BF16)],
        compiler_params=_params(("arbitrary", "arbitrary")),
        name="matmul",
    )(a, w)


def _short_conv(p, w, b):
    x = p.astype(F32)
    n = x.shape[0]
    r = lax.broadcasted_iota(I32, x.shape, 0)
    prev = jnp.where(r == 0, 0.0, pltpu.roll(x, 1, 0))
    nxt = jnp.where(r == n - 1, 0.0, pltpu.roll(x, n - 1, 0))
    return b + w[0:1, :] * prev + w[1:2, :] * x + w[2:3, :] * nxt


def _sconv_kernel(p_ref, w_ref, b_ref, o_ref):
    o_ref[...] = _short_conv(p_ref[...], w_ref[...], b_ref[...]).astype(o_ref.dtype)


def _sconv_call(p, conv_w, conv_b, col0, ncols, out_dtype, batch, n_tok, tc=512):
    c0 = col0 // tc
    return pl.pallas_call(
        _sconv_kernel,
        grid=(batch, ncols // tc),
        in_specs=[pl.BlockSpec((n_tok, tc), lambda b, j: (b, c0 + j)),
                  pl.BlockSpec((HY_SHORT, tc), lambda b, j: (0, c0 + j)),
                  pl.BlockSpec((1, tc), lambda b, j: (0, c0 + j))],
        out_specs=pl.BlockSpec((n_tok, tc), lambda b, j: (b, j)),
        out_shape=jax.ShapeDtypeStruct((batch * n_tok, ncols), out_dtype),
        compiler_params=_params(("arbitrary", "arbitrary")),
        name="hyena_short_conv",
    )(p, conv_w, conv_b.reshape(1, -1))


def _dft_fwd_kernel(fc_ref, fs_ref, r1_ref, r2_ref, *rest, spectral):
    xr = _dot(fc_ref[...], r1_ref[...])
    xi = _dot(fs_ref[...], r2_ref[...])
    if spectral:
        h_ref, nyq_ref, o_ref = rest
        hr = h_ref[0, 0]
        hi = h_ref[0, 1]
        dc = lax.broadcasted_iota(I32, xr.shape, 0) == 0
        yr = xr * hr - jnp.where(dc, 0.0, xi * hi)
        yi = jnp.where(dc, xi * nyq_ref[...], xr * hi + xi * hr)
    else:
        (o_ref,) = rest
        yr, yi = xr, xi
    o_ref[0, 0] = yr.astype(o_ref.dtype)
    o_ref[0, 1] = yi.astype(o_ref.dtype)


def _resident_spec(shape):
    return pl.BlockSpec(shape, lambda *_: (0,) * len(shape), pipeline_mode=pl.Buffered(1))


def _dft_fwd_call(fc, fs, r1, r2, spec, out_dtype, batch, n_tok, tn=256):
    ncols = r1.shape[1]
    r_spec = pl.BlockSpec((n_tok, tn), lambda j, b: (b, j))
    in_specs = [_resident_spec(fc.shape), _resident_spec(fs.shape), r_spec, r_spec]
    args = [fc, fs, r1, r2]
    if spec is not None:
        h, nyq, hcol0 = spec
        c0 = hcol0 // tn
        in_specs += [pl.BlockSpec((1, 2, n_tok, tn), lambda j, b: (0, 0, 0, c0 + j)),
                     pl.BlockSpec((1, tn), lambda j, b: (0, c0 + j))]
        args += [h, nyq]
    return pl.pallas_call(
        functools.partial(_dft_fwd_kernel, spectral=spec is not None),
        grid=(ncols // tn, batch),
        in_specs=in_specs,
        out_specs=pl.BlockSpec((1, 2, n_tok, tn), lambda j, b: (b, 0, 0, j)),
        out_shape=jax.ShapeDtypeStruct((batch, 2, n_tok, ncols), out_dtype),
        compiler_params=_params(("arbitrary", "arbitrary")),
        name="dft_forward",
    )(*args)


def _dual_kernel(a1_ref, a2_ref, b1_ref, b2_ref, *rest, hyena, planes):
    b1 = b1_ref[0, 0] if planes else b1_ref[...]
    b2 = b2_ref[0, 0] if planes else b2_ref[...]
    y = _dot(a1_ref[...], b1) + _dot(a2_ref[...], b2)
    if hyena:
        p_ref, cw_ref, cb_ref, z_ref, skip_ref, o_ref = rest
        gate = _short_conv(p_ref[...], cw_ref[...], cb_ref[...])
        y = gate * (y + skip_ref[...] * z_ref[...].astype(F32))
    else:
        (o_ref,) = rest
    o_ref[...] = y.astype(o_ref.dtype)


def _dual_call(a1, a2, b1, b2, hy, batch, n_tok, ncols, tn=256):
    planes = b1.ndim == 4
    if planes:
        b_specs = [pl.BlockSpec((1, 1, n_tok, tn), lambda b, j: (b, 0, 0, j)),
                   pl.BlockSpec((1, 1, n_tok, tn), lambda b, j: (b, 1, 0, j))]
    else:
        b_specs = [pl.BlockSpec((n_tok, tn), lambda b, j: (b, j))] * 2
    in_specs = [_resident_spec(a1.shape), _resident_spec(a2.shape)] + b_specs
    args = [a1, a2, b1, b2]
    o_spec = pl.BlockSpec((n_tok, tn), lambda b, j: (b, j))
    if hy is not None:
        p, conv_w, conv_b, pcol0, z, skip = hy
        c0 = pcol0 // tn
        in_specs += [pl.BlockSpec((n_tok, tn), lambda b, j: (b, c0 + j)),
                     pl.BlockSpec((HY_SHORT, tn), lambda b, j: (0, c0 + j)),
                     pl.BlockSpec((1, tn), lambda b, j: (0, c0 + j)),
                     o_spec,
                     pl.BlockSpec((1, tn), lambda b, j: (0, j))]
        args += [p, conv_w, conv_b.reshape(1, -1), z, skip]
    return pl.pallas_call(
        functools.partial(_dual_kernel, hyena=hy is not None, planes=planes),
        grid=(batch, ncols // tn),
        in_specs=in_specs,
        out_specs=o_spec,
        out_shape=jax.ShapeDtypeStruct((batch * n_tok, ncols), BF16),
        compiler_params=_params(("arbitrary", "arbitrary")),
        name="seq_mix_matmul",
    )(*args)


def _fnet_kernel(cl_ref, msl_ref, x_ref, cw_ref, o_ref):
    gw = cw_ref.shape[1]
    for c0 in range(0, x_ref.shape[1], gw):
        x = x_ref[:, c0:c0 + gw]
        seq = jnp.concatenate([_dot(cl_ref[...], x).astype(BF16), _dot(msl_ref[...], x).astype(BF16)], axis=1)
        o_ref[:, c0:c0 + gw] = _dot(seq, cw_ref[...]).astype(o_ref.dtype)


def _fnet_call(cl, msl, p, cw, col0, batch, n_tok, groups_per_step=2):
    gw = cw.shape[1]
    tw = gw * groups_per_step
    c0 = col0 // tw
    return pl.pallas_call(
        _fnet_kernel,
        grid=(batch, FN_GROUPS // groups_per_step),
        in_specs=[_resident_spec(cl.shape), _resident_spec(msl.shape),
                  pl.BlockSpec((n_tok, tw), lambda b, g: (b, c0 + g)),
                  pl.BlockSpec(cw.shape, lambda b, g: (0, 0))],
        out_specs=pl.BlockSpec((n_tok, tw), lambda b, g: (b, g)),
        out_shape=jax.ShapeDtypeStruct((batch * n_tok, FN_GROUPS * gw), BF16),
        compiler_params=_params(("arbitrary", "arbitrary")),
        name="fnet_mix",
    )(cl, msl, p, cw)


def _cos_sin_matrix(n_rows, n_cols, period, split=64):
    r = np.arange(n_rows, dtype=np.int64)[:, None]
    c_hi = (np.arange(n_cols // split, dtype=np.int64) * split)[None, :]
    c_lo = np.arange(split, dtype=np.int64)[None, :]
    ang = lambda c: 2.0 * np.pi * ((r * c) % period).astype(np.float64) / period
    ca, sa = jnp.asarray(np.cos(ang(c_hi)), F32), jnp.asarray(np.sin(ang(c_hi)), F32)
    cb, sb = jnp.asarray(np.cos(ang(c_lo)), F32), jnp.asarray(np.sin(ang(c_lo)), F32)
    cos = ca[:, :, None] * cb[:, None, :] - sa[:, :, None] * sb[:, None, :]
    sin = sa[:, :, None] * cb[:, None, :] + ca[:, :, None] * sb[:, None, :]
    return cos.reshape(n_rows, n_cols), sin.reshape(n_rows, n_cols)


def _trig_kernel(ta_ref, tb_ref, ea_ref, eb_ref, *o_refs, mode, period, scale):
    sa_ca = _dot(ta_ref[...], ea_ref[...])
    sb_cb = _dot(tb_ref[...], eb_ref[...])
    nc = sa_ca.shape[1] // 2
    ca, sa, cb, sb = sa_ca[:, :nc], sa_ca[:, nc:], sb_cb[:, :nc], sb_cb[:, nc:]
    cos = ca * cb - sa * sb
    sin = sa * cb + ca * sb
    tm = cos.shape[0]
    row = pl.program_id(0) * tm + lax.broadcasted_iota(I32, cos.shape, 0)
    col = lax.broadcasted_iota(I32, cos.shape, 1)
    alt = lambda idx: (1 - 2 * jnp.bitwise_and(idx, 1)).astype(F32)
    if mode == "hyena":
        fc_ref, fs_ref, gc_ref, gs_ref = o_refs
        fc_ref[...] = cos.astype(fc_ref.dtype)
        fs_ref[...] = jnp.where(row == 0, alt(col), -sin).astype(fs_ref.dtype)
        gc_ref[...] = (cos * jnp.where(col == 0, 1.0 / period, 2.0 / period)).astype(gc_ref.dtype)
        gs_ref[...] = jnp.where(col == 0, alt(row) * (1.0 / period), sin * (-2.0 / period)).astype(gs_ref.dtype)
    else:
        c_ref, ms_ref = o_refs
        c_ref[...] = (cos * scale).astype(c_ref.dtype)
        ms_ref[...] = (sin * (-scale)).astype(ms_ref.dtype)


def _split3(x):
    bf = jnp.bfloat16
    x = np.asarray(x, np.float32)
    hi = x.astype(bf)
    r1 = x - hi.astype(np.float32)
    mid = r1.astype(bf)
    lo = (r1 - mid.astype(np.float32)).astype(bf)
    return [hi, mid, lo]


def _trig_call(n, period, mode, scale=1.0, split=64, tm=256):
    r = np.arange(n, dtype=np.int64)[:, None]
    c_hi = (np.arange(n // split, dtype=np.int64) * split)[None, :]
    c_lo = np.arange(split, dtype=np.int64)[None, :]
    ang = lambda c: 2.0 * np.pi * ((r * c) % period).astype(np.float64) / period
    def operands(a, col_group):
        terms = _split3(np.cos(a)) + _split3(np.sin(a))
        k = a.shape[1]
        spread = (col_group[None, :] == np.arange(k)[:, None]).astype(np.float32)
        zero = np.zeros_like(spread)
        e = np.concatenate([np.concatenate([spread, zero], 1)] * 3 + [np.concatenate([zero, spread], 1)] * 3, 0)
        return jnp.asarray(np.concatenate(terms, axis=1)), jnp.asarray(e, BF16)
    cols = np.arange(n)
    ta, ea = operands(ang(c_hi), cols // split)
    tb, eb = operands(ang(c_lo), cols % split)
    n_out = 4 if mode == "hyena" else 2
    o_spec = pl.BlockSpec((tm, n), lambda i: (i, 0))
    return pl.pallas_call(
        functools.partial(_trig_kernel, mode=mode, period=period, scale=scale),
        grid=(n // tm,),
        in_specs=[pl.BlockSpec((tm, ta.shape[1]), lambda i: (i, 0)),
                  pl.BlockSpec((tm, tb.shape[1]), lambda i: (i, 0)),
                  pl.BlockSpec(ea.shape, lambda i: (0, 0)),
                  pl.BlockSpec(eb.shape, lambda i: (0, 0))],
        out_specs=[o_spec] * n_out,
        out_shape=[jax.ShapeDtypeStruct((n, n), BF16)] * n_out,
        compiler_params=_params(("arbitrary",)),
        name="dft_tables",
    )(ta, tb, ea, eb)


def _hyena_dft_operands(n_tok):
    return _trig_call(n_tok, 2 * n_tok, "hyena")


def _fnet_operands(n_tok, gw):
    cl, msl = _trig_call(n_tok, n_tok, "fnet", scale=1.0 / math.sqrt(n_tok * gw))
    cw, sw = _cos_sin_matrix(gw, gw, gw)
    return cl, msl, jnp.concatenate([cw, sw], axis=0).astype(BF16)


def _filter_kernel(z_ref, w1_ref, b1_ref, f1_ref, w2_ref, b2_ref, f2_ref, w3f_ref, w3b_ref, dl_ref,
                   sum_ref, dif_ref, nyq_ref, hdn_ref):
    @pl.when(jnp.logical_and(pl.program_id(0) == 0, pl.program_id(1) == 0))
    def _():
        hdot = lambda a, b: jnp.dot(a, b, precision=lax.Precision.HIGHEST, preferred_element_type=F32)
        hdn = jnp.sin(f1_ref[...] * (hdot(z_ref[...], w1_ref[...]) + b1_ref[...]))
        hdn = jnp.sin(f2_ref[...] * (hdot(hdn, w2_ref[...]) + b2_ref[...]))
        hdn_ref[...] = hdn.astype(hdn_ref.dtype)

    n = hdn_ref.shape[0]
    row = lax.broadcasted_iota(I32, (n, dl_ref.shape[1]), 0)
    decay = jnp.exp(-(row.astype(F32) / (n - 1)) * dl_ref[...])
    hf = _dot(hdn_ref[...], w3f_ref[...].astype(BF16)) * decay
    hb = jnp.where(row == 0, 0.0, _dot(hdn_ref[...], w3b_ref[...].astype(BF16)) * decay)
    tot = hf + hb
    sum_ref[...] = tot.astype(sum_ref.dtype)
    dif_ref[...] = (hf - hb).astype(dif_ref.dtype)
    nyq_ref[...] = jnp.sum(jnp.where(jnp.bitwise_and(row, 1) == 0, tot, -tot), axis=0, keepdims=True)


def _filter_call(n_tok, fw1, fb1, ff1, fw2, fb2, ff2, fw3, tc=512):
    t01 = np.linspace(0.0, 1.0, n_tok, dtype=np.float32)
    w = (2.0 * math.pi * np.arange(n_tok, dtype=np.float32) / n_tok).astype(np.float32)
    bands = np.linspace(1e-4, HY_BANDS - 1, HY_BANDS, dtype=np.float32)
    z = np.concatenate([t01[:, None], np.cos(w[:, None] * bands), -np.sin(w[:, None] * bands)], -1)
    deltas = np.abs(np.linspace(HY_MIN_DECAY, HY_MAX_DECAY, HY_W, dtype=np.float32))[None, :]
    emb, hid = fw1.shape
    nc = HY_W // tc
    full = lambda shape: pl.BlockSpec(shape, lambda o, j: (0,) * len(shape))
    o_spec = pl.BlockSpec((n_tok, tc), lambda o, j: (0, o * nc + j))
    return pl.pallas_call(
        _filter_kernel,
        grid=(HY_ORDER, nc),
        in_specs=[full((n_tok, emb)), full((emb, hid)), full((1, hid)), full((1, hid)),
                  full((hid, hid)), full((1, hid)), full((1, hid)),
                  pl.BlockSpec((hid, tc), lambda o, j: (0, (2 * o) * nc + j)),
                  pl.BlockSpec((hid, tc), lambda o, j: (0, (2 * o + 1) * nc + j)),
                  pl.BlockSpec((1, tc), lambda o, j: (0, j))],
        out_specs=[o_spec, o_spec, pl.BlockSpec((1, tc), lambda o, j: (0, o * nc + j))],
        out_shape=[jax.ShapeDtypeStruct((n_tok, HY_ORDER * HY_W), BF16),
                   jax.ShapeDtypeStruct((n_tok, HY_ORDER * HY_W), BF16),
                   jax.ShapeDtypeStruct((1, HY_ORDER * HY_W), F32)],
        scratch_shapes=[pltpu.VMEM((n_tok, hid), BF16)],
        compiler_params=_params(("arbitrary", "arbitrary")),
        name="hyena_filters",
    )(jnp.asarray(z.astype(np.float32)), fw1, fb1.reshape(1, hid), ff1.reshape(1, hid),
      fw2, fb2.reshape(1, hid), ff2.reshape(1, hid), fw3, fw3, jnp.asarray(deltas))


def kernel(x, c, ctx, c_ctx, ada_w, ada_b, ln1_g, ln1_b, ln2_g, ln2_b, router, exp_w1, exp_w3, exp_w2,
           ab_w_in, ab_kv_norm, ab_w_ukv, ab_rpb, ab_w_out,
           cd_w_in, cd_conv_w, cd_conv_b, cd_filt_w1, cd_filt_b1, cd_filt_freq1, cd_filt_w2, cd_filt_b2,
           cd_filt_freq2, cd_filt_w3, cd_skip, cd_w_out):
    batch, n_tok, d = x.shape
    lc = ctx.shape[1]
    x2d = x.reshape(batch * n_tok, d)
    ctx2d = ctx.reshape(batch * lc, d)

    pad_rows = (-(batch + 1)) % 8
    cv = jnp.concatenate([c, c_ctx[None, :], jnp.zeros((pad_rows, d), F32)], axis=0)
    ada = _ada_call(cv, ada_w, ada_b)
    mods = [ada[i, :batch].reshape(batch, 6, d) for i in range(DEPTH)]
    ctx_ss = ada[0, batch, :2 * d].reshape(1, 2, d)

    w_in = _w_in_prep_call(jnp.swapaxes(ab_w_in, 1, 2), 0)
    p_lat = _mm_rope_call(x2d, mods[0][:, 0:2], n_tok, w_in, _rope_tables(n_tok), tm=1024)
    p_ctx = _mm_rope_call(ctx2d, ctx_ss, batch * lc, w_in, _identity_rope_tables(batch * lc), tm=batch * lc,
                          tile0=CTX_TILE0)
    w_ukv = _permute_w_ukv(ab_w_ukv[0])
    kv_lat = _kvup_call(p_lat, ab_kv_norm[0], w_ukv, tm=1024)
    kv_ctx = _kvup_call(p_ctx, ab_kv_norm[0], w_ukv, tm=lc, tile0=CTX_TILE0)
    a_out = _mla_call(p_lat, kv_lat, p_ctx, kv_ctx, batch, n_tok, lc)
    var_map, na_er, na_ec = _na_tables(n_tok // GRID_W)
    b_out = _na_call(p_lat, p_ctx, _na_bias(ab_rpb[0], na_er, na_ec), var_map, batch, n_tok, lc)
    ln1 = jnp.stack([ln1_g, ln1_b], axis=1)
    ln2 = jnp.stack([ln2_g, ln2_b], axis=1)
    h1, u2, logits = _outln_call(a_out, b_out, ab_w_out, 0, x2d,
                                 mods[0][:, 2:5], ln1[0], _router_operand(router[0]), n_tok)
    h, u = _ec_moe(h1, u2, logits, exp_w1, exp_w3, exp_w2, 0, mods[0][:, 5:6], ln2[0],
                   mods[1][:, 0:2], batch, n_tok)

    p1 = _mm_call(u, cd_w_in, 0, BF16)
    s0 = _sconv_call(p1, cd_conv_w[0], cd_conv_b[0], 0, HY_W, BF16, batch, n_tok)
    h_sum, h_dif, h_nyq = _filter_call(n_tok, cd_filt_w1[0], cd_filt_b1[0], cd_filt_freq1[0], cd_filt_w2[0],
                                       cd_filt_b2[0], cd_filt_freq2[0], cd_filt_w3[0])
    fc, fs, gc, gs = _hyena_dft_operands(n_tok)
    h_spec = _dft_fwd_call(fc, fs, h_sum, h_dif, None, F32, 1, n_tok)
    z = s0
    for o in range(HY_ORDER):
        y_spec = _dft_fwd_call(fc, fs, z, z, (h_spec, h_nyq, o * HY_W), BF16, batch, n_tok)
        z = _dual_call(gc, gs, y_spec, y_spec,
                       (p1, cd_conv_w[0], cd_conv_b[0], (o + 1) * HY_W, z, cd_skip[0][o:o + 1]),
                       batch, n_tok, HY_W)
    cl, msl, cw = _fnet_operands(n_tok, FN_GROUP_W)
    y_fn = _fnet_call(cl, msl, p1, cw, HY_IN_W, batch, n_tok)
    h1, u2, logits = _outln_call(z, y_fn, cd_w_out, 0, h,
                                 mods[1][:, 2:5], ln1[1], _router_operand(router[1]), n_tok)
    h, _ = _ec_moe(h1, u2, logits, exp_w1, exp_w3, exp_w2, 1, mods[1][:, 5:6], ln2[1],
                   None, batch, n_tok)
    return h.reshape(batch, n_tok, d)
```

```python
import functools
import math

import numpy as np
import jax
import jax.numpy as jnp
from jax import lax
from jax.experimental import pallas as pl
from jax.experimental.pallas import tpu as pltpu

F32 = jnp.float32
BF16 = jnp.bfloat16
I32 = jnp.int32

DEPTH = 2
GRID_W = 64
DEEPNORM_ALPHA = (2.0 * DEPTH) ** 0.25
LN_EPS = 1e-6
NEG_INF = -1e30
LOG2E = math.log2(math.e)

MLA_HEADS = 8
MLA_NOPE = 128
MLA_ROPE = 64
MLA_QK = MLA_NOPE + MLA_ROPE
MLA_V = 128
MLA_KV_RANK = 512
ROPE_THETA = 10000.0

NA_HEADS = 8
NA_HEAD_DIM = 128
NA_KH = 8
NA_KW = 16

OFF_CKV = MLA_HEADS * MLA_QK
OFF_KPE = OFF_CKV + MLA_KV_RANK
OFF_QNA = OFF_KPE + MLA_ROPE

HY_W = 1024
HY_ORDER = 2
HY_IN_W = (HY_ORDER + 1) * HY_W
HY_SHORT = 3
HY_BANDS = 16
HY_DECAY_TARGET = 1e-2
HY_MIN_DECAY = math.log(HY_DECAY_TARGET) / 1.5
HY_MAX_DECAY = math.log(HY_DECAY_TARGET) / 0.3

FN_W = 1024
FN_GROUPS = 4
FN_GROUP_W = FN_W // FN_GROUPS

N_EXPERTS = 16
EC_CAPACITY_FACTOR = 2

LANES = 128
VMEM_LIMIT = 56 * 1024 * 1024

QTILE_W = 2 * LANES
T_KPE = MLA_HEADS
T_CKV = T_KPE + 2
T_QNA = T_CKV + MLA_KV_RANK // QTILE_W
T_KNA = T_QNA + NA_HEADS * NA_HEAD_DIM // QTILE_W
T_VNA = T_KNA + NA_HEADS * NA_HEAD_DIM // QTILE_W
N_ABTILES = T_VNA + NA_HEADS * NA_HEAD_DIM // QTILE_W
AB_PERM_W = N_ABTILES * QTILE_W
CTX_TILE0 = T_KPE
INPROJ_TILES_PER_STEP = 8

MLA_SUBTILES = 8
OUTLN_SUBTILES = 2
MOE_CHUNK = 256
MOE_WIN = 64
GATHER_CHUNK = 256
GATHER_WIN = 64
NA_G = 4
NA_WR = NA_KH + NA_G - 1
NA_HEADS_PER_STEP = 2


def _params(sem, vmem=VMEM_LIMIT):
    return pltpu.CompilerParams(dimension_semantics=sem, vmem_limit_bytes=vmem)


def _dot(a, b):
    return jnp.dot(a, b, preferred_element_type=F32)


def _dot_nt(a, b):
    return lax.dot_general(a, b, (((1,), (1,)), ((), ())), preferred_element_type=F32)


def _dot_tn(a, b):
    return lax.dot_general(a, b, (((0,), (0,)), ((), ())), preferred_element_type=F32)


def _ada_kernel(c_ref, w_ref, b_ref, o_ref):
    c = c_ref[...]
    a = c / (1.0 + jnp.exp(-c))
    o_ref[0] = _dot(a.astype(BF16), w_ref[0].astype(BF16)) + b_ref[0]


def _ada_call(cv, ada_w, ada_b, tn=2048):
    depth, d, n = ada_w.shape
    rows = cv.shape[0]
    return pl.pallas_call(
        _ada_kernel,
        grid=(depth, n // tn),
        in_specs=[pl.BlockSpec((rows, d), lambda l, j: (0, 0)),
                  pl.BlockSpec((1, d, tn), lambda l, j: (l, 0, j)),
                  pl.BlockSpec((1, 1, tn), lambda l, j: (l, 0, j))],
        out_specs=pl.BlockSpec((1, rows, tn), lambda l, j: (l, 0, j)),
        out_shape=jax.ShapeDtypeStruct((depth, rows, n), F32),
        compiler_params=_params(("arbitrary", "arbitrary")),
        name="ada_params",
    )(cv, ada_w, ada_b.reshape(depth, 1, n))


def _mm_rope_kernel(x_ref, ss_ref, w_ref, ta_ref, tb_ref, tc_ref, o_ref, a_ref, *, tile0):
    @pl.when(pl.program_id(1) == 0)
    def _():
        a_ref[...] = (x_ref[...] * (1.0 + ss_ref[0, 1:2, :]) + ss_ref[0, 0:1, :]).astype(a_ref.dtype)

    a = a_ref[...]
    for q in range(INPROJ_TILES_PER_STEP):
        t = tile0 + INPROJ_TILES_PER_STEP * pl.program_id(1) + q
        c0 = q * QTILE_W
        acc = _dot_nt(a, w_ref[c0:c0 + QTILE_W, :])
        is_rope = t <= T_KPE
        f = jnp.where(t < T_KPE, LOG2E * MLA_QK ** -0.5,
                      jnp.where(jnp.logical_and(t >= T_QNA, t < T_KNA), LOG2E * NA_HEAD_DIM ** -0.5, 1.0)
                      ).astype(F32)
        hi = acc[:, LANES:]
        rot = (hi * jnp.where(is_rope, ta_ref[...], 1.0)
               + pltpu.roll(hi, LANES - MLA_ROPE // 2, 1) * jnp.where(is_rope, tb_ref[...], 0.0)
               + pltpu.roll(hi, MLA_ROPE // 2, 1) * jnp.where(is_rope, tc_ref[...], 0.0))
        o_ref[:, c0:c0 + LANES] = (acc[:, :LANES] * f).astype(o_ref.dtype)
        o_ref[:, c0 + LANES:c0 + QTILE_W] = (rot * f).astype(o_ref.dtype)


def _mm_rope_call(x, ss, rows_per_group, w, tabs, tm, tile0=0):
    m, k = x.shape
    tn = INPROJ_TILES_PER_STEP * QTILE_W
    n = w.shape[0] - tile0 * QTILE_W
    j0 = tile0 // INPROJ_TILES_PER_STEP
    ntab = tabs[0].shape[0] // tm
    tpg = rows_per_group // tm
    tab_spec = pl.BlockSpec((tm, LANES), lambda i, j: (i % ntab, 0))
    return pl.pallas_call(
        functools.partial(_mm_rope_kernel, tile0=tile0),
        grid=(m // tm, n // tn),
        in_specs=[pl.BlockSpec((tm, k), lambda i, j: (i, 0)),
                  pl.BlockSpec((1, 2, k), lambda i, j: (i // tpg, 0, 0)),
                  pl.BlockSpec((tn, k), lambda i, j: (j0 + j, 0)),
                  tab_spec, tab_spec, tab_spec],
        out_specs=pl.BlockSpec((tm, tn), lambda i, j: (i, j)),
        out_shape=jax.ShapeDtypeStruct((m, n), BF16),
        scratch_shapes=[pltpu.VMEM((tm, k), BF16)],
        compiler_params=_params(("arbitrary", "arbitrary")),
        name="attn_in_proj",
    )(x, ss, w, *tabs)


def _rope_tables(n_tok):
    t = np.arange(n_tok)
    row = (t // GRID_W).astype(np.float32)
    col = (t % GRID_W).astype(np.float32)
    n_freq = MLA_ROPE // 4
    inv = (ROPE_THETA ** (-np.arange(n_freq, dtype=np.float32) / n_freq)).astype(np.float32)
    ang = np.concatenate([row[:, None] * inv, col[:, None] * inv], axis=1)
    cos, sin = np.cos(ang).astype(np.float32), np.sin(ang).astype(np.float32)
    half = MLA_ROPE // 2
    ta = np.zeros((n_tok, LANES), np.float32)
    tb = np.zeros((n_tok, LANES), np.float32)
    tc = np.zeros((n_tok, LANES), np.float32)
    ta[:, :half] = cos
    ta[:, half:2 * half] = cos
    tb[:, :half] = -sin
    tc[:, half:2 * half] = sin
    return jnp.asarray(ta), jnp.asarray(tb), jnp.asarray(tc)


def _identity_rope_tables(n_tok):
    ta = np.zeros((n_tok, LANES), np.float32)
    ta[:, :MLA_ROPE] = 1.0
    z = np.zeros((n_tok, LANES), np.float32)
    return jnp.asarray(ta), jnp.asarray(z), jnp.asarray(z)


def _w_in_prep_kernel(w_ref, o_ref):
    x = w_ref[0]
    zeros = lambda n: jnp.zeros((n, x.shape[1]), x.dtype)
    rows = []
    for h in range(MLA_HEADS):
        rows += [x[h * MLA_QK:(h + 1) * MLA_QK, :], zeros(QTILE_W - MLA_QK)]
    rows += [zeros(LANES), x[OFF_KPE:OFF_QNA, :], zeros(LANES - MLA_ROPE), zeros(QTILE_W),
             x[OFF_CKV:OFF_KPE, :], x[OFF_QNA:, :]]
    o_ref[...] = jnp.concatenate(rows, axis=0).astype(o_ref.dtype)


def _w_in_prep_call(w_in_t, layer, tk=256):
    _, n, d = w_in_t.shape
    return pl.pallas_call(
        _w_in_prep_kernel,
        grid=(d // tk,),
        in_specs=[pl.BlockSpec((1, n, tk), lambda i: (layer, 0, i))],
        out_specs=pl.BlockSpec((AB_PERM_W, tk), lambda i: (0, i)),
        out_shape=jax.ShapeDtypeStruct((AB_PERM_W, d), BF16),
        compiler_params=_params(("arbitrary",)),
        name="attn_w_in_prep",
    )(w_in_t)


def _kvup_kernel(x_ref, g_ref, w_ref, o_ref):
    x = x_ref[...].astype(F32)
    y = x * lax.rsqrt(jnp.mean(x * x, axis=-1, keepdims=True) + LN_EPS) * g_ref[...]
    o_ref[...] = _dot(y.astype(BF16), w_ref[...]).astype(o_ref.dtype)


def _kvup_call(p, g, w, tm, tile0=0):
    m = p.shape[0]
    r, n = w.shape
    return pl.pallas_call(
        _kvup_kernel,
        grid=(m // tm,),
        in_specs=[pl.BlockSpec((tm, r), lambda i: (i, (T_CKV - tile0) * QTILE_W // r)),
                  pl.BlockSpec((1, r), lambda i: (0, 0)),
                  pl.BlockSpec((r, n), lambda i: (0, 0))],
        out_specs=pl.BlockSpec((tm, n), lambda i: (i, 0)),
        out_shape=jax.ShapeDtypeStruct((m, n), BF16),
        compiler_params=_params(("arbitrary",)),
        name="kv_up",
    )(p, g.reshape(1, r), w)


def _permute_w_ukv(w):
    r = w.shape[0]
    w3 = w.reshape(r, MLA_HEADS, MLA_NOPE + MLA_V)
    return jnp.concatenate([w3[:, :, :MLA_NOPE].reshape(r, -1), w3[:, :, MLA_NOPE:].reshape(r, -1)],
                           axis=1).astype(BF16)


def _mla_kernel(q_ref, kn_ref, kp_ref, v_ref, kcn_ref, kcp_ref, vc_ref, o_ref, kcat, vcat):
    s_len = kn_ref.shape[0]
    n_h = kcat.shape[0]
    for h in range(n_h):
        kcat[h, :s_len, :LANES] = kn_ref[:, h * LANES:(h + 1) * LANES]
        kcat[h, :s_len, LANES:] = kp_ref[...]
        kcat[h, s_len:, :LANES] = kcn_ref[:, h * LANES:(h + 1) * LANES]
        kcat[h, s_len:, LANES:] = kcp_ref[...]
        vcat[h, :s_len, :] = v_ref[:, h * LANES:(h + 1) * LANES]
        vcat[h, s_len:, :] = vc_ref[:, h * LANES:(h + 1) * LANES]

    sub = q_ref.shape[0] // MLA_SUBTILES
    for h in range(n_h):
        for r0 in range(0, q_ref.shape[0], sub):
            s = _dot_nt(q_ref[r0:r0 + sub, h * QTILE_W:(h + 1) * QTILE_W], kcat[h])
            m = jnp.max(s, axis=-1, keepdims=True)
            p = jnp.exp2(s - m)
            l = jnp.sum(p, axis=-1, keepdims=True)
            o = _dot(p.astype(BF16), vcat[h])
            o_ref[r0:r0 + sub, h * LANES:(h + 1) * LANES] = (o / l).astype(o_ref.dtype)


def _mla_call(p_lat, kv_lat, p_ctx, kv_ctx, batch, s_len, lc, heads_per_step=2):
    hp = heads_per_step
    kpe_blk = T_KPE * 2 + 1
    v0 = MLA_HEADS // hp
    return pl.pallas_call(
        _mla_kernel,
        grid=(batch, MLA_HEADS // hp),
        in_specs=[pl.BlockSpec((s_len, hp * QTILE_W), lambda b, h: (b, h)),
                  pl.BlockSpec((s_len, hp * LANES), lambda b, h: (b, h)),
                  pl.BlockSpec((s_len, LANES), lambda b, h: (b, kpe_blk)),
                  pl.BlockSpec((s_len, hp * LANES), lambda b, h: (b, v0 + h)),
                  pl.BlockSpec((lc, hp * LANES), lambda b, h: (b, h)),
                  pl.BlockSpec((lc, LANES), lambda b, h: (b, kpe_blk - 2 * CTX_TILE0)),
                  pl.BlockSpec((lc, hp * LANES), lambda b, h: (b, v0 + h))],
        out_specs=pl.BlockSpec((s_len, hp * LANES), lambda b, h: (b, h)),
        out_shape=jax.ShapeDtypeStruct((batch * s_len, MLA_HEADS * MLA_V), BF16),
        scratch_shapes=[pltpu.VMEM((hp, s_len + lc, QTILE_W), BF16),
                        pltpu.VMEM((hp, s_len + lc, LANES), BF16)],
        compiler_params=_params(("arbitrary", "arbitrary")),
        name="mla_attention",
    )(p_lat, kv_lat, p_lat, kv_lat, kv_ctx, p_ctx, kv_ctx)


def _na_kernel(q_ref, k_ref, v_ref, kc_ref, vc_ref, bias_ref, o_ref, *, rows, var_map):
    win = NA_WR * GRID_W
    tq = NA_G * GRID_W
    for h in range(q_ref.shape[1] // LANES):
        hs = slice(h * LANES, (h + 1) * LANES)
        for g in range(rows // NA_G):
            start = int(np.clip(NA_G * g - NA_KH // 2, 0, rows - NA_WR)) * GRID_W
            q = q_ref[g * tq:(g + 1) * tq, hs]
            sw = _dot_nt(q, k_ref[start:start + win, hs]) + bias_ref[var_map[g], h]
            sc = _dot_nt(q, kc_ref[:, hs])
            m = jnp.maximum(jnp.max(sw, axis=-1, keepdims=True), jnp.max(sc, axis=-1, keepdims=True))
            pw = jnp.exp2(sw - m)
            pc = jnp.exp2(sc - m)
            l = jnp.sum(pw, axis=-1, keepdims=True) + jnp.sum(pc, axis=-1, keepdims=True)
            o = _dot(pw.astype(BF16), v_ref[start:start + win, hs]) + _dot(pc.astype(BF16), vc_ref[:, hs])
            o_ref[g * tq:(g + 1) * tq, hs] = (o / l).astype(o_ref.dtype)


def _na_tables(rows):
    kh = min(NA_KH, rows)
    ng = rows // NA_G
    qr = np.arange(NA_G)[:, None]
    kr = np.arange(NA_WR)[None, :]
    sel_rows = []
    for g in range(ng):
        start_row = int(np.clip(NA_G * g - NA_KH // 2, 0, rows - NA_WR))
        r = NA_G * g + qr
        krow = start_row + kr
        rs = np.clip(r - kh // 2, 0, rows - kh)
        ok = (krow >= rs) & (krow < rs + kh)
        dr = krow - r + (NA_KH - 1)
        sel_rows.append(np.stack([ok & (dr == d) for d in range(2 * NA_KH - 1)]))
    sel_rows = np.stack(sel_rows)
    uniq, inverse = np.unique(sel_rows.reshape(ng, -1), axis=0, return_inverse=True)
    er = uniq.reshape((-1,) + sel_rows.shape[1:]).astype(np.float32)
    qc = np.arange(GRID_W)[:, None]
    kc = np.arange(GRID_W)[None, :]
    cs = np.clip(qc - NA_KW // 2, 0, GRID_W - NA_KW)
    ok_c = (kc >= cs) & (kc < cs + NA_KW)
    dc = np.clip(kc - qc, -(NA_KW - 1), NA_KW - 1) + (NA_KW - 1)
    ec = np.stack([ok_c & (dc == e) for e in range(2 * NA_KW - 1)]).astype(np.float32)
    return inverse.reshape(-1).astype(np.int32), er, ec


def _na_bias(rpb, er, ec):
    hp = lax.Precision.HIGHEST
    t = jnp.einsum('hde,eqk->hdqk', rpb.astype(F32), jnp.asarray(ec), precision=hp)
    t = jnp.where(jnp.asarray(ec.sum(axis=0) > 0.5), t * LOG2E, NEG_INF)
    outside = jnp.full(t[:, 0].shape, NEG_INF, F32)
    d_of = np.where(er.sum(axis=1) > 0.5, er.argmax(axis=1), -1)
    block = lambda d: outside if d < 0 else t[:, d]
    return jnp.stack([
        jnp.concatenate([jnp.concatenate([block(int(d)) for d in row], axis=-1) for row in var], axis=-2)
        for var in d_of])


def _na_call(p_lat, p_ctx, bias, var_map, batch, s_len, lc):
    rows = s_len // GRID_W
    hp = NA_HEADS_PER_STEP
    tw = hp * LANES
    tq = NA_G * GRID_W
    win = NA_WR * GRID_W
    qb, kb, vb = (T_QNA * QTILE_W // tw, T_KNA * QTILE_W // tw, T_VNA * QTILE_W // tw)
    cb = CTX_TILE0 * QTILE_W // tw
    n_var = bias.shape[0]
    return pl.pallas_call(
        functools.partial(_na_kernel, rows=rows, var_map=tuple(int(v) for v in var_map)),
        grid=(batch, NA_HEADS // hp),
        in_specs=[pl.BlockSpec((s_len, tw), lambda b, h: (b, qb + h)),
                  pl.BlockSpec((s_len, tw), lambda b, h: (b, kb + h)),
                  pl.BlockSpec((s_len, tw), lambda b, h: (b, vb + h)),
                  pl.BlockSpec((lc, tw), lambda b, h: (b, kb + h - cb)),
                  pl.BlockSpec((lc, tw), lambda b, h: (b, vb + h - cb)),
                  pl.BlockSpec((n_var, hp, tq, win), lambda b, h: (0, h, 0, 0))],
        out_specs=pl.BlockSpec((s_len, tw), lambda b, h: (b, h)),
        out_shape=jax.ShapeDtypeStruct((batch * s_len, NA_HEADS * NA_HEAD_DIM), BF16),
        compiler_params=_params(("arbitrary", "arbitrary")),
        name="na_attention",
    )(p_lat, p_lat, p_lat, p_ctx, p_ctx, bias)


def _layer_norm(x, g, b):
    mu = jnp.mean(x, axis=-1, keepdims=True)
    xc = x - mu
    var = jnp.mean(xc * xc, axis=-1, keepdims=True)
    return xc * lax.rsqrt(var + LN_EPS) * g + b


def _outln_kernel(a1_ref, a2_ref, w_ref, h_ref, mod_ref, ln_ref, r_ref, h1_ref, u2_ref, lg_ref, wb_ref):
    half = a1_ref.shape[1]

    @pl.when(pl.program_id(0) == 0)
    def _():
        rows = wb_ref.shape[0] // 4
        for r0 in range(0, wb_ref.shape[0], rows):
            wb_ref[r0:r0 + rows, :] = w_ref[0, r0:r0 + rows, :].astype(wb_ref.dtype)

    sub = a1_ref.shape[0] // OUTLN_SUBTILES
    for r0 in range(0, a1_ref.shape[0], sub):
        rs = slice(r0, r0 + sub)
        y = _dot(a1_ref[rs, :], wb_ref[:half, :]) + _dot(a2_ref[rs, :], wb_ref[half:, :])
        x = DEEPNORM_ALPHA * h_ref[rs, :] + mod_ref[0, 0:1, :] * y
        hn = _layer_norm(x, ln_ref[0:1, :], ln_ref[1:2, :])
        h1_ref[rs, :] = hn
        u = hn * (1.0 + mod_ref[0, 2:3, :]) + mod_ref[0, 1:2, :]
        u_bf = u.astype(BF16)
        u2_ref[rs, :] = u_bf
        lg_ref[rs, :] = _dot(u_bf, r_ref[...])


def _outln_call(a1, a2, w, layer, h, mod, ln, router2, rows_per_group, tm=512):
    m, d = h.shape
    half = a1.shape[1]
    tpg = rows_per_group // tm
    return pl.pallas_call(
        _outln_kernel,
        grid=(m // tm,),
        in_specs=[pl.BlockSpec((tm, half), lambda i: (i, 0)),
                  pl.BlockSpec((tm, half), lambda i: (i, 0)),
                  pl.BlockSpec((1, 2 * half, d), lambda i: (layer, 0, 0), pipeline_mode=pl.Buffered(1)),
                  pl.BlockSpec((tm, d), lambda i: (i, 0)),
                  pl.BlockSpec((1, 3, d), lambda i: (i // tpg, 0, 0)),
                  pl.BlockSpec((2, d), lambda i: (0, 0)),
                  _resident_spec((d, LANES))],
        out_specs=[pl.BlockSpec((tm, d), lambda i: (i, 0)),
                   pl.BlockSpec((tm, d), lambda i: (i, 0)),
                   pl.BlockSpec((tm, LANES), lambda i: (i, 0))],
        out_shape=[jax.ShapeDtypeStruct((m, d), F32),
                   jax.ShapeDtypeStruct((m, d), BF16),
                   jax.ShapeDtypeStruct((m, LANES), F32)],
        scratch_shapes=[pltpu.VMEM((2 * half, d), BF16)],
        compiler_params=_params(("arbitrary",)),
        name="out_proj_postnorm",
    )(a1, a2, w, h, mod, ln, router2)


def _router_operand(router):
    d, e = router.shape
    return jnp.pad(router, ((0, 0), (0, LANES - e))).astype(BF16)


def _route_kernel(lg_ref, slot_ref, aff_ref, cnt_ref, tri_ref, *, cap):
    n_tok = lg_ref.shape[2]

    @pl.when(pl.program_id(0) == 0)
    def _():
        chunk = 256
        for r0 in range(0, n_tok, chunk):
            r = r0 + lax.broadcasted_iota(I32, (chunk, n_tok), 0)
            c = lax.broadcasted_iota(I32, (chunk, n_tok), 1)
            tri_ref[r0:r0 + chunk, :] = jnp.where(r < c, 1.0, 0.0).astype(BF16)

    lg = lg_ref[0]
    ex = jnp.exp(lg - jnp.max(lg, axis=0, keepdims=True))
    aff = ex / jnp.sum(ex, axis=0, keepdims=True)
    bits = lax.bitcast_convert_type(aff, I32)
    n_e = lg.shape[0]
    count = lambda mask: jnp.sum(jnp.where(mask, 1.0, 0.0), axis=1, keepdims=True)

    def body(_, lohi):
        lo, hi = lohi
        mid = lo + jnp.right_shift(hi - lo, 1)
        ge = count(bits >= mid) >= cap
        return jnp.where(ge, mid, lo), jnp.where(ge, hi, mid)

    lo0 = jnp.zeros((n_e, 1), I32)
    hi0 = jnp.full((n_e, 1), 0x7F800000, I32)
    thr, _ = lax.fori_loop(0, 31, body, (lo0, hi0))
    gt = bits > thr
    eq = bits == thr
    need = cap - count(gt)
    pre_eq = _dot(jnp.where(eq, 1.0, 0.0).astype(BF16), tri_ref[...])
    sel = jnp.logical_or(gt, jnp.logical_and(eq, pre_eq < need))
    sel_bf = jnp.where(sel, 1.0, 0.0).astype(BF16)
    slot = _dot(sel_bf, tri_ref[...])
    slot_ref[0] = jnp.where(sel, slot.astype(I32), -1)
    aff_ref[0] = aff
    r = lax.broadcasted_iota(I32, (n_tok, LANES), 0)
    c = lax.broadcasted_iota(I32, (n_tok, LANES), 1)
    before = jnp.where(r < c * MOE_CHUNK, 1.0, 0.0).astype(BF16)
    cnt_ref[0] = _dot(sel_bf, before).astype(I32)


def _route_call(lg_t, cap):
    b, e, n_tok = lg_t.shape
    spec = pl.BlockSpec((1, e, n_tok), lambda i: (i, 0, 0))
    return pl.pallas_call(
        functools.partial(_route_kernel, cap=cap),
        grid=(b,),
        in_specs=[spec],
        out_specs=[spec, spec, pl.BlockSpec((1, e, LANES), lambda i: (i, 0, 0))],
        out_shape=[jax.ShapeDtypeStruct((b, e, n_tok), I32), jax.ShapeDtypeStruct((b, e, n_tok), F32),
                   jax.ShapeDtypeStruct((b, e, LANES), I32)],
        scratch_shapes=[pltpu.VMEM((n_tok, n_tok), BF16)],
        compiler_params=_params(("arbitrary",)),
        name="ec_route",
    )(lg_t)


def _slot_windows(cnt, cap, n_tok, chunk, win):
    bounds = cnt[:, :, 0:n_tok // MOE_CHUNK + 1:chunk // MOE_CHUNK]
    start, end = bounds[:, :, :-1], bounds[:, :, 1:]
    a = jnp.minimum((start // 16) * 16, cap - win)
    fits = jnp.all(end <= a + win, axis=1)
    return jnp.swapaxes(a, 1, 2).reshape(-1).astype(I32), fits.reshape(-1).astype(I32)


def _gather_kernel(win_ref, fit_ref, slot_ref, u_ref, o_ref, *, win, chunk):
    b, ks = pl.program_id(0), pl.program_id(1)
    n_e, cap = o_ref.shape[0], o_ref.shape[1]
    n_sub = u_ref.shape[0] // chunk
    step = b * pl.num_programs(1) + ks

    @pl.when(ks == 0)
    def _():
        o_ref[...] = jnp.zeros(o_ref.shape, o_ref.dtype)

    def place(rows, start_of, group, tok0, n_t):
        row = lax.broadcasted_iota(I32, (rows, n_t), 0)
        for g0 in range(0, n_e, group):
            starts = [start_of(e) for e in range(g0, g0 + group)]
            onehot = jnp.concatenate(
                [jnp.where(row + st == slot_ref[0, e:e + 1, tok0:tok0 + n_t], 1.0, 0.0).astype(BF16)
                 for st, e in zip(starts, range(g0, g0 + group))], axis=0)
            got = _dot(onehot, u_ref[tok0:tok0 + n_t, :]).astype(o_ref.dtype)
            for i, (st, e) in enumerate(zip(starts, range(g0, g0 + group))):
                o_ref[e, pl.ds(st, rows), :] += got[i * rows:(i + 1) * rows]

    for sub in range(n_sub):
        c = step * n_sub + sub

        @pl.when(fit_ref[c] != 0)
        def _(c=c, sub=sub):
            place(win, lambda e: pl.multiple_of(win_ref[c * n_e + e], 16), 512 // win, sub * chunk, chunk)

        @pl.when(fit_ref[c] == 0)
        def _(sub=sub):
            place(cap, lambda e: 0, 2, sub * chunk, chunk)


def _gather_call(starts, fit, slot, u2, cap, chunk, win, chunks_per_step=2):
    b, e, n_tok = slot.shape
    d = u2.shape[1]
    tt = chunk * chunks_per_step
    nc = n_tok // tt
    grid_spec = pltpu.PrefetchScalarGridSpec(
        num_scalar_prefetch=2,
        grid=(b, nc),
        in_specs=[pl.BlockSpec((1, e, tt), lambda i, k, w, f: (i, 0, k)),
                  pl.BlockSpec((tt, d), lambda i, k, w, f: (i * nc + k, 0))],
        out_specs=pl.BlockSpec((e, cap, d), lambda i, k, w, f: (0, i, 0)),
    )
    return pl.pallas_call(
        functools.partial(_gather_kernel, win=win, chunk=chunk),
        grid_spec=grid_spec,
        out_shape=jax.ShapeDtypeStruct((e, b * cap, d), BF16),
        compiler_params=_params(("arbitrary", "arbitrary")),
        name="moe_gather",
    )(starts, fit, slot, u2)


def _expert_hidden_kernel(x_ref, w1_ref, w3_ref, o_ref, *, ff):
    tf = w1_ref.shape[3]
    last = pl.num_programs(1) - 1

    def block(width):
        w = jnp.concatenate([w1_ref[0, 0, :, :width].astype(BF16), w3_ref[0, 0, :, :width].astype(BF16)], axis=1)
        h = _dot(x_ref[0], w)
        h1 = h[:, :width]
        o_ref[0, :, :width] = ((h1 / (1.0 + jnp.exp(-h1))) * h[:, width:]).astype(o_ref.dtype)

    tail = ff - (pl.cdiv(ff, tf) - 1) * tf
    if tail == tf:
        block(tf)
    else:
        pl.when(pl.program_id(1) < last)(lambda: block(tf))
        pl.when(pl.program_id(1) == last)(lambda: block(tail))


def _expert_out_kernel(hd_ref, w2_ref, o_ref):
    y = _dot(hd_ref[0], w2_ref[0, 0].astype(BF16)).astype(o_ref.dtype)
    cap = o_ref.shape[1]
    for b in range(o_ref.shape[0]):
        o_ref[b] = y[b * cap:(b + 1) * cap]


def _expert_call(xe, w1, w3, w2, layer, batch, tf=768, tn=2048):
    e, m, d = xe.shape
    cap = m // batch
    ff = w1.shape[3]
    hidden = pl.pallas_call(
        functools.partial(_expert_hidden_kernel, ff=ff),
        grid=(e, pl.cdiv(ff, tf)),
        in_specs=[pl.BlockSpec((1, m, d), lambda i, f: (i, 0, 0)),
                  pl.BlockSpec((1, 1, d, tf), lambda i, f: (layer, i, 0, f)),
                  pl.BlockSpec((1, 1, d, tf), lambda i, f: (layer, i, 0, f))],
        out_specs=pl.BlockSpec((1, m, tf), lambda i, f: (i, 0, f)),
        out_shape=jax.ShapeDtypeStruct((e, m, ff), BF16),
        compiler_params=_params(("arbitrary", "arbitrary")),
        name="moe_expert_hidden",
    )(xe, w1, w3)
    return pl.pallas_call(
        _expert_out_kernel,
        grid=(e, d // tn),
        in_specs=[pl.BlockSpec((1, m, ff), lambda i, j: (i, 0, 0)),
                  pl.BlockSpec((1, 1, ff, tn), lambda i, j: (layer, i, 0, j))],
        out_specs=pl.BlockSpec((batch, cap, tn), lambda i, j: (0, i, j)),
        out_shape=jax.ShapeDtypeStruct((batch, e * cap, d), BF16),
        compiler_params=_params(("arbitrary", "arbitrary")),
        name="moe_expert_out",
    )(hidden, w2)


def _scatter_kernel(win_ref, fit_ref, slot_ref, aff_ref, ye_ref, h_ref, mod_ref, ln_ref, *rest, with_next):
    if with_next:
        nmod_ref, h2_ref, un_ref, pt_ref, yew_ref, moe_ref = rest
    else:
        h2_ref, pt_ref, yew_ref, moe_ref = rest
    n_e = slot_ref.shape[1]
    cap = ye_ref.shape[1] // n_e
    tt = h_ref.shape[0]
    step = pl.program_id(0) * pl.num_programs(1) + pl.program_id(1)

    def gates(rows, e, start):
        row = lax.broadcasted_iota(I32, (rows, tt), 0)
        return jnp.where(row + start == slot_ref[0, e:e + 1, :], aff_ref[0, e:e + 1, :], 0.0).astype(BF16)

    @pl.when(fit_ref[step] != 0)
    def _():
        for e in range(n_e):
            start = pl.multiple_of(win_ref[step * n_e + e], 16)
            pt_ref[e * MOE_WIN:(e + 1) * MOE_WIN, :] = gates(MOE_WIN, e, start)
            yew_ref[e * MOE_WIN:(e + 1) * MOE_WIN, :] = ye_ref[0, pl.ds(e * cap + start, MOE_WIN), :]
        moe_ref[...] = _dot_tn(pt_ref[:n_e * MOE_WIN, :], yew_ref[...])

    @pl.when(fit_ref[step] == 0)
    def _():
        for e in range(n_e):
            pt_ref[e * cap:(e + 1) * cap, :] = gates(cap, e, 0)
        moe_ref[...] = _dot_tn(pt_ref[...], ye_ref[0])

    x = DEEPNORM_ALPHA * h_ref[...] + mod_ref[0] * moe_ref[...]
    hn = _layer_norm(x, ln_ref[0:1, :], ln_ref[1:2, :])
    h2_ref[...] = hn
    if with_next:
        un_ref[...] = (hn * (1.0 + nmod_ref[0, 1:2, :]) + nmod_ref[0, 0:1, :]).astype(un_ref.dtype)


def _scatter_call(win, fit, slot, aff, ye, h1, gate, ln, next_ss):
    b, e, n_tok = slot.shape
    m, d = h1.shape
    tt = MOE_CHUNK
    nt = n_tok // tt
    rows = ye.shape[1]
    with_next = next_ss is not None
    tok_spec = pl.BlockSpec((1, e, tt), lambda i, t, w, f: (i, 0, t))
    row_spec = pl.BlockSpec((tt, d), lambda i, t, w, f: (i * nt + t, 0))
    in_specs = [tok_spec, tok_spec,
                pl.BlockSpec((1, rows, d), lambda i, t, w, f: (i, 0, 0)),
                row_spec,
                pl.BlockSpec((1, 1, d), lambda i, t, w, f: (i, 0, 0)),
                pl.BlockSpec((2, d), lambda i, t, w, f: (0, 0))]
    args = [slot, aff, ye, h1, gate, ln]
    out_specs = [row_spec]
    out_shape = [jax.ShapeDtypeStruct((m, d), F32)]
    if with_next:
        in_specs.append(pl.BlockSpec((1, 2, d), lambda i, t, w, f: (i, 0, 0)))
        args.append(next_ss)
        out_specs.append(row_spec)
        out_shape.append(jax.ShapeDtypeStruct((m, d), BF16))
    grid_spec = pltpu.PrefetchScalarGridSpec(
        num_scalar_prefetch=2,
        grid=(b, nt),
        in_specs=in_specs,
        out_specs=out_specs,
        scratch_shapes=[pltpu.VMEM((rows, tt), BF16),
                        pltpu.VMEM((e * MOE_WIN, d), BF16),
                        pltpu.VMEM((tt, d), F32)],
    )
    res = pl.pallas_call(
        functools.partial(_scatter_kernel, with_next=with_next),
        grid_spec=grid_spec,
        out_shape=out_shape,
        compiler_params=_params(("arbitrary", "arbitrary")),
        name="moe_combine_postnorm",
    )(win, fit, *args)
    return res if with_next else (res[0], None)


def _ec_moe(h1, u2, logits, w1, w3, w2, layer, gate, ln, next_ss, batch, n_tok):
    cap = EC_CAPACITY_FACTOR * n_tok // N_EXPERTS
    lg_t = jnp.swapaxes(logits[:, :N_EXPERTS].reshape(batch, n_tok, N_EXPERTS), 1, 2)
    slot, aff, cnt = _route_call(lg_t, cap)
    g_win, g_fit = _slot_windows(cnt, cap, n_tok, GATHER_CHUNK, GATHER_WIN)
    xe = _gather_call(g_win, g_fit, slot, u2, cap, GATHER_CHUNK, GATHER_WIN)
    ye = _expert_call(xe, w1, w3, w2, layer, batch)
    c_win, c_fit = _slot_windows(cnt, cap, n_tok, MOE_CHUNK, MOE_WIN)
    return _scatter_call(c_win, c_fit, slot, aff, ye, h1, gate, ln, next_ss)


def _mm_kernel(a_ref, w_ref, o_ref, wb_ref):
    @pl.when(pl.program_id(1) == 0)
    def _():
        wb_ref[...] = w_ref[0].astype(wb_ref.dtype)

    o_ref[...] = _dot(a_ref[...], wb_ref[...]).astype(o_ref.dtype)


def _mm_call(a, w, layer, out_dtype, tm=1024, tn=1024):
    m, k = a.shape
    n = w.shape[2]
    return pl.pallas_call(
        _mm_kernel,
        grid=(n // tn, m // tm),
        in_specs=[pl.BlockSpec((tm, k), lambda j, i: (i, 0)),
                  pl.BlockSpec((1, k, tn), lambda j, i: (layer, 0, j))],
        out_specs=pl.BlockSpec((tm, tn), lambda j, i: (i, j)),
        out_shape=jax.ShapeDtypeStruct((m, n), out_dtype),
        scratch_shapes=[pltpu.VMEM((k, tn), BF16)],
        compiler_params=_params(("arbitrary", "arbitrary")),
        name="matmul",
    )(a, w)


def _short_conv(p, w, b):
    x = p.astype(F32)
    n = x.shape[0]
    r = lax.broadcasted_iota(I32, x.shape, 0)
    prev = jnp.where(r == 0, 0.0, pltpu.roll(x, 1, 0))
    nxt = jnp.where(r == n - 1, 0.0, pltpu.roll(x, n - 1, 0))
    return b + w[0:1, :] * prev + w[1:2, :] * x + w[2:3, :] * nxt


def _sconv_kernel(p_ref, w_ref, b_ref, o_ref):
    o_ref[...] = _short_conv(p_ref[...], w_ref[...], b_ref[...]).astype(o_ref.dtype)


def _sconv_call(p, conv_w, conv_b, col0, ncols, out_dtype, batch, n_tok, tc=512):
    c0 = col0 // tc
    return pl.pallas_call(
        _sconv_kernel,
        grid=(batch, ncols // tc),
        in_specs=[pl.BlockSpec((n_tok, tc), lambda b, j: (b, c0 + j)),
                  pl.BlockSpec((HY_SHORT, tc), lambda b, j: (0, c0 + j)),
                  pl.BlockSpec((1, tc), lambda b, j: (0, c0 + j))],
        out_specs=pl.BlockSpec((n_tok, tc), lambda b, j: (b, j)),
        out_shape=jax.ShapeDtypeStruct((batch * n_tok, ncols), out_dtype),
        compiler_params=_params(("arbitrary", "arbitrary")),
        name="hyena_short_conv",
    )(p, conv_w, conv_b.reshape(1, -1))


def _dft_fwd_kernel(fc_ref, fs_ref, r1_ref, r2_ref, *rest, spectral):
    xr = _dot(fc_ref[...], r1_ref[...])
    xi = _dot(fs_ref[...], r2_ref[...])
    if spectral:
        h_ref, nyq_ref, o_ref = rest
        hr = h_ref[0, 0]
        hi = h_ref[0, 1]
        dc = lax.broadcasted_iota(I32, xr.shape, 0) == 0
        yr = xr * hr - jnp.where(dc, 0.0, xi * hi)
        yi = jnp.where(dc, xi * nyq_ref[...], xr * hi + xi * hr)
    else:
        (o_ref,) = rest
        yr, yi = xr, xi
    o_ref[0, 0] = yr.astype(o_ref.dtype)
    o_ref[0, 1] = yi.astype(o_ref.dtype)


def _resident_spec(shape):
    return pl.BlockSpec(shape, lambda *_: (0,) * len(shape), pipeline_mode=pl.Buffered(1))


def _dft_fwd_call(fc, fs, r1, r2, spec, out_dtype, batch, n_tok, tn=256):
    ncols = r1.shape[1]
    r_spec = pl.BlockSpec((n_tok, tn), lambda j, b: (b, j))
    in_specs = [_resident_spec(fc.shape), _resident_spec(fs.shape), r_spec, r_spec]
    args = [fc, fs, r1, r2]
    if spec is not None:
        h, nyq, hcol0 = spec
        c0 = hcol0 // tn
        in_specs += [pl.BlockSpec((1, 2, n_tok, tn), lambda j, b: (0, 0, 0, c0 + j)),
                     pl.BlockSpec((1, tn), lambda j, b: (0, c0 + j))]
        args += [h, nyq]
    return pl.pallas_call(
        functools.partial(_dft_fwd_kernel, spectral=spec is not None),
        grid=(ncols // tn, batch),
        in_specs=in_specs,
        out_specs=pl.BlockSpec((1, 2, n_tok, tn), lambda j, b: (b, 0, 0, j)),
        out_shape=jax.ShapeDtypeStruct((batch, 2, n_tok, ncols), out_dtype),
        compiler_params=_params(("arbitrary", "arbitrary")),
        name="dft_forward",
    )(*args)


def _dual_kernel(a1_ref, a2_ref, b1_ref, b2_ref, *rest, hyena, planes):
    b1 = b1_ref[0, 0] if planes else b1_ref[...]
    b2 = b2_ref[0, 0] if planes else b2_ref[...]
    y = _dot(a1_ref[...], b1) + _dot(a2_ref[...], b2)
    if hyena:
        p_ref, cw_ref, cb_ref, z_ref, skip_ref, o_ref = rest
        gate = _short_conv(p_ref[...], cw_ref[...], cb_ref[...])
        y = gate * (y + skip_ref[...] * z_ref[...].astype(F32))
    else:
        (o_ref,) = rest
    o_ref[...] = y.astype(o_ref.dtype)


def _dual_call(a1, a2, b1, b2, hy, batch, n_tok, ncols, tn=256):
    planes = b1.ndim == 4
    if planes:
        b_specs = [pl.BlockSpec((1, 1, n_tok, tn), lambda b, j: (b, 0, 0, j)),
                   pl.BlockSpec((1, 1, n_tok, tn), lambda b, j: (b, 1, 0, j))]
    else:
        b_specs = [pl.BlockSpec((n_tok, tn), lambda b, j: (b, j))] * 2
    in_specs = [_resident_spec(a1.shape), _resident_spec(a2.shape)] + b_specs
    args = [a1, a2, b1, b2]
    o_spec = pl.BlockSpec((n_tok, tn), lambda b, j: (b, j))
    if hy is not None:
        p, conv_w, conv_b, pcol0, z, skip = hy
        c0 = pcol0 // tn
        in_specs += [pl.BlockSpec((n_tok, tn), lambda b, j: (b, c0 + j)),
                     pl.BlockSpec((HY_SHORT, tn), lambda b, j: (0, c0 + j)),
                     pl.BlockSpec((1, tn), lambda b, j: (0, c0 + j)),
                     o_spec,
                     pl.BlockSpec((1, tn), lambda b, j: (0, j))]
        args += [p, conv_w, conv_b.reshape(1, -1), z, skip]
    return pl.pallas_call(
        functools.partial(_dual_kernel, hyena=hy is not None, planes=planes),
        grid=(batch, ncols // tn),
        in_specs=in_specs,
        out_specs=o_spec,
        out_shape=jax.ShapeDtypeStruct((batch * n_tok, ncols), BF16),
        compiler_params=_params(("arbitrary", "arbitrary")),
        name="seq_mix_matmul",
    )(*args)


def _fnet_kernel(cl_ref, msl_ref, x_ref, cw_ref, o_ref):
    gw = cw_ref.shape[1]
    for c0 in range(0, x_ref.shape[1], gw):
        x = x_ref[:, c0:c0 + gw]
        seq = jnp.concatenate([_dot(cl_ref[...], x).astype(BF16), _dot(msl_ref[...], x).astype(BF16)], axis=1)
        o_ref[:, c0:c0 + gw] = _dot(seq, cw_ref[...]).astype(o_ref.dtype)


def _fnet_call(cl, msl, p, cw, col0, batch, n_tok, groups_per_step=2):
    gw = cw.shape[1]
    tw = gw * groups_per_step
    c0 = col0 // tw
    return pl.pallas_call(
        _fnet_kernel,
        grid=(batch, FN_GROUPS // groups_per_step),
        in_specs=[_resident_spec(cl.shape), _resident_spec(msl.shape),
                  pl.BlockSpec((n_tok, tw), lambda b, g: (b, c0 + g)),
                  pl.BlockSpec(cw.shape, lambda b, g: (0, 0))],
        out_specs=pl.BlockSpec((n_tok, tw), lambda b, g: (b, g)),
        out_shape=jax.ShapeDtypeStruct((batch * n_tok, FN_GROUPS * gw), BF16),
        compiler_params=_params(("arbitrary", "arbitrary")),
        name="fnet_mix",
    )(cl, msl, p, cw)


def _cos_sin_matrix(n_rows, n_cols, period, split=64):
    r = np.arange(n_rows, dtype=np.int64)[:, None]
    c_hi = (np.arange(n_cols // split, dtype=np.int64) * split)[None, :]
    c_lo = np.arange(split, dtype=np.int64)[None, :]
    ang = lambda c: 2.0 * np.pi * ((r * c) % period).astype(np.float64) / period
    ca, sa = jnp.asarray(np.cos(ang(c_hi)), F32), jnp.asarray(np.sin(ang(c_hi)), F32)
    cb, sb = jnp.asarray(np.cos(ang(c_lo)), F32), jnp.asarray(np.sin(ang(c_lo)), F32)
    cos = ca[:, :, None] * cb[:, None, :] - sa[:, :, None] * sb[:, None, :]
    sin = sa[:, :, None] * cb[:, None, :] + ca[:, :, None] * sb[:, None, :]
    return cos.reshape(n_rows, n_cols), sin.reshape(n_rows, n_cols)


def _trig_kernel(ta_ref, tb_ref, ea_ref, eb_ref, *o_refs, mode, period, scale):
    sa_ca = _dot(ta_ref[...], ea_ref[...])
    sb_cb = _dot(tb_ref[...], eb_ref[...])
    nc = sa_ca.shape[1] // 2
    ca, sa, cb, sb = sa_ca[:, :nc], sa_ca[:, nc:], sb_cb[:, :nc], sb_cb[:, nc:]
    cos = ca * cb - sa * sb
    sin = sa * cb + ca * sb
    tm = cos.shape[0]
    row = pl.program_id(0) * tm + lax.broadcasted_iota(I32, cos.shape, 0)
    col = lax.broadcasted_iota(I32, cos.shape, 1)
    alt = lambda idx: (1 - 2 * jnp.bitwise_and(idx, 1)).astype(F32)
    if mode == "hyena":
        fc_ref, fs_ref, gc_ref, gs_ref = o_refs
        fc_ref[...] = cos.astype(fc_ref.dtype)
        fs_ref[...] = jnp.where(row == 0, alt(col), -sin).astype(fs_ref.dtype)
        gc_ref[...] = (cos * jnp.where(col == 0, 1.0 / period, 2.0 / period)).astype(gc_ref.dtype)
        gs_ref[...] = jnp.where(col == 0, alt(row) * (1.0 / period), sin * (-2.0 / period)).astype(gs_ref.dtype)
    else:
        c_ref, ms_ref = o_refs
        c_ref[...] = (cos * scale).astype(c_ref.dtype)
        ms_ref[...] = (sin * (-scale)).astype(ms_ref.dtype)


def _split3(x):
    bf = jnp.bfloat16
    x = np.asarray(x, np.float32)
    hi = x.astype(bf)
    r1 = x - hi.astype(np.float32)
    mid = r1.astype(bf)
    lo = (r1 - mid.astype(np.float32)).astype(bf)
    return [hi, mid, lo]


def _trig_call(n, period, mode, scale=1.0, split=64, tm=256):
    r = np.arange(n, dtype=np.int64)[:, None]
    c_hi = (np.arange(n // split, dtype=np.int64) * split)[None, :]
    c_lo = np.arange(split, dtype=np.int64)[None, :]
    ang = lambda c: 2.0 * np.pi * ((r * c) % period).astype(np.float64) / period
    def operands(a, col_group):
        terms = _split3(np.cos(a)) + _split3(np.sin(a))
        k = a.shape[1]
        spread = (col_group[None, :] == np.arange(k)[:, None]).astype(np.float32)
        zero = np.zeros_like(spread)
        e = np.concatenate([np.concatenate([spread, zero], 1)] * 3 + [np.concatenate([zero, spread], 1)] * 3, 0)
        return jnp.asarray(np.concatenate(terms, axis=1)), jnp.asarray(e, BF16)
    cols = np.arange(n)
    ta, ea = operands(ang(c_hi), cols // split)
    tb, eb = operands(ang(c_lo), cols % split)
    n_out = 4 if mode == "hyena" else 2
    o_spec = pl.BlockSpec((tm, n), lambda i: (i, 0))
    return pl.pallas_call(
        functools.partial(_trig_kernel, mode=mode, period=period, scale=scale),
        grid=(n // tm,),
        in_specs=[pl.BlockSpec((tm, ta.shape[1]), lambda i: (i, 0)),
                  pl.BlockSpec((tm, tb.shape[1]), lambda i: (i, 0)),
                  pl.BlockSpec(ea.shape, lambda i: (0, 0)),
                  pl.BlockSpec(eb.shape, lambda i: (0, 0))],
        out_specs=[o_spec] * n_out,
        out_shape=[jax.ShapeDtypeStruct((n, n), BF16)] * n_out,
        compiler_params=_params(("arbitrary",)),
        name="dft_tables",
    )(ta, tb, ea, eb)


def _hyena_dft_operands(n_tok):
    return _trig_call(n_tok, 2 * n_tok, "hyena")


def _fnet_operands(n_tok, gw):
    cl, msl = _trig_call(n_tok, n_tok, "fnet", scale=1.0 / math.sqrt(n_tok * gw))
    cw, sw = _cos_sin_matrix(gw, gw, gw)
    return cl, msl, jnp.concatenate([cw, sw], axis=0).astype(BF16)


def _filter_kernel(z_ref, w1_ref, b1_ref, f1_ref, w2_ref, b2_ref, f2_ref, w3f_ref, w3b_ref, dl_ref,
                   sum_ref, dif_ref, nyq_ref, hdn_ref):
    @pl.when(jnp.logical_and(pl.program_id(0) == 0, pl.program_id(1) == 0))
    def _():
        hdot = lambda a, b: jnp.dot(a, b, precision=lax.Precision.HIGHEST, preferred_element_type=F32)
        hdn = jnp.sin(f1_ref[...] * (hdot(z_ref[...], w1_ref[...]) + b1_ref[...]))
        hdn = jnp.sin(f2_ref[...] * (hdot(hdn, w2_ref[...]) + b2_ref[...]))
        hdn_ref[...] = hdn.astype(hdn_ref.dtype)

    n = hdn_ref.shape[0]
    row = lax.broadcasted_iota(I32, (n, dl_ref.shape[1]), 0)
    decay = jnp.exp(-(row.astype(F32) / (n - 1)) * dl_ref[...])
    hf = _dot(hdn_ref[...], w3f_ref[...].astype(BF16)) * decay
    hb = jnp.where(row == 0, 0.0, _dot(hdn_ref[...], w3b_ref[...].astype(BF16)) * decay)
    tot = hf + hb
    sum_ref[...] = tot.astype(sum_ref.dtype)
    dif_ref[...] = (hf - hb).astype(dif_ref.dtype)
    nyq_ref[...] = jnp.sum(jnp.where(jnp.bitwise_and(row, 1) == 0, tot, -tot), axis=0, keepdims=True)


def _filter_call(n_tok, fw1, fb1, ff1, fw2, fb2, ff2, fw3, tc=512):
    t01 = np.linspace(0.0, 1.0, n_tok, dtype=np.float32)
    w = (2.0 * math.pi * np.arange(n_tok, dtype=np.float32) / n_tok).astype(np.float32)
    bands = np.linspace(1e-4, HY_BANDS - 1, HY_BANDS, dtype=np.float32)
    z = np.concatenate([t01[:, None], np.cos(w[:, None] * bands), -np.sin(w[:, None] * bands)], -1)
    deltas = np.abs(np.linspace(HY_MIN_DECAY, HY_MAX_DECAY, HY_W, dtype=np.float32))[None, :]
    emb, hid = fw1.shape
    nc = HY_W // tc
    full = lambda shape: pl.BlockSpec(shape, lambda o, j: (0,) * len(shape))
    o_spec = pl.BlockSpec((n_tok, tc), lambda o, j: (0, o * nc + j))
    return pl.pallas_call(
        _filter_kernel,
        grid=(HY_ORDER, nc),
        in_specs=[full((n_tok, emb)), full((emb, hid)), full((1, hid)), full((1, hid)),
                  full((hid, hid)), full((1, hid)), full((1, hid)),
                  pl.BlockSpec((hid, tc), lambda o, j: (0, (2 * o) * nc + j)),
                  pl.BlockSpec((hid, tc), lambda o, j: (0, (2 * o + 1) * nc + j)),
                  pl.BlockSpec((1, tc), lambda o, j: (0, j))],
        out_specs=[o_spec, o_spec, pl.BlockSpec((1, tc), lambda o, j: (0, o * nc + j))],
        out_shape=[jax.ShapeDtypeStruct((n_tok, HY_ORDER * HY_W), BF16),
                   jax.ShapeDtypeStruct((n_tok, HY_ORDER * HY_W), BF16),
                   jax.ShapeDtypeStruct((1, HY_ORDER * HY_W), F32)],
        scratch_shapes=[pltpu.VMEM((n_tok, hid), BF16)],
        compiler_params=_params(("arbitrary", "arbitrary")),
        name="hyena_filters",
    )(jnp.asarray(z.astype(np.float32)), fw1, fb1.reshape(1, hid), ff1.reshape(1, hid),
      fw2, fb2.reshape(1, hid), ff2.reshape(1, hid), fw3, fw3, jnp.asarray(deltas))


def kernel(x, c, ctx, c_ctx, ada_w, ada_b, ln1_g, ln1_b, ln2_g, ln2_b, router, exp_w1, exp_w3, exp_w2,
           ab_w_in, ab_kv_norm, ab_w_ukv, ab_rpb, ab_w_out,
           cd_w_in, cd_conv_w, cd_conv_b, cd_filt_w1, cd_filt_b1, cd_filt_freq1, cd_filt_w2, cd_filt_b2,
           cd_filt_freq2, cd_filt_w3, cd_skip, cd_w_out):
    batch, n_tok, d = x.shape
    lc = ctx.shape[1]
    x2d = x.reshape(batch * n_tok, d)
    ctx2d = ctx.reshape(batch * lc, d)

    pad_rows = (-(batch + 1)) % 8
    cv = jnp.concatenate([c, c_ctx[None, :], jnp.zeros((pad_rows, d), F32)], axis=0)
    ada = _ada_call(cv, ada_w, ada_b)
    mods = [ada[i, :batch].reshape(batch, 6, d) for i in range(DEPTH)]
    ctx_ss = ada[0, batch, :2 * d].reshape(1, 2, d)

    w_in = _w_in_prep_call(jnp.swapaxes(ab_w_in, 1, 2), 0)
    p_lat = _mm_rope_call(x2d, mods[0][:, 0:2], n_tok, w_in, _rope_tables(n_tok), tm=1024)
    p_ctx = _mm_rope_call(ctx2d, ctx_ss, batch * lc, w_in, _identity_rope_tables(batch * lc), tm=batch * lc,
                          tile0=CTX_TILE0)
    w_ukv = _permute_w_ukv(ab_w_ukv[0])
    kv_lat = _kvup_call(p_lat, ab_kv_norm[0], w_ukv, tm=1024)
    kv_ctx = _kvup_call(p_ctx, ab_kv_norm[0], w_ukv, tm=lc, tile0=CTX_TILE0)
    a_out = _mla_call(p_lat, kv_lat, p_ctx, kv_ctx, batch, n_tok, lc)
    var_map, na_er, na_ec = _na_tables(n_tok // GRID_W)
    b_out = _na_call(p_lat, p_ctx, _na_bias(ab_rpb[0], na_er, na_ec), var_map, batch, n_tok, lc)
    ln1 = jnp.stack([ln1_g, ln1_b], axis=1)
    ln2 = jnp.stack([ln2_g, ln2_b], axis=1)
    h1, u2, logits = _outln_call(a_out, b_out, ab_w_out, 0, x2d,
                                 mods[0][:, 2:5], ln1[0], _router_operand(router[0]), n_tok)
    h, u = _ec_moe(h1, u2, logits, exp_w1, exp_w3, exp_w2, 0, mods[0][:, 5:6], ln2[0],
                   mods[1][:, 0:2], batch, n_tok)

    p1 = _mm_call(u, cd_w_in, 0, BF16)
    s0 = _sconv_call(p1, cd_conv_w[0], cd_conv_b[0], 0, HY_W, BF16, batch, n_tok)
    h_sum, h_dif, h_nyq = _filter_call(n_tok, cd_filt_w1[0], cd_filt_b1[0], cd_filt_freq1[0], cd_filt_w2[0],
                                       cd_filt_b2[0], cd_filt_freq2[0], cd_filt_w3[0])
    fc, fs, gc, gs = _hyena_dft_operands(n_tok)
    h_spec = _dft_fwd_call(fc, fs, h_sum, h_dif, None, F32, 1, n_tok)
    z = s0
    for o in range(HY_ORDER):
        y_spec = _dft_fwd_call(fc, fs, z, z, (h_spec, h_nyq, o * HY_W), BF16, batch, n_tok)
        z = _dual_call(gc, gs, y_spec, y_spec,
                       (p1, cd_conv_w[0], cd_conv_b[0], (o + 1) * HY_W, z, cd_skip[0][o:o + 1]),
                       batch, n_tok, HY_W)
    cl, msl, cw = _fnet_operands(n_tok, FN_GROUP_W)
    y_fn = _fnet_call(cl, msl, p1, cw, HY_IN_W, batch, n_tok)
    h1, u2, logits = _outln_call(z, y_fn, cd_w_out, 0, h,
                                 mods[1][:, 2:5], ln1[1], _router_operand(router[1]), n_tok)
    h, _ = _ec_moe(h1, u2, logits, exp_w1, exp_w3, exp_w2, 1, mods[1][:, 5:6], ln2[1],
                   None, batch, n_tok)
    return h.reshape(batch, n_tok, d)
```

```python
import functools
import math

import numpy as np
import jax
import jax.numpy as jnp
from jax import lax
from jax.experimental import pallas as pl
from jax.experimental.pallas import tpu as pltpu

F32 = jnp.float32
BF16 = jnp.bfloat16
I32 = jnp.int32

DEPTH = 2
GRID_W = 64
DEEPNORM_ALPHA = (2.0 * DEPTH) ** 0.25
LN_EPS = 1e-6
NEG_INF = -1e30
LOG2E = math.log2(math.e)

MLA_HEADS = 8
MLA_NOPE = 128
MLA_ROPE = 64
MLA_QK = MLA_NOPE + MLA_ROPE
MLA_V = 128
MLA_KV_RANK = 512
ROPE_THETA = 10000.0

NA_HEADS = 8
NA_HEAD_DIM = 128
NA_KH = 8
NA_KW = 16

OFF_CKV = MLA_HEADS * MLA_QK
OFF_KPE = OFF_CKV + MLA_KV_RANK
OFF_QNA = OFF_KPE + MLA_ROPE

HY_W = 1024
HY_ORDER = 2
HY_IN_W = (HY_ORDER + 1) * HY_W
HY_SHORT = 3
HY_BANDS = 16
HY_DECAY_TARGET = 1e-2
HY_MIN_DECAY = math.log(HY_DECAY_TARGET) / 1.5
HY_MAX_DECAY = math.log(HY_DECAY_TARGET) / 0.3

FN_W = 1024
FN_GROUPS = 4
FN_GROUP_W = FN_W // FN_GROUPS

N_EXPERTS = 16
EC_CAPACITY_FACTOR = 2

LANES = 128
VMEM_LIMIT = 56 * 1024 * 1024

QTILE_W = 2 * LANES
T_KPE = MLA_HEADS
T_CKV = T_KPE + 2
T_QNA = T_CKV + MLA_KV_RANK // QTILE_W
T_KNA = T_QNA + NA_HEADS * NA_HEAD_DIM // QTILE_W
T_VNA = T_KNA + NA_HEADS * NA_HEAD_DIM // QTILE_W
N_ABTILES = T_VNA + NA_HEADS * NA_HEAD_DIM // QTILE_W
AB_PERM_W = N_ABTILES * QTILE_W
CTX_TILE0 = T_KPE
INPROJ_TILES_PER_STEP = 8

MLA_SUBTILES = 8
OUTLN_SUBTILES = 2
SEQMIX_SUBTILES = 4
MOE_CHUNK = 256
MOE_WIN = 64
GATHER_CHUNK = 256
GATHER_WIN = 64
NA_G = 4
NA_WR = NA_KH + NA_G - 1
NA_HEADS_PER_STEP = 2


def _params(sem, vmem=VMEM_LIMIT):
    return pltpu.CompilerParams(dimension_semantics=sem, vmem_limit_bytes=vmem)


def _dot(a, b):
    return jnp.dot(a, b, preferred_element_type=F32)


def _dot_nt(a, b):
    return lax.dot_general(a, b, (((1,), (1,)), ((), ())), preferred_element_type=F32)


def _dot_tn(a, b):
    return lax.dot_general(a, b, (((0,), (0,)), ((), ())), preferred_element_type=F32)


def _ada_kernel(c_ref, w_ref, b_ref, o_ref):
    c = c_ref[...]
    a = c / (1.0 + jnp.exp(-c))
    o_ref[0] = _dot(a.astype(BF16), w_ref[0].astype(BF16)) + b_ref[0]


def _ada_call(cv, ada_w, ada_b, tn=2048):
    depth, d, n = ada_w.shape
    rows = cv.shape[0]
    return pl.pallas_call(
        _ada_kernel,
        grid=(depth, n // tn),
        in_specs=[pl.BlockSpec((rows, d), lambda l, j: (0, 0)),
                  pl.BlockSpec((1, d, tn), lambda l, j: (l, 0, j)),
                  pl.BlockSpec((1, 1, tn), lambda l, j: (l, 0, j))],
        out_specs=pl.BlockSpec((1, rows, tn), lambda l, j: (l, 0, j)),
        out_shape=jax.ShapeDtypeStruct((depth, rows, n), F32),
        compiler_params=_params(("arbitrary", "arbitrary")),
        name="ada_params",
    )(cv, ada_w, ada_b.reshape(depth, 1, n))


def _mm_rope_kernel(x_ref, ss_ref, w_ref, ta_ref, tb_ref, tc_ref, o_ref, a_ref, *, tile0):
    @pl.when(pl.program_id(1) == 0)
    def _():
        a_ref[...] = (x_ref[...] * (1.0 + ss_ref[0, 1:2, :]) + ss_ref[0, 0:1, :]).astype(a_ref.dtype)

    a = a_ref[...]
    for q in range(INPROJ_TILES_PER_STEP):
        t = tile0 + INPROJ_TILES_PER_STEP * pl.program_id(1) + q
        c0 = q * QTILE_W
        acc = _dot_nt(a, w_ref[c0:c0 + QTILE_W, :])
        is_rope = t <= T_KPE
        f = jnp.where(t < T_KPE, LOG2E * MLA_QK ** -0.5,
                      jnp.where(jnp.logical_and(t >= T_QNA, t < T_KNA), LOG2E * NA_HEAD_DIM ** -0.5, 1.0)
                      ).astype(F32)
        hi = acc[:, LANES:]
        rot = (hi * jnp.where(is_rope, ta_ref[...], 1.0)
               + pltpu.roll(hi, LANES - MLA_ROPE // 2, 1) * jnp.where(is_rope, tb_ref[...], 0.0)
               + pltpu.roll(hi, MLA_ROPE // 2, 1) * jnp.where(is_rope, tc_ref[...], 0.0))
        o_ref[:, c0:c0 + LANES] = (acc[:, :LANES] * f).astype(o_ref.dtype)
        o_ref[:, c0 + LANES:c0 + QTILE_W] = (rot * f).astype(o_ref.dtype)


def _mm_rope_call(x, ss, rows_per_group, w, tabs, tm, tile0=0):
    m, k = x.shape
    tn = INPROJ_TILES_PER_STEP * QTILE_W
    n = w.shape[0] - tile0 * QTILE_W
    j0 = tile0 // INPROJ_TILES_PER_STEP
    ntab = tabs[0].shape[0] // tm
    tpg = rows_per_group // tm
    tab_spec = pl.BlockSpec((tm, LANES), lambda i, j: (i % ntab, 0))
    return pl.pallas_call(
        functools.partial(_mm_rope_kernel, tile0=tile0),
        grid=(m // tm, n // tn),
        in_specs=[pl.BlockSpec((tm, k), lambda i, j: (i, 0)),
                  pl.BlockSpec((1, 2, k), lambda i, j: (i // tpg, 0, 0)),
                  pl.BlockSpec((tn, k), lambda i, j: (j0 + j, 0)),
                  tab_spec, tab_spec, tab_spec],
        out_specs=pl.BlockSpec((tm, tn), lambda i, j: (i, j)),
        out_shape=jax.ShapeDtypeStruct((m, n), BF16),
        scratch_shapes=[pltpu.VMEM((tm, k), BF16)],
        compiler_params=_params(("arbitrary", "arbitrary")),
        name="attn_in_proj",
    )(x, ss, w, *tabs)


def _rope_tables(n_tok):
    t = np.arange(n_tok)
    row = (t // GRID_W).astype(np.float32)
    col = (t % GRID_W).astype(np.float32)
    n_freq = MLA_ROPE // 4
    inv = (ROPE_THETA ** (-np.arange(n_freq, dtype=np.float32) / n_freq)).astype(np.float32)
    ang = np.concatenate([row[:, None] * inv, col[:, None] * inv], axis=1)
    cos, sin = np.cos(ang).astype(np.float32), np.sin(ang).astype(np.float32)
    half = MLA_ROPE // 2
    ta = np.zeros((n_tok, LANES), np.float32)
    tb = np.zeros((n_tok, LANES), np.float32)
    tc = np.zeros((n_tok, LANES), np.float32)
    ta[:, :half] = cos
    ta[:, half:2 * half] = cos
    tb[:, :half] = -sin
    tc[:, half:2 * half] = sin
    return jnp.asarray(ta), jnp.asarray(tb), jnp.asarray(tc)


def _identity_rope_tables(n_tok):
    ta = np.zeros((n_tok, LANES), np.float32)
    ta[:, :MLA_ROPE] = 1.0
    z = np.zeros((n_tok, LANES), np.float32)
    return jnp.asarray(ta), jnp.asarray(z), jnp.asarray(z)


def _w_in_prep_kernel(w_ref, o_ref):
    x = w_ref[0]
    zeros = lambda n: jnp.zeros((n, x.shape[1]), x.dtype)
    rows = []
    for h in range(MLA_HEADS):
        rows += [x[h * MLA_QK:(h + 1) * MLA_QK, :], zeros(QTILE_W - MLA_QK)]
    rows += [zeros(LANES), x[OFF_KPE:OFF_QNA, :], zeros(LANES - MLA_ROPE), zeros(QTILE_W),
             x[OFF_CKV:OFF_KPE, :], x[OFF_QNA:, :]]
    o_ref[...] = jnp.concatenate(rows, axis=0).astype(o_ref.dtype)


def _w_in_prep_call(w_in_t, layer, tk=256):
    _, n, d = w_in_t.shape
    return pl.pallas_call(
        _w_in_prep_kernel,
        grid=(d // tk,),
        in_specs=[pl.BlockSpec((1, n, tk), lambda i: (layer, 0, i))],
        out_specs=pl.BlockSpec((AB_PERM_W, tk), lambda i: (0, i)),
        out_shape=jax.ShapeDtypeStruct((AB_PERM_W, d), BF16),
        compiler_params=_params(("arbitrary",)),
        name="attn_w_in_prep",
    )(w_in_t)


def _kvup_kernel(x_ref, g_ref, w_ref, o_ref):
    x = x_ref[...].astype(F32)
    y = x * lax.rsqrt(jnp.mean(x * x, axis=-1, keepdims=True) + LN_EPS) * g_ref[...]
    o_ref[...] = _dot(y.astype(BF16), w_ref[...]).astype(o_ref.dtype)


def _kvup_call(p, g, w, tm, tile0=0):
    m = p.shape[0]
    r, n = w.shape
    return pl.pallas_call(
        _kvup_kernel,
        grid=(m // tm,),
        in_specs=[pl.BlockSpec((tm, r), lambda i: (i, (T_CKV - tile0) * QTILE_W // r)),
                  pl.BlockSpec((1, r), lambda i: (0, 0)),
                  pl.BlockSpec((r, n), lambda i: (0, 0))],
        out_specs=pl.BlockSpec((tm, n), lambda i: (i, 0)),
        out_shape=jax.ShapeDtypeStruct((m, n), BF16),
        compiler_params=_params(("arbitrary",)),
        name="kv_up",
    )(p, g.reshape(1, r), w)


def _permute_w_ukv(w):
    r = w.shape[0]
    w3 = w.reshape(r, MLA_HEADS, MLA_NOPE + MLA_V)
    return jnp.concatenate([w3[:, :, :MLA_NOPE].reshape(r, -1), w3[:, :, MLA_NOPE:].reshape(r, -1)],
                           axis=1).astype(BF16)


def _mla_kernel(q_ref, kn_ref, kp_ref, v_ref, kcn_ref, kcp_ref, vc_ref, o_ref, kcat, vcat):
    s_len = kn_ref.shape[0]
    n_h = kcat.shape[0]
    for h in range(n_h):
        kcat[h, :s_len, :LANES] = kn_ref[:, h * LANES:(h + 1) * LANES]
        kcat[h, :s_len, LANES:] = kp_ref[...]
        kcat[h, s_len:, :LANES] = kcn_ref[:, h * LANES:(h + 1) * LANES]
        kcat[h, s_len:, LANES:] = kcp_ref[...]
        vcat[h, :s_len, :] = v_ref[:, h * LANES:(h + 1) * LANES]
        vcat[h, s_len:, :] = vc_ref[:, h * LANES:(h + 1) * LANES]

    sub = q_ref.shape[0] // MLA_SUBTILES
    for h in range(n_h):
        for r0 in range(0, q_ref.shape[0], sub):
            s = _dot_nt(q_ref[r0:r0 + sub, h * QTILE_W:(h + 1) * QTILE_W], kcat[h])
            m = jnp.max(s, axis=-1, keepdims=True)
            p = jnp.exp2(s - m)
            l = jnp.sum(p, axis=-1, keepdims=True)
            o = _dot(p.astype(BF16), vcat[h])
            o_ref[r0:r0 + sub, h * LANES:(h + 1) * LANES] = (o / l).astype(o_ref.dtype)


def _mla_call(p_lat, kv_lat, p_ctx, kv_ctx, batch, s_len, lc, heads_per_step=2):
    hp = heads_per_step
    kpe_blk = T_KPE * 2 + 1
    v0 = MLA_HEADS // hp
    return pl.pallas_call(
        _mla_kernel,
        grid=(batch, MLA_HEADS // hp),
        in_specs=[pl.BlockSpec((s_len, hp * QTILE_W), lambda b, h: (b, h)),
                  pl.BlockSpec((s_len, hp * LANES), lambda b, h: (b, h)),
                  pl.BlockSpec((s_len, LANES), lambda b, h: (b, kpe_blk)),
                  pl.BlockSpec((s_len, hp * LANES), lambda b, h: (b, v0 + h)),
                  pl.BlockSpec((lc, hp * LANES), lambda b, h: (b, h)),
                  pl.BlockSpec((lc, LANES), lambda b, h: (b, kpe_blk - 2 * CTX_TILE0)),
                  pl.BlockSpec((lc, hp * LANES), lambda b, h: (b, v0 + h))],
        out_specs=pl.BlockSpec((s_len, hp * LANES), lambda b, h: (b, h)),
        out_shape=jax.ShapeDtypeStruct((batch * s_len, MLA_HEADS * MLA_V), BF16),
        scratch_shapes=[pltpu.VMEM((hp, s_len + lc, QTILE_W), BF16),
                        pltpu.VMEM((hp, s_len + lc, LANES), BF16)],
        compiler_params=_params(("arbitrary", "arbitrary")),
        name="mla_attention",
    )(p_lat, kv_lat, p_lat, kv_lat, kv_ctx, p_ctx, kv_ctx)


def _na_kernel(q_ref, k_ref, v_ref, kc_ref, vc_ref, bias_ref, o_ref, *, rows, var_map):
    win = NA_WR * GRID_W
    tq = NA_G * GRID_W
    for h in range(q_ref.shape[1] // LANES):
        hs = slice(h * LANES, (h + 1) * LANES)
        for g in range(rows // NA_G):
            start = int(np.clip(NA_G * g - NA_KH // 2, 0, rows - NA_WR)) * GRID_W
            q = q_ref[g * tq:(g + 1) * tq, hs]
            sw = _dot_nt(q, k_ref[start:start + win, hs]) + bias_ref[var_map[g], h]
            sc = _dot_nt(q, kc_ref[:, hs])
            m = jnp.maximum(jnp.max(sw, axis=-1, keepdims=True), jnp.max(sc, axis=-1, keepdims=True))
            pw = jnp.exp2(sw - m)
            pc = jnp.exp2(sc - m)
            l = jnp.sum(pw, axis=-1, keepdims=True) + jnp.sum(pc, axis=-1, keepdims=True)
            o = _dot(pw.astype(BF16), v_ref[start:start + win, hs]) + _dot(pc.astype(BF16), vc_ref[:, hs])
            o_ref[g * tq:(g + 1) * tq, hs] = (o / l).astype(o_ref.dtype)


def _na_tables(rows):
    kh = min(NA_KH, rows)
    ng = rows // NA_G
    qr = np.arange(NA_G)[:, None]
    kr = np.arange(NA_WR)[None, :]
    sel_rows = []
    for g in range(ng):
        start_row = int(np.clip(NA_G * g - NA_KH // 2, 0, rows - NA_WR))
        r = NA_G * g + qr
        krow = start_row + kr
        rs = np.clip(r - kh // 2, 0, rows - kh)
        ok = (krow >= rs) & (krow < rs + kh)
        dr = krow - r + (NA_KH - 1)
        sel_rows.append(np.stack([ok & (dr == d) for d in range(2 * NA_KH - 1)]))
    sel_rows = np.stack(sel_rows)
    uniq, inverse = np.unique(sel_rows.reshape(ng, -1), axis=0, return_inverse=True)
    er = uniq.reshape((-1,) + sel_rows.shape[1:]).astype(np.float32)
    qc = np.arange(GRID_W)[:, None]
    kc = np.arange(GRID_W)[None, :]
    cs = np.clip(qc - NA_KW // 2, 0, GRID_W - NA_KW)
    ok_c = (kc >= cs) & (kc < cs + NA_KW)
    dc = np.clip(kc - qc, -(NA_KW - 1), NA_KW - 1) + (NA_KW - 1)
    ec = np.stack([ok_c & (dc == e) for e in range(2 * NA_KW - 1)]).astype(np.float32)
    return inverse.reshape(-1).astype(np.int32), er, ec


def _na_bias(rpb, er, ec):
    hp = lax.Precision.HIGHEST
    t = jnp.einsum('hde,eqk->hdqk', rpb.astype(F32), jnp.asarray(ec), precision=hp)
    t = jnp.where(jnp.asarray(ec.sum(axis=0) > 0.5), t * LOG2E, NEG_INF)
    outside = jnp.full(t[:, 0].shape, NEG_INF, F32)
    d_of = np.where(er.sum(axis=1) > 0.5, er.argmax(axis=1), -1)
    block = lambda d: outside if d < 0 else t[:, d]
    return jnp.stack([
        jnp.concatenate([jnp.concatenate([block(int(d)) for d in row], axis=-1) for row in var], axis=-2)
        for var in d_of])


def _na_call(p_lat, p_ctx, bias, var_map, batch, s_len, lc):
    rows = s_len // GRID_W
    hp = NA_HEADS_PER_STEP
    tw = hp * LANES
    tq = NA_G * GRID_W
    win = NA_WR * GRID_W
    qb, kb, vb = (T_QNA * QTILE_W // tw, T_KNA * QTILE_W // tw, T_VNA * QTILE_W // tw)
    cb = CTX_TILE0 * QTILE_W // tw
    n_var = bias.shape[0]
    return pl.pallas_call(
        functools.partial(_na_kernel, rows=rows, var_map=tuple(int(v) for v in var_map)),
        grid=(batch, NA_HEADS // hp),
        in_specs=[pl.BlockSpec((s_len, tw), lambda b, h: (b, qb + h)),
                  pl.BlockSpec((s_len, tw), lambda b, h: (b, kb + h)),
                  pl.BlockSpec((s_len, tw), lambda b, h: (b, vb + h)),
                  pl.BlockSpec((lc, tw), lambda b, h: (b, kb + h - cb)),
                  pl.BlockSpec((lc, tw), lambda b, h: (b, vb + h - cb)),
                  pl.BlockSpec((n_var, hp, tq, win), lambda b, h: (0, h, 0, 0))],
        out_specs=pl.BlockSpec((s_len, tw), lambda b, h: (b, h)),
        out_shape=jax.ShapeDtypeStruct((batch * s_len, NA_HEADS * NA_HEAD_DIM), BF16),
        compiler_params=_params(("arbitrary", "arbitrary")),
        name="na_attention",
    )(p_lat, p_lat, p_lat, p_ctx, p_ctx, bias)


def _layer_norm(x, g, b):
    mu = jnp.mean(x, axis=-1, keepdims=True)
    xc = x - mu
    var = jnp.mean(xc * xc, axis=-1, keepdims=True)
    return xc * lax.rsqrt(var + LN_EPS) * g + b


def _outln_kernel(a1_ref, a2_ref, w_ref, h_ref, mod_ref, ln_ref, r_ref, h1_ref, u2_ref, lg_ref, wb_ref):
    half = a1_ref.shape[1]

    @pl.when(pl.program_id(0) == 0)
    def _():
        rows = wb_ref.shape[0] // 4
        for r0 in range(0, wb_ref.shape[0], rows):
            wb_ref[r0:r0 + rows, :] = w_ref[0, r0:r0 + rows, :].astype(wb_ref.dtype)

    sub = a1_ref.shape[0] // OUTLN_SUBTILES
    for r0 in range(0, a1_ref.shape[0], sub):
        rs = slice(r0, r0 + sub)
        y = _dot(a1_ref[rs, :], wb_ref[:half, :]) + _dot(a2_ref[rs, :], wb_ref[half:, :])
        x = DEEPNORM_ALPHA * h_ref[rs, :] + mod_ref[0, 0:1, :] * y
        hn = _layer_norm(x, ln_ref[0:1, :], ln_ref[1:2, :])
        h1_ref[rs, :] = hn
        u = hn * (1.0 + mod_ref[0, 2:3, :]) + mod_ref[0, 1:2, :]
        u_bf = u.astype(BF16)
        u2_ref[rs, :] = u_bf
        lg_ref[rs, :] = _dot(u_bf, r_ref[...])


def _outln_call(a1, a2, w, layer, h, mod, ln, router2, rows_per_group, tm=512):
    m, d = h.shape
    half = a1.shape[1]
    tpg = rows_per_group // tm
    return pl.pallas_call(
        _outln_kernel,
        grid=(m // tm,),
        in_specs=[pl.BlockSpec((tm, half), lambda i: (i, 0)),
                  pl.BlockSpec((tm, half), lambda i: (i, 0)),
                  pl.BlockSpec((1, 2 * half, d), lambda i: (layer, 0, 0), pipeline_mode=pl.Buffered(1)),
                  pl.BlockSpec((tm, d), lambda i: (i, 0)),
                  pl.BlockSpec((1, 3, d), lambda i: (i // tpg, 0, 0)),
                  pl.BlockSpec((2, d), lambda i: (0, 0)),
                  _resident_spec((d, LANES))],
        out_specs=[pl.BlockSpec((tm, d), lambda i: (i, 0)),
                   pl.BlockSpec((tm, d), lambda i: (i, 0)),
                   pl.BlockSpec((tm, LANES), lambda i: (i, 0))],
        out_shape=[jax.ShapeDtypeStruct((m, d), F32),
                   jax.ShapeDtypeStruct((m, d), BF16),
                   jax.ShapeDtypeStruct((m, LANES), F32)],
        scratch_shapes=[pltpu.VMEM((2 * half, d), BF16)],
        compiler_params=_params(("arbitrary",)),
        name="out_proj_postnorm",
    )(a1, a2, w, h, mod, ln, router2)


def _router_operand(router):
    d, e = router.shape
    return jnp.pad(router, ((0, 0), (0, LANES - e))).astype(BF16)


def _route_kernel(lg_ref, slot_ref, aff_ref, cnt_ref, tri_ref, *, cap):
    n_tok = lg_ref.shape[2]

    @pl.when(pl.program_id(0) == 0)
    def _():
        chunk = 256
        for r0 in range(0, n_tok, chunk):
            r = r0 + lax.broadcasted_iota(I32, (chunk, n_tok), 0)
            c = lax.broadcasted_iota(I32, (chunk, n_tok), 1)
            tri_ref[r0:r0 + chunk, :] = jnp.where(r < c, 1.0, 0.0).astype(BF16)

    lg = lg_ref[0]
    ex = jnp.exp(lg - jnp.max(lg, axis=0, keepdims=True))
    aff = ex / jnp.sum(ex, axis=0, keepdims=True)
    bits = lax.bitcast_convert_type(aff, I32)
    n_e = lg.shape[0]
    count = lambda mask: jnp.sum(jnp.where(mask, 1.0, 0.0), axis=1, keepdims=True)

    def body(_, lohi):
        lo, hi = lohi
        mid = lo + jnp.right_shift(hi - lo, 1)
        ge = count(bits >= mid) >= cap
        return jnp.where(ge, mid, lo), jnp.where(ge, hi, mid)

    lo0 = jnp.zeros((n_e, 1), I32)
    hi0 = jnp.full((n_e, 1), 0x7F800000, I32)
    thr, _ = lax.fori_loop(0, 31, body, (lo0, hi0))
    gt = bits > thr
    eq = bits == thr
    need = cap - count(gt)
    pre_eq = _dot(jnp.where(eq, 1.0, 0.0).astype(BF16), tri_ref[...])
    sel = jnp.logical_or(gt, jnp.logical_and(eq, pre_eq < need))
    sel_bf = jnp.where(sel, 1.0, 0.0).astype(BF16)
    slot = _dot(sel_bf, tri_ref[...])
    slot_ref[0] = jnp.where(sel, slot.astype(I32), -1)
    aff_ref[0] = aff
    r = lax.broadcasted_iota(I32, (n_tok, LANES), 0)
    c = lax.broadcasted_iota(I32, (n_tok, LANES), 1)
    before = jnp.where(r < c * MOE_CHUNK, 1.0, 0.0).astype(BF16)
    cnt_ref[0] = _dot(sel_bf, before).astype(I32)


def _route_call(lg_t, cap):
    b, e, n_tok = lg_t.shape
    spec = pl.BlockSpec((1, e, n_tok), lambda i: (i, 0, 0))
    return pl.pallas_call(
        functools.partial(_route_kernel, cap=cap),
        grid=(b,),
        in_specs=[spec],
        out_specs=[spec, spec, pl.BlockSpec((1, e, LANES), lambda i: (i, 0, 0))],
        out_shape=[jax.ShapeDtypeStruct((b, e, n_tok), I32), jax.ShapeDtypeStruct((b, e, n_tok), F32),
                   jax.ShapeDtypeStruct((b, e, LANES), I32)],
        scratch_shapes=[pltpu.VMEM((n_tok, n_tok), BF16)],
        compiler_params=_params(("arbitrary",)),
        name="ec_route",
    )(lg_t)


def _slot_windows(cnt, cap, n_tok, chunk, win):
    bounds = cnt[:, :, 0:n_tok // MOE_CHUNK + 1:chunk // MOE_CHUNK]
    start, end = bounds[:, :, :-1], bounds[:, :, 1:]
    a = jnp.minimum((start // 16) * 16, cap - win)
    fits = jnp.all(end <= a + win, axis=1)
    return jnp.swapaxes(a, 1, 2).reshape(-1).astype(I32), fits.reshape(-1).astype(I32)


def _gather_kernel(win_ref, fit_ref, slot_ref, u_ref, o_ref, *, win, chunk):
    b, ks = pl.program_id(0), pl.program_id(1)
    n_e, cap = o_ref.shape[0], o_ref.shape[1]
    n_sub = u_ref.shape[0] // chunk
    step = b * pl.num_programs(1) + ks

    @pl.when(ks == 0)
    def _():
        o_ref[...] = jnp.zeros(o_ref.shape, o_ref.dtype)

    def place(rows, start_of, group, tok0, n_t):
        row = lax.broadcasted_iota(I32, (rows, n_t), 0)
        for g0 in range(0, n_e, group):
            starts = [start_of(e) for e in range(g0, g0 + group)]
            onehot = jnp.concatenate(
                [jnp.where(row + st == slot_ref[0, e:e + 1, tok0:tok0 + n_t], 1.0, 0.0).astype(BF16)
                 for st, e in zip(starts, range(g0, g0 + group))], axis=0)
            got = _dot(onehot, u_ref[tok0:tok0 + n_t, :]).astype(o_ref.dtype)
            for i, (st, e) in enumerate(zip(starts, range(g0, g0 + group))):
                o_ref[e, pl.ds(st, rows), :] += got[i * rows:(i + 1) * rows]

    for sub in range(n_sub):
        c = step * n_sub + sub

        @pl.when(fit_ref[c] != 0)
        def _(c=c, sub=sub):
            place(win, lambda e: pl.multiple_of(win_ref[c * n_e + e], 16), 512 // win, sub * chunk, chunk)

        @pl.when(fit_ref[c] == 0)
        def _(sub=sub):
            place(cap, lambda e: 0, 2, sub * chunk, chunk)


def _gather_call(starts, fit, slot, u2, cap, chunk, win, chunks_per_step=2):
    b, e, n_tok = slot.shape
    d = u2.shape[1]
    tt = chunk * chunks_per_step
    nc = n_tok // tt
    grid_spec = pltpu.PrefetchScalarGridSpec(
        num_scalar_prefetch=2,
        grid=(b, nc),
        in_specs=[pl.BlockSpec((1, e, tt), lambda i, k, w, f: (i, 0, k)),
                  pl.BlockSpec((tt, d), lambda i, k, w, f: (i * nc + k, 0))],
        out_specs=pl.BlockSpec((e, cap, d), lambda i, k, w, f: (0, i, 0)),
    )
    return pl.pallas_call(
        functools.partial(_gather_kernel, win=win, chunk=chunk),
        grid_spec=grid_spec,
        out_shape=jax.ShapeDtypeStruct((e, b * cap, d), BF16),
        compiler_params=_params(("arbitrary", "arbitrary")),
        name="moe_gather",
    )(starts, fit, slot, u2)


def _expert_hidden_kernel(x_ref, w1_ref, w3_ref, o_ref, *, ff):
    tf = w1_ref.shape[3]
    last = pl.num_programs(1) - 1

    def block(width):
        w = jnp.concatenate([w1_ref[0, 0, :, :width].astype(BF16), w3_ref[0, 0, :, :width].astype(BF16)], axis=1)
        h = _dot(x_ref[0], w)
        h1 = h[:, :width]
        o_ref[0, :, :width] = ((h1 / (1.0 + jnp.exp(-h1))) * h[:, width:]).astype(o_ref.dtype)

    tail = ff - (pl.cdiv(ff, tf) - 1) * tf
    if tail == tf:
        block(tf)
    else:
        pl.when(pl.program_id(1) < last)(lambda: block(tf))
        pl.when(pl.program_id(1) == last)(lambda: block(tail))


def _expert_out_kernel(hd_ref, w2_ref, o_ref):
    y = _dot(hd_ref[0], w2_ref[0, 0].astype(BF16)).astype(o_ref.dtype)
    cap = o_ref.shape[1]
    for b in range(o_ref.shape[0]):
        o_ref[b] = y[b * cap:(b + 1) * cap]


def _expert_call(xe, w1, w3, w2, layer, batch, tf=768, tn=2048):
    e, m, d = xe.shape
    cap = m // batch
    ff = w1.shape[3]
    hidden = pl.pallas_call(
        functools.partial(_expert_hidden_kernel, ff=ff),
        grid=(e, pl.cdiv(ff, tf)),
        in_specs=[pl.BlockSpec((1, m, d), lambda i, f: (i, 0, 0)),
                  pl.BlockSpec((1, 1, d, tf), lambda i, f: (layer, i, 0, f)),
                  pl.BlockSpec((1, 1, d, tf), lambda i, f: (layer, i, 0, f))],
        out_specs=pl.BlockSpec((1, m, tf), lambda i, f: (i, 0, f)),
        out_shape=jax.ShapeDtypeStruct((e, m, ff), BF16),
        compiler_params=_params(("arbitrary", "arbitrary")),
        name="moe_expert_hidden",
    )(xe, w1, w3)
    return pl.pallas_call(
        _expert_out_kernel,
        grid=(e, d // tn),
        in_specs=[pl.BlockSpec((1, m, ff), lambda i, j: (i, 0, 0)),
                  pl.BlockSpec((1, 1, ff, tn), lambda i, j: (layer, i, 0, j))],
        out_specs=pl.BlockSpec((batch, cap, tn), lambda i, j: (0, i, j)),
        out_shape=jax.ShapeDtypeStruct((batch, e * cap, d), BF16),
        compiler_params=_params(("arbitrary", "arbitrary")),
        name="moe_expert_out",
    )(hidden, w2)


def _scatter_kernel(win_ref, fit_ref, slot_ref, aff_ref, ye_ref, h_ref, mod_ref, ln_ref, *rest, with_next):
    if with_next:
        nmod_ref, h2_ref, un_ref, pt_ref, yew_ref, moe_ref = rest
    else:
        h2_ref, pt_ref, yew_ref, moe_ref = rest
    n_e = slot_ref.shape[1]
    cap = ye_ref.shape[1] // n_e
    tt = h_ref.shape[0]
    step = pl.program_id(0) * pl.num_programs(1) + pl.program_id(1)

    def gates(rows, e, start):
        row = lax.broadcasted_iota(I32, (rows, tt), 0)
        return jnp.where(row + start == slot_ref[0, e:e + 1, :], aff_ref[0, e:e + 1, :], 0.0).astype(BF16)

    @pl.when(fit_ref[step] != 0)
    def _():
        for e in range(n_e):
            start = pl.multiple_of(win_ref[step * n_e + e], 16)
            pt_ref[e * MOE_WIN:(e + 1) * MOE_WIN, :] = gates(MOE_WIN, e, start)
            yew_ref[e * MOE_WIN:(e + 1) * MOE_WIN, :] = ye_ref[0, pl.ds(e * cap + start, MOE_WIN), :]
        moe_ref[...] = _dot_tn(pt_ref[:n_e * MOE_WIN, :], yew_ref[...])

    @pl.when(fit_ref[step] == 0)
    def _():
        for e in range(n_e):
            pt_ref[e * cap:(e + 1) * cap, :] = gates(cap, e, 0)
        moe_ref[...] = _dot_tn(pt_ref[...], ye_ref[0])

    x = DEEPNORM_ALPHA * h_ref[...] + mod_ref[0] * moe_ref[...]
    hn = _layer_norm(x, ln_ref[0:1, :], ln_ref[1:2, :])
    h2_ref[...] = hn
    if with_next:
        un_ref[...] = (hn * (1.0 + nmod_ref[0, 1:2, :]) + nmod_ref[0, 0:1, :]).astype(un_ref.dtype)


def _scatter_call(win, fit, slot, aff, ye, h1, gate, ln, next_ss):
    b, e, n_tok = slot.shape
    m, d = h1.shape
    tt = MOE_CHUNK
    nt = n_tok // tt
    rows = ye.shape[1]
    with_next = next_ss is not None
    tok_spec = pl.BlockSpec((1, e, tt), lambda i, t, w, f: (i, 0, t))
    row_spec = pl.BlockSpec((tt, d), lambda i, t, w, f: (i * nt + t, 0))
    in_specs = [tok_spec, tok_spec,
                pl.BlockSpec((1, rows, d), lambda i, t, w, f: (i, 0, 0)),
                row_spec,
                pl.BlockSpec((1, 1, d), lambda i, t, w, f: (i, 0, 0)),
                pl.BlockSpec((2, d), lambda i, t, w, f: (0, 0))]
    args = [slot, aff, ye, h1, gate, ln]
    out_specs = [row_spec]
    out_shape = [jax.ShapeDtypeStruct((m, d), F32)]
    if with_next:
        in_specs.append(pl.BlockSpec((1, 2, d), lambda i, t, w, f: (i, 0, 0)))
        args.append(next_ss)
        out_specs.append(row_spec)
        out_shape.append(jax.ShapeDtypeStruct((m, d), BF16))
    grid_spec = pltpu.PrefetchScalarGridSpec(
        num_scalar_prefetch=2,
        grid=(b, nt),
        in_specs=in_specs,
        out_specs=out_specs,
        scratch_shapes=[pltpu.VMEM((rows, tt), BF16),
                        pltpu.VMEM((e * MOE_WIN, d), BF16),
                        pltpu.VMEM((tt, d), F32)],
    )
    res = pl.pallas_call(
        functools.partial(_scatter_kernel, with_next=with_next),
        grid_spec=grid_spec,
        out_shape=out_shape,
        compiler_params=_params(("arbitrary", "arbitrary")),
        name="moe_combine_postnorm",
    )(win, fit, *args)
    return res if with_next else (res[0], None)


def _ec_moe(h1, u2, logits, w1, w3, w2, layer, gate, ln, next_ss, batch, n_tok):
    cap = EC_CAPACITY_FACTOR * n_tok // N_EXPERTS
    lg_t = jnp.swapaxes(logits[:, :N_EXPERTS].reshape(batch, n_tok, N_EXPERTS), 1, 2)
    slot, aff, cnt = _route_call(lg_t, cap)
    g_win, g_fit = _slot_windows(cnt, cap, n_tok, GATHER_CHUNK, GATHER_WIN)
    xe = _gather_call(g_win, g_fit, slot, u2, cap, GATHER_CHUNK, GATHER_WIN)
    ye = _expert_call(xe, w1, w3, w2, layer, batch)
    c_win, c_fit = _slot_windows(cnt, cap, n_tok, MOE_CHUNK, MOE_WIN)
    return _scatter_call(c_win, c_fit, slot, aff, ye, h1, gate, ln, next_ss)


def _mm_kernel(a_ref, w_ref, o_ref, wb_ref):
    @pl.when(pl.program_id(1) == 0)
    def _():
        wb_ref[...] = w_ref[0].astype(wb_ref.dtype)

    o_ref[...] = _dot(a_ref[...], wb_ref[...]).astype(o_ref.dtype)


def _mm_call(a, w, layer, out_dtype, tm=1024, tn=1024):
    m, k = a.shape
    n = w.shape[2]
    return pl.pallas_call(
        _mm_kernel,
        grid=(n // tn, m // tm),
        in_specs=[pl.BlockSpec((tm, k), lambda j, i: (i, 0)),
                  pl.BlockSpec((1, k, tn), lambda j, i: (layer, 0, j))],
        out_specs=pl.BlockSpec((tm, tn), lambda j, i: (i, j)),
        out_shape=jax.ShapeDtypeStruct((m, n), out_dtype),
        scratch_shapes=[pltpu.VMEM((k, tn), BF16)],
        compiler_params=_params(("arbitrary", "arbitrary")),
        name="matmul",
    )(a, w)


def _short_conv(p, w, b):
    x = p.astype(F32)
    n = x.shape[0]
    r = lax.broadcasted_iota(I32, x.shape, 0)
    prev = jnp.where(r == 0, 0.0, pltpu.roll(x, 1, 0))
    nxt = jnp.where(r == n - 1, 0.0, pltpu.roll(x, n - 1, 0))
    return b + w[0:1, :] * prev + w[1:2, :] * x + w[2:3, :] * nxt


def _sconv_kernel(p_ref, w_ref, b_ref, o_ref):
    o_ref[...] = _short_conv(p_ref[...], w_ref[...], b_ref[...]).astype(o_ref.dtype)


def _sconv_call(p, conv_w, conv_b, col0, ncols, out_dtype, batch, n_tok, tc=512):
    c0 = col0 // tc
    return pl.pallas_call(
        _sconv_kernel,
        grid=(batch, ncols // tc),
        in_specs=[pl.BlockSpec((n_tok, tc), lambda b, j: (b, c0 + j)),
                  pl.BlockSpec((HY_SHORT, tc), lambda b, j: (0, c0 + j)),
                  pl.BlockSpec((1, tc), lambda b, j: (0, c0 + j))],
        out_specs=pl.BlockSpec((n_tok, tc), lambda b, j: (b, j)),
        out_shape=jax.ShapeDtypeStruct((batch * n_tok, ncols), out_dtype),
        compiler_params=_params(("arbitrary", "arbitrary")),
        name="hyena_short_conv",
    )(p, conv_w, conv_b.reshape(1, -1))


def _dft_fwd_kernel(fc_ref, fs_ref, r1_ref, r2_ref, *rest, spectral):
    o_ref = rest[-1]
    sub = fc_ref.shape[0] // SEQMIX_SUBTILES
    for r0 in range(0, fc_ref.shape[0], sub):
        rs = slice(r0, r0 + sub)
        xr = _dot(fc_ref[rs, :], r1_ref[...])
        xi = _dot(fs_ref[rs, :], r2_ref[...])
        if spectral:
            h_ref, nyq_ref = rest[:2]
            hr = h_ref[0, 0, rs, :]
            hi = h_ref[0, 1, rs, :]
            if r0 == 0:
                dc = lax.broadcasted_iota(I32, xr.shape, 0) == 0
                yr = xr * hr - jnp.where(dc, 0.0, xi * hi)
                yi = jnp.where(dc, xi * nyq_ref[...], xr * hi + xi * hr)
            else:
                yr = xr * hr - xi * hi
                yi = xr * hi + xi * hr
        else:
            yr, yi = xr, xi
        o_ref[0, 0, rs, :] = yr.astype(o_ref.dtype)
        o_ref[0, 1, rs, :] = yi.astype(o_ref.dtype)


def _resident_spec(shape):
    return pl.BlockSpec(shape, lambda *_: (0,) * len(shape), pipeline_mode=pl.Buffered(1))


def _dft_fwd_call(fc, fs, r1, r2, spec, out_dtype, batch, n_tok, tn=256):
    ncols = r1.shape[1]
    r_spec = pl.BlockSpec((n_tok, tn), lambda j, b: (b, j))
    in_specs = [_resident_spec(fc.shape), _resident_spec(fs.shape), r_spec, r_spec]
    args = [fc, fs, r1, r2]
    if spec is not None:
        h, nyq, hcol0 = spec
        c0 = hcol0 // tn
        in_specs += [pl.BlockSpec((1, 2, n_tok, tn), lambda j, b: (0, 0, 0, c0 + j)),
                     pl.BlockSpec((1, tn), lambda j, b: (0, c0 + j))]
        args += [h, nyq]
    return pl.pallas_call(
        functools.partial(_dft_fwd_kernel, spectral=spec is not None),
        grid=(ncols // tn, batch),
        in_specs=in_specs,
        out_specs=pl.BlockSpec((1, 2, n_tok, tn), lambda j, b: (b, 0, 0, j)),
        out_shape=jax.ShapeDtypeStruct((batch, 2, n_tok, ncols), out_dtype),
        compiler_params=_params(("arbitrary", "arbitrary")),
        name="dft_forward",
    )(*args)


def _dual_kernel(gc_ref, gs_ref, y_ref, p_ref, cw_ref, cb_ref, z_ref, skip_ref, o_ref):
    yr, yi = y_ref[0, 0], y_ref[0, 1]
    gate = _short_conv(p_ref[...], cw_ref[...], cb_ref[...])
    sub = o_ref.shape[0] // SEQMIX_SUBTILES
    for r0 in range(0, o_ref.shape[0], sub):
        rs = slice(r0, r0 + sub)
        y = _dot(gc_ref[rs, :], yr) + _dot(gs_ref[rs, :], yi)
        o_ref[rs, :] = (gate[rs] * (y + skip_ref[...] * z_ref[rs, :].astype(F32))).astype(o_ref.dtype)


def _dual_call(gc, gs, y_spec, p, conv_w, conv_b, pcol0, z, skip, batch, n_tok, ncols, tn=256):
    c0 = pcol0 // tn
    o_spec = pl.BlockSpec((n_tok, tn), lambda b, j: (b, j))
    return pl.pallas_call(
        _dual_kernel,
        grid=(batch, ncols // tn),
        in_specs=[_resident_spec(gc.shape), _resident_spec(gs.shape),
                  pl.BlockSpec((1, 2, n_tok, tn), lambda b, j: (b, 0, 0, j)),
                  pl.BlockSpec((n_tok, tn), lambda b, j: (b, c0 + j)),
                  pl.BlockSpec((HY_SHORT, tn), lambda b, j: (0, c0 + j)),
                  pl.BlockSpec((1, tn), lambda b, j: (0, c0 + j)),
                  o_spec,
                  pl.BlockSpec((1, tn), lambda b, j: (0, j))],
        out_specs=o_spec,
        out_shape=jax.ShapeDtypeStruct((batch * n_tok, ncols), BF16),
        compiler_params=_params(("arbitrary", "arbitrary")),
        name="seq_mix_matmul",
    )(gc, gs, y_spec, p, conv_w, conv_b.reshape(1, -1), z, skip)


def _fnet_kernel(cl_ref, msl_ref, x_ref, cw_ref, o_ref):
    gw = cw_ref.shape[1]
    for c0 in range(0, x_ref.shape[1], gw):
        x = x_ref[:, c0:c0 + gw]
        seq = jnp.concatenate([_dot(cl_ref[...], x).astype(BF16), _dot(msl_ref[...], x).astype(BF16)], axis=1)
        o_ref[:, c0:c0 + gw] = _dot(seq, cw_ref[...]).astype(o_ref.dtype)


def _fnet_call(cl, msl, p, cw, col0, batch, n_tok, groups_per_step=2):
    gw = cw.shape[1]
    tw = gw * groups_per_step
    c0 = col0 // tw
    return pl.pallas_call(
        _fnet_kernel,
        grid=(batch, FN_GROUPS // groups_per_step),
        in_specs=[_resident_spec(cl.shape), _resident_spec(msl.shape),
                  pl.BlockSpec((n_tok, tw), lambda b, g: (b, c0 + g)),
                  pl.BlockSpec(cw.shape, lambda b, g: (0, 0))],
        out_specs=pl.BlockSpec((n_tok, tw), lambda b, g: (b, g)),
        out_shape=jax.ShapeDtypeStruct((batch * n_tok, FN_GROUPS * gw), BF16),
        compiler_params=_params(("arbitrary", "arbitrary")),
        name="fnet_mix",
    )(cl, msl, p, cw)


def _cos_sin_matrix(n_rows, n_cols, period, split=64):
    r = np.arange(n_rows, dtype=np.int64)[:, None]
    c_hi = (np.arange(n_cols // split, dtype=np.int64) * split)[None, :]
    c_lo = np.arange(split, dtype=np.int64)[None, :]
    ang = lambda c: 2.0 * np.pi * ((r * c) % period).astype(np.float64) / period
    ca, sa = jnp.asarray(np.cos(ang(c_hi)), F32), jnp.asarray(np.sin(ang(c_hi)), F32)
    cb, sb = jnp.asarray(np.cos(ang(c_lo)), F32), jnp.asarray(np.sin(ang(c_lo)), F32)
    cos = ca[:, :, None] * cb[:, None, :] - sa[:, :, None] * sb[:, None, :]
    sin = sa[:, :, None] * cb[:, None, :] + ca[:, :, None] * sb[:, None, :]
    return cos.reshape(n_rows, n_cols), sin.reshape(n_rows, n_cols)


def _trig_kernel(ta_ref, tb_ref, ea_ref, eb_ref, *o_refs, mode, period, scale):
    sa_ca = _dot(ta_ref[...], ea_ref[...])
    sb_cb = _dot(tb_ref[...], eb_ref[...])
    nc = sa_ca.shape[1] // 2
    ca, sa, cb, sb = sa_ca[:, :nc], sa_ca[:, nc:], sb_cb[:, :nc], sb_cb[:, nc:]
    cos = ca * cb - sa * sb
    sin = sa * cb + ca * sb
    tm = cos.shape[0]
    row = pl.program_id(0) * tm + lax.broadcasted_iota(I32, cos.shape, 0)
    col = lax.broadcasted_iota(I32, cos.shape, 1)
    alt = lambda idx: (1 - 2 * jnp.bitwise_and(idx, 1)).astype(F32)
    if mode == "hyena":
        fc_ref, fs_ref, gc_ref, gs_ref = o_refs
        fc_ref[...] = cos.astype(fc_ref.dtype)
        fs_ref[...] = jnp.where(row == 0, alt(col), -sin).astype(fs_ref.dtype)
        gc_ref[...] = (cos * jnp.where(col == 0, 1.0 / period, 2.0 / period)).astype(gc_ref.dtype)
        gs_ref[...] = jnp.where(col == 0, alt(row) * (1.0 / period), sin * (-2.0 / period)).astype(gs_ref.dtype)
    else:
        c_ref, ms_ref = o_refs
        c_ref[...] = (cos * scale).astype(c_ref.dtype)
        ms_ref[...] = (sin * (-scale)).astype(ms_ref.dtype)


def _split3(x):
    bf = jnp.bfloat16
    x = np.asarray(x, np.float32)
    hi = x.astype(bf)
    r1 = x - hi.astype(np.float32)
    mid = r1.astype(bf)
    lo = (r1 - mid.astype(np.float32)).astype(bf)
    return [hi, mid, lo]


def _trig_call(n, period, mode, scale=1.0, split=64, tm=256):
    r = np.arange(n, dtype=np.int64)[:, None]
    c_hi = (np.arange(n // split, dtype=np.int64) * split)[None, :]
    c_lo = np.arange(split, dtype=np.int64)[None, :]
    ang = lambda c: 2.0 * np.pi * ((r * c) % period).astype(np.float64) / period
    def operands(a, col_group):
        terms = _split3(np.cos(a)) + _split3(np.sin(a))
        k = a.shape[1]
        spread = (col_group[None, :] == np.arange(k)[:, None]).astype(np.float32)
        zero = np.zeros_like(spread)
        e = np.concatenate([np.concatenate([spread, zero], 1)] * 3 + [np.concatenate([zero, spread], 1)] * 3, 0)
        return jnp.asarray(np.concatenate(terms, axis=1)), jnp.asarray(e, BF16)
    cols = np.arange(n)
    ta, ea = operands(ang(c_hi), cols // split)
    tb, eb = operands(ang(c_lo), cols % split)
    n_out = 4 if mode == "hyena" else 2
    o_spec = pl.BlockSpec((tm, n), lambda i: (i, 0))
    return pl.pallas_call(
        functools.partial(_trig_kernel, mode=mode, period=period, scale=scale),
        grid=(n // tm,),
        in_specs=[pl.BlockSpec((tm, ta.shape[1]), lambda i: (i, 0)),
                  pl.BlockSpec((tm, tb.shape[1]), lambda i: (i, 0)),
                  pl.BlockSpec(ea.shape, lambda i: (0, 0)),
                  pl.BlockSpec(eb.shape, lambda i: (0, 0))],
        out_specs=[o_spec] * n_out,
        out_shape=[jax.ShapeDtypeStruct((n, n), BF16)] * n_out,
        compiler_params=_params(("arbitrary",)),
        name="dft_tables",
    )(ta, tb, ea, eb)


def _hyena_dft_operands(n_tok):
    return _trig_call(n_tok, 2 * n_tok, "hyena")


def _fnet_operands(n_tok, gw):
    cl, msl = _trig_call(n_tok, n_tok, "fnet", scale=1.0 / math.sqrt(n_tok * gw))
    cw, sw = _cos_sin_matrix(gw, gw, gw)
    return cl, msl, jnp.concatenate([cw, sw], axis=0).astype(BF16)


def _filter_kernel(z_ref, w1_ref, b1_ref, f1_ref, w2_ref, b2_ref, f2_ref, w3f_ref, w3b_ref, dl_ref,
                   sum_ref, dif_ref, nyq_ref, hdn_ref):
    @pl.when(jnp.logical_and(pl.program_id(0) == 0, pl.program_id(1) == 0))
    def _():
        hdot = lambda a, b: jnp.dot(a, b, precision=lax.Precision.HIGHEST, preferred_element_type=F32)
        hdn = jnp.sin(f1_ref[...] * (hdot(z_ref[...], w1_ref[...]) + b1_ref[...]))
        hdn = jnp.sin(f2_ref[...] * (hdot(hdn, w2_ref[...]) + b2_ref[...]))
        hdn_ref[...] = hdn.astype(hdn_ref.dtype)

    n = hdn_ref.shape[0]
    row = lax.broadcasted_iota(I32, (n, dl_ref.shape[1]), 0)
    decay = jnp.exp(-(row.astype(F32) / (n - 1)) * dl_ref[...])
    hf = _dot(hdn_ref[...], w3f_ref[...].astype(BF16)) * decay
    hb = jnp.where(row == 0, 0.0, _dot(hdn_ref[...], w3b_ref[...].astype(BF16)) * decay)
    tot = hf + hb
    sum_ref[...] = tot.astype(sum_ref.dtype)
    dif_ref[...] = (hf - hb).astype(dif_ref.dtype)
    nyq_ref[...] = jnp.sum(jnp.where(jnp.bitwise_and(row, 1) == 0, tot, -tot), axis=0, keepdims=True)


def _filter_call(n_tok, fw1, fb1, ff1, fw2, fb2, ff2, fw3, tc=512):
    t01 = np.linspace(0.0, 1.0, n_tok, dtype=np.float32)
    w = (2.0 * math.pi * np.arange(n_tok, dtype=np.float32) / n_tok).astype(np.float32)
    bands = np.linspace(1e-4, HY_BANDS - 1, HY_BANDS, dtype=np.float32)
    z = np.concatenate([t01[:, None], np.cos(w[:, None] * bands), -np.sin(w[:, None] * bands)], -1)
    deltas = np.abs(np.linspace(HY_MIN_DECAY, HY_MAX_DECAY, HY_W, dtype=np.float32))[None, :]
    emb, hid = fw1.shape
    nc = HY_W // tc
    full = lambda shape: pl.BlockSpec(shape, lambda o, j: (0,) * len(shape))
    o_spec = pl.BlockSpec((n_tok, tc), lambda o, j: (0, o * nc + j))
    return pl.pallas_call(
        _filter_kernel,
        grid=(HY_ORDER, nc),
        in_specs=[full((n_tok, emb)), full((emb, hid)), full((1, hid)), full((1, hid)),
                  full((hid, hid)), full((1, hid)), full((1, hid)),
                  pl.BlockSpec((hid, tc), lambda o, j: (0, (2 * o) * nc + j)),
                  pl.BlockSpec((hid, tc), lambda o, j: (0, (2 * o + 1) * nc + j)),
                  pl.BlockSpec((1, tc), lambda o, j: (0, j))],
        out_specs=[o_spec, o_spec, pl.BlockSpec((1, tc), lambda o, j: (0, o * nc + j))],
        out_shape=[jax.ShapeDtypeStruct((n_tok, HY_ORDER * HY_W), BF16),
                   jax.ShapeDtypeStruct((n_tok, HY_ORDER * HY_W), BF16),
                   jax.ShapeDtypeStruct((1, HY_ORDER * HY_W), F32)],
        scratch_shapes=[pltpu.VMEM((n_tok, hid), BF16)],
        compiler_params=_params(("arbitrary", "arbitrary")),
        name="hyena_filters",
    )(jnp.asarray(z.astype(np.float32)), fw1, fb1.reshape(1, hid), ff1.reshape(1, hid),
      fw2, fb2.reshape(1, hid), ff2.reshape(1, hid), fw3, fw3, jnp.asarray(deltas))


def kernel(x, c, ctx, c_ctx, ada_w, ada_b, ln1_g, ln1_b, ln2_g, ln2_b, router, exp_w1, exp_w3, exp_w2,
           ab_w_in, ab_kv_norm, ab_w_ukv, ab_rpb, ab_w_out,
           cd_w_in, cd_conv_w, cd_conv_b, cd_filt_w1, cd_filt_b1, cd_filt_freq1, cd_filt_w2, cd_filt_b2,
           cd_filt_freq2, cd_filt_w3, cd_skip, cd_w_out):
    batch, n_tok, d = x.shape
    lc = ctx.shape[1]
    x2d = x.reshape(batch * n_tok, d)
    ctx2d = ctx.reshape(batch * lc, d)

    pad_rows = (-(batch + 1)) % 8
    cv = jnp.concatenate([c, c_ctx[None, :], jnp.zeros((pad_rows, d), F32)], axis=0)
    ada = _ada_call(cv, ada_w, ada_b)
    mods = [ada[i, :batch].reshape(batch, 6, d) for i in range(DEPTH)]
    ctx_ss = ada[0, batch, :2 * d].reshape(1, 2, d)

    w_in = _w_in_prep_call(jnp.swapaxes(ab_w_in, 1, 2), 0)
    p_lat = _mm_rope_call(x2d, mods[0][:, 0:2], n_tok, w_in, _rope_tables(n_tok), tm=1024)
    p_ctx = _mm_rope_call(ctx2d, ctx_ss, batch * lc, w_in, _identity_rope_tables(batch * lc), tm=batch * lc,
                          tile0=CTX_TILE0)
    w_ukv = _permute_w_ukv(ab_w_ukv[0])
    kv_lat = _kvup_call(p_lat, ab_kv_norm[0], w_ukv, tm=1024)
    kv_ctx = _kvup_call(p_ctx, ab_kv_norm[0], w_ukv, tm=lc, tile0=CTX_TILE0)
    a_out = _mla_call(p_lat, kv_lat, p_ctx, kv_ctx, batch, n_tok, lc)
    var_map, na_er, na_ec = _na_tables(n_tok // GRID_W)
    b_out = _na_call(p_lat, p_ctx, _na_bias(ab_rpb[0], na_er, na_ec), var_map, batch, n_tok, lc)
    ln1 = jnp.stack([ln1_g, ln1_b], axis=1)
    ln2 = jnp.stack([ln2_g, ln2_b], axis=1)
    h1, u2, logits = _outln_call(a_out, b_out, ab_w_out, 0, x2d,
                                 mods[0][:, 2:5], ln1[0], _router_operand(router[0]), n_tok)
    h, u = _ec_moe(h1, u2, logits, exp_w1, exp_w3, exp_w2, 0, mods[0][:, 5:6], ln2[0],
                   mods[1][:, 0:2], batch, n_tok)

    p1 = _mm_call(u, cd_w_in, 0, BF16)
    s0 = _sconv_call(p1, cd_conv_w[0], cd_conv_b[0], 0, HY_W, BF16, batch, n_tok)
    h_sum, h_dif, h_nyq = _filter_call(n_tok, cd_filt_w1[0], cd_filt_b1[0], cd_filt_freq1[0], cd_filt_w2[0],
                                       cd_filt_b2[0], cd_filt_freq2[0], cd_filt_w3[0])
    fc, fs, gc, gs = _hyena_dft_operands(n_tok)
    h_spec = _dft_fwd_call(fc, fs, h_sum, h_dif, None, F32, 1, n_tok)
    z = s0
    for o in range(HY_ORDER):
        y_spec = _dft_fwd_call(fc, fs, z, z, (h_spec, h_nyq, o * HY_W), BF16, batch, n_tok)
        z = _dual_call(gc, gs, y_spec, p1, cd_conv_w[0], cd_conv_b[0], (o + 1) * HY_W, z, cd_skip[0][o:o + 1],
                       batch, n_tok, HY_W)
    cl, msl, cw = _fnet_operands(n_tok, FN_GROUP_W)
    y_fn = _fnet_call(cl, msl, p1, cw, HY_IN_W, batch, n_tok)
    h1, u2, logits = _outln_call(z, y_fn, cd_w_out, 0, h,
                                 mods[1][:, 2:5], ln1[1], _router_operand(router[1]), n_tok)
    h, _ = _ec_moe(h1, u2, logits, exp_w1, exp_w3, exp_w2, 1, mods[1][:, 5:6], ln2[1],
                   None, batch, n_tok)
    return h.reshape(batch, n_tok, d)
```

```python
import functools
import math

import numpy as np
import jax
import jax.numpy as jnp
from jax import lax
from jax.experimental import pallas as pl
from jax.experimental.pallas import tpu as pltpu

F32 = jnp.float32
BF16 = jnp.bfloat16
I32 = jnp.int32

DEPTH = 2
GRID_W = 64
DEEPNORM_ALPHA = (2.0 * DEPTH) ** 0.25
LN_EPS = 1e-6
NEG_INF = -1e30
LOG2E = math.log2(math.e)

MLA_HEADS = 8
MLA_NOPE = 128
MLA_ROPE = 64
MLA_QK = MLA_NOPE + MLA_ROPE
MLA_V = 128
MLA_KV_RANK = 512
ROPE_THETA = 10000.0

NA_HEADS = 8
NA_HEAD_DIM = 128
NA_KH = 8
NA_KW = 16

OFF_CKV = MLA_HEADS * MLA_QK
OFF_KPE = OFF_CKV + MLA_KV_RANK
OFF_QNA = OFF_KPE + MLA_ROPE

HY_W = 1024
HY_ORDER = 2
HY_IN_W = (HY_ORDER + 1) * HY_W
HY_SHORT = 3
HY_BANDS = 16
HY_DECAY_TARGET = 1e-2
HY_MIN_DECAY = math.log(HY_DECAY_TARGET) / 1.5
HY_MAX_DECAY = math.log(HY_DECAY_TARGET) / 0.3

FN_W = 1024
FN_GROUPS = 4
FN_GROUP_W = FN_W // FN_GROUPS

N_EXPERTS = 16
EC_CAPACITY_FACTOR = 2

LANES = 128
VMEM_LIMIT = 56 * 1024 * 1024

QTILE_W = 2 * LANES
T_KPE = MLA_HEADS
T_CKV = T_KPE + 2
T_QNA = T_CKV + MLA_KV_RANK // QTILE_W
T_KNA = T_QNA + NA_HEADS * NA_HEAD_DIM // QTILE_W
T_VNA = T_KNA + NA_HEADS * NA_HEAD_DIM // QTILE_W
N_ABTILES = T_VNA + NA_HEADS * NA_HEAD_DIM // QTILE_W
AB_PERM_W = N_ABTILES * QTILE_W
CTX_TILE0 = T_KPE
INPROJ_TILES_PER_STEP = 8

MLA_SUBTILES = 8
OUTLN_SUBTILES = 2
SEQMIX_SUBTILES = 4
MOE_CHUNK = 256
MOE_WINS = (64, 128)
NA_G = 4
NA_WR = NA_KH + NA_G - 1
NA_HEADS_PER_STEP = 2


def _params(sem, vmem=VMEM_LIMIT):
    return pltpu.CompilerParams(dimension_semantics=sem, vmem_limit_bytes=vmem)


def _dot(a, b):
    return jnp.dot(a, b, preferred_element_type=F32)


def _dot_nt(a, b):
    return lax.dot_general(a, b, (((1,), (1,)), ((), ())), preferred_element_type=F32)


def _dot_tn(a, b):
    return lax.dot_general(a, b, (((0,), (0,)), ((), ())), preferred_element_type=F32)


def _ada_kernel(c_ref, w_ref, b_ref, o_ref):
    c = c_ref[...]
    a = c / (1.0 + jnp.exp(-c))
    o_ref[0] = _dot(a.astype(BF16), w_ref[0].astype(BF16)) + b_ref[0]


def _ada_call(cv, ada_w, ada_b, tn=2048):
    depth, d, n = ada_w.shape
    rows = cv.shape[0]
    return pl.pallas_call(
        _ada_kernel,
        grid=(depth, n // tn),
        in_specs=[pl.BlockSpec((rows, d), lambda l, j: (0, 0)),
                  pl.BlockSpec((1, d, tn), lambda l, j: (l, 0, j)),
                  pl.BlockSpec((1, 1, tn), lambda l, j: (l, 0, j))],
        out_specs=pl.BlockSpec((1, rows, tn), lambda l, j: (l, 0, j)),
        out_shape=jax.ShapeDtypeStruct((depth, rows, n), F32),
        compiler_params=_params(("arbitrary", "arbitrary")),
        name="ada_params",
    )(cv, ada_w, ada_b.reshape(depth, 1, n))


def _mm_rope_kernel(x_ref, ss_ref, w_ref, ta_ref, tb_ref, tc_ref, o_ref, a_ref, *, tile0):
    @pl.when(pl.program_id(1) == 0)
    def _():
        a_ref[...] = (x_ref[...] * (1.0 + ss_ref[0, 1:2, :]) + ss_ref[0, 0:1, :]).astype(a_ref.dtype)

    a = a_ref[...]
    for q in range(INPROJ_TILES_PER_STEP):
        t = tile0 + INPROJ_TILES_PER_STEP * pl.program_id(1) + q
        c0 = q * QTILE_W
        acc = _dot_nt(a, w_ref[c0:c0 + QTILE_W, :])
        is_rope = t <= T_KPE
        f = jnp.where(t < T_KPE, LOG2E * MLA_QK ** -0.5,
                      jnp.where(jnp.logical_and(t >= T_QNA, t < T_KNA), LOG2E * NA_HEAD_DIM ** -0.5, 1.0)
                      ).astype(F32)
        hi = acc[:, LANES:]
        rot = (hi * jnp.where(is_rope, ta_ref[...], 1.0)
               + pltpu.roll(hi, LANES - MLA_ROPE // 2, 1) * jnp.where(is_rope, tb_ref[...], 0.0)
               + pltpu.roll(hi, MLA_ROPE // 2, 1) * jnp.where(is_rope, tc_ref[...], 0.0))
        o_ref[:, c0:c0 + LANES] = (acc[:, :LANES] * f).astype(o_ref.dtype)
        o_ref[:, c0 + LANES:c0 + QTILE_W] = (rot * f).astype(o_ref.dtype)


def _mm_rope_call(x, ss, rows_per_group, w, tabs, tm, tile0=0):
    m, k = x.shape
    tn = INPROJ_TILES_PER_STEP * QTILE_W
    n = w.shape[0] - tile0 * QTILE_W
    j0 = tile0 // INPROJ_TILES_PER_STEP
    ntab = tabs[0].shape[0] // tm
    tpg = rows_per_group // tm
    tab_spec = pl.BlockSpec((tm, LANES), lambda i, j: (i % ntab, 0))
    return pl.pallas_call(
        functools.partial(_mm_rope_kernel, tile0=tile0),
        grid=(m // tm, n // tn),
        in_specs=[pl.BlockSpec((tm, k), lambda i, j: (i, 0)),
                  pl.BlockSpec((1, 2, k), lambda i, j: (i // tpg, 0, 0)),
                  pl.BlockSpec((tn, k), lambda i, j: (j0 + j, 0)),
                  tab_spec, tab_spec, tab_spec],
        out_specs=pl.BlockSpec((tm, tn), lambda i, j: (i, j)),
        out_shape=jax.ShapeDtypeStruct((m, n), BF16),
        scratch_shapes=[pltpu.VMEM((tm, k), BF16)],
        compiler_params=_params(("arbitrary", "arbitrary")),
        name="attn_in_proj",
    )(x, ss, w, *tabs)


def _rope_tables(n_tok):
    t = np.arange(n_tok)
    row = (t // GRID_W).astype(np.float32)
    col = (t % GRID_W).astype(np.float32)
    n_freq = MLA_ROPE // 4
    inv = (ROPE_THETA ** (-np.arange(n_freq, dtype=np.float32) / n_freq)).astype(np.float32)
    ang = np.concatenate([row[:, None] * inv, col[:, None] * inv], axis=1)
    cos, sin = np.cos(ang).astype(np.float32), np.sin(ang).astype(np.float32)
    half = MLA_ROPE // 2
    ta = np.zeros((n_tok, LANES), np.float32)
    tb = np.zeros((n_tok, LANES), np.float32)
    tc = np.zeros((n_tok, LANES), np.float32)
    ta[:, :half] = cos
    ta[:, half:2 * half] = cos
    tb[:, :half] = -sin
    tc[:, half:2 * half] = sin
    return jnp.asarray(ta), jnp.asarray(tb), jnp.asarray(tc)


def _identity_rope_tables(n_tok):
    ta = np.zeros((n_tok, LANES), np.float32)
    ta[:, :MLA_ROPE] = 1.0
    z = np.zeros((n_tok, LANES), np.float32)
    return jnp.asarray(ta), jnp.asarray(z), jnp.asarray(z)


def _w_in_prep_kernel(w_ref, o_ref):
    x = w_ref[0]
    zeros = lambda n: jnp.zeros((n, x.shape[1]), x.dtype)
    rows = []
    for h in range(MLA_HEADS):
        rows += [x[h * MLA_QK:(h + 1) * MLA_QK, :], zeros(QTILE_W - MLA_QK)]
    rows += [zeros(LANES), x[OFF_KPE:OFF_QNA, :], zeros(LANES - MLA_ROPE), zeros(QTILE_W),
             x[OFF_CKV:OFF_KPE, :], x[OFF_QNA:, :]]
    o_ref[...] = jnp.concatenate(rows, axis=0).astype(o_ref.dtype)


def _w_in_prep_call(w_in_t, layer, tk=256):
    _, n, d = w_in_t.shape
    return pl.pallas_call(
        _w_in_prep_kernel,
        grid=(d // tk,),
        in_specs=[pl.BlockSpec((1, n, tk), lambda i: (layer, 0, i))],
        out_specs=pl.BlockSpec((AB_PERM_W, tk), lambda i: (0, i)),
        out_shape=jax.ShapeDtypeStruct((AB_PERM_W, d), BF16),
        compiler_params=_params(("arbitrary",)),
        name="attn_w_in_prep",
    )(w_in_t)


def _kvup_kernel(x_ref, g_ref, w_ref, o_ref):
    x = x_ref[...].astype(F32)
    y = x * lax.rsqrt(jnp.mean(x * x, axis=-1, keepdims=True) + LN_EPS) * g_ref[...]
    o_ref[...] = _dot(y.astype(BF16), w_ref[...]).astype(o_ref.dtype)


def _kvup_call(p, g, w, tm, tile0=0):
    m = p.shape[0]
    r, n = w.shape
    return pl.pallas_call(
        _kvup_kernel,
        grid=(m // tm,),
        in_specs=[pl.BlockSpec((tm, r), lambda i: (i, (T_CKV - tile0) * QTILE_W // r)),
                  pl.BlockSpec((1, r), lambda i: (0, 0)),
                  pl.BlockSpec((r, n), lambda i: (0, 0))],
        out_specs=pl.BlockSpec((tm, n), lambda i: (i, 0)),
        out_shape=jax.ShapeDtypeStruct((m, n), BF16),
        compiler_params=_params(("arbitrary",)),
        name="kv_up",
    )(p, g.reshape(1, r), w)


def _permute_w_ukv(w):
    r = w.shape[0]
    w3 = w.reshape(r, MLA_HEADS, MLA_NOPE + MLA_V)
    return jnp.concatenate([w3[:, :, :MLA_NOPE].reshape(r, -1), w3[:, :, MLA_NOPE:].reshape(r, -1)],
                           axis=1).astype(BF16)


def _mla_kernel(q_ref, kn_ref, kp_ref, v_ref, kcn_ref, kcp_ref, vc_ref, o_ref, kcat, vcat):
    s_len = kn_ref.shape[0]
    n_h = kcat.shape[0]
    for h in range(n_h):
        kcat[h, :s_len, :LANES] = kn_ref[:, h * LANES:(h + 1) * LANES]
        kcat[h, :s_len, LANES:] = kp_ref[...]
        kcat[h, s_len:, :LANES] = kcn_ref[:, h * LANES:(h + 1) * LANES]
        kcat[h, s_len:, LANES:] = kcp_ref[...]
        vcat[h, :s_len, :] = v_ref[:, h * LANES:(h + 1) * LANES]
        vcat[h, s_len:, :] = vc_ref[:, h * LANES:(h + 1) * LANES]

    sub = q_ref.shape[0] // MLA_SUBTILES
    for h in range(n_h):
        for r0 in range(0, q_ref.shape[0], sub):
            s = _dot_nt(q_ref[r0:r0 + sub, h * QTILE_W:(h + 1) * QTILE_W], kcat[h])
            m = jnp.max(s, axis=-1, keepdims=True)
            p = jnp.exp2(s - m)
            l = jnp.sum(p, axis=-1, keepdims=True)
            o = _dot(p.astype(BF16), vcat[h])
            o_ref[r0:r0 + sub, h * LANES:(h + 1) * LANES] = (o / l).astype(o_ref.dtype)


def _mla_call(p_lat, kv_lat, p_ctx, kv_ctx, batch, s_len, lc, heads_per_step=2):
    hp = heads_per_step
    kpe_blk = T_KPE * 2 + 1
    v0 = MLA_HEADS // hp
    return pl.pallas_call(
        _mla_kernel,
        grid=(batch, MLA_HEADS // hp),
        in_specs=[pl.BlockSpec((s_len, hp * QTILE_W), lambda b, h: (b, h)),
                  pl.BlockSpec((s_len, hp * LANES), lambda b, h: (b, h)),
                  pl.BlockSpec((s_len, LANES), lambda b, h: (b, kpe_blk)),
                  pl.BlockSpec((s_len, hp * LANES), lambda b, h: (b, v0 + h)),
                  pl.BlockSpec((lc, hp * LANES), lambda b, h: (b, h)),
                  pl.BlockSpec((lc, LANES), lambda b, h: (b, kpe_blk - 2 * CTX_TILE0)),
                  pl.BlockSpec((lc, hp * LANES), lambda b, h: (b, v0 + h))],
        out_specs=pl.BlockSpec((s_len, hp * LANES), lambda b, h: (b, h)),
        out_shape=jax.ShapeDtypeStruct((batch * s_len, MLA_HEADS * MLA_V), BF16),
        scratch_shapes=[pltpu.VMEM((hp, s_len + lc, QTILE_W), BF16),
                        pltpu.VMEM((hp, s_len + lc, LANES), BF16)],
        compiler_params=_params(("arbitrary", "arbitrary")),
        name="mla_attention",
    )(p_lat, kv_lat, p_lat, kv_lat, kv_ctx, p_ctx, kv_ctx)


def _na_kernel(q_ref, k_ref, v_ref, kc_ref, vc_ref, bias_ref, o_ref, *, rows, var_map):
    win = NA_WR * GRID_W
    tq = NA_G * GRID_W
    for h in range(q_ref.shape[1] // LANES):
        hs = slice(h * LANES, (h + 1) * LANES)
        for g in range(rows // NA_G):
            start = int(np.clip(NA_G * g - NA_KH // 2, 0, rows - NA_WR)) * GRID_W
            q = q_ref[g * tq:(g + 1) * tq, hs]
            sw = _dot_nt(q, k_ref[start:start + win, hs]) + bias_ref[var_map[g], h]
            sc = _dot_nt(q, kc_ref[:, hs])
            m = jnp.maximum(jnp.max(sw, axis=-1, keepdims=True), jnp.max(sc, axis=-1, keepdims=True))
            pw = jnp.exp2(sw - m)
            pc = jnp.exp2(sc - m)
            l = jnp.sum(pw, axis=-1, keepdims=True) + jnp.sum(pc, axis=-1, keepdims=True)
            o = _dot(pw.astype(BF16), v_ref[start:start + win, hs]) + _dot(pc.astype(BF16), vc_ref[:, hs])
            o_ref[g * tq:(g + 1) * tq, hs] = (o / l).astype(o_ref.dtype)


def _na_tables(rows):
    kh = min(NA_KH, rows)
    ng = rows // NA_G
    qr = np.arange(NA_G)[:, None]
    kr = np.arange(NA_WR)[None, :]
    sel_rows = []
    for g in range(ng):
        start_row = int(np.clip(NA_G * g - NA_KH // 2, 0, rows - NA_WR))
        r = NA_G * g + qr
        krow = start_row + kr
        rs = np.clip(r - kh // 2, 0, rows - kh)
        ok = (krow >= rs) & (krow < rs + kh)
        dr = krow - r + (NA_KH - 1)
        sel_rows.append(np.stack([ok & (dr == d) for d in range(2 * NA_KH - 1)]))
    sel_rows = np.stack(sel_rows)
    uniq, inverse = np.unique(sel_rows.reshape(ng, -1), axis=0, return_inverse=True)
    er = uniq.reshape((-1,) + sel_rows.shape[1:]).astype(np.float32)
    qc = np.arange(GRID_W)[:, None]
    kc = np.arange(GRID_W)[None, :]
    cs = np.clip(qc - NA_KW // 2, 0, GRID_W - NA_KW)
    ok_c = (kc >= cs) & (kc < cs + NA_KW)
    dc = np.clip(kc - qc, -(NA_KW - 1), NA_KW - 1) + (NA_KW - 1)
    ec = np.stack([ok_c & (dc == e) for e in range(2 * NA_KW - 1)]).astype(np.float32)
    return inverse.reshape(-1).astype(np.int32), er, ec


def _na_bias(rpb, er, ec):
    hp = lax.Precision.HIGHEST
    t = jnp.einsum('hde,eqk->hdqk', rpb.astype(F32), jnp.asarray(ec), precision=hp)
    t = jnp.where(jnp.asarray(ec.sum(axis=0) > 0.5), t * LOG2E, NEG_INF)
    outside = jnp.full(t[:, 0].shape, NEG_INF, F32)
    d_of = np.where(er.sum(axis=1) > 0.5, er.argmax(axis=1), -1)
    block = lambda d: outside if d < 0 else t[:, d]
    return jnp.stack([
        jnp.concatenate([jnp.concatenate([block(int(d)) for d in row], axis=-1) for row in var], axis=-2)
        for var in d_of])


def _na_call(p_lat, p_ctx, bias, var_map, batch, s_len, lc):
    rows = s_len // GRID_W
    hp = NA_HEADS_PER_STEP
    tw = hp * LANES
    tq = NA_G * GRID_W
    win = NA_WR * GRID_W
    qb, kb, vb = (T_QNA * QTILE_W // tw, T_KNA * QTILE_W // tw, T_VNA * QTILE_W // tw)
    cb = CTX_TILE0 * QTILE_W // tw
    n_var = bias.shape[0]
    return pl.pallas_call(
        functools.partial(_na_kernel, rows=rows, var_map=tuple(int(v) for v in var_map)),
        grid=(batch, NA_HEADS // hp),
        in_specs=[pl.BlockSpec((s_len, tw), lambda b, h: (b, qb + h)),
                  pl.BlockSpec((s_len, tw), lambda b, h: (b, kb + h)),
                  pl.BlockSpec((s_len, tw), lambda b, h: (b, vb + h)),
                  pl.BlockSpec((lc, tw), lambda b, h: (b, kb + h - cb)),
                  pl.BlockSpec((lc, tw), lambda b, h: (b, vb + h - cb)),
                  pl.BlockSpec((n_var, hp, tq, win), lambda b, h: (0, h, 0, 0))],
        out_specs=pl.BlockSpec((s_len, tw), lambda b, h: (b, h)),
        out_shape=jax.ShapeDtypeStruct((batch * s_len, NA_HEADS * NA_HEAD_DIM), BF16),
        compiler_params=_params(("arbitrary", "arbitrary")),
        name="na_attention",
    )(p_lat, p_lat, p_lat, p_ctx, p_ctx, bias)


def _layer_norm(x, g, b):
    mu = jnp.mean(x, axis=-1, keepdims=True)
    xc = x - mu
    var = jnp.mean(xc * xc, axis=-1, keepdims=True)
    return xc * lax.rsqrt(var + LN_EPS) * g + b


def _outln_kernel(a1_ref, a2_ref, w_ref, h_ref, mod_ref, ln_ref, r_ref, h1_ref, u2_ref, lg_ref, wb_ref):
    half = a1_ref.shape[1]

    @pl.when(pl.program_id(0) == 0)
    def _():
        rows = wb_ref.shape[0] // 4
        for r0 in range(0, wb_ref.shape[0], rows):
            wb_ref[r0:r0 + rows, :] = w_ref[0, r0:r0 + rows, :].astype(wb_ref.dtype)

    sub = a1_ref.shape[0] // OUTLN_SUBTILES
    for r0 in range(0, a1_ref.shape[0], sub):
        rs = slice(r0, r0 + sub)
        y = _dot(a1_ref[rs, :], wb_ref[:half, :]) + _dot(a2_ref[rs, :], wb_ref[half:, :])
        x = DEEPNORM_ALPHA * h_ref[rs, :] + mod_ref[0, 0:1, :] * y
        hn = _layer_norm(x, ln_ref[0:1, :], ln_ref[1:2, :])
        h1_ref[rs, :] = hn
        u = hn * (1.0 + mod_ref[0, 2:3, :]) + mod_ref[0, 1:2, :]
        u_bf = u.astype(BF16)
        u2_ref[rs, :] = u_bf
        lg_ref[rs, :] = _dot(u_bf, r_ref[...])


def _outln_call(a1, a2, w, layer, h, mod, ln, router2, rows_per_group, tm=512):
    m, d = h.shape
    half = a1.shape[1]
    tpg = rows_per_group // tm
    return pl.pallas_call(
        _outln_kernel,
        grid=(m // tm,),
        in_specs=[pl.BlockSpec((tm, half), lambda i: (i, 0)),
                  pl.BlockSpec((tm, half), lambda i: (i, 0)),
                  pl.BlockSpec((1, 2 * half, d), lambda i: (layer, 0, 0), pipeline_mode=pl.Buffered(1)),
                  pl.BlockSpec((tm, d), lambda i: (i, 0)),
                  pl.BlockSpec((1, 3, d), lambda i: (i // tpg, 0, 0)),
                  pl.BlockSpec((2, d), lambda i: (0, 0)),
                  _resident_spec((d, LANES))],
        out_specs=[pl.BlockSpec((tm, d), lambda i: (i, 0)),
                   pl.BlockSpec((tm, d), lambda i: (i, 0)),
                   pl.BlockSpec((tm, LANES), lambda i: (i, 0))],
        out_shape=[jax.ShapeDtypeStruct((m, d), F32),
                   jax.ShapeDtypeStruct((m, d), BF16),
                   jax.ShapeDtypeStruct((m, LANES), F32)],
        scratch_shapes=[pltpu.VMEM((2 * half, d), BF16)],
        compiler_params=_params(("arbitrary",)),
        name="out_proj_postnorm",
    )(a1, a2, w, h, mod, ln, router2)


def _router_operand(router):
    d, e = router.shape
    return jnp.pad(router, ((0, 0), (0, LANES - e))).astype(BF16)


def _route_kernel(lg_ref, slot_ref, aff_ref, cnt_ref, tri_ref, *, cap):
    n_tok = lg_ref.shape[2]

    @pl.when(pl.program_id(0) == 0)
    def _():
        chunk = 256
        for r0 in range(0, n_tok, chunk):
            r = r0 + lax.broadcasted_iota(I32, (chunk, n_tok), 0)
            c = lax.broadcasted_iota(I32, (chunk, n_tok), 1)
            tri_ref[r0:r0 + chunk, :] = jnp.where(r < c, 1.0, 0.0).astype(BF16)

    lg = lg_ref[0]
    ex = jnp.exp(lg - jnp.max(lg, axis=0, keepdims=True))
    aff = ex / jnp.sum(ex, axis=0, keepdims=True)
    bits = lax.bitcast_convert_type(aff, I32)
    n_e = lg.shape[0]
    count = lambda mask: jnp.sum(jnp.where(mask, 1.0, 0.0), axis=1, keepdims=True)

    def body(_, lohi):
        lo, hi = lohi
        mid = lo + jnp.right_shift(hi - lo, 1)
        ge = count(bits >= mid) >= cap
        return jnp.where(ge, mid, lo), jnp.where(ge, hi, mid)

    lo0 = jnp.zeros((n_e, 1), I32)
    hi0 = jnp.full((n_e, 1), 0x7F800000, I32)
    thr, _ = lax.fori_loop(0, 31, body, (lo0, hi0))
    gt = bits > thr
    eq = bits == thr
    need = cap - count(gt)
    pre_eq = _dot(jnp.where(eq, 1.0, 0.0).astype(BF16), tri_ref[...])
    sel = jnp.logical_or(gt, jnp.logical_and(eq, pre_eq < need))
    sel_bf = jnp.where(sel, 1.0, 0.0).astype(BF16)
    slot = _dot(sel_bf, tri_ref[...])
    slot_ref[0] = jnp.where(sel, slot.astype(I32), -1)
    aff_ref[0] = aff
    r = lax.broadcasted_iota(I32, (n_tok, LANES), 0)
    c = lax.broadcasted_iota(I32, (n_tok, LANES), 1)
    before = jnp.where(r < c * MOE_CHUNK, 1.0, 0.0).astype(BF16)
    cnt_ref[0] = _dot(sel_bf, before).astype(I32)


def _route_call(lg_t, cap):
    b, e, n_tok = lg_t.shape
    spec = pl.BlockSpec((1, e, n_tok), lambda i: (i, 0, 0))
    return pl.pallas_call(
        functools.partial(_route_kernel, cap=cap),
        grid=(b,),
        in_specs=[spec],
        out_specs=[spec, spec, pl.BlockSpec((1, e, LANES), lambda i: (i, 0, 0))],
        out_shape=[jax.ShapeDtypeStruct((b, e, n_tok), I32), jax.ShapeDtypeStruct((b, e, n_tok), F32),
                   jax.ShapeDtypeStruct((b, e, LANES), I32)],
        scratch_shapes=[pltpu.VMEM((n_tok, n_tok), BF16)],
        compiler_params=_params(("arbitrary",)),
        name="ec_route",
    )(lg_t)


def _slot_windows(cnt, cap, n_tok):
    bounds = cnt[:, :, :n_tok // MOE_CHUNK + 1]
    start, end = bounds[:, :, :-1], bounds[:, :, 1:]
    starts, tier = [], jnp.full(start[:, 0].shape, len(MOE_WINS), I32)
    for i, win in reversed(list(enumerate(MOE_WINS))):
        a = jnp.minimum((start // 16) * 16, cap - win)
        tier = jnp.where(jnp.all(end <= a + win, axis=1), i, tier)
        starts.insert(0, jnp.swapaxes(a, 1, 2).reshape(-1))
    return jnp.concatenate(starts).astype(I32), tier.reshape(-1).astype(I32)


def _gather_kernel(win_ref, tier_ref, slot_ref, u_ref, o_ref, *, chunk):
    b, ks = pl.program_id(0), pl.program_id(1)
    n_e, cap = o_ref.shape[0], o_ref.shape[1]
    n_sub = u_ref.shape[0] // chunk
    step = b * pl.num_programs(1) + ks

    @pl.when(ks == 0)
    def _():
        o_ref[...] = jnp.zeros(o_ref.shape, o_ref.dtype)

    def place(rows, start_of, group, tok0, n_t):
        row = lax.broadcasted_iota(I32, (rows, n_t), 0)
        for g0 in range(0, n_e, group):
            starts = [start_of(e) for e in range(g0, g0 + group)]
            onehot = jnp.concatenate(
                [jnp.where(row + st == slot_ref[0, e:e + 1, tok0:tok0 + n_t], 1.0, 0.0).astype(BF16)
                 for st, e in zip(starts, range(g0, g0 + group))], axis=0)
            got = _dot(onehot, u_ref[tok0:tok0 + n_t, :]).astype(o_ref.dtype)
            for i, (st, e) in enumerate(zip(starts, range(g0, g0 + group))):
                o_ref[e, pl.ds(st, rows), :] += got[i * rows:(i + 1) * rows]

    n_chunks = pl.num_programs(0) * pl.num_programs(1) * n_sub
    for sub in range(n_sub):
        c = step * n_sub + sub
        for i, win in enumerate(MOE_WINS):
            @pl.when(tier_ref[c] == i)
            def _(c=c, sub=sub, i=i, win=win):
                base = (i * n_chunks + c) * n_e
                place(win, lambda e: pl.multiple_of(win_ref[base + e], 16), 512 // win, sub * chunk, chunk)

        @pl.when(tier_ref[c] == len(MOE_WINS))
        def _(sub=sub):
            place(cap, lambda e: 0, 2, sub * chunk, chunk)


def _gather_call(starts, tier, slot, u2, cap, chunks_per_step=2):
    b, e, n_tok = slot.shape
    d = u2.shape[1]
    chunk = MOE_CHUNK
    tt = chunk * chunks_per_step
    nc = n_tok // tt
    grid_spec = pltpu.PrefetchScalarGridSpec(
        num_scalar_prefetch=2,
        grid=(b, nc),
        in_specs=[pl.BlockSpec((1, e, tt), lambda i, k, w, f: (i, 0, k)),
                  pl.BlockSpec((tt, d), lambda i, k, w, f: (i * nc + k, 0))],
        out_specs=pl.BlockSpec((e, cap, d), lambda i, k, w, f: (0, i, 0)),
    )
    return pl.pallas_call(
        functools.partial(_gather_kernel, chunk=chunk),
        grid_spec=grid_spec,
        out_shape=jax.ShapeDtypeStruct((e, b * cap, d), BF16),
        compiler_params=_params(("arbitrary", "arbitrary")),
        name="moe_gather",
    )(starts, tier, slot, u2)


def _expert_hidden_kernel(x_ref, w1_ref, w3_ref, o_ref, *, ff):
    tf = w1_ref.shape[3]
    last = pl.num_programs(1) - 1

    def block(width):
        w = jnp.concatenate([w1_ref[0, 0, :, :width].astype(BF16), w3_ref[0, 0, :, :width].astype(BF16)], axis=1)
        h = _dot(x_ref[0], w)
        h1 = h[:, :width]
        o_ref[0, :, :width] = ((h1 / (1.0 + jnp.exp(-h1))) * h[:, width:]).astype(o_ref.dtype)

    tail = ff - (pl.cdiv(ff, tf) - 1) * tf
    if tail == tf:
        block(tf)
    else:
        pl.when(pl.program_id(1) < last)(lambda: block(tf))
        pl.when(pl.program_id(1) == last)(lambda: block(tail))


def _expert_out_kernel(hd_ref, w2_ref, o_ref):
    y = _dot(hd_ref[0], w2_ref[0, 0].astype(BF16)).astype(o_ref.dtype)
    cap = o_ref.shape[1]
    for b in range(o_ref.shape[0]):
        o_ref[b] = y[b * cap:(b + 1) * cap]


def _expert_call(xe, w1, w3, w2, layer, batch, tf=768, tn=2048):
    e, m, d = xe.shape
    cap = m // batch
    ff = w1.shape[3]
    hidden = pl.pallas_call(
        functools.partial(_expert_hidden_kernel, ff=ff),
        grid=(e, pl.cdiv(ff, tf)),
        in_specs=[pl.BlockSpec((1, m, d), lambda i, f: (i, 0, 0)),
                  pl.BlockSpec((1, 1, d, tf), lambda i, f: (layer, i, 0, f)),
                  pl.BlockSpec((1, 1, d, tf), lambda i, f: (layer, i, 0, f))],
        out_specs=pl.BlockSpec((1, m, tf), lambda i, f: (i, 0, f)),
        out_shape=jax.ShapeDtypeStruct((e, m, ff), BF16),
        compiler_params=_params(("arbitrary", "arbitrary")),
        name="moe_expert_hidden",
    )(xe, w1, w3)
    return pl.pallas_call(
        _expert_out_kernel,
        grid=(e, d // tn),
        in_specs=[pl.BlockSpec((1, m, ff), lambda i, j: (i, 0, 0)),
                  pl.BlockSpec((1, 1, ff, tn), lambda i, j: (layer, i, 0, j))],
        out_specs=pl.BlockSpec((batch, cap, tn), lambda i, j: (0, i, j)),
        out_shape=jax.ShapeDtypeStruct((batch, e * cap, d), BF16),
        compiler_params=_params(("arbitrary", "arbitrary")),
        name="moe_expert_out",
    )(hidden, w2)


def _scatter_kernel(win_ref, tier_ref, slot_ref, aff_ref, ye_ref, h_ref, mod_ref, ln_ref, *rest, with_next):
    if with_next:
        nmod_ref, h2_ref, un_ref, pt_ref, yew_ref, moe_ref = rest
    else:
        h2_ref, pt_ref, yew_ref, moe_ref = rest
    n_e = slot_ref.shape[1]
    cap = ye_ref.shape[1] // n_e
    tt = h_ref.shape[0]
    step = pl.program_id(0) * pl.num_programs(1) + pl.program_id(1)

    def gates(rows, e, start):
        row = lax.broadcasted_iota(I32, (rows, tt), 0)
        return jnp.where(row + start == slot_ref[0, e:e + 1, :], aff_ref[0, e:e + 1, :], 0.0).astype(BF16)

    n_chunks = pl.num_programs(0) * pl.num_programs(1)
    for i, win in enumerate(MOE_WINS):
        @pl.when(tier_ref[step] == i)
        def _(i=i, win=win):
            for e in range(n_e):
                start = pl.multiple_of(win_ref[(i * n_chunks + step) * n_e + e], 16)
                pt_ref[e * win:(e + 1) * win, :] = gates(win, e, start)
                yew_ref[e * win:(e + 1) * win, :] = ye_ref[0, pl.ds(e * cap + start, win), :]
            moe_ref[...] = _dot_tn(pt_ref[:n_e * win, :], yew_ref[:n_e * win, :])

    @pl.when(tier_ref[step] == len(MOE_WINS))
    def _():
        for e in range(n_e):
            pt_ref[e * cap:(e + 1) * cap, :] = gates(cap, e, 0)
        moe_ref[...] = _dot_tn(pt_ref[...], ye_ref[0])

    x = DEEPNORM_ALPHA * h_ref[...] + mod_ref[0] * moe_ref[...]
    hn = _layer_norm(x, ln_ref[0:1, :], ln_ref[1:2, :])
    h2_ref[...] = hn
    if with_next:
        un_ref[...] = (hn * (1.0 + nmod_ref[0, 1:2, :]) + nmod_ref[0, 0:1, :]).astype(un_ref.dtype)


def _scatter_call(starts, tier, slot, aff, ye, h1, gate, ln, next_ss):
    b, e, n_tok = slot.shape
    m, d = h1.shape
    tt = MOE_CHUNK
    nt = n_tok // tt
    rows = ye.shape[1]
    with_next = next_ss is not None
    tok_spec = pl.BlockSpec((1, e, tt), lambda i, t, w, f: (i, 0, t))
    row_spec = pl.BlockSpec((tt, d), lambda i, t, w, f: (i * nt + t, 0))
    in_specs = [tok_spec, tok_spec,
                pl.BlockSpec((1, rows, d), lambda i, t, w, f: (i, 0, 0), pipeline_mode=pl.Buffered(1)),
                row_spec,
                pl.BlockSpec((1, 1, d), lambda i, t, w, f: (i, 0, 0)),
                pl.BlockSpec((2, d), lambda i, t, w, f: (0, 0))]
    args = [slot, aff, ye, h1, gate, ln]
    out_specs = [row_spec]
    out_shape = [jax.ShapeDtypeStruct((m, d), F32)]
    if with_next:
        in_specs.append(pl.BlockSpec((1, 2, d), lambda i, t, w, f: (i, 0, 0)))
        args.append(next_ss)
        out_specs.append(row_spec)
        out_shape.append(jax.ShapeDtypeStruct((m, d), BF16))
    grid_spec = pltpu.PrefetchScalarGridSpec(
        num_scalar_prefetch=2,
        grid=(b, nt),
        in_specs=in_specs,
        out_specs=out_specs,
        scratch_shapes=[pltpu.VMEM((rows, tt), BF16),
                        pltpu.VMEM((e * max(MOE_WINS), d), BF16),
                        pltpu.VMEM((tt, d), F32)],
    )
    res = pl.pallas_call(
        functools.partial(_scatter_kernel, with_next=with_next),
        grid_spec=grid_spec,
        out_shape=out_shape,
        compiler_params=_params(("arbitrary", "arbitrary")),
        name="moe_combine_postnorm",
    )(starts, tier, *args)
    return res if with_next else (res[0], None)


def _ec_moe(h1, u2, logits, w1, w3, w2, layer, gate, ln, next_ss, batch, n_tok):
    cap = EC_CAPACITY_FACTOR * n_tok // N_EXPERTS
    lg_t = jnp.swapaxes(logits[:, :N_EXPERTS].reshape(batch, n_tok, N_EXPERTS), 1, 2)
    slot, aff, cnt = _route_call(lg_t, cap)
    starts, tier = _slot_windows(cnt, cap, n_tok)
    xe = _gather_call(starts, tier, slot, u2, cap)
    ye = _expert_call(xe, w1, w3, w2, layer, batch)
    return _scatter_call(starts, tier, slot, aff, ye, h1, gate, ln, next_ss)


def _mm_kernel(a_ref, w_ref, o_ref, wb_ref):
    @pl.when(pl.program_id(1) == 0)
    def _():
        wb_ref[...] = w_ref[0].astype(wb_ref.dtype)

    o_ref[...] = _dot(a_ref[...], wb_ref[...]).astype(o_ref.dtype)


def _mm_call(a, w, layer, out_dtype, tm=1024, tn=1024):
    m, k = a.shape
    n = w.shape[2]
    return pl.pallas_call(
        _mm_kernel,
        grid=(n // tn, m // tm),
        in_specs=[pl.BlockSpec((tm, k), lambda j, i: (i, 0)),
                  pl.BlockSpec((1, k, tn), lambda j, i: (layer, 0, j))],
        out_specs=pl.BlockSpec((tm, tn), lambda j, i: (i, j)),
        out_shape=jax.ShapeDtypeStruct((m, n), out_dtype),
        scratch_shapes=[pltpu.VMEM((k, tn), BF16)],
        compiler_params=_params(("arbitrary", "arbitrary")),
        name="matmul",
    )(a, w)


def _short_conv(p, w, b):
    x = p.astype(F32)
    n = x.shape[0]
    r = lax.broadcasted_iota(I32, x.shape, 0)
    prev = jnp.where(r == 0, 0.0, pltpu.roll(x, 1, 0))
    nxt = jnp.where(r == n - 1, 0.0, pltpu.roll(x, n - 1, 0))
    return b + w[0:1, :] * prev + w[1:2, :] * x + w[2:3, :] * nxt


def _sconv_kernel(p_ref, w_ref, b_ref, o_ref):
    o_ref[...] = _short_conv(p_ref[...], w_ref[...], b_ref[...]).astype(o_ref.dtype)


def _sconv_call(p, conv_w, conv_b, col0, ncols, out_dtype, batch, n_tok, tc=512):
    c0 = col0 // tc
    return pl.pallas_call(
        _sconv_kernel,
        grid=(batch, ncols // tc),
        in_specs=[pl.BlockSpec((n_tok, tc), lambda b, j: (b, c0 + j)),
                  pl.BlockSpec((HY_SHORT, tc), lambda b, j: (0, c0 + j)),
                  pl.BlockSpec((1, tc), lambda b, j: (0, c0 + j))],
        out_specs=pl.BlockSpec((n_tok, tc), lambda b, j: (b, j)),
        out_shape=jax.ShapeDtypeStruct((batch * n_tok, ncols), out_dtype),
        compiler_params=_params(("arbitrary", "arbitrary")),
        name="hyena_short_conv",
    )(p, conv_w, conv_b.reshape(1, -1))


def _dft_fwd_kernel(fc_ref, fs_ref, r1_ref, r2_ref, *rest, spectral):
    o_ref = rest[-1]
    sub = fc_ref.shape[0] // SEQMIX_SUBTILES
    for r0 in range(0, fc_ref.shape[0], sub):
        rs = slice(r0, r0 + sub)
        xr = _dot(fc_ref[rs, :], r1_ref[...])
        xi = _dot(fs_ref[rs, :], r2_ref[...])
        if spectral:
            h_ref, nyq_ref = rest[:2]
            hr = h_ref[0, 0, rs, :]
            hi = h_ref[0, 1, rs, :]
            if r0 == 0:
                dc = lax.broadcasted_iota(I32, xr.shape, 0) == 0
                yr = xr * hr - jnp.where(dc, 0.0, xi * hi)
                yi = jnp.where(dc, xi * nyq_ref[...], xr * hi + xi * hr)
            else:
                yr = xr * hr - xi * hi
                yi = xr * hi + xi * hr
        else:
            yr, yi = xr, xi
        o_ref[0, 0, rs, :] = yr.astype(o_ref.dtype)
        o_ref[0, 1, rs, :] = yi.astype(o_ref.dtype)


def _resident_spec(shape):
    return pl.BlockSpec(shape, lambda *_: (0,) * len(shape), pipeline_mode=pl.Buffered(1))


def _dft_fwd_call(fc, fs, r1, r2, spec, out_dtype, batch, n_tok, tn=256):
    ncols = r1.shape[1]
    r_spec = pl.BlockSpec((n_tok, tn), lambda j, b: (b, j))
    in_specs = [_resident_spec(fc.shape), _resident_spec(fs.shape), r_spec, r_spec]
    args = [fc, fs, r1, r2]
    if spec is not None:
        h, nyq, hcol0 = spec
        c0 = hcol0 // tn
        in_specs += [pl.BlockSpec((1, 2, n_tok, tn), lambda j, b: (0, 0, 0, c0 + j)),
                     pl.BlockSpec((1, tn), lambda j, b: (0, c0 + j))]
        args += [h, nyq]
    return pl.pallas_call(
        functools.partial(_dft_fwd_kernel, spectral=spec is not None),
        grid=(ncols // tn, batch),
        in_specs=in_specs,
        out_specs=pl.BlockSpec((1, 2, n_tok, tn), lambda j, b: (b, 0, 0, j)),
        out_shape=jax.ShapeDtypeStruct((batch, 2, n_tok, ncols), out_dtype),
        compiler_params=_params(("arbitrary", "arbitrary")),
        name="dft_forward",
    )(*args)


def _dual_kernel(gc_ref, gs_ref, y_ref, p_ref, cw_ref, cb_ref, z_ref, skip_ref, o_ref):
    yr, yi = y_ref[0, 0], y_ref[0, 1]
    gate = _short_conv(p_ref[...], cw_ref[...], cb_ref[...])
    sub = o_ref.shape[0] // SEQMIX_SUBTILES
    for r0 in range(0, o_ref.shape[0], sub):
        rs = slice(r0, r0 + sub)
        y = _dot(gc_ref[rs, :], yr) + _dot(gs_ref[rs, :], yi)
        o_ref[rs, :] = (gate[rs] * (y + skip_ref[...] * z_ref[rs, :].astype(F32))).astype(o_ref.dtype)


def _dual_call(gc, gs, y_spec, p, conv_w, conv_b, pcol0, z, skip, batch, n_tok, ncols, tn=256):
    c0 = pcol0 // tn
    o_spec = pl.BlockSpec((n_tok, tn), lambda b, j: (b, j))
    return pl.pallas_call(
        _dual_kernel,
        grid=(batch, ncols // tn),
        in_specs=[_resident_spec(gc.shape), _resident_spec(gs.shape),
                  pl.BlockSpec((1, 2, n_tok, tn), lambda b, j: (b, 0, 0, j)),
                  pl.BlockSpec((n_tok, tn), lambda b, j: (b, c0 + j)),
                  pl.BlockSpec((HY_SHORT, tn), lambda b, j: (0, c0 + j)),
                  pl.BlockSpec((1, tn), lambda b, j: (0, c0 + j)),
                  o_spec,
                  pl.BlockSpec((1, tn), lambda b, j: (0, j))],
        out_specs=o_spec,
        out_shape=jax.ShapeDtypeStruct((batch * n_tok, ncols), BF16),
        compiler_params=_params(("arbitrary", "arbitrary")),
        name="seq_mix_matmul",
    )(gc, gs, y_spec, p, conv_w, conv_b.reshape(1, -1), z, skip)


def _fnet_kernel(cl_ref, msl_ref, x_ref, cw_ref, o_ref):
    gw = cw_ref.shape[1]
    for c0 in range(0, x_ref.shape[1], gw):
        x = x_ref[:, c0:c0 + gw]
        seq = jnp.concatenate([_dot(cl_ref[...], x).astype(BF16), _dot(msl_ref[...], x).astype(BF16)], axis=1)
        o_ref[:, c0:c0 + gw] = _dot(seq, cw_ref[...]).astype(o_ref.dtype)


def _fnet_call(cl, msl, p, cw, col0, batch, n_tok, groups_per_step=2):
    gw = cw.shape[1]
    tw = gw * groups_per_step
    c0 = col0 // tw
    return pl.pallas_call(
        _fnet_kernel,
        grid=(batch, FN_GROUPS // groups_per_step),
        in_specs=[_resident_spec(cl.shape), _resident_spec(msl.shape),
                  pl.BlockSpec((n_tok, tw), lambda b, g: (b, c0 + g)),
                  pl.BlockSpec(cw.shape, lambda b, g: (0, 0))],
        out_specs=pl.BlockSpec((n_tok, tw), lambda b, g: (b, g)),
        out_shape=jax.ShapeDtypeStruct((batch * n_tok, FN_GROUPS * gw), BF16),
        compiler_params=_params(("arbitrary", "arbitrary")),
        name="fnet_mix",
    )(cl, msl, p, cw)


def _cos_sin_matrix(n_rows, n_cols, period, split=64):
    r = np.arange(n_rows, dtype=np.int64)[:, None]
    c_hi = (np.arange(n_cols // split, dtype=np.int64) * split)[None, :]
    c_lo = np.arange(split, dtype=np.int64)[None, :]
    ang = lambda c: 2.0 * np.pi * ((r * c) % period).astype(np.float64) / period
    ca, sa = jnp.asarray(np.cos(ang(c_hi)), F32), jnp.asarray(np.sin(ang(c_hi)), F32)
    cb, sb = jnp.asarray(np.cos(ang(c_lo)), F32), jnp.asarray(np.sin(ang(c_lo)), F32)
    cos = ca[:, :, None] * cb[:, None, :] - sa[:, :, None] * sb[:, None, :]
    sin = sa[:, :, None] * cb[:, None, :] + ca[:, :, None] * sb[:, None, :]
    return cos.reshape(n_rows, n_cols), sin.reshape(n_rows, n_cols)


def _trig_kernel(ta_ref, tb_ref, ea_ref, eb_ref, *o_refs, mode, period, scale):
    sa_ca = _dot(ta_ref[...], ea_ref[...])
    sb_cb = _dot(tb_ref[...], eb_ref[...])
    nc = sa_ca.shape[1] // 2
    ca, sa, cb, sb = sa_ca[:, :nc], sa_ca[:, nc:], sb_cb[:, :nc], sb_cb[:, nc:]
    cos = ca * cb - sa * sb
    sin = sa * cb + ca * sb
    tm = cos.shape[0]
    row = pl.program_id(0) * tm + lax.broadcasted_iota(I32, cos.shape, 0)
    col = lax.broadcasted_iota(I32, cos.shape, 1)
    alt = lambda idx: (1 - 2 * jnp.bitwise_and(idx, 1)).astype(F32)
    if mode == "hyena":
        fc_ref, fs_ref, gc_ref, gs_ref = o_refs
        fc_ref[...] = cos.astype(fc_ref.dtype)
        fs_ref[...] = jnp.where(row == 0, alt(col), -sin).astype(fs_ref.dtype)
        gc_ref[...] = (cos * jnp.where(col == 0, 1.0 / period, 2.0 / period)).astype(gc_ref.dtype)
        gs_ref[...] = jnp.where(col == 0, alt(row) * (1.0 / period), sin * (-2.0 / period)).astype(gs_ref.dtype)
    else:
        c_ref, ms_ref = o_refs
        c_ref[...] = (cos * scale).astype(c_ref.dtype)
        ms_ref[...] = (sin * (-scale)).astype(ms_ref.dtype)


def _split3(x):
    bf = jnp.bfloat16
    x = np.asarray(x, np.float32)
    hi = x.astype(bf)
    r1 = x - hi.astype(np.float32)
    mid = r1.astype(bf)
    lo = (r1 - mid.astype(np.float32)).astype(bf)
    return [hi, mid, lo]


def _trig_call(n, period, mode, scale=1.0, split=64, tm=256):
    r = np.arange(n, dtype=np.int64)[:, None]
    c_hi = (np.arange(n // split, dtype=np.int64) * split)[None, :]
    c_lo = np.arange(split, dtype=np.int64)[None, :]
    ang = lambda c: 2.0 * np.pi * ((r * c) % period).astype(np.float64) / period
    def operands(a, col_group):
        terms = _split3(np.cos(a)) + _split3(np.sin(a))
        k = a.shape[1]
        spread = (col_group[None, :] == np.arange(k)[:, None]).astype(np.float32)
        zero = np.zeros_like(spread)
        e = np.concatenate([np.concatenate([spread, zero], 1)] * 3 + [np.concatenate([zero, spread], 1)] * 3, 0)
        return jnp.asarray(np.concatenate(terms, axis=1)), jnp.asarray(e, BF16)
    cols = np.arange(n)
    ta, ea = operands(ang(c_hi), cols // split)
    tb, eb = operands(ang(c_lo), cols % split)
    n_out = 4 if mode == "hyena" else 2
    o_spec = pl.BlockSpec((tm, n), lambda i: (i, 0))
    return pl.pallas_call(
        functools.partial(_trig_kernel, mode=mode, period=period, scale=scale),
        grid=(n // tm,),
        in_specs=[pl.BlockSpec((tm, ta.shape[1]), lambda i: (i, 0)),
                  pl.BlockSpec((tm, tb.shape[1]), lambda i: (i, 0)),
                  pl.BlockSpec(ea.shape, lambda i: (0, 0)),
                  pl.BlockSpec(eb.shape, lambda i: (0, 0))],
        out_specs=[o_spec] * n_out,
        out_shape=[jax.ShapeDtypeStruct((n, n), BF16)] * n_out,
        compiler_params=_params(("arbitrary",)),
        name="dft_tables",
    )(ta, tb, ea, eb)


def _hyena_dft_operands(n_tok):
    return _trig_call(n_tok, 2 * n_tok, "hyena")


def _fnet_operands(n_tok, gw):
    cl, msl = _trig_call(n_tok, n_tok, "fnet", scale=1.0 / math.sqrt(n_tok * gw))
    cw, sw = _cos_sin_matrix(gw, gw, gw)
    return cl, msl, jnp.concatenate([cw, sw], axis=0).astype(BF16)


def _filter_kernel(z_ref, w1_ref, b1_ref, f1_ref, w2_ref, b2_ref, f2_ref, w3f_ref, w3b_ref, dl_ref,
                   sum_ref, dif_ref, nyq_ref, hdn_ref):
    @pl.when(jnp.logical_and(pl.program_id(0) == 0, pl.program_id(1) == 0))
    def _():
        hdot = lambda a, b: jnp.dot(a, b, precision=lax.Precision.HIGHEST, preferred_element_type=F32)
        hdn = jnp.sin(f1_ref[...] * (hdot(z_ref[...], w1_ref[...]) + b1_ref[...]))
        hdn = jnp.sin(f2_ref[...] * (hdot(hdn, w2_ref[...]) + b2_ref[...]))
        hdn_ref[...] = hdn.astype(hdn_ref.dtype)

    n = hdn_ref.shape[0]
    row = lax.broadcasted_iota(I32, (n, dl_ref.shape[1]), 0)
    decay = jnp.exp(-(row.astype(F32) / (n - 1)) * dl_ref[...])
    hf = _dot(hdn_ref[...], w3f_ref[...].astype(BF16)) * decay
    hb = jnp.where(row == 0, 0.0, _dot(hdn_ref[...], w3b_ref[...].astype(BF16)) * decay)
    tot = hf + hb
    sum_ref[...] = tot.astype(sum_ref.dtype)
    dif_ref[...] = (hf - hb).astype(dif_ref.dtype)
    nyq_ref[...] = jnp.sum(jnp.where(jnp.bitwise_and(row, 1) == 0, tot, -tot), axis=0, keepdims=True)


def _filter_call(n_tok, fw1, fb1, ff1, fw2, fb2, ff2, fw3, tc=512):
    t01 = np.linspace(0.0, 1.0, n_tok, dtype=np.float32)
    w = (2.0 * math.pi * np.arange(n_tok, dtype=np.float32) / n_tok).astype(np.float32)
    bands = np.linspace(1e-4, HY_BANDS - 1, HY_BANDS, dtype=np.float32)
    z = np.concatenate([t01[:, None], np.cos(w[:, None] * bands), -np.sin(w[:, None] * bands)], -1)
    deltas = np.abs(np.linspace(HY_MIN_DECAY, HY_MAX_DECAY, HY_W, dtype=np.float32))[None, :]
    emb, hid = fw1.shape
    nc = HY_W // tc
    full = lambda shape: pl.BlockSpec(shape, lambda o, j: (0,) * len(shape))
    o_spec = pl.BlockSpec((n_tok, tc), lambda o, j: (0, o * nc + j))
    return pl.pallas_call(
        _filter_kernel,
        grid=(HY_ORDER, nc),
        in_specs=[full((n_tok, emb)), full((emb, hid)), full((1, hid)), full((1, hid)),
                  full((hid, hid)), full((1, hid)), full((1, hid)),
                  pl.BlockSpec((hid, tc), lambda o, j: (0, (2 * o) * nc + j)),
                  pl.BlockSpec((hid, tc), lambda o, j: (0, (2 * o + 1) * nc + j)),
                  pl.BlockSpec((1, tc), lambda o, j: (0, j))],
        out_specs=[o_spec, o_spec, pl.BlockSpec((1, tc), lambda o, j: (0, o * nc + j))],
        out_shape=[jax.ShapeDtypeStruct((n_tok, HY_ORDER * HY_W), BF16),
                   jax.ShapeDtypeStruct((n_tok, HY_ORDER * HY_W), BF16),
                   jax.ShapeDtypeStruct((1, HY_ORDER * HY_W), F32)],
        scratch_shapes=[pltpu.VMEM((n_tok, hid), BF16)],
        compiler_params=_params(("arbitrary", "arbitrary")),
        name="hyena_filters",
    )(jnp.asarray(z.astype(np.float32)), fw1, fb1.reshape(1, hid), ff1.reshape(1, hid),
      fw2, fb2.reshape(1, hid), ff2.reshape(1, hid), fw3, fw3, jnp.asarray(deltas))


def kernel(x, c, ctx, c_ctx, ada_w, ada_b, ln1_g, ln1_b, ln2_g, ln2_b, router, exp_w1, exp_w3, exp_w2,
           ab_w_in, ab_kv_norm, ab_w_ukv, ab_rpb, ab_w_out,
           cd_w_in, cd_conv_w, cd_conv_b, cd_filt_w1, cd_filt_b1, cd_filt_freq1, cd_filt_w2, cd_filt_b2,
           cd_filt_freq2, cd_filt_w3, cd_skip, cd_w_out):
    batch, n_tok, d = x.shape
    lc = ctx.shape[1]
    x2d = x.reshape(batch * n_tok, d)
    ctx2d = ctx.reshape(batch * lc, d)

    pad_rows = (-(batch + 1)) % 8
    cv = jnp.concatenate([c, c_ctx[None, :], jnp.zeros((pad_rows, d), F32)], axis=0)
    ada = _ada_call(cv, ada_w, ada_b)
    mods = [ada[i, :batch].reshape(batch, 6, d) for i in range(DEPTH)]
    ctx_ss = ada[0, batch, :2 * d].reshape(1, 2, d)

    w_in = _w_in_prep_call(jnp.swapaxes(ab_w_in, 1, 2), 0)
    p_lat = _mm_rope_call(x2d, mods[0][:, 0:2], n_tok, w_in, _rope_tables(n_tok), tm=1024)
    p_ctx = _mm_rope_call(ctx2d, ctx_ss, batch * lc, w_in, _identity_rope_tables(batch * lc), tm=batch * lc,
                          tile0=CTX_TILE0)
    w_ukv = _permute_w_ukv(ab_w_ukv[0])
    kv_lat = _kvup_call(p_lat, ab_kv_norm[0], w_ukv, tm=1024)
    kv_ctx = _kvup_call(p_ctx, ab_kv_norm[0], w_ukv, tm=lc, tile0=CTX_TILE0)
    a_out = _mla_call(p_lat, kv_lat, p_ctx, kv_ctx, batch, n_tok, lc)
    var_map, na_er, na_ec = _na_tables(n_tok // GRID_W)
    b_out = _na_call(p_lat, p_ctx, _na_bias(ab_rpb[0], na_er, na_ec), var_map, batch, n_tok, lc)
    ln1 = jnp.stack([ln1_g, ln1_b], axis=1)
    ln2 = jnp.stack([ln2_g, ln2_b], axis=1)
    h1, u2, logits = _outln_call(a_out, b_out, ab_w_out, 0, x2d,
                                 mods[0][:, 2:5], ln1[0], _router_operand(router[0]), n_tok)
    h, u = _ec_moe(h1, u2, logits, exp_w1, exp_w3, exp_w2, 0, mods[0][:, 5:6], ln2[0],
                   mods[1][:, 0:2], batch, n_tok)

    p1 = _mm_call(u, cd_w_in, 0, BF16)
    s0 = _sconv_call(p1, cd_conv_w[0], cd_conv_b[0], 0, HY_W, BF16, batch, n_tok)
    h_sum, h_dif, h_nyq = _filter_call(n_tok, cd_filt_w1[0], cd_filt_b1[0], cd_filt_freq1[0], cd_filt_w2[0],
                                       cd_filt_b2[0], cd_filt_freq2[0], cd_filt_w3[0])
    fc, fs, gc, gs = _hyena_dft_operands(n_tok)
    h_spec = _dft_fwd_call(fc, fs, h_sum, h_dif, None, F32, 1, n_tok)
    z = s0
    for o in range(HY_ORDER):
        y_spec = _dft_fwd_call(fc, fs, z, z, (h_spec, h_nyq, o * HY_W), BF16, batch, n_tok)
        z = _dual_call(gc, gs, y_spec, p1, cd_conv_w[0], cd_conv_b[0], (o + 1) * HY_W, z, cd_skip[0][o:o + 1],
                       batch, n_tok, HY_W)
    cl, msl, cw = _fnet_operands(n_tok, FN_GROUP_W)
    y_fn = _fnet_call(cl, msl, p1, cw, HY_IN_W, batch, n_tok)
    h1, u2, logits = _outln_call(z, y_fn, cd_w_out, 0, h,
                                 mods[1][:, 2:5], ln1[1], _router_operand(router[1]), n_tok)
    h, _ = _ec_moe(h1, u2, logits, exp_w1, exp_w3, exp_w2, 1, mods[1][:, 5:6], ln2[1],
                   None, batch, n_tok)
    return h.reshape(batch, n_tok, d)
```

```python
import functools
import math

import numpy as np
import jax
import jax.numpy as jnp
from jax import lax
from jax.experimental import pallas as pl
from jax.experimental.pallas import tpu as pltpu

F32 = jnp.float32
BF16 = jnp.bfloat16
I32 = jnp.int32

DEPTH = 2
GRID_W = 64
DEEPNORM_ALPHA = (2.0 * DEPTH) ** 0.25
LN_EPS = 1e-6
NEG_INF = -1e30
LOG2E = math.log2(math.e)

MLA_HEADS = 8
MLA_NOPE = 128
MLA_ROPE = 64
MLA_QK = MLA_NOPE + MLA_ROPE
MLA_V = 128
MLA_KV_RANK = 512
ROPE_THETA = 10000.0

NA_HEADS = 8
NA_HEAD_DIM = 128
NA_KH = 8
NA_KW = 16

OFF_CKV = MLA_HEADS * MLA_QK
OFF_KPE = OFF_CKV + MLA_KV_RANK
OFF_QNA = OFF_KPE + MLA_ROPE

HY_W = 1024
HY_ORDER = 2
HY_IN_W = (HY_ORDER + 1) * HY_W
HY_SHORT = 3
HY_BANDS = 16
HY_DECAY_TARGET = 1e-2
HY_MIN_DECAY = math.log(HY_DECAY_TARGET) / 1.5
HY_MAX_DECAY = math.log(HY_DECAY_TARGET) / 0.3

FN_W = 1024
FN_GROUPS = 4
FN_GROUP_W = FN_W // FN_GROUPS

N_EXPERTS = 16
EC_CAPACITY_FACTOR = 2

LANES = 128
VMEM_LIMIT = 56 * 1024 * 1024

QTILE_W = 2 * LANES
T_KPE = MLA_HEADS
T_CKV = T_KPE + 2
T_QNA = T_CKV + MLA_KV_RANK // QTILE_W
T_KNA = T_QNA + NA_HEADS * NA_HEAD_DIM // QTILE_W
T_VNA = T_KNA + NA_HEADS * NA_HEAD_DIM // QTILE_W
N_ABTILES = T_VNA + NA_HEADS * NA_HEAD_DIM // QTILE_W
AB_PERM_W = N_ABTILES * QTILE_W
CTX_TILE0 = T_KPE
INPROJ_TILES_PER_STEP = 8

MLA_SUBTILES = 4
OUTLN_SUBTILES = 2
SEQMIX_SUBTILES = 4
MOE_CHUNK = 256
MOE_WINS = (64, 128)
NA_G = 4
NA_WR = NA_KH + NA_G - 1
NA_HEADS_PER_STEP = 2


def _params(sem, vmem=VMEM_LIMIT):
    return pltpu.CompilerParams(dimension_semantics=sem, vmem_limit_bytes=vmem)


def _dot(a, b):
    return jnp.dot(a, b, preferred_element_type=F32)


def _dot_nt(a, b):
    return lax.dot_general(a, b, (((1,), (1,)), ((), ())), preferred_element_type=F32)


def _dot_tn(a, b):
    return lax.dot_general(a, b, (((0,), (0,)), ((), ())), preferred_element_type=F32)


def _ada_kernel(c_ref, w_ref, b_ref, o_ref):
    c = c_ref[...]
    a = c / (1.0 + jnp.exp(-c))
    o_ref[0] = _dot(a.astype(BF16), w_ref[0].astype(BF16)) + b_ref[0]


def _ada_call(cv, ada_w, ada_b, tn=2048):
    depth, d, n = ada_w.shape
    rows = cv.shape[0]
    return pl.pallas_call(
        _ada_kernel,
        grid=(depth, n // tn),
        in_specs=[pl.BlockSpec((rows, d), lambda l, j: (0, 0)),
                  pl.BlockSpec((1, d, tn), lambda l, j: (l, 0, j)),
                  pl.BlockSpec((1, 1, tn), lambda l, j: (l, 0, j))],
        out_specs=pl.BlockSpec((1, rows, tn), lambda l, j: (l, 0, j)),
        out_shape=jax.ShapeDtypeStruct((depth, rows, n), F32),
        compiler_params=_params(("arbitrary", "arbitrary")),
        name="ada_params",
    )(cv, ada_w, ada_b.reshape(depth, 1, n))


def _mm_rope_kernel(x_ref, ss_ref, w_ref, ta_ref, tb_ref, tc_ref, o_ref, a_ref, *, tile0):
    @pl.when(pl.program_id(1) == 0)
    def _():
        a_ref[...] = (x_ref[...] * (1.0 + ss_ref[0, 1:2, :]) + ss_ref[0, 0:1, :]).astype(a_ref.dtype)

    a = a_ref[...]
    for q in range(INPROJ_TILES_PER_STEP):
        t = tile0 + INPROJ_TILES_PER_STEP * pl.program_id(1) + q
        c0 = q * QTILE_W
        acc = _dot_nt(a, w_ref[c0:c0 + QTILE_W, :])
        is_rope = t <= T_KPE
        f = jnp.where(t < T_KPE, LOG2E * MLA_QK ** -0.5,
                      jnp.where(jnp.logical_and(t >= T_QNA, t < T_KNA), LOG2E * NA_HEAD_DIM ** -0.5, 1.0)
                      ).astype(F32)
        hi = acc[:, LANES:]
        rot = (hi * jnp.where(is_rope, ta_ref[...], 1.0)
               + pltpu.roll(hi, LANES - MLA_ROPE // 2, 1) * jnp.where(is_rope, tb_ref[...], 0.0)
               + pltpu.roll(hi, MLA_ROPE // 2, 1) * jnp.where(is_rope, tc_ref[...], 0.0))
        o_ref[:, c0:c0 + LANES] = (acc[:, :LANES] * f).astype(o_ref.dtype)
        o_ref[:, c0 + LANES:c0 + QTILE_W] = (rot * f).astype(o_ref.dtype)


def _mm_rope_call(x, ss, rows_per_group, w, tabs, tm, tile0=0):
    m, k = x.shape
    tn = INPROJ_TILES_PER_STEP * QTILE_W
    n = w.shape[0] - tile0 * QTILE_W
    j0 = tile0 // INPROJ_TILES_PER_STEP
    ntab = tabs[0].shape[0] // tm
    tpg = rows_per_group // tm
    tab_spec = pl.BlockSpec((tm, LANES), lambda i, j: (i % ntab, 0))
    return pl.pallas_call(
        functools.partial(_mm_rope_kernel, tile0=tile0),
        grid=(m // tm, n // tn),
        in_specs=[pl.BlockSpec((tm, k), lambda i, j: (i, 0)),
                  pl.BlockSpec((1, 2, k), lambda i, j: (i // tpg, 0, 0)),
                  pl.BlockSpec((tn, k), lambda i, j: (j0 + j, 0)),
                  tab_spec, tab_spec, tab_spec],
        out_specs=pl.BlockSpec((tm, tn), lambda i, j: (i, j)),
        out_shape=jax.ShapeDtypeStruct((m, n), BF16),
        scratch_shapes=[pltpu.VMEM((tm, k), BF16)],
        compiler_params=_params(("arbitrary", "arbitrary")),
        name="attn_in_proj",
    )(x, ss, w, *tabs)


def _rope_tables(n_tok):
    t = np.arange(n_tok)
    row = (t // GRID_W).astype(np.float32)
    col = (t % GRID_W).astype(np.float32)
    n_freq = MLA_ROPE // 4
    inv = (ROPE_THETA ** (-np.arange(n_freq, dtype=np.float32) / n_freq)).astype(np.float32)
    ang = np.concatenate([row[:, None] * inv, col[:, None] * inv], axis=1)
    cos, sin = np.cos(ang).astype(np.float32), np.sin(ang).astype(np.float32)
    half = MLA_ROPE // 2
    ta = np.zeros((n_tok, LANES), np.float32)
    tb = np.zeros((n_tok, LANES), np.float32)
    tc = np.zeros((n_tok, LANES), np.float32)
    ta[:, :half] = cos
    ta[:, half:2 * half] = cos
    tb[:, :half] = -sin
    tc[:, half:2 * half] = sin
    return jnp.asarray(ta), jnp.asarray(tb), jnp.asarray(tc)


def _identity_rope_tables(n_tok):
    ta = np.zeros((n_tok, LANES), np.float32)
    ta[:, :MLA_ROPE] = 1.0
    z = np.zeros((n_tok, LANES), np.float32)
    return jnp.asarray(ta), jnp.asarray(z), jnp.asarray(z)


def _w_in_prep_kernel(w_ref, o_ref):
    x = w_ref[0]
    zeros = lambda n: jnp.zeros((n, x.shape[1]), x.dtype)
    rows = []
    for h in range(MLA_HEADS):
        rows += [x[h * MLA_QK:(h + 1) * MLA_QK, :], zeros(QTILE_W - MLA_QK)]
    rows += [zeros(LANES), x[OFF_KPE:OFF_QNA, :], zeros(LANES - MLA_ROPE), zeros(QTILE_W),
             x[OFF_CKV:OFF_KPE, :], x[OFF_QNA:, :]]
    o_ref[...] = jnp.concatenate(rows, axis=0).astype(o_ref.dtype)


def _w_in_prep_call(w_in_t, layer, tk=256):
    _, n, d = w_in_t.shape
    return pl.pallas_call(
        _w_in_prep_kernel,
        grid=(d // tk,),
        in_specs=[pl.BlockSpec((1, n, tk), lambda i: (layer, 0, i))],
        out_specs=pl.BlockSpec((AB_PERM_W, tk), lambda i: (0, i)),
        out_shape=jax.ShapeDtypeStruct((AB_PERM_W, d), BF16),
        compiler_params=_params(("arbitrary",)),
        name="attn_w_in_prep",
    )(w_in_t)


def _kvup_kernel(x_ref, g_ref, w_ref, o_ref):
    x = x_ref[...].astype(F32)
    y = x * lax.rsqrt(jnp.mean(x * x, axis=-1, keepdims=True) + LN_EPS) * g_ref[...]
    o_ref[...] = _dot(y.astype(BF16), w_ref[...]).astype(o_ref.dtype)


def _kvup_call(p, g, w, tm, tile0=0):
    m = p.shape[0]
    r, n = w.shape
    return pl.pallas_call(
        _kvup_kernel,
        grid=(m // tm,),
        in_specs=[pl.BlockSpec((tm, r), lambda i: (i, (T_CKV - tile0) * QTILE_W // r)),
                  pl.BlockSpec((1, r), lambda i: (0, 0)),
                  pl.BlockSpec((r, n), lambda i: (0, 0))],
        out_specs=pl.BlockSpec((tm, n), lambda i: (i, 0)),
        out_shape=jax.ShapeDtypeStruct((m, n), BF16),
        compiler_params=_params(("arbitrary",)),
        name="kv_up",
    )(p, g.reshape(1, r), w)


def _permute_w_ukv(w):
    r = w.shape[0]
    w3 = w.reshape(r, MLA_HEADS, MLA_NOPE + MLA_V)
    return jnp.concatenate([w3[:, :, :MLA_NOPE].reshape(r, -1), w3[:, :, MLA_NOPE:].reshape(r, -1)],
                           axis=1).astype(BF16)


def _mla_kernel(q_ref, kn_ref, kp_ref, v_ref, kcn_ref, kcp_ref, vc_ref, o_ref, kcat, vcat):
    s_len = kn_ref.shape[0]
    n_h = kcat.shape[0]
    for h in range(n_h):
        kcat[h, :s_len, :LANES] = kn_ref[:, h * LANES:(h + 1) * LANES]
        kcat[h, :s_len, LANES:] = kp_ref[...]
        kcat[h, s_len:, :LANES] = kcn_ref[:, h * LANES:(h + 1) * LANES]
        kcat[h, s_len:, LANES:] = kcp_ref[...]
        vcat[h, :s_len, :] = v_ref[:, h * LANES:(h + 1) * LANES]
        vcat[h, s_len:, :] = vc_ref[:, h * LANES:(h + 1) * LANES]

    sub = q_ref.shape[0] // MLA_SUBTILES
    for h in range(n_h):
        for r0 in range(0, q_ref.shape[0], sub):
            s = _dot_nt(q_ref[r0:r0 + sub, h * QTILE_W:(h + 1) * QTILE_W], kcat[h])
            m = jnp.max(s, axis=-1, keepdims=True)
            p = jnp.exp2(s - m)
            l = jnp.sum(p, axis=-1, keepdims=True)
            o = _dot(p.astype(BF16), vcat[h])
            o_ref[r0:r0 + sub, h * LANES:(h + 1) * LANES] = (o / l).astype(o_ref.dtype)


def _mla_call(p_lat, kv_lat, p_ctx, kv_ctx, batch, s_len, lc, heads_per_step=2):
    hp = heads_per_step
    kpe_blk = T_KPE * 2 + 1
    v0 = MLA_HEADS // hp
    return pl.pallas_call(
        _mla_kernel,
        grid=(batch, MLA_HEADS // hp),
        in_specs=[pl.BlockSpec((s_len, hp * QTILE_W), lambda b, h: (b, h)),
                  pl.BlockSpec((s_len, hp * LANES), lambda b, h: (b, h)),
                  pl.BlockSpec((s_len, LANES), lambda b, h: (b, kpe_blk)),
                  pl.BlockSpec((s_len, hp * LANES), lambda b, h: (b, v0 + h)),
                  pl.BlockSpec((lc, hp * LANES), lambda b, h: (b, h)),
                  pl.BlockSpec((lc, LANES), lambda b, h: (b, kpe_blk - 2 * CTX_TILE0)),
                  pl.BlockSpec((lc, hp * LANES), lambda b, h: (b, v0 + h))],
        out_specs=pl.BlockSpec((s_len, hp * LANES), lambda b, h: (b, h)),
        out_shape=jax.ShapeDtypeStruct((batch * s_len, MLA_HEADS * MLA_V), BF16),
        scratch_shapes=[pltpu.VMEM((hp, s_len + lc, QTILE_W), BF16),
                        pltpu.VMEM((hp, s_len + lc, LANES), BF16)],
        compiler_params=_params(("arbitrary", "arbitrary")),
        name="mla_attention",
    )(p_lat, kv_lat, p_lat, kv_lat, kv_ctx, p_ctx, kv_ctx)


def _na_kernel(q_ref, k_ref, v_ref, kc_ref, vc_ref, bias_ref, o_ref, *, rows, var_map):
    win = NA_WR * GRID_W
    tq = NA_G * GRID_W
    for h in range(q_ref.shape[1] // LANES):
        hs = slice(h * LANES, (h + 1) * LANES)
        for g in range(rows // NA_G):
            start = int(np.clip(NA_G * g - NA_KH // 2, 0, rows - NA_WR)) * GRID_W
            q = q_ref[g * tq:(g + 1) * tq, hs]
            sw = _dot_nt(q, k_ref[start:start + win, hs]) + bias_ref[var_map[g], h]
            sc = _dot_nt(q, kc_ref[:, hs])
            m = jnp.maximum(jnp.max(sw, axis=-1, keepdims=True), jnp.max(sc, axis=-1, keepdims=True))
            pw = jnp.exp2(sw - m)
            pc = jnp.exp2(sc - m)
            l = jnp.sum(pw, axis=-1, keepdims=True) + jnp.sum(pc, axis=-1, keepdims=True)
            o = _dot(pw.astype(BF16), v_ref[start:start + win, hs]) + _dot(pc.astype(BF16), vc_ref[:, hs])
            o_ref[g * tq:(g + 1) * tq, hs] = (o / l).astype(o_ref.dtype)


def _na_tables(rows):
    kh = min(NA_KH, rows)
    ng = rows // NA_G
    qr = np.arange(NA_G)[:, None]
    kr = np.arange(NA_WR)[None, :]
    sel_rows = []
    for g in range(ng):
        start_row = int(np.clip(NA_G * g - NA_KH // 2, 0, rows - NA_WR))
        r = NA_G * g + qr
        krow = start_row + kr
        rs = np.clip(r - kh // 2, 0, rows - kh)
        ok = (krow >= rs) & (krow < rs + kh)
        dr = krow - r + (NA_KH - 1)
        sel_rows.append(np.stack([ok & (dr == d) for d in range(2 * NA_KH - 1)]))
    sel_rows = np.stack(sel_rows)
    uniq, inverse = np.unique(sel_rows.reshape(ng, -1), axis=0, return_inverse=True)
    er = uniq.reshape((-1,) + sel_rows.shape[1:]).astype(np.float32)
    qc = np.arange(GRID_W)[:, None]
    kc = np.arange(GRID_W)[None, :]
    cs = np.clip(qc - NA_KW // 2, 0, GRID_W - NA_KW)
    ok_c = (kc >= cs) & (kc < cs + NA_KW)
    dc = np.clip(kc - qc, -(NA_KW - 1), NA_KW - 1) + (NA_KW - 1)
    ec = np.stack([ok_c & (dc == e) for e in range(2 * NA_KW - 1)]).astype(np.float32)
    return inverse.reshape(-1).astype(np.int32), er, ec


def _na_bias(rpb, er, ec):
    hp = lax.Precision.HIGHEST
    t = jnp.einsum('hde,eqk->hdqk', rpb.astype(F32), jnp.asarray(ec), precision=hp)
    t = jnp.where(jnp.asarray(ec.sum(axis=0) > 0.5), t * LOG2E, NEG_INF)
    outside = jnp.full(t[:, 0].shape, NEG_INF, F32)
    d_of = np.where(er.sum(axis=1) > 0.5, er.argmax(axis=1), -1)
    block = lambda d: outside if d < 0 else t[:, d]
    return jnp.stack([
        jnp.concatenate([jnp.concatenate([block(int(d)) for d in row], axis=-1) for row in var], axis=-2)
        for var in d_of])


def _na_call(p_lat, p_ctx, bias, var_map, batch, s_len, lc):
    rows = s_len // GRID_W
    hp = NA_HEADS_PER_STEP
    tw = hp * LANES
    tq = NA_G * GRID_W
    win = NA_WR * GRID_W
    qb, kb, vb = (T_QNA * QTILE_W // tw, T_KNA * QTILE_W // tw, T_VNA * QTILE_W // tw)
    cb = CTX_TILE0 * QTILE_W // tw
    n_var = bias.shape[0]
    return pl.pallas_call(
        functools.partial(_na_kernel, rows=rows, var_map=tuple(int(v) for v in var_map)),
        grid=(batch, NA_HEADS // hp),
        in_specs=[pl.BlockSpec((s_len, tw), lambda b, h: (b, qb + h)),
                  pl.BlockSpec((s_len, tw), lambda b, h: (b, kb + h)),
                  pl.BlockSpec((s_len, tw), lambda b, h: (b, vb + h)),
                  pl.BlockSpec((lc, tw), lambda b, h: (b, kb + h - cb)),
                  pl.BlockSpec((lc, tw), lambda b, h: (b, vb + h - cb)),
                  pl.BlockSpec((n_var, hp, tq, win), lambda b, h: (0, h, 0, 0))],
        out_specs=pl.BlockSpec((s_len, tw), lambda b, h: (b, h)),
        out_shape=jax.ShapeDtypeStruct((batch * s_len, NA_HEADS * NA_HEAD_DIM), BF16),
        compiler_params=_params(("arbitrary", "arbitrary")),
        name="na_attention",
    )(p_lat, p_lat, p_lat, p_ctx, p_ctx, bias)


def _layer_norm(x, g, b):
    mu = jnp.mean(x, axis=-1, keepdims=True)
    xc = x - mu
    var = jnp.mean(xc * xc, axis=-1, keepdims=True)
    return xc * lax.rsqrt(var + LN_EPS) * g + b


def _outln_kernel(a1_ref, a2_ref, w_ref, h_ref, mod_ref, ln_ref, r_ref, h1_ref, u2_ref, lg_ref, wb_ref):
    half = a1_ref.shape[1]

    @pl.when(pl.program_id(0) == 0)
    def _():
        rows = wb_ref.shape[0] // 4
        for r0 in range(0, wb_ref.shape[0], rows):
            wb_ref[r0:r0 + rows, :] = w_ref[0, r0:r0 + rows, :].astype(wb_ref.dtype)

    sub = a1_ref.shape[0] // OUTLN_SUBTILES
    for r0 in range(0, a1_ref.shape[0], sub):
        rs = slice(r0, r0 + sub)
        y = _dot(a1_ref[rs, :], wb_ref[:half, :]) + _dot(a2_ref[rs, :], wb_ref[half:, :])
        x = DEEPNORM_ALPHA * h_ref[rs, :] + mod_ref[0, 0:1, :] * y
        hn = _layer_norm(x, ln_ref[0:1, :], ln_ref[1:2, :])
        h1_ref[rs, :] = hn
        u = hn * (1.0 + mod_ref[0, 2:3, :]) + mod_ref[0, 1:2, :]
        u_bf = u.astype(BF16)
        u2_ref[rs, :] = u_bf
        lg_ref[rs, :] = _dot(u_bf, r_ref[...])


def _outln_call(a1, a2, w, layer, h, mod, ln, router2, rows_per_group, tm=512):
    m, d = h.shape
    half = a1.shape[1]
    tpg = rows_per_group // tm
    return pl.pallas_call(
        _outln_kernel,
        grid=(m // tm,),
        in_specs=[pl.BlockSpec((tm, half), lambda i: (i, 0)),
                  pl.BlockSpec((tm, half), lambda i: (i, 0)),
                  pl.BlockSpec((1, 2 * half, d), lambda i: (layer, 0, 0), pipeline_mode=pl.Buffered(1)),
                  pl.BlockSpec((tm, d), lambda i: (i, 0)),
                  pl.BlockSpec((1, 3, d), lambda i: (i // tpg, 0, 0)),
                  pl.BlockSpec((2, d), lambda i: (0, 0)),
                  _resident_spec((d, LANES))],
        out_specs=[pl.BlockSpec((tm, d), lambda i: (i, 0)),
                   pl.BlockSpec((tm, d), lambda i: (i, 0)),
                   pl.BlockSpec((tm, LANES), lambda i: (i, 0))],
        out_shape=[jax.ShapeDtypeStruct((m, d), F32),
                   jax.ShapeDtypeStruct((m, d), BF16),
                   jax.ShapeDtypeStruct((m, LANES), F32)],
        scratch_shapes=[pltpu.VMEM((2 * half, d), BF16)],
        compiler_params=_params(("arbitrary",)),
        name="out_proj_postnorm",
    )(a1, a2, w, h, mod, ln, router2)


def _router_operand(router):
    d, e = router.shape
    return jnp.pad(router, ((0, 0), (0, LANES - e))).astype(BF16)


def _route_kernel(lg_ref, slot_ref, aff_ref, cnt_ref, tri_ref, *, cap):
    n_tok = lg_ref.shape[2]

    @pl.when(pl.program_id(0) == 0)
    def _():
        chunk = 256
        for r0 in range(0, n_tok, chunk):
            r = r0 + lax.broadcasted_iota(I32, (chunk, n_tok), 0)
            c = lax.broadcasted_iota(I32, (chunk, n_tok), 1)
            tri_ref[r0:r0 + chunk, :] = jnp.where(r < c, 1.0, 0.0).astype(BF16)

    lg = lg_ref[0]
    ex = jnp.exp(lg - jnp.max(lg, axis=0, keepdims=True))
    aff = ex / jnp.sum(ex, axis=0, keepdims=True)
    bits = lax.bitcast_convert_type(aff, I32)
    n_e = lg.shape[0]
    count = lambda mask: jnp.sum(jnp.where(mask, 1.0, 0.0), axis=1, keepdims=True)

    def body(_, lohi):
        lo, hi = lohi
        mid = lo + jnp.right_shift(hi - lo, 1)
        ge = count(bits >= mid) >= cap
        return jnp.where(ge, mid, lo), jnp.where(ge, hi, mid)

    lo0 = jnp.zeros((n_e, 1), I32)
    hi0 = jnp.full((n_e, 1), 0x7F800000, I32)
    thr, _ = lax.fori_loop(0, 31, body, (lo0, hi0))
    gt = bits > thr
    eq = bits == thr
    need = cap - count(gt)
    pre_eq = _dot(jnp.where(eq, 1.0, 0.0).astype(BF16), tri_ref[...])
    sel = jnp.logical_or(gt, jnp.logical_and(eq, pre_eq < need))
    sel_bf = jnp.where(sel, 1.0, 0.0).astype(BF16)
    slot = _dot(sel_bf, tri_ref[...])
    slot_ref[0] = jnp.where(sel, slot.astype(I32), -1)
    aff_ref[0] = aff
    r = lax.broadcasted_iota(I32, (n_tok, LANES), 0)
    c = lax.broadcasted_iota(I32, (n_tok, LANES), 1)
    before = jnp.where(r < c * MOE_CHUNK, 1.0, 0.0).astype(BF16)
    cnt_ref[0] = _dot(sel_bf, before).astype(I32)


def _route_call(lg_t, cap):
    b, e, n_tok = lg_t.shape
    spec = pl.BlockSpec((1, e, n_tok), lambda i: (i, 0, 0))
    return pl.pallas_call(
        functools.partial(_route_kernel, cap=cap),
        grid=(b,),
        in_specs=[spec],
        out_specs=[spec, spec, pl.BlockSpec((1, e, LANES), lambda i: (i, 0, 0))],
        out_shape=[jax.ShapeDtypeStruct((b, e, n_tok), I32), jax.ShapeDtypeStruct((b, e, n_tok), F32),
                   jax.ShapeDtypeStruct((b, e, LANES), I32)],
        scratch_shapes=[pltpu.VMEM((n_tok, n_tok), BF16)],
        compiler_params=_params(("arbitrary",)),
        name="ec_route",
    )(lg_t)


def _slot_windows(cnt, cap, n_tok):
    bounds = cnt[:, :, :n_tok // MOE_CHUNK + 1]
    start, end = bounds[:, :, :-1], bounds[:, :, 1:]
    starts, tier = [], jnp.full(start[:, 0].shape, len(MOE_WINS), I32)
    for i, win in reversed(list(enumerate(MOE_WINS))):
        a = jnp.minimum((start // 16) * 16, cap - win)
        tier = jnp.where(jnp.all(end <= a + win, axis=1), i, tier)
        starts.insert(0, jnp.swapaxes(a, 1, 2).reshape(-1))
    return jnp.concatenate(starts).astype(I32), tier.reshape(-1).astype(I32)


def _gather_kernel(win_ref, tier_ref, slot_ref, u_ref, o_ref, *, chunk):
    b, ks = pl.program_id(0), pl.program_id(1)
    n_e, cap = o_ref.shape[0], o_ref.shape[1]
    n_sub = u_ref.shape[0] // chunk
    step = b * pl.num_programs(1) + ks

    @pl.when(ks == 0)
    def _():
        o_ref[...] = jnp.zeros(o_ref.shape, o_ref.dtype)

    def place(rows, start_of, group, tok0, n_t):
        row = lax.broadcasted_iota(I32, (rows, n_t), 0)
        for g0 in range(0, n_e, group):
            starts = [start_of(e) for e in range(g0, g0 + group)]
            onehot = jnp.concatenate(
                [jnp.where(row + st == slot_ref[0, e:e + 1, tok0:tok0 + n_t], 1.0, 0.0).astype(BF16)
                 for st, e in zip(starts, range(g0, g0 + group))], axis=0)
            got = _dot(onehot, u_ref[tok0:tok0 + n_t, :]).astype(o_ref.dtype)
            for i, (st, e) in enumerate(zip(starts, range(g0, g0 + group))):
                o_ref[e, pl.ds(st, rows), :] += got[i * rows:(i + 1) * rows]

    n_chunks = pl.num_programs(0) * pl.num_programs(1) * n_sub
    for sub in range(n_sub):
        c = step * n_sub + sub
        for i, win in enumerate(MOE_WINS):
            @pl.when(tier_ref[c] == i)
            def _(c=c, sub=sub, i=i, win=win):
                base = (i * n_chunks + c) * n_e
                place(win, lambda e: pl.multiple_of(win_ref[base + e], 16), 512 // win, sub * chunk, chunk)

        @pl.when(tier_ref[c] == len(MOE_WINS))
        def _(sub=sub):
            place(cap, lambda e: 0, 2, sub * chunk, chunk)


def _gather_call(starts, tier, slot, u2, cap, chunks_per_step=2):
    b, e, n_tok = slot.shape
    d = u2.shape[1]
    chunk = MOE_CHUNK
    tt = chunk * chunks_per_step
    nc = n_tok // tt
    grid_spec = pltpu.PrefetchScalarGridSpec(
        num_scalar_prefetch=2,
        grid=(b, nc),
        in_specs=[pl.BlockSpec((1, e, tt), lambda i, k, w, f: (i, 0, k)),
                  pl.BlockSpec((tt, d), lambda i, k, w, f: (i * nc + k, 0))],
        out_specs=pl.BlockSpec((e, cap, d), lambda i, k, w, f: (0, i, 0)),
    )
    return pl.pallas_call(
        functools.partial(_gather_kernel, chunk=chunk),
        grid_spec=grid_spec,
        out_shape=jax.ShapeDtypeStruct((e, b * cap, d), BF16),
        compiler_params=_params(("arbitrary", "arbitrary")),
        name="moe_gather",
    )(starts, tier, slot, u2)


def _expert_hidden_kernel(x_ref, w1_ref, w3_ref, o_ref, *, ff):
    tf = w1_ref.shape[3]
    last = pl.num_programs(1) - 1

    def block(width):
        w = jnp.concatenate([w1_ref[0, 0, :, :width].astype(BF16), w3_ref[0, 0, :, :width].astype(BF16)], axis=1)
        h = _dot(x_ref[0], w)
        h1 = h[:, :width]
        o_ref[0, :, :width] = ((h1 / (1.0 + jnp.exp(-h1))) * h[:, width:]).astype(o_ref.dtype)

    tail = ff - (pl.cdiv(ff, tf) - 1) * tf
    if tail == tf:
        block(tf)
    else:
        pl.when(pl.program_id(1) < last)(lambda: block(tf))
        pl.when(pl.program_id(1) == last)(lambda: block(tail))


def _expert_out_kernel(hd_ref, w2_ref, o_ref):
    y = _dot(hd_ref[0], w2_ref[0, 0].astype(BF16)).astype(o_ref.dtype)
    cap = o_ref.shape[1]
    for b in range(o_ref.shape[0]):
        o_ref[b] = y[b * cap:(b + 1) * cap]


def _expert_call(xe, w1, w3, w2, layer, batch, tf=768, tn=2048):
    e, m, d = xe.shape
    cap = m // batch
    ff = w1.shape[3]
    hidden = pl.pallas_call(
        functools.partial(_expert_hidden_kernel, ff=ff),
        grid=(e, pl.cdiv(ff, tf)),
        in_specs=[pl.BlockSpec((1, m, d), lambda i, f: (i, 0, 0)),
                  pl.BlockSpec((1, 1, d, tf), lambda i, f: (layer, i, 0, f)),
                  pl.BlockSpec((1, 1, d, tf), lambda i, f: (layer, i, 0, f))],
        out_specs=pl.BlockSpec((1, m, tf), lambda i, f: (i, 0, f)),
        out_shape=jax.ShapeDtypeStruct((e, m, ff), BF16),
        compiler_params=_params(("arbitrary", "arbitrary")),
        name="moe_expert_hidden",
    )(xe, w1, w3)
    return pl.pallas_call(
        _expert_out_kernel,
        grid=(e, d // tn),
        in_specs=[pl.BlockSpec((1, m, ff), lambda i, j: (i, 0, 0)),
                  pl.BlockSpec((1, 1, ff, tn), lambda i, j: (layer, i, 0, j))],
        out_specs=pl.BlockSpec((batch, cap, tn), lambda i, j: (0, i, j)),
        out_shape=jax.ShapeDtypeStruct((batch, e * cap, d), BF16),
        compiler_params=_params(("arbitrary", "arbitrary")),
        name="moe_expert_out",
    )(hidden, w2)


def _scatter_kernel(win_ref, tier_ref, slot_ref, aff_ref, ye_ref, h_ref, mod_ref, ln_ref, *rest, with_next):
    if with_next:
        nmod_ref, h2_ref, un_ref, pt_ref, yew_ref, moe_ref = rest
    else:
        h2_ref, pt_ref, yew_ref, moe_ref = rest
    n_e = slot_ref.shape[1]
    cap = ye_ref.shape[1] // n_e
    tt = h_ref.shape[0]
    step = pl.program_id(0) * pl.num_programs(1) + pl.program_id(1)

    def gates(rows, e, start):
        row = lax.broadcasted_iota(I32, (rows, tt), 0)
        return jnp.where(row + start == slot_ref[0, e:e + 1, :], aff_ref[0, e:e + 1, :], 0.0).astype(BF16)

    n_chunks = pl.num_programs(0) * pl.num_programs(1)
    for i, win in enumerate(MOE_WINS):
        @pl.when(tier_ref[step] == i)
        def _(i=i, win=win):
            for e in range(n_e):
                start = pl.multiple_of(win_ref[(i * n_chunks + step) * n_e + e], 16)
                pt_ref[e * win:(e + 1) * win, :] = gates(win, e, start)
                yew_ref[e * win:(e + 1) * win, :] = ye_ref[0, pl.ds(e * cap + start, win), :]
            moe_ref[...] = _dot_tn(pt_ref[:n_e * win, :], yew_ref[:n_e * win, :])

    @pl.when(tier_ref[step] == len(MOE_WINS))
    def _():
        for e in range(n_e):
            pt_ref[e * cap:(e + 1) * cap, :] = gates(cap, e, 0)
        moe_ref[...] = _dot_tn(pt_ref[...], ye_ref[0])

    x = DEEPNORM_ALPHA * h_ref[...] + mod_ref[0] * moe_ref[...]
    hn = _layer_norm(x, ln_ref[0:1, :], ln_ref[1:2, :])
    h2_ref[...] = hn
    if with_next:
        un_ref[...] = (hn * (1.0 + nmod_ref[0, 1:2, :]) + nmod_ref[0, 0:1, :]).astype(un_ref.dtype)


def _scatter_call(starts, tier, slot, aff, ye, h1, gate, ln, next_ss):
    b, e, n_tok = slot.shape
    m, d = h1.shape
    tt = MOE_CHUNK
    nt = n_tok // tt
    rows = ye.shape[1]
    with_next = next_ss is not None
    tok_spec = pl.BlockSpec((1, e, tt), lambda i, t, w, f: (i, 0, t))
    row_spec = pl.BlockSpec((tt, d), lambda i, t, w, f: (i * nt + t, 0))
    in_specs = [tok_spec, tok_spec,
                pl.BlockSpec((1, rows, d), lambda i, t, w, f: (i, 0, 0), pipeline_mode=pl.Buffered(1)),
                row_spec,
                pl.BlockSpec((1, 1, d), lambda i, t, w, f: (i, 0, 0)),
                pl.BlockSpec((2, d), lambda i, t, w, f: (0, 0))]
    args = [slot, aff, ye, h1, gate, ln]
    out_specs = [row_spec]
    out_shape = [jax.ShapeDtypeStruct((m, d), F32)]
    if with_next:
        in_specs.append(pl.BlockSpec((1, 2, d), lambda i, t, w, f: (i, 0, 0)))
        args.append(next_ss)
        out_specs.append(row_spec)
        out_shape.append(jax.ShapeDtypeStruct((m, d), BF16))
    grid_spec = pltpu.PrefetchScalarGridSpec(
        num_scalar_prefetch=2,
        grid=(b, nt),
        in_specs=in_specs,
        out_specs=out_specs,
        scratch_shapes=[pltpu.VMEM((rows, tt), BF16),
                        pltpu.VMEM((e * max(MOE_WINS), d), BF16),
                        pltpu.VMEM((tt, d), F32)],
    )
    res = pl.pallas_call(
        functools.partial(_scatter_kernel, with_next=with_next),
        grid_spec=grid_spec,
        out_shape=out_shape,
        compiler_params=_params(("arbitrary", "arbitrary")),
        name="moe_combine_postnorm",
    )(starts, tier, *args)
    return res if with_next else (res[0], None)


def _ec_moe(h1, u2, logits, w1, w3, w2, layer, gate, ln, next_ss, batch, n_tok):
    cap = EC_CAPACITY_FACTOR * n_tok // N_EXPERTS
    lg_t = jnp.swapaxes(logits[:, :N_EXPERTS].reshape(batch, n_tok, N_EXPERTS), 1, 2)
    slot, aff, cnt = _route_call(lg_t, cap)
    starts, tier = _slot_windows(cnt, cap, n_tok)
    xe = _gather_call(starts, tier, slot, u2, cap)
    ye = _expert_call(xe, w1, w3, w2, layer, batch)
    return _scatter_call(starts, tier, slot, aff, ye, h1, gate, ln, next_ss)


def _mm_kernel(a_ref, w_ref, o_ref, wb_ref):
    @pl.when(pl.program_id(1) == 0)
    def _():
        wb_ref[...] = w_ref[0].astype(wb_ref.dtype)

    o_ref[...] = _dot(a_ref[...], wb_ref[...]).astype(o_ref.dtype)


def _mm_call(a, w, layer, out_dtype, tm=1024, tn=1024):
    m, k = a.shape
    n = w.shape[2]
    return pl.pallas_call(
        _mm_kernel,
        grid=(n // tn, m // tm),
        in_specs=[pl.BlockSpec((tm, k), lambda j, i: (i, 0)),
                  pl.BlockSpec((1, k, tn), lambda j, i: (layer, 0, j))],
        out_specs=pl.BlockSpec((tm, tn), lambda j, i: (i, j)),
        out_shape=jax.ShapeDtypeStruct((m, n), out_dtype),
        scratch_shapes=[pltpu.VMEM((k, tn), BF16)],
        compiler_params=_params(("arbitrary", "arbitrary")),
        name="matmul",
    )(a, w)


def _short_conv(p, w, b):
    x = p.astype(F32)
    n = x.shape[0]
    r = lax.broadcasted_iota(I32, x.shape, 0)
    prev = jnp.where(r == 0, 0.0, pltpu.roll(x, 1, 0))
    nxt = jnp.where(r == n - 1, 0.0, pltpu.roll(x, n - 1, 0))
    return b + w[0:1, :] * prev + w[1:2, :] * x + w[2:3, :] * nxt


def _sconv_kernel(p_ref, w_ref, b_ref, o_ref):
    o_ref[...] = _short_conv(p_ref[...], w_ref[...], b_ref[...]).astype(o_ref.dtype)


def _sconv_call(p, conv_w, conv_b, col0, ncols, out_dtype, batch, n_tok, tc=512):
    c0 = col0 // tc
    return pl.pallas_call(
        _sconv_kernel,
        grid=(batch, ncols // tc),
        in_specs=[pl.BlockSpec((n_tok, tc), lambda b, j: (b, c0 + j)),
                  pl.BlockSpec((HY_SHORT, tc), lambda b, j: (0, c0 + j)),
                  pl.BlockSpec((1, tc), lambda b, j: (0, c0 + j))],
        out_specs=pl.BlockSpec((n_tok, tc), lambda b, j: (b, j)),
        out_shape=jax.ShapeDtypeStruct((batch * n_tok, ncols), out_dtype),
        compiler_params=_params(("arbitrary", "arbitrary")),
        name="hyena_short_conv",
    )(p, conv_w, conv_b.reshape(1, -1))


def _dft_fwd_kernel(fc_ref, fs_ref, r1_ref, r2_ref, *rest, spectral):
    o_ref = rest[-1]
    sub = fc_ref.shape[0] // SEQMIX_SUBTILES
    for r0 in range(0, fc_ref.shape[0], sub):
        rs = slice(r0, r0 + sub)
        xr = _dot(fc_ref[rs, :], r1_ref[...])
        xi = _dot(fs_ref[rs, :], r2_ref[...])
        if spectral:
            h_ref, nyq_ref = rest[:2]
            hr = h_ref[0, 0, rs, :]
            hi = h_ref[0, 1, rs, :]
            if r0 == 0:
                dc = lax.broadcasted_iota(I32, xr.shape, 0) == 0
                yr = xr * hr - jnp.where(dc, 0.0, xi * hi)
                yi = jnp.where(dc, xi * nyq_ref[...], xr * hi + xi * hr)
            else:
                yr = xr * hr - xi * hi
                yi = xr * hi + xi * hr
        else:
            yr, yi = xr, xi
        o_ref[0, 0, rs, :] = yr.astype(o_ref.dtype)
        o_ref[0, 1, rs, :] = yi.astype(o_ref.dtype)


def _resident_spec(shape):
    return pl.BlockSpec(shape, lambda *_: (0,) * len(shape), pipeline_mode=pl.Buffered(1))


def _dft_fwd_call(fc, fs, r1, r2, spec, out_dtype, batch, n_tok, tn=256):
    ncols = r1.shape[1]
    r_spec = pl.BlockSpec((n_tok, tn), lambda j, b: (b, j))
    in_specs = [_resident_spec(fc.shape), _resident_spec(fs.shape), r_spec, r_spec]
    args = [fc, fs, r1, r2]
    if spec is not None:
        h, nyq, hcol0 = spec
        c0 = hcol0 // tn
        in_specs += [pl.BlockSpec((1, 2, n_tok, tn), lambda j, b: (0, 0, 0, c0 + j)),
                     pl.BlockSpec((1, tn), lambda j, b: (0, c0 + j))]
        args += [h, nyq]
    return pl.pallas_call(
        functools.partial(_dft_fwd_kernel, spectral=spec is not None),
        grid=(ncols // tn, batch),
        in_specs=in_specs,
        out_specs=pl.BlockSpec((1, 2, n_tok, tn), lambda j, b: (b, 0, 0, j)),
        out_shape=jax.ShapeDtypeStruct((batch, 2, n_tok, ncols), out_dtype),
        compiler_params=_params(("arbitrary", "arbitrary")),
        name="dft_forward",
    )(*args)


def _dual_kernel(gc_ref, gs_ref, y_ref, p_ref, cw_ref, cb_ref, z_ref, skip_ref, o_ref):
    yr, yi = y_ref[0, 0], y_ref[0, 1]
    gate = _short_conv(p_ref[...], cw_ref[...], cb_ref[...])
    sub = o_ref.shape[0] // SEQMIX_SUBTILES
    for r0 in range(0, o_ref.shape[0], sub):
        rs = slice(r0, r0 + sub)
        y = _dot(gc_ref[rs, :], yr) + _dot(gs_ref[rs, :], yi)
        o_ref[rs, :] = (gate[rs] * (y + skip_ref[...] * z_ref[rs, :].astype(F32))).astype(o_ref.dtype)


def _dual_call(gc, gs, y_spec, p, conv_w, conv_b, pcol0, z, skip, batch, n_tok, ncols, tn=256):
    c0 = pcol0 // tn
    o_spec = pl.BlockSpec((n_tok, tn), lambda b, j: (b, j))
    return pl.pallas_call(
        _dual_kernel,
        grid=(batch, ncols // tn),
        in_specs=[_resident_spec(gc.shape), _resident_spec(gs.shape),
                  pl.BlockSpec((1, 2, n_tok, tn), lambda b, j: (b, 0, 0, j)),
                  pl.BlockSpec((n_tok, tn), lambda b, j: (b, c0 + j)),
                  pl.BlockSpec((HY_SHORT, tn), lambda b, j: (0, c0 + j)),
                  pl.BlockSpec((1, tn), lambda b, j: (0, c0 + j)),
                  o_spec,
                  pl.BlockSpec((1, tn), lambda b, j: (0, j))],
        out_specs=o_spec,
        out_shape=jax.ShapeDtypeStruct((batch * n_tok, ncols), BF16),
        compiler_params=_params(("arbitrary", "arbitrary")),
        name="seq_mix_matmul",
    )(gc, gs, y_spec, p, conv_w, conv_b.reshape(1, -1), z, skip)


def _fnet_kernel(cl_ref, msl_ref, x_ref, cw_ref, o_ref):
    gw = cw_ref.shape[1]
    for c0 in range(0, x_ref.shape[1], gw):
        x = x_ref[:, c0:c0 + gw]
        seq = jnp.concatenate([_dot(cl_ref[...], x).astype(BF16), _dot(msl_ref[...], x).astype(BF16)], axis=1)
        o_ref[:, c0:c0 + gw] = _dot(seq, cw_ref[...]).astype(o_ref.dtype)


def _fnet_call(cl, msl, p, cw, col0, batch, n_tok, groups_per_step=2):
    gw = cw.shape[1]
    tw = gw * groups_per_step
    c0 = col0 // tw
    return pl.pallas_call(
        _fnet_kernel,
        grid=(batch, FN_GROUPS // groups_per_step),
        in_specs=[_resident_spec(cl.shape), _resident_spec(msl.shape),
                  pl.BlockSpec((n_tok, tw), lambda b, g: (b, c0 + g)),
                  pl.BlockSpec(cw.shape, lambda b, g: (0, 0))],
        out_specs=pl.BlockSpec((n_tok, tw), lambda b, g: (b, g)),
        out_shape=jax.ShapeDtypeStruct((batch * n_tok, FN_GROUPS * gw), BF16),
        compiler_params=_params(("arbitrary", "arbitrary")),
        name="fnet_mix",
    )(cl, msl, p, cw)


def _cos_sin_matrix(n_rows, n_cols, period, split=64):
    r = np.arange(n_rows, dtype=np.int64)[:, None]
    c_hi = (np.arange(n_cols // split, dtype=np.int64) * split)[None, :]
    c_lo = np.arange(split, dtype=np.int64)[None, :]
    ang = lambda c: 2.0 * np.pi * ((r * c) % period).astype(np.float64) / period
    ca, sa = jnp.asarray(np.cos(ang(c_hi)), F32), jnp.asarray(np.sin(ang(c_hi)), F32)
    cb, sb = jnp.asarray(np.cos(ang(c_lo)), F32), jnp.asarray(np.sin(ang(c_lo)), F32)
    cos = ca[:, :, None] * cb[:, None, :] - sa[:, :, None] * sb[:, None, :]
    sin = sa[:, :, None] * cb[:, None, :] + ca[:, :, None] * sb[:, None, :]
    return cos.reshape(n_rows, n_cols), sin.reshape(n_rows, n_cols)


def _trig_kernel(ta_ref, tb_ref, ea_ref, eb_ref, *o_refs, mode, period, scale):
    sa_ca = _dot(ta_ref[...], ea_ref[...])
    sb_cb = _dot(tb_ref[...], eb_ref[...])
    nc = sa_ca.shape[1] // 2
    ca, sa, cb, sb = sa_ca[:, :nc], sa_ca[:, nc:], sb_cb[:, :nc], sb_cb[:, nc:]
    cos = ca * cb - sa * sb
    sin = sa * cb + ca * sb
    tm = cos.shape[0]
    row = pl.program_id(0) * tm + lax.broadcasted_iota(I32, cos.shape, 0)
    col = lax.broadcasted_iota(I32, cos.shape, 1)
    alt = lambda idx: (1 - 2 * jnp.bitwise_and(idx, 1)).astype(F32)
    if mode == "hyena":
        fc_ref, fs_ref, gc_ref, gs_ref = o_refs
        fc_ref[...] = cos.astype(fc_ref.dtype)
        fs_ref[...] = jnp.where(row == 0, alt(col), -sin).astype(fs_ref.dtype)
        gc_ref[...] = (cos * jnp.where(col == 0, 1.0 / period, 2.0 / period)).astype(gc_ref.dtype)
        gs_ref[...] = jnp.where(col == 0, alt(row) * (1.0 / period), sin * (-2.0 / period)).astype(gs_ref.dtype)
    else:
        c_ref, ms_ref = o_refs
        c_ref[...] = (cos * scale).astype(c_ref.dtype)
        ms_ref[...] = (sin * (-scale)).astype(ms_ref.dtype)


def _split3(x):
    bf = jnp.bfloat16
    x = np.asarray(x, np.float32)
    hi = x.astype(bf)
    r1 = x - hi.astype(np.float32)
    mid = r1.astype(bf)
    lo = (r1 - mid.astype(np.float32)).astype(bf)
    return [hi, mid, lo]


def _trig_call(n, period, mode, scale=1.0, split=64, tm=256):
    r = np.arange(n, dtype=np.int64)[:, None]
    c_hi = (np.arange(n // split, dtype=np.int64) * split)[None, :]
    c_lo = np.arange(split, dtype=np.int64)[None, :]
    ang = lambda c: 2.0 * np.pi * ((r * c) % period).astype(np.float64) / period
    def operands(a, col_group):
        terms = _split3(np.cos(a)) + _split3(np.sin(a))
        k = a.shape[1]
        spread = (col_group[None, :] == np.arange(k)[:, None]).astype(np.float32)
        zero = np.zeros_like(spread)
        e = np.concatenate([np.concatenate([spread, zero], 1)] * 3 + [np.concatenate([zero, spread], 1)] * 3, 0)
        return jnp.asarray(np.concatenate(terms, axis=1)), jnp.asarray(e, BF16)
    cols = np.arange(n)
    ta, ea = operands(ang(c_hi), cols // split)
    tb, eb = operands(ang(c_lo), cols % split)
    n_out = 4 if mode == "hyena" else 2
    o_spec = pl.BlockSpec((tm, n), lambda i: (i, 0))
    return pl.pallas_call(
        functools.partial(_trig_kernel, mode=mode, period=period, scale=scale),
        grid=(n // tm,),
        in_specs=[pl.BlockSpec((tm, ta.shape[1]), lambda i: (i, 0)),
                  pl.BlockSpec((tm, tb.shape[1]), lambda i: (i, 0)),
                  pl.BlockSpec(ea.shape, lambda i: (0, 0)),
                  pl.BlockSpec(eb.shape, lambda i: (0, 0))],
        out_specs=[o_spec] * n_out,
        out_shape=[jax.ShapeDtypeStruct((n, n), BF16)] * n_out,
        compiler_params=_params(("arbitrary",)),
        name="dft_tables",
    )(ta, tb, ea, eb)


def _hyena_dft_operands(n_tok):
    return _trig_call(n_tok, 2 * n_tok, "hyena")


def _fnet_operands(n_tok, gw):
    cl, msl = _trig_call(n_tok, n_tok, "fnet", scale=1.0 / math.sqrt(n_tok * gw))
    cw, sw = _cos_sin_matrix(gw, gw, gw)
    return cl, msl, jnp.concatenate([cw, sw], axis=0).astype(BF16)


def _filter_kernel(z_ref, w1_ref, b1_ref, f1_ref, w2_ref, b2_ref, f2_ref, w3f_ref, w3b_ref, dl_ref,
                   sum_ref, dif_ref, nyq_ref, hdn_ref):
    @pl.when(jnp.logical_and(pl.program_id(0) == 0, pl.program_id(1) == 0))
    def _():
        hdot = lambda a, b: jnp.dot(a, b, precision=lax.Precision.HIGHEST, preferred_element_type=F32)
        hdn = jnp.sin(f1_ref[...] * (hdot(z_ref[...], w1_ref[...]) + b1_ref[...]))
        hdn = jnp.sin(f2_ref[...] * (hdot(hdn, w2_ref[...]) + b2_ref[...]))
        hdn_ref[...] = hdn.astype(hdn_ref.dtype)

    n = hdn_ref.shape[0]
    row = lax.broadcasted_iota(I32, (n, dl_ref.shape[1]), 0)
    decay = jnp.exp(-(row.astype(F32) / (n - 1)) * dl_ref[...])
    hf = _dot(hdn_ref[...], w3f_ref[...].astype(BF16)) * decay
    hb = jnp.where(row == 0, 0.0, _dot(hdn_ref[...], w3b_ref[...].astype(BF16)) * decay)
    tot = hf + hb
    sum_ref[...] = tot.astype(sum_ref.dtype)
    dif_ref[...] = (hf - hb).astype(dif_ref.dtype)
    nyq_ref[...] = jnp.sum(jnp.where(jnp.bitwise_and(row, 1) == 0, tot, -tot), axis=0, keepdims=True)


def _filter_call(n_tok, fw1, fb1, ff1, fw2, fb2, ff2, fw3, tc=512):
    t01 = np.linspace(0.0, 1.0, n_tok, dtype=np.float32)
    w = (2.0 * math.pi * np.arange(n_tok, dtype=np.float32) / n_tok).astype(np.float32)
    bands = np.linspace(1e-4, HY_BANDS - 1, HY_BANDS, dtype=np.float32)
    z = np.concatenate([t01[:, None], np.cos(w[:, None] * bands), -np.sin(w[:, None] * bands)], -1)
    deltas = np.abs(np.linspace(HY_MIN_DECAY, HY_MAX_DECAY, HY_W, dtype=np.float32))[None, :]
    emb, hid = fw1.shape
    nc = HY_W // tc
    full = lambda shape: pl.BlockSpec(shape, lambda o, j: (0,) * len(shape))
    o_spec = pl.BlockSpec((n_tok, tc), lambda o, j: (0, o * nc + j))
    return pl.pallas_call(
        _filter_kernel,
        grid=(HY_ORDER, nc),
        in_specs=[full((n_tok, emb)), full((emb, hid)), full((1, hid)), full((1, hid)),
                  full((hid, hid)), full((1, hid)), full((1, hid)),
                  pl.BlockSpec((hid, tc), lambda o, j: (0, (2 * o) * nc + j)),
                  pl.BlockSpec((hid, tc), lambda o, j: (0, (2 * o + 1) * nc + j)),
                  pl.BlockSpec((1, tc), lambda o, j: (0, j))],
        out_specs=[o_spec, o_spec, pl.BlockSpec((1, tc), lambda o, j: (0, o * nc + j))],
        out_shape=[jax.ShapeDtypeStruct((n_tok, HY_ORDER * HY_W), BF16),
                   jax.ShapeDtypeStruct((n_tok, HY_ORDER * HY_W), BF16),
                   jax.ShapeDtypeStruct((1, HY_ORDER * HY_W), F32)],
        scratch_shapes=[pltpu.VMEM((n_tok, hid), BF16)],
        compiler_params=_params(("arbitrary", "arbitrary")),
        name="hyena_filters",
    )(jnp.asarray(z.astype(np.float32)), fw1, fb1.reshape(1, hid), ff1.reshape(1, hid),
      fw2, fb2.reshape(1, hid), ff2.reshape(1, hid), fw3, fw3, jnp.asarray(deltas))


def kernel(x, c, ctx, c_ctx, ada_w, ada_b, ln1_g, ln1_b, ln2_g, ln2_b, router, exp_w1, exp_w3, exp_w2,
           ab_w_in, ab_kv_norm, ab_w_ukv, ab_rpb, ab_w_out,
           cd_w_in, cd_conv_w, cd_conv_b, cd_filt_w1, cd_filt_b1, cd_filt_freq1, cd_filt_w2, cd_filt_b2,
           cd_filt_freq2, cd_filt_w3, cd_skip, cd_w_out):
    batch, n_tok, d = x.shape
    lc = ctx.shape[1]
    x2d = x.reshape(batch * n_tok, d)
    ctx2d = ctx.reshape(batch * lc, d)

    pad_rows = (-(batch + 1)) % 8
    cv = jnp.concatenate([c, c_ctx[None, :], jnp.zeros((pad_rows, d), F32)], axis=0)
    ada = _ada_call(cv, ada_w, ada_b)
    mods = [ada[i, :batch].reshape(batch, 6, d) for i in range(DEPTH)]
    ctx_ss = ada[0, batch, :2 * d].reshape(1, 2, d)

    w_in = _w_in_prep_call(jnp.swapaxes(ab_w_in, 1, 2), 0)
    p_lat = _mm_rope_call(x2d, mods[0][:, 0:2], n_tok, w_in, _rope_tables(n_tok), tm=1024)
    p_ctx = _mm_rope_call(ctx2d, ctx_ss, batch * lc, w_in, _identity_rope_tables(batch * lc), tm=batch * lc,
                          tile0=CTX_TILE0)
    w_ukv = _permute_w_ukv(ab_w_ukv[0])
    kv_lat = _kvup_call(p_lat, ab_kv_norm[0], w_ukv, tm=1024)
    kv_ctx = _kvup_call(p_ctx, ab_kv_norm[0], w_ukv, tm=lc, tile0=CTX_TILE0)
    a_out = _mla_call(p_lat, kv_lat, p_ctx, kv_ctx, batch, n_tok, lc)
    var_map, na_er, na_ec = _na_tables(n_tok // GRID_W)
    b_out = _na_call(p_lat, p_ctx, _na_bias(ab_rpb[0], na_er, na_ec), var_map, batch, n_tok, lc)
    ln1 = jnp.stack([ln1_g, ln1_b], axis=1)
    ln2 = jnp.stack([ln2_g, ln2_b], axis=1)
    h1, u2, logits = _outln_call(a_out, b_out, ab_w_out, 0, x2d,
                                 mods[0][:, 2:5], ln1[0], _router_operand(router[0]), n_tok)
    h, u = _ec_moe(h1, u2, logits, exp_w1, exp_w3, exp_w2, 0, mods[0][:, 5:6], ln2[0],
                   mods[1][:, 0:2], batch, n_tok)

    p1 = _mm_call(u, cd_w_in, 0, BF16)
    s0 = _sconv_call(p1, cd_conv_w[0], cd_conv_b[0], 0, HY_W, BF16, batch, n_tok)
    h_sum, h_dif, h_nyq = _filter_call(n_tok, cd_filt_w1[0], cd_filt_b1[0], cd_filt_freq1[0], cd_filt_w2[0],
                                       cd_filt_b2[0], cd_filt_freq2[0], cd_filt_w3[0])
    fc, fs, gc, gs = _hyena_dft_operands(n_tok)
    h_spec = _dft_fwd_call(fc, fs, h_sum, h_dif, None, F32, 1, n_tok)
    z = s0
    for o in range(HY_ORDER):
        y_spec = _dft_fwd_call(fc, fs, z, z, (h_spec, h_nyq, o * HY_W), BF16, batch, n_tok)
        z = _dual_call(gc, gs, y_spec, p1, cd_conv_w[0], cd_conv_b[0], (o + 1) * HY_W, z, cd_skip[0][o:o + 1],
                       batch, n_tok, HY_W)
    cl, msl, cw = _fnet_operands(n_tok, FN_GROUP_W)
    y_fn = _fnet_call(cl, msl, p1, cw, HY_IN_W, batch, n_tok)
    h1, u2, logits = _outln_call(z, y_fn, cd_w_out, 0, h,
                                 mods[1][:, 2:5], ln1[1], _router_operand(router[1]), n_tok)
    h, _ = _ec_moe(h1, u2, logits, exp_w1, exp_w3, exp_w2, 1, mods[1][:, 5:6], ln2[1],
                   None, batch, n_tok)
    return h.reshape(batch, n_tok, d)
```

```python
import functools
import math

import numpy as np
import jax
import jax.numpy as jnp
from jax import lax
from jax.experimental import pallas as pl
from jax.experimental.pallas import tpu as pltpu

F32 = jnp.float32
BF16 = jnp.bfloat16
I32 = jnp.int32

DEPTH = 2
GRID_W = 64
DEEPNORM_ALPHA = (2.0 * DEPTH) ** 0.25
LN_EPS = 1e-6
NEG_INF = -1e30
LOG2E = math.log2(math.e)

MLA_HEADS = 8
MLA_NOPE = 128
MLA_ROPE = 64
MLA_QK = MLA_NOPE + MLA_ROPE
MLA_V = 128
MLA_KV_RANK = 512
ROPE_THETA = 10000.0

NA_HEADS = 8
NA_HEAD_DIM = 128
NA_KH = 8
NA_KW = 16

OFF_CKV = MLA_HEADS * MLA_QK
OFF_KPE = OFF_CKV + MLA_KV_RANK
OFF_QNA = OFF_KPE + MLA_ROPE

HY_W = 1024
HY_ORDER = 2
HY_IN_W = (HY_ORDER + 1) * HY_W
HY_SHORT = 3
HY_BANDS = 16
HY_DECAY_TARGET = 1e-2
HY_MIN_DECAY = math.log(HY_DECAY_TARGET) / 1.5
HY_MAX_DECAY = math.log(HY_DECAY_TARGET) / 0.3

FN_W = 1024
FN_GROUPS = 4
FN_GROUP_W = FN_W // FN_GROUPS

N_EXPERTS = 16
EC_CAPACITY_FACTOR = 2

LANES = 128
VMEM_LIMIT = 56 * 1024 * 1024

QTILE_W = 2 * LANES
T_KPE = MLA_HEADS
T_CKV = T_KPE + 2
T_QNA = T_CKV + MLA_KV_RANK // QTILE_W
T_KNA = T_QNA + NA_HEADS * NA_HEAD_DIM // QTILE_W
T_VNA = T_KNA + NA_HEADS * NA_HEAD_DIM // QTILE_W
N_ABTILES = T_VNA + NA_HEADS * NA_HEAD_DIM // QTILE_W
AB_PERM_W = N_ABTILES * QTILE_W
CTX_TILE0 = T_KPE
INPROJ_TILES_PER_STEP = 8

MLA_SUBTILES = 4
OUTLN_SUBTILES = 2
SEQMIX_SUBTILES = 4
MOE_CHUNK = 256
MOE_WINS = (64, 128)
NA_G = 4
NA_WR = NA_KH + NA_G - 1
NA_HEADS_PER_STEP = 2


def _params(sem, vmem=VMEM_LIMIT):
    return pltpu.CompilerParams(dimension_semantics=sem, vmem_limit_bytes=vmem)


def _dot(a, b):
    return jnp.dot(a, b, preferred_element_type=F32)


def _dot_nt(a, b):
    return lax.dot_general(a, b, (((1,), (1,)), ((), ())), preferred_element_type=F32)


def _dot_tn(a, b):
    return lax.dot_general(a, b, (((0,), (0,)), ((), ())), preferred_element_type=F32)


def _ada_kernel(c_ref, w_ref, b_ref, o_ref):
    c = c_ref[...]
    a = c / (1.0 + jnp.exp(-c))
    o_ref[0] = _dot(a.astype(BF16), w_ref[0].astype(BF16)) + b_ref[0]


def _ada_call(cv, ada_w, ada_b, tn=2048):
    depth, d, n = ada_w.shape
    rows = cv.shape[0]
    return pl.pallas_call(
        _ada_kernel,
        grid=(depth, n // tn),
        in_specs=[pl.BlockSpec((rows, d), lambda l, j: (0, 0)),
                  pl.BlockSpec((1, d, tn), lambda l, j: (l, 0, j)),
                  pl.BlockSpec((1, 1, tn), lambda l, j: (l, 0, j))],
        out_specs=pl.BlockSpec((1, rows, tn), lambda l, j: (l, 0, j)),
        out_shape=jax.ShapeDtypeStruct((depth, rows, n), F32),
        compiler_params=_params(("arbitrary", "arbitrary")),
        name="ada_params",
    )(cv, ada_w, ada_b.reshape(depth, 1, n))


def _mm_rope_kernel(x_ref, ss_ref, w_ref, ta_ref, tb_ref, tc_ref, o_ref, a_ref, *, tile0):
    @pl.when(pl.program_id(1) == 0)
    def _():
        a_ref[...] = (x_ref[...] * (1.0 + ss_ref[0, 1:2, :]) + ss_ref[0, 0:1, :]).astype(a_ref.dtype)

    a = a_ref[...]
    for q in range(INPROJ_TILES_PER_STEP):
        t = tile0 + INPROJ_TILES_PER_STEP * pl.program_id(1) + q
        c0 = q * QTILE_W
        acc = _dot_nt(a, w_ref[c0:c0 + QTILE_W, :])
        is_rope = t <= T_KPE
        f = jnp.where(t < T_KPE, LOG2E * MLA_QK ** -0.5,
                      jnp.where(jnp.logical_and(t >= T_QNA, t < T_KNA), LOG2E * NA_HEAD_DIM ** -0.5, 1.0)
                      ).astype(F32)
        hi = acc[:, LANES:]
        rot = (hi * jnp.where(is_rope, ta_ref[...], 1.0)
               + pltpu.roll(hi, LANES - MLA_ROPE // 2, 1) * jnp.where(is_rope, tb_ref[...], 0.0)
               + pltpu.roll(hi, MLA_ROPE // 2, 1) * jnp.where(is_rope, tc_ref[...], 0.0))
        o_ref[:, c0:c0 + LANES] = (acc[:, :LANES] * f).astype(o_ref.dtype)
        o_ref[:, c0 + LANES:c0 + QTILE_W] = (rot * f).astype(o_ref.dtype)


def _mm_rope_call(x, ss, rows_per_group, w, tabs, tm, tile0=0):
    m, k = x.shape
    tn = INPROJ_TILES_PER_STEP * QTILE_W
    n = w.shape[0] - tile0 * QTILE_W
    j0 = tile0 // INPROJ_TILES_PER_STEP
    ntab = tabs[0].shape[0] // tm
    tpg = rows_per_group // tm
    tab_spec = pl.BlockSpec((tm, LANES), lambda i, j: (i % ntab, 0))
    return pl.pallas_call(
        functools.partial(_mm_rope_kernel, tile0=tile0),
        grid=(m // tm, n // tn),
        in_specs=[pl.BlockSpec((tm, k), lambda i, j: (i, 0)),
                  pl.BlockSpec((1, 2, k), lambda i, j: (i // tpg, 0, 0)),
                  pl.BlockSpec((tn, k), lambda i, j: (j0 + j, 0)),
                  tab_spec, tab_spec, tab_spec],
        out_specs=pl.BlockSpec((tm, tn), lambda i, j: (i, j)),
        out_shape=jax.ShapeDtypeStruct((m, n), BF16),
        scratch_shapes=[pltpu.VMEM((tm, k), BF16)],
        compiler_params=_params(("arbitrary", "arbitrary")),
        name="attn_in_proj",
    )(x, ss, w, *tabs)


def _rope_tables(n_tok):
    t = np.arange(n_tok)
    row = (t // GRID_W).astype(np.float32)
    col = (t % GRID_W).astype(np.float32)
    n_freq = MLA_ROPE // 4
    inv = (ROPE_THETA ** (-np.arange(n_freq, dtype=np.float32) / n_freq)).astype(np.float32)
    ang = np.concatenate([row[:, None] * inv, col[:, None] * inv], axis=1)
    cos, sin = np.cos(ang).astype(np.float32), np.sin(ang).astype(np.float32)
    half = MLA_ROPE // 2
    ta = np.zeros((n_tok, LANES), np.float32)
    tb = np.zeros((n_tok, LANES), np.float32)
    tc = np.zeros((n_tok, LANES), np.float32)
    ta[:, :half] = cos
    ta[:, half:2 * half] = cos
    tb[:, :half] = -sin
    tc[:, half:2 * half] = sin
    return jnp.asarray(ta), jnp.asarray(tb), jnp.asarray(tc)


def _identity_rope_tables(n_tok):
    ta = np.zeros((n_tok, LANES), np.float32)
    ta[:, :MLA_ROPE] = 1.0
    z = np.zeros((n_tok, LANES), np.float32)
    return jnp.asarray(ta), jnp.asarray(z), jnp.asarray(z)


def _w_in_prep_kernel(w_ref, o_ref):
    x = w_ref[0]
    zeros = lambda n: jnp.zeros((n, x.shape[1]), x.dtype)
    rows = []
    for h in range(MLA_HEADS):
        rows += [x[h * MLA_QK:(h + 1) * MLA_QK, :], zeros(QTILE_W - MLA_QK)]
    rows += [zeros(LANES), x[OFF_KPE:OFF_QNA, :], zeros(LANES - MLA_ROPE), zeros(QTILE_W),
             x[OFF_CKV:OFF_KPE, :], x[OFF_QNA:, :]]
    o_ref[...] = jnp.concatenate(rows, axis=0).astype(o_ref.dtype)


def _w_in_prep_call(w_in_t, layer, tk=512):
    _, n, d = w_in_t.shape
    return pl.pallas_call(
        _w_in_prep_kernel,
        grid=(d // tk,),
        in_specs=[pl.BlockSpec((1, n, tk), lambda i: (layer, 0, i))],
        out_specs=pl.BlockSpec((AB_PERM_W, tk), lambda i: (0, i)),
        out_shape=jax.ShapeDtypeStruct((AB_PERM_W, d), BF16),
        compiler_params=_params(("arbitrary",)),
        name="attn_w_in_prep",
    )(w_in_t)


def _kvup_kernel(x_ref, g_ref, w_ref, o_ref):
    x = x_ref[...].astype(F32)
    y = x * lax.rsqrt(jnp.mean(x * x, axis=-1, keepdims=True) + LN_EPS) * g_ref[...]
    o_ref[...] = _dot(y.astype(BF16), w_ref[...]).astype(o_ref.dtype)


def _kvup_call(p, g, w, tm, tile0=0):
    m = p.shape[0]
    r, n = w.shape
    return pl.pallas_call(
        _kvup_kernel,
        grid=(m // tm,),
        in_specs=[pl.BlockSpec((tm, r), lambda i: (i, (T_CKV - tile0) * QTILE_W // r)),
                  pl.BlockSpec((1, r), lambda i: (0, 0)),
                  pl.BlockSpec((r, n), lambda i: (0, 0))],
        out_specs=pl.BlockSpec((tm, n), lambda i: (i, 0)),
        out_shape=jax.ShapeDtypeStruct((m, n), BF16),
        compiler_params=_params(("arbitrary",)),
        name="kv_up",
    )(p, g.reshape(1, r), w)


def _permute_w_ukv(w):
    r = w.shape[0]
    w3 = w.reshape(r, MLA_HEADS, MLA_NOPE + MLA_V)
    return jnp.concatenate([w3[:, :, :MLA_NOPE].reshape(r, -1), w3[:, :, MLA_NOPE:].reshape(r, -1)],
                           axis=1).astype(BF16)


def _mla_kernel(q_ref, kn_ref, kp_ref, v_ref, kcn_ref, kcp_ref, vc_ref, o_ref, kcat, vcat):
    s_len = kn_ref.shape[0]
    n_h = kcat.shape[0]
    for h in range(n_h):
        kcat[h, :s_len, :LANES] = kn_ref[:, h * LANES:(h + 1) * LANES]
        kcat[h, :s_len, LANES:] = kp_ref[...]
        kcat[h, s_len:, :LANES] = kcn_ref[:, h * LANES:(h + 1) * LANES]
        kcat[h, s_len:, LANES:] = kcp_ref[...]
        vcat[h, :s_len, :] = v_ref[:, h * LANES:(h + 1) * LANES]
        vcat[h, s_len:, :] = vc_ref[:, h * LANES:(h + 1) * LANES]

    sub = q_ref.shape[0] // MLA_SUBTILES
    for h in range(n_h):
        for r0 in range(0, q_ref.shape[0], sub):
            s = _dot_nt(q_ref[r0:r0 + sub, h * QTILE_W:(h + 1) * QTILE_W], kcat[h])
            m = jnp.max(s, axis=-1, keepdims=True)
            p = jnp.exp2(s - m)
            l = jnp.sum(p, axis=-1, keepdims=True)
            o = _dot(p.astype(BF16), vcat[h])
            o_ref[r0:r0 + sub, h * LANES:(h + 1) * LANES] = (o / l).astype(o_ref.dtype)


def _mla_call(p_lat, kv_lat, p_ctx, kv_ctx, batch, s_len, lc, heads_per_step=2):
    hp = heads_per_step
    kpe_blk = T_KPE * 2 + 1
    v0 = MLA_HEADS // hp
    return pl.pallas_call(
        _mla_kernel,
        grid=(batch, MLA_HEADS // hp),
        in_specs=[pl.BlockSpec((s_len, hp * QTILE_W), lambda b, h: (b, h)),
                  pl.BlockSpec((s_len, hp * LANES), lambda b, h: (b, h)),
                  pl.BlockSpec((s_len, LANES), lambda b, h: (b, kpe_blk)),
                  pl.BlockSpec((s_len, hp * LANES), lambda b, h: (b, v0 + h)),
                  pl.BlockSpec((lc, hp * LANES), lambda b, h: (b, h)),
                  pl.BlockSpec((lc, LANES), lambda b, h: (b, kpe_blk - 2 * CTX_TILE0)),
                  pl.BlockSpec((lc, hp * LANES), lambda b, h: (b, v0 + h))],
        out_specs=pl.BlockSpec((s_len, hp * LANES), lambda b, h: (b, h)),
        out_shape=jax.ShapeDtypeStruct((batch * s_len, MLA_HEADS * MLA_V), BF16),
        scratch_shapes=[pltpu.VMEM((hp, s_len + lc, QTILE_W), BF16),
                        pltpu.VMEM((hp, s_len + lc, LANES), BF16)],
        compiler_params=_params(("arbitrary", "arbitrary")),
        name="mla_attention",
    )(p_lat, kv_lat, p_lat, kv_lat, kv_ctx, p_ctx, kv_ctx)


def _na_kernel(q_ref, k_ref, v_ref, kc_ref, vc_ref, bias_ref, o_ref, *, rows, var_map):
    win = NA_WR * GRID_W
    tq = NA_G * GRID_W
    for h in range(q_ref.shape[1] // LANES):
        hs = slice(h * LANES, (h + 1) * LANES)
        for g in range(rows // NA_G):
            start = int(np.clip(NA_G * g - NA_KH // 2, 0, rows - NA_WR)) * GRID_W
            q = q_ref[g * tq:(g + 1) * tq, hs]
            sw = _dot_nt(q, k_ref[start:start + win, hs]) + bias_ref[var_map[g], h]
            sc = _dot_nt(q, kc_ref[:, hs])
            m = jnp.maximum(jnp.max(sw, axis=-1, keepdims=True), jnp.max(sc, axis=-1, keepdims=True))
            pw = jnp.exp2(sw - m)
            pc = jnp.exp2(sc - m)
            l = jnp.sum(pw, axis=-1, keepdims=True) + jnp.sum(pc, axis=-1, keepdims=True)
            o = _dot(pw.astype(BF16), v_ref[start:start + win, hs]) + _dot(pc.astype(BF16), vc_ref[:, hs])
            o_ref[g * tq:(g + 1) * tq, hs] = (o / l).astype(o_ref.dtype)


def _na_tables(rows):
    kh = min(NA_KH, rows)
    ng = rows // NA_G
    qr = np.arange(NA_G)[:, None]
    kr = np.arange(NA_WR)[None, :]
    sel_rows = []
    for g in range(ng):
        start_row = int(np.clip(NA_G * g - NA_KH // 2, 0, rows - NA_WR))
        r = NA_G * g + qr
        krow = start_row + kr
        rs = np.clip(r - kh // 2, 0, rows - kh)
        ok = (krow >= rs) & (krow < rs + kh)
        dr = krow - r + (NA_KH - 1)
        sel_rows.append(np.stack([ok & (dr == d) for d in range(2 * NA_KH - 1)]))
    sel_rows = np.stack(sel_rows)
    uniq, inverse = np.unique(sel_rows.reshape(ng, -1), axis=0, return_inverse=True)
    er = uniq.reshape((-1,) + sel_rows.shape[1:]).astype(np.float32)
    qc = np.arange(GRID_W)[:, None]
    kc = np.arange(GRID_W)[None, :]
    cs = np.clip(qc - NA_KW // 2, 0, GRID_W - NA_KW)
    ok_c = (kc >= cs) & (kc < cs + NA_KW)
    dc = np.clip(kc - qc, -(NA_KW - 1), NA_KW - 1) + (NA_KW - 1)
    ec = np.stack([ok_c & (dc == e) for e in range(2 * NA_KW - 1)]).astype(np.float32)
    return inverse.reshape(-1).astype(np.int32), er, ec


def _na_bias(rpb, er, ec):
    hp = lax.Precision.HIGHEST
    t = jnp.einsum('hde,eqk->hdqk', rpb.astype(F32), jnp.asarray(ec), precision=hp)
    t = jnp.where(jnp.asarray(ec.sum(axis=0) > 0.5), t * LOG2E, NEG_INF)
    outside = jnp.full(t[:, 0].shape, NEG_INF, F32)
    d_of = np.where(er.sum(axis=1) > 0.5, er.argmax(axis=1), -1)
    block = lambda d: outside if d < 0 else t[:, d]
    return jnp.stack([
        jnp.concatenate([jnp.concatenate([block(int(d)) for d in row], axis=-1) for row in var], axis=-2)
        for var in d_of])


def _na_call(p_lat, p_ctx, bias, var_map, batch, s_len, lc):
    rows = s_len // GRID_W
    hp = NA_HEADS_PER_STEP
    tw = hp * LANES
    tq = NA_G * GRID_W
    win = NA_WR * GRID_W
    qb, kb, vb = (T_QNA * QTILE_W // tw, T_KNA * QTILE_W // tw, T_VNA * QTILE_W // tw)
    cb = CTX_TILE0 * QTILE_W // tw
    n_var = bias.shape[0]
    return pl.pallas_call(
        functools.partial(_na_kernel, rows=rows, var_map=tuple(int(v) for v in var_map)),
        grid=(batch, NA_HEADS // hp),
        in_specs=[pl.BlockSpec((s_len, tw), lambda b, h: (b, qb + h)),
                  pl.BlockSpec((s_len, tw), lambda b, h: (b, kb + h)),
                  pl.BlockSpec((s_len, tw), lambda b, h: (b, vb + h)),
                  pl.BlockSpec((lc, tw), lambda b, h: (b, kb + h - cb)),
                  pl.BlockSpec((lc, tw), lambda b, h: (b, vb + h - cb)),
                  pl.BlockSpec((n_var, hp, tq, win), lambda b, h: (0, h, 0, 0))],
        out_specs=pl.BlockSpec((s_len, tw), lambda b, h: (b, h)),
        out_shape=jax.ShapeDtypeStruct((batch * s_len, NA_HEADS * NA_HEAD_DIM), BF16),
        compiler_params=_params(("arbitrary", "arbitrary")),
        name="na_attention",
    )(p_lat, p_lat, p_lat, p_ctx, p_ctx, bias)


def _layer_norm(x, g, b):
    mu = jnp.mean(x, axis=-1, keepdims=True)
    xc = x - mu
    var = jnp.mean(xc * xc, axis=-1, keepdims=True)
    return xc * lax.rsqrt(var + LN_EPS) * g + b


def _outln_kernel(a1_ref, a2_ref, w_ref, h_ref, mod_ref, ln_ref, r_ref, h1_ref, u2_ref, lg_ref, wb_ref):
    half = a1_ref.shape[1]

    @pl.when(pl.program_id(0) == 0)
    def _():
        rows = wb_ref.shape[0] // 4
        for r0 in range(0, wb_ref.shape[0], rows):
            wb_ref[r0:r0 + rows, :] = w_ref[0, r0:r0 + rows, :].astype(wb_ref.dtype)

    sub = a1_ref.shape[0] // OUTLN_SUBTILES
    for r0 in range(0, a1_ref.shape[0], sub):
        rs = slice(r0, r0 + sub)
        y = _dot(a1_ref[rs, :], wb_ref[:half, :]) + _dot(a2_ref[rs, :], wb_ref[half:, :])
        x = DEEPNORM_ALPHA * h_ref[rs, :] + mod_ref[0, 0:1, :] * y
        hn = _layer_norm(x, ln_ref[0:1, :], ln_ref[1:2, :])
        h1_ref[rs, :] = hn
        u = hn * (1.0 + mod_ref[0, 2:3, :]) + mod_ref[0, 1:2, :]
        u_bf = u.astype(BF16)
        u2_ref[rs, :] = u_bf
        lg_ref[rs, :] = _dot(u_bf, r_ref[...])


def _outln_call(a1, a2, w, layer, h, mod, ln, router2, rows_per_group, tm=512):
    m, d = h.shape
    half = a1.shape[1]
    tpg = rows_per_group // tm
    return pl.pallas_call(
        _outln_kernel,
        grid=(m // tm,),
        in_specs=[pl.BlockSpec((tm, half), lambda i: (i, 0)),
                  pl.BlockSpec((tm, half), lambda i: (i, 0)),
                  pl.BlockSpec((1, 2 * half, d), lambda i: (layer, 0, 0), pipeline_mode=pl.Buffered(1)),
                  pl.BlockSpec((tm, d), lambda i: (i, 0)),
                  pl.BlockSpec((1, 3, d), lambda i: (i // tpg, 0, 0)),
                  pl.BlockSpec((2, d), lambda i: (0, 0)),
                  _resident_spec((d, LANES))],
        out_specs=[pl.BlockSpec((tm, d), lambda i: (i, 0)),
                   pl.BlockSpec((tm, d), lambda i: (i, 0)),
                   pl.BlockSpec((tm, LANES), lambda i: (i, 0))],
        out_shape=[jax.ShapeDtypeStruct((m, d), F32),
                   jax.ShapeDtypeStruct((m, d), BF16),
                   jax.ShapeDtypeStruct((m, LANES), F32)],
        scratch_shapes=[pltpu.VMEM((2 * half, d), BF16)],
        compiler_params=_params(("arbitrary",)),
        name="out_proj_postnorm",
    )(a1, a2, w, h, mod, ln, router2)


def _router_operand(router):
    d, e = router.shape
    return jnp.pad(router, ((0, 0), (0, LANES - e))).astype(BF16)


def _route_kernel(lg_ref, slot_ref, aff_ref, cnt_ref, tri_ref, *, cap):
    n_tok = lg_ref.shape[2]

    @pl.when(pl.program_id(0) == 0)
    def _():
        chunk = 256
        for r0 in range(0, n_tok, chunk):
            r = r0 + lax.broadcasted_iota(I32, (chunk, n_tok), 0)
            c = lax.broadcasted_iota(I32, (chunk, n_tok), 1)
            tri_ref[r0:r0 + chunk, :] = jnp.where(r < c, 1.0, 0.0).astype(BF16)

    lg = lg_ref[0]
    ex = jnp.exp(lg - jnp.max(lg, axis=0, keepdims=True))
    aff = ex / jnp.sum(ex, axis=0, keepdims=True)
    bits = lax.bitcast_convert_type(aff, I32)
    n_e = lg.shape[0]
    count = lambda mask: jnp.sum(jnp.where(mask, 1.0, 0.0), axis=1, keepdims=True)

    def body(_, lohi):
        lo, hi = lohi
        mid = lo + jnp.right_shift(hi - lo, 1)
        ge = count(bits >= mid) >= cap
        return jnp.where(ge, mid, lo), jnp.where(ge, hi, mid)

    lo0 = jnp.zeros((n_e, 1), I32)
    hi0 = jnp.full((n_e, 1), 0x7F800000, I32)
    thr, _ = lax.fori_loop(0, 31, body, (lo0, hi0))
    gt = bits > thr
    eq = bits == thr
    need = cap - count(gt)
    pre_eq = _dot(jnp.where(eq, 1.0, 0.0).astype(BF16), tri_ref[...])
    sel = jnp.logical_or(gt, jnp.logical_and(eq, pre_eq < need))
    sel_bf = jnp.where(sel, 1.0, 0.0).astype(BF16)
    slot = _dot(sel_bf, tri_ref[...])
    slot_ref[0] = jnp.where(sel, slot.astype(I32), -1)
    aff_ref[0] = aff
    r = lax.broadcasted_iota(I32, (n_tok, LANES), 0)
    c = lax.broadcasted_iota(I32, (n_tok, LANES), 1)
    before = jnp.where(r < c * MOE_CHUNK, 1.0, 0.0).astype(BF16)
    cnt_ref[0] = _dot(sel_bf, before).astype(I32)


def _route_call(lg_t, cap):
    b, e, n_tok = lg_t.shape
    spec = pl.BlockSpec((1, e, n_tok), lambda i: (i, 0, 0))
    return pl.pallas_call(
        functools.partial(_route_kernel, cap=cap),
        grid=(b,),
        in_specs=[spec],
        out_specs=[spec, spec, pl.BlockSpec((1, e, LANES), lambda i: (i, 0, 0))],
        out_shape=[jax.ShapeDtypeStruct((b, e, n_tok), I32), jax.ShapeDtypeStruct((b, e, n_tok), F32),
                   jax.ShapeDtypeStruct((b, e, LANES), I32)],
        scratch_shapes=[pltpu.VMEM((n_tok, n_tok), BF16)],
        compiler_params=_params(("arbitrary",)),
        name="ec_route",
    )(lg_t)


def _slot_windows(cnt, cap, n_tok):
    bounds = cnt[:, :, :n_tok // MOE_CHUNK + 1]
    start, end = bounds[:, :, :-1], bounds[:, :, 1:]
    starts, tier = [], jnp.full(start[:, 0].shape, len(MOE_WINS), I32)
    for i, win in reversed(list(enumerate(MOE_WINS))):
        a = jnp.minimum((start // 16) * 16, cap - win)
        tier = jnp.where(jnp.all(end <= a + win, axis=1), i, tier)
        starts.insert(0, jnp.swapaxes(a, 1, 2).reshape(-1))
    return jnp.concatenate(starts).astype(I32), tier.reshape(-1).astype(I32)


def _gather_kernel(win_ref, tier_ref, slot_ref, u_ref, o_ref, *, chunk):
    b, ks = pl.program_id(0), pl.program_id(1)
    n_e, cap = o_ref.shape[0], o_ref.shape[1]
    n_sub = u_ref.shape[0] // chunk
    step = b * pl.num_programs(1) + ks

    @pl.when(ks == 0)
    def _():
        o_ref[...] = jnp.zeros(o_ref.shape, o_ref.dtype)

    def place(rows, start_of, group, tok0, n_t):
        row = lax.broadcasted_iota(I32, (rows, n_t), 0)
        for g0 in range(0, n_e, group):
            starts = [start_of(e) for e in range(g0, g0 + group)]
            onehot = jnp.concatenate(
                [jnp.where(row + st == slot_ref[0, e:e + 1, tok0:tok0 + n_t], 1.0, 0.0).astype(BF16)
                 for st, e in zip(starts, range(g0, g0 + group))], axis=0)
            got = _dot(onehot, u_ref[tok0:tok0 + n_t, :]).astype(o_ref.dtype)
            for i, (st, e) in enumerate(zip(starts, range(g0, g0 + group))):
                o_ref[e, pl.ds(st, rows), :] += got[i * rows:(i + 1) * rows]

    n_chunks = pl.num_programs(0) * pl.num_programs(1) * n_sub
    for sub in range(n_sub):
        c = step * n_sub + sub
        for i, win in enumerate(MOE_WINS):
            @pl.when(tier_ref[c] == i)
            def _(c=c, sub=sub, i=i, win=win):
                base = (i * n_chunks + c) * n_e
                place(win, lambda e: pl.multiple_of(win_ref[base + e], 16), 512 // win, sub * chunk, chunk)

        @pl.when(tier_ref[c] == len(MOE_WINS))
        def _(sub=sub):
            place(cap, lambda e: 0, 2, sub * chunk, chunk)


def _gather_call(starts, tier, slot, u2, cap, chunks_per_step=4):
    b, e, n_tok = slot.shape
    d = u2.shape[1]
    chunk = MOE_CHUNK
    tt = chunk * chunks_per_step
    nc = n_tok // tt
    grid_spec = pltpu.PrefetchScalarGridSpec(
        num_scalar_prefetch=2,
        grid=(b, nc),
        in_specs=[pl.BlockSpec((1, e, tt), lambda i, k, w, f: (i, 0, k)),
                  pl.BlockSpec((tt, d), lambda i, k, w, f: (i * nc + k, 0))],
        out_specs=pl.BlockSpec((e, cap, d), lambda i, k, w, f: (0, i, 0)),
    )
    return pl.pallas_call(
        functools.partial(_gather_kernel, chunk=chunk),
        grid_spec=grid_spec,
        out_shape=jax.ShapeDtypeStruct((e, b * cap, d), BF16),
        compiler_params=_params(("arbitrary", "arbitrary")),
        name="moe_gather",
    )(starts, tier, slot, u2)


def _expert_hidden_kernel(x_ref, w1_ref, w3_ref, o_ref, *, ff):
    tf = w1_ref.shape[3]
    last = pl.num_programs(1) - 1

    def block(width):
        w = jnp.concatenate([w1_ref[0, 0, :, :width].astype(BF16), w3_ref[0, 0, :, :width].astype(BF16)], axis=1)
        h = _dot(x_ref[0], w)
        h1 = h[:, :width]
        o_ref[0, :, :width] = ((h1 / (1.0 + jnp.exp(-h1))) * h[:, width:]).astype(o_ref.dtype)

    tail = ff - (pl.cdiv(ff, tf) - 1) * tf
    if tail == tf:
        block(tf)
    else:
        pl.when(pl.program_id(1) < last)(lambda: block(tf))
        pl.when(pl.program_id(1) == last)(lambda: block(tail))


def _expert_out_kernel(hd_ref, w2_ref, o_ref):
    y = _dot(hd_ref[0], w2_ref[0, 0].astype(BF16)).astype(o_ref.dtype)
    cap = o_ref.shape[1]
    for b in range(o_ref.shape[0]):
        o_ref[b] = y[b * cap:(b + 1) * cap]


def _expert_call(xe, w1, w3, w2, layer, batch, tf=768, tn=2048):
    e, m, d = xe.shape
    cap = m // batch
    ff = w1.shape[3]
    hidden = pl.pallas_call(
        functools.partial(_expert_hidden_kernel, ff=ff),
        grid=(e, pl.cdiv(ff, tf)),
        in_specs=[pl.BlockSpec((1, m, d), lambda i, f: (i, 0, 0)),
                  pl.BlockSpec((1, 1, d, tf), lambda i, f: (layer, i, 0, f)),
                  pl.BlockSpec((1, 1, d, tf), lambda i, f: (layer, i, 0, f))],
        out_specs=pl.BlockSpec((1, m, tf), lambda i, f: (i, 0, f)),
        out_shape=jax.ShapeDtypeStruct((e, m, ff), BF16),
        compiler_params=_params(("arbitrary", "arbitrary")),
        name="moe_expert_hidden",
    )(xe, w1, w3)
    return pl.pallas_call(
        _expert_out_kernel,
        grid=(e, d // tn),
        in_specs=[pl.BlockSpec((1, m, ff), lambda i, j: (i, 0, 0)),
                  pl.BlockSpec((1, 1, ff, tn), lambda i, j: (layer, i, 0, j))],
        out_specs=pl.BlockSpec((batch, cap, tn), lambda i, j: (0, i, j)),
        out_shape=jax.ShapeDtypeStruct((batch, e * cap, d), BF16),
        compiler_params=_params(("arbitrary", "arbitrary")),
        name="moe_expert_out",
    )(hidden, w2)


def _scatter_kernel(win_ref, tier_ref, slot_ref, aff_ref, ye_ref, h_ref, mod_ref, ln_ref, *rest, with_next):
    if with_next:
        nmod_ref, h2_ref, un_ref, pt_ref, yew_ref, moe_ref = rest
    else:
        h2_ref, pt_ref, yew_ref, moe_ref = rest
    n_e = slot_ref.shape[1]
    cap = ye_ref.shape[1] // n_e
    tt = h_ref.shape[0]
    step = pl.program_id(0) * pl.num_programs(1) + pl.program_id(1)

    def gates(rows, e, start):
        row = lax.broadcasted_iota(I32, (rows, tt), 0)
        return jnp.where(row + start == slot_ref[0, e:e + 1, :], aff_ref[0, e:e + 1, :], 0.0).astype(BF16)

    n_chunks = pl.num_programs(0) * pl.num_programs(1)
    for i, win in enumerate(MOE_WINS):
        @pl.when(tier_ref[step] == i)
        def _(i=i, win=win):
            for e in range(n_e):
                start = pl.multiple_of(win_ref[(i * n_chunks + step) * n_e + e], 16)
                pt_ref[e * win:(e + 1) * win, :] = gates(win, e, start)
                yew_ref[e * win:(e + 1) * win, :] = ye_ref[0, pl.ds(e * cap + start, win), :]
            moe_ref[...] = _dot_tn(pt_ref[:n_e * win, :], yew_ref[:n_e * win, :])

    @pl.when(tier_ref[step] == len(MOE_WINS))
    def _():
        for e in range(n_e):
            pt_ref[e * cap:(e + 1) * cap, :] = gates(cap, e, 0)
        moe_ref[...] = _dot_tn(pt_ref[...], ye_ref[0])

    x = DEEPNORM_ALPHA * h_ref[...] + mod_ref[0] * moe_ref[...]
    hn = _layer_norm(x, ln_ref[0:1, :], ln_ref[1:2, :])
    h2_ref[...] = hn
    if with_next:
        un_ref[...] = (hn * (1.0 + nmod_ref[0, 1:2, :]) + nmod_ref[0, 0:1, :]).astype(un_ref.dtype)


def _scatter_call(starts, tier, slot, aff, ye, h1, gate, ln, next_ss):
    b, e, n_tok = slot.shape
    m, d = h1.shape
    tt = MOE_CHUNK
    nt = n_tok // tt
    rows = ye.shape[1]
    with_next = next_ss is not None
    tok_spec = pl.BlockSpec((1, e, tt), lambda i, t, w, f: (i, 0, t))
    row_spec = pl.BlockSpec((tt, d), lambda i, t, w, f: (i * nt + t, 0))
    in_specs = [tok_spec, tok_spec,
                pl.BlockSpec((1, rows, d), lambda i, t, w, f: (i, 0, 0), pipeline_mode=pl.Buffered(1)),
                row_spec,
                pl.BlockSpec((1, 1, d), lambda i, t, w, f: (i, 0, 0)),
                pl.BlockSpec((2, d), lambda i, t, w, f: (0, 0))]
    args = [slot, aff, ye, h1, gate, ln]
    out_specs = [row_spec]
    out_shape = [jax.ShapeDtypeStruct((m, d), F32)]
    if with_next:
        in_specs.append(pl.BlockSpec((1, 2, d), lambda i, t, w, f: (i, 0, 0)))
        args.append(next_ss)
        out_specs.append(row_spec)
        out_shape.append(jax.ShapeDtypeStruct((m, d), BF16))
    grid_spec = pltpu.PrefetchScalarGridSpec(
        num_scalar_prefetch=2,
        grid=(b, nt),
        in_specs=in_specs,
        out_specs=out_specs,
        scratch_shapes=[pltpu.VMEM((rows, tt), BF16),
                        pltpu.VMEM((e * max(MOE_WINS), d), BF16),
                        pltpu.VMEM((tt, d), F32)],
    )
    res = pl.pallas_call(
        functools.partial(_scatter_kernel, with_next=with_next),
        grid_spec=grid_spec,
        out_shape=out_shape,
        compiler_params=_params(("arbitrary", "arbitrary")),
        name="moe_combine_postnorm",
    )(starts, tier, *args)
    return res if with_next else (res[0], None)


def _ec_moe(h1, u2, logits, w1, w3, w2, layer, gate, ln, next_ss, batch, n_tok):
    cap = EC_CAPACITY_FACTOR * n_tok // N_EXPERTS
    lg_t = jnp.swapaxes(logits[:, :N_EXPERTS].reshape(batch, n_tok, N_EXPERTS), 1, 2)
    slot, aff, cnt = _route_call(lg_t, cap)
    starts, tier = _slot_windows(cnt, cap, n_tok)
    xe = _gather_call(starts, tier, slot, u2, cap)
    ye = _expert_call(xe, w1, w3, w2, layer, batch)
    return _scatter_call(starts, tier, slot, aff, ye, h1, gate, ln, next_ss)


def _mm_kernel(a_ref, w_ref, o_ref, wb_ref):
    @pl.when(pl.program_id(1) == 0)
    def _():
        wb_ref[...] = w_ref[0].astype(wb_ref.dtype)

    o_ref[...] = _dot(a_ref[...], wb_ref[...]).astype(o_ref.dtype)


def _mm_call(a, w, layer, out_dtype, tm=1024, tn=1024):
    m, k = a.shape
    n = w.shape[2]
    return pl.pallas_call(
        _mm_kernel,
        grid=(n // tn, m // tm),
        in_specs=[pl.BlockSpec((tm, k), lambda j, i: (i, 0)),
                  pl.BlockSpec((1, k, tn), lambda j, i: (layer, 0, j))],
        out_specs=pl.BlockSpec((tm, tn), lambda j, i: (i, j)),
        out_shape=jax.ShapeDtypeStruct((m, n), out_dtype),
        scratch_shapes=[pltpu.VMEM((k, tn), BF16)],
        compiler_params=_params(("arbitrary", "arbitrary")),
        name="matmul",
    )(a, w)


def _short_conv(p, w, b):
    x = p.astype(F32)
    n = x.shape[0]
    r = lax.broadcasted_iota(I32, x.shape, 0)
    prev = jnp.where(r == 0, 0.0, pltpu.roll(x, 1, 0))
    nxt = jnp.where(r == n - 1, 0.0, pltpu.roll(x, n - 1, 0))
    return b + w[0:1, :] * prev + w[1:2, :] * x + w[2:3, :] * nxt


def _sconv_kernel(p_ref, w_ref, b_ref, o_ref):
    o_ref[...] = _short_conv(p_ref[...], w_ref[...], b_ref[...]).astype(o_ref.dtype)


def _sconv_call(p, conv_w, conv_b, col0, ncols, out_dtype, batch, n_tok, tc=1024):
    c0 = col0 // tc
    return pl.pallas_call(
        _sconv_kernel,
        grid=(batch, ncols // tc),
        in_specs=[pl.BlockSpec((n_tok, tc), lambda b, j: (b, c0 + j)),
                  pl.BlockSpec((HY_SHORT, tc), lambda b, j: (0, c0 + j)),
                  pl.BlockSpec((1, tc), lambda b, j: (0, c0 + j))],
        out_specs=pl.BlockSpec((n_tok, tc), lambda b, j: (b, j)),
        out_shape=jax.ShapeDtypeStruct((batch * n_tok, ncols), out_dtype),
        compiler_params=_params(("arbitrary", "arbitrary")),
        name="hyena_short_conv",
    )(p, conv_w, conv_b.reshape(1, -1))


def _dft_fwd_kernel(fc_ref, fs_ref, r1_ref, r2_ref, *rest, spectral):
    o_ref = rest[-1]
    sub = fc_ref.shape[0] // SEQMIX_SUBTILES
    for r0 in range(0, fc_ref.shape[0], sub):
        rs = slice(r0, r0 + sub)
        xr = _dot(fc_ref[rs, :], r1_ref[...])
        xi = _dot(fs_ref[rs, :], r2_ref[...])
        if spectral:
            h_ref, nyq_ref = rest[:2]
            hr = h_ref[0, 0, rs, :]
            hi = h_ref[0, 1, rs, :]
            if r0 == 0:
                dc = lax.broadcasted_iota(I32, xr.shape, 0) == 0
                yr = xr * hr - jnp.where(dc, 0.0, xi * hi)
                yi = jnp.where(dc, xi * nyq_ref[...], xr * hi + xi * hr)
            else:
                yr = xr * hr - xi * hi
                yi = xr * hi + xi * hr
        else:
            yr, yi = xr, xi
        o_ref[0, 0, rs, :] = yr.astype(o_ref.dtype)
        o_ref[0, 1, rs, :] = yi.astype(o_ref.dtype)


def _resident_spec(shape):
    return pl.BlockSpec(shape, lambda *_: (0,) * len(shape), pipeline_mode=pl.Buffered(1))


def _dft_fwd_call(fc, fs, r1, r2, spec, out_dtype, batch, n_tok, tn=256):
    ncols = r1.shape[1]
    r_spec = pl.BlockSpec((n_tok, tn), lambda j, b: (b, j))
    in_specs = [_resident_spec(fc.shape), _resident_spec(fs.shape), r_spec, r_spec]
    args = [fc, fs, r1, r2]
    if spec is not None:
        h, nyq, hcol0 = spec
        c0 = hcol0 // tn
        in_specs += [pl.BlockSpec((1, 2, n_tok, tn), lambda j, b: (0, 0, 0, c0 + j)),
                     pl.BlockSpec((1, tn), lambda j, b: (0, c0 + j))]
        args += [h, nyq]
    return pl.pallas_call(
        functools.partial(_dft_fwd_kernel, spectral=spec is not None),
        grid=(ncols // tn, batch),
        in_specs=in_specs,
        out_specs=pl.BlockSpec((1, 2, n_tok, tn), lambda j, b: (b, 0, 0, j)),
        out_shape=jax.ShapeDtypeStruct((batch, 2, n_tok, ncols), out_dtype),
        compiler_params=_params(("arbitrary", "arbitrary")),
        name="dft_forward",
    )(*args)


def _dual_kernel(gc_ref, gs_ref, y_ref, p_ref, cw_ref, cb_ref, z_ref, skip_ref, o_ref):
    yr, yi = y_ref[0, 0], y_ref[0, 1]
    gate = _short_conv(p_ref[...], cw_ref[...], cb_ref[...])
    sub = o_ref.shape[0] // SEQMIX_SUBTILES
    for r0 in range(0, o_ref.shape[0], sub):
        rs = slice(r0, r0 + sub)
        y = _dot(gc_ref[rs, :], yr) + _dot(gs_ref[rs, :], yi)
        o_ref[rs, :] = (gate[rs] * (y + skip_ref[...] * z_ref[rs, :].astype(F32))).astype(o_ref.dtype)


def _dual_call(gc, gs, y_spec, p, conv_w, conv_b, pcol0, z, skip, batch, n_tok, ncols, tn=256):
    c0 = pcol0 // tn
    o_spec = pl.BlockSpec((n_tok, tn), lambda b, j: (b, j))
    return pl.pallas_call(
        _dual_kernel,
        grid=(batch, ncols // tn),
        in_specs=[_resident_spec(gc.shape), _resident_spec(gs.shape),
                  pl.BlockSpec((1, 2, n_tok, tn), lambda b, j: (b, 0, 0, j)),
                  pl.BlockSpec((n_tok, tn), lambda b, j: (b, c0 + j)),
                  pl.BlockSpec((HY_SHORT, tn), lambda b, j: (0, c0 + j)),
                  pl.BlockSpec((1, tn), lambda b, j: (0, c0 + j)),
                  o_spec,
                  pl.BlockSpec((1, tn), lambda b, j: (0, j))],
        out_specs=o_spec,
        out_shape=jax.ShapeDtypeStruct((batch * n_tok, ncols), BF16),
        compiler_params=_params(("arbitrary", "arbitrary")),
        name="seq_mix_matmul",
    )(gc, gs, y_spec, p, conv_w, conv_b.reshape(1, -1), z, skip)


def _fnet_kernel(cl_ref, msl_ref, x_ref, cw_ref, o_ref):
    gw = cw_ref.shape[1]
    for c0 in range(0, x_ref.shape[1], gw):
        x = x_ref[:, c0:c0 + gw]
        seq = jnp.concatenate([_dot(cl_ref[...], x).astype(BF16), _dot(msl_ref[...], x).astype(BF16)], axis=1)
        o_ref[:, c0:c0 + gw] = _dot(seq, cw_ref[...]).astype(o_ref.dtype)


def _fnet_call(cl, msl, p, cw, col0, batch, n_tok, groups_per_step=2):
    gw = cw.shape[1]
    tw = gw * groups_per_step
    c0 = col0 // tw
    return pl.pallas_call(
        _fnet_kernel,
        grid=(batch, FN_GROUPS // groups_per_step),
        in_specs=[_resident_spec(cl.shape), _resident_spec(msl.shape),
                  pl.BlockSpec((n_tok, tw), lambda b, g: (b, c0 + g)),
                  pl.BlockSpec(cw.shape, lambda b, g: (0, 0))],
        out_specs=pl.BlockSpec((n_tok, tw), lambda b, g: (b, g)),
        out_shape=jax.ShapeDtypeStruct((batch * n_tok, FN_GROUPS * gw), BF16),
        compiler_params=_params(("arbitrary", "arbitrary")),
        name="fnet_mix",
    )(cl, msl, p, cw)


def _cos_sin_matrix(n_rows, n_cols, period, split=64):
    r = np.arange(n_rows, dtype=np.int64)[:, None]
    c_hi = (np.arange(n_cols // split, dtype=np.int64) * split)[None, :]
    c_lo = np.arange(split, dtype=np.int64)[None, :]
    ang = lambda c: 2.0 * np.pi * ((r * c) % period).astype(np.float64) / period
    ca, sa = jnp.asarray(np.cos(ang(c_hi)), F32), jnp.asarray(np.sin(ang(c_hi)), F32)
    cb, sb = jnp.asarray(np.cos(ang(c_lo)), F32), jnp.asarray(np.sin(ang(c_lo)), F32)
    cos = ca[:, :, None] * cb[:, None, :] - sa[:, :, None] * sb[:, None, :]
    sin = sa[:, :, None] * cb[:, None, :] + ca[:, :, None] * sb[:, None, :]
    return cos.reshape(n_rows, n_cols), sin.reshape(n_rows, n_cols)


def _trig_kernel(ta_ref, tb_ref, ea_ref, eb_ref, *o_refs, mode, period, scale):
    sa_ca = _dot(ta_ref[...], ea_ref[...])
    sb_cb = _dot(tb_ref[...], eb_ref[...])
    nc = sa_ca.shape[1] // 2
    ca, sa, cb, sb = sa_ca[:, :nc], sa_ca[:, nc:], sb_cb[:, :nc], sb_cb[:, nc:]
    cos = ca * cb - sa * sb
    sin = sa * cb + ca * sb
    tm = cos.shape[0]
    row = pl.program_id(0) * tm + lax.broadcasted_iota(I32, cos.shape, 0)
    col = lax.broadcasted_iota(I32, cos.shape, 1)
    alt = lambda idx: (1 - 2 * jnp.bitwise_and(idx, 1)).astype(F32)
    if mode == "hyena":
        fc_ref, fs_ref, gc_ref, gs_ref = o_refs
        fc_ref[...] = cos.astype(fc_ref.dtype)
        fs_ref[...] = jnp.where(row == 0, alt(col), -sin).astype(fs_ref.dtype)
        gc_ref[...] = (cos * jnp.where(col == 0, 1.0 / period, 2.0 / period)).astype(gc_ref.dtype)
        gs_ref[...] = jnp.where(col == 0, alt(row) * (1.0 / period), sin * (-2.0 / period)).astype(gs_ref.dtype)
    else:
        c_ref, ms_ref = o_refs
        c_ref[...] = (cos * scale).astype(c_ref.dtype)
        ms_ref[...] = (sin * (-scale)).astype(ms_ref.dtype)


def _split3(x):
    bf = jnp.bfloat16
    x = np.asarray(x, np.float32)
    hi = x.astype(bf)
    r1 = x - hi.astype(np.float32)
    mid = r1.astype(bf)
    lo = (r1 - mid.astype(np.float32)).astype(bf)
    return [hi, mid, lo]


def _trig_call(n, period, mode, scale=1.0, split=64, tm=256):
    r = np.arange(n, dtype=np.int64)[:, None]
    c_hi = (np.arange(n // split, dtype=np.int64) * split)[None, :]
    c_lo = np.arange(split, dtype=np.int64)[None, :]
    ang = lambda c: 2.0 * np.pi * ((r * c) % period).astype(np.float64) / period
    def operands(a, col_group):
        terms = _split3(np.cos(a)) + _split3(np.sin(a))
        k = a.shape[1]
        spread = (col_group[None, :] == np.arange(k)[:, None]).astype(np.float32)
        zero = np.zeros_like(spread)
        e = np.concatenate([np.concatenate([spread, zero], 1)] * 3 + [np.concatenate([zero, spread], 1)] * 3, 0)
        return jnp.asarray(np.concatenate(terms, axis=1)), jnp.asarray(e, BF16)
    cols = np.arange(n)
    ta, ea = operands(ang(c_hi), cols // split)
    tb, eb = operands(ang(c_lo), cols % split)
    n_out = 4 if mode == "hyena" else 2
    o_spec = pl.BlockSpec((tm, n), lambda i: (i, 0))
    return pl.pallas_call(
        functools.partial(_trig_kernel, mode=mode, period=period, scale=scale),
        grid=(n // tm,),
        in_specs=[pl.BlockSpec((tm, ta.shape[1]), lambda i: (i, 0)),
                  pl.BlockSpec((tm, tb.shape[1]), lambda i: (i, 0)),
                  pl.BlockSpec(ea.shape, lambda i: (0, 0)),
                  pl.BlockSpec(eb.shape, lambda i: (0, 0))],
        out_specs=[o_spec] * n_out,
        out_shape=[jax.ShapeDtypeStruct((n, n), BF16)] * n_out,
        compiler_params=_params(("arbitrary",)),
        name="dft_tables",
    )(ta, tb, ea, eb)


def _hyena_dft_operands(n_tok):
    return _trig_call(n_tok, 2 * n_tok, "hyena")


def _fnet_operands(n_tok, gw):
    cl, msl = _trig_call(n_tok, n_tok, "fnet", scale=1.0 / math.sqrt(n_tok * gw))
    cw, sw = _cos_sin_matrix(gw, gw, gw)
    return cl, msl, jnp.concatenate([cw, sw], axis=0).astype(BF16)


def _filter_kernel(z_ref, w1_ref, b1_ref, f1_ref, w2_ref, b2_ref, f2_ref, w3f_ref, w3b_ref, dl_ref,
                   sum_ref, dif_ref, nyq_ref, hdn_ref):
    @pl.when(jnp.logical_and(pl.program_id(0) == 0, pl.program_id(1) == 0))
    def _():
        hdot = lambda a, b: jnp.dot(a, b, precision=lax.Precision.HIGHEST, preferred_element_type=F32)
        hdn = jnp.sin(f1_ref[...] * (hdot(z_ref[...], w1_ref[...]) + b1_ref[...]))
        hdn = jnp.sin(f2_ref[...] * (hdot(hdn, w2_ref[...]) + b2_ref[...]))
        hdn_ref[...] = hdn.astype(hdn_ref.dtype)

    n = hdn_ref.shape[0]
    row = lax.broadcasted_iota(I32, (n, dl_ref.shape[1]), 0)
    decay = jnp.exp(-(row.astype(F32) / (n - 1)) * dl_ref[...])
    hf = _dot(hdn_ref[...], w3f_ref[...].astype(BF16)) * decay
    hb = jnp.where(row == 0, 0.0, _dot(hdn_ref[...], w3b_ref[...].astype(BF16)) * decay)
    tot = hf + hb
    sum_ref[...] = tot.astype(sum_ref.dtype)
    dif_ref[...] = (hf - hb).astype(dif_ref.dtype)
    nyq_ref[...] = jnp.sum(jnp.where(jnp.bitwise_and(row, 1) == 0, tot, -tot), axis=0, keepdims=True)


def _filter_call(n_tok, fw1, fb1, ff1, fw2, fb2, ff2, fw3, tc=512):
    t01 = np.linspace(0.0, 1.0, n_tok, dtype=np.float32)
    w = (2.0 * math.pi * np.arange(n_tok, dtype=np.float32) / n_tok).astype(np.float32)
    bands = np.linspace(1e-4, HY_BANDS - 1, HY_BANDS, dtype=np.float32)
    z = np.concatenate([t01[:, None], np.cos(w[:, None] * bands), -np.sin(w[:, None] * bands)], -1)
    deltas = np.abs(np.linspace(HY_MIN_DECAY, HY_MAX_DECAY, HY_W, dtype=np.float32))[None, :]
    emb, hid = fw1.shape
    nc = HY_W // tc
    full = lambda shape: pl.BlockSpec(shape, lambda o, j: (0,) * len(shape))
    o_spec = pl.BlockSpec((n_tok, tc), lambda o, j: (0, o * nc + j))
    return pl.pallas_call(
        _filter_kernel,
        grid=(HY_ORDER, nc),
        in_specs=[full((n_tok, emb)), full((emb, hid)), full((1, hid)), full((1, hid)),
                  full((hid, hid)), full((1, hid)), full((1, hid)),
                  pl.BlockSpec((hid, tc), lambda o, j: (0, (2 * o) * nc + j)),
                  pl.BlockSpec((hid, tc), lambda o, j: (0, (2 * o + 1) * nc + j)),
                  pl.BlockSpec((1, tc), lambda o, j: (0, j))],
        out_specs=[o_spec, o_spec, pl.BlockSpec((1, tc), lambda o, j: (0, o * nc + j))],
        out_shape=[jax.ShapeDtypeStruct((n_tok, HY_ORDER * HY_W), BF16),
                   jax.ShapeDtypeStruct((n_tok, HY_ORDER * HY_W), BF16),
                   jax.ShapeDtypeStruct((1, HY_ORDER * HY_W), F32)],
        scratch_shapes=[pltpu.VMEM((n_tok, hid), BF16)],
        compiler_params=_params(("arbitrary", "arbitrary")),
        name="hyena_filters",
    )(jnp.asarray(z.astype(np.float32)), fw1, fb1.reshape(1, hid), ff1.reshape(1, hid),
      fw2, fb2.reshape(1, hid), ff2.reshape(1, hid), fw3, fw3, jnp.asarray(deltas))


def kernel(x, c, ctx, c_ctx, ada_w, ada_b, ln1_g, ln1_b, ln2_g, ln2_b, router, exp_w1, exp_w3, exp_w2,
           ab_w_in, ab_kv_norm, ab_w_ukv, ab_rpb, ab_w_out,
           cd_w_in, cd_conv_w, cd_conv_b, cd_filt_w1, cd_filt_b1, cd_filt_freq1, cd_filt_w2, cd_filt_b2,
           cd_filt_freq2, cd_filt_w3, cd_skip, cd_w_out):
    batch, n_tok, d = x.shape
    lc = ctx.shape[1]
    x2d = x.reshape(batch * n_tok, d)
    ctx2d = ctx.reshape(batch * lc, d)

    pad_rows = (-(batch + 1)) % 8
    cv = jnp.concatenate([c, c_ctx[None, :], jnp.zeros((pad_rows, d), F32)], axis=0)
    ada = _ada_call(cv, ada_w, ada_b)
    mods = [ada[i, :batch].reshape(batch, 6, d) for i in range(DEPTH)]
    ctx_ss = ada[0, batch, :2 * d].reshape(1, 2, d)

    w_in = _w_in_prep_call(jnp.swapaxes(ab_w_in, 1, 2), 0)
    p_lat = _mm_rope_call(x2d, mods[0][:, 0:2], n_tok, w_in, _rope_tables(n_tok), tm=1024)
    p_ctx = _mm_rope_call(ctx2d, ctx_ss, batch * lc, w_in, _identity_rope_tables(batch * lc), tm=batch * lc,
                          tile0=CTX_TILE0)
    w_ukv = _permute_w_ukv(ab_w_ukv[0])
    kv_lat = _kvup_call(p_lat, ab_kv_norm[0], w_ukv, tm=2048)
    kv_ctx = _kvup_call(p_ctx, ab_kv_norm[0], w_ukv, tm=batch * lc, tile0=CTX_TILE0)
    a_out = _mla_call(p_lat, kv_lat, p_ctx, kv_ctx, batch, n_tok, lc)
    var_map, na_er, na_ec = _na_tables(n_tok // GRID_W)
    b_out = _na_call(p_lat, p_ctx, _na_bias(ab_rpb[0], na_er, na_ec), var_map, batch, n_tok, lc)
    ln1 = jnp.stack([ln1_g, ln1_b], axis=1)
    ln2 = jnp.stack([ln2_g, ln2_b], axis=1)
    h1, u2, logits = _outln_call(a_out, b_out, ab_w_out, 0, x2d,
                                 mods[0][:, 2:5], ln1[0], _router_operand(router[0]), n_tok)
    h, u = _ec_moe(h1, u2, logits, exp_w1, exp_w3, exp_w2, 0, mods[0][:, 5:6], ln2[0],
                   mods[1][:, 0:2], batch, n_tok)

    p1 = _mm_call(u, cd_w_in, 0, BF16)
    s0 = _sconv_call(p1, cd_conv_w[0], cd_conv_b[0], 0, HY_W, BF16, batch, n_tok)
    h_sum, h_dif, h_nyq = _filter_call(n_tok, cd_filt_w1[0], cd_filt_b1[0], cd_filt_freq1[0], cd_filt_w2[0],
                                       cd_filt_b2[0], cd_filt_freq2[0], cd_filt_w3[0])
    fc, fs, gc, gs = _hyena_dft_operands(n_tok)
    h_spec = _dft_fwd_call(fc, fs, h_sum, h_dif, None, F32, 1, n_tok)
    z = s0
    for o in range(HY_ORDER):
        y_spec = _dft_fwd_call(fc, fs, z, z, (h_spec, h_nyq, o * HY_W), BF16, batch, n_tok)
        z = _dual_call(gc, gs, y_spec, p1, cd_conv_w[0], cd_conv_b[0], (o + 1) * HY_W, z, cd_skip[0][o:o + 1],
                       batch, n_tok, HY_W)
    cl, msl, cw = _fnet_operands(n_tok, FN_GROUP_W)
    y_fn = _fnet_call(cl, msl, p1, cw, HY_IN_W, batch, n_tok)
    h1, u2, logits = _outln_call(z, y_fn, cd_w_out, 0, h,
                                 mods[1][:, 2:5], ln1[1], _router_operand(router[1]), n_tok)
    h, _ = _ec_moe(h1, u2, logits, exp_w1, exp_w3, exp_w2, 1, mods[1][:, 5:6], ln2[1],
                   None, batch, n_tok)
    return h.reshape(batch, n_tok, d)
```

```python
import functools
import math

import numpy as np
import jax
import jax.numpy as jnp
from jax import lax
from jax.experimental import pallas as pl
from jax.experimental.pallas import tpu as pltpu

F32 = jnp.float32
BF16 = jnp.bfloat16
I32 = jnp.int32

DEPTH = 2
GRID_W = 64
DEEPNORM_ALPHA = (2.0 * DEPTH) ** 0.25
LN_EPS = 1e-6
NEG_INF = -1e30
LOG2E = math.log2(math.e)

MLA_HEADS = 8
MLA_NOPE = 128
MLA_ROPE = 64
MLA_QK = MLA_NOPE + MLA_ROPE
MLA_V = 128
MLA_KV_RANK = 512
ROPE_THETA = 10000.0

NA_HEADS = 8
NA_HEAD_DIM = 128
NA_KH = 8
NA_KW = 16

OFF_CKV = MLA_HEADS * MLA_QK
OFF_KPE = OFF_CKV + MLA_KV_RANK
OFF_QNA = OFF_KPE + MLA_ROPE

HY_W = 1024
HY_ORDER = 2
HY_IN_W = (HY_ORDER + 1) * HY_W
HY_SHORT = 3
HY_BANDS = 16
HY_DECAY_TARGET = 1e-2
HY_MIN_DECAY = math.log(HY_DECAY_TARGET) / 1.5
HY_MAX_DECAY = math.log(HY_DECAY_TARGET) / 0.3

FN_W = 1024
FN_GROUPS = 4
FN_GROUP_W = FN_W // FN_GROUPS

N_EXPERTS = 16
EC_CAPACITY_FACTOR = 2

LANES = 128
VMEM_LIMIT = 56 * 1024 * 1024

QTILE_W = 2 * LANES
T_KPE = MLA_HEADS
T_CKV = T_KPE + 2
T_QNA = T_CKV + MLA_KV_RANK // QTILE_W
T_KNA = T_QNA + NA_HEADS * NA_HEAD_DIM // QTILE_W
T_VNA = T_KNA + NA_HEADS * NA_HEAD_DIM // QTILE_W
N_ABTILES = T_VNA + NA_HEADS * NA_HEAD_DIM // QTILE_W
AB_PERM_W = N_ABTILES * QTILE_W
CTX_TILE0 = T_KPE
INPROJ_TILES_PER_STEP = 8

MLA_SUBTILES = 4
OUTLN_SUBTILES = 2
SEQMIX_SUBTILES = 4
MOE_CHUNK = 256
MOE_WINS = (64, 128)
NA_G = 4
NA_WR = NA_KH + NA_G - 1
NA_HEADS_PER_STEP = 2


def _params(sem, vmem=VMEM_LIMIT):
    return pltpu.CompilerParams(dimension_semantics=sem, vmem_limit_bytes=vmem)


def _dot(a, b):
    return jnp.dot(a, b, preferred_element_type=F32)


def _dot_nt(a, b):
    return lax.dot_general(a, b, (((1,), (1,)), ((), ())), preferred_element_type=F32)


def _dot_tn(a, b):
    return lax.dot_general(a, b, (((0,), (0,)), ((), ())), preferred_element_type=F32)


def _ada_kernel(c_ref, w_ref, b_ref, o_ref):
    c = c_ref[...]
    a = c / (1.0 + jnp.exp(-c))
    o_ref[0] = _dot(a.astype(BF16), w_ref[0].astype(BF16)) + b_ref[0]


def _ada_call(cv, ada_w, ada_b, tn=2048):
    depth, d, n = ada_w.shape
    rows = cv.shape[0]
    return pl.pallas_call(
        _ada_kernel,
        grid=(depth, n // tn),
        in_specs=[pl.BlockSpec((rows, d), lambda l, j: (0, 0)),
                  pl.BlockSpec((1, d, tn), lambda l, j: (l, 0, j)),
                  pl.BlockSpec((1, 1, tn), lambda l, j: (l, 0, j))],
        out_specs=pl.BlockSpec((1, rows, tn), lambda l, j: (l, 0, j)),
        out_shape=jax.ShapeDtypeStruct((depth, rows, n), F32),
        compiler_params=_params(("arbitrary", "arbitrary")),
        name="ada_params",
    )(cv, ada_w, ada_b.reshape(depth, 1, n))


def _mm_rope_kernel(x_ref, ss_ref, w_ref, ta_ref, tb_ref, tc_ref, o_ref, a_ref, *, tile0):
    @pl.when(pl.program_id(1) == 0)
    def _():
        a_ref[...] = (x_ref[...] * (1.0 + ss_ref[0, 1:2, :]) + ss_ref[0, 0:1, :]).astype(a_ref.dtype)

    a = a_ref[...]
    for q in range(INPROJ_TILES_PER_STEP):
        t = tile0 + INPROJ_TILES_PER_STEP * pl.program_id(1) + q
        c0 = q * QTILE_W
        acc = _dot_nt(a, w_ref[c0:c0 + QTILE_W, :])
        is_rope = t <= T_KPE
        f = jnp.where(t < T_KPE, LOG2E * MLA_QK ** -0.5,
                      jnp.where(jnp.logical_and(t >= T_QNA, t < T_KNA), LOG2E * NA_HEAD_DIM ** -0.5, 1.0)
                      ).astype(F32)
        hi = acc[:, LANES:]
        rot = (hi * jnp.where(is_rope, ta_ref[...], 1.0)
               + pltpu.roll(hi, LANES - MLA_ROPE // 2, 1) * jnp.where(is_rope, tb_ref[...], 0.0)
               + pltpu.roll(hi, MLA_ROPE // 2, 1) * jnp.where(is_rope, tc_ref[...], 0.0))
        o_ref[:, c0:c0 + LANES] = (acc[:, :LANES] * f).astype(o_ref.dtype)
        o_ref[:, c0 + LANES:c0 + QTILE_W] = (rot * f).astype(o_ref.dtype)


def _mm_rope_call(x, ss, rows_per_group, w, tabs, tm, tile0=0):
    m, k = x.shape
    tn = INPROJ_TILES_PER_STEP * QTILE_W
    n = w.shape[0] - tile0 * QTILE_W
    j0 = tile0 // INPROJ_TILES_PER_STEP
    ntab = tabs[0].shape[0] // tm
    tpg = rows_per_group // tm
    tab_spec = pl.BlockSpec((tm, LANES), lambda i, j: (i % ntab, 0))
    return pl.pallas_call(
        functools.partial(_mm_rope_kernel, tile0=tile0),
        grid=(m // tm, n // tn),
        in_specs=[pl.BlockSpec((tm, k), lambda i, j: (i, 0)),
                  pl.BlockSpec((1, 2, k), lambda i, j: (i // tpg, 0, 0)),
                  pl.BlockSpec((tn, k), lambda i, j: (j0 + j, 0)),
                  tab_spec, tab_spec, tab_spec],
        out_specs=pl.BlockSpec((tm, tn), lambda i, j: (i, j)),
        out_shape=jax.ShapeDtypeStruct((m, n), BF16),
        scratch_shapes=[pltpu.VMEM((tm, k), BF16)],
        compiler_params=_params(("arbitrary", "arbitrary")),
        name="attn_in_proj",
    )(x, ss, w, *tabs)


def _rope_tables(n_tok):
    t = np.arange(n_tok)
    row = (t // GRID_W).astype(np.float32)
    col = (t % GRID_W).astype(np.float32)
    n_freq = MLA_ROPE // 4
    inv = (ROPE_THETA ** (-np.arange(n_freq, dtype=np.float32) / n_freq)).astype(np.float32)
    ang = np.concatenate([row[:, None] * inv, col[:, None] * inv], axis=1)
    cos, sin = np.cos(ang).astype(np.float32), np.sin(ang).astype(np.float32)
    half = MLA_ROPE // 2
    ta = np.zeros((n_tok, LANES), np.float32)
    tb = np.zeros((n_tok, LANES), np.float32)
    tc = np.zeros((n_tok, LANES), np.float32)
    ta[:, :half] = cos
    ta[:, half:2 * half] = cos
    tb[:, :half] = -sin
    tc[:, half:2 * half] = sin
    return jnp.asarray(ta), jnp.asarray(tb), jnp.asarray(tc)


def _identity_rope_tables(n_tok):
    ta = np.zeros((n_tok, LANES), np.float32)
    ta[:, :MLA_ROPE] = 1.0
    z = np.zeros((n_tok, LANES), np.float32)
    return jnp.asarray(ta), jnp.asarray(z), jnp.asarray(z)


def _w_in_prep_kernel(w_ref, o_ref):
    x = w_ref[0]
    zeros = lambda n: jnp.zeros((n, x.shape[1]), x.dtype)
    rows = []
    for h in range(MLA_HEADS):
        rows += [x[h * MLA_QK:(h + 1) * MLA_QK, :], zeros(QTILE_W - MLA_QK)]
    rows += [zeros(LANES), x[OFF_KPE:OFF_QNA, :], zeros(LANES - MLA_ROPE), zeros(QTILE_W),
             x[OFF_CKV:OFF_KPE, :], x[OFF_QNA:, :]]
    o_ref[...] = jnp.concatenate(rows, axis=0).astype(o_ref.dtype)


def _w_in_prep_call(w_in_t, layer, tk=256):
    _, n, d = w_in_t.shape
    return pl.pallas_call(
        _w_in_prep_kernel,
        grid=(d // tk,),
        in_specs=[pl.BlockSpec((1, n, tk), lambda i: (layer, 0, i))],
        out_specs=pl.BlockSpec((AB_PERM_W, tk), lambda i: (0, i)),
        out_shape=jax.ShapeDtypeStruct((AB_PERM_W, d), BF16),
        compiler_params=_params(("arbitrary",)),
        name="attn_w_in_prep",
    )(w_in_t)


def _kvup_kernel(x_ref, g_ref, w_ref, o_ref):
    x = x_ref[...].astype(F32)
    y = x * lax.rsqrt(jnp.mean(x * x, axis=-1, keepdims=True) + LN_EPS) * g_ref[...]
    o_ref[...] = _dot(y.astype(BF16), w_ref[...]).astype(o_ref.dtype)


def _kvup_call(p, g, w, tm, tile0=0):
    m = p.shape[0]
    r, n = w.shape
    return pl.pallas_call(
        _kvup_kernel,
        grid=(m // tm,),
        in_specs=[pl.BlockSpec((tm, r), lambda i: (i, (T_CKV - tile0) * QTILE_W // r)),
                  pl.BlockSpec((1, r), lambda i: (0, 0)),
                  pl.BlockSpec((r, n), lambda i: (0, 0))],
        out_specs=pl.BlockSpec((tm, n), lambda i: (i, 0)),
        out_shape=jax.ShapeDtypeStruct((m, n), BF16),
        compiler_params=_params(("arbitrary",)),
        name="kv_up",
    )(p, g.reshape(1, r), w)


def _permute_w_ukv(w):
    r = w.shape[0]
    w3 = w.reshape(r, MLA_HEADS, MLA_NOPE + MLA_V)
    return jnp.concatenate([w3[:, :, :MLA_NOPE].reshape(r, -1), w3[:, :, MLA_NOPE:].reshape(r, -1)],
                           axis=1).astype(BF16)


def _mla_kernel(q_ref, kn_ref, kp_ref, v_ref, kcn_ref, kcp_ref, vc_ref, o_ref, kcat, vcat):
    s_len = kn_ref.shape[0]
    n_h = kcat.shape[0]
    for h in range(n_h):
        kcat[h, :s_len, :LANES] = kn_ref[:, h * LANES:(h + 1) * LANES]
        kcat[h, :s_len, LANES:] = kp_ref[...]
        kcat[h, s_len:, :LANES] = kcn_ref[:, h * LANES:(h + 1) * LANES]
        kcat[h, s_len:, LANES:] = kcp_ref[...]
        vcat[h, :s_len, :] = v_ref[:, h * LANES:(h + 1) * LANES]
        vcat[h, s_len:, :] = vc_ref[:, h * LANES:(h + 1) * LANES]

    sub = q_ref.shape[0] // MLA_SUBTILES
    for h in range(n_h):
        for r0 in range(0, q_ref.shape[0], sub):
            s = _dot_nt(q_ref[r0:r0 + sub, h * QTILE_W:(h + 1) * QTILE_W], kcat[h])
            m = jnp.max(s, axis=-1, keepdims=True)
            p = jnp.exp2(s - m)
            l = jnp.sum(p, axis=-1, keepdims=True)
            o = _dot(p.astype(BF16), vcat[h])
            o_ref[r0:r0 + sub, h * LANES:(h + 1) * LANES] = (o / l).astype(o_ref.dtype)


def _mla_call(p_lat, kv_lat, p_ctx, kv_ctx, batch, s_len, lc, heads_per_step=2):
    hp = heads_per_step
    kpe_blk = T_KPE * 2 + 1
    v0 = MLA_HEADS // hp
    return pl.pallas_call(
        _mla_kernel,
        grid=(batch, MLA_HEADS // hp),
        in_specs=[pl.BlockSpec((s_len, hp * QTILE_W), lambda b, h: (b, h)),
                  pl.BlockSpec((s_len, hp * LANES), lambda b, h: (b, h)),
                  pl.BlockSpec((s_len, LANES), lambda b, h: (b, kpe_blk)),
                  pl.BlockSpec((s_len, hp * LANES), lambda b, h: (b, v0 + h)),
                  pl.BlockSpec((lc, hp * LANES), lambda b, h: (b, h)),
                  pl.BlockSpec((lc, LANES), lambda b, h: (b, kpe_blk - 2 * CTX_TILE0)),
                  pl.BlockSpec((lc, hp * LANES), lambda b, h: (b, v0 + h))],
        out_specs=pl.BlockSpec((s_len, hp * LANES), lambda b, h: (b, h)),
        out_shape=jax.ShapeDtypeStruct((batch * s_len, MLA_HEADS * MLA_V), BF16),
        scratch_shapes=[pltpu.VMEM((hp, s_len + lc, QTILE_W), BF16),
                        pltpu.VMEM((hp, s_len + lc, LANES), BF16)],
        compiler_params=_params(("arbitrary", "arbitrary")),
        name="mla_attention",
    )(p_lat, kv_lat, p_lat, kv_lat, kv_ctx, p_ctx, kv_ctx)


def _na_kernel(q_ref, k_ref, v_ref, kc_ref, vc_ref, bias_ref, o_ref, *, rows, var_map):
    win = NA_WR * GRID_W
    tq = NA_G * GRID_W
    for h in range(q_ref.shape[1] // LANES):
        hs = slice(h * LANES, (h + 1) * LANES)
        for g in range(rows // NA_G):
            start = int(np.clip(NA_G * g - NA_KH // 2, 0, rows - NA_WR)) * GRID_W
            q = q_ref[g * tq:(g + 1) * tq, hs]
            sw = _dot_nt(q, k_ref[start:start + win, hs]) + bias_ref[var_map[g], h]
            sc = _dot_nt(q, kc_ref[:, hs])
            m = jnp.maximum(jnp.max(sw, axis=-1, keepdims=True), jnp.max(sc, axis=-1, keepdims=True))
            pw = jnp.exp2(sw - m)
            pc = jnp.exp2(sc - m)
            l = jnp.sum(pw, axis=-1, keepdims=True) + jnp.sum(pc, axis=-1, keepdims=True)
            o = _dot(pw.astype(BF16), v_ref[start:start + win, hs]) + _dot(pc.astype(BF16), vc_ref[:, hs])
            o_ref[g * tq:(g + 1) * tq, hs] = (o / l).astype(o_ref.dtype)


def _na_tables(rows):
    kh = min(NA_KH, rows)
    ng = rows // NA_G
    qr = np.arange(NA_G)[:, None]
    kr = np.arange(NA_WR)[None, :]
    sel_rows = []
    for g in range(ng):
        start_row = int(np.clip(NA_G * g - NA_KH // 2, 0, rows - NA_WR))
        r = NA_G * g + qr
        krow = start_row + kr
        rs = np.clip(r - kh // 2, 0, rows - kh)
        ok = (krow >= rs) & (krow < rs + kh)
        dr = krow - r + (NA_KH - 1)
        sel_rows.append(np.stack([ok & (dr == d) for d in range(2 * NA_KH - 1)]))
    sel_rows = np.stack(sel_rows)
    uniq, inverse = np.unique(sel_rows.reshape(ng, -1), axis=0, return_inverse=True)
    er = uniq.reshape((-1,) + sel_rows.shape[1:]).astype(np.float32)
    qc = np.arange(GRID_W)[:, None]
    kc = np.arange(GRID_W)[None, :]
    cs = np.clip(qc - NA_KW // 2, 0, GRID_W - NA_KW)
    ok_c = (kc >= cs) & (kc < cs + NA_KW)
    dc = np.clip(kc - qc, -(NA_KW - 1), NA_KW - 1) + (NA_KW - 1)
    ec = np.stack([ok_c & (dc == e) for e in range(2 * NA_KW - 1)]).astype(np.float32)
    return inverse.reshape(-1).astype(np.int32), er, ec


def _na_bias(rpb, er, ec):
    hp = lax.Precision.HIGHEST
    t = jnp.einsum('hde,eqk->hdqk', rpb.astype(F32), jnp.asarray(ec), precision=hp)
    t = jnp.where(jnp.asarray(ec.sum(axis=0) > 0.5), t * LOG2E, NEG_INF)
    d_of = np.where(er.sum(axis=1) > 0.5, er.argmax(axis=1), -1)
    n_var, n_h = d_of.shape[0], t.shape[0]

    def assemble(t_ref, o_ref):
        for v in range(n_var):
            @pl.when(pl.program_id(0) == v)
            def _(v=v):
                for a in range(NA_G):
                    for b in range(NA_WR):
                        d = int(d_of[v, a, b])
                        blk = t_ref[0, d] if d >= 0 else jnp.full((GRID_W, GRID_W), NEG_INF, F32)
                        o_ref[0, 0, a * GRID_W:(a + 1) * GRID_W, b * GRID_W:(b + 1) * GRID_W] = blk

    return pl.pallas_call(
        assemble,
        grid=(n_var, n_h),
        in_specs=[pl.BlockSpec((1,) + t.shape[1:], lambda v, h: (h, 0, 0, 0))],
        out_specs=pl.BlockSpec((1, 1, NA_G * GRID_W, NA_WR * GRID_W), lambda v, h: (v, h, 0, 0)),
        out_shape=jax.ShapeDtypeStruct((n_var, n_h, NA_G * GRID_W, NA_WR * GRID_W), F32),
        compiler_params=_params(("arbitrary", "arbitrary")),
        name="na_bias_table",
    )(t)


def _na_call(p_lat, p_ctx, bias, var_map, batch, s_len, lc):
    rows = s_len // GRID_W
    hp = NA_HEADS_PER_STEP
    tw = hp * LANES
    tq = NA_G * GRID_W
    win = NA_WR * GRID_W
    qb, kb, vb = (T_QNA * QTILE_W // tw, T_KNA * QTILE_W // tw, T_VNA * QTILE_W // tw)
    cb = CTX_TILE0 * QTILE_W // tw
    n_var = bias.shape[0]
    return pl.pallas_call(
        functools.partial(_na_kernel, rows=rows, var_map=tuple(int(v) for v in var_map)),
        grid=(batch, NA_HEADS // hp),
        in_specs=[pl.BlockSpec((s_len, tw), lambda b, h: (b, qb + h)),
                  pl.BlockSpec((s_len, tw), lambda b, h: (b, kb + h)),
                  pl.BlockSpec((s_len, tw), lambda b, h: (b, vb + h)),
                  pl.BlockSpec((lc, tw), lambda b, h: (b, kb + h - cb)),
                  pl.BlockSpec((lc, tw), lambda b, h: (b, vb + h - cb)),
                  pl.BlockSpec((n_var, hp, tq, win), lambda b, h: (0, h, 0, 0))],
        out_specs=pl.BlockSpec((s_len, tw), lambda b, h: (b, h)),
        out_shape=jax.ShapeDtypeStruct((batch * s_len, NA_HEADS * NA_HEAD_DIM), BF16),
        compiler_params=_params(("arbitrary", "arbitrary")),
        name="na_attention",
    )(p_lat, p_lat, p_lat, p_ctx, p_ctx, bias)


def _layer_norm(x, g, b):
    mu = jnp.mean(x, axis=-1, keepdims=True)
    xc = x - mu
    var = jnp.mean(xc * xc, axis=-1, keepdims=True)
    return xc * lax.rsqrt(var + LN_EPS) * g + b


def _outln_kernel(a1_ref, a2_ref, w_ref, h_ref, mod_ref, ln_ref, r_ref, h1_ref, u2_ref, lg_ref, wb_ref):
    half = a1_ref.shape[1]

    @pl.when(pl.program_id(0) == 0)
    def _():
        rows = wb_ref.shape[0] // 4
        for r0 in range(0, wb_ref.shape[0], rows):
            wb_ref[r0:r0 + rows, :] = w_ref[0, r0:r0 + rows, :].astype(wb_ref.dtype)

    sub = a1_ref.shape[0] // OUTLN_SUBTILES
    for r0 in range(0, a1_ref.shape[0], sub):
        rs = slice(r0, r0 + sub)
        y = _dot(a1_ref[rs, :], wb_ref[:half, :]) + _dot(a2_ref[rs, :], wb_ref[half:, :])
        x = DEEPNORM_ALPHA * h_ref[rs, :] + mod_ref[0, 0:1, :] * y
        hn = _layer_norm(x, ln_ref[0:1, :], ln_ref[1:2, :])
        h1_ref[rs, :] = hn
        u = hn * (1.0 + mod_ref[0, 2:3, :]) + mod_ref[0, 1:2, :]
        u_bf = u.astype(BF16)
        u2_ref[rs, :] = u_bf
        lg_ref[rs, :] = _dot(u_bf, r_ref[...])


def _outln_call(a1, a2, w, layer, h, mod, ln, router2, rows_per_group, tm=512):
    m, d = h.shape
    half = a1.shape[1]
    tpg = rows_per_group // tm
    return pl.pallas_call(
        _outln_kernel,
        grid=(m // tm,),
        in_specs=[pl.BlockSpec((tm, half), lambda i: (i, 0)),
                  pl.BlockSpec((tm, half), lambda i: (i, 0)),
                  pl.BlockSpec((1, 2 * half, d), lambda i: (layer, 0, 0), pipeline_mode=pl.Buffered(1)),
                  pl.BlockSpec((tm, d), lambda i: (i, 0)),
                  pl.BlockSpec((1, 3, d), lambda i: (i // tpg, 0, 0)),
                  pl.BlockSpec((2, d), lambda i: (0, 0)),
                  _resident_spec((d, LANES))],
        out_specs=[pl.BlockSpec((tm, d), lambda i: (i, 0)),
                   pl.BlockSpec((tm, d), lambda i: (i, 0)),
                   pl.BlockSpec((tm, LANES), lambda i: (i, 0))],
        out_shape=[jax.ShapeDtypeStruct((m, d), F32),
                   jax.ShapeDtypeStruct((m, d), BF16),
                   jax.ShapeDtypeStruct((m, LANES), F32)],
        scratch_shapes=[pltpu.VMEM((2 * half, d), BF16)],
        compiler_params=_params(("arbitrary",)),
        name="out_proj_postnorm",
    )(a1, a2, w, h, mod, ln, router2)


def _router_operand(router):
    d, e = router.shape
    return jnp.pad(router, ((0, 0), (0, LANES - e))).astype(BF16)


def _route_kernel(lg_ref, slot_ref, aff_ref, cnt_ref, tri_ref, *, cap):
    n_tok = lg_ref.shape[2]

    @pl.when(pl.program_id(0) == 0)
    def _():
        chunk = 256
        for r0 in range(0, n_tok, chunk):
            r = r0 + lax.broadcasted_iota(I32, (chunk, n_tok), 0)
            c = lax.broadcasted_iota(I32, (chunk, n_tok), 1)
            tri_ref[r0:r0 + chunk, :] = jnp.where(r < c, 1.0, 0.0).astype(BF16)

    lg = lg_ref[0]
    ex = jnp.exp(lg - jnp.max(lg, axis=0, keepdims=True))
    aff = ex / jnp.sum(ex, axis=0, keepdims=True)
    bits = lax.bitcast_convert_type(aff, I32)
    n_e = lg.shape[0]
    count = lambda mask: jnp.sum(jnp.where(mask, 1.0, 0.0), axis=1, keepdims=True)

    def body(_, lohi):
        lo, hi = lohi
        mid = lo + jnp.right_shift(hi - lo, 1)
        ge = count(bits >= mid) >= cap
        return jnp.where(ge, mid, lo), jnp.where(ge, hi, mid)

    lo0 = jnp.zeros((n_e, 1), I32)
    hi0 = jnp.full((n_e, 1), 0x7F800000, I32)
    thr, _ = lax.fori_loop(0, 31, body, (lo0, hi0))
    gt = bits > thr
    eq = bits == thr
    need = cap - count(gt)
    pre_eq = _dot(jnp.where(eq, 1.0, 0.0).astype(BF16), tri_ref[...])
    sel = jnp.logical_or(gt, jnp.logical_and(eq, pre_eq < need))
    sel_bf = jnp.where(sel, 1.0, 0.0).astype(BF16)
    slot = _dot(sel_bf, tri_ref[...])
    slot_ref[0] = jnp.where(sel, slot.astype(I32), -1)
    aff_ref[0] = aff
    r = lax.broadcasted_iota(I32, (n_tok, LANES), 0)
    c = lax.broadcasted_iota(I32, (n_tok, LANES), 1)
    before = jnp.where(r < c * MOE_CHUNK, 1.0, 0.0).astype(BF16)
    cnt_ref[0] = _dot(sel_bf, before).astype(I32)


def _route_call(lg_t, cap):
    b, e, n_tok = lg_t.shape
    spec = pl.BlockSpec((1, e, n_tok), lambda i: (i, 0, 0))
    return pl.pallas_call(
        functools.partial(_route_kernel, cap=cap),
        grid=(b,),
        in_specs=[spec],
        out_specs=[spec, spec, pl.BlockSpec((1, e, LANES), lambda i: (i, 0, 0))],
        out_shape=[jax.ShapeDtypeStruct((b, e, n_tok), I32), jax.ShapeDtypeStruct((b, e, n_tok), F32),
                   jax.ShapeDtypeStruct((b, e, LANES), I32)],
        scratch_shapes=[pltpu.VMEM((n_tok, n_tok), BF16)],
        compiler_params=_params(("arbitrary",)),
        name="ec_route",
    )(lg_t)


def _slot_windows(cnt, cap, n_tok):
    bounds = cnt[:, :, :n_tok // MOE_CHUNK + 1]
    start, end = bounds[:, :, :-1], bounds[:, :, 1:]
    starts, tier = [], jnp.full(start[:, 0].shape, len(MOE_WINS), I32)
    for i, win in reversed(list(enumerate(MOE_WINS))):
        a = jnp.minimum((start // 16) * 16, cap - win)
        tier = jnp.where(jnp.all(end <= a + win, axis=1), i, tier)
        starts.insert(0, jnp.swapaxes(a, 1, 2).reshape(-1))
    return jnp.concatenate(starts).astype(I32), tier.reshape(-1).astype(I32)


def _gather_kernel(win_ref, tier_ref, slot_ref, u_ref, o_ref, *, chunk):
    b, ks = pl.program_id(0), pl.program_id(1)
    n_e, cap = o_ref.shape[0], o_ref.shape[1]
    n_sub = u_ref.shape[0] // chunk
    step = b * pl.num_programs(1) + ks

    @pl.when(ks == 0)
    def _():
        o_ref[...] = jnp.zeros(o_ref.shape, o_ref.dtype)

    def place(rows, start_of, group, tok0, n_t):
        row = lax.broadcasted_iota(I32, (rows, n_t), 0)
        for g0 in range(0, n_e, group):
            starts = [start_of(e) for e in range(g0, g0 + group)]
            onehot = jnp.concatenate(
                [jnp.where(row + st == slot_ref[0, e:e + 1, tok0:tok0 + n_t], 1.0, 0.0).astype(BF16)
                 for st, e in zip(starts, range(g0, g0 + group))], axis=0)
            got = _dot(onehot, u_ref[tok0:tok0 + n_t, :]).astype(o_ref.dtype)
            for i, (st, e) in enumerate(zip(starts, range(g0, g0 + group))):
                o_ref[e, pl.ds(st, rows), :] += got[i * rows:(i + 1) * rows]

    n_chunks = pl.num_programs(0) * pl.num_programs(1) * n_sub
    for sub in range(n_sub):
        c = step * n_sub + sub
        for i, win in enumerate(MOE_WINS):
            @pl.when(tier_ref[c] == i)
            def _(c=c, sub=sub, i=i, win=win):
                base = (i * n_chunks + c) * n_e
                place(win, lambda e: pl.multiple_of(win_ref[base + e], 16), 512 // win, sub * chunk, chunk)

        @pl.when(tier_ref[c] == len(MOE_WINS))
        def _(sub=sub):
            place(cap, lambda e: 0, 2, sub * chunk, chunk)


def _gather_call(starts, tier, slot, u2, cap, chunks_per_step=2):
    b, e, n_tok = slot.shape
    d = u2.shape[1]
    chunk = MOE_CHUNK
    tt = chunk * chunks_per_step
    nc = n_tok // tt
    grid_spec = pltpu.PrefetchScalarGridSpec(
        num_scalar_prefetch=2,
        grid=(b, nc),
        in_specs=[pl.BlockSpec((1, e, tt), lambda i, k, w, f: (i, 0, k)),
                  pl.BlockSpec((tt, d), lambda i, k, w, f: (i * nc + k, 0))],
        out_specs=pl.BlockSpec((e, cap, d), lambda i, k, w, f: (0, i, 0)),
    )
    return pl.pallas_call(
        functools.partial(_gather_kernel, chunk=chunk),
        grid_spec=grid_spec,
        out_shape=jax.ShapeDtypeStruct((e, b * cap, d), BF16),
        compiler_params=_params(("arbitrary", "arbitrary")),
        name="moe_gather",
    )(starts, tier, slot, u2)


def _expert_hidden_kernel(x_ref, w1_ref, w3_ref, o_ref, *, ff):
    tf = w1_ref.shape[3]
    last = pl.num_programs(1) - 1

    def block(width):
        w = jnp.concatenate([w1_ref[0, 0, :, :width].astype(BF16), w3_ref[0, 0, :, :width].astype(BF16)], axis=1)
        h = _dot(x_ref[0], w)
        h1 = h[:, :width]
        o_ref[0, :, :width] = ((h1 / (1.0 + jnp.exp(-h1))) * h[:, width:]).astype(o_ref.dtype)

    tail = ff - (pl.cdiv(ff, tf) - 1) * tf
    if tail == tf:
        block(tf)
    else:
        pl.when(pl.program_id(1) < last)(lambda: block(tf))
        pl.when(pl.program_id(1) == last)(lambda: block(tail))


def _expert_out_kernel(hd_ref, w2_ref, o_ref):
    y = _dot(hd_ref[0], w2_ref[0, 0].astype(BF16)).astype(o_ref.dtype)
    cap = o_ref.shape[1]
    for b in range(o_ref.shape[0]):
        o_ref[b] = y[b * cap:(b + 1) * cap]


def _expert_call(xe, w1, w3, w2, layer, batch, tf=768, tn=2048):
    e, m, d = xe.shape
    cap = m // batch
    ff = w1.shape[3]
    hidden = pl.pallas_call(
        functools.partial(_expert_hidden_kernel, ff=ff),
        grid=(e, pl.cdiv(ff, tf)),
        in_specs=[pl.BlockSpec((1, m, d), lambda i, f: (i, 0, 0)),
                  pl.BlockSpec((1, 1, d, tf), lambda i, f: (layer, i, 0, f)),
                  pl.BlockSpec((1, 1, d, tf), lambda i, f: (layer, i, 0, f))],
        out_specs=pl.BlockSpec((1, m, tf), lambda i, f: (i, 0, f)),
        out_shape=jax.ShapeDtypeStruct((e, m, ff), BF16),
        compiler_params=_params(("arbitrary", "arbitrary")),
        name="moe_expert_hidden",
    )(xe, w1, w3)
    return pl.pallas_call(
        _expert_out_kernel,
        grid=(e, d // tn),
        in_specs=[pl.BlockSpec((1, m, ff), lambda i, j: (i, 0, 0)),
                  pl.BlockSpec((1, 1, ff, tn), lambda i, j: (layer, i, 0, j))],
        out_specs=pl.BlockSpec((batch, cap, tn), lambda i, j: (0, i, j)),
        out_shape=jax.ShapeDtypeStruct((batch, e * cap, d), BF16),
        compiler_params=_params(("arbitrary", "arbitrary")),
        name="moe_expert_out",
    )(hidden, w2)


def _scatter_kernel(win_ref, tier_ref, slot_ref, aff_ref, ye_ref, h_ref, mod_ref, ln_ref, *rest, with_next):
    if with_next:
        nmod_ref, h2_ref, un_ref, pt_ref, yew_ref, moe_ref = rest
    else:
        h2_ref, pt_ref, yew_ref, moe_ref = rest
    n_e = slot_ref.shape[1]
    cap = ye_ref.shape[1] // n_e
    tt = h_ref.shape[0]
    step = pl.program_id(0) * pl.num_programs(1) + pl.program_id(1)

    def gates(rows, e, start):
        row = lax.broadcasted_iota(I32, (rows, tt), 0)
        return jnp.where(row + start == slot_ref[0, e:e + 1, :], aff_ref[0, e:e + 1, :], 0.0).astype(BF16)

    n_chunks = pl.num_programs(0) * pl.num_programs(1)
    for i, win in enumerate(MOE_WINS):
        @pl.when(tier_ref[step] == i)
        def _(i=i, win=win):
            for e in range(n_e):
                start = pl.multiple_of(win_ref[(i * n_chunks + step) * n_e + e], 16)
                pt_ref[e * win:(e + 1) * win, :] = gates(win, e, start)
                yew_ref[e * win:(e + 1) * win, :] = ye_ref[0, pl.ds(e * cap + start, win), :]
            moe_ref[...] = _dot_tn(pt_ref[:n_e * win, :], yew_ref[:n_e * win, :])

    @pl.when(tier_ref[step] == len(MOE_WINS))
    def _():
        for e in range(n_e):
            pt_ref[e * cap:(e + 1) * cap, :] = gates(cap, e, 0)
        moe_ref[...] = _dot_tn(pt_ref[...], ye_ref[0])

    x = DEEPNORM_ALPHA * h_ref[...] + mod_ref[0] * moe_ref[...]
    hn = _layer_norm(x, ln_ref[0:1, :], ln_ref[1:2, :])
    h2_ref[...] = hn
    if with_next:
        un_ref[...] = (hn * (1.0 + nmod_ref[0, 1:2, :]) + nmod_ref[0, 0:1, :]).astype(un_ref.dtype)


def _scatter_call(starts, tier, slot, aff, ye, h1, gate, ln, next_ss):
    b, e, n_tok = slot.shape
    m, d = h1.shape
    tt = MOE_CHUNK
    nt = n_tok // tt
    rows = ye.shape[1]
    with_next = next_ss is not None
    tok_spec = pl.BlockSpec((1, e, tt), lambda i, t, w, f: (i, 0, t))
    row_spec = pl.BlockSpec((tt, d), lambda i, t, w, f: (i * nt + t, 0))
    in_specs = [tok_spec, tok_spec,
                pl.BlockSpec((1, rows, d), lambda i, t, w, f: (i, 0, 0), pipeline_mode=pl.Buffered(1)),
                row_spec,
                pl.BlockSpec((1, 1, d), lambda i, t, w, f: (i, 0, 0)),
                pl.BlockSpec((2, d), lambda i, t, w, f: (0, 0))]
    args = [slot, aff, ye, h1, gate, ln]
    out_specs = [row_spec]
    out_shape = [jax.ShapeDtypeStruct((m, d), F32)]
    if with_next:
        in_specs.append(pl.BlockSpec((1, 2, d), lambda i, t, w, f: (i, 0, 0)))
        args.append(next_ss)
        out_specs.append(row_spec)
        out_shape.append(jax.ShapeDtypeStruct((m, d), BF16))
    grid_spec = pltpu.PrefetchScalarGridSpec(
        num_scalar_prefetch=2,
        grid=(b, nt),
        in_specs=in_specs,
        out_specs=out_specs,
        scratch_shapes=[pltpu.VMEM((rows, tt), BF16),
                        pltpu.VMEM((e * max(MOE_WINS), d), BF16),
                        pltpu.VMEM((tt, d), F32)],
    )
    res = pl.pallas_call(
        functools.partial(_scatter_kernel, with_next=with_next),
        grid_spec=grid_spec,
        out_shape=out_shape,
        compiler_params=_params(("arbitrary", "arbitrary")),
        name="moe_combine_postnorm",
    )(starts, tier, *args)
    return res if with_next else (res[0], None)


def _ec_moe(h1, u2, logits, w1, w3, w2, layer, gate, ln, next_ss, batch, n_tok):
    cap = EC_CAPACITY_FACTOR * n_tok // N_EXPERTS
    lg_t = jnp.swapaxes(logits[:, :N_EXPERTS].reshape(batch, n_tok, N_EXPERTS), 1, 2)
    slot, aff, cnt = _route_call(lg_t, cap)
    starts, tier = _slot_windows(cnt, cap, n_tok)
    xe = _gather_call(starts, tier, slot, u2, cap)
    ye = _expert_call(xe, w1, w3, w2, layer, batch)
    return _scatter_call(starts, tier, slot, aff, ye, h1, gate, ln, next_ss)


def _mm_kernel(a_ref, w_ref, o_ref, wb_ref):
    @pl.when(pl.program_id(1) == 0)
    def _():
        wb_ref[...] = w_ref[0].astype(wb_ref.dtype)

    o_ref[...] = _dot(a_ref[...], wb_ref[...]).astype(o_ref.dtype)


def _mm_call(a, w, layer, out_dtype, tm=1024, tn=1024):
    m, k = a.shape
    n = w.shape[2]
    return pl.pallas_call(
        _mm_kernel,
        grid=(n // tn, m // tm),
        in_specs=[pl.BlockSpec((tm, k), lambda j, i: (i, 0)),
                  pl.BlockSpec((1, k, tn), lambda j, i: (layer, 0, j))],
        out_specs=pl.BlockSpec((tm, tn), lambda j, i: (i, j)),
        out_shape=jax.ShapeDtypeStruct((m, n), out_dtype),
        scratch_shapes=[pltpu.VMEM((k, tn), BF16)],
        compiler_params=_params(("arbitrary", "arbitrary")),
        name="matmul",
    )(a, w)


def _short_conv(p, w, b):
    x = p.astype(F32)
    n = x.shape[0]
    r = lax.broadcasted_iota(I32, x.shape, 0)
    prev = jnp.where(r == 0, 0.0, pltpu.roll(x, 1, 0))
    nxt = jnp.where(r == n - 1, 0.0, pltpu.roll(x, n - 1, 0))
    return b + w[0:1, :] * prev + w[1:2, :] * x + w[2:3, :] * nxt


def _sconv_kernel(p_ref, w_ref, b_ref, o_ref):
    o_ref[...] = _short_conv(p_ref[...], w_ref[...], b_ref[...]).astype(o_ref.dtype)


def _sconv_call(p, conv_w, conv_b, col0, ncols, out_dtype, batch, n_tok, tc=512):
    c0 = col0 // tc
    return pl.pallas_call(
        _sconv_kernel,
        grid=(batch, ncols // tc),
        in_specs=[pl.BlockSpec((n_tok, tc), lambda b, j: (b, c0 + j)),
                  pl.BlockSpec((HY_SHORT, tc), lambda b, j: (0, c0 + j)),
                  pl.BlockSpec((1, tc), lambda b, j: (0, c0 + j))],
        out_specs=pl.BlockSpec((n_tok, tc), lambda b, j: (b, j)),
        out_shape=jax.ShapeDtypeStruct((batch * n_tok, ncols), out_dtype),
        compiler_params=_params(("arbitrary", "arbitrary")),
        name="hyena_short_conv",
    )(p, conv_w, conv_b.reshape(1, -1))


def _dft_fwd_kernel(fc_ref, fs_ref, r1_ref, r2_ref, *rest, spectral):
    o_ref = rest[-1]
    sub = fc_ref.shape[0] // SEQMIX_SUBTILES
    for r0 in range(0, fc_ref.shape[0], sub):
        rs = slice(r0, r0 + sub)
        xr = _dot(fc_ref[rs, :], r1_ref[...])
        xi = _dot(fs_ref[rs, :], r2_ref[...])
        if spectral:
            h_ref, nyq_ref = rest[:2]
            hr = h_ref[0, 0, rs, :]
            hi = h_ref[0, 1, rs, :]
            if r0 == 0:
                dc = lax.broadcasted_iota(I32, xr.shape, 0) == 0
                yr = xr * hr - jnp.where(dc, 0.0, xi * hi)
                yi = jnp.where(dc, xi * nyq_ref[...], xr * hi + xi * hr)
            else:
                yr = xr * hr - xi * hi
                yi = xr * hi + xi * hr
        else:
            yr, yi = xr, xi
        o_ref[0, 0, rs, :] = yr.astype(o_ref.dtype)
        o_ref[0, 1, rs, :] = yi.astype(o_ref.dtype)


def _resident_spec(shape):
    return pl.BlockSpec(shape, lambda *_: (0,) * len(shape), pipeline_mode=pl.Buffered(1))


def _dft_fwd_call(fc, fs, r1, r2, spec, out_dtype, batch, n_tok, tn=256):
    ncols = r1.shape[1]
    r_spec = pl.BlockSpec((n_tok, tn), lambda j, b: (b, j))
    in_specs = [_resident_spec(fc.shape), _resident_spec(fs.shape), r_spec, r_spec]
    args = [fc, fs, r1, r2]
    if spec is not None:
        h, nyq, hcol0 = spec
        c0 = hcol0 // tn
        in_specs += [pl.BlockSpec((1, 2, n_tok, tn), lambda j, b: (0, 0, 0, c0 + j)),
                     pl.BlockSpec((1, tn), lambda j, b: (0, c0 + j))]
        args += [h, nyq]
    return pl.pallas_call(
        functools.partial(_dft_fwd_kernel, spectral=spec is not None),
        grid=(ncols // tn, batch),
        in_specs=in_specs,
        out_specs=pl.BlockSpec((1, 2, n_tok, tn), lambda j, b: (b, 0, 0, j)),
        out_shape=jax.ShapeDtypeStruct((batch, 2, n_tok, ncols), out_dtype),
        compiler_params=_params(("arbitrary", "arbitrary")),
        name="dft_forward",
    )(*args)


def _dual_kernel(gc_ref, gs_ref, y_ref, p_ref, cw_ref, cb_ref, z_ref, skip_ref, o_ref):
    yr, yi = y_ref[0, 0], y_ref[0, 1]
    gate = _short_conv(p_ref[...], cw_ref[...], cb_ref[...])
    sub = o_ref.shape[0] // SEQMIX_SUBTILES
    for r0 in range(0, o_ref.shape[0], sub):
        rs = slice(r0, r0 + sub)
        y = _dot(gc_ref[rs, :], yr) + _dot(gs_ref[rs, :], yi)
        o_ref[rs, :] = (gate[rs] * (y + skip_ref[...] * z_ref[rs, :].astype(F32))).astype(o_ref.dtype)


def _dual_call(gc, gs, y_spec, p, conv_w, conv_b, pcol0, z, skip, batch, n_tok, ncols, tn=256):
    c0 = pcol0 // tn
    o_spec = pl.BlockSpec((n_tok, tn), lambda b, j: (b, j))
    return pl.pallas_call(
        _dual_kernel,
        grid=(batch, ncols // tn),
        in_specs=[_resident_spec(gc.shape), _resident_spec(gs.shape),
                  pl.BlockSpec((1, 2, n_tok, tn), lambda b, j: (b, 0, 0, j)),
                  pl.BlockSpec((n_tok, tn), lambda b, j: (b, c0 + j)),
                  pl.BlockSpec((HY_SHORT, tn), lambda b, j: (0, c0 + j)),
                  pl.BlockSpec((1, tn), lambda b, j: (0, c0 + j)),
                  o_spec,
                  pl.BlockSpec((1, tn), lambda b, j: (0, j))],
        out_specs=o_spec,
        out_shape=jax.ShapeDtypeStruct((batch * n_tok, ncols), BF16),
        compiler_params=_params(("arbitrary", "arbitrary")),
        name="seq_mix_matmul",
    )(gc, gs, y_spec, p, conv_w, conv_b.reshape(1, -1), z, skip)


def _fnet_kernel(cl_ref, msl_ref, x_ref, cw_ref, o_ref):
    gw = cw_ref.shape[1]
    for c0 in range(0, x_ref.shape[1], gw):
        x = x_ref[:, c0:c0 + gw]
        seq = jnp.concatenate([_dot(cl_ref[...], x).astype(BF16), _dot(msl_ref[...], x).astype(BF16)], axis=1)
        o_ref[:, c0:c0 + gw] = _dot(seq, cw_ref[...]).astype(o_ref.dtype)


def _fnet_call(cl, msl, p, cw, col0, batch, n_tok, groups_per_step=2):
    gw = cw.shape[1]
    tw = gw * groups_per_step
    c0 = col0 // tw
    return pl.pallas_call(
        _fnet_kernel,
        grid=(batch, FN_GROUPS // groups_per_step),
        in_specs=[_resident_spec(cl.shape), _resident_spec(msl.shape),
                  pl.BlockSpec((n_tok, tw), lambda b, g: (b, c0 + g)),
                  pl.BlockSpec(cw.shape, lambda b, g: (0, 0))],
        out_specs=pl.BlockSpec((n_tok, tw), lambda b, g: (b, g)),
        out_shape=jax.ShapeDtypeStruct((batch * n_tok, FN_GROUPS * gw), BF16),
        compiler_params=_params(("arbitrary", "arbitrary")),
        name="fnet_mix",
    )(cl, msl, p, cw)


def _cos_sin_matrix(n_rows, n_cols, period, split=64):
    r = np.arange(n_rows, dtype=np.int64)[:, None]
    c_hi = (np.arange(n_cols // split, dtype=np.int64) * split)[None, :]
    c_lo = np.arange(split, dtype=np.int64)[None, :]
    ang = lambda c: 2.0 * np.pi * ((r * c) % period).astype(np.float64) / period
    ca, sa = jnp.asarray(np.cos(ang(c_hi)), F32), jnp.asarray(np.sin(ang(c_hi)), F32)
    cb, sb = jnp.asarray(np.cos(ang(c_lo)), F32), jnp.asarray(np.sin(ang(c_lo)), F32)
    cos = ca[:, :, None] * cb[:, None, :] - sa[:, :, None] * sb[:, None, :]
    sin = sa[:, :, None] * cb[:, None, :] + ca[:, :, None] * sb[:, None, :]
    return cos.reshape(n_rows, n_cols), sin.reshape(n_rows, n_cols)


def _trig_kernel(ta_ref, tb_ref, ea_ref, eb_ref, *o_refs, mode, period, scale):
    sa_ca = _dot(ta_ref[...], ea_ref[...])
    sb_cb = _dot(tb_ref[...], eb_ref[...])
    nc = sa_ca.shape[1] // 2
    ca, sa, cb, sb = sa_ca[:, :nc], sa_ca[:, nc:], sb_cb[:, :nc], sb_cb[:, nc:]
    cos = ca * cb - sa * sb
    sin = sa * cb + ca * sb
    tm = cos.shape[0]
    row = pl.program_id(0) * tm + lax.broadcasted_iota(I32, cos.shape, 0)
    col = lax.broadcasted_iota(I32, cos.shape, 1)
    alt = lambda idx: (1 - 2 * jnp.bitwise_and(idx, 1)).astype(F32)
    if mode == "hyena":
        fc_ref, fs_ref, gc_ref, gs_ref = o_refs
        fc_ref[...] = cos.astype(fc_ref.dtype)
        fs_ref[...] = jnp.where(row == 0, alt(col), -sin).astype(fs_ref.dtype)
        gc_ref[...] = (cos * jnp.where(col == 0, 1.0 / period, 2.0 / period)).astype(gc_ref.dtype)
        gs_ref[...] = jnp.where(col == 0, alt(row) * (1.0 / period), sin * (-2.0 / period)).astype(gs_ref.dtype)
    else:
        c_ref, ms_ref = o_refs
        c_ref[...] = (cos * scale).astype(c_ref.dtype)
        ms_ref[...] = (sin * (-scale)).astype(ms_ref.dtype)


def _split3(x):
    bf = jnp.bfloat16
    x = np.asarray(x, np.float32)
    hi = x.astype(bf)
    r1 = x - hi.astype(np.float32)
    mid = r1.astype(bf)
    lo = (r1 - mid.astype(np.float32)).astype(bf)
    return [hi, mid, lo]


def _trig_call(n, period, mode, scale=1.0, split=64, tm=256):
    r = np.arange(n, dtype=np.int64)[:, None]
    c_hi = (np.arange(n // split, dtype=np.int64) * split)[None, :]
    c_lo = np.arange(split, dtype=np.int64)[None, :]
    ang = lambda c: 2.0 * np.pi * ((r * c) % period).astype(np.float64) / period
    def operands(a, col_group):
        terms = _split3(np.cos(a)) + _split3(np.sin(a))
        k = a.shape[1]
        spread = (col_group[None, :] == np.arange(k)[:, None]).astype(np.float32)
        zero = np.zeros_like(spread)
        e = np.concatenate([np.concatenate([spread, zero], 1)] * 3 + [np.concatenate([zero, spread], 1)] * 3, 0)
        return jnp.asarray(np.concatenate(terms, axis=1)), jnp.asarray(e, BF16)
    cols = np.arange(n)
    ta, ea = operands(ang(c_hi), cols // split)
    tb, eb = operands(ang(c_lo), cols % split)
    n_out = 4 if mode == "hyena" else 2
    o_spec = pl.BlockSpec((tm, n), lambda i: (i, 0))
    return pl.pallas_call(
        functools.partial(_trig_kernel, mode=mode, period=period, scale=scale),
        grid=(n // tm,),
        in_specs=[pl.BlockSpec((tm, ta.shape[1]), lambda i: (i, 0)),
                  pl.BlockSpec((tm, tb.shape[1]), lambda i: (i, 0)),
                  pl.BlockSpec(ea.shape, lambda i: (0, 0)),
                  pl.BlockSpec(eb.shape, lambda i: (0, 0))],
        out_specs=[o_spec] * n_out,
        out_shape=[jax.ShapeDtypeStruct((n, n), BF16)] * n_out,
        compiler_params=_params(("arbitrary",)),
        name="dft_tables",
    )(ta, tb, ea, eb)


def _hyena_dft_operands(n_tok):
    return _trig_call(n_tok, 2 * n_tok, "hyena")


def _fnet_operands(n_tok, gw):
    cl, msl = _trig_call(n_tok, n_tok, "fnet", scale=1.0 / math.sqrt(n_tok * gw))
    cw, sw = _cos_sin_matrix(gw, gw, gw)
    return cl, msl, jnp.concatenate([cw, sw], axis=0).astype(BF16)


def _filter_kernel(z_ref, w1_ref, b1_ref, f1_ref, w2_ref, b2_ref, f2_ref, w3f_ref, w3b_ref, dl_ref,
                   sum_ref, dif_ref, nyq_ref, hdn_ref):
    @pl.when(jnp.logical_and(pl.program_id(0) == 0, pl.program_id(1) == 0))
    def _():
        hdot = lambda a, b: jnp.dot(a, b, precision=lax.Precision.HIGHEST, preferred_element_type=F32)
        hdn = jnp.sin(f1_ref[...] * (hdot(z_ref[...], w1_ref[...]) + b1_ref[...]))
        hdn = jnp.sin(f2_ref[...] * (hdot(hdn, w2_ref[...]) + b2_ref[...]))
        hdn_ref[...] = hdn.astype(hdn_ref.dtype)

    n = hdn_ref.shape[0]
    row = lax.broadcasted_iota(I32, (n, dl_ref.shape[1]), 0)
    decay = jnp.exp(-(row.astype(F32) / (n - 1)) * dl_ref[...])
    hf = _dot(hdn_ref[...], w3f_ref[...].astype(BF16)) * decay
    hb = jnp.where(row == 0, 0.0, _dot(hdn_ref[...], w3b_ref[...].astype(BF16)) * decay)
    tot = hf + hb
    sum_ref[...] = tot.astype(sum_ref.dtype)
    dif_ref[...] = (hf - hb).astype(dif_ref.dtype)
    nyq_ref[...] = jnp.sum(jnp.where(jnp.bitwise_and(row, 1) == 0, tot, -tot), axis=0, keepdims=True)


def _filter_call(n_tok, fw1, fb1, ff1, fw2, fb2, ff2, fw3, tc=512):
    t01 = np.linspace(0.0, 1.0, n_tok, dtype=np.float32)
    w = (2.0 * math.pi * np.arange(n_tok, dtype=np.float32) / n_tok).astype(np.float32)
    bands = np.linspace(1e-4, HY_BANDS - 1, HY_BANDS, dtype=np.float32)
    z = np.concatenate([t01[:, None], np.cos(w[:, None] * bands), -np.sin(w[:, None] * bands)], -1)
    deltas = np.abs(np.linspace(HY_MIN_DECAY, HY_MAX_DECAY, HY_W, dtype=np.float32))[None, :]
    emb, hid = fw1.shape
    nc = HY_W // tc
    full = lambda shape: pl.BlockSpec(shape, lambda o, j: (0,) * len(shape))
    o_spec = pl.BlockSpec((n_tok, tc), lambda o, j: (0, o * nc + j))
    return pl.pallas_call(
        _filter_kernel,
        grid=(HY_ORDER, nc),
        in_specs=[full((n_tok, emb)), full((emb, hid)), full((1, hid)), full((1, hid)),
                  full((hid, hid)), full((1, hid)), full((1, hid)),
                  pl.BlockSpec((hid, tc), lambda o, j: (0, (2 * o) * nc + j)),
                  pl.BlockSpec((hid, tc), lambda o, j: (0, (2 * o + 1) * nc + j)),
                  pl.BlockSpec((1, tc), lambda o, j: (0, j))],
        out_specs=[o_spec, o_spec, pl.BlockSpec((1, tc), lambda o, j: (0, o * nc + j))],
        out_shape=[jax.ShapeDtypeStruct((n_tok, HY_ORDER * HY_W), BF16),
                   jax.ShapeDtypeStruct((n_tok, HY_ORDER * HY_W), BF16),
                   jax.ShapeDtypeStruct((1, HY_ORDER * HY_W), F32)],
        scratch_shapes=[pltpu.VMEM((n_tok, hid), BF16)],
        compiler_params=_params(("arbitrary", "arbitrary")),
        name="hyena_filters",
    )(jnp.asarray(z.astype(np.float32)), fw1, fb1.reshape(1, hid), ff1.reshape(1, hid),
      fw2, fb2.reshape(1, hid), ff2.reshape(1, hid), fw3, fw3, jnp.asarray(deltas))


def kernel(x, c, ctx, c_ctx, ada_w, ada_b, ln1_g, ln1_b, ln2_g, ln2_b, router, exp_w1, exp_w3, exp_w2,
           ab_w_in, ab_kv_norm, ab_w_ukv, ab_rpb, ab_w_out,
           cd_w_in, cd_conv_w, cd_conv_b, cd_filt_w1, cd_filt_b1, cd_filt_freq1, cd_filt_w2, cd_filt_b2,
           cd_filt_freq2, cd_filt_w3, cd_skip, cd_w_out):
    batch, n_tok, d = x.shape
    lc = ctx.shape[1]
    x2d = x.reshape(batch * n_tok, d)
    ctx2d = ctx.reshape(batch * lc, d)

    pad_rows = (-(batch + 1)) % 8
    cv = jnp.concatenate([c, c_ctx[None, :], jnp.zeros((pad_rows, d), F32)], axis=0)
    ada = _ada_call(cv, ada_w, ada_b)
    mods = [ada[i, :batch].reshape(batch, 6, d) for i in range(DEPTH)]
    ctx_ss = ada[0, batch, :2 * d].reshape(1, 2, d)

    w_in = _w_in_prep_call(jnp.swapaxes(ab_w_in, 1, 2), 0)
    p_lat = _mm_rope_call(x2d, mods[0][:, 0:2], n_tok, w_in, _rope_tables(n_tok), tm=1024)
    p_ctx = _mm_rope_call(ctx2d, ctx_ss, batch * lc, w_in, _identity_rope_tables(batch * lc), tm=batch * lc,
                          tile0=CTX_TILE0)
    w_ukv = _permute_w_ukv(ab_w_ukv[0])
    kv_lat = _kvup_call(p_lat, ab_kv_norm[0], w_ukv, tm=1024)
    kv_ctx = _kvup_call(p_ctx, ab_kv_norm[0], w_ukv, tm=lc, tile0=CTX_TILE0)
    a_out = _mla_call(p_lat, kv_lat, p_ctx, kv_ctx, batch, n_tok, lc)
    var_map, na_er, na_ec = _na_tables(n_tok // GRID_W)
    b_out = _na_call(p_lat, p_ctx, _na_bias(ab_rpb[0], na_er, na_ec), var_map, batch, n_tok, lc)
    ln1 = jnp.stack([ln1_g, ln1_b], axis=1)
    ln2 = jnp.stack([ln2_g, ln2_b], axis=1)
    h1, u2, logits = _outln_call(a_out, b_out, ab_w_out, 0, x2d,
                                 mods[0][:, 2:5], ln1[0], _router_operand(router[0]), n_tok)
    h, u = _ec_moe(h1, u2, logits, exp_w1, exp_w3, exp_w2, 0, mods[0][:, 5:6], ln2[0],
                   mods[1][:, 0:2], batch, n_tok)

    p1 = _mm_call(u, cd_w_in, 0, BF16)
    s0 = _sconv_call(p1, cd_conv_w[0], cd_conv_b[0], 0, HY_W, BF16, batch, n_tok)
    h_sum, h_dif, h_nyq = _filter_call(n_tok, cd_filt_w1[0], cd_filt_b1[0], cd_filt_freq1[0], cd_filt_w2[0],
                                       cd_filt_b2[0], cd_filt_freq2[0], cd_filt_w3[0])
    fc, fs, gc, gs = _hyena_dft_operands(n_tok)
    h_spec = _dft_fwd_call(fc, fs, h_sum, h_dif, None, F32, 1, n_tok)
    z = s0
    for o in range(HY_ORDER):
        y_spec = _dft_fwd_call(fc, fs, z, z, (h_spec, h_nyq, o * HY_W), BF16, batch, n_tok)
        z = _dual_call(gc, gs, y_spec, p1, cd_conv_w[0], cd_conv_b[0], (o + 1) * HY_W, z, cd_skip[0][o:o + 1],
                       batch, n_tok, HY_W)
    cl, msl, cw = _fnet_operands(n_tok, FN_GROUP_W)
    y_fn = _fnet_call(cl, msl, p1, cw, HY_IN_W, batch, n_tok)
    h1, u2, logits = _outln_call(z, y_fn, cd_w_out, 0, h,
                                 mods[1][:, 2:5], ln1[1], _router_operand(router[1]), n_tok)
    h, _ = _ec_moe(h1, u2, logits, exp_w1, exp_w3, exp_w2, 1, mods[1][:, 5:6], ln2[1],
                   None, batch, n_tok)
    return h.reshape(batch, n_tok, d)
```
